```python
import math
import jax, jax.numpy as jnp
from jax import lax
import numpy as np

D_MODEL = 1024
BATCH = 8
SEQ = 4096
DEPTH = 2

N_BRANCH = 4
BR_WIDTH = D_MODEL // N_BRANCH
HEAD_DIM = 64
N_HEADS = BR_WIDTH // HEAD_DIM
CHUNK = 128
Q_BLOCK = 128
SHORT_CONV = 3
CONF_CONV = 31
EPS = 1e-6

SPLIT_SIZES = (
    2 * BR_WIDTH,
    BR_WIDTH,
    3 * BR_WIDTH,
    N_HEADS,
    BR_WIDTH,
    3 * BR_WIDTH,
    BR_WIDTH,
    2 * BR_WIDTH,
    BR_WIDTH,
    N_BRANCH * D_MODEL,
)
IN_COLS = sum(SPLIT_SIZES)
SPLIT_POINTS = tuple(int(v) for v in np.cumsum(SPLIT_SIZES)[:-1])

kernel_name = "hybrid_gated_parallel_mixers"


def rmsnorm(x, g):
    x32 = x.astype(jnp.float32)
    y = x32 * lax.rsqrt(jnp.mean(x32 * x32, axis=-1, keepdims=True) + EPS)
    return y.astype(x.dtype) * g


def layernorm(x, g, b):
    x32 = x.astype(jnp.float32)
    mu = jnp.mean(x32, axis=-1, keepdims=True)
    xc = x32 - mu
    var = jnp.mean(xc * xc, axis=-1, keepdims=True)
    return (xc * lax.rsqrt(var + EPS)).astype(x.dtype) * g + b


def causal_depthwise_conv(x, w):
    k_width, ch = w.shape
    return lax.conv_general_dilated(
        x, w.astype(x.dtype)[:, None, :],
        window_strides=(1,), padding=[(k_width - 1, 0)],
        dimension_numbers=("NWC", "WIO", "NWC"),
        feature_group_count=ch)


def gmlp_chunk_mixer(uv, sgu_w, sgu_b, ln_g, ln_b):
    bsz, seq, _ = uv.shape
    u, v = jnp.split(jax.nn.gelu(uv), 2, axis=-1)
    v = layernorm(v, ln_g, ln_b)
    n_chunks = seq // CHUNK
    vc = v.reshape(bsz, n_chunks, CHUNK, N_HEADS, HEAD_DIM)
    causal = jnp.tril(jnp.ones((CHUNK, CHUNK), dtype=bool))
    w = jnp.where(causal[None], sgu_w, jnp.zeros_like(sgu_w))
    mixed = jnp.einsum("hts,bnshd->bnthd", w, vc) + sgu_b.T[None, None, :, :, None]
    return u * mixed.reshape(bsz, seq, BR_WIDTH)


def forgetting_attention(qkv, f_logit, f_bias):
    bsz, seq, _ = qkv.shape
    q, k, v = jnp.split(qkv, 3, axis=-1)
    q = q.reshape(bsz, seq, N_HEADS, HEAD_DIM)
    k = k.reshape(bsz, seq, N_HEADS, HEAD_DIM)
    v = v.reshape(bsz, seq, N_HEADS, HEAD_DIM)
    log_f = jax.nn.log_sigmoid(f_logit.astype(jnp.float32) + f_bias.astype(jnp.float32))
    cum = jnp.cumsum(log_f, axis=1).transpose(0, 2, 1)
    n_blocks = seq // Q_BLOCK
    q_blocks = q.reshape(bsz, n_blocks, Q_BLOCK, N_HEADS, HEAD_DIM).transpose(1, 0, 2, 3, 4)
    c_blocks = cum.reshape(bsz, N_HEADS, n_blocks, Q_BLOCK).transpose(2, 0, 1, 3)
    key_pos = jnp.arange(seq)
    scale = 1.0 / math.sqrt(HEAD_DIM)
    neg = jnp.finfo(jnp.float32).min

    def one_block(args):
        qb, cb, idx = args
        s = jnp.einsum("bqhd,bkhd->bhqk", qb, k).astype(jnp.float32) * scale
        s = s + (cb[..., :, None] - cum[:, :, None, :])
        q_pos = idx * Q_BLOCK + jnp.arange(Q_BLOCK)
        mask = key_pos[None, :] <= q_pos[:, None]
        p = jax.nn.softmax(jnp.where(mask, s, neg), axis=-1).astype(v.dtype)
        return jnp.einsum("bhqk,bkhd->bqhd", p, v)

    out = lax.map(one_block, (q_blocks, c_blocks, jnp.arange(n_blocks)))
    return out.transpose(1, 0, 2, 3, 4).reshape(bsz, seq, BR_WIDTH)


def short_gated_conv(cin, conv_w):
    b_gate, c_gate, xin = jnp.split(cin, 3, axis=-1)
    return b_gate * causal_depthwise_conv(c_gate * xin, conv_w)


def conformer_conv(glu_in, dw_w, dw_b, ln_g, ln_b):
    a, g = jnp.split(glu_in, 2, axis=-1)
    h = a * jax.nn.sigmoid(g)
    h = causal_depthwise_conv(h, dw_w) + dw_b
    return jax.nn.silu(layernorm(h, ln_g, ln_b))


def _fwd_setup_inputs(seed: int = 0) -> dict:
    key = jax.random.key(seed)
    ks = jax.random.split(key, 18)
    f32 = jnp.float32
    nrm = lambda k, shape, s: jax.random.normal(k, shape, f32) * s
    return {
        "x": jax.random.normal(ks[0], (BATCH, SEQ, D_MODEL), f32),
        "norm_g": 1.0 + nrm(ks[1], (DEPTH, D_MODEL), 0.02),
        "w_in": nrm(ks[2], (DEPTH, D_MODEL, IN_COLS), D_MODEL ** -0.5),
        "f_bias": jax.random.uniform(ks[3], (DEPTH, N_HEADS), f32, 1.0, 4.0),
        "sgu_w": nrm(ks[4], (DEPTH, N_HEADS, CHUNK, CHUNK), CHUNK ** -0.5),
        "sgu_b": 1.0 + nrm(ks[5], (DEPTH, N_HEADS, CHUNK), 0.02),
        "sgu_ln_g": 1.0 + nrm(ks[6], (DEPTH, BR_WIDTH), 0.02),
        "sgu_ln_b": nrm(ks[7], (DEPTH, BR_WIDTH), 0.02),
        "short_conv_w": nrm(ks[8], (DEPTH, SHORT_CONV, BR_WIDTH), SHORT_CONV ** -0.5),
        "conf_dw_w": nrm(ks[9], (DEPTH, CONF_CONV, BR_WIDTH), CONF_CONV ** -0.5),
        "conf_dw_b": nrm(ks[10], (DEPTH, BR_WIDTH), 0.02),
        "conf_ln_g": 1.0 + nrm(ks[11], (DEPTH, BR_WIDTH), 0.02),
        "conf_ln_b": nrm(ks[12], (DEPTH, BR_WIDTH), 0.02),
        "w_branch": nrm(ks[13], (DEPTH, N_BRANCH, BR_WIDTH, D_MODEL), BR_WIDTH ** -0.5),
        "w_out": nrm(ks[14], (DEPTH, D_MODEL, D_MODEL), D_MODEL ** -0.5),
        "final_g": 1.0 + nrm(ks[15], (D_MODEL,), 0.02),
    }


def _fwd_reference(x, norm_g, w_in, f_bias, sgu_w, sgu_b, sgu_ln_g, sgu_ln_b,
              short_conv_w, conf_dw_w, conf_dw_b, conf_ln_g, conf_ln_b,
              w_branch, w_out, final_g):
    bsz, seq, _ = x.shape
    for layer in range(DEPTH):
        h = rmsnorm(x, norm_g[layer])
        proj = jnp.einsum("bsd,dc->bsc", h, w_in[layer])
        (a_uv, a_gate, b_qkv, b_f, b_gate, c_in, c_gate,
         d_glu, d_gate, merge_logits) = jnp.split(proj, SPLIT_POINTS, axis=-1)

        y_a = gmlp_chunk_mixer(a_uv, sgu_w[layer], sgu_b[layer],
                               sgu_ln_g[layer], sgu_ln_b[layer]) * jax.nn.silu(a_gate)
        y_b = forgetting_attention(b_qkv, b_f, f_bias[layer]) * jax.nn.silu(b_gate)
        y_c = short_gated_conv(c_in, short_conv_w[layer]) * jax.nn.silu(c_gate)
        y_d = conformer_conv(d_glu, conf_dw_w[layer], conf_dw_b[layer],
                             conf_ln_g[layer], conf_ln_b[layer]) * jax.nn.silu(d_gate)

        branches = jnp.stack([y_a, y_b, y_c, y_d], axis=2)
        projected = jnp.einsum("bsnc,ncd->bsnd", branches, w_branch[layer])
        gates = jax.nn.sigmoid(merge_logits.reshape(bsz, seq, N_BRANCH, D_MODEL))
        merged = jnp.sum(gates * projected, axis=2)
        x = x + jnp.einsum("bsd,de->bse", merged, w_out[layer])
    return rmsnorm(x, final_g)


import jax as _jax
import jax.numpy as _jnp

TWIN_FORMAT = 'train_step'
FWD_PARAMS = ['x', 'norm_g', 'w_in', 'f_bias', 'sgu_w', 'sgu_b', 'sgu_ln_g', 'sgu_ln_b', 'short_conv_w', 'conf_dw_w', 'conf_dw_b', 'conf_ln_g', 'conf_ln_b', 'w_branch', 'w_out', 'final_g']
TWIN_WEIGHTS = ['norm_g', 'w_in', 'f_bias', 'sgu_w', 'sgu_b', 'sgu_ln_g', 'sgu_ln_b', 'short_conv_w', 'conf_dw_w', 'conf_dw_b', 'conf_ln_g', 'conf_ln_b', 'w_branch', 'w_out', 'final_g']
TWIN_DIFF_INPUT = 'x'
TWIN_INPUTS = ['x', 'norm_g', 'w_in', 'f_bias', 'sgu_w', 'sgu_b', 'sgu_ln_g', 'sgu_ln_b', 'short_conv_w', 'conf_dw_w', 'conf_dw_b', 'conf_ln_g', 'conf_ln_b', 'w_branch', 'w_out', 'final_g', 'loss_target', 'm_norm_g', 'm_w_in', 'm_f_bias', 'm_sgu_w', 'm_sgu_b', 'm_sgu_ln_g', 'm_sgu_ln_b', 'm_short_conv_w', 'm_conf_dw_w', 'm_conf_dw_b', 'm_conf_ln_g', 'm_conf_ln_b', 'm_w_branch', 'm_w_out', 'm_final_g', 'v_norm_g', 'v_w_in', 'v_f_bias', 'v_sgu_w', 'v_sgu_b', 'v_sgu_ln_g', 'v_sgu_ln_b', 'v_short_conv_w', 'v_conf_dw_w', 'v_conf_dw_b', 'v_conf_ln_g', 'v_conf_ln_b', 'v_w_branch', 'v_w_out', 'v_final_g']
TWIN_OUTPUTS = ['loss', 'grad_x', 'grad_norm_g', 'grad_w_in', 'grad_f_bias', 'grad_sgu_w', 'grad_sgu_b', 'grad_sgu_ln_g', 'grad_sgu_ln_b', 'grad_short_conv_w', 'grad_conf_dw_w', 'grad_conf_dw_b', 'grad_conf_ln_g', 'grad_conf_ln_b', 'grad_w_branch', 'grad_w_out', 'grad_final_g', 'delta_norm_g', 'delta_w_in', 'delta_f_bias', 'delta_sgu_w', 'delta_sgu_b', 'delta_sgu_ln_g', 'delta_sgu_ln_b', 'delta_short_conv_w', 'delta_conf_dw_w', 'delta_conf_dw_b', 'delta_conf_ln_g', 'delta_conf_ln_b', 'delta_w_branch', 'delta_w_out', 'delta_final_g', 'new_m_norm_g', 'new_m_w_in', 'new_m_f_bias', 'new_m_sgu_w', 'new_m_sgu_b', 'new_m_sgu_ln_g', 'new_m_sgu_ln_b', 'new_m_short_conv_w', 'new_m_conf_dw_w', 'new_m_conf_dw_b', 'new_m_conf_ln_g', 'new_m_conf_ln_b', 'new_m_w_branch', 'new_m_w_out', 'new_m_final_g', 'new_v_norm_g', 'new_v_w_in', 'new_v_f_bias', 'new_v_sgu_w', 'new_v_sgu_b', 'new_v_sgu_ln_g', 'new_v_sgu_ln_b', 'new_v_short_conv_w', 'new_v_conf_dw_w', 'new_v_conf_dw_b', 'new_v_conf_ln_g', 'new_v_conf_ln_b', 'new_v_w_branch', 'new_v_w_out', 'new_v_final_g']
TWIN_LEAF_KINDS = {'loss': 'loss', 'grad_x': 'grad_x', 'grad_norm_g': 'grad_w', 'grad_w_in': 'grad_w', 'grad_f_bias': 'grad_w', 'grad_sgu_w': 'grad_w', 'grad_sgu_b': 'grad_w', 'grad_sgu_ln_g': 'grad_w', 'grad_sgu_ln_b': 'grad_w', 'grad_short_conv_w': 'grad_w', 'grad_conf_dw_w': 'grad_w', 'grad_conf_dw_b': 'grad_w', 'grad_conf_ln_g': 'grad_w', 'grad_conf_ln_b': 'grad_w', 'grad_w_branch': 'grad_w', 'grad_w_out': 'grad_w', 'grad_final_g': 'grad_w', 'delta_norm_g': 'delta_w', 'delta_w_in': 'delta_w', 'delta_f_bias': 'delta_w', 'delta_sgu_w': 'delta_w', 'delta_sgu_b': 'delta_w', 'delta_sgu_ln_g': 'delta_w', 'delta_sgu_ln_b': 'delta_w', 'delta_short_conv_w': 'delta_w', 'delta_conf_dw_w': 'delta_w', 'delta_conf_dw_b': 'delta_w', 'delta_conf_ln_g': 'delta_w', 'delta_conf_ln_b': 'delta_w', 'delta_w_branch': 'delta_w', 'delta_w_out': 'delta_w', 'delta_final_g': 'delta_w', 'new_m_norm_g': 'new_m', 'new_m_w_in': 'new_m', 'new_m_f_bias': 'new_m', 'new_m_sgu_w': 'new_m', 'new_m_sgu_b': 'new_m', 'new_m_sgu_ln_g': 'new_m', 'new_m_sgu_ln_b': 'new_m', 'new_m_short_conv_w': 'new_m', 'new_m_conf_dw_w': 'new_m', 'new_m_conf_dw_b': 'new_m', 'new_m_conf_ln_g': 'new_m', 'new_m_conf_ln_b': 'new_m', 'new_m_w_branch': 'new_m', 'new_m_w_out': 'new_m', 'new_m_final_g': 'new_m', 'new_v_norm_g': 'new_v', 'new_v_w_in': 'new_v', 'new_v_f_bias': 'new_v', 'new_v_sgu_w': 'new_v', 'new_v_sgu_b': 'new_v', 'new_v_sgu_ln_g': 'new_v', 'new_v_sgu_ln_b': 'new_v', 'new_v_short_conv_w': 'new_v', 'new_v_conf_dw_w': 'new_v', 'new_v_conf_dw_b': 'new_v', 'new_v_conf_ln_g': 'new_v', 'new_v_conf_ln_b': 'new_v', 'new_v_w_branch': 'new_v', 'new_v_w_out': 'new_v', 'new_v_final_g': 'new_v'}


def _forward(args):
    return _fwd_reference(*[args[k] for k in FWD_PARAMS])


def _output_shape():
    out = _jax.eval_shape(lambda: _forward(_fwd_setup_inputs(0)))
    return out.shape, out.dtype

N_MICROBATCH = 1
ADAM_LR = 0.001
ADAM_B1 = 0.9
ADAM_B2 = 0.999
ADAM_EPS = 1e-08
ADAM_WD = 0.01
ADAM_STEP = 10
PER_EXAMPLE_BATCH_AXIS = {'x': 0, 'loss_target': 0}
SHARED_INPUTS = []
_WEIGHT_DTYPES = {'norm_g': _jnp.float32, 'w_in': _jnp.float32, 'f_bias': _jnp.float32, 'sgu_w': _jnp.float32, 'sgu_b': _jnp.float32, 'sgu_ln_g': _jnp.float32, 'sgu_ln_b': _jnp.float32, 'short_conv_w': _jnp.float32, 'conf_dw_w': _jnp.float32, 'conf_dw_b': _jnp.float32, 'conf_ln_g': _jnp.float32, 'conf_ln_b': _jnp.float32, 'w_branch': _jnp.float32, 'w_out': _jnp.float32, 'final_g': _jnp.float32}
MOMENT_SCALE = {'norm_g': 1.457413e-01, 'w_in': 5.291822e-02, 'f_bias': 1.936622e-01, 'sgu_w': 3.571312e-02, 'sgu_b': 5.374574e-02, 'sgu_ln_g': 5.039171e-02, 'sgu_ln_b': 5.016007e-02, 'short_conv_w': 1.089323e-01, 'conf_dw_w': 6.854213e-02, 'conf_dw_b': 1.404139e-01, 'conf_ln_g': 7.620797e-02, 'conf_ln_b': 6.552810e-02, 'w_branch': 3.869584e-02, 'w_out': 7.745748e-02, 'final_g': 3.194655e+01}


def _to_microbatches(a, axis):
    t = _jnp.moveaxis(a, axis, 0)
    t = t.reshape((N_MICROBATCH, t.shape[0] // N_MICROBATCH) + t.shape[1:])
    return _jnp.moveaxis(t, 1, axis + 1)


def setup_inputs(seed: int = 0) -> dict:
    inp = _fwd_setup_inputs(seed)
    key = _jax.random.fold_in(_jax.random.key(seed), 7919)
    shape, _ = _output_shape()
    out = dict(inp)
    out["loss_target"] = _jax.random.normal(_jax.random.fold_in(key, 0), shape, _jnp.float32)
    for i, name in enumerate(TWIN_WEIGHTS):
        w = inp[name].astype(_jnp.float32)
        if MOMENT_SCALE is None:
            s = _jnp.sqrt(_jnp.mean(_jnp.square(w)) + 1e-30)
        else:
            s = MOMENT_SCALE[name]
        km, kv = _jax.random.split(_jax.random.fold_in(key, i + 1))
        out[name] = w
        out["m_" + name] = s * _jax.random.normal(km, w.shape, _jnp.float32)
        out["v_" + name] = (s * s) * _jax.random.uniform(kv, w.shape, _jnp.float32, 0.5, 1.5)
    if N_MICROBATCH > 1:
        for name, axis in PER_EXAMPLE_BATCH_AXIS.items():
            out[name] = _to_microbatches(out[name], axis)
    return {'x': out['x'], 'norm_g': out['norm_g'], 'w_in': out['w_in'], 'f_bias': out['f_bias'], 'sgu_w': out['sgu_w'], 'sgu_b': out['sgu_b'], 'sgu_ln_g': out['sgu_ln_g'], 'sgu_ln_b': out['sgu_ln_b'], 'short_conv_w': out['short_conv_w'], 'conf_dw_w': out['conf_dw_w'], 'conf_dw_b': out['conf_dw_b'], 'conf_ln_g': out['conf_ln_g'], 'conf_ln_b': out['conf_ln_b'], 'w_branch': out['w_branch'], 'w_out': out['w_out'], 'final_g': out['final_g'], 'loss_target': out['loss_target'], 'm_norm_g': out['m_norm_g'], 'm_w_in': out['m_w_in'], 'm_f_bias': out['m_f_bias'], 'm_sgu_w': out['m_sgu_w'], 'm_sgu_b': out['m_sgu_b'], 'm_sgu_ln_g': out['m_sgu_ln_g'], 'm_sgu_ln_b': out['m_sgu_ln_b'], 'm_short_conv_w': out['m_short_conv_w'], 'm_conf_dw_w': out['m_conf_dw_w'], 'm_conf_dw_b': out['m_conf_dw_b'], 'm_conf_ln_g': out['m_conf_ln_g'], 'm_conf_ln_b': out['m_conf_ln_b'], 'm_w_branch': out['m_w_branch'], 'm_w_out': out['m_w_out'], 'm_final_g': out['m_final_g'], 'v_norm_g': out['v_norm_g'], 'v_w_in': out['v_w_in'], 'v_f_bias': out['v_f_bias'], 'v_sgu_w': out['v_sgu_w'], 'v_sgu_b': out['v_sgu_b'], 'v_sgu_ln_g': out['v_sgu_ln_g'], 'v_sgu_ln_b': out['v_sgu_ln_b'], 'v_short_conv_w': out['v_short_conv_w'], 'v_conf_dw_w': out['v_conf_dw_w'], 'v_conf_dw_b': out['v_conf_dw_b'], 'v_conf_ln_g': out['v_conf_ln_g'], 'v_conf_ln_b': out['v_conf_ln_b'], 'v_w_branch': out['v_w_branch'], 'v_w_out': out['v_w_out'], 'v_final_g': out['v_final_g']}


def _loss(weights, diff, rest, loss_target):
    with _jax.named_scope("forward"):
        args = {**rest, TWIN_DIFF_INPUT: diff, **{k: w.astype(_WEIGHT_DTYPES[k]) for k, w in weights.items()}}
        y = _forward(args)
    with _jax.named_scope("loss_head"):
        err = _jnp.square(y.astype(_jnp.float32) - loss_target)
        return 0.5 * _jnp.sum(_jnp.mean(err, axis=-1)) if err.ndim else 0.5 * err


def _adamw(w, g, m, v):
    m = ADAM_B1 * m + (1.0 - ADAM_B1) * g
    v = ADAM_B2 * v + (1.0 - ADAM_B2) * _jnp.square(g)
    m_hat = m / (1.0 - ADAM_B1 ** ADAM_STEP)
    v_hat = v / (1.0 - ADAM_B2 ** ADAM_STEP)
    delta = -ADAM_LR * (m_hat / (_jnp.sqrt(v_hat) + ADAM_EPS) + ADAM_WD * w)
    return delta, m, v


def reference(x, norm_g, w_in, f_bias, sgu_w, sgu_b, sgu_ln_g, sgu_ln_b, short_conv_w, conf_dw_w, conf_dw_b, conf_ln_g, conf_ln_b, w_branch, w_out, final_g, loss_target, m_norm_g, m_w_in, m_f_bias, m_sgu_w, m_sgu_b, m_sgu_ln_g, m_sgu_ln_b, m_short_conv_w, m_conf_dw_w, m_conf_dw_b, m_conf_ln_g, m_conf_ln_b, m_w_branch, m_w_out, m_final_g, v_norm_g, v_w_in, v_f_bias, v_sgu_w, v_sgu_b, v_sgu_ln_g, v_sgu_ln_b, v_short_conv_w, v_conf_dw_w, v_conf_dw_b, v_conf_ln_g, v_conf_ln_b, v_w_branch, v_w_out, v_final_g):
    given = dict(x=x, norm_g=norm_g, w_in=w_in, f_bias=f_bias, sgu_w=sgu_w, sgu_b=sgu_b, sgu_ln_g=sgu_ln_g, sgu_ln_b=sgu_ln_b, short_conv_w=short_conv_w, conf_dw_w=conf_dw_w, conf_dw_b=conf_dw_b, conf_ln_g=conf_ln_g, conf_ln_b=conf_ln_b, w_branch=w_branch, w_out=w_out, final_g=final_g, loss_target=loss_target, m_norm_g=m_norm_g, m_w_in=m_w_in, m_f_bias=m_f_bias, m_sgu_w=m_sgu_w, m_sgu_b=m_sgu_b, m_sgu_ln_g=m_sgu_ln_g, m_sgu_ln_b=m_sgu_ln_b, m_short_conv_w=m_short_conv_w, m_conf_dw_w=m_conf_dw_w, m_conf_dw_b=m_conf_dw_b, m_conf_ln_g=m_conf_ln_g, m_conf_ln_b=m_conf_ln_b, m_w_branch=m_w_branch, m_w_out=m_w_out, m_final_g=m_final_g, v_norm_g=v_norm_g, v_w_in=v_w_in, v_f_bias=v_f_bias, v_sgu_w=v_sgu_w, v_sgu_b=v_sgu_b, v_sgu_ln_g=v_sgu_ln_g, v_sgu_ln_b=v_sgu_ln_b, v_short_conv_w=v_short_conv_w, v_conf_dw_w=v_conf_dw_w, v_conf_dw_b=v_conf_dw_b, v_conf_ln_g=v_conf_ln_g, v_conf_ln_b=v_conf_ln_b, v_w_branch=v_w_branch, v_w_out=v_w_out, v_final_g=v_final_g)
    weights = {n: given[n] for n in TWIN_WEIGHTS}
    shared = {n: given[n] for n in SHARED_INPUTS}
    per_example = {n: given[n] for n in ['x']}
    grad_fn = _jax.value_and_grad(_loss, argnums=(0, 1))

    def one_microbatch(ex, loss_target):
        ex = dict(ex)
        diff = ex.pop(TWIN_DIFF_INPUT)
        return grad_fn(weights, diff, {**shared, **ex}, loss_target)

    if N_MICROBATCH == 1:
        loss, (grad_w, grad_x) = one_microbatch(per_example, given["loss_target"])
    else:
        def body(carry, xs):
            loss_sum, grad_sum = carry
            l_k, (gw_k, gx_k) = one_microbatch(xs[0], xs[1])
            with _jax.named_scope("update"):
                return (loss_sum + l_k, _jax.tree.map(_jnp.add, grad_sum, gw_k)), gx_k

        init = (_jnp.zeros((), _jnp.float32), _jax.tree.map(_jnp.zeros_like, weights))
        (loss, grad_w), grad_x = _jax.lax.scan(body, init, (per_example, given["loss_target"]))
    with _jax.named_scope("update"):
        delta_w, new_m, new_v = {}, {}, {}
        for n in TWIN_WEIGHTS:
            delta_w[n], new_m[n], new_v[n] = _adamw(weights[n], grad_w[n], given["m_" + n], given["v_" + n])
    return (loss, grad_x, *[grad_w[n] for n in TWIN_WEIGHTS], *[delta_w[n] for n in TWIN_WEIGHTS],
            *[new_m[n] for n in TWIN_WEIGHTS], *[new_v[n] for n in TWIN_WEIGHTS])
```

```python
import functools
import math

import jax
import jax.numpy as jnp
from jax import lax
from jax.experimental import pallas as pl
from jax.experimental.pallas import tpu as pltpu

F32 = jnp.float32
BF16 = jnp.bfloat16

D_MODEL = 1024
BR = 256
N_HEADS = 4
HEAD_DIM = 64
CHUNK = 128
K_SHORT = 3
K_CONF = 31
EPS = 1e-6
IN_COLS = 7684
SHARD_COLS = IN_COLS // 4
SHARD_PAD = 2048
N_CHIPS = 4

QKV0, CIN0, AUV0, DGLU0 = 0, 768, 1536, 2048
AG0, BG0, CG0, DG0 = 2560, 2816, 3072, 3328
BF0 = 3584
MRG0 = 4096
PCOLS = 8192

V7X_VMEM_BYTES = 64 * 1024 * 1024
VMEM_LIMIT = V7X_VMEM_BYTES * 7 // 8

ADAM_LR, ADAM_B1, ADAM_B2, ADAM_EPS, ADAM_WD, ADAM_STEP = 0.001, 0.9, 0.999, 1e-08, 0.01, 10

MESH = pl.DeviceIdType.MESH
ANY = pl.BlockSpec(memory_space=pl.ANY)
VMEM = pl.BlockSpec(memory_space=pltpu.VMEM)

GELU_C0 = math.sqrt(2.0 / math.pi)
GELU_C1 = 0.044715


def _params(*sem):
    return pltpu.CompilerParams(dimension_semantics=sem, vmem_limit_bytes=VMEM_LIMIT)


def _sigmoid(x):
    return 1.0 / (1.0 + jnp.exp(-x))


def _silu_and_grad(x):
    s = _sigmoid(x)
    return x * s, s * (1.0 + x * (1.0 - s))


def _gelu_and_grad(z):
    z2 = z * z
    t = jnp.tanh(GELU_C0 * (z + GELU_C1 * z2 * z))
    half = 0.5 * (1.0 + t)
    return z * half, half + 0.5 * z * (1.0 - t * t) * (GELU_C0 * (1.0 + 3.0 * GELU_C1 * z2))


def _ln_fwd(v):
    mu = jnp.mean(v, axis=-1, keepdims=True)
    xc = v - mu
    rs = lax.rsqrt(jnp.mean(xc * xc, axis=-1, keepdims=True) + EPS)
    return xc * rs, rs


def _ln_bwd(d_xh, xh, rs):
    return rs * (d_xh - jnp.mean(d_xh, axis=-1, keepdims=True) - xh * jnp.mean(d_xh * xh, axis=-1, keepdims=True))


def _part8(a):
    return a.reshape(a.shape[0] // 8, 8, a.shape[1]).sum(axis=0)


def _dot(a, b):
    return jnp.dot(a, b, preferred_element_type=F32)


def _dot_nt(a, b):
    return lax.dot_general(a, b, (((1,), (1,)), ((), ())), preferred_element_type=F32)


def _dot_tn(a, b):
    return lax.dot_general(a, b, (((0,), (0,)), ((), ())), preferred_element_type=F32)


def _head_masks(dtype):
    lane = lax.broadcasted_iota(jnp.int32, (1, BR), 1) // HEAD_DIM
    return [(lane == h).astype(dtype) for h in range(N_HEADS)]


def _window(ref, col0, width):
    return ref.at[:, pl.ds(col0, width)]


def _copy_all(pairs, sems):
    cps = [pltpu.make_async_copy(s, d, sems.at[i]) for i, (s, d) in enumerate(pairs)]
    for cp in cps:
        cp.start()
    for cp in cps:
        cp.wait()


def _inproj_fwd(x, g, w):
    s = x.shape[0]
    tm, tn = min(512, s), 1024

    def body(x_ref, g_ref, w_ref, proj_ref, h_ref):
        @pl.when(pl.program_id(1) == 0)
        def _():
            xv = x_ref[...]
            r = lax.rsqrt(jnp.mean(xv * xv, axis=-1, keepdims=True) + EPS)
            h_ref[...] = ((xv * r) * g_ref[...]).astype(BF16)
        proj_ref[...] = _dot(h_ref[...], w_ref[...])

    return pl.pallas_call(
        body, name="inproj_fwd", grid=(s // tm, PCOLS // tn),
        in_specs=[pl.BlockSpec((tm, D_MODEL), lambda i, j: (i, 0)), pl.BlockSpec((1, D_MODEL), lambda i, j: (0, 0)),
                  pl.BlockSpec((D_MODEL, tn), lambda i, j: (0, j))],
        out_specs=[pl.BlockSpec((tm, tn), lambda i, j: (i, j)), pl.BlockSpec((tm, D_MODEL), lambda i, j: (i, 0))],
        out_shape=[jax.ShapeDtypeStruct((s, PCOLS), F32), jax.ShapeDtypeStruct((s, D_MODEL), BF16)],
        compiler_params=_params("arbitrary", "arbitrary"))(x, g, w)


def _rms_bwd(dh, x, g):
    r = lax.rsqrt(jnp.mean(x * x, axis=-1, keepdims=True) + EPS)
    xn = x * r
    gy = dh * g
    dx = r * (gy - xn * jnp.mean(xn * gy, axis=-1, keepdims=True))
    return dx, _part8(dh * xn)


def _inproj_bwd_x(dproj, w, x, dxn, g):
    s = x.shape[0]
    tm, tk = min(512, s), 1024
    nk = PCOLS // tk

    def body(dp_ref, w_ref, x_ref, dxn_ref, g_ref, dx_ref, dg_ref, acc_ref):
        i, k = pl.program_id(0), pl.program_id(1)

        @pl.when(k == 0)
        def _():
            acc_ref[...] = jnp.zeros_like(acc_ref)

        @pl.when((i == 0) & (k == 0))
        def _():
            dg_ref[...] = jnp.zeros_like(dg_ref)

        acc_ref[...] += _dot_nt(dp_ref[...], w_ref[...])

        @pl.when(k == nk - 1)
        def _():
            dx, dg8 = _rms_bwd(acc_ref[...], x_ref[...], g_ref[...])
            dx_ref[...] = dxn_ref[...] + dx
            dg_ref[...] += dg8

    dx, dg8 = pl.pallas_call(
        body, name="inproj_bwd_x", grid=(s // tm, nk),
        in_specs=[pl.BlockSpec((tm, tk), lambda i, k: (i, k)), pl.BlockSpec((D_MODEL, tk), lambda i, k: (0, k)),
                  pl.BlockSpec((tm, D_MODEL), lambda i, k: (i, 0)), pl.BlockSpec((tm, D_MODEL), lambda i, k: (i, 0)),
                  pl.BlockSpec((1, D_MODEL), lambda i, k: (0, 0))],
        out_specs=[pl.BlockSpec((tm, D_MODEL), lambda i, k: (i, 0)), pl.BlockSpec((8, D_MODEL), lambda i, k: (0, 0))],
        out_shape=[jax.ShapeDtypeStruct((s, D_MODEL), F32), jax.ShapeDtypeStruct((8, D_MODEL), F32)],
        scratch_shapes=[pltpu.VMEM((tm, D_MODEL), F32)],
        compiler_params=_params("arbitrary", "arbitrary"))(dproj, w, x, dxn, g)
    return dx, dg8


def _inproj_bwd_w(h, dproj):
    s = h.shape[0]
    tn, tk = 1024, min(512, s)
    nk = s // tk

    def body(h_ref, dp_ref, dw_ref, acc_ref):
        k = pl.program_id(1)

        @pl.when(k == 0)
        def _():
            acc_ref[...] = jnp.zeros_like(acc_ref)

        acc_ref[...] += _dot_tn(h_ref[...], dp_ref[...])

        @pl.when(k == nk - 1)
        def _():
            dw_ref[...] = acc_ref[...].astype(BF16)

    return pl.pallas_call(
        body, name="inproj_bwd_w", grid=(PCOLS // tn, nk),
        in_specs=[pl.BlockSpec((tk, D_MODEL), lambda j, k: (k, 0)), pl.BlockSpec((tk, tn), lambda j, k: (k, j))],
        out_specs=pl.BlockSpec((D_MODEL, tn), lambda j, k: (0, j)),
        out_shape=jax.ShapeDtypeStruct((D_MODEL, PCOLS), BF16),
        scratch_shapes=[pltpu.VMEM((D_MODEL, tn), F32)],
        compiler_params=_params("arbitrary", "arbitrary"))(h, dproj)


def _mix_a_chunk(uvp, agp, wm_ref, bias, lg, lb):
    u, du = _gelu_and_grad(uvp[:, :BR])
    v, dv = _gelu_and_grad(uvp[:, BR:])
    xh, rs = _ln_fwd(v)
    vnb = (xh * lg + lb).astype(BF16)
    masks = _head_masks(BF16)
    mixed = bias
    for h in range(N_HEADS):
        mixed = mixed + _dot(wm_ref[h], vnb * masks[h])
    sg, dsg = _silu_and_grad(agp)
    return u, du, dv, xh, rs, vnb, masks, mixed, sg, dsg


def _store_masked_sgu(sw_ref, wm_ref):
    row = lax.broadcasted_iota(jnp.int32, (CHUNK, CHUNK), 0)
    col = lax.broadcasted_iota(jnp.int32, (CHUNK, CHUNK), 1)
    for h in range(N_HEADS):
        wm_ref[h] = jnp.where(row >= col, sw_ref[h], 0.0).astype(BF16)


def _mix_a_fwd(proj, sgu_w, bias, lg, lb):
    s = proj.shape[0]

    def body(proj_ref, sw_ref, bias_ref, lg_ref, lb_ref, y_ref, uv_buf, ag_buf, wm_ref, sems):
        _copy_all([(_window(proj_ref, AUV0, 2 * BR), uv_buf), (_window(proj_ref, AG0, BR), ag_buf)], sems)
        _store_masked_sgu(sw_ref, wm_ref)

        def chunk(i, c):
            rows = pl.ds(pl.multiple_of(i * CHUNK, CHUNK), CHUNK)
            u, _, _, _, _, _, _, mixed, sg, _ = _mix_a_chunk(uv_buf[rows, :], ag_buf[rows, :], wm_ref, bias_ref[...],
                                                            lg_ref[...], lb_ref[...])
            y_ref[rows, :] = (u * mixed * sg).astype(BF16)
            return c

        lax.fori_loop(0, s // CHUNK, chunk, 0)

    return pl.pallas_call(
        body, name="mix_a_fwd", in_specs=[ANY, VMEM, VMEM, VMEM, VMEM], out_specs=VMEM,
        out_shape=jax.ShapeDtypeStruct((s, BR), BF16),
        scratch_shapes=[pltpu.VMEM((s, 2 * BR), F32), pltpu.VMEM((s, BR), F32), pltpu.VMEM((N_HEADS, CHUNK, CHUNK), BF16),
                        pltpu.SemaphoreType.DMA((2,))],
        compiler_params=_params())(proj, sgu_w, bias, lg, lb)


def _mix_a_bwd(proj, dproj, dy, sgu_w, bias, lg, lb):
    s = proj.shape[0]

    def body(proj_ref, dproj_in, dy_ref, sw_ref, bias_ref, lg_ref, lb_ref,
             dproj_ref, dsw_ref, dbias_ref, dlg_ref, dlb_ref,
             uv_buf, ag_buf, duv_buf, dag_buf, wm_ref, acc_lg, acc_lb, sems):
        del dproj_in
        _copy_all([(_window(proj_ref, AUV0, 2 * BR), uv_buf), (_window(proj_ref, AG0, BR), ag_buf)], sems)
        _store_masked_sgu(sw_ref, wm_ref)
        dsw_ref[...] = jnp.zeros_like(dsw_ref)
        dbias_ref[...] = jnp.zeros_like(dbias_ref)
        acc_lg[...] = jnp.zeros_like(acc_lg)
        acc_lb[...] = jnp.zeros_like(acc_lb)

        def chunk(i, c):
            rows = pl.ds(pl.multiple_of(i * CHUNK, CHUNK), CHUNK)
            lg_v = lg_ref[...]
            u, du, dv, xh, rs, vnb, masks, mixed, sg, dsg = _mix_a_chunk(
                uv_buf[rows, :], ag_buf[rows, :], wm_ref, bias_ref[...], lg_v, lb_ref[...])
            dyv = dy_ref[rows, :]
            t1 = dyv * sg
            d_u = t1 * mixed
            d_mixed = t1 * u
            d_ag = dyv * u * mixed * dsg
            dbias_ref[...] += d_mixed
            dmb = d_mixed.astype(BF16)
            d_vn = jnp.zeros((CHUNK, BR), F32)
            for h in range(N_HEADS):
                dm_h = dmb * masks[h]
                dsw_ref[h] += _dot_nt(dm_h, vnb)
                d_vn = d_vn + _dot_tn(wm_ref[h], dm_h)
            acc_lg[...] += _part8(d_vn * xh)
            acc_lb[...] += _part8(d_vn)
            d_v = _ln_bwd(d_vn * lg_v, xh, rs)
            duv_buf[rows, :] = jnp.concatenate([d_u * du, d_v * dv], axis=1).astype(BF16)
            dag_buf[rows, :] = d_ag.astype(BF16)
            return c

        lax.fori_loop(0, s // CHUNK, chunk, 0)
        row = lax.broadcasted_iota(jnp.int32, (CHUNK, CHUNK), 0)
        col = lax.broadcasted_iota(jnp.int32, (CHUNK, CHUNK), 1)
        for h in range(N_HEADS):
            dsw_ref[h] = jnp.where(row >= col, dsw_ref[h], 0.0)
        dlg_ref[...] = jnp.sum(acc_lg[...], axis=0, keepdims=True)
        dlb_ref[...] = jnp.sum(acc_lb[...], axis=0, keepdims=True)
        _copy_all([(duv_buf, _window(dproj_ref, AUV0, 2 * BR)), (dag_buf, _window(dproj_ref, AG0, BR))], sems)

    return pl.pallas_call(
        body, name="mix_a_bwd", in_specs=[ANY, ANY, VMEM, VMEM, VMEM, VMEM, VMEM],
        out_specs=[ANY, VMEM, VMEM, VMEM, VMEM],
        out_shape=[jax.ShapeDtypeStruct(dproj.shape, dproj.dtype), jax.ShapeDtypeStruct((N_HEADS, CHUNK, CHUNK), F32),
                   jax.ShapeDtypeStruct((CHUNK, BR), F32), jax.ShapeDtypeStruct((1, BR), F32), jax.ShapeDtypeStruct((1, BR), F32)],
        scratch_shapes=[pltpu.VMEM((s, 2 * BR), F32), pltpu.VMEM((s, BR), F32), pltpu.VMEM((s, 2 * BR), BF16),
                        pltpu.VMEM((s, BR), BF16), pltpu.VMEM((N_HEADS, CHUNK, CHUNK), BF16),
                        pltpu.VMEM((8, BR), F32), pltpu.VMEM((8, BR), F32), pltpu.SemaphoreType.DMA((2,))],
        input_output_aliases={1: 0}, compiler_params=_params())(proj, dproj, dy, sgu_w, bias, lg, lb)


def _tri_ones(n, upper):
    row = lax.broadcasted_iota(jnp.int32, (n, n), 0)
    col = lax.broadcasted_iota(jnp.int32, (n, n), 1)
    return ((row <= col) if upper else (row >= col)).astype(BF16)


def _split3(c):
    hi = c.astype(BF16)
    r1 = c - hi.astype(F32)
    mid = r1.astype(BF16)
    lo = (r1 - mid.astype(F32)).astype(BF16)
    return [hi, mid, lo]


def _tri_sum(tri, a):
    parts = _split3(a)
    return _dot(tri, parts[0]) + _dot(tri, parts[1]) + _dot(tri, parts[2])


def _attn_prep_fwd(proj, f_bias):
    s = proj.shape[0]

    def body(proj_ref, fb_ref, cum_ref, z_buf, sems):
        _copy_all([(_window(proj_ref, BF0, CHUNK), z_buf)], sems)
        tri = _tri_ones(CHUNK, upper=False)

        def chunk(i, carry):
            rows = pl.ds(pl.multiple_of(i * CHUNK, CHUNK), CHUNK)
            lf = jax.nn.log_sigmoid(z_buf[rows, :] + fb_ref[...])
            cum = _tri_sum(tri, lf) + carry
            cum_ref[rows, :] = cum
            return cum[CHUNK - 1:CHUNK, :]

        lax.fori_loop(0, s // CHUNK, chunk, jnp.zeros((1, CHUNK), F32))

    return pl.pallas_call(
        body, name="attn_prep_fwd", in_specs=[ANY, VMEM], out_specs=VMEM,
        out_shape=jax.ShapeDtypeStruct((s, CHUNK), F32),
        scratch_shapes=[pltpu.VMEM((s, CHUNK), F32), pltpu.SemaphoreType.DMA((1,))],
        compiler_params=_params())(proj, f_bias)


def _attn_prep_bwd(proj, dproj, dqkv, dcum, dbg, f_bias):
    s = proj.shape[0]
    pad = MRG0 - BF0

    def body(proj_ref, dproj_in, dqkv_ref, dcum_ref, dbg_ref, fb_ref, dproj_ref, dfb_ref, z_buf, dz_buf, sems):
        del dproj_in
        cps = [pltpu.make_async_copy(dqkv_ref, _window(dproj_ref, QKV0, 3 * BR), sems.at[1]),
               pltpu.make_async_copy(dbg_ref, _window(dproj_ref, BG0, BR), sems.at[2])]
        for cp in cps:
            cp.start()
        _copy_all([(_window(proj_ref, BF0, CHUNK), z_buf)], sems)
        tri = _tri_ones(CHUNK, upper=True)
        dz_buf[...] = jnp.zeros_like(dz_buf)
        n = s // CHUNK

        def chunk(t, carry):
            suffix, acc = carry
            i = n - 1 - t
            rows = pl.ds(pl.multiple_of(i * CHUNK, CHUNK), CHUNK)
            dc = dcum_ref[rows, :]
            dlf = _tri_sum(tri, dc) + suffix
            dz = dlf * _sigmoid(-(z_buf[rows, :] + fb_ref[...]))
            dz_buf[rows, pl.ds(0, CHUNK)] = dz.astype(BF16)
            return dlf[0:1, :], acc + _part8(dz)

        _, acc = lax.fori_loop(0, n, chunk, (jnp.zeros((1, CHUNK), F32), jnp.zeros((8, CHUNK), F32)))
        dfb_ref[...] = jnp.sum(acc, axis=0, keepdims=True)
        _copy_all([(dz_buf, _window(dproj_ref, BF0, pad))], sems)
        for cp in cps:
            cp.wait()

    return pl.pallas_call(
        body, name="attn_prep_bwd", in_specs=[ANY, ANY, ANY, VMEM, ANY, VMEM], out_specs=[ANY, VMEM],
        out_shape=[jax.ShapeDtypeStruct(dproj.shape, dproj.dtype), jax.ShapeDtypeStruct((1, CHUNK), F32)],
        scratch_shapes=[pltpu.VMEM((s, CHUNK), F32), pltpu.VMEM((s, pad), BF16), pltpu.SemaphoreType.DMA((3,))],
        input_output_aliases={1: 0}, compiler_params=_params())(proj, dproj, dqkv, dcum, dbg, f_bias)


ATT_TQ = 256
ATT_TK = 256
NEG_BIG = -1e30


def _causal(q0, k0, tq, tk):
    qpos = q0 + lax.broadcasted_iota(jnp.int32, (tq, tk), 0)
    kpos = k0 + lax.broadcasted_iota(jnp.int32, (tq, tk), 1)
    return qpos >= kpos


def _attn_fwd(qe, ke, ve):
    nh, s, w = qe.shape
    tq = tk = min(ATT_TQ, s)

    def body(q_ref, k_ref, v_ref, o_ref):
        i = pl.program_id(1)
        q = q_ref[0]

        def step(j, carry, masked):
            m, l, acc = carry
            ks = pl.ds(pl.multiple_of(j * tk, tk), tk)
            sc = _dot_nt(q, k_ref[0, ks, :])
            if masked:
                sc = jnp.where(_causal(i * tq, j * tk, tq, tk), sc, NEG_BIG)
            m_new = jnp.maximum(m, jnp.max(sc, axis=-1, keepdims=True))
            alpha = jnp.exp(m - m_new)
            p = jnp.exp(sc - m_new)
            l = alpha * l + jnp.sum(p, axis=-1, keepdims=True)
            acc = alpha * acc + _dot(p.astype(BF16), v_ref[0, ks, :])
            return m_new, l, acc

        init = (jnp.full((tq, 1), NEG_BIG, F32), jnp.zeros((tq, 1), F32), jnp.zeros((tq, w), F32))
        carry = lax.fori_loop(0, i, functools.partial(step, masked=False), init)
        m, l, acc = step(i, carry, True)
        lane = lax.broadcasted_iota(jnp.int32, (tq, w), 1)
        o_ref[0] = jnp.where(lane < HEAD_DIM, acc / l, m + jnp.log(l))

    return pl.pallas_call(
        body, name="attn_fwd", grid=(nh, s // tq),
        in_specs=[pl.BlockSpec((1, tq, w), lambda h, i: (h, i, 0)), pl.BlockSpec((1, s, w), lambda h, i: (h, 0, 0)),
                  pl.BlockSpec((1, s, w), lambda h, i: (h, 0, 0))],
        out_specs=pl.BlockSpec((1, tq, w), lambda h, i: (h, i, 0)),
        out_shape=jax.ShapeDtypeStruct((nh, s, w), F32),
        compiler_params=_params("arbitrary", "arbitrary"))(qe, ke, ve)


def _attn_bwd(qe, ke, ve, doe, stats):
    nh, s, w = qe.shape
    tk = min(ATT_TK, s)
    tq = min(128, s)
    nq = s // tq

    def body(q_ref, k_ref, v_ref, do_ref, st_ref, dq_ref, dk_ref, dv_ref):
        j = pl.program_id(1)

        @pl.when(j == 0)
        def _():
            dq_ref[...] = jnp.zeros_like(dq_ref)

        k = k_ref[0]
        v = v_ref[0]

        def step(i, carry, masked):
            dk, dv = carry
            qs = pl.ds(pl.multiple_of(i * tq, tq), tq)
            q = q_ref[0, qs, :]
            do = do_ref[0, qs, :]
            st = st_ref[0, qs, :]
            sc = _dot_nt(q, k)
            p = jnp.exp(sc - st[:, 0:1])
            if masked:
                p = jnp.where(_causal(i * tq, j * tk, tq, tk), p, 0.0)
            dp = _dot_nt(do, v)
            ds = (p * (dp - st[:, 1:2])).astype(BF16)
            dv = dv + _dot_tn(p.astype(BF16), do)
            dk = dk + _dot_tn(ds, q)
            dq_ref[0, qs, :] += _dot(ds, k)
            return dk, dv

        first = (j * tk) // tq
        n_diag = tk // tq
        carry = (jnp.zeros((tk, w), F32), jnp.zeros((tk, w), F32))
        carry = lax.fori_loop(first, first + n_diag, functools.partial(step, masked=True), carry)
        dk, dv = lax.fori_loop(first + n_diag, nq, functools.partial(step, masked=False), carry)
        dk_ref[0] = dk
        dv_ref[0] = dv

    whole = pl.BlockSpec((1, s, w), lambda h, j: (h, 0, 0))
    blk = pl.BlockSpec((1, tk, w), lambda h, j: (h, j, 0))
    return pl.pallas_call(
        body, name="attn_bwd", grid=(nh, s // tk),
        in_specs=[whole, blk, blk, whole, whole], out_specs=[whole, blk, blk],
        out_shape=[jax.ShapeDtypeStruct((nh, s, w), F32)] * 3,
        compiler_params=_params("arbitrary", "arbitrary"))(qe, ke, ve, doe, stats)


def _bgate_fwd(proj, o):
    s = proj.shape[0]

    def body(proj_ref, o_ref, y_ref, g_buf, sems):
        _copy_all([(_window(proj_ref, BG0, BR), g_buf)], sems)

        def chunk(i, c):
            rows = pl.ds(pl.multiple_of(i * CHUNK, CHUNK), CHUNK)
            sg, _ = _silu_and_grad(g_buf[rows, :])
            y_ref[rows, :] = (o_ref[rows, :] * sg).astype(BF16)
            return c

        lax.fori_loop(0, s // CHUNK, chunk, 0)

    return pl.pallas_call(
        body, name="bgate_fwd", in_specs=[ANY, VMEM], out_specs=VMEM, out_shape=jax.ShapeDtypeStruct((s, BR), BF16),
        scratch_shapes=[pltpu.VMEM((s, BR), F32), pltpu.SemaphoreType.DMA((1,))], compiler_params=_params())(proj, o)


def _bgate_bwd(proj, o, dy):
    s = proj.shape[0]

    def body(proj_ref, o_ref, dy_ref, do_ref, dg_ref, dd_ref, g_buf, sems):
        _copy_all([(_window(proj_ref, BG0, BR), g_buf)], sems)
        lane = lax.broadcasted_iota(jnp.int32, (BR, CHUNK), 0) // HEAD_DIM
        col = lax.broadcasted_iota(jnp.int32, (BR, CHUNK), 1)
        sel = (lane == col).astype(BF16)

        def chunk(i, c):
            rows = pl.ds(pl.multiple_of(i * CHUNK, CHUNK), CHUNK)
            sg, dsg = _silu_and_grad(g_buf[rows, :])
            dyv = dy_ref[rows, :]
            ov = o_ref[rows, :]
            do = dyv * sg
            do_ref[rows, :] = do
            dg_ref[rows, :] = (dyv * ov * dsg).astype(BF16)
            prod = _split3(do * ov)
            dd_ref[rows, :] = _dot(prod[0], sel) + _dot(prod[1], sel) + _dot(prod[2], sel)
            return c

        lax.fori_loop(0, s // CHUNK, chunk, 0)

    return pl.pallas_call(
        body, name="bgate_bwd", in_specs=[ANY, VMEM, VMEM], out_specs=[VMEM, VMEM, VMEM],
        out_shape=[jax.ShapeDtypeStruct((s, BR), F32), jax.ShapeDtypeStruct((s, BR), BF16), jax.ShapeDtypeStruct((s, CHUNK), F32)],
        scratch_shapes=[pltpu.VMEM((s, BR), F32), pltpu.SemaphoreType.DMA((1,))], compiler_params=_params())(proj, o, dy)


def _heads(a):
    return a.reshape(a.shape[0], N_HEADS, HEAD_DIM).transpose(1, 0, 2)


def _unheads(a):
    return a.transpose(1, 0, 2).reshape(a.shape[1], BR)


def _attn_operands(proj, cum):
    s = proj.shape[0]
    q = _heads(proj[:, QKV0:QKV0 + BR] * (1.0 / math.sqrt(HEAD_DIM))).astype(BF16)
    k = _heads(proj[:, QKV0 + BR:QKV0 + 2 * BR]).astype(BF16)
    v = _heads(proj[:, QKV0 + 2 * BR:QKV0 + 3 * BR]).astype(BF16)
    c = cum[:, :N_HEADS].T[:, :, None]
    hi = lax.reduce_precision(c, 8, 7)
    mid = lax.reduce_precision(c - hi, 8, 7)
    lo = lax.reduce_precision(c - hi - mid, 8, 7)
    parts = [hi.astype(BF16), mid.astype(BF16), lo.astype(BF16)]
    ones = jnp.ones((N_HEADS, s, 3), BF16)
    zq = jnp.zeros((N_HEADS, s, CHUNK - HEAD_DIM - 6), BF16)
    qe = jnp.concatenate([q] + parts + [ones, zq], axis=-1)
    ke = jnp.concatenate([k, ones] + [-p for p in parts] + [zq], axis=-1)
    ve = jnp.concatenate([v, jnp.zeros((N_HEADS, s, CHUNK - HEAD_DIM), BF16)], axis=-1)
    return qe, ke, ve


C_PAD = 8


def _mix_c_fwd(proj, w):
    s = proj.shape[0]

    def body(proj_ref, w_ref, y_ref, cin_buf, g_buf, z_buf, sems):
        _copy_all([(_window(proj_ref, CIN0, 3 * BR), cin_buf), (_window(proj_ref, CG0, BR), g_buf)], sems)
        z_buf[pl.ds(0, C_PAD), :] = jnp.zeros((C_PAD, BR), F32)

        def fill(i, c):
            rows = pl.ds(pl.multiple_of(i * CHUNK, CHUNK), CHUNK)
            z_buf[pl.ds(pl.multiple_of(i * CHUNK + C_PAD, 8), CHUNK), :] = cin_buf[rows, BR:2 * BR] * cin_buf[rows, 2 * BR:]
            return c

        lax.fori_loop(0, s // CHUNK, fill, 0)

        def chunk(i, c):
            r0 = pl.multiple_of(i * CHUNK, CHUNK)
            rows = pl.ds(r0, CHUNK)
            ze = z_buf[pl.ds(r0, CHUNK + C_PAD), :]
            conv = jnp.zeros((CHUNK, BR), F32)
            for k in range(K_SHORT):
                off = C_PAD - (K_SHORT - 1) + k
                conv = conv + w_ref[k:k + 1, :] * ze[off:off + CHUNK]
            sg, _ = _silu_and_grad(g_buf[rows, :])
            y_ref[rows, :] = (cin_buf[rows, 0:BR] * conv * sg).astype(BF16)
            return c

        lax.fori_loop(0, s // CHUNK, chunk, 0)

    return pl.pallas_call(
        body, name="mix_c_fwd", in_specs=[ANY, VMEM], out_specs=VMEM, out_shape=jax.ShapeDtypeStruct((s, BR), BF16),
        scratch_shapes=[pltpu.VMEM((s, 3 * BR), F32), pltpu.VMEM((s, BR), F32), pltpu.VMEM((s + C_PAD, BR), F32),
                        pltpu.SemaphoreType.DMA((2,))],
        compiler_params=_params())(proj, w)


def _mix_c_bwd(proj, dproj, dy, w):
    s = proj.shape[0]

    def body(proj_ref, dproj_in, dy_ref, w_ref, dproj_ref, dw_ref, cin_buf, g_buf, z_buf, dc_buf, dcin_buf, dg_buf, acc, sems):
        del dproj_in
        _copy_all([(_window(proj_ref, CIN0, 3 * BR), cin_buf), (_window(proj_ref, CG0, BR), g_buf)], sems)
        z_buf[pl.ds(0, C_PAD), :] = jnp.zeros((C_PAD, BR), F32)
        dc_buf[pl.ds(s, C_PAD), :] = jnp.zeros((C_PAD, BR), F32)
        acc[...] = jnp.zeros_like(acc)

        def fill(i, c):
            rows = pl.ds(pl.multiple_of(i * CHUNK, CHUNK), CHUNK)
            z_buf[pl.ds(pl.multiple_of(i * CHUNK + C_PAD, 8), CHUNK), :] = cin_buf[rows, BR:2 * BR] * cin_buf[rows, 2 * BR:]
            return c

        lax.fori_loop(0, s // CHUNK, fill, 0)

        def chunk(i, c):
            r0 = pl.multiple_of(i * CHUNK, CHUNK)
            rows = pl.ds(r0, CHUNK)
            ze = z_buf[pl.ds(r0, CHUNK + C_PAD), :]
            taps = [ze[C_PAD - (K_SHORT - 1) + k:C_PAD - (K_SHORT - 1) + k + CHUNK] for k in range(K_SHORT)]
            conv = jnp.zeros((CHUNK, BR), F32)
            for k in range(K_SHORT):
                conv = conv + w_ref[k:k + 1, :] * taps[k]
            sg, dsg = _silu_and_grad(g_buf[rows, :])
            bg = cin_buf[rows, 0:BR]
            dyv = dy_ref[rows, :]
            dconv = dyv * bg * sg
            dc_buf[rows, :] = dconv
            dcin_buf[rows, 0:BR] = (dyv * conv * sg).astype(BF16)
            dg_buf[rows, :] = (dyv * bg * conv * dsg).astype(BF16)
            for k in range(K_SHORT):
                acc[k] += _part8(dconv * taps[k])
            return c

        lax.fori_loop(0, s // CHUNK, chunk, 0)

        def chunk2(i, c):
            r0 = pl.multiple_of(i * CHUNK, CHUNK)
            rows = pl.ds(r0, CHUNK)
            de = dc_buf[pl.ds(r0, CHUNK + C_PAD), :]
            dz = jnp.zeros((CHUNK, BR), F32)
            for k in range(K_SHORT):
                off = K_SHORT - 1 - k
                dz = dz + w_ref[k:k + 1, :] * de[off:off + CHUNK]
            dcin_buf[rows, BR:2 * BR] = (dz * cin_buf[rows, 2 * BR:]).astype(BF16)
            dcin_buf[rows, 2 * BR:] = (dz * cin_buf[rows, BR:2 * BR]).astype(BF16)
            return c

        lax.fori_loop(0, s // CHUNK, chunk2, 0)
        dw_ref[...] = jnp.zeros_like(dw_ref)
        for k in range(K_SHORT):
            dw_ref[k:k + 1, :] = jnp.sum(acc[k], axis=0, keepdims=True)
        _copy_all([(dcin_buf, _window(dproj_ref, CIN0, 3 * BR)), (dg_buf, _window(dproj_ref, CG0, BR))], sems)

    return pl.pallas_call(
        body, name="mix_c_bwd", in_specs=[ANY, ANY, VMEM, VMEM], out_specs=[ANY, VMEM],
        out_shape=[jax.ShapeDtypeStruct(dproj.shape, dproj.dtype), jax.ShapeDtypeStruct((8, BR), F32)],
        scratch_shapes=[pltpu.VMEM((s, 3 * BR), F32), pltpu.VMEM((s, BR), F32), pltpu.VMEM((s + C_PAD, BR), F32),
                        pltpu.VMEM((s + C_PAD, BR), F32), pltpu.VMEM((s, 3 * BR), BF16), pltpu.VMEM((s, BR), BF16),
                        pltpu.VMEM((K_SHORT, 8, BR), F32), pltpu.SemaphoreType.DMA((2,))],
        input_output_aliases={1: 0}, compiler_params=_params())(proj, dproj, dy, w)


D_PAD = 32


def _mix_d_common(cin_ref, g_ref, w_ref, b_ref, lg_ref, lb_ref, hh_buf, r0):
    he = hh_buf[pl.ds(r0, CHUNK + D_PAD), :]
    taps = [he[D_PAD - (K_CONF - 1) + k:D_PAD - (K_CONF - 1) + k + CHUNK] for k in range(K_CONF)]
    conv = jnp.zeros((CHUNK, BR), F32) + b_ref[...]
    for k in range(K_CONF):
        conv = conv + w_ref[k:k + 1, :] * taps[k]
    xh, rs = _ln_fwd(conv)
    sw, dsw = _silu_and_grad(xh * lg_ref[...] + lb_ref[...])
    sg, dsg = _silu_and_grad(g_ref[pl.ds(r0, CHUNK), :])
    return taps, xh, rs, sw, dsw, sg, dsg


def _mix_d_fill(cin_buf, hh_buf, s):
    hh_buf[pl.ds(0, D_PAD), :] = jnp.zeros((D_PAD, BR), F32)

    def fill(i, c):
        rows = pl.ds(pl.multiple_of(i * CHUNK, CHUNK), CHUNK)
        hh_buf[pl.ds(pl.multiple_of(i * CHUNK + D_PAD, 8), CHUNK), :] = cin_buf[rows, 0:BR] * _sigmoid(cin_buf[rows, BR:])
        return c

    lax.fori_loop(0, s // CHUNK, fill, 0)


def _mix_d_fwd(proj, w, b, lg, lb):
    s = proj.shape[0]

    def body(proj_ref, w_ref, b_ref, lg_ref, lb_ref, y_ref, cin_buf, g_buf, hh_buf, sems):
        _copy_all([(_window(proj_ref, DGLU0, 2 * BR), cin_buf), (_window(proj_ref, DG0, BR), g_buf)], sems)
        _mix_d_fill(cin_buf, hh_buf, s)

        def chunk(i, c):
            r0 = pl.multiple_of(i * CHUNK, CHUNK)
            _, _, _, sw, _, sg, _ = _mix_d_common(cin_buf, g_buf, w_ref, b_ref, lg_ref, lb_ref, hh_buf, r0)
            y_ref[pl.ds(r0, CHUNK), :] = (sw * sg).astype(BF16)
            return c

        lax.fori_loop(0, s // CHUNK, chunk, 0)

    return pl.pallas_call(
        body, name="mix_d_fwd", in_specs=[ANY, VMEM, VMEM, VMEM, VMEM], out_specs=VMEM,
        out_shape=jax.ShapeDtypeStruct((s, BR), BF16),
        scratch_shapes=[pltpu.VMEM((s, 2 * BR), F32), pltpu.VMEM((s, BR), F32), pltpu.VMEM((s + D_PAD, BR), F32),
                        pltpu.SemaphoreType.DMA((2,))],
        compiler_params=_params())(proj, w, b, lg, lb)


def _mix_d_bwd(proj, dproj, dy, w, b, lg, lb):
    s = proj.shape[0]

    def body(proj_ref, dproj_in, dy_ref, w_ref, b_ref, lg_ref, lb_ref, dproj_ref, dw_ref, db_ref, dlg_ref, dlb_ref,
             cin_buf, g_buf, hh_buf, dc_buf, dcin_buf, dg_buf, acc_w, acc_s, sems):
        del dproj_in
        _copy_all([(_window(proj_ref, DGLU0, 2 * BR), cin_buf), (_window(proj_ref, DG0, BR), g_buf)], sems)
        _mix_d_fill(cin_buf, hh_buf, s)
        dc_buf[pl.ds(s, D_PAD), :] = jnp.zeros((D_PAD, BR), F32)
        acc_w[...] = jnp.zeros_like(acc_w)
        acc_s[...] = jnp.zeros_like(acc_s)

        def chunk(i, c):
            r0 = pl.multiple_of(i * CHUNK, CHUNK)
            rows = pl.ds(r0, CHUNK)
            taps, xh, rs, sw, dsw, sg, dsg = _mix_d_common(cin_buf, g_buf, w_ref, b_ref, lg_ref, lb_ref, hh_buf, r0)
            dyv = dy_ref[rows, :]
            dg_buf[rows, :] = (dyv * sw * dsg).astype(BF16)
            d_ln = dyv * sg * dsw
            acc_s[0] += _part8(d_ln * xh)
            acc_s[1] += _part8(d_ln)
            dc = _ln_bwd(d_ln * lg_ref[...], xh, rs)
            acc_s[2] += _part8(dc)
            dc_buf[rows, :] = dc
            for k in range(K_CONF):
                acc_w[k] += _part8(dc * taps[k])
            return c

        lax.fori_loop(0, s // CHUNK, chunk, 0)

        def chunk2(i, c):
            r0 = pl.multiple_of(i * CHUNK, CHUNK)
            rows = pl.ds(r0, CHUNK)
            de = dc_buf[pl.ds(r0, CHUNK + D_PAD), :]
            dh = jnp.zeros((CHUNK, BR), F32)
            for k in range(K_CONF):
                off = K_CONF - 1 - k
                dh = dh + w_ref[k:k + 1, :] * de[off:off + CHUNK]
            a = cin_buf[rows, 0:BR]
            sig = _sigmoid(cin_buf[rows, BR:])
            dcin_buf[rows, 0:BR] = (dh * sig).astype(BF16)
            dcin_buf[rows, BR:] = (dh * a * sig * (1.0 - sig)).astype(BF16)
            return c

        lax.fori_loop(0, s // CHUNK, chunk2, 0)
        dw_ref[...] = jnp.zeros_like(dw_ref)
        for k in range(K_CONF):
            dw_ref[k:k + 1, :] = jnp.sum(acc_w[k], axis=0, keepdims=True)
        dlg_ref[...] = jnp.sum(acc_s[0], axis=0, keepdims=True)
        dlb_ref[...] = jnp.sum(acc_s[1], axis=0, keepdims=True)
        db_ref[...] = jnp.sum(acc_s[2], axis=0, keepdims=True)
        _copy_all([(dcin_buf, _window(dproj_ref, DGLU0, 2 * BR)), (dg_buf, _window(dproj_ref, DG0, BR))], sems)

    vec = jax.ShapeDtypeStruct((1, BR), F32)
    return pl.pallas_call(
        body, name="mix_d_bwd", in_specs=[ANY, ANY, VMEM, VMEM, VMEM, VMEM, VMEM], out_specs=[ANY, VMEM, VMEM, VMEM, VMEM],
        out_shape=[jax.ShapeDtypeStruct(dproj.shape, dproj.dtype), jax.ShapeDtypeStruct((32, BR), F32), vec, vec, vec],
        scratch_shapes=[pltpu.VMEM((s, 2 * BR), F32), pltpu.VMEM((s, BR), F32), pltpu.VMEM((s + D_PAD, BR), F32),
                        pltpu.VMEM((s + D_PAD, BR), F32), pltpu.VMEM((s, 2 * BR), BF16), pltpu.VMEM((s, BR), BF16),
                        pltpu.VMEM((K_CONF, 8, BR), F32), pltpu.VMEM((3, 8, BR), F32), pltpu.SemaphoreType.DMA((2,))],
        input_output_aliases={1: 0}, compiler_params=_params())(proj, dproj, dy, w, b, lg, lb)


MERGE_TM = 256


def _merge_fwd(x, proj, ys, wb, wo):
    s = x.shape[0]
    tm = min(MERGE_TM, s)

    def body(x_ref, lg_ref, ya, yb, yc, yd, wb_ref, wo_ref, xn_ref, mg_ref):
        merged = jnp.zeros((tm, D_MODEL), F32)
        for n, y_ref in enumerate((ya, yb, yc, yd)):
            gate = _sigmoid(lg_ref[:, n * D_MODEL:(n + 1) * D_MODEL])
            merged = merged + gate * _dot(y_ref[...], wb_ref[n])
        mb = merged.astype(BF16)
        mg_ref[...] = mb
        xn_ref[...] = x_ref[...] + _dot(mb, wo_ref[...])

    row = lambda w: pl.BlockSpec((tm, w), lambda i: (i, 0))
    return pl.pallas_call(
        body, name="merge_fwd", grid=(s // tm,),
        in_specs=[row(D_MODEL), pl.BlockSpec((tm, MRG0), lambda i: (i, 1)), row(BR), row(BR), row(BR), row(BR),
                  pl.BlockSpec((N_HEADS, BR, D_MODEL), lambda i: (0, 0, 0)), pl.BlockSpec((D_MODEL, D_MODEL), lambda i: (0, 0))],
        out_specs=[row(D_MODEL), row(D_MODEL)],
        out_shape=[jax.ShapeDtypeStruct((s, D_MODEL), F32), jax.ShapeDtypeStruct((s, D_MODEL), BF16)],
        compiler_params=_params("arbitrary"))(x, proj, *ys, wb, wo)


def _merge_bwd(dxn, proj, ys, merged, wb, wo):
    s = dxn.shape[0]
    tm = min(MERGE_TM, s)

    def body(dx_ref, lg_ref, ya, yb, yc, yd, mg_ref, wb_ref, wo_ref, dlg_ref, da, db, dc, dd, dwo_ref, dwb_ref):
        @pl.when(pl.program_id(0) == 0)
        def _():
            dwo_ref[...] = jnp.zeros_like(dwo_ref)
            dwb_ref[...] = jnp.zeros_like(dwb_ref)

        dxb = dx_ref[...].astype(BF16)
        d_merged = _dot_nt(dxb, wo_ref[...])
        dwo_ref[...] += _dot_tn(mg_ref[...], dxb)
        for n, (y_ref, dy_ref) in enumerate(((ya, da), (yb, db), (yc, dc), (yd, dd))):
            yv = y_ref[...]
            gate = _sigmoid(lg_ref[:, n * D_MODEL:(n + 1) * D_MODEL])
            pn = _dot(yv, wb_ref[n])
            dlg_ref[:, n * D_MODEL:(n + 1) * D_MODEL] = (d_merged * pn * gate * (1.0 - gate)).astype(BF16)
            dpn = (d_merged * gate).astype(BF16)
            dy_ref[...] = _dot_nt(dpn, wb_ref[n])
            dwb_ref[n] += _dot_tn(yv, dpn)

    row = lambda w: pl.BlockSpec((tm, w), lambda i: (i, 0))
    wb_spec = pl.BlockSpec((N_HEADS, BR, D_MODEL), lambda i: (0, 0, 0))
    wo_spec = pl.BlockSpec((D_MODEL, D_MODEL), lambda i: (0, 0))
    dy_shape = jax.ShapeDtypeStruct((s, BR), F32)
    outs = pl.pallas_call(
        body, name="merge_bwd", grid=(s // tm,),
        in_specs=[row(D_MODEL), pl.BlockSpec((tm, MRG0), lambda i: (i, 1)), row(BR), row(BR), row(BR), row(BR), row(D_MODEL),
                  wb_spec, wo_spec],
        out_specs=[pl.BlockSpec((tm, MRG0), lambda i: (i, 1)), row(BR), row(BR), row(BR), row(BR), wo_spec, wb_spec],
        out_shape=[jax.ShapeDtypeStruct((s, PCOLS), BF16), dy_shape, dy_shape, dy_shape, dy_shape,
                   jax.ShapeDtypeStruct((D_MODEL, D_MODEL), F32), jax.ShapeDtypeStruct((N_HEADS, BR, D_MODEL), F32)],
        compiler_params=_params("arbitrary"))(dxn, proj, *ys, merged, wb, wo)
    return outs[0], outs[1:5], outs[5], outs[6]


def _loss_head(x, target, g):
    s = x.shape[0]
    tm = min(512, s)

    def body(x_ref, t_ref, g_ref, loss_ref, dx_ref, dg_ref):
        @pl.when(pl.program_id(0) == 0)
        def _():
            loss_ref[...] = jnp.zeros_like(loss_ref)
            dg_ref[...] = jnp.zeros_like(dg_ref)

        xv = x_ref[...]
        gv = g_ref[...]
        r = lax.rsqrt(jnp.mean(xv * xv, axis=-1, keepdims=True) + EPS)
        xn = xv * r
        err = xn * gv - t_ref[...]
        loss_ref[...] += 0.5 * jnp.sum(jnp.mean(err * err, axis=-1, keepdims=True))
        dy = err * (1.0 / D_MODEL)
        dg_ref[...] += _part8(dy * xn)
        gy = dy * gv
        dx_ref[...] = r * (gy - xn * jnp.mean(xn * gy, axis=-1, keepdims=True))

    row = pl.BlockSpec((tm, D_MODEL), lambda i: (i, 0))
    return pl.pallas_call(
        body, name="loss_head", grid=(s // tm,),
        in_specs=[row, row, pl.BlockSpec((1, D_MODEL), lambda i: (0, 0))],
        out_specs=[pl.BlockSpec((8, 128), lambda i: (0, 0)), row, pl.BlockSpec((8, D_MODEL), lambda i: (0, 0))],
        out_shape=[jax.ShapeDtypeStruct((8, 128), F32), jax.ShapeDtypeStruct((s, D_MODEL), F32), jax.ShapeDtypeStruct((8, D_MODEL), F32)],
        compiler_params=_params("arbitrary"))(x, target, g)


_SEGMENTS = ((0, 512, AUV0), (512, 256, AG0), (768, 768, QKV0), (1536, 4, BF0), (1540, 256, BG0), (1796, 768, CIN0),
             (2564, 256, CG0), (2820, 512, DGLU0), (3332, 256, DG0), (3588, 4096, MRG0))


def _to_aligned_cols(w):
    out, pos = [], 0
    for src, width, dst in sorted(_SEGMENTS, key=lambda t: t[2]):
        if dst > pos:
            out.append(jnp.zeros(w.shape[:-1] + (dst - pos,), w.dtype))
        out.append(w[..., src:src + width])
        pos = dst + width
    return jnp.concatenate(out, axis=-1)


def _from_aligned_cols(g):
    return jnp.concatenate([g[..., dst:dst + width] for _, width, dst in _SEGMENTS], axis=-1)


def _sgu_bias_rows(sgu_b):
    return jnp.repeat(sgu_b.T, HEAD_DIM, axis=1)


def _layer_fwd(x, p):
    proj, h = _inproj_fwd(x, p["norm_g"], p["w_in"])
    ya = _mix_a_fwd(proj, p["sgu_w"], _sgu_bias_rows(p["sgu_b"]), p["sgu_ln_g"], p["sgu_ln_b"])
    cum = _attn_prep_fwd(proj, p["f_bias"])
    qe, ke, ve = _attn_operands(proj, cum)
    oe = _attn_fwd(qe, ke, ve)
    o = _unheads(oe[:, :, :HEAD_DIM])
    yb = _bgate_fwd(proj, o)
    yc = _mix_c_fwd(proj, p["short_conv_w"])
    yd = _mix_d_fwd(proj, p["conf_dw_w"], p["conf_dw_b"], p["conf_ln_g"], p["conf_ln_b"])
    ys = (ya, yb, yc, yd)
    x_next, merged = _merge_fwd(x, proj, ys, p["w_branch"], p["w_out"])
    saved = dict(x=x, proj=proj, h=h, ys=ys, merged=merged, qe=qe, ke=ke, ve=ve, oe=oe, o=o)
    return x_next, saved


def _layer_bwd(dxn, p, sv):
    proj = sv["proj"]
    dproj, dys, d_wo, d_wb = _merge_bwd(dxn, proj, sv["ys"], sv["merged"], p["w_branch"], p["w_out"])
    dproj, d_sgu_w, d_bias_rows, d_sgu_lg, d_sgu_lb = _mix_a_bwd(
        proj, dproj, dys[0], p["sgu_w"], _sgu_bias_rows(p["sgu_b"]), p["sgu_ln_g"], p["sgu_ln_b"])
    do, d_bg, dd = _bgate_bwd(proj, sv["o"], dys[1])
    doe = jnp.concatenate([_heads(do).astype(BF16), jnp.zeros((N_HEADS, do.shape[0], CHUNK - HEAD_DIM), BF16)], axis=-1)
    lse = sv["oe"][:, :, HEAD_DIM:HEAD_DIM + 1]
    stats = jnp.concatenate([lse, dd[:, :N_HEADS].T[:, :, None], jnp.zeros((N_HEADS, do.shape[0], CHUNK - 2), F32)], axis=-1)
    dqe, dke, dve = _attn_bwd(sv["qe"], sv["ke"], sv["ve"], doe, stats)
    scale = 1.0 / math.sqrt(HEAD_DIM)
    dqkv = jnp.concatenate([_unheads(dqe[:, :, :HEAD_DIM]) * scale, _unheads(dke[:, :, :HEAD_DIM]),
                            _unheads(dve[:, :, :HEAD_DIM])], axis=1).astype(BF16)
    dcum = (dqe[:, :, HEAD_DIM] - dke[:, :, HEAD_DIM + 3]).T
    dcum = jnp.pad(dcum, ((0, 0), (0, CHUNK - N_HEADS)))
    dproj, d_fb = _attn_prep_bwd(proj, dproj, dqkv, dcum, d_bg, p["f_bias"])
    dproj, d_sc = _mix_c_bwd(proj, dproj, dys[2], p["short_conv_w"])
    dproj, d_cw, d_cb, d_clg, d_clb = _mix_d_bwd(proj, dproj, dys[3], p["conf_dw_w"], p["conf_dw_b"], p["conf_ln_g"], p["conf_ln_b"])
    dx, dg8 = _inproj_bwd_x(dproj, p["w_in"], sv["x"], dxn, p["norm_g"])
    d_win = _inproj_bwd_w(sv["h"], dproj)
    grads = dict(
        norm_g=jnp.sum(dg8, axis=0), w_in=d_win, f_bias=d_fb[0, :N_HEADS], sgu_w=d_sgu_w,
        sgu_b=d_bias_rows.reshape(CHUNK, N_HEADS, HEAD_DIM).sum(axis=-1).T,
        sgu_ln_g=d_sgu_lg[0], sgu_ln_b=d_sgu_lb[0], short_conv_w=d_sc[:K_SHORT], conf_dw_w=d_cw[:K_CONF],
        conf_dw_b=d_cb[0], conf_ln_g=d_clg[0], conf_ln_b=d_clb[0], w_branch=d_wb, w_out=d_wo)
    return dx, grads


def _local_step(x, target, layers, final_g):
    saved = []
    for p in layers:
        x, sv = _layer_fwd(x, p)
        saved.append(sv)
    loss8, dx, dfg8 = _loss_head(x, target, final_g)
    grads = [None] * len(layers)
    for l in reversed(range(len(layers))):
        dx, grads[l] = _layer_bwd(dx, layers[l], saved[l])
    return loss8[0, 0], dx, grads, jnp.sum(dfg8, axis=0)


def _place():
    return lax.axis_index("x"), lax.axis_index("y"), lax.axis_index("c")


def _other_chips(x, y):
    return [(1 - x, y), (x, 1 - y), (1 - x, 1 - y)]


def _all_gather8(blk):
    r, c = blk.shape

    def body(x_ref, out_ref, send_sems, recv_sems, local_sem):
        x, y, cc = _place()
        me, sibling = (x, y, cc), (x, y, 1 - cc)
        chips = _other_chips(x, y)

        def slot(px, py, pc):
            return out_ref.at[4 * px + 2 * py + pc]

        def copy(k, block, to, src=None):
            return pltpu.make_async_remote_copy(
                src_ref=slot(*block) if src is None else src, dst_ref=slot(*block),
                send_sem=send_sems.at[k], recv_sem=recv_sems.at[k], device_id=to, device_id_type=MESH)

        mine = pltpu.make_async_copy(x_ref, slot(*me), local_sem)
        mine.start()
        first = [copy(0, me, sibling, src=x_ref)]
        first += [copy(1 + j, me, (*chip, cc), src=x_ref) for j, chip in enumerate(chips)]
        for cp in first:
            cp.start()
        passed = [copy(4 + j, (*chip, cc), sibling) for j, chip in enumerate(chips)]
        for j, chip in enumerate(chips):
            copy(1 + j, (*chip, cc), me).wait_recv()
            passed[j].start()
        copy(0, sibling, me).wait_recv()
        for j, chip in enumerate(chips):
            copy(4 + j, (*chip, 1 - cc), me).wait_recv()
        for cp in first + passed:
            cp.wait_send()
        mine.wait()

    return pl.pallas_call(
        body, name="all_gather8", in_specs=[ANY], out_specs=ANY, out_shape=jax.ShapeDtypeStruct((8, r, c), blk.dtype),
        scratch_shapes=[pltpu.SemaphoreType.DMA((7,)), pltpu.SemaphoreType.DMA((7,)), pltpu.SemaphoreType.DMA])(blk)


def _pair_exchange(send):
    def body(s_ref, r_ref, send_sem, recv_sem):
        x, y, cc = _place()
        cp = pltpu.make_async_remote_copy(src_ref=s_ref, dst_ref=r_ref, send_sem=send_sem, recv_sem=recv_sem,
                                          device_id=(x, y, 1 - cc), device_id_type=MESH)
        cp.start()
        cp.wait()

    return pl.pallas_call(
        body, name="pair_exchange", in_specs=[ANY], out_specs=ANY, out_shape=jax.ShapeDtypeStruct(send.shape, send.dtype),
        scratch_shapes=[pltpu.SemaphoreType.DMA, pltpu.SemaphoreType.DMA])(send)


def _chip_exchange(h):
    _, r, c = h.shape

    def body(h_ref, r_ref, send_sems, recv_sems):
        x, y, cc = _place()
        cps = [pltpu.make_async_remote_copy(src_ref=h_ref.at[2 * px + py], dst_ref=r_ref.at[k], send_sem=send_sems.at[k],
                                            recv_sem=recv_sems.at[k], device_id=(px, py, cc), device_id_type=MESH)
               for k, (px, py) in enumerate(_other_chips(x, y))]
        for cp in cps:
            cp.start()
        for cp in cps:
            cp.wait()

    return pl.pallas_call(
        body, name="chip_exchange", in_specs=[ANY], out_specs=ANY, out_shape=jax.ShapeDtypeStruct((3, r, c), h.dtype),
        scratch_shapes=[pltpu.SemaphoreType.DMA((3,)), pltpu.SemaphoreType.DMA((3,))])(h)


def _pair_gather(blk):
    r, c = blk.shape

    def body(x_ref, out_ref, send_sem, recv_sem, local_sem):
        x, y, cc = _place()
        mine = pltpu.make_async_copy(x_ref, out_ref.at[cc], local_sem)
        mine.start()
        push = pltpu.make_async_remote_copy(src_ref=x_ref, dst_ref=out_ref.at[cc], send_sem=send_sem, recv_sem=recv_sem,
                                            device_id=(x, y, 1 - cc), device_id_type=MESH)
        push.start()
        pltpu.make_async_remote_copy(src_ref=x_ref, dst_ref=out_ref.at[1 - cc], send_sem=send_sem, recv_sem=recv_sem,
                                     device_id=(x, y, 1 - cc), device_id_type=MESH).wait_recv()
        push.wait_send()
        mine.wait()

    return pl.pallas_call(
        body, name="pair_gather", in_specs=[ANY], out_specs=ANY, out_shape=jax.ShapeDtypeStruct((2, r, c), blk.dtype),
        scratch_shapes=[pltpu.SemaphoreType.DMA, pltpu.SemaphoreType.DMA, pltpu.SemaphoreType.DMA])(blk)


def _allreduce8(buf):
    r, c = buf.shape

    def body(x_ref, o_ref, recv_buf, send_sems, recv_sems):
        x, y, cc = _place()
        o_ref[...] = x_ref[...]
        for k, peer in enumerate([(x, y, 1 - cc), (1 - x, y, cc), (x, 1 - y, cc)]):
            cp = pltpu.make_async_remote_copy(src_ref=o_ref, dst_ref=recv_buf.at[k], send_sem=send_sems.at[k],
                                              recv_sem=recv_sems.at[k], device_id=peer, device_id_type=MESH)
            cp.start()
            cp.wait()
            o_ref[...] = o_ref[...] + recv_buf[k]

    return pl.pallas_call(
        body, name="allreduce8", in_specs=[VMEM], out_specs=VMEM, out_shape=jax.ShapeDtypeStruct((r, c), F32),
        scratch_shapes=[pltpu.VMEM((3, r, c), F32), pltpu.SemaphoreType.DMA((3,)), pltpu.SemaphoreType.DMA((3,))],
        compiler_params=_params())(buf)


def _row_tile(rows, cols, limit_bytes=1 << 20):
    t = rows
    while t % 16 == 0 and t * cols * 4 > limit_bytes:
        t //= 2
    return t


def _add_pair(a, b):
    r, c = a.shape
    t = _row_tile(r, c)

    def body(a_ref, b_ref, o_ref):
        o_ref[...] = (a_ref[...].astype(F32) + b_ref[...].astype(F32)).astype(o_ref.dtype)

    spec = pl.BlockSpec((t, c), lambda i: (i, 0))
    return pl.pallas_call(body, name="add_pair", grid=(r // t,), in_specs=[spec, spec], out_specs=spec,
                          out_shape=jax.ShapeDtypeStruct((r, c), a.dtype), compiler_params=_params("arbitrary"))(a, b)


def _sum_chips(own, recv):
    r, c = own.shape
    t = _row_tile(r, c)

    def body(o_ref, r_ref, out_ref):
        acc = o_ref[...].astype(F32)
        for k in range(3):
            acc = acc + r_ref[k].astype(F32)
        out_ref[...] = acc

    return pl.pallas_call(
        body, name="sum_chips", grid=(r // t,),
        in_specs=[pl.BlockSpec((t, c), lambda i: (i, 0)), pl.BlockSpec((3, t, c), lambda i: (0, i, 0))],
        out_specs=pl.BlockSpec((t, c), lambda i: (i, 0)), out_shape=jax.ShapeDtypeStruct((r, c), F32),
        compiler_params=_params("arbitrary"))(own, recv)


def _adamw(w, m, v, g):
    r, c = w.shape
    t = _row_tile(r, c)

    def body(w_ref, m_ref, v_ref, g_ref, d_ref, mo_ref, vo_ref):
        gv = g_ref[...]
        mn = ADAM_B1 * m_ref[...] + (1.0 - ADAM_B1) * gv
        vn = ADAM_B2 * v_ref[...] + (1.0 - ADAM_B2) * (gv * gv)
        m_hat = mn / (1.0 - ADAM_B1 ** ADAM_STEP)
        v_hat = vn / (1.0 - ADAM_B2 ** ADAM_STEP)
        d_ref[...] = -ADAM_LR * (m_hat / (jnp.sqrt(v_hat) + ADAM_EPS) + ADAM_WD * w_ref[...])
        mo_ref[...] = mn
        vo_ref[...] = vn

    spec = pl.BlockSpec((t, c), lambda i: (i, 0))
    shape = jax.ShapeDtypeStruct((r, c), F32)
    return pl.pallas_call(body, name="adamw", grid=(r // t,), in_specs=[spec] * 4, out_specs=[spec] * 3,
                          out_shape=[shape] * 3, compiler_params=_params("arbitrary"))(w, m, v, g)


WIN_ROWS = D_MODEL * SHARD_PAD // D_MODEL
WB_ROWS = N_HEADS * BR * BR // D_MODEL
WO_ROWS = BR
BLOCK_ROWS = WIN_ROWS + WB_ROWS + WO_ROWS

SMALL = ("norm_g", "f_bias", "sgu_w", "sgu_b", "sgu_ln_g", "sgu_ln_b", "short_conv_w", "conf_dw_w", "conf_dw_b",
         "conf_ln_g", "conf_ln_b", "final_g")
WEIGHTS = ("norm_g", "w_in", "f_bias", "sgu_w", "sgu_b", "sgu_ln_g", "sgu_ln_b", "short_conv_w", "conf_dw_w",
           "conf_dw_b", "conf_ln_g", "conf_ln_b", "w_branch", "w_out", "final_g")


def _pack(arrays):
    flat = jnp.concatenate([a.reshape(-1) for a in arrays])
    n = flat.shape[0]
    pad = (-n) % 1024
    return jnp.pad(flat, (0, pad)).reshape(-1, 128)


def _unpack(buf, shapes):
    flat, out, pos = buf.reshape(-1), [], 0
    for shp in shapes:
        n = math.prod(shp)
        out.append(flat[pos:pos + n].reshape(shp))
        pos += n
    return out


def _shard_block(w_in_l, w_branch_l, w_out_l):
    wi = jnp.pad(w_in_l, ((0, 0), (0, SHARD_PAD - SHARD_COLS))).astype(BF16).reshape(WIN_ROWS, D_MODEL)
    return jnp.concatenate([wi, w_branch_l.astype(BF16).reshape(WB_ROWS, D_MODEL), w_out_l.astype(BF16)], axis=0)


def kernel(x, norm_g, w_in, f_bias, sgu_w, sgu_b, sgu_ln_g, sgu_ln_b, short_conv_w, conf_dw_w, conf_dw_b, conf_ln_g, conf_ln_b, w_branch, w_out, final_g, loss_target, m_norm_g, m_w_in, m_f_bias, m_sgu_w, m_sgu_b, m_sgu_ln_g, m_sgu_ln_b, m_short_conv_w, m_conf_dw_w, m_conf_dw_b, m_conf_ln_g, m_conf_ln_b, m_w_branch, m_w_out, m_final_g, v_norm_g, v_w_in, v_f_bias, v_sgu_w, v_sgu_b, v_sgu_ln_g, v_sgu_ln_b, v_short_conv_w, v_conf_dw_w, v_conf_dw_b, v_conf_ln_g, v_conf_ln_b, v_w_branch, v_w_out, v_final_g):
    px, py, pc = _place()
    chip = 2 * px + py
    depth = w_in.shape[0]
    w = dict(norm_g=norm_g, w_in=w_in, f_bias=f_bias, sgu_w=sgu_w, sgu_b=sgu_b, sgu_ln_g=sgu_ln_g, sgu_ln_b=sgu_ln_b,
             short_conv_w=short_conv_w, conf_dw_w=conf_dw_w, conf_dw_b=conf_dw_b, conf_ln_g=conf_ln_g, conf_ln_b=conf_ln_b,
             w_branch=w_branch, w_out=w_out, final_g=final_g)
    m = dict(norm_g=m_norm_g, w_in=m_w_in, f_bias=m_f_bias, sgu_w=m_sgu_w, sgu_b=m_sgu_b, sgu_ln_g=m_sgu_ln_g,
             sgu_ln_b=m_sgu_ln_b, short_conv_w=m_short_conv_w, conf_dw_w=m_conf_dw_w, conf_dw_b=m_conf_dw_b,
             conf_ln_g=m_conf_ln_g, conf_ln_b=m_conf_ln_b, w_branch=m_w_branch, w_out=m_w_out, final_g=m_final_g)
    v = dict(norm_g=v_norm_g, w_in=v_w_in, f_bias=v_f_bias, sgu_w=v_sgu_w, sgu_b=v_sgu_b, sgu_ln_g=v_sgu_ln_g,
             sgu_ln_b=v_sgu_ln_b, short_conv_w=v_short_conv_w, conf_dw_w=v_conf_dw_w, conf_dw_b=v_conf_dw_b,
             conf_ln_g=v_conf_ln_g, conf_ln_b=v_conf_ln_b, w_branch=v_w_branch, w_out=v_w_out, final_g=v_final_g)

    blocks = jnp.stack([_shard_block(w_in[l], w_branch[l], w_out[l]) for l in range(depth)])
    wall = _all_gather8(lax.dynamic_index_in_dim(blocks, pc, 0, keepdims=False))
    wall = wall.reshape(N_CHIPS, depth, BLOCK_ROWS, D_MODEL)
    conv_ch = BR // N_CHIPS
    place = lambda a: lax.dynamic_update_slice_in_dim(jnp.zeros(a.shape[:-1] + (BR,), F32), a, conv_ch * chip, axis=2)
    conv_shapes = [(depth, K_SHORT, BR), (depth, K_CONF, BR)]
    conv_buf = _pack([place(short_conv_w), place(conf_dw_w)]) * (pc == 0).astype(F32)
    short_full, conf_full = _unpack(_allreduce8(conv_buf), conv_shapes)

    layers = []
    for l in range(depth):
        wi = jnp.concatenate([wall[j, l, :WIN_ROWS].reshape(D_MODEL, SHARD_PAD)[:, :SHARD_COLS] for j in range(N_CHIPS)], axis=1)
        wb = jnp.concatenate([wall[j, l, WIN_ROWS:WIN_ROWS + WB_ROWS].reshape(N_HEADS, BR, BR) for j in range(N_CHIPS)], axis=2)
        wo = jnp.concatenate([wall[j, l, WIN_ROWS + WB_ROWS:] for j in range(N_CHIPS)], axis=0)
        layers.append(dict(
            norm_g=norm_g[l][None], w_in=_to_aligned_cols(wi), f_bias=jnp.pad(f_bias[l], (0, CHUNK - N_HEADS))[None],
            sgu_w=sgu_w[l], sgu_b=sgu_b[l], sgu_ln_g=sgu_ln_g[l][None], sgu_ln_b=sgu_ln_b[l][None],
            short_conv_w=short_full[l], conf_dw_w=conf_full[l], conf_dw_b=conf_dw_b[l][None],
            conf_ln_g=conf_ln_g[l][None], conf_ln_b=conf_ln_b[l][None], w_branch=wb, w_out=wo))

    loss_local, grad_x, grads, d_final_g = _local_step(x[0], loss_target[0], layers, final_g[None])
    loss = lax.psum(loss_local, ("x", "y", "c"))

    def grad_block(l, j):
        gi = _from_aligned_cols(grads[l]["w_in"])[:, j * SHARD_COLS:(j + 1) * SHARD_COLS]
        return _shard_block(gi, grads[l]["w_branch"][:, :, j * BR:(j + 1) * BR], grads[l]["w_out"][j * BR:(j + 1) * BR])

    g2 = jnp.stack([jnp.stack([grad_block(l, j) for j in range(N_CHIPS)]) for l in range(depth)])
    keep = lax.dynamic_index_in_dim(g2, pc, 0, keepdims=False).reshape(N_CHIPS * BLOCK_ROWS, D_MODEL)
    send = lax.dynamic_index_in_dim(g2, 1 - pc, 0, keepdims=False).reshape(N_CHIPS * BLOCK_ROWS, D_MODEL)
    h = _add_pair(keep, _pair_exchange(send)).reshape(N_CHIPS, BLOCK_ROWS, D_MODEL)
    summed = _sum_chips(lax.dynamic_index_in_dim(h, chip, 0, keepdims=False), _chip_exchange(h))
    big = _pair_gather(summed)
    g = dict(
        w_in=big[:, :WIN_ROWS].reshape(depth, D_MODEL, SHARD_PAD)[:, :, :SHARD_COLS],
        w_branch=big[:, WIN_ROWS:WIN_ROWS + WB_ROWS].reshape(depth, N_HEADS, BR, BR),
        w_out=big[:, WIN_ROWS + WB_ROWS:])

    small_local = [jnp.stack([grads[l][n] for l in range(depth)]) for n in SMALL[:-1]] + [d_final_g]
    small_shapes = [a.shape for a in small_local]
    for n, a in zip(SMALL, _unpack(_allreduce8(_pack(small_local)), small_shapes)):
        g[n] = a
    for n in ("short_conv_w", "conf_dw_w"):
        g[n] = lax.dynamic_slice_in_dim(g[n], conv_ch * chip, conv_ch, axis=2)

    delta, new_m, new_v = {}, {}, {}
    for n in ("w_in", "w_branch", "w_out"):
        shp = w[n].shape
        two_d = lambda a: a.reshape(-1, shp[-1])
        d_, m_, v_ = _adamw(two_d(w[n]), two_d(m[n]), two_d(v[n]), two_d(g[n]))
        delta[n], new_m[n], new_v[n] = d_.reshape(shp), m_.reshape(shp), v_.reshape(shp)
    shapes = [w[n].shape for n in SMALL]
    d_, m_, v_ = _adamw(*[_pack([t[n] for n in SMALL]) for t in (w, m, v, g)])
    for n, a, b, c_ in zip(SMALL, _unpack(d_, shapes), _unpack(m_, shapes), _unpack(v_, shapes)):
        delta[n], new_m[n], new_v[n] = a, b, c_

    return (loss, grad_x[None], *[g[n] for n in WEIGHTS], *[delta[n] for n in WEIGHTS],
            *[new_m[n] for n in WEIGHTS], *[new_v[n] for n in WEIGHTS])
```

```python
import functools
import math

import jax
import jax.numpy as jnp
from jax import lax
from jax.experimental import pallas as pl
from jax.experimental.pallas import tpu as pltpu

F32 = jnp.float32
BF16 = jnp.bfloat16

D_MODEL = 1024
BR = 256
N_HEADS = 4
HEAD_DIM = 64
CHUNK = 128
K_SHORT = 3
K_CONF = 31
EPS = 1e-6
IN_COLS = 7684
SHARD_COLS = IN_COLS // 4
SHARD_PAD = 2048
N_CHIPS = 4

QKV0, CIN0, AUV0, DGLU0 = 0, 768, 1536, 2048
AG0, BG0, CG0, DG0 = 2560, 2816, 3072, 3328
BF0 = 3584
MRG0 = 4096
PCOLS = 8192

V7X_VMEM_BYTES = 64 * 1024 * 1024
VMEM_LIMIT = V7X_VMEM_BYTES * 7 // 8

ADAM_LR, ADAM_B1, ADAM_B2, ADAM_EPS, ADAM_WD, ADAM_STEP = 0.001, 0.9, 0.999, 1e-08, 0.01, 10

MESH = pl.DeviceIdType.MESH
ANY = pl.BlockSpec(memory_space=pl.ANY)
VMEM = pl.BlockSpec(memory_space=pltpu.VMEM)

GELU_C0 = math.sqrt(2.0 / math.pi)
GELU_C1 = 0.044715


def _params(*sem):
    return pltpu.CompilerParams(dimension_semantics=sem, vmem_limit_bytes=VMEM_LIMIT)


def _sigmoid(x):
    return 1.0 / (1.0 + jnp.exp(-x))


def _silu_and_grad(x):
    s = _sigmoid(x)
    return x * s, s * (1.0 + x * (1.0 - s))


def _gelu_and_grad(z):
    z2 = z * z
    t = jnp.tanh(GELU_C0 * (z + GELU_C1 * z2 * z))
    half = 0.5 * (1.0 + t)
    return z * half, half + 0.5 * z * (1.0 - t * t) * (GELU_C0 * (1.0 + 3.0 * GELU_C1 * z2))


def _ln_fwd(v):
    mu = jnp.mean(v, axis=-1, keepdims=True)
    xc = v - mu
    rs = lax.rsqrt(jnp.mean(xc * xc, axis=-1, keepdims=True) + EPS)
    return xc * rs, rs


def _ln_bwd(d_xh, xh, rs):
    return rs * (d_xh - jnp.mean(d_xh, axis=-1, keepdims=True) - xh * jnp.mean(d_xh * xh, axis=-1, keepdims=True))


def _part8(a):
    return a.reshape(a.shape[0] // 8, 8, a.shape[1]).sum(axis=0)


def _dot(a, b):
    return jnp.dot(a, b, preferred_element_type=F32)


def _dot_nt(a, b):
    return lax.dot_general(a, b, (((1,), (1,)), ((), ())), preferred_element_type=F32)


def _dot_tn(a, b):
    return lax.dot_general(a, b, (((0,), (0,)), ((), ())), preferred_element_type=F32)


def _head_masks(dtype):
    lane = lax.broadcasted_iota(jnp.int32, (1, BR), 1) // HEAD_DIM
    return [(lane == h).astype(dtype) for h in range(N_HEADS)]


def _window(ref, col0, width):
    return ref.at[:, pl.ds(col0, width)]


def _copy_all(pairs, sems):
    cps = [pltpu.make_async_copy(s, d, sems.at[i]) for i, (s, d) in enumerate(pairs)]
    for cp in cps:
        cp.start()
    for cp in cps:
        cp.wait()


def _inproj_fwd(x, g, w):
    s = x.shape[0]
    tm, tn = min(512, s), 1024

    def body(x_ref, g_ref, w_ref, proj_ref, h_ref):
        @pl.when(pl.program_id(1) == 0)
        def _():
            xv = x_ref[...]
            r = lax.rsqrt(jnp.mean(xv * xv, axis=-1, keepdims=True) + EPS)
            h_ref[...] = ((xv * r) * g_ref[...]).astype(BF16)
        proj_ref[...] = _dot(h_ref[...], w_ref[...])

    return pl.pallas_call(
        body, name="inproj_fwd", grid=(s // tm, PCOLS // tn),
        in_specs=[pl.BlockSpec((tm, D_MODEL), lambda i, j: (i, 0)), pl.BlockSpec((1, D_MODEL), lambda i, j: (0, 0)),
                  pl.BlockSpec((D_MODEL, tn), lambda i, j: (0, j))],
        out_specs=[pl.BlockSpec((tm, tn), lambda i, j: (i, j)), pl.BlockSpec((tm, D_MODEL), lambda i, j: (i, 0))],
        out_shape=[jax.ShapeDtypeStruct((s, PCOLS), F32), jax.ShapeDtypeStruct((s, D_MODEL), BF16)],
        compiler_params=_params("arbitrary", "arbitrary"))(x, g, w)


def _rms_bwd(dh, x, g):
    r = lax.rsqrt(jnp.mean(x * x, axis=-1, keepdims=True) + EPS)
    xn = x * r
    gy = dh * g
    dx = r * (gy - xn * jnp.mean(xn * gy, axis=-1, keepdims=True))
    return dx, _part8(dh * xn)


def _inproj_bwd_x(dproj, w, x, dxn, g):
    s = x.shape[0]
    tm, tk = min(512, s), 1024
    nk = PCOLS // tk

    def body(dp_ref, w_ref, x_ref, dxn_ref, g_ref, dx_ref, dg_ref, acc_ref):
        i, k = pl.program_id(0), pl.program_id(1)

        @pl.when(k == 0)
        def _():
            acc_ref[...] = jnp.zeros_like(acc_ref)

        @pl.when((i == 0) & (k == 0))
        def _():
            dg_ref[...] = jnp.zeros_like(dg_ref)

        acc_ref[...] += _dot_nt(dp_ref[...], w_ref[...])

        @pl.when(k == nk - 1)
        def _():
            dx, dg8 = _rms_bwd(acc_ref[...], x_ref[...], g_ref[...])
            dx_ref[...] = dxn_ref[...] + dx
            dg_ref[...] += dg8

    dx, dg8 = pl.pallas_call(
        body, name="inproj_bwd_x", grid=(s // tm, nk),
        in_specs=[pl.BlockSpec((tm, tk), lambda i, k: (i, k)), pl.BlockSpec((D_MODEL, tk), lambda i, k: (0, k)),
                  pl.BlockSpec((tm, D_MODEL), lambda i, k: (i, 0)), pl.BlockSpec((tm, D_MODEL), lambda i, k: (i, 0)),
                  pl.BlockSpec((1, D_MODEL), lambda i, k: (0, 0))],
        out_specs=[pl.BlockSpec((tm, D_MODEL), lambda i, k: (i, 0)), pl.BlockSpec((8, D_MODEL), lambda i, k: (0, 0))],
        out_shape=[jax.ShapeDtypeStruct((s, D_MODEL), F32), jax.ShapeDtypeStruct((8, D_MODEL), F32)],
        scratch_shapes=[pltpu.VMEM((tm, D_MODEL), F32)],
        compiler_params=_params("arbitrary", "arbitrary"))(dproj, w, x, dxn, g)
    return dx, dg8


def _inproj_bwd_w(h, dproj):
    s = h.shape[0]
    tn, tk = 1024, min(512, s)
    nk = s // tk

    def body(h_ref, dp_ref, dw_ref, acc_ref):
        k = pl.program_id(1)

        @pl.when(k == 0)
        def _():
            acc_ref[...] = jnp.zeros_like(acc_ref)

        acc_ref[...] += _dot_tn(h_ref[...], dp_ref[...])

        @pl.when(k == nk - 1)
        def _():
            dw_ref[...] = acc_ref[...].astype(BF16)

    return pl.pallas_call(
        body, name="inproj_bwd_w", grid=(PCOLS // tn, nk),
        in_specs=[pl.BlockSpec((tk, D_MODEL), lambda j, k: (k, 0)), pl.BlockSpec((tk, tn), lambda j, k: (k, j))],
        out_specs=pl.BlockSpec((D_MODEL, tn), lambda j, k: (0, j)),
        out_shape=jax.ShapeDtypeStruct((D_MODEL, PCOLS), BF16),
        scratch_shapes=[pltpu.VMEM((D_MODEL, tn), F32)],
        compiler_params=_params("arbitrary", "arbitrary"))(h, dproj)


def _mix_a_chunk(uvp, agp, wm_ref, bias, lg, lb):
    u, du = _gelu_and_grad(uvp[:, :BR])
    v, dv = _gelu_and_grad(uvp[:, BR:])
    xh, rs = _ln_fwd(v)
    vnb = (xh * lg + lb).astype(BF16)
    masks = _head_masks(BF16)
    mixed = bias
    for h in range(N_HEADS):
        mixed = mixed + _dot(wm_ref[h], vnb * masks[h])
    sg, dsg = _silu_and_grad(agp)
    return u, du, dv, xh, rs, vnb, masks, mixed, sg, dsg


def _store_masked_sgu(sw_ref, wm_ref):
    row = lax.broadcasted_iota(jnp.int32, (CHUNK, CHUNK), 0)
    col = lax.broadcasted_iota(jnp.int32, (CHUNK, CHUNK), 1)
    for h in range(N_HEADS):
        wm_ref[h] = jnp.where(row >= col, sw_ref[h], 0.0).astype(BF16)


def _mix_a_fwd(proj, sgu_w, bias, lg, lb):
    s = proj.shape[0]

    def body(proj_ref, sw_ref, bias_ref, lg_ref, lb_ref, y_ref, uv_buf, ag_buf, wm_ref, sems):
        _copy_all([(_window(proj_ref, AUV0, 2 * BR), uv_buf), (_window(proj_ref, AG0, BR), ag_buf)], sems)
        _store_masked_sgu(sw_ref, wm_ref)

        def chunk(i, c):
            rows = pl.ds(pl.multiple_of(i * CHUNK, CHUNK), CHUNK)
            u, _, _, _, _, _, _, mixed, sg, _ = _mix_a_chunk(uv_buf[rows, :], ag_buf[rows, :], wm_ref, bias_ref[...],
                                                            lg_ref[...], lb_ref[...])
            y_ref[rows, :] = (u * mixed * sg).astype(BF16)
            return c

        lax.fori_loop(0, s // CHUNK, chunk, 0)

    return pl.pallas_call(
        body, name="mix_a_fwd", in_specs=[ANY, VMEM, VMEM, VMEM, VMEM], out_specs=VMEM,
        out_shape=jax.ShapeDtypeStruct((s, BR), BF16),
        scratch_shapes=[pltpu.VMEM((s, 2 * BR), F32), pltpu.VMEM((s, BR), F32), pltpu.VMEM((N_HEADS, CHUNK, CHUNK), BF16),
                        pltpu.SemaphoreType.DMA((2,))],
        compiler_params=_params())(proj, sgu_w, bias, lg, lb)


def _mix_a_bwd(proj, dproj, dy, sgu_w, bias, lg, lb):
    s = proj.shape[0]

    def body(proj_ref, dproj_in, dy_ref, sw_ref, bias_ref, lg_ref, lb_ref,
             dproj_ref, dsw_ref, dbias_ref, dlg_ref, dlb_ref,
             uv_buf, ag_buf, duv_buf, dag_buf, wm_ref, acc_lg, acc_lb, sems):
        del dproj_in
        _copy_all([(_window(proj_ref, AUV0, 2 * BR), uv_buf), (_window(proj_ref, AG0, BR), ag_buf)], sems)
        _store_masked_sgu(sw_ref, wm_ref)
        dsw_ref[...] = jnp.zeros_like(dsw_ref)
        dbias_ref[...] = jnp.zeros_like(dbias_ref)
        acc_lg[...] = jnp.zeros_like(acc_lg)
        acc_lb[...] = jnp.zeros_like(acc_lb)

        def chunk(i, c):
            rows = pl.ds(pl.multiple_of(i * CHUNK, CHUNK), CHUNK)
            lg_v = lg_ref[...]
            u, du, dv, xh, rs, vnb, masks, mixed, sg, dsg = _mix_a_chunk(
                uv_buf[rows, :], ag_buf[rows, :], wm_ref, bias_ref[...], lg_v, lb_ref[...])
            dyv = dy_ref[rows, :]
            t1 = dyv * sg
            d_u = t1 * mixed
            d_mixed = t1 * u
            d_ag = dyv * u * mixed * dsg
            dbias_ref[...] += d_mixed
            dmb = d_mixed.astype(BF16)
            d_vn = jnp.zeros((CHUNK, BR), F32)
            for h in range(N_HEADS):
                dm_h = dmb * masks[h]
                dsw_ref[h] += _dot_nt(dm_h, vnb)
                d_vn = d_vn + _dot_tn(wm_ref[h], dm_h)
            acc_lg[...] += _part8(d_vn * xh)
            acc_lb[...] += _part8(d_vn)
            d_v = _ln_bwd(d_vn * lg_v, xh, rs)
            duv_buf[rows, :] = jnp.concatenate([d_u * du, d_v * dv], axis=1).astype(BF16)
            dag_buf[rows, :] = d_ag.astype(BF16)
            return c

        lax.fori_loop(0, s // CHUNK, chunk, 0)
        row = lax.broadcasted_iota(jnp.int32, (CHUNK, CHUNK), 0)
        col = lax.broadcasted_iota(jnp.int32, (CHUNK, CHUNK), 1)
        for h in range(N_HEADS):
            dsw_ref[h] = jnp.where(row >= col, dsw_ref[h], 0.0)
        dlg_ref[...] = jnp.sum(acc_lg[...], axis=0, keepdims=True)
        dlb_ref[...] = jnp.sum(acc_lb[...], axis=0, keepdims=True)
        _copy_all([(duv_buf, _window(dproj_ref, AUV0, 2 * BR)), (dag_buf, _window(dproj_ref, AG0, BR))], sems)

    return pl.pallas_call(
        body, name="mix_a_bwd", in_specs=[ANY, ANY, VMEM, VMEM, VMEM, VMEM, VMEM],
        out_specs=[ANY, VMEM, VMEM, VMEM, VMEM],
        out_shape=[jax.ShapeDtypeStruct(dproj.shape, dproj.dtype), jax.ShapeDtypeStruct((N_HEADS, CHUNK, CHUNK), F32),
                   jax.ShapeDtypeStruct((CHUNK, BR), F32), jax.ShapeDtypeStruct((1, BR), F32), jax.ShapeDtypeStruct((1, BR), F32)],
        scratch_shapes=[pltpu.VMEM((s, 2 * BR), F32), pltpu.VMEM((s, BR), F32), pltpu.VMEM((s, 2 * BR), BF16),
                        pltpu.VMEM((s, BR), BF16), pltpu.VMEM((N_HEADS, CHUNK, CHUNK), BF16),
                        pltpu.VMEM((8, BR), F32), pltpu.VMEM((8, BR), F32), pltpu.SemaphoreType.DMA((2,))],
        input_output_aliases={1: 0}, compiler_params=_params())(proj, dproj, dy, sgu_w, bias, lg, lb)


def _tri_ones(n, upper):
    row = lax.broadcasted_iota(jnp.int32, (n, n), 0)
    col = lax.broadcasted_iota(jnp.int32, (n, n), 1)
    return ((row <= col) if upper else (row >= col)).astype(BF16)


def _split3(c):
    hi = c.astype(BF16)
    r1 = c - hi.astype(F32)
    mid = r1.astype(BF16)
    lo = (r1 - mid.astype(F32)).astype(BF16)
    return [hi, mid, lo]


def _tri_sum(tri, a):
    parts = _split3(a)
    return _dot(tri, parts[0]) + _dot(tri, parts[1]) + _dot(tri, parts[2])


EXT = 2 * HEAD_DIM
LANE_CQ = HEAD_DIM
LANE_CK = HEAD_DIM + 3


def _to_head(h):
    r = lax.broadcasted_iota(jnp.int32, (BR, EXT), 0)
    c = lax.broadcasted_iota(jnp.int32, (BR, EXT), 1)
    return ((r == c + h * HEAD_DIM) & (c < HEAD_DIM)).astype(BF16)


def _from_head(h):
    r = lax.broadcasted_iota(jnp.int32, (EXT, BR), 0)
    c = lax.broadcasted_iota(jnp.int32, (EXT, BR), 1)
    return ((c == r + h * HEAD_DIM) & (r < HEAD_DIM)).astype(BF16)


def _attn_prep_fwd(proj, f_bias):
    s = proj.shape[0]
    scale = 1.0 / math.sqrt(HEAD_DIM)

    def body(proj_ref, fb_ref, qe_ref, ke_ref, ve_ref, qkv_buf, z_buf, sems):
        _copy_all([(_window(proj_ref, QKV0, 3 * BR), qkv_buf), (_window(proj_ref, BF0, CHUNK), z_buf)], sems)
        tri = _tri_ones(CHUNK, upper=False)
        lane = lax.broadcasted_iota(jnp.int32, (CHUNK, EXT), 1)
        ones_q = ((lane >= LANE_CK) & (lane < LANE_CK + 3)).astype(F32)
        ones_k = ((lane >= LANE_CQ) & (lane < LANE_CQ + 3)).astype(F32)

        def chunk(i, carry):
            rows = pl.ds(pl.multiple_of(i * CHUNK, CHUNK), CHUNK)
            cum = _tri_sum(tri, jax.nn.log_sigmoid(z_buf[rows, :] + fb_ref[...])) + carry
            qb = (qkv_buf[rows, 0:BR] * scale).astype(BF16)
            kb = qkv_buf[rows, BR:2 * BR].astype(BF16)
            vb = qkv_buf[rows, 2 * BR:].astype(BF16)
            parts = [p.astype(F32) for p in _split3(cum)]
            for h in range(N_HEADS):
                sel = _to_head(h)
                dec_q, dec_k = ones_q, ones_k
                for t, part in enumerate(parts):
                    pf = part[:, h:h + 1]
                    dec_q = dec_q + jnp.where(lane == LANE_CQ + t, pf, 0.0)
                    dec_k = dec_k - jnp.where(lane == LANE_CK + t, pf, 0.0)
                qe_ref[h, rows, :] = (_dot(qb, sel) + dec_q).astype(BF16)
                ke_ref[h, rows, :] = (_dot(kb, sel) + dec_k).astype(BF16)
                ve_ref[h, rows, :] = _dot(vb, sel).astype(BF16)
            return cum[CHUNK - 1:CHUNK, :]

        lax.fori_loop(0, s // CHUNK, chunk, jnp.zeros((1, CHUNK), F32))

    shape = jax.ShapeDtypeStruct((N_HEADS, s, EXT), BF16)
    return pl.pallas_call(
        body, name="attn_prep_fwd", in_specs=[ANY, VMEM], out_specs=[VMEM, VMEM, VMEM], out_shape=[shape, shape, shape],
        scratch_shapes=[pltpu.VMEM((s, 3 * BR), F32), pltpu.VMEM((s, CHUNK), F32), pltpu.SemaphoreType.DMA((2,))],
        compiler_params=_params())(proj, f_bias)


def _attn_prep_bwd(proj, dproj, dqe, dke, dve, f_bias):
    s = proj.shape[0]
    pad = MRG0 - BF0
    scale = 1.0 / math.sqrt(HEAD_DIM)

    def body(proj_ref, dproj_in, dq_ref, dk_ref, dv_ref, fb_ref, dproj_ref, dfb_ref, z_buf, dqkv_buf, dz_buf, sems):
        del dproj_in
        _copy_all([(_window(proj_ref, BF0, CHUNK), z_buf)], sems)
        tri = _tri_ones(CHUNK, upper=True)
        lane = lax.broadcasted_iota(jnp.int32, (CHUNK, CHUNK), 1)
        dz_buf[...] = jnp.zeros_like(dz_buf)
        n = s // CHUNK

        def chunk(t, carry):
            suffix, acc = carry
            i = n - 1 - t
            rows = pl.ds(pl.multiple_of(i * CHUNK, CHUNK), CHUNK)
            dq = jnp.zeros((CHUNK, BR), F32)
            dk = jnp.zeros((CHUNK, BR), F32)
            dv = jnp.zeros((CHUNK, BR), F32)
            dcum = jnp.zeros((CHUNK, CHUNK), F32)
            for h in range(N_HEADS):
                back = _from_head(h)
                dqh = dq_ref[h, rows, :]
                dkh = dk_ref[h, rows, :]
                dq = dq + _dot((dqh * scale).astype(BF16), back)
                dk = dk + _dot(dkh.astype(BF16), back)
                dv = dv + _dot(dv_ref[h, rows, :].astype(BF16), back)
                dcum = dcum + jnp.where(lane == h, dqh[:, LANE_CQ:LANE_CQ + 1] - dkh[:, LANE_CK:LANE_CK + 1], 0.0)
            dqkv_buf[rows, 0:BR] = dq.astype(BF16)
            dqkv_buf[rows, BR:2 * BR] = dk.astype(BF16)
            dqkv_buf[rows, 2 * BR:] = dv.astype(BF16)
            dlf = _tri_sum(tri, dcum) + suffix
            dz = dlf * _sigmoid(-(z_buf[rows, :] + fb_ref[...]))
            dz_buf[rows, pl.ds(0, CHUNK)] = dz.astype(BF16)
            return dlf[0:1, :], acc + _part8(dz)

        _, acc = lax.fori_loop(0, n, chunk, (jnp.zeros((1, CHUNK), F32), jnp.zeros((8, CHUNK), F32)))
        dfb_ref[...] = jnp.sum(acc, axis=0, keepdims=True)
        _copy_all([(dqkv_buf, _window(dproj_ref, QKV0, 3 * BR)), (dz_buf, _window(dproj_ref, BF0, pad))], sems)

    return pl.pallas_call(
        body, name="attn_prep_bwd", in_specs=[ANY, ANY, VMEM, VMEM, VMEM, VMEM], out_specs=[ANY, VMEM],
        out_shape=[jax.ShapeDtypeStruct(dproj.shape, dproj.dtype), jax.ShapeDtypeStruct((1, CHUNK), F32)],
        scratch_shapes=[pltpu.VMEM((s, CHUNK), F32), pltpu.VMEM((s, 3 * BR), BF16), pltpu.VMEM((s, pad), BF16),
                        pltpu.SemaphoreType.DMA((2,))],
        input_output_aliases={1: 0}, compiler_params=_params())(proj, dproj, dqe, dke, dve, f_bias)


ATT_TQ = 256
ATT_TK = 256
NEG_BIG = -1e30


def _causal(q0, k0, tq, tk):
    qpos = q0 + lax.broadcasted_iota(jnp.int32, (tq, tk), 0)
    kpos = k0 + lax.broadcasted_iota(jnp.int32, (tq, tk), 1)
    return qpos >= kpos


def _attn_fwd(qe, ke, ve):
    nh, s, w = qe.shape
    tq = tk = min(ATT_TQ, s)

    def body(q_ref, k_ref, v_ref, o_ref):
        i = pl.program_id(1)
        q = q_ref[0]

        def step(j, carry, masked):
            m, l, acc = carry
            ks = pl.ds(pl.multiple_of(j * tk, tk), tk)
            sc = _dot_nt(q, k_ref[0, ks, :])
            if masked:
                sc = jnp.where(_causal(i * tq, j * tk, tq, tk), sc, NEG_BIG)
            m_new = jnp.maximum(m, jnp.max(sc, axis=-1, keepdims=True))
            alpha = jnp.exp(m - m_new)
            p = jnp.exp(sc - m_new)
            l = alpha * l + jnp.sum(p, axis=-1, keepdims=True)
            acc = alpha * acc + _dot(p.astype(BF16), v_ref[0, ks, :])
            return m_new, l, acc

        init = (jnp.full((tq, 1), NEG_BIG, F32), jnp.zeros((tq, 1), F32), jnp.zeros((tq, w), F32))
        carry = lax.fori_loop(0, i, functools.partial(step, masked=False), init)
        m, l, acc = step(i, carry, True)
        lane = lax.broadcasted_iota(jnp.int32, (tq, w), 1)
        o_ref[0] = jnp.where(lane < HEAD_DIM, acc / l, m + jnp.log(l))

    return pl.pallas_call(
        body, name="attn_fwd", grid=(nh, s // tq),
        in_specs=[pl.BlockSpec((1, tq, w), lambda h, i: (h, i, 0)), pl.BlockSpec((1, s, w), lambda h, i: (h, 0, 0)),
                  pl.BlockSpec((1, s, w), lambda h, i: (h, 0, 0))],
        out_specs=pl.BlockSpec((1, tq, w), lambda h, i: (h, i, 0)),
        out_shape=jax.ShapeDtypeStruct((nh, s, w), F32),
        compiler_params=_params("arbitrary", "arbitrary"))(qe, ke, ve)


def _attn_bwd(qe, ke, ve, doe, stats):
    nh, s, w = qe.shape
    tk = min(ATT_TK, s)
    tq = min(128, s)
    nq = s // tq

    def body(q_ref, k_ref, v_ref, do_ref, st_ref, dq_ref, dk_ref, dv_ref):
        j = pl.program_id(1)

        @pl.when(j == 0)
        def _():
            dq_ref[...] = jnp.zeros_like(dq_ref)

        k = k_ref[0]
        v = v_ref[0]

        def step(i, carry, masked):
            dk, dv = carry
            qs = pl.ds(pl.multiple_of(i * tq, tq), tq)
            q = q_ref[0, qs, :]
            do = do_ref[0, qs, :]
            st = st_ref[0, qs, :]
            sc = _dot_nt(q, k)
            p = jnp.exp(sc - st[:, 0:1])
            if masked:
                p = jnp.where(_causal(i * tq, j * tk, tq, tk), p, 0.0)
            dp = _dot_nt(do, v)
            ds = (p * (dp - st[:, 1:2])).astype(BF16)
            dv = dv + _dot_tn(p.astype(BF16), do)
            dk = dk + _dot_tn(ds, q)
            dq_ref[0, qs, :] += _dot(ds, k)
            return dk, dv

        first = (j * tk) // tq
        n_diag = tk // tq
        carry = (jnp.zeros((tk, w), F32), jnp.zeros((tk, w), F32))
        carry = lax.fori_loop(first, first + n_diag, functools.partial(step, masked=True), carry)
        dk, dv = lax.fori_loop(first + n_diag, nq, functools.partial(step, masked=False), carry)
        dk_ref[0] = dk
        dv_ref[0] = dv

    whole = pl.BlockSpec((1, s, w), lambda h, j: (h, 0, 0))
    blk = pl.BlockSpec((1, tk, w), lambda h, j: (h, j, 0))
    return pl.pallas_call(
        body, name="attn_bwd", grid=(nh, s // tk),
        in_specs=[whole, blk, blk, whole, whole], out_specs=[whole, blk, blk],
        out_shape=[jax.ShapeDtypeStruct((nh, s, w), F32)] * 3,
        compiler_params=_params("arbitrary", "arbitrary"))(qe, ke, ve, doe, stats)


def _bgate_fwd(proj, oe):
    s = proj.shape[0]

    def body(proj_ref, oe_ref, y_ref, o_ref, g_buf, sems):
        _copy_all([(_window(proj_ref, BG0, BR), g_buf)], sems)

        def chunk(i, c):
            rows = pl.ds(pl.multiple_of(i * CHUNK, CHUNK), CHUNK)
            o = jnp.zeros((CHUNK, BR), F32)
            for h in range(N_HEADS):
                back = _from_head(h)
                for part in _split3(oe_ref[h, rows, :]):
                    o = o + _dot(part, back)
            sg, _ = _silu_and_grad(g_buf[rows, :])
            o_ref[rows, :] = o
            y_ref[rows, :] = (o * sg).astype(BF16)
            return c

        lax.fori_loop(0, s // CHUNK, chunk, 0)

    return pl.pallas_call(
        body, name="bgate_fwd", in_specs=[ANY, VMEM], out_specs=[VMEM, VMEM],
        out_shape=[jax.ShapeDtypeStruct((s, BR), BF16), jax.ShapeDtypeStruct((s, BR), F32)],
        scratch_shapes=[pltpu.VMEM((s, BR), F32), pltpu.SemaphoreType.DMA((1,))], compiler_params=_params())(proj, oe)


def _bgate_bwd(proj, dproj, o, oe, dy):
    s = proj.shape[0]

    def body(proj_ref, dproj_in, o_ref, oe_ref, dy_ref, dproj_ref, do_ref, st_ref, g_buf, dg_buf, sems):
        del dproj_in
        _copy_all([(_window(proj_ref, BG0, BR), g_buf)], sems)
        lane = lax.broadcasted_iota(jnp.int32, (BR, CHUNK), 0) // HEAD_DIM
        col = lax.broadcasted_iota(jnp.int32, (BR, CHUNK), 1)
        sel = (lane == col).astype(BF16)
        out_lane = lax.broadcasted_iota(jnp.int32, (CHUNK, EXT), 1)

        def chunk(i, c):
            rows = pl.ds(pl.multiple_of(i * CHUNK, CHUNK), CHUNK)
            sg, dsg = _silu_and_grad(g_buf[rows, :])
            dyv = dy_ref[rows, :]
            ov = o_ref[rows, :]
            do = dyv * sg
            dg_buf[rows, :] = (dyv * ov * dsg).astype(BF16)
            prod = _split3(do * ov)
            dd = _dot(prod[0], sel) + _dot(prod[1], sel) + _dot(prod[2], sel)
            dob = do.astype(BF16)
            for h in range(N_HEADS):
                do_ref[h, rows, :] = _dot(dob, _to_head(h)).astype(BF16)
                lse = oe_ref[h, rows, :][:, HEAD_DIM:HEAD_DIM + 1]
                st_ref[h, rows, :] = jnp.where(out_lane == 0, lse, jnp.where(out_lane == 1, dd[:, h:h + 1], 0.0))
            return c

        lax.fori_loop(0, s // CHUNK, chunk, 0)
        _copy_all([(dg_buf, _window(dproj_ref, BG0, BR))], sems)

    return pl.pallas_call(
        body, name="bgate_bwd", in_specs=[ANY, ANY, VMEM, VMEM, VMEM], out_specs=[ANY, VMEM, VMEM],
        out_shape=[jax.ShapeDtypeStruct(dproj.shape, dproj.dtype), jax.ShapeDtypeStruct((N_HEADS, s, EXT), BF16),
                   jax.ShapeDtypeStruct((N_HEADS, s, EXT), F32)],
        scratch_shapes=[pltpu.VMEM((s, BR), F32), pltpu.VMEM((s, BR), BF16), pltpu.SemaphoreType.DMA((1,))],
        input_output_aliases={1: 0}, compiler_params=_params())(proj, dproj, o, oe, dy)


C_PAD = 8


def _mix_c_fwd(proj, w):
    s = proj.shape[0]

    def body(proj_ref, w_ref, y_ref, cin_buf, g_buf, z_buf, sems):
        _copy_all([(_window(proj_ref, CIN0, 3 * BR), cin_buf), (_window(proj_ref, CG0, BR), g_buf)], sems)
        z_buf[pl.ds(0, C_PAD), :] = jnp.zeros((C_PAD, BR), F32)

        def fill(i, c):
            rows = pl.ds(pl.multiple_of(i * CHUNK, CHUNK), CHUNK)
            z_buf[pl.ds(pl.multiple_of(i * CHUNK + C_PAD, 8), CHUNK), :] = cin_buf[rows, BR:2 * BR] * cin_buf[rows, 2 * BR:]
            return c

        lax.fori_loop(0, s // CHUNK, fill, 0)

        def chunk(i, c):
            r0 = pl.multiple_of(i * CHUNK, CHUNK)
            rows = pl.ds(r0, CHUNK)
            ze = z_buf[pl.ds(r0, CHUNK + C_PAD), :]
            conv = jnp.zeros((CHUNK, BR), F32)
            for k in range(K_SHORT):
                off = C_PAD - (K_SHORT - 1) + k
                conv = conv + w_ref[k:k + 1, :] * ze[off:off + CHUNK]
            sg, _ = _silu_and_grad(g_buf[rows, :])
            y_ref[rows, :] = (cin_buf[rows, 0:BR] * conv * sg).astype(BF16)
            return c

        lax.fori_loop(0, s // CHUNK, chunk, 0)

    return pl.pallas_call(
        body, name="mix_c_fwd", in_specs=[ANY, VMEM], out_specs=VMEM, out_shape=jax.ShapeDtypeStruct((s, BR), BF16),
        scratch_shapes=[pltpu.VMEM((s, 3 * BR), F32), pltpu.VMEM((s, BR), F32), pltpu.VMEM((s + C_PAD, BR), F32),
                        pltpu.SemaphoreType.DMA((2,))],
        compiler_params=_params())(proj, w)


def _mix_c_bwd(proj, dproj, dy, w):
    s = proj.shape[0]

    def body(proj_ref, dproj_in, dy_ref, w_ref, dproj_ref, dw_ref, cin_buf, g_buf, z_buf, dc_buf, dcin_buf, dg_buf, acc, sems):
        del dproj_in
        _copy_all([(_window(proj_ref, CIN0, 3 * BR), cin_buf), (_window(proj_ref, CG0, BR), g_buf)], sems)
        z_buf[pl.ds(0, C_PAD), :] = jnp.zeros((C_PAD, BR), F32)
        dc_buf[pl.ds(s, C_PAD), :] = jnp.zeros((C_PAD, BR), F32)
        acc[...] = jnp.zeros_like(acc)

        def fill(i, c):
            rows = pl.ds(pl.multiple_of(i * CHUNK, CHUNK), CHUNK)
            z_buf[pl.ds(pl.multiple_of(i * CHUNK + C_PAD, 8), CHUNK), :] = cin_buf[rows, BR:2 * BR] * cin_buf[rows, 2 * BR:]
            return c

        lax.fori_loop(0, s // CHUNK, fill, 0)

        def chunk(i, c):
            r0 = pl.multiple_of(i * CHUNK, CHUNK)
            rows = pl.ds(r0, CHUNK)
            ze = z_buf[pl.ds(r0, CHUNK + C_PAD), :]
            taps = [ze[C_PAD - (K_SHORT - 1) + k:C_PAD - (K_SHORT - 1) + k + CHUNK] for k in range(K_SHORT)]
            conv = jnp.zeros((CHUNK, BR), F32)
            for k in range(K_SHORT):
                conv = conv + w_ref[k:k + 1, :] * taps[k]
            sg, dsg = _silu_and_grad(g_buf[rows, :])
            bg = cin_buf[rows, 0:BR]
            dyv = dy_ref[rows, :]
            dconv = dyv * bg * sg
            dc_buf[rows, :] = dconv
            dcin_buf[rows, 0:BR] = (dyv * conv * sg).astype(BF16)
            dg_buf[rows, :] = (dyv * bg * conv * dsg).astype(BF16)
            for k in range(K_SHORT):
                acc[k] += _part8(dconv * taps[k])
            return c

        lax.fori_loop(0, s // CHUNK, chunk, 0)

        def chunk2(i, c):
            r0 = pl.multiple_of(i * CHUNK, CHUNK)
            rows = pl.ds(r0, CHUNK)
            de = dc_buf[pl.ds(r0, CHUNK + C_PAD), :]
            dz = jnp.zeros((CHUNK, BR), F32)
            for k in range(K_SHORT):
                off = K_SHORT - 1 - k
                dz = dz + w_ref[k:k + 1, :] * de[off:off + CHUNK]
            dcin_buf[rows, BR:2 * BR] = (dz * cin_buf[rows, 2 * BR:]).astype(BF16)
            dcin_buf[rows, 2 * BR:] = (dz * cin_buf[rows, BR:2 * BR]).astype(BF16)
            return c

        lax.fori_loop(0, s // CHUNK, chunk2, 0)
        dw_ref[...] = jnp.zeros_like(dw_ref)
        for k in range(K_SHORT):
            dw_ref[k:k + 1, :] = jnp.sum(acc[k], axis=0, keepdims=True)
        _copy_all([(dcin_buf, _window(dproj_ref, CIN0, 3 * BR)), (dg_buf, _window(dproj_ref, CG0, BR))], sems)

    return pl.pallas_call(
        body, name="mix_c_bwd", in_specs=[ANY, ANY, VMEM, VMEM], out_specs=[ANY, VMEM],
        out_shape=[jax.ShapeDtypeStruct(dproj.shape, dproj.dtype), jax.ShapeDtypeStruct((8, BR), F32)],
        scratch_shapes=[pltpu.VMEM((s, 3 * BR), F32), pltpu.VMEM((s, BR), F32), pltpu.VMEM((s + C_PAD, BR), F32),
                        pltpu.VMEM((s + C_PAD, BR), F32), pltpu.VMEM((s, 3 * BR), BF16), pltpu.VMEM((s, BR), BF16),
                        pltpu.VMEM((K_SHORT, 8, BR), F32), pltpu.SemaphoreType.DMA((2,))],
        input_output_aliases={1: 0}, compiler_params=_params())(proj, dproj, dy, w)


D_PAD = 32


def _mix_d_common(cin_ref, g_ref, w_ref, b_ref, lg_ref, lb_ref, hh_buf, r0):
    he = hh_buf[pl.ds(r0, CHUNK + D_PAD), :]
    taps = [he[D_PAD - (K_CONF - 1) + k:D_PAD - (K_CONF - 1) + k + CHUNK] for k in range(K_CONF)]
    conv = jnp.zeros((CHUNK, BR), F32) + b_ref[...]
    for k in range(K_CONF):
        conv = conv + w_ref[k:k + 1, :] * taps[k]
    xh, rs = _ln_fwd(conv)
    sw, dsw = _silu_and_grad(xh * lg_ref[...] + lb_ref[...])
    sg, dsg = _silu_and_grad(g_ref[pl.ds(r0, CHUNK), :])
    return taps, xh, rs, sw, dsw, sg, dsg


def _mix_d_fill(cin_buf, hh_buf, s):
    hh_buf[pl.ds(0, D_PAD), :] = jnp.zeros((D_PAD, BR), F32)

    def fill(i, c):
        rows = pl.ds(pl.multiple_of(i * CHUNK, CHUNK), CHUNK)
        hh_buf[pl.ds(pl.multiple_of(i * CHUNK + D_PAD, 8), CHUNK), :] = cin_buf[rows, 0:BR] * _sigmoid(cin_buf[rows, BR:])
        return c

    lax.fori_loop(0, s // CHUNK, fill, 0)


def _mix_d_fwd(proj, w, b, lg, lb):
    s = proj.shape[0]

    def body(proj_ref, w_ref, b_ref, lg_ref, lb_ref, y_ref, cin_buf, g_buf, hh_buf, sems):
        _copy_all([(_window(proj_ref, DGLU0, 2 * BR), cin_buf), (_window(proj_ref, DG0, BR), g_buf)], sems)
        _mix_d_fill(cin_buf, hh_buf, s)

        def chunk(i, c):
            r0 = pl.multiple_of(i * CHUNK, CHUNK)
            _, _, _, sw, _, sg, _ = _mix_d_common(cin_buf, g_buf, w_ref, b_ref, lg_ref, lb_ref, hh_buf, r0)
            y_ref[pl.ds(r0, CHUNK), :] = (sw * sg).astype(BF16)
            return c

        lax.fori_loop(0, s // CHUNK, chunk, 0)

    return pl.pallas_call(
        body, name="mix_d_fwd", in_specs=[ANY, VMEM, VMEM, VMEM, VMEM], out_specs=VMEM,
        out_shape=jax.ShapeDtypeStruct((s, BR), BF16),
        scratch_shapes=[pltpu.VMEM((s, 2 * BR), F32), pltpu.VMEM((s, BR), F32), pltpu.VMEM((s + D_PAD, BR), F32),
                        pltpu.SemaphoreType.DMA((2,))],
        compiler_params=_params())(proj, w, b, lg, lb)


def _mix_d_bwd(proj, dproj, dy, w, b, lg, lb):
    s = proj.shape[0]

    def body(proj_ref, dproj_in, dy_ref, w_ref, b_ref, lg_ref, lb_ref, dproj_ref, dw_ref, db_ref, dlg_ref, dlb_ref,
             cin_buf, g_buf, hh_buf, dc_buf, dcin_buf, dg_buf, acc_w, acc_s, sems):
        del dproj_in
        _copy_all([(_window(proj_ref, DGLU0, 2 * BR), cin_buf), (_window(proj_ref, DG0, BR), g_buf)], sems)
        _mix_d_fill(cin_buf, hh_buf, s)
        dc_buf[pl.ds(s, D_PAD), :] = jnp.zeros((D_PAD, BR), F32)
        acc_w[...] = jnp.zeros_like(acc_w)
        acc_s[...] = jnp.zeros_like(acc_s)

        def chunk(i, c):
            r0 = pl.multiple_of(i * CHUNK, CHUNK)
            rows = pl.ds(r0, CHUNK)
            taps, xh, rs, sw, dsw, sg, dsg = _mix_d_common(cin_buf, g_buf, w_ref, b_ref, lg_ref, lb_ref, hh_buf, r0)
            dyv = dy_ref[rows, :]
            dg_buf[rows, :] = (dyv * sw * dsg).astype(BF16)
            d_ln = dyv * sg * dsw
            acc_s[0] += _part8(d_ln * xh)
            acc_s[1] += _part8(d_ln)
            dc = _ln_bwd(d_ln * lg_ref[...], xh, rs)
            acc_s[2] += _part8(dc)
            dc_buf[rows, :] = dc
            for k in range(K_CONF):
                acc_w[k] += _part8(dc * taps[k])
            return c

        lax.fori_loop(0, s // CHUNK, chunk, 0)

        def chunk2(i, c):
            r0 = pl.multiple_of(i * CHUNK, CHUNK)
            rows = pl.ds(r0, CHUNK)
            de = dc_buf[pl.ds(r0, CHUNK + D_PAD), :]
            dh = jnp.zeros((CHUNK, BR), F32)
            for k in range(K_CONF):
                off = K_CONF - 1 - k
                dh = dh + w_ref[k:k + 1, :] * de[off:off + CHUNK]
            a = cin_buf[rows, 0:BR]
            sig = _sigmoid(cin_buf[rows, BR:])
            dcin_buf[rows, 0:BR] = (dh * sig).astype(BF16)
            dcin_buf[rows, BR:] = (dh * a * sig * (1.0 - sig)).astype(BF16)
            return c

        lax.fori_loop(0, s // CHUNK, chunk2, 0)
        dw_ref[...] = jnp.zeros_like(dw_ref)
        for k in range(K_CONF):
            dw_ref[k:k + 1, :] = jnp.sum(acc_w[k], axis=0, keepdims=True)
        dlg_ref[...] = jnp.sum(acc_s[0], axis=0, keepdims=True)
        dlb_ref[...] = jnp.sum(acc_s[1], axis=0, keepdims=True)
        db_ref[...] = jnp.sum(acc_s[2], axis=0, keepdims=True)
        _copy_all([(dcin_buf, _window(dproj_ref, DGLU0, 2 * BR)), (dg_buf, _window(dproj_ref, DG0, BR))], sems)

    vec = jax.ShapeDtypeStruct((1, BR), F32)
    return pl.pallas_call(
        body, name="mix_d_bwd", in_specs=[ANY, ANY, VMEM, VMEM, VMEM, VMEM, VMEM], out_specs=[ANY, VMEM, VMEM, VMEM, VMEM],
        out_shape=[jax.ShapeDtypeStruct(dproj.shape, dproj.dtype), jax.ShapeDtypeStruct((32, BR), F32), vec, vec, vec],
        scratch_shapes=[pltpu.VMEM((s, 2 * BR), F32), pltpu.VMEM((s, BR), F32), pltpu.VMEM((s + D_PAD, BR), F32),
                        pltpu.VMEM((s + D_PAD, BR), F32), pltpu.VMEM((s, 2 * BR), BF16), pltpu.VMEM((s, BR), BF16),
                        pltpu.VMEM((K_CONF, 8, BR), F32), pltpu.VMEM((3, 8, BR), F32), pltpu.SemaphoreType.DMA((2,))],
        input_output_aliases={1: 0}, compiler_params=_params())(proj, dproj, dy, w, b, lg, lb)


MERGE_TM = 256


def _merge_fwd(x, proj, ys, wb, wo):
    s = x.shape[0]
    tm = min(MERGE_TM, s)

    def body(x_ref, lg_ref, ya, yb, yc, yd, wb_ref, wo_ref, xn_ref, mg_ref):
        merged = jnp.zeros((tm, D_MODEL), F32)
        for n, y_ref in enumerate((ya, yb, yc, yd)):
            gate = _sigmoid(lg_ref[:, n * D_MODEL:(n + 1) * D_MODEL])
            merged = merged + gate * _dot(y_ref[...], wb_ref[n])
        mb = merged.astype(BF16)
        mg_ref[...] = mb
        xn_ref[...] = x_ref[...] + _dot(mb, wo_ref[...])

    row = lambda w: pl.BlockSpec((tm, w), lambda i: (i, 0))
    return pl.pallas_call(
        body, name="merge_fwd", grid=(s // tm,),
        in_specs=[row(D_MODEL), pl.BlockSpec((tm, MRG0), lambda i: (i, 1)), row(BR), row(BR), row(BR), row(BR),
                  pl.BlockSpec((N_HEADS, BR, D_MODEL), lambda i: (0, 0, 0)), pl.BlockSpec((D_MODEL, D_MODEL), lambda i: (0, 0))],
        out_specs=[row(D_MODEL), row(D_MODEL)],
        out_shape=[jax.ShapeDtypeStruct((s, D_MODEL), F32), jax.ShapeDtypeStruct((s, D_MODEL), BF16)],
        compiler_params=_params("arbitrary"))(x, proj, *ys, wb, wo)


def _merge_bwd(dxn, proj, ys, merged, wb, wo):
    s = dxn.shape[0]
    tm = min(MERGE_TM, s)

    def body(dx_ref, lg_ref, ya, yb, yc, yd, mg_ref, wb_ref, wo_ref, dlg_ref, da, db, dc, dd, dwo_ref, dwb_ref):
        @pl.when(pl.program_id(0) == 0)
        def _():
            dwo_ref[...] = jnp.zeros_like(dwo_ref)
            dwb_ref[...] = jnp.zeros_like(dwb_ref)

        dxb = dx_ref[...].astype(BF16)
        d_merged = _dot_nt(dxb, wo_ref[...])
        dwo_ref[...] += _dot_tn(mg_ref[...], dxb)
        for n, (y_ref, dy_ref) in enumerate(((ya, da), (yb, db), (yc, dc), (yd, dd))):
            yv = y_ref[...]
            gate = _sigmoid(lg_ref[:, n * D_MODEL:(n + 1) * D_MODEL])
            pn = _dot(yv, wb_ref[n])
            dlg_ref[:, n * D_MODEL:(n + 1) * D_MODEL] = (d_merged * pn * gate * (1.0 - gate)).astype(BF16)
            dpn = (d_merged * gate).astype(BF16)
            dy_ref[...] = _dot_nt(dpn, wb_ref[n])
            dwb_ref[n] += _dot_tn(yv, dpn)

    row = lambda w: pl.BlockSpec((tm, w), lambda i: (i, 0))
    wb_spec = pl.BlockSpec((N_HEADS, BR, D_MODEL), lambda i: (0, 0, 0))
    wo_spec = pl.BlockSpec((D_MODEL, D_MODEL), lambda i: (0, 0))
    dy_shape = jax.ShapeDtypeStruct((s, BR), F32)
    outs = pl.pallas_call(
        body, name="merge_bwd", grid=(s // tm,),
        in_specs=[row(D_MODEL), pl.BlockSpec((tm, MRG0), lambda i: (i, 1)), row(BR), row(BR), row(BR), row(BR), row(D_MODEL),
                  wb_spec, wo_spec],
        out_specs=[pl.BlockSpec((tm, MRG0), lambda i: (i, 1)), row(BR), row(BR), row(BR), row(BR), wo_spec, wb_spec],
        out_shape=[jax.ShapeDtypeStruct((s, PCOLS), BF16), dy_shape, dy_shape, dy_shape, dy_shape,
                   jax.ShapeDtypeStruct((D_MODEL, D_MODEL), F32), jax.ShapeDtypeStruct((N_HEADS, BR, D_MODEL), F32)],
        compiler_params=_params("arbitrary"))(dxn, proj, *ys, merged, wb, wo)
    return outs[0], outs[1:5], outs[5], outs[6]


def _loss_head(x, target, g):
    s = x.shape[0]
    tm = min(512, s)

    def body(x_ref, t_ref, g_ref, loss_ref, dx_ref, dg_ref):
        @pl.when(pl.program_id(0) == 0)
        def _():
            loss_ref[...] = jnp.zeros_like(loss_ref)
            dg_ref[...] = jnp.zeros_like(dg_ref)

        xv = x_ref[...]
        gv = g_ref[...]
        r = lax.rsqrt(jnp.mean(xv * xv, axis=-1, keepdims=True) + EPS)
        xn = xv * r
        err = xn * gv - t_ref[...]
        loss_ref[...] += 0.5 * jnp.sum(jnp.mean(err * err, axis=-1, keepdims=True))
        dy = err * (1.0 / D_MODEL)
        dg_ref[...] += _part8(dy * xn)
        gy = dy * gv
        dx_ref[...] = r * (gy - xn * jnp.mean(xn * gy, axis=-1, keepdims=True))

    row = pl.BlockSpec((tm, D_MODEL), lambda i: (i, 0))
    return pl.pallas_call(
        body, name="loss_head", grid=(s // tm,),
        in_specs=[row, row, pl.BlockSpec((1, D_MODEL), lambda i: (0, 0))],
        out_specs=[pl.BlockSpec((8, 128), lambda i: (0, 0)), row, pl.BlockSpec((8, D_MODEL), lambda i: (0, 0))],
        out_shape=[jax.ShapeDtypeStruct((8, 128), F32), jax.ShapeDtypeStruct((s, D_MODEL), F32), jax.ShapeDtypeStruct((8, D_MODEL), F32)],
        compiler_params=_params("arbitrary"))(x, target, g)


_SEGMENTS = ((0, 512, AUV0), (512, 256, AG0), (768, 768, QKV0), (1536, 4, BF0), (1540, 256, BG0), (1796, 768, CIN0),
             (2564, 256, CG0), (2820, 512, DGLU0), (3332, 256, DG0), (3588, 4096, MRG0))


def _to_aligned_cols(w):
    out, pos = [], 0
    for src, width, dst in sorted(_SEGMENTS, key=lambda t: t[2]):
        if dst > pos:
            out.append(jnp.zeros(w.shape[:-1] + (dst - pos,), w.dtype))
        out.append(w[..., src:src + width])
        pos = dst + width
    return jnp.concatenate(out, axis=-1)


def _from_aligned_cols(g):
    return jnp.concatenate([g[..., dst:dst + width] for _, width, dst in _SEGMENTS], axis=-1)


def _sgu_bias_rows(sgu_b):
    return jnp.repeat(sgu_b.T, HEAD_DIM, axis=1)


def _layer_fwd(x, p):
    proj, h = _inproj_fwd(x, p["norm_g"], p["w_in"])
    ya = _mix_a_fwd(proj, p["sgu_w"], _sgu_bias_rows(p["sgu_b"]), p["sgu_ln_g"], p["sgu_ln_b"])
    qe, ke, ve = _attn_prep_fwd(proj, p["f_bias"])
    oe = _attn_fwd(qe, ke, ve)
    yb, o = _bgate_fwd(proj, oe)
    yc = _mix_c_fwd(proj, p["short_conv_w"])
    yd = _mix_d_fwd(proj, p["conf_dw_w"], p["conf_dw_b"], p["conf_ln_g"], p["conf_ln_b"])
    ys = (ya, yb, yc, yd)
    x_next, merged = _merge_fwd(x, proj, ys, p["w_branch"], p["w_out"])
    saved = dict(x=x, proj=proj, h=h, ys=ys, merged=merged, qe=qe, ke=ke, ve=ve, oe=oe, o=o)
    return x_next, saved


def _layer_bwd(dxn, p, sv):
    proj = sv["proj"]
    dproj, dys, d_wo, d_wb = _merge_bwd(dxn, proj, sv["ys"], sv["merged"], p["w_branch"], p["w_out"])
    dproj, d_sgu_w, d_bias_rows, d_sgu_lg, d_sgu_lb = _mix_a_bwd(
        proj, dproj, dys[0], p["sgu_w"], _sgu_bias_rows(p["sgu_b"]), p["sgu_ln_g"], p["sgu_ln_b"])
    dproj, doe, stats = _bgate_bwd(proj, dproj, sv["o"], sv["oe"], dys[1])
    dqe, dke, dve = _attn_bwd(sv["qe"], sv["ke"], sv["ve"], doe, stats)
    dproj, d_fb = _attn_prep_bwd(proj, dproj, dqe, dke, dve, p["f_bias"])
    dproj, d_sc = _mix_c_bwd(proj, dproj, dys[2], p["short_conv_w"])
    dproj, d_cw, d_cb, d_clg, d_clb = _mix_d_bwd(proj, dproj, dys[3], p["conf_dw_w"], p["conf_dw_b"], p["conf_ln_g"], p["conf_ln_b"])
    dx, dg8 = _inproj_bwd_x(dproj, p["w_in"], sv["x"], dxn, p["norm_g"])
    d_win = _inproj_bwd_w(sv["h"], dproj)
    grads = dict(
        norm_g=jnp.sum(dg8, axis=0), w_in=d_win, f_bias=d_fb[0, :N_HEADS], sgu_w=d_sgu_w,
        sgu_b=d_bias_rows.reshape(CHUNK, N_HEADS, HEAD_DIM).sum(axis=-1).T,
        sgu_ln_g=d_sgu_lg[0], sgu_ln_b=d_sgu_lb[0], short_conv_w=d_sc[:K_SHORT], conf_dw_w=d_cw[:K_CONF],
        conf_dw_b=d_cb[0], conf_ln_g=d_clg[0], conf_ln_b=d_clb[0], w_branch=d_wb, w_out=d_wo)
    return dx, grads


def _local_step(x, target, layers, final_g):
    saved = []
    for p in layers:
        x, sv = _layer_fwd(x, p)
        saved.append(sv)
    loss8, dx, dfg8 = _loss_head(x, target, final_g)
    grads = [None] * len(layers)
    for l in reversed(range(len(layers))):
        dx, grads[l] = _layer_bwd(dx, layers[l], saved[l])
    return loss8[0, 0], dx, grads, jnp.sum(dfg8, axis=0)


def _place():
    return lax.axis_index("x"), lax.axis_index("y"), lax.axis_index("c")


def _other_chips(x, y):
    return [(1 - x, y), (x, 1 - y), (1 - x, 1 - y)]


def _gather_chips(blocks):
    n = len(blocks)

    def body(*refs):
        ins, outs, (send_sems, recv_sems) = refs[:n], refs[n:2 * n], refs[2 * n:]
        x, y, cc = _place()
        me, sibling = (x, y, cc), (x, y, 1 - cc)
        chips = _other_chips(x, y)

        def copy(a, k, chip, layer, to, src=None):
            dst = outs[a].at[2 * chip[0] + chip[1], layer]
            return pltpu.make_async_remote_copy(src_ref=dst if src is None else src, dst_ref=dst, send_sem=send_sems.at[a, k],
                                                recv_sem=recv_sems.at[a, k], device_id=to, device_id_type=MESH)

        first = [copy(a, j, (x, y), cc, (*chip, cc), src=ins[a]) for j, chip in enumerate(chips) for a in range(n)]
        for cp in first:
            cp.start()
        passed = []
        for j, chip in enumerate(chips):
            for a in range(n):
                copy(a, j, chip, cc, me).wait_recv()
                passed.append(copy(a, 3 + j, chip, cc, sibling))
                passed[-1].start()
        for j, chip in enumerate(chips):
            for a in range(n):
                copy(a, 3 + j, chip, 1 - cc, me).wait_recv()
        for cp in first + passed:
            cp.wait_send()

    return pl.pallas_call(
        body, name="gather_chips", in_specs=[ANY] * n, out_specs=[ANY] * n,
        out_shape=[jax.ShapeDtypeStruct((N_CHIPS, 2) + b.shape, b.dtype) for b in blocks],
        scratch_shapes=[pltpu.SemaphoreType.DMA((n, 6)), pltpu.SemaphoreType.DMA((n, 6))])(*blocks)


def _pair_exchange(arrays):
    n = len(arrays)

    def body(*refs):
        ins, outs, (send_sems, recv_sems) = refs[:n], refs[n:2 * n], refs[2 * n:]
        x, y, cc = _place()
        cps = [pltpu.make_async_remote_copy(src_ref=ins[a], dst_ref=outs[a], send_sem=send_sems.at[a], recv_sem=recv_sems.at[a],
                                            device_id=(x, y, 1 - cc), device_id_type=MESH) for a in range(n)]
        for cp in cps:
            cp.start()
        for cp in cps:
            cp.wait()

    return pl.pallas_call(
        body, name="pair_exchange", in_specs=[ANY] * n, out_specs=[ANY] * n,
        out_shape=[jax.ShapeDtypeStruct(a.shape, a.dtype) for a in arrays],
        scratch_shapes=[pltpu.SemaphoreType.DMA((n,)), pltpu.SemaphoreType.DMA((n,))])(*arrays)


def _chip_exchange(arrays):
    n = len(arrays)

    def body(*refs):
        ins, outs, (send_sems, recv_sems) = refs[:n], refs[n:2 * n], refs[2 * n:]
        x, y, cc = _place()
        cps = [pltpu.make_async_remote_copy(src_ref=ins[a].at[2 * px + py], dst_ref=outs[a].at[k], send_sem=send_sems.at[a, k],
                                            recv_sem=recv_sems.at[a, k], device_id=(px, py, cc), device_id_type=MESH)
               for k, (px, py) in enumerate(_other_chips(x, y)) for a in range(n)]
        for cp in cps:
            cp.start()
        for cp in cps:
            cp.wait()

    return pl.pallas_call(
        body, name="chip_exchange", in_specs=[ANY] * n, out_specs=[ANY] * n,
        out_shape=[jax.ShapeDtypeStruct((3,) + a.shape[1:], a.dtype) for a in arrays],
        scratch_shapes=[pltpu.SemaphoreType.DMA((n, 3)), pltpu.SemaphoreType.DMA((n, 3))])(*arrays)


def _allreduce8(buf):
    r, c = buf.shape

    def body(x_ref, o_ref, recv_buf, send_sems, recv_sems):
        x, y, cc = _place()
        o_ref[...] = x_ref[...]
        for k, peer in enumerate([(x, y, 1 - cc), (1 - x, y, cc), (x, 1 - y, cc)]):
            cp = pltpu.make_async_remote_copy(src_ref=o_ref, dst_ref=recv_buf.at[k], send_sem=send_sems.at[k],
                                              recv_sem=recv_sems.at[k], device_id=peer, device_id_type=MESH)
            cp.start()
            cp.wait()
            o_ref[...] = o_ref[...] + recv_buf[k]

    return pl.pallas_call(
        body, name="allreduce8", in_specs=[VMEM], out_specs=VMEM, out_shape=jax.ShapeDtypeStruct((r, c), F32),
        scratch_shapes=[pltpu.VMEM((3, r, c), F32), pltpu.SemaphoreType.DMA((3,)), pltpu.SemaphoreType.DMA((3,))],
        compiler_params=_params())(buf)


def _row_tile(rows, cols, limit_bytes=1 << 20):
    t = rows
    while t % 16 == 0 and t * cols * 4 > limit_bytes:
        t //= 2
    return t


def _add_pair(a, b):
    r, c = a.shape
    t = _row_tile(r, c)

    def body(a_ref, b_ref, o_ref):
        o_ref[...] = (a_ref[...].astype(F32) + b_ref[...].astype(F32)).astype(o_ref.dtype)

    spec = pl.BlockSpec((t, c), lambda i: (i, 0))
    return pl.pallas_call(body, name="add_pair", grid=(r // t,), in_specs=[spec, spec], out_specs=spec,
                          out_shape=jax.ShapeDtypeStruct((r, c), a.dtype), compiler_params=_params("arbitrary"))(a, b)


def _sum_chips(own, recv):
    r, c = own.shape
    t = _row_tile(r, c)

    def body(o_ref, r_ref, out_ref):
        acc = o_ref[...].astype(F32)
        for k in range(3):
            acc = acc + r_ref[k].astype(F32)
        out_ref[...] = acc

    return pl.pallas_call(
        body, name="sum_chips", grid=(r // t,),
        in_specs=[pl.BlockSpec((t, c), lambda i: (i, 0)), pl.BlockSpec((3, t, c), lambda i: (0, i, 0))],
        out_specs=pl.BlockSpec((t, c), lambda i: (i, 0)), out_shape=jax.ShapeDtypeStruct((r, c), F32),
        compiler_params=_params("arbitrary"))(own, recv)


def _adamw(w, m, v, g):
    r, c = w.shape
    t = _row_tile(r, c)

    def body(w_ref, m_ref, v_ref, g_ref, d_ref, mo_ref, vo_ref):
        gv = g_ref[...]
        mn = ADAM_B1 * m_ref[...] + (1.0 - ADAM_B1) * gv
        vn = ADAM_B2 * v_ref[...] + (1.0 - ADAM_B2) * (gv * gv)
        m_hat = mn / (1.0 - ADAM_B1 ** ADAM_STEP)
        v_hat = vn / (1.0 - ADAM_B2 ** ADAM_STEP)
        d_ref[...] = -ADAM_LR * (m_hat / (jnp.sqrt(v_hat) + ADAM_EPS) + ADAM_WD * w_ref[...])
        mo_ref[...] = mn
        vo_ref[...] = vn

    spec = pl.BlockSpec((t, c), lambda i: (i, 0))
    shape = jax.ShapeDtypeStruct((r, c), F32)
    return pl.pallas_call(body, name="adamw", grid=(r // t,), in_specs=[spec] * 4, out_specs=[spec] * 3,
                          out_shape=[shape] * 3, compiler_params=_params("arbitrary"))(w, m, v, g)


SMALL =("norm_g", "f_bias", "sgu_w", "sgu_b", "sgu_ln_g", "sgu_ln_b", "short_conv_w", "conf_dw_w", "conf_dw_b",
         "conf_ln_g", "conf_ln_b", "final_g")
WEIGHTS = ("norm_g", "w_in", "f_bias", "sgu_w", "sgu_b", "sgu_ln_g", "sgu_ln_b", "short_conv_w", "conf_dw_w",
           "conf_dw_b", "conf_ln_g", "conf_ln_b", "w_branch", "w_out", "final_g")


def _pack(arrays):
    flat = jnp.concatenate([a.reshape(-1) for a in arrays])
    n = flat.shape[0]
    pad = (-n) % 1024
    return jnp.pad(flat, (0, pad)).reshape(-1, 128)


def _unpack(buf, shapes):
    flat, out, pos = buf.reshape(-1), [], 0
    for shp in shapes:
        n = math.prod(shp)
        out.append(flat[pos:pos + n].reshape(shp))
        pos += n
    return out


def _aligned_pieces():
    out, pos = [], 0
    for src, width, dst in sorted(_SEGMENTS, key=lambda t: t[2]):
        if dst > pos:
            out.append((None, 0, dst - pos))
        lo = src
        while lo < src + width:
            j = lo // SHARD_COLS
            hi = min(src + width, (j + 1) * SHARD_COLS)
            out.append((j, lo - j * SHARD_COLS, hi - lo))
            lo = hi
        pos = dst + width
    return out


def _shard_pieces(j):
    lo_s, hi_s = j * SHARD_COLS, (j + 1) * SHARD_COLS
    out = []
    for src, width, dst in _SEGMENTS:
        lo, hi = max(src, lo_s), min(src + width, hi_s)
        if lo < hi:
            out.append((dst + lo - src, hi - lo))
    return out


def _w_in_aligned(shards):
    rows, dtype = shards[0].shape[0], shards[0].dtype
    return jnp.concatenate([jnp.zeros((rows, w), dtype) if j is None else shards[j][:, c0:c0 + w]
                            for j, c0, w in _aligned_pieces()], axis=1)


def _w_in_shard(g, j):
    parts = [g[:, c0:c0 + w] for c0, w in _shard_pieces(j)]
    return jnp.concatenate(parts + [jnp.zeros((g.shape[0], SHARD_PAD - SHARD_COLS), g.dtype)], axis=1)


def _by_layer(pc, own, other):
    return jnp.where(pc == 0, jnp.stack([own, other]), jnp.stack([other, own]))


def kernel(x, norm_g, w_in, f_bias, sgu_w, sgu_b, sgu_ln_g, sgu_ln_b, short_conv_w, conf_dw_w, conf_dw_b, conf_ln_g, conf_ln_b, w_branch, w_out, final_g, loss_target, m_norm_g, m_w_in, m_f_bias, m_sgu_w, m_sgu_b, m_sgu_ln_g, m_sgu_ln_b, m_short_conv_w, m_conf_dw_w, m_conf_dw_b, m_conf_ln_g, m_conf_ln_b, m_w_branch, m_w_out, m_final_g, v_norm_g, v_w_in, v_f_bias, v_sgu_w, v_sgu_b, v_sgu_ln_g, v_sgu_ln_b, v_short_conv_w, v_conf_dw_w, v_conf_dw_b, v_conf_ln_g, v_conf_ln_b, v_w_branch, v_w_out, v_final_g):
    px, py, pc = _place()
    chip = 2 * px + py
    depth = w_in.shape[0]
    w = dict(norm_g=norm_g, w_in=w_in, f_bias=f_bias, sgu_w=sgu_w, sgu_b=sgu_b, sgu_ln_g=sgu_ln_g, sgu_ln_b=sgu_ln_b,
             short_conv_w=short_conv_w, conf_dw_w=conf_dw_w, conf_dw_b=conf_dw_b, conf_ln_g=conf_ln_g, conf_ln_b=conf_ln_b,
             w_branch=w_branch, w_out=w_out, final_g=final_g)
    m = dict(norm_g=m_norm_g, w_in=m_w_in, f_bias=m_f_bias, sgu_w=m_sgu_w, sgu_b=m_sgu_b, sgu_ln_g=m_sgu_ln_g,
             sgu_ln_b=m_sgu_ln_b, short_conv_w=m_short_conv_w, conf_dw_w=m_conf_dw_w, conf_dw_b=m_conf_dw_b,
             conf_ln_g=m_conf_ln_g, conf_ln_b=m_conf_ln_b, w_branch=m_w_branch, w_out=m_w_out, final_g=m_final_g)
    v = dict(norm_g=v_norm_g, w_in=v_w_in, f_bias=v_f_bias, sgu_w=v_sgu_w, sgu_b=v_sgu_b, sgu_ln_g=v_sgu_ln_g,
             sgu_ln_b=v_sgu_ln_b, short_conv_w=v_short_conv_w, conf_dw_w=v_conf_dw_w, conf_dw_b=v_conf_dw_b,
             conf_ln_g=v_conf_ln_g, conf_ln_b=v_conf_ln_b, w_branch=v_w_branch, w_out=v_w_out, final_g=v_final_g)

    local = (jnp.pad(w_in, ((0, 0), (0, 0), (0, SHARD_PAD - SHARD_COLS))).astype(BF16),
             w_branch.astype(BF16).reshape(depth, N_HEADS * BR, BR), w_out.astype(BF16))
    pick = lambda a, i: lax.dynamic_index_in_dim(a, i, 0, keepdims=False)
    gathered = _gather_chips([pick(a, pc) for a in local])
    wi_all, wb_all, wo_all = [lax.dynamic_update_index_in_dim(got, mine, chip, 0) for got, mine in zip(gathered, local)]
    conv_ch = BR // N_CHIPS
    place = lambda a: lax.dynamic_update_slice_in_dim(jnp.zeros(a.shape[:-1] + (BR,), F32), a, conv_ch * chip, axis=2)
    conv_shapes = [(depth, K_SHORT, BR), (depth, K_CONF, BR)]
    conv_buf = _pack([place(short_conv_w), place(conf_dw_w)]) * (pc == 0).astype(F32)
    short_full, conf_full = _unpack(_allreduce8(conv_buf), conv_shapes)

    layers = []
    for l in range(depth):
        wb = jnp.concatenate([wb_all[j, l].reshape(N_HEADS, BR, BR) for j in range(N_CHIPS)], axis=2)
        wo = jnp.concatenate([wo_all[j, l] for j in range(N_CHIPS)], axis=0)
        layers.append(dict(
            norm_g=norm_g[l][None], w_in=_w_in_aligned([wi_all[j, l] for j in range(N_CHIPS)]),
            f_bias=jnp.pad(f_bias[l], (0, CHUNK - N_HEADS))[None],
            sgu_w=sgu_w[l], sgu_b=sgu_b[l], sgu_ln_g=sgu_ln_g[l][None], sgu_ln_b=sgu_ln_b[l][None],
            short_conv_w=short_full[l], conf_dw_w=conf_full[l], conf_dw_b=conf_dw_b[l][None],
            conf_ln_g=conf_ln_g[l][None], conf_ln_b=conf_ln_b[l][None], w_branch=wb, w_out=wo))

    loss_local, grad_x, grads, d_final_g = _local_step(x[0], loss_target[0], layers, final_g[None])
    loss = lax.psum(loss_local, ("x", "y", "c"))

    g2 = (jnp.stack([jnp.stack([_w_in_shard(grads[l]["w_in"], j) for j in range(N_CHIPS)]) for l in range(depth)]),
          jnp.stack([jnp.stack([grads[l]["w_branch"][:, :, j * BR:(j + 1) * BR].reshape(N_HEADS * BR, BR).astype(BF16)
                                for j in range(N_CHIPS)]) for l in range(depth)]),
          jnp.stack([jnp.stack([grads[l]["w_out"][j * BR:(j + 1) * BR].astype(BF16) for j in range(N_CHIPS)]) for l in range(depth)]))
    flat = lambda a: a.reshape(-1, a.shape[-1])
    from_sibling = _pair_exchange([flat(pick(a, 1 - pc)) for a in g2])
    h = [_add_pair(flat(pick(a, pc)), r).reshape(a.shape[1:]) for a, r in zip(g2, from_sibling)]
    summed = [_sum_chips(pick(a, chip), r) for a, r in zip(h, _chip_exchange(h))]
    gi, gb, go = [_by_layer(pc, own, other) for own, other in zip(summed, _pair_exchange(summed))]
    g = dict(w_in=gi[:, :, :SHARD_COLS], w_branch=gb.reshape(depth, N_HEADS, BR, BR), w_out=go)

    small_local = [jnp.stack([grads[l][n] for l in range(depth)]) for n in SMALL[:-1]] + [d_final_g]
    small_shapes = [a.shape for a in small_local]
    for n, a in zip(SMALL, _unpack(_allreduce8(_pack(small_local)), small_shapes)):
        g[n] = a
    for n in ("short_conv_w", "conf_dw_w"):
        g[n] = lax.dynamic_slice_in_dim(g[n], conv_ch * chip, conv_ch, axis=2)

    delta, new_m, new_v = {}, {}, {}
    for n in ("w_in", "w_branch", "w_out"):
        shp = w[n].shape
        two_d = lambda a: a.reshape(-1, shp[-1])
        d_, m_, v_ = _adamw(two_d(w[n]), two_d(m[n]), two_d(v[n]), two_d(g[n]))
        delta[n], new_m[n], new_v[n] = d_.reshape(shp), m_.reshape(shp), v_.reshape(shp)
    shapes = [w[n].shape for n in SMALL]
    d_, m_, v_ = _adamw(*[_pack([t[n] for n in SMALL]) for t in (w, m, v, g)])
    for n, a, b, c_ in zip(SMALL, _unpack(d_, shapes), _unpack(m_, shapes), _unpack(v_, shapes)):
        delta[n], new_m[n], new_v[n] = a, b, c_

    return (loss, grad_x[None], *[g[n] for n in WEIGHTS], *[delta[n] for n in WEIGHTS],
            *[new_m[n] for n in WEIGHTS], *[new_v[n] for n in WEIGHTS])
```

```python
import functools
import math

import jax
import jax.numpy as jnp
from jax import lax
from jax.experimental import pallas as pl
from jax.experimental.pallas import tpu as pltpu

F32 = jnp.float32
BF16 = jnp.bfloat16

D_MODEL = 1024
BR = 256
N_HEADS = 4
HEAD_DIM = 64
CHUNK = 128
K_SHORT = 3
K_CONF = 31
EPS = 1e-6
IN_COLS = 7684
SHARD_COLS = IN_COLS // 4
SHARD_PAD = 2048
N_CHIPS = 4

QKV0, CIN0, AUV0, DGLU0 = 0, 768, 1536, 2048
AG0, BG0, CG0, DG0 = 2560, 2816, 3072, 3328
BF0 = 3584
MRG0 = 4096
PCOLS = 8192

V7X_VMEM_BYTES = 64 * 1024 * 1024
VMEM_LIMIT = V7X_VMEM_BYTES * 7 // 8

ADAM_LR, ADAM_B1, ADAM_B2, ADAM_EPS, ADAM_WD, ADAM_STEP = 0.001, 0.9, 0.999, 1e-08, 0.01, 10

MESH = pl.DeviceIdType.MESH
ANY = pl.BlockSpec(memory_space=pl.ANY)
VMEM = pl.BlockSpec(memory_space=pltpu.VMEM)

GELU_C0 = math.sqrt(2.0 / math.pi)
GELU_C1 = 0.044715


def _params(*sem):
    return pltpu.CompilerParams(dimension_semantics=sem, vmem_limit_bytes=VMEM_LIMIT)


def _sigmoid(x):
    return 1.0 / (1.0 + jnp.exp(-x))


def _silu_and_grad(x):
    s = _sigmoid(x)
    return x * s, s * (1.0 + x * (1.0 - s))


def _gelu_and_grad(z):
    z2 = z * z
    t = jnp.tanh(GELU_C0 * (z + GELU_C1 * z2 * z))
    half = 0.5 * (1.0 + t)
    return z * half, half + 0.5 * z * (1.0 - t * t) * (GELU_C0 * (1.0 + 3.0 * GELU_C1 * z2))


def _ln_fwd(v):
    mu = jnp.mean(v, axis=-1, keepdims=True)
    xc = v - mu
    rs = lax.rsqrt(jnp.mean(xc * xc, axis=-1, keepdims=True) + EPS)
    return xc * rs, rs


def _ln_bwd(d_xh, xh, rs):
    return rs * (d_xh - jnp.mean(d_xh, axis=-1, keepdims=True) - xh * jnp.mean(d_xh * xh, axis=-1, keepdims=True))


def _part8(a):
    return a.reshape(a.shape[0] // 8, 8, a.shape[1]).sum(axis=0)


def _dot(a, b):
    return jnp.dot(a, b, preferred_element_type=F32)


def _dot_nt(a, b):
    return lax.dot_general(a, b, (((1,), (1,)), ((), ())), preferred_element_type=F32)


def _dot_tn(a, b):
    return lax.dot_general(a, b, (((0,), (0,)), ((), ())), preferred_element_type=F32)


def _head_masks(dtype):
    lane = lax.broadcasted_iota(jnp.int32, (1, BR), 1) // HEAD_DIM
    return [(lane == h).astype(dtype) for h in range(N_HEADS)]


def _window(ref, col0, width):
    return ref.at[:, pl.ds(col0, width)]


def _copy_all(pairs, sems):
    cps = [pltpu.make_async_copy(s, d, sems.at[i]) for i, (s, d) in enumerate(pairs)]
    for cp in cps:
        cp.start()
    for cp in cps:
        cp.wait()


def _inproj_fwd(x, g, w):
    s = x.shape[0]
    tm, tn = min(1024, s), 1024

    def body(x_ref, g_ref, w_ref, proj_ref, h_ref):
        @pl.when(pl.program_id(1) == 0)
        def _():
            xv = x_ref[...]
            r = lax.rsqrt(jnp.mean(xv * xv, axis=-1, keepdims=True) + EPS)
            h_ref[...] = ((xv * r) * g_ref[...]).astype(BF16)
        proj_ref[...] = _dot(h_ref[...], w_ref[...])

    return pl.pallas_call(
        body, name="inproj_fwd", grid=(s // tm, PCOLS // tn),
        in_specs=[pl.BlockSpec((tm, D_MODEL), lambda i, j: (i, 0)), pl.BlockSpec((1, D_MODEL), lambda i, j: (0, 0)),
                  pl.BlockSpec((D_MODEL, tn), lambda i, j: (0, j))],
        out_specs=[pl.BlockSpec((tm, tn), lambda i, j: (i, j)), pl.BlockSpec((tm, D_MODEL), lambda i, j: (i, 0))],
        out_shape=[jax.ShapeDtypeStruct((s, PCOLS), F32), jax.ShapeDtypeStruct((s, D_MODEL), BF16)],
        compiler_params=_params("arbitrary", "arbitrary"))(x, g, w)


def _rms_bwd(dh, x, g):
    r = lax.rsqrt(jnp.mean(x * x, axis=-1, keepdims=True) + EPS)
    xn = x * r
    gy = dh * g
    dx = r * (gy - xn * jnp.mean(xn * gy, axis=-1, keepdims=True))
    return dx, _part8(dh * xn)


def _inproj_bwd_x(dproj, w, x, dxn, g):
    s = x.shape[0]
    tm, tk = min(512, s), 4096
    nk = PCOLS // tk

    def body(dp_ref, w_ref, x_ref, dxn_ref, g_ref, dx_ref, dg_ref, acc_ref):
        i, k = pl.program_id(0), pl.program_id(1)

        @pl.when(k == 0)
        def _():
            acc_ref[...] = jnp.zeros_like(acc_ref)

        @pl.when((i == 0) & (k == 0))
        def _():
            dg_ref[...] = jnp.zeros_like(dg_ref)

        acc_ref[...] += _dot_nt(dp_ref[...], w_ref[...])

        @pl.when(k == nk - 1)
        def _():
            dx, dg8 = _rms_bwd(acc_ref[...], x_ref[...], g_ref[...])
            dx_ref[...] = dxn_ref[...] + dx
            dg_ref[...] += dg8

    dx, dg8 = pl.pallas_call(
        body, name="inproj_bwd_x", grid=(s // tm, nk),
        in_specs=[pl.BlockSpec((tm, tk), lambda i, k: (i, k)), pl.BlockSpec((D_MODEL, tk), lambda i, k: (0, k)),
                  pl.BlockSpec((tm, D_MODEL), lambda i, k: (i, 0)), pl.BlockSpec((tm, D_MODEL), lambda i, k: (i, 0)),
                  pl.BlockSpec((1, D_MODEL), lambda i, k: (0, 0))],
        out_specs=[pl.BlockSpec((tm, D_MODEL), lambda i, k: (i, 0)), pl.BlockSpec((8, D_MODEL), lambda i, k: (0, 0))],
        out_shape=[jax.ShapeDtypeStruct((s, D_MODEL), F32), jax.ShapeDtypeStruct((8, D_MODEL), F32)],
        scratch_shapes=[pltpu.VMEM((tm, D_MODEL), F32)],
        compiler_params=_params("arbitrary", "arbitrary"))(dproj, w, x, dxn, g)
    return dx, dg8


def _inproj_bwd_w(h, dproj):
    s = h.shape[0]
    tn, tk = 1024, min(1024, s)
    nk = s // tk

    def body(h_ref, dp_ref, dw_ref, acc_ref):
        k = pl.program_id(1)

        @pl.when(k == 0)
        def _():
            acc_ref[...] = jnp.zeros_like(acc_ref)

        acc_ref[...] += _dot_tn(h_ref[...], dp_ref[...])

        @pl.when(k == nk - 1)
        def _():
            dw_ref[...] = acc_ref[...].astype(BF16)

    return pl.pallas_call(
        body, name="inproj_bwd_w", grid=(PCOLS // tn, nk),
        in_specs=[pl.BlockSpec((tk, D_MODEL), lambda j, k: (k, 0)), pl.BlockSpec((tk, tn), lambda j, k: (k, j))],
        out_specs=pl.BlockSpec((D_MODEL, tn), lambda j, k: (0, j)),
        out_shape=jax.ShapeDtypeStruct((D_MODEL, PCOLS), BF16),
        scratch_shapes=[pltpu.VMEM((D_MODEL, tn), F32)],
        compiler_params=_params("arbitrary", "arbitrary"))(h, dproj)


def _mix_a_chunk(uvp, agp, wm_ref, bias, lg, lb):
    u, du = _gelu_and_grad(uvp[:, :BR])
    v, dv = _gelu_and_grad(uvp[:, BR:])
    xh, rs = _ln_fwd(v)
    vnb = (xh * lg + lb).astype(BF16)
    masks = _head_masks(BF16)
    mixed = bias
    for h in range(N_HEADS):
        mixed = mixed + _dot(wm_ref[h], vnb * masks[h])
    sg, dsg = _silu_and_grad(agp)
    return u, du, dv, xh, rs, vnb, masks, mixed, sg, dsg


def _store_masked_sgu(sw_ref, wm_ref):
    row = lax.broadcasted_iota(jnp.int32, (CHUNK, CHUNK), 0)
    col = lax.broadcasted_iota(jnp.int32, (CHUNK, CHUNK), 1)
    for h in range(N_HEADS):
        wm_ref[h] = jnp.where(row >= col, sw_ref[h], 0.0).astype(BF16)


def _mix_a_fwd(proj, sgu_w, bias, lg, lb):
    s = proj.shape[0]

    def body(proj_ref, sw_ref, bias_ref, lg_ref, lb_ref, y_ref, uv_buf, ag_buf, wm_ref, sems):
        _copy_all([(_window(proj_ref, AUV0, 2 * BR), uv_buf), (_window(proj_ref, AG0, BR), ag_buf)], sems)
        _store_masked_sgu(sw_ref, wm_ref)

        def chunk(i, c):
            rows = pl.ds(pl.multiple_of(i * CHUNK, CHUNK), CHUNK)
            u, _, _, _, _, _, _, mixed, sg, _ = _mix_a_chunk(uv_buf[rows, :], ag_buf[rows, :], wm_ref, bias_ref[...],
                                                            lg_ref[...], lb_ref[...])
            y_ref[rows, :] = (u * mixed * sg).astype(BF16)
            return c

        lax.fori_loop(0, s // CHUNK, chunk, 0)

    return pl.pallas_call(
        body, name="mix_a_fwd", in_specs=[ANY, VMEM, VMEM, VMEM, VMEM], out_specs=VMEM,
        out_shape=jax.ShapeDtypeStruct((s, BR), BF16),
        scratch_shapes=[pltpu.VMEM((s, 2 * BR), F32), pltpu.VMEM((s, BR), F32), pltpu.VMEM((N_HEADS, CHUNK, CHUNK), BF16),
                        pltpu.SemaphoreType.DMA((2,))],
        compiler_params=_params())(proj, sgu_w, bias, lg, lb)


def _mix_a_bwd(proj, dproj, dy, sgu_w, bias, lg, lb):
    s = proj.shape[0]

    def body(proj_ref, dproj_in, dy_ref, sw_ref, bias_ref, lg_ref, lb_ref,
             dproj_ref, dsw_ref, dbias_ref, dlg_ref, dlb_ref,
             uv_buf, ag_buf, duv_buf, dag_buf, wm_ref, acc_lg, acc_lb, sems):
        del dproj_in
        _copy_all([(_window(proj_ref, AUV0, 2 * BR), uv_buf), (_window(proj_ref, AG0, BR), ag_buf)], sems)
        _store_masked_sgu(sw_ref, wm_ref)
        dsw_ref[...] = jnp.zeros_like(dsw_ref)
        dbias_ref[...] = jnp.zeros_like(dbias_ref)
        acc_lg[...] = jnp.zeros_like(acc_lg)
        acc_lb[...] = jnp.zeros_like(acc_lb)

        def chunk(i, c):
            rows = pl.ds(pl.multiple_of(i * CHUNK, CHUNK), CHUNK)
            lg_v = lg_ref[...]
            u, du, dv, xh, rs, vnb, masks, mixed, sg, dsg = _mix_a_chunk(
                uv_buf[rows, :], ag_buf[rows, :], wm_ref, bias_ref[...], lg_v, lb_ref[...])
            dyv = dy_ref[rows, :]
            t1 = dyv * sg
            d_u = t1 * mixed
            d_mixed = t1 * u
            d_ag = dyv * u * mixed * dsg
            dbias_ref[...] += d_mixed
            dmb = d_mixed.astype(BF16)
            d_vn = jnp.zeros((CHUNK, BR), F32)
            for h in range(N_HEADS):
                dm_h = dmb * masks[h]
                dsw_ref[h] += _dot_nt(dm_h, vnb)
                d_vn = d_vn + _dot_tn(wm_ref[h], dm_h)
            acc_lg[...] += _part8(d_vn * xh)
            acc_lb[...] += _part8(d_vn)
            d_v = _ln_bwd(d_vn * lg_v, xh, rs)
            duv_buf[rows, :] = jnp.concatenate([d_u * du, d_v * dv], axis=1).astype(BF16)
            dag_buf[rows, :] = d_ag.astype(BF16)
            return c

        lax.fori_loop(0, s // CHUNK, chunk, 0)
        row = lax.broadcasted_iota(jnp.int32, (CHUNK, CHUNK), 0)
        col = lax.broadcasted_iota(jnp.int32, (CHUNK, CHUNK), 1)
        for h in range(N_HEADS):
            dsw_ref[h] = jnp.where(row >= col, dsw_ref[h], 0.0)
        dlg_ref[...] = jnp.sum(acc_lg[...], axis=0, keepdims=True)
        dlb_ref[...] = jnp.sum(acc_lb[...], axis=0, keepdims=True)
        _copy_all([(duv_buf, _window(dproj_ref, AUV0, 2 * BR)), (dag_buf, _window(dproj_ref, AG0, BR))], sems)

    return pl.pallas_call(
        body, name="mix_a_bwd", in_specs=[ANY, ANY, VMEM, VMEM, VMEM, VMEM, VMEM],
        out_specs=[ANY, VMEM, VMEM, VMEM, VMEM],
        out_shape=[jax.ShapeDtypeStruct(dproj.shape, dproj.dtype), jax.ShapeDtypeStruct((N_HEADS, CHUNK, CHUNK), F32),
                   jax.ShapeDtypeStruct((CHUNK, BR), F32), jax.ShapeDtypeStruct((1, BR), F32), jax.ShapeDtypeStruct((1, BR), F32)],
        scratch_shapes=[pltpu.VMEM((s, 2 * BR), F32), pltpu.VMEM((s, BR), F32), pltpu.VMEM((s, 2 * BR), BF16),
                        pltpu.VMEM((s, BR), BF16), pltpu.VMEM((N_HEADS, CHUNK, CHUNK), BF16),
                        pltpu.VMEM((8, BR), F32), pltpu.VMEM((8, BR), F32), pltpu.SemaphoreType.DMA((2,))],
        input_output_aliases={1: 0}, compiler_params=_params())(proj, dproj, dy, sgu_w, bias, lg, lb)


def _tri_ones(n, upper):
    row = lax.broadcasted_iota(jnp.int32, (n, n), 0)
    col = lax.broadcasted_iota(jnp.int32, (n, n), 1)
    return ((row <= col) if upper else (row >= col)).astype(BF16)


def _split3(c):
    hi = c.astype(BF16)
    r1 = c - hi.astype(F32)
    mid = r1.astype(BF16)
    lo = (r1 - mid.astype(F32)).astype(BF16)
    return [hi, mid, lo]


def _tri_sum(tri, a):
    parts = _split3(a)
    return _dot(tri, parts[0]) + _dot(tri, parts[1]) + _dot(tri, parts[2])


EXT = 2 * HEAD_DIM
LANE_CQ = HEAD_DIM
LANE_CK = HEAD_DIM + 3


def _to_head(h):
    r = lax.broadcasted_iota(jnp.int32, (BR, EXT), 0)
    c = lax.broadcasted_iota(jnp.int32, (BR, EXT), 1)
    return ((r == c + h * HEAD_DIM) & (c < HEAD_DIM)).astype(BF16)


def _from_head(h):
    r = lax.broadcasted_iota(jnp.int32, (EXT, BR), 0)
    c = lax.broadcasted_iota(jnp.int32, (EXT, BR), 1)
    return ((c == r + h * HEAD_DIM) & (r < HEAD_DIM)).astype(BF16)


def _attn_prep_fwd(proj, f_bias):
    s = proj.shape[0]
    scale = 1.0 / math.sqrt(HEAD_DIM)

    def body(proj_ref, fb_ref, qe_ref, ke_ref, ve_ref, ket_ref, vet_ref, qkv_buf, z_buf, sems):
        _copy_all([(_window(proj_ref, QKV0, 3 * BR), qkv_buf), (_window(proj_ref, BF0, CHUNK), z_buf)], sems)
        tri = _tri_ones(CHUNK, upper=False)
        lane = lax.broadcasted_iota(jnp.int32, (CHUNK, EXT), 1)
        ones_q = ((lane >= LANE_CK) & (lane < LANE_CK + 3)).astype(F32)
        ones_k = ((lane >= LANE_CQ) & (lane < LANE_CQ + 3)).astype(F32)

        def chunk(i, carry):
            rows = pl.ds(pl.multiple_of(i * CHUNK, CHUNK), CHUNK)
            cum = _tri_sum(tri, jax.nn.log_sigmoid(z_buf[rows, :] + fb_ref[...])) + carry
            qb = (qkv_buf[rows, 0:BR] * scale).astype(BF16)
            kb = qkv_buf[rows, BR:2 * BR].astype(BF16)
            vb = qkv_buf[rows, 2 * BR:].astype(BF16)
            parts = [p.astype(F32) for p in _split3(cum)]
            for h in range(N_HEADS):
                sel = _to_head(h)
                dec_q, dec_k = ones_q, ones_k
                for t, part in enumerate(parts):
                    pf = part[:, h:h + 1]
                    dec_q = dec_q + jnp.where(lane == LANE_CQ + t, pf, 0.0)
                    dec_k = dec_k - jnp.where(lane == LANE_CK + t, pf, 0.0)
                qe_ref[h, rows, :] = (_dot(qb, sel) + dec_q).astype(BF16)
                kh = _dot(kb, sel) + dec_k
                vh = _dot(vb, sel)
                ke_ref[h, rows, :] = kh.astype(BF16)
                ve_ref[h, rows, :] = vh.astype(BF16)
                ket_ref[h, :, rows] = kh.T.astype(BF16)
                vet_ref[h, :, rows] = vh.T.astype(BF16)
            return cum[CHUNK - 1:CHUNK, :]

        lax.fori_loop(0, s // CHUNK, chunk, jnp.zeros((1, CHUNK), F32))

    shape = jax.ShapeDtypeStruct((N_HEADS, s, EXT), BF16)
    shape_t = jax.ShapeDtypeStruct((N_HEADS, EXT, s), BF16)
    return pl.pallas_call(
        body, name="attn_prep_fwd", in_specs=[ANY, VMEM], out_specs=[VMEM] * 5, out_shape=[shape, shape, shape, shape_t, shape_t],
        scratch_shapes=[pltpu.VMEM((s, 3 * BR), F32), pltpu.VMEM((s, CHUNK), F32), pltpu.SemaphoreType.DMA((2,))],
        compiler_params=_params())(proj, f_bias)


def _attn_prep_bwd(proj, dproj, dqe, dke, dve, f_bias):
    s = proj.shape[0]
    pad = MRG0 - BF0
    scale = 1.0 / math.sqrt(HEAD_DIM)

    def body(proj_ref, dproj_in, dq_ref, dk_ref, dv_ref, fb_ref, dproj_ref, dfb_ref, z_buf, dqkv_buf, dz_buf, sems):
        del dproj_in
        _copy_all([(_window(proj_ref, BF0, CHUNK), z_buf)], sems)
        tri = _tri_ones(CHUNK, upper=True)
        lane = lax.broadcasted_iota(jnp.int32, (CHUNK, CHUNK), 1)
        dz_buf[...] = jnp.zeros_like(dz_buf)
        n = s // CHUNK

        def chunk(t, carry):
            suffix, acc = carry
            i = n - 1 - t
            rows = pl.ds(pl.multiple_of(i * CHUNK, CHUNK), CHUNK)
            dq = jnp.zeros((CHUNK, BR), F32)
            dk = jnp.zeros((CHUNK, BR), F32)
            dv = jnp.zeros((CHUNK, BR), F32)
            dcum = jnp.zeros((CHUNK, CHUNK), F32)
            for h in range(N_HEADS):
                back = _from_head(h)
                dqh = dq_ref[h, :, rows].T
                dkh = dk_ref[h, rows, :]
                dq = dq + _dot((dqh * scale).astype(BF16), back)
                dk = dk + _dot(dkh.astype(BF16), back)
                dv = dv + _dot(dv_ref[h, rows, :].astype(BF16), back)
                dcum = dcum + jnp.where(lane == h, dqh[:, LANE_CQ:LANE_CQ + 1] - dkh[:, LANE_CK:LANE_CK + 1], 0.0)
            dqkv_buf[rows, 0:BR] = dq.astype(BF16)
            dqkv_buf[rows, BR:2 * BR] = dk.astype(BF16)
            dqkv_buf[rows, 2 * BR:] = dv.astype(BF16)
            dlf = _tri_sum(tri, dcum) + suffix
            dz = dlf * _sigmoid(-(z_buf[rows, :] + fb_ref[...]))
            dz_buf[rows, pl.ds(0, CHUNK)] = dz.astype(BF16)
            return dlf[0:1, :], acc + _part8(dz)

        _, acc = lax.fori_loop(0, n, chunk, (jnp.zeros((1, CHUNK), F32), jnp.zeros((8, CHUNK), F32)))
        dfb_ref[...] = jnp.sum(acc, axis=0, keepdims=True)
        _copy_all([(dqkv_buf, _window(dproj_ref, QKV0, 3 * BR)), (dz_buf, _window(dproj_ref, BF0, pad))], sems)

    return pl.pallas_call(
        body, name="attn_prep_bwd", in_specs=[ANY, ANY, VMEM, VMEM, VMEM, VMEM], out_specs=[ANY, VMEM],
        out_shape=[jax.ShapeDtypeStruct(dproj.shape, dproj.dtype), jax.ShapeDtypeStruct((1, CHUNK), F32)],
        scratch_shapes=[pltpu.VMEM((s, CHUNK), F32), pltpu.VMEM((s, 3 * BR), BF16), pltpu.VMEM((s, pad), BF16),
                        pltpu.SemaphoreType.DMA((2,))],
        input_output_aliases={1: 0}, compiler_params=_params())(proj, dproj, dqe, dke, dve, f_bias)


ATT_TQ = 256
ATT_FWD_GROUP = 4
ATT_BWD_GROUP = 2
NEG_BIG = -1e30


def _causal_t(q0, k0, tk, tq):
    kpos = k0 + lax.broadcasted_iota(jnp.int32, (tk, tq), 0)
    qpos = q0 + lax.broadcasted_iota(jnp.int32, (tk, tq), 1)
    return kpos <= qpos


STAT_ROWS = 8


def _attn_fwd(qe, ke, vet):
    nh, s, w = qe.shape
    tq = min(ATT_TQ, s)
    tk = min(ATT_FWD_GROUP * tq, s)
    per = tk // tq

    def body(q_ref, k_ref, vt_ref, ot_ref, lse_ref):
        i = pl.program_id(1)
        q = q_ref[0]

        def step(j, carry, masked):
            m, l, acc = carry
            ks = pl.ds(pl.multiple_of(j * tk, tk), tk)
            st = _dot_nt(k_ref[0, ks, :], q)
            if masked:
                st = jnp.where(_causal_t(i * tq, j * tk, tk, tq), st, NEG_BIG)
            m_new = jnp.maximum(m, jnp.max(st, axis=0, keepdims=True))
            alpha = jnp.exp(m - m_new)
            pt = jnp.exp(st - m_new)
            l = alpha * l + jnp.sum(pt, axis=0, keepdims=True)
            acc = alpha * acc + _dot(vt_ref[0, :, ks], pt.astype(BF16))
            return m_new, l, acc

        init = (jnp.full((1, tq), NEG_BIG, F32), jnp.zeros((1, tq), F32), jnp.zeros((w, tq), F32))
        carry = lax.fori_loop(0, i // per, functools.partial(step, masked=False), init)
        m, l, acc = step(i // per, carry, True)
        ot_ref[0] = acc / l
        lse_ref[0] = jnp.broadcast_to(m + jnp.log(l), (STAT_ROWS, tq))

    return pl.pallas_call(
        body, name="attn_fwd", grid=(nh, s // tq),
        in_specs=[pl.BlockSpec((1, tq, w), lambda h, i: (h, i, 0)), pl.BlockSpec((1, s, w), lambda h, i: (h, 0, 0)),
                  pl.BlockSpec((1, w, s), lambda h, i: (h, 0, 0))],
        out_specs=[pl.BlockSpec((1, w, tq), lambda h, i: (h, 0, i)), pl.BlockSpec((1, STAT_ROWS, tq), lambda h, i: (h, 0, i))],
        out_shape=[jax.ShapeDtypeStruct((nh, w, s), F32), jax.ShapeDtypeStruct((nh, STAT_ROWS, s), F32)],
        compiler_params=_params("arbitrary", "arbitrary"))(qe, ke, vet)


def _attn_bwd(qe, ke, ket, ve, doe, lse, dd):
    nh, s, w = qe.shape
    tq = min(ATT_TQ, s)
    tk = min(ATT_BWD_GROUP * tq, s)
    per = tk // tq

    def body(q_ref, do_ref, lse_ref, dd_ref, k_ref, kt_ref, v_ref, dqt_ref, dk_ref, dv_ref):
        i = pl.program_id(1)

        @pl.when(i == 0)
        def _():
            dk_ref[...] = jnp.zeros_like(dk_ref)
            dv_ref[...] = jnp.zeros_like(dv_ref)

        q = q_ref[0]
        do = do_ref[0]
        lse_row = lse_ref[0, 0:1, :]
        dd_row = dd_ref[0, 0:1, :]

        def step(j, dqt, masked):
            ks = pl.ds(pl.multiple_of(j * tk, tk), tk)
            st = _dot_nt(k_ref[0, ks, :], q)
            pt = jnp.exp(st - lse_row)
            if masked:
                pt = jnp.where(_causal_t(i * tq, j * tk, tk, tq), pt, 0.0)
            dpt = _dot_nt(v_ref[0, ks, :], do)
            dst = (pt * (dpt - dd_row)).astype(BF16)
            dv_ref[0, ks, :] += _dot(pt.astype(BF16), do)
            dk_ref[0, ks, :] += _dot(dst, q)
            return dqt + _dot(kt_ref[0, :, ks], dst)

        dqt = lax.fori_loop(0, i // per, functools.partial(step, masked=False), jnp.zeros((w, tq), F32))
        dqt_ref[0] = step(i // per, dqt, True)

    qblk = pl.BlockSpec((1, tq, w), lambda h, i: (h, i, 0))
    stat = pl.BlockSpec((1, STAT_ROWS, tq), lambda h, i: (h, 0, i))
    whole = pl.BlockSpec((1, s, w), lambda h, i: (h, 0, 0))
    whole_t = pl.BlockSpec((1, w, s), lambda h, i: (h, 0, 0))
    return pl.pallas_call(
        body, name="attn_bwd", grid=(nh, s // tq),
        in_specs=[qblk, qblk, stat, stat, whole, whole_t, whole],
        out_specs=[pl.BlockSpec((1, w, tq), lambda h, i: (h, 0, i)), whole, whole],
        out_shape=[jax.ShapeDtypeStruct((nh, w, s), F32), jax.ShapeDtypeStruct((nh, s, w), F32), jax.ShapeDtypeStruct((nh, s, w), F32)],
        compiler_params=_params("arbitrary", "arbitrary"))(qe, doe, lse, dd, ke, ket, ve)


def _bgate_fwd(proj, ot):
    s = proj.shape[0]

    def body(proj_ref, ot_ref, y_ref, o_ref, g_buf, sems):
        _copy_all([(_window(proj_ref, BG0, BR), g_buf)], sems)

        def chunk(i, c):
            rows = pl.ds(pl.multiple_of(i * CHUNK, CHUNK), CHUNK)
            o = jnp.zeros((CHUNK, BR), F32)
            for h in range(N_HEADS):
                back = _from_head(h)
                for part in _split3(ot_ref[h, :, rows].T):
                    o = o + _dot(part, back)
            sg, _ = _silu_and_grad(g_buf[rows, :])
            o_ref[rows, :] = o
            y_ref[rows, :] = (o * sg).astype(BF16)
            return c

        lax.fori_loop(0, s // CHUNK, chunk, 0)

    return pl.pallas_call(
        body, name="bgate_fwd", in_specs=[ANY, VMEM], out_specs=[VMEM, VMEM],
        out_shape=[jax.ShapeDtypeStruct((s, BR), BF16), jax.ShapeDtypeStruct((s, BR), F32)],
        scratch_shapes=[pltpu.VMEM((s, BR), F32), pltpu.SemaphoreType.DMA((1,))], compiler_params=_params())(proj, ot)


def _bgate_bwd(proj, dproj, o, dy):
    s = proj.shape[0]

    def body(proj_ref, dproj_in, o_ref, dy_ref, dproj_ref, do_ref, dd_ref, g_buf, dg_buf, sems):
        del dproj_in
        _copy_all([(_window(proj_ref, BG0, BR), g_buf)], sems)
        lane = lax.broadcasted_iota(jnp.int32, (BR, CHUNK), 0) // HEAD_DIM
        col = lax.broadcasted_iota(jnp.int32, (BR, CHUNK), 1)
        sel = (lane == col).astype(BF16)

        def chunk(i, c):
            rows = pl.ds(pl.multiple_of(i * CHUNK, CHUNK), CHUNK)
            sg, dsg = _silu_and_grad(g_buf[rows, :])
            dyv = dy_ref[rows, :]
            ov = o_ref[rows, :]
            do = dyv * sg
            dg_buf[rows, :] = (dyv * ov * dsg).astype(BF16)
            prod = _split3(do * ov)
            ddt = (_dot(prod[0], sel) + _dot(prod[1], sel) + _dot(prod[2], sel)).T
            dob = do.astype(BF16)
            for h in range(N_HEADS):
                do_ref[h, rows, :] = _dot(dob, _to_head(h)).astype(BF16)
                dd_ref[h, :, rows] = jnp.broadcast_to(ddt[h:h + 1, :], (STAT_ROWS, CHUNK))
            return c

        lax.fori_loop(0, s // CHUNK, chunk, 0)
        _copy_all([(dg_buf, _window(dproj_ref, BG0, BR))], sems)

    return pl.pallas_call(
        body, name="bgate_bwd", in_specs=[ANY, ANY, VMEM, VMEM], out_specs=[ANY, VMEM, VMEM],
        out_shape=[jax.ShapeDtypeStruct(dproj.shape, dproj.dtype), jax.ShapeDtypeStruct((N_HEADS, s, EXT), BF16),
                   jax.ShapeDtypeStruct((N_HEADS, STAT_ROWS, s), F32)],
        scratch_shapes=[pltpu.VMEM((s, BR), F32), pltpu.VMEM((s, BR), BF16), pltpu.SemaphoreType.DMA((1,))],
        input_output_aliases={1: 0}, compiler_params=_params())(proj, dproj, o, dy)


C_PAD = 8


def _mix_c_fwd(proj, w):
    s = proj.shape[0]

    def body(proj_ref, w_ref, y_ref, cin_buf, g_buf, z_buf, sems):
        _copy_all([(_window(proj_ref, CIN0, 3 * BR), cin_buf), (_window(proj_ref, CG0, BR), g_buf)], sems)
        z_buf[pl.ds(0, C_PAD), :] = jnp.zeros((C_PAD, BR), F32)

        def fill(i, c):
            rows = pl.ds(pl.multiple_of(i * CHUNK, CHUNK), CHUNK)
            z_buf[pl.ds(pl.multiple_of(i * CHUNK + C_PAD, 8), CHUNK), :] = cin_buf[rows, BR:2 * BR] * cin_buf[rows, 2 * BR:]
            return c

        lax.fori_loop(0, s // CHUNK, fill, 0)

        def chunk(i, c):
            r0 = pl.multiple_of(i * CHUNK, CHUNK)
            rows = pl.ds(r0, CHUNK)
            ze = z_buf[pl.ds(r0, CHUNK + C_PAD), :]
            conv = jnp.zeros((CHUNK, BR), F32)
            for k in range(K_SHORT):
                off = C_PAD - (K_SHORT - 1) + k
                conv = conv + w_ref[k:k + 1, :] * ze[off:off + CHUNK]
            sg, _ = _silu_and_grad(g_buf[rows, :])
            y_ref[rows, :] = (cin_buf[rows, 0:BR] * conv * sg).astype(BF16)
            return c

        lax.fori_loop(0, s // CHUNK, chunk, 0)

    return pl.pallas_call(
        body, name="mix_c_fwd", in_specs=[ANY, VMEM], out_specs=VMEM, out_shape=jax.ShapeDtypeStruct((s, BR), BF16),
        scratch_shapes=[pltpu.VMEM((s, 3 * BR), F32), pltpu.VMEM((s, BR), F32), pltpu.VMEM((s + C_PAD, BR), F32),
                        pltpu.SemaphoreType.DMA((2,))],
        compiler_params=_params())(proj, w)


def _mix_c_bwd(proj, dproj, dy, w):
    s = proj.shape[0]

    def body(proj_ref, dproj_in, dy_ref, w_ref, dproj_ref, dw_ref, cin_buf, g_buf, z_buf, dc_buf, dcin_buf, dg_buf, acc, sems):
        del dproj_in
        _copy_all([(_window(proj_ref, CIN0, 3 * BR), cin_buf), (_window(proj_ref, CG0, BR), g_buf)], sems)
        z_buf[pl.ds(0, C_PAD), :] = jnp.zeros((C_PAD, BR), F32)
        dc_buf[pl.ds(s, C_PAD), :] = jnp.zeros((C_PAD, BR), F32)
        acc[...] = jnp.zeros_like(acc)

        def fill(i, c):
            rows = pl.ds(pl.multiple_of(i * CHUNK, CHUNK), CHUNK)
            z_buf[pl.ds(pl.multiple_of(i * CHUNK + C_PAD, 8), CHUNK), :] = cin_buf[rows, BR:2 * BR] * cin_buf[rows, 2 * BR:]
            return c

        lax.fori_loop(0, s // CHUNK, fill, 0)

        def chunk(i, c):
            r0 = pl.multiple_of(i * CHUNK, CHUNK)
            rows = pl.ds(r0, CHUNK)
            ze = z_buf[pl.ds(r0, CHUNK + C_PAD), :]
            taps = [ze[C_PAD - (K_SHORT - 1) + k:C_PAD - (K_SHORT - 1) + k + CHUNK] for k in range(K_SHORT)]
            conv = jnp.zeros((CHUNK, BR), F32)
            for k in range(K_SHORT):
                conv = conv + w_ref[k:k + 1, :] * taps[k]
            sg, dsg = _silu_and_grad(g_buf[rows, :])
            bg = cin_buf[rows, 0:BR]
            dyv = dy_ref[rows, :]
            dconv = dyv * bg * sg
            dc_buf[rows, :] = dconv
            dcin_buf[rows, 0:BR] = (dyv * conv * sg).astype(BF16)
            dg_buf[rows, :] = (dyv * bg * conv * dsg).astype(BF16)
            for k in range(K_SHORT):
                acc[k] += _part8(dconv * taps[k])
            return c

        lax.fori_loop(0, s // CHUNK, chunk, 0)

        def chunk2(i, c):
            r0 = pl.multiple_of(i * CHUNK, CHUNK)
            rows = pl.ds(r0, CHUNK)
            de = dc_buf[pl.ds(r0, CHUNK + C_PAD), :]
            dz = jnp.zeros((CHUNK, BR), F32)
            for k in range(K_SHORT):
                off = K_SHORT - 1 - k
                dz = dz + w_ref[k:k + 1, :] * de[off:off + CHUNK]
            dcin_buf[rows, BR:2 * BR] = (dz * cin_buf[rows, 2 * BR:]).astype(BF16)
            dcin_buf[rows, 2 * BR:] = (dz * cin_buf[rows, BR:2 * BR]).astype(BF16)
            return c

        lax.fori_loop(0, s // CHUNK, chunk2, 0)
        dw_ref[...] = jnp.zeros_like(dw_ref)
        for k in range(K_SHORT):
            dw_ref[k:k + 1, :] = jnp.sum(acc[k], axis=0, keepdims=True)
        _copy_all([(dcin_buf, _window(dproj_ref, CIN0, 3 * BR)), (dg_buf, _window(dproj_ref, CG0, BR))], sems)

    return pl.pallas_call(
        body, name="mix_c_bwd", in_specs=[ANY, ANY, VMEM, VMEM], out_specs=[ANY, VMEM],
        out_shape=[jax.ShapeDtypeStruct(dproj.shape, dproj.dtype), jax.ShapeDtypeStruct((8, BR), F32)],
        scratch_shapes=[pltpu.VMEM((s, 3 * BR), F32), pltpu.VMEM((s, BR), F32), pltpu.VMEM((s + C_PAD, BR), F32),
                        pltpu.VMEM((s + C_PAD, BR), F32), pltpu.VMEM((s, 3 * BR), BF16), pltpu.VMEM((s, BR), BF16),
                        pltpu.VMEM((K_SHORT, 8, BR), F32), pltpu.SemaphoreType.DMA((2,))],
        input_output_aliases={1: 0}, compiler_params=_params())(proj, dproj, dy, w)


D_PAD = 32


def _mix_d_common(cin_ref, g_ref, w_ref, b_ref, lg_ref, lb_ref, hh_buf, r0):
    he = hh_buf[pl.ds(r0, CHUNK + D_PAD), :]
    taps = [he[D_PAD - (K_CONF - 1) + k:D_PAD - (K_CONF - 1) + k + CHUNK] for k in range(K_CONF)]
    conv = jnp.zeros((CHUNK, BR), F32) + b_ref[...]
    for k in range(K_CONF):
        conv = conv + w_ref[k:k + 1, :] * taps[k]
    xh, rs = _ln_fwd(conv)
    sw, dsw = _silu_and_grad(xh * lg_ref[...] + lb_ref[...])
    sg, dsg = _silu_and_grad(g_ref[pl.ds(r0, CHUNK), :])
    return taps, xh, rs, sw, dsw, sg, dsg


def _mix_d_fill(cin_buf, hh_buf, s):
    hh_buf[pl.ds(0, D_PAD), :] = jnp.zeros((D_PAD, BR), F32)

    def fill(i, c):
        rows = pl.ds(pl.multiple_of(i * CHUNK, CHUNK), CHUNK)
        hh_buf[pl.ds(pl.multiple_of(i * CHUNK + D_PAD, 8), CHUNK), :] = cin_buf[rows, 0:BR] * _sigmoid(cin_buf[rows, BR:])
        return c

    lax.fori_loop(0, s // CHUNK, fill, 0)


def _mix_d_fwd(proj, w, b, lg, lb):
    s = proj.shape[0]

    def body(proj_ref, w_ref, b_ref, lg_ref, lb_ref, y_ref, cin_buf, g_buf, hh_buf, sems):
        _copy_all([(_window(proj_ref, DGLU0, 2 * BR), cin_buf), (_window(proj_ref, DG0, BR), g_buf)], sems)
        _mix_d_fill(cin_buf, hh_buf, s)

        def chunk(i, c):
            r0 = pl.multiple_of(i * CHUNK, CHUNK)
            _, _, _, sw, _, sg, _ = _mix_d_common(cin_buf, g_buf, w_ref, b_ref, lg_ref, lb_ref, hh_buf, r0)
            y_ref[pl.ds(r0, CHUNK), :] = (sw * sg).astype(BF16)
            return c

        lax.fori_loop(0, s // CHUNK, chunk, 0)

    return pl.pallas_call(
        body, name="mix_d_fwd", in_specs=[ANY, VMEM, VMEM, VMEM, VMEM], out_specs=VMEM,
        out_shape=jax.ShapeDtypeStruct((s, BR), BF16),
        scratch_shapes=[pltpu.VMEM((s, 2 * BR), F32), pltpu.VMEM((s, BR), F32), pltpu.VMEM((s + D_PAD, BR), F32),
                        pltpu.SemaphoreType.DMA((2,))],
        compiler_params=_params())(proj, w, b, lg, lb)


def _mix_d_bwd(proj, dproj, dy, w, b, lg, lb):
    s = proj.shape[0]

    def body(proj_ref, dproj_in, dy_ref, w_ref, b_ref, lg_ref, lb_ref, dproj_ref, dw_ref, db_ref, dlg_ref, dlb_ref,
             cin_buf, g_buf, hh_buf, dc_buf, dcin_buf, dg_buf, acc_w, acc_s, sems):
        del dproj_in
        _copy_all([(_window(proj_ref, DGLU0, 2 * BR), cin_buf), (_window(proj_ref, DG0, BR), g_buf)], sems)
        _mix_d_fill(cin_buf, hh_buf, s)
        dc_buf[pl.ds(s, D_PAD), :] = jnp.zeros((D_PAD, BR), F32)
        acc_w[...] = jnp.zeros_like(acc_w)
        acc_s[...] = jnp.zeros_like(acc_s)

        def chunk(i, c):
            r0 = pl.multiple_of(i * CHUNK, CHUNK)
            rows = pl.ds(r0, CHUNK)
            taps, xh, rs, sw, dsw, sg, dsg = _mix_d_common(cin_buf, g_buf, w_ref, b_ref, lg_ref, lb_ref, hh_buf, r0)
            dyv = dy_ref[rows, :]
            dg_buf[rows, :] = (dyv * sw * dsg).astype(BF16)
            d_ln = dyv * sg * dsw
            acc_s[0] += _part8(d_ln * xh)
            acc_s[1] += _part8(d_ln)
            dc = _ln_bwd(d_ln * lg_ref[...], xh, rs)
            acc_s[2] += _part8(dc)
            dc_buf[rows, :] = dc
            for k in range(K_CONF):
                acc_w[k] += _part8(dc * taps[k])
            return c

        lax.fori_loop(0, s // CHUNK, chunk, 0)

        def chunk2(i, c):
            r0 = pl.multiple_of(i * CHUNK, CHUNK)
            rows = pl.ds(r0, CHUNK)
            de = dc_buf[pl.ds(r0, CHUNK + D_PAD), :]
            dh = jnp.zeros((CHUNK, BR), F32)
            for k in range(K_CONF):
                off = K_CONF - 1 - k
                dh = dh + w_ref[k:k + 1, :] * de[off:off + CHUNK]
            a = cin_buf[rows, 0:BR]
            sig = _sigmoid(cin_buf[rows, BR:])
            dcin_buf[rows, 0:BR] = (dh * sig).astype(BF16)
            dcin_buf[rows, BR:] = (dh * a * sig * (1.0 - sig)).astype(BF16)
            return c

        lax.fori_loop(0, s // CHUNK, chunk2, 0)
        dw_ref[...] = jnp.zeros_like(dw_ref)
        for k in range(K_CONF):
            dw_ref[k:k + 1, :] = jnp.sum(acc_w[k], axis=0, keepdims=True)
        dlg_ref[...] = jnp.sum(acc_s[0], axis=0, keepdims=True)
        dlb_ref[...] = jnp.sum(acc_s[1], axis=0, keepdims=True)
        db_ref[...] = jnp.sum(acc_s[2], axis=0, keepdims=True)
        _copy_all([(dcin_buf, _window(dproj_ref, DGLU0, 2 * BR)), (dg_buf, _window(dproj_ref, DG0, BR))], sems)

    vec = jax.ShapeDtypeStruct((1, BR), F32)
    return pl.pallas_call(
        body, name="mix_d_bwd", in_specs=[ANY, ANY, VMEM, VMEM, VMEM, VMEM, VMEM], out_specs=[ANY, VMEM, VMEM, VMEM, VMEM],
        out_shape=[jax.ShapeDtypeStruct(dproj.shape, dproj.dtype), jax.ShapeDtypeStruct((32, BR), F32), vec, vec, vec],
        scratch_shapes=[pltpu.VMEM((s, 2 * BR), F32), pltpu.VMEM((s, BR), F32), pltpu.VMEM((s + D_PAD, BR), F32),
                        pltpu.VMEM((s + D_PAD, BR), F32), pltpu.VMEM((s, 2 * BR), BF16), pltpu.VMEM((s, BR), BF16),
                        pltpu.VMEM((K_CONF, 8, BR), F32), pltpu.VMEM((3, 8, BR), F32), pltpu.SemaphoreType.DMA((2,))],
        input_output_aliases={1: 0}, compiler_params=_params())(proj, dproj, dy, w, b, lg, lb)


MERGE_TM = 256


def _merge_fwd(x, proj, ys, wb, wo):
    s = x.shape[0]
    tm = min(MERGE_TM, s)

    def body(x_ref, lg_ref, ya, yb, yc, yd, wb_ref, wo_ref, xn_ref, mg_ref):
        merged = jnp.zeros((tm, D_MODEL), F32)
        for n, y_ref in enumerate((ya, yb, yc, yd)):
            gate = _sigmoid(lg_ref[:, n * D_MODEL:(n + 1) * D_MODEL])
            merged = merged + gate * _dot(y_ref[...], wb_ref[n])
        mb = merged.astype(BF16)
        mg_ref[...] = mb
        xn_ref[...] = x_ref[...] + _dot(mb, wo_ref[...])

    row = lambda w: pl.BlockSpec((tm, w), lambda i: (i, 0))
    return pl.pallas_call(
        body, name="merge_fwd", grid=(s // tm,),
        in_specs=[row(D_MODEL), pl.BlockSpec((tm, MRG0), lambda i: (i, 1)), row(BR), row(BR), row(BR), row(BR),
                  pl.BlockSpec((N_HEADS, BR, D_MODEL), lambda i: (0, 0, 0)), pl.BlockSpec((D_MODEL, D_MODEL), lambda i: (0, 0))],
        out_specs=[row(D_MODEL), row(D_MODEL)],
        out_shape=[jax.ShapeDtypeStruct((s, D_MODEL), F32), jax.ShapeDtypeStruct((s, D_MODEL), BF16)],
        compiler_params=_params("arbitrary"))(x, proj, *ys, wb, wo)


def _merge_bwd(dxn, proj, ys, merged, wb, wo):
    s = dxn.shape[0]
    tm = min(MERGE_TM, s)

    def body(dx_ref, lg_ref, ya, yb, yc, yd, mg_ref, wb_ref, wo_ref, dlg_ref, da, db, dc, dd, dwo_ref, dwb_ref):
        @pl.when(pl.program_id(0) == 0)
        def _():
            dwo_ref[...] = jnp.zeros_like(dwo_ref)
            dwb_ref[...] = jnp.zeros_like(dwb_ref)

        dxb = dx_ref[...].astype(BF16)
        d_merged = _dot_nt(dxb, wo_ref[...])
        dwo_ref[...] += _dot_tn(mg_ref[...], dxb)
        for n, (y_ref, dy_ref) in enumerate(((ya, da), (yb, db), (yc, dc), (yd, dd))):
            yv = y_ref[...]
            gate = _sigmoid(lg_ref[:, n * D_MODEL:(n + 1) * D_MODEL])
            pn = _dot(yv, wb_ref[n])
            dlg_ref[:, n * D_MODEL:(n + 1) * D_MODEL] = (d_merged * pn * gate * (1.0 - gate)).astype(BF16)
            dpn = (d_merged * gate).astype(BF16)
            dy_ref[...] = _dot_nt(dpn, wb_ref[n])
            dwb_ref[n] += _dot_tn(yv, dpn)

    row = lambda w: pl.BlockSpec((tm, w), lambda i: (i, 0))
    wb_spec = pl.BlockSpec((N_HEADS, BR, D_MODEL), lambda i: (0, 0, 0))
    wo_spec = pl.BlockSpec((D_MODEL, D_MODEL), lambda i: (0, 0))
    dy_shape = jax.ShapeDtypeStruct((s, BR), F32)
    outs = pl.pallas_call(
        body, name="merge_bwd", grid=(s // tm,),
        in_specs=[row(D_MODEL), pl.BlockSpec((tm, MRG0), lambda i: (i, 1)), row(BR), row(BR), row(BR), row(BR), row(D_MODEL),
                  wb_spec, wo_spec],
        out_specs=[pl.BlockSpec((tm, MRG0), lambda i: (i, 1)), row(BR), row(BR), row(BR), row(BR), wo_spec, wb_spec],
        out_shape=[jax.ShapeDtypeStruct((s, PCOLS), BF16), dy_shape, dy_shape, dy_shape, dy_shape,
                   jax.ShapeDtypeStruct((D_MODEL, D_MODEL), F32), jax.ShapeDtypeStruct((N_HEADS, BR, D_MODEL), F32)],
        compiler_params=_params("arbitrary"))(dxn, proj, *ys, merged, wb, wo)
    return outs[0], outs[1:5], outs[5], outs[6]


def _loss_head(x, target, g):
    s = x.shape[0]
    tm = min(512, s)

    def body(x_ref, t_ref, g_ref, loss_ref, dx_ref, dg_ref):
        @pl.when(pl.program_id(0) == 0)
        def _():
            loss_ref[...] = jnp.zeros_like(loss_ref)
            dg_ref[...] = jnp.zeros_like(dg_ref)

        xv = x_ref[...]
        gv = g_ref[...]
        r = lax.rsqrt(jnp.mean(xv * xv, axis=-1, keepdims=True) + EPS)
        xn = xv * r
        err = xn * gv - t_ref[...]
        loss_ref[...] += 0.5 * jnp.sum(jnp.mean(err * err, axis=-1, keepdims=True))
        dy = err * (1.0 / D_MODEL)
        dg_ref[...] += _part8(dy * xn)
        gy = dy * gv
        dx_ref[...] = r * (gy - xn * jnp.mean(xn * gy, axis=-1, keepdims=True))

    row = pl.BlockSpec((tm, D_MODEL), lambda i: (i, 0))
    return pl.pallas_call(
        body, name="loss_head", grid=(s // tm,),
        in_specs=[row, row, pl.BlockSpec((1, D_MODEL), lambda i: (0, 0))],
        out_specs=[pl.BlockSpec((8, 128), lambda i: (0, 0)), row, pl.BlockSpec((8, D_MODEL), lambda i: (0, 0))],
        out_shape=[jax.ShapeDtypeStruct((8, 128), F32), jax.ShapeDtypeStruct((s, D_MODEL), F32), jax.ShapeDtypeStruct((8, D_MODEL), F32)],
        compiler_params=_params("arbitrary"))(x, target, g)


_SEGMENTS = ((0, 512, AUV0), (512, 256, AG0), (768, 768, QKV0), (1536, 4, BF0), (1540, 256, BG0), (1796, 768, CIN0),
             (2564, 256, CG0), (2820, 512, DGLU0), (3332, 256, DG0), (3588, 4096, MRG0))


def _to_aligned_cols(w):
    out, pos = [], 0
    for src, width, dst in sorted(_SEGMENTS, key=lambda t: t[2]):
        if dst > pos:
            out.append(jnp.zeros(w.shape[:-1] + (dst - pos,), w.dtype))
        out.append(w[..., src:src + width])
        pos = dst + width
    return jnp.concatenate(out, axis=-1)


def _from_aligned_cols(g):
    return jnp.concatenate([g[..., dst:dst + width] for _, width, dst in _SEGMENTS], axis=-1)


def _sgu_bias_rows(sgu_b):
    return jnp.repeat(sgu_b.T, HEAD_DIM, axis=1)


def _layer_fwd(x, p):
    proj, h = _inproj_fwd(x, p["norm_g"], p["w_in"])
    ya = _mix_a_fwd(proj, p["sgu_w"], _sgu_bias_rows(p["sgu_b"]), p["sgu_ln_g"], p["sgu_ln_b"])
    qe, ke, ve, ket, vet = _attn_prep_fwd(proj, p["f_bias"])
    ot, lse = _attn_fwd(qe, ke, vet)
    yb, o = _bgate_fwd(proj, ot)
    yc = _mix_c_fwd(proj, p["short_conv_w"])
    yd = _mix_d_fwd(proj, p["conf_dw_w"], p["conf_dw_b"], p["conf_ln_g"], p["conf_ln_b"])
    ys = (ya, yb, yc, yd)
    x_next, merged = _merge_fwd(x, proj, ys, p["w_branch"], p["w_out"])
    saved = dict(x=x, proj=proj, h=h, ys=ys, merged=merged, qe=qe, ke=ke, ve=ve, ket=ket, lse=lse, o=o)
    return x_next, saved


def _layer_bwd(dxn, p, sv):
    proj = sv["proj"]
    dproj, dys, d_wo, d_wb = _merge_bwd(dxn, proj, sv["ys"], sv["merged"], p["w_branch"], p["w_out"])
    dproj, d_sgu_w, d_bias_rows, d_sgu_lg, d_sgu_lb = _mix_a_bwd(
        proj, dproj, dys[0], p["sgu_w"], _sgu_bias_rows(p["sgu_b"]), p["sgu_ln_g"], p["sgu_ln_b"])
    dproj, doe, dd = _bgate_bwd(proj, dproj, sv["o"], dys[1])
    dqe, dke, dve = _attn_bwd(sv["qe"], sv["ke"], sv["ket"], sv["ve"], doe, sv["lse"], dd)
    dproj, d_fb = _attn_prep_bwd(proj, dproj, dqe, dke, dve, p["f_bias"])
    dproj, d_sc = _mix_c_bwd(proj, dproj, dys[2], p["short_conv_w"])
    dproj, d_cw, d_cb, d_clg, d_clb = _mix_d_bwd(proj, dproj, dys[3], p["conf_dw_w"], p["conf_dw_b"], p["conf_ln_g"], p["conf_ln_b"])
    dx, dg8 = _inproj_bwd_x(dproj, p["w_in"], sv["x"], dxn, p["norm_g"])
    d_win = _inproj_bwd_w(sv["h"], dproj)
    grads = dict(
        norm_g=jnp.sum(dg8, axis=0), w_in=d_win, f_bias=d_fb[0, :N_HEADS], sgu_w=d_sgu_w,
        sgu_b=d_bias_rows.reshape(CHUNK, N_HEADS, HEAD_DIM).sum(axis=-1).T,
        sgu_ln_g=d_sgu_lg[0], sgu_ln_b=d_sgu_lb[0], short_conv_w=d_sc[:K_SHORT], conf_dw_w=d_cw[:K_CONF],
        conf_dw_b=d_cb[0], conf_ln_g=d_clg[0], conf_ln_b=d_clb[0], w_branch=d_wb, w_out=d_wo)
    return dx, grads


def _local_step(x, target, layers, final_g):
    saved = []
    for p in layers:
        x, sv = _layer_fwd(x, p)
        saved.append(sv)
    loss8, dx, dfg8 = _loss_head(x, target, final_g)
    grads = [None] * len(layers)
    for l in reversed(range(len(layers))):
        dx, grads[l] = _layer_bwd(dx, layers[l], saved[l])
    return loss8[0, 0], dx, grads, jnp.sum(dfg8, axis=0)


def _place():
    return lax.axis_index("x"), lax.axis_index("y"), lax.axis_index("c")


def _other_chips(x, y):
    return [(1 - x, y), (x, 1 - y), (1 - x, 1 - y)]


def _gather_chips(blocks):
    n = len(blocks)

    def body(*refs):
        ins, outs, (send_sems, recv_sems) = refs[:n], refs[n:2 * n], refs[2 * n:]
        x, y, cc = _place()
        me, sibling = (x, y, cc), (x, y, 1 - cc)
        chips = _other_chips(x, y)

        def copy(a, k, chip, layer, to, src=None):
            dst = outs[a].at[2 * chip[0] + chip[1], layer]
            return pltpu.make_async_remote_copy(src_ref=dst if src is None else src, dst_ref=dst, send_sem=send_sems.at[a, k],
                                                recv_sem=recv_sems.at[a, k], device_id=to, device_id_type=MESH)

        first = [copy(a, j, (x, y), cc, (*chip, cc), src=ins[a]) for j, chip in enumerate(chips) for a in range(n)]
        for cp in first:
            cp.start()
        passed = []
        for j, chip in enumerate(chips):
            for a in range(n):
                copy(a, j, chip, cc, me).wait_recv()
                passed.append(copy(a, 3 + j, chip, cc, sibling))
                passed[-1].start()
        for j, chip in enumerate(chips):
            for a in range(n):
                copy(a, 3 + j, chip, 1 - cc, me).wait_recv()
        for cp in first + passed:
            cp.wait_send()

    return pl.pallas_call(
        body, name="gather_chips", in_specs=[ANY] * n, out_specs=[ANY] * n,
        out_shape=[jax.ShapeDtypeStruct((N_CHIPS, 2) + b.shape, b.dtype) for b in blocks],
        scratch_shapes=[pltpu.SemaphoreType.DMA((n, 6)), pltpu.SemaphoreType.DMA((n, 6))])(*blocks)


def _pair_exchange(arrays):
    n = len(arrays)

    def body(*refs):
        ins, outs, (send_sems, recv_sems) = refs[:n], refs[n:2 * n], refs[2 * n:]
        x, y, cc = _place()
        cps = [pltpu.make_async_remote_copy(src_ref=ins[a], dst_ref=outs[a], send_sem=send_sems.at[a], recv_sem=recv_sems.at[a],
                                            device_id=(x, y, 1 - cc), device_id_type=MESH) for a in range(n)]
        for cp in cps:
            cp.start()
        for cp in cps:
            cp.wait()

    return pl.pallas_call(
        body, name="pair_exchange", in_specs=[ANY] * n, out_specs=[ANY] * n,
        out_shape=[jax.ShapeDtypeStruct(a.shape, a.dtype) for a in arrays],
        scratch_shapes=[pltpu.SemaphoreType.DMA((n,)), pltpu.SemaphoreType.DMA((n,))])(*arrays)


def _chip_exchange(arrays):
    n = len(arrays)

    def body(*refs):
        ins, outs, (send_sems, recv_sems) = refs[:n], refs[n:2 * n], refs[2 * n:]
        x, y, cc = _place()
        cps = [pltpu.make_async_remote_copy(src_ref=ins[a].at[2 * px + py], dst_ref=outs[a].at[k], send_sem=send_sems.at[a, k],
                                            recv_sem=recv_sems.at[a, k], device_id=(px, py, cc), device_id_type=MESH)
               for k, (px, py) in enumerate(_other_chips(x, y)) for a in range(n)]
        for cp in cps:
            cp.start()
        for cp in cps:
            cp.wait()

    return pl.pallas_call(
        body, name="chip_exchange", in_specs=[ANY] * n, out_specs=[ANY] * n,
        out_shape=[jax.ShapeDtypeStruct((3,) + a.shape[1:], a.dtype) for a in arrays],
        scratch_shapes=[pltpu.SemaphoreType.DMA((n, 3)), pltpu.SemaphoreType.DMA((n, 3))])(*arrays)


def _allreduce8(buf):
    r, c = buf.shape

    def body(x_ref, o_ref, recv_buf, send_sems, recv_sems):
        x, y, cc = _place()
        o_ref[...] = x_ref[...]
        for k, peer in enumerate([(x, y, 1 - cc), (1 - x, y, cc), (x, 1 - y, cc)]):
            cp = pltpu.make_async_remote_copy(src_ref=o_ref, dst_ref=recv_buf.at[k], send_sem=send_sems.at[k],
                                              recv_sem=recv_sems.at[k], device_id=peer, device_id_type=MESH)
            cp.start()
            cp.wait()
            o_ref[...] = o_ref[...] + recv_buf[k]

    return pl.pallas_call(
        body, name="allreduce8", in_specs=[VMEM], out_specs=VMEM, out_shape=jax.ShapeDtypeStruct((r, c), F32),
        scratch_shapes=[pltpu.VMEM((3, r, c), F32), pltpu.SemaphoreType.DMA((3,)), pltpu.SemaphoreType.DMA((3,))],
        compiler_params=_params())(buf)


def _row_tile(rows, cols, limit_bytes=1 << 20):
    t = rows
    while t % 16 == 0 and t * cols * 4 > limit_bytes:
        t //= 2
    return t


def _add_pair(a, b):
    r, c = a.shape
    t = _row_tile(r, c)

    def body(a_ref, b_ref, o_ref):
        o_ref[...] = (a_ref[...].astype(F32) + b_ref[...].astype(F32)).astype(o_ref.dtype)

    spec = pl.BlockSpec((t, c), lambda i: (i, 0))
    return pl.pallas_call(body, name="add_pair", grid=(r // t,), in_specs=[spec, spec], out_specs=spec,
                          out_shape=jax.ShapeDtypeStruct((r, c), a.dtype), compiler_params=_params("arbitrary"))(a, b)


def _sum_chips(own, recv):
    r, c = own.shape
    t = _row_tile(r, c)

    def body(o_ref, r_ref, out_ref):
        acc = o_ref[...].astype(F32)
        for k in range(3):
            acc = acc + r_ref[k].astype(F32)
        out_ref[...] = acc

    return pl.pallas_call(
        body, name="sum_chips", grid=(r // t,),
        in_specs=[pl.BlockSpec((t, c), lambda i: (i, 0)), pl.BlockSpec((3, t, c), lambda i: (0, i, 0))],
        out_specs=pl.BlockSpec((t, c), lambda i: (i, 0)), out_shape=jax.ShapeDtypeStruct((r, c), F32),
        compiler_params=_params("arbitrary"))(own, recv)


def _adamw(w, m, v, g):
    r, c = w.shape
    t = _row_tile(r, c)

    def body(w_ref, m_ref, v_ref, g_ref, d_ref, mo_ref, vo_ref):
        gv = g_ref[...]
        mn = ADAM_B1 * m_ref[...] + (1.0 - ADAM_B1) * gv
        vn = ADAM_B2 * v_ref[...] + (1.0 - ADAM_B2) * (gv * gv)
        m_hat = mn / (1.0 - ADAM_B1 ** ADAM_STEP)
        v_hat = vn / (1.0 - ADAM_B2 ** ADAM_STEP)
        d_ref[...] = -ADAM_LR * (m_hat / (jnp.sqrt(v_hat) + ADAM_EPS) + ADAM_WD * w_ref[...])
        mo_ref[...] = mn
        vo_ref[...] = vn

    spec = pl.BlockSpec((t, c), lambda i: (i, 0))
    shape = jax.ShapeDtypeStruct((r, c), F32)
    return pl.pallas_call(body, name="adamw", grid=(r // t,), in_specs=[spec] * 4, out_specs=[spec] * 3,
                          out_shape=[shape] * 3, compiler_params=_params("arbitrary"))(w, m, v, g)


SMALL =("norm_g", "f_bias", "sgu_w", "sgu_b", "sgu_ln_g", "sgu_ln_b", "short_conv_w", "conf_dw_w", "conf_dw_b",
         "conf_ln_g", "conf_ln_b", "final_g")
WEIGHTS = ("norm_g", "w_in", "f_bias", "sgu_w", "sgu_b", "sgu_ln_g", "sgu_ln_b", "short_conv_w", "conf_dw_w",
           "conf_dw_b", "conf_ln_g", "conf_ln_b", "w_branch", "w_out", "final_g")


def _pack(arrays):
    flat = jnp.concatenate([a.reshape(-1) for a in arrays])
    n = flat.shape[0]
    pad = (-n) % 1024
    return jnp.pad(flat, (0, pad)).reshape(-1, 128)


def _unpack(buf, shapes):
    flat, out, pos = buf.reshape(-1), [], 0
    for shp in shapes:
        n = math.prod(shp)
        out.append(flat[pos:pos + n].reshape(shp))
        pos += n
    return out


def _aligned_pieces():
    out, pos = [], 0
    for src, width, dst in sorted(_SEGMENTS, key=lambda t: t[2]):
        if dst > pos:
            out.append((None, 0, dst - pos))
        lo = src
        while lo < src + width:
            j = lo // SHARD_COLS
            hi = min(src + width, (j + 1) * SHARD_COLS)
            out.append((j, lo - j * SHARD_COLS, hi - lo))
            lo = hi
        pos = dst + width
    return out


def _shard_pieces(j):
    lo_s, hi_s = j * SHARD_COLS, (j + 1) * SHARD_COLS
    out = []
    for src, width, dst in _SEGMENTS:
        lo, hi = max(src, lo_s), min(src + width, hi_s)
        if lo < hi:
            out.append((dst + lo - src, hi - lo))
    return out


def _w_in_aligned(shards):
    rows, dtype = shards[0].shape[0], shards[0].dtype
    return jnp.concatenate([jnp.zeros((rows, w), dtype) if j is None else shards[j][:, c0:c0 + w]
                            for j, c0, w in _aligned_pieces()], axis=1)


def _w_in_shard(g, j):
    parts = [g[:, c0:c0 + w] for c0, w in _shard_pieces(j)]
    return jnp.concatenate(parts + [jnp.zeros((g.shape[0], SHARD_PAD - SHARD_COLS), g.dtype)], axis=1)


def _by_layer(pc, own, other):
    return jnp.where(pc == 0, jnp.stack([own, other]), jnp.stack([other, own]))


def kernel(x, norm_g, w_in, f_bias, sgu_w, sgu_b, sgu_ln_g, sgu_ln_b, short_conv_w, conf_dw_w, conf_dw_b, conf_ln_g, conf_ln_b, w_branch, w_out, final_g, loss_target, m_norm_g, m_w_in, m_f_bias, m_sgu_w, m_sgu_b, m_sgu_ln_g, m_sgu_ln_b, m_short_conv_w, m_conf_dw_w, m_conf_dw_b, m_conf_ln_g, m_conf_ln_b, m_w_branch, m_w_out, m_final_g, v_norm_g, v_w_in, v_f_bias, v_sgu_w, v_sgu_b, v_sgu_ln_g, v_sgu_ln_b, v_short_conv_w, v_conf_dw_w, v_conf_dw_b, v_conf_ln_g, v_conf_ln_b, v_w_branch, v_w_out, v_final_g):
    px, py, pc = _place()
    chip = 2 * px + py
    depth = w_in.shape[0]
    w = dict(norm_g=norm_g, w_in=w_in, f_bias=f_bias, sgu_w=sgu_w, sgu_b=sgu_b, sgu_ln_g=sgu_ln_g, sgu_ln_b=sgu_ln_b,
             short_conv_w=short_conv_w, conf_dw_w=conf_dw_w, conf_dw_b=conf_dw_b, conf_ln_g=conf_ln_g, conf_ln_b=conf_ln_b,
             w_branch=w_branch, w_out=w_out, final_g=final_g)
    m = dict(norm_g=m_norm_g, w_in=m_w_in, f_bias=m_f_bias, sgu_w=m_sgu_w, sgu_b=m_sgu_b, sgu_ln_g=m_sgu_ln_g,
             sgu_ln_b=m_sgu_ln_b, short_conv_w=m_short_conv_w, conf_dw_w=m_conf_dw_w, conf_dw_b=m_conf_dw_b,
             conf_ln_g=m_conf_ln_g, conf_ln_b=m_conf_ln_b, w_branch=m_w_branch, w_out=m_w_out, final_g=m_final_g)
    v = dict(norm_g=v_norm_g, w_in=v_w_in, f_bias=v_f_bias, sgu_w=v_sgu_w, sgu_b=v_sgu_b, sgu_ln_g=v_sgu_ln_g,
             sgu_ln_b=v_sgu_ln_b, short_conv_w=v_short_conv_w, conf_dw_w=v_conf_dw_w, conf_dw_b=v_conf_dw_b,
             conf_ln_g=v_conf_ln_g, conf_ln_b=v_conf_ln_b, w_branch=v_w_branch, w_out=v_w_out, final_g=v_final_g)

    local = (jnp.pad(w_in, ((0, 0), (0, 0), (0, SHARD_PAD - SHARD_COLS))).astype(BF16),
             w_branch.astype(BF16).reshape(depth, N_HEADS * BR, BR), w_out.astype(BF16))
    pick = lambda a, i: lax.dynamic_index_in_dim(a, i, 0, keepdims=False)
    gathered = _gather_chips([pick(a, pc) for a in local])
    wi_all, wb_all, wo_all = [lax.dynamic_update_index_in_dim(got, mine, chip, 0) for got, mine in zip(gathered, local)]
    conv_ch = BR // N_CHIPS
    place = lambda a: lax.dynamic_update_slice_in_dim(jnp.zeros(a.shape[:-1] + (BR,), F32), a, conv_ch * chip, axis=2)
    conv_shapes = [(depth, K_SHORT, BR), (depth, K_CONF, BR)]
    conv_buf = _pack([place(short_conv_w), place(conf_dw_w)]) * (pc == 0).astype(F32)
    short_full, conf_full = _unpack(_allreduce8(conv_buf), conv_shapes)

    layers = []
    for l in range(depth):
        wb = jnp.concatenate([wb_all[j, l].reshape(N_HEADS, BR, BR) for j in range(N_CHIPS)], axis=2)
        wo = jnp.concatenate([wo_all[j, l] for j in range(N_CHIPS)], axis=0)
        layers.append(dict(
            norm_g=norm_g[l][None], w_in=_w_in_aligned([wi_all[j, l] for j in range(N_CHIPS)]),
            f_bias=jnp.pad(f_bias[l], (0, CHUNK - N_HEADS))[None],
            sgu_w=sgu_w[l], sgu_b=sgu_b[l], sgu_ln_g=sgu_ln_g[l][None], sgu_ln_b=sgu_ln_b[l][None],
            short_conv_w=short_full[l], conf_dw_w=conf_full[l], conf_dw_b=conf_dw_b[l][None],
            conf_ln_g=conf_ln_g[l][None], conf_ln_b=conf_ln_b[l][None], w_branch=wb, w_out=wo))

    loss_local, grad_x, grads, d_final_g = _local_step(x[0], loss_target[0], layers, final_g[None])
    loss = lax.psum(loss_local, ("x", "y", "c"))

    g2 = (jnp.stack([jnp.stack([_w_in_shard(grads[l]["w_in"], j) for j in range(N_CHIPS)]) for l in range(depth)]),
          jnp.stack([jnp.stack([grads[l]["w_branch"][:, :, j * BR:(j + 1) * BR].reshape(N_HEADS * BR, BR).astype(BF16)
                                for j in range(N_CHIPS)]) for l in range(depth)]),
          jnp.stack([jnp.stack([grads[l]["w_out"][j * BR:(j + 1) * BR].astype(BF16) for j in range(N_CHIPS)]) for l in range(depth)]))
    flat = lambda a: a.reshape(-1, a.shape[-1])
    from_sibling = _pair_exchange([flat(pick(a, 1 - pc)) for a in g2])
    h = [_add_pair(flat(pick(a, pc)), r).reshape(a.shape[1:]) for a, r in zip(g2, from_sibling)]
    summed = [_sum_chips(pick(a, chip), r) for a, r in zip(h, _chip_exchange(h))]
    gi, gb, go = [_by_layer(pc, own, other) for own, other in zip(summed, _pair_exchange(summed))]
    g = dict(w_in=gi[:, :, :SHARD_COLS], w_branch=gb.reshape(depth, N_HEADS, BR, BR), w_out=go)

    small_local = [jnp.stack([grads[l][n] for l in range(depth)]) for n in SMALL[:-1]] + [d_final_g]
    small_shapes = [a.shape for a in small_local]
    for n, a in zip(SMALL, _unpack(_allreduce8(_pack(small_local)), small_shapes)):
        g[n] = a
    for n in ("short_conv_w", "conf_dw_w"):
        g[n] = lax.dynamic_slice_in_dim(g[n], conv_ch * chip, conv_ch, axis=2)

    delta, new_m, new_v = {}, {}, {}
    for n in ("w_in", "w_branch", "w_out"):
        shp = w[n].shape
        two_d = lambda a: a.reshape(-1, shp[-1])
        d_, m_, v_ = _adamw(two_d(w[n]), two_d(m[n]), two_d(v[n]), two_d(g[n]))
        delta[n], new_m[n], new_v[n] = d_.reshape(shp), m_.reshape(shp), v_.reshape(shp)
    shapes = [w[n].shape for n in SMALL]
    d_, m_, v_ = _adamw(*[_pack([t[n] for n in SMALL]) for t in (w, m, v, g)])
    for n, a, b, c_ in zip(SMALL, _unpack(d_, shapes), _unpack(m_, shapes), _unpack(v_, shapes)):
        delta[n], new_m[n], new_v[n] = a, b, c_

    return (loss, grad_x[None], *[g[n] for n in WEIGHTS], *[delta[n] for n in WEIGHTS],
            *[new_m[n] for n in WEIGHTS], *[new_v[n] for n in WEIGHTS])
```

```python
import functools
import math

import jax
import jax.numpy as jnp
from jax import lax
from jax.experimental import pallas as pl
from jax.experimental.pallas import tpu as pltpu

F32 = jnp.float32
BF16 = jnp.bfloat16

D_MODEL = 1024
BR = 256
N_HEADS = 4
HEAD_DIM = 64
CHUNK = 128
K_SHORT = 3
K_CONF = 31
EPS = 1e-6
IN_COLS = 7684
SHARD_COLS = IN_COLS // 4
SHARD_PAD = 2048
N_CHIPS = 4

QKV0, CIN0, AUV0, DGLU0 = 0, 768, 1536, 2048
AG0, BG0, CG0, DG0 = 2560, 2816, 3072, 3328
BF0 = 3584
MRG0 = 4096
PCOLS = 8192

V7X_VMEM_BYTES = 64 * 1024 * 1024
VMEM_LIMIT = V7X_VMEM_BYTES * 7 // 8

ADAM_LR, ADAM_B1, ADAM_B2, ADAM_EPS, ADAM_WD, ADAM_STEP = 0.001, 0.9, 0.999, 1e-08, 0.01, 10

MESH = pl.DeviceIdType.MESH
ANY = pl.BlockSpec(memory_space=pl.ANY)
VMEM = pl.BlockSpec(memory_space=pltpu.VMEM)

GELU_C0 = math.sqrt(2.0 / math.pi)
GELU_C1 = 0.044715


def _params(*sem):
    return pltpu.CompilerParams(dimension_semantics=sem, vmem_limit_bytes=VMEM_LIMIT)


def _sigmoid(x):
    return 0.5 * jnp.tanh(0.5 * x) + 0.5


def _silu_and_grad(x):
    s = _sigmoid(x)
    return x * s, s * (1.0 + x * (1.0 - s))


def _gelu_and_grad(z):
    z2 = z * z
    t = jnp.tanh(GELU_C0 * (z + GELU_C1 * z2 * z))
    half = 0.5 * (1.0 + t)
    return z * half, half + 0.5 * z * (1.0 - t * t) * (GELU_C0 * (1.0 + 3.0 * GELU_C1 * z2))


def _ln_fwd(v):
    mu = jnp.mean(v, axis=-1, keepdims=True)
    xc = v - mu
    rs = lax.rsqrt(jnp.mean(xc * xc, axis=-1, keepdims=True) + EPS)
    return xc * rs, rs


def _ln_bwd(d_xh, xh, rs):
    return rs * (d_xh - jnp.mean(d_xh, axis=-1, keepdims=True) - xh * jnp.mean(d_xh * xh, axis=-1, keepdims=True))


def _part8(a):
    return a.reshape(a.shape[0] // 8, 8, a.shape[1]).sum(axis=0)


def _dot(a, b):
    return jnp.dot(a, b, preferred_element_type=F32)


def _dot_nt(a, b):
    return lax.dot_general(a, b, (((1,), (1,)), ((), ())), preferred_element_type=F32)


def _dot_tn(a, b):
    return lax.dot_general(a, b, (((0,), (0,)), ((), ())), preferred_element_type=F32)


def _head_masks(dtype):
    lane = lax.broadcasted_iota(jnp.int32, (1, BR), 1) // HEAD_DIM
    return [(lane == h).astype(dtype) for h in range(N_HEADS)]


def _window(ref, col0, width):
    return ref.at[:, pl.ds(col0, width)]


def _copy_all(pairs, sems):
    cps = [pltpu.make_async_copy(s, d, sems.at[i]) for i, (s, d) in enumerate(pairs)]
    for cp in cps:
        cp.start()
    for cp in cps:
        cp.wait()


def _inproj_fwd(x, g, w):
    s = x.shape[0]
    tm, tn = min(1024, s), 1024

    def body(x_ref, g_ref, w_ref, proj_ref, h_ref):
        @pl.when(pl.program_id(1) == 0)
        def _():
            xv = x_ref[...]
            r = lax.rsqrt(jnp.mean(xv * xv, axis=-1, keepdims=True) + EPS)
            h_ref[...] = ((xv * r) * g_ref[...]).astype(BF16)
        proj_ref[...] = _dot(h_ref[...], w_ref[...])

    return pl.pallas_call(
        body, name="inproj_fwd", grid=(s // tm, PCOLS // tn),
        in_specs=[pl.BlockSpec((tm, D_MODEL), lambda i, j: (i, 0)), pl.BlockSpec((1, D_MODEL), lambda i, j: (0, 0)),
                  pl.BlockSpec((D_MODEL, tn), lambda i, j: (0, j))],
        out_specs=[pl.BlockSpec((tm, tn), lambda i, j: (i, j)), pl.BlockSpec((tm, D_MODEL), lambda i, j: (i, 0))],
        out_shape=[jax.ShapeDtypeStruct((s, PCOLS), F32), jax.ShapeDtypeStruct((s, D_MODEL), BF16)],
        compiler_params=_params("arbitrary", "arbitrary"))(x, g, w)


def _rms_bwd(dh, x, g):
    r = lax.rsqrt(jnp.mean(x * x, axis=-1, keepdims=True) + EPS)
    xn = x * r
    gy = dh * g
    dx = r * (gy - xn * jnp.mean(xn * gy, axis=-1, keepdims=True))
    return dx, _part8(dh * xn)


def _inproj_bwd_x(dproj, w, x, dxn, g):
    s = x.shape[0]
    tm, tk = min(512, s), 4096
    nk = PCOLS // tk

    def body(dp_ref, w_ref, x_ref, dxn_ref, g_ref, dx_ref, dg_ref, acc_ref):
        i, k = pl.program_id(0), pl.program_id(1)

        @pl.when(k == 0)
        def _():
            acc_ref[...] = jnp.zeros_like(acc_ref)

        @pl.when((i == 0) & (k == 0))
        def _():
            dg_ref[...] = jnp.zeros_like(dg_ref)

        acc_ref[...] += _dot_nt(dp_ref[...], w_ref[...])

        @pl.when(k == nk - 1)
        def _():
            dx, dg8 = _rms_bwd(acc_ref[...], x_ref[...], g_ref[...])
            dx_ref[...] = dxn_ref[...] + dx
            dg_ref[...] += dg8

    dx, dg8 = pl.pallas_call(
        body, name="inproj_bwd_x", grid=(s // tm, nk),
        in_specs=[pl.BlockSpec((tm, tk), lambda i, k: (i, k)), pl.BlockSpec((D_MODEL, tk), lambda i, k: (0, k)),
                  pl.BlockSpec((tm, D_MODEL), lambda i, k: (i, 0)), pl.BlockSpec((tm, D_MODEL), lambda i, k: (i, 0)),
                  pl.BlockSpec((1, D_MODEL), lambda i, k: (0, 0))],
        out_specs=[pl.BlockSpec((tm, D_MODEL), lambda i, k: (i, 0)), pl.BlockSpec((8, D_MODEL), lambda i, k: (0, 0))],
        out_shape=[jax.ShapeDtypeStruct((s, D_MODEL), F32), jax.ShapeDtypeStruct((8, D_MODEL), F32)],
        scratch_shapes=[pltpu.VMEM((tm, D_MODEL), F32)],
        compiler_params=_params("arbitrary", "arbitrary"))(dproj, w, x, dxn, g)
    return dx, dg8


def _inproj_bwd_w(h, dproj):
    s = h.shape[0]
    tn, tk = 1024, min(1024, s)
    nk = s // tk

    def body(h_ref, dp_ref, dw_ref, acc_ref):
        k = pl.program_id(1)

        @pl.when(k == 0)
        def _():
            acc_ref[...] = jnp.zeros_like(acc_ref)

        acc_ref[...] += _dot_tn(h_ref[...], dp_ref[...])

        @pl.when(k == nk - 1)
        def _():
            dw_ref[...] = acc_ref[...].astype(BF16)

    return pl.pallas_call(
        body, name="inproj_bwd_w", grid=(PCOLS // tn, nk),
        in_specs=[pl.BlockSpec((tk, D_MODEL), lambda j, k: (k, 0)), pl.BlockSpec((tk, tn), lambda j, k: (k, j))],
        out_specs=pl.BlockSpec((D_MODEL, tn), lambda j, k: (0, j)),
        out_shape=jax.ShapeDtypeStruct((D_MODEL, PCOLS), BF16),
        scratch_shapes=[pltpu.VMEM((D_MODEL, tn), F32)],
        compiler_params=_params("arbitrary", "arbitrary"))(h, dproj)


def _mix_a_chunk(uvp, agp, wm_ref, bias, lg, lb):
    u, du = _gelu_and_grad(uvp[:, :BR])
    v, dv = _gelu_and_grad(uvp[:, BR:])
    xh, rs = _ln_fwd(v)
    vnb = (xh * lg + lb).astype(BF16)
    masks = _head_masks(BF16)
    mixed = bias
    for h in range(N_HEADS):
        mixed = mixed + _dot(wm_ref[h], vnb * masks[h])
    sg, dsg = _silu_and_grad(agp)
    return u, du, dv, xh, rs, vnb, masks, mixed, sg, dsg


def _store_masked_sgu(sw_ref, wm_ref):
    row = lax.broadcasted_iota(jnp.int32, (CHUNK, CHUNK), 0)
    col = lax.broadcasted_iota(jnp.int32, (CHUNK, CHUNK), 1)
    for h in range(N_HEADS):
        wm_ref[h] = jnp.where(row >= col, sw_ref[h], 0.0).astype(BF16)


def _mix_a_fwd(proj, sgu_w, bias, lg, lb):
    s = proj.shape[0]

    def body(proj_ref, sw_ref, bias_ref, lg_ref, lb_ref, y_ref, uv_buf, ag_buf, wm_ref, sems):
        _copy_all([(_window(proj_ref, AUV0, 2 * BR), uv_buf), (_window(proj_ref, AG0, BR), ag_buf)], sems)
        _store_masked_sgu(sw_ref, wm_ref)

        def chunk(i, c):
            rows = pl.ds(pl.multiple_of(i * CHUNK, CHUNK), CHUNK)
            u, _, _, _, _, _, _, mixed, sg, _ = _mix_a_chunk(uv_buf[rows, :], ag_buf[rows, :], wm_ref, bias_ref[...],
                                                            lg_ref[...], lb_ref[...])
            y_ref[rows, :] = (u * mixed * sg).astype(BF16)
            return c

        lax.fori_loop(0, s // CHUNK, chunk, 0)

    return pl.pallas_call(
        body, name="mix_a_fwd", in_specs=[ANY, VMEM, VMEM, VMEM, VMEM], out_specs=VMEM,
        out_shape=jax.ShapeDtypeStruct((s, BR), BF16),
        scratch_shapes=[pltpu.VMEM((s, 2 * BR), F32), pltpu.VMEM((s, BR), F32), pltpu.VMEM((N_HEADS, CHUNK, CHUNK), BF16),
                        pltpu.SemaphoreType.DMA((2,))],
        compiler_params=_params())(proj, sgu_w, bias, lg, lb)


def _mix_a_bwd(proj, dproj, dy, sgu_w, bias, lg, lb):
    s = proj.shape[0]

    def body(proj_ref, dproj_in, dy_ref, sw_ref, bias_ref, lg_ref, lb_ref,
             dproj_ref, dsw_ref, dbias_ref, dlg_ref, dlb_ref,
             uv_buf, ag_buf, duv_buf, dag_buf, wm_ref, acc_lg, acc_lb, sems):
        del dproj_in
        _copy_all([(_window(proj_ref, AUV0, 2 * BR), uv_buf), (_window(proj_ref, AG0, BR), ag_buf)], sems)
        _store_masked_sgu(sw_ref, wm_ref)
        dsw_ref[...] = jnp.zeros_like(dsw_ref)
        dbias_ref[...] = jnp.zeros_like(dbias_ref)
        acc_lg[...] = jnp.zeros_like(acc_lg)
        acc_lb[...] = jnp.zeros_like(acc_lb)

        def chunk(i, c):
            rows = pl.ds(pl.multiple_of(i * CHUNK, CHUNK), CHUNK)
            lg_v = lg_ref[...]
            u, du, dv, xh, rs, vnb, masks, mixed, sg, dsg = _mix_a_chunk(
                uv_buf[rows, :], ag_buf[rows, :], wm_ref, bias_ref[...], lg_v, lb_ref[...])
            dyv = dy_ref[rows, :]
            t1 = dyv * sg
            d_u = t1 * mixed
            d_mixed = t1 * u
            d_ag = dyv * u * mixed * dsg
            dbias_ref[...] += d_mixed
            dmb = d_mixed.astype(BF16)
            d_vn = jnp.zeros((CHUNK, BR), F32)
            for h in range(N_HEADS):
                dm_h = dmb * masks[h]
                dsw_ref[h] += _dot_nt(dm_h, vnb)
                d_vn = d_vn + _dot_tn(wm_ref[h], dm_h)
            acc_lg[...] += _part8(d_vn * xh)
            acc_lb[...] += _part8(d_vn)
            d_v = _ln_bwd(d_vn * lg_v, xh, rs)
            duv_buf[rows, :] = jnp.concatenate([d_u * du, d_v * dv], axis=1).astype(BF16)
            dag_buf[rows, :] = d_ag.astype(BF16)
            return c

        lax.fori_loop(0, s // CHUNK, chunk, 0)
        row = lax.broadcasted_iota(jnp.int32, (CHUNK, CHUNK), 0)
        col = lax.broadcasted_iota(jnp.int32, (CHUNK, CHUNK), 1)
        for h in range(N_HEADS):
            dsw_ref[h] = jnp.where(row >= col, dsw_ref[h], 0.0)
        dlg_ref[...] = jnp.sum(acc_lg[...], axis=0, keepdims=True)
        dlb_ref[...] = jnp.sum(acc_lb[...], axis=0, keepdims=True)
        _copy_all([(duv_buf, _window(dproj_ref, AUV0, 2 * BR)), (dag_buf, _window(dproj_ref, AG0, BR))], sems)

    return pl.pallas_call(
        body, name="mix_a_bwd", in_specs=[ANY, ANY, VMEM, VMEM, VMEM, VMEM, VMEM],
        out_specs=[ANY, VMEM, VMEM, VMEM, VMEM],
        out_shape=[jax.ShapeDtypeStruct(dproj.shape, dproj.dtype), jax.ShapeDtypeStruct((N_HEADS, CHUNK, CHUNK), F32),
                   jax.ShapeDtypeStruct((CHUNK, BR), F32), jax.ShapeDtypeStruct((1, BR), F32), jax.ShapeDtypeStruct((1, BR), F32)],
        scratch_shapes=[pltpu.VMEM((s, 2 * BR), F32), pltpu.VMEM((s, BR), F32), pltpu.VMEM((s, 2 * BR), BF16),
                        pltpu.VMEM((s, BR), BF16), pltpu.VMEM((N_HEADS, CHUNK, CHUNK), BF16),
                        pltpu.VMEM((8, BR), F32), pltpu.VMEM((8, BR), F32), pltpu.SemaphoreType.DMA((2,))],
        input_output_aliases={1: 0}, compiler_params=_params())(proj, dproj, dy, sgu_w, bias, lg, lb)


def _tri_ones(n, upper):
    row = lax.broadcasted_iota(jnp.int32, (n, n), 0)
    col = lax.broadcasted_iota(jnp.int32, (n, n), 1)
    return ((row <= col) if upper else (row >= col)).astype(BF16)


def _split3(c):
    hi = c.astype(BF16)
    r1 = c - hi.astype(F32)
    mid = r1.astype(BF16)
    lo = (r1 - mid.astype(F32)).astype(BF16)
    return [hi, mid, lo]


def _tri_sum(tri, a):
    parts = _split3(a)
    return _dot(tri, parts[0]) + _dot(tri, parts[1]) + _dot(tri, parts[2])


EXT = 2 * HEAD_DIM
LANE_CQ = HEAD_DIM
LANE_CK = HEAD_DIM + 3


def _to_head(h):
    r = lax.broadcasted_iota(jnp.int32, (BR, EXT), 0)
    c = lax.broadcasted_iota(jnp.int32, (BR, EXT), 1)
    return ((r == c + h * HEAD_DIM) & (c < HEAD_DIM)).astype(BF16)


def _from_head(h):
    r = lax.broadcasted_iota(jnp.int32, (EXT, BR), 0)
    c = lax.broadcasted_iota(jnp.int32, (EXT, BR), 1)
    return ((c == r + h * HEAD_DIM) & (r < HEAD_DIM)).astype(BF16)


def _attn_prep_fwd(proj, f_bias):
    s = proj.shape[0]
    scale = 1.0 / math.sqrt(HEAD_DIM)

    def body(proj_ref, fb_ref, qe_ref, ke_ref, ve_ref, ket_ref, vet_ref, qkv_buf, z_buf, sems):
        _copy_all([(_window(proj_ref, QKV0, 3 * BR), qkv_buf), (_window(proj_ref, BF0, CHUNK), z_buf)], sems)
        tri = _tri_ones(CHUNK, upper=False)
        lane = lax.broadcasted_iota(jnp.int32, (CHUNK, EXT), 1)
        ones_q = ((lane >= LANE_CK) & (lane < LANE_CK + 3)).astype(F32)
        ones_k = ((lane >= LANE_CQ) & (lane < LANE_CQ + 3)).astype(F32)

        def chunk(i, carry):
            rows = pl.ds(pl.multiple_of(i * CHUNK, CHUNK), CHUNK)
            cum = _tri_sum(tri, jax.nn.log_sigmoid(z_buf[rows, :] + fb_ref[...])) + carry
            qb = (qkv_buf[rows, 0:BR] * scale).astype(BF16)
            kb = qkv_buf[rows, BR:2 * BR].astype(BF16)
            vb = qkv_buf[rows, 2 * BR:].astype(BF16)
            parts = [p.astype(F32) for p in _split3(cum)]
            for h in range(N_HEADS):
                sel = _to_head(h)
                dec_q, dec_k = ones_q, ones_k
                for t, part in enumerate(parts):
                    pf = part[:, h:h + 1]
                    dec_q = dec_q + jnp.where(lane == LANE_CQ + t, pf, 0.0)
                    dec_k = dec_k - jnp.where(lane == LANE_CK + t, pf, 0.0)
                qe_ref[h, rows, :] = (_dot(qb, sel) + dec_q).astype(BF16)
                kh = _dot(kb, sel) + dec_k
                vh = _dot(vb, sel)
                ke_ref[h, rows, :] = kh.astype(BF16)
                ve_ref[h, rows, :] = vh.astype(BF16)
                ket_ref[h, :, rows] = kh.T.astype(BF16)
                vet_ref[h, :, rows] = vh.T.astype(BF16)
            return cum[CHUNK - 1:CHUNK, :]

        lax.fori_loop(0, s // CHUNK, chunk, jnp.zeros((1, CHUNK), F32))

    shape = jax.ShapeDtypeStruct((N_HEADS, s, EXT), BF16)
    shape_t = jax.ShapeDtypeStruct((N_HEADS, EXT, s), BF16)
    return pl.pallas_call(
        body, name="attn_prep_fwd", in_specs=[ANY, VMEM], out_specs=[VMEM] * 5, out_shape=[shape, shape, shape, shape_t, shape_t],
        scratch_shapes=[pltpu.VMEM((s, 3 * BR), F32), pltpu.VMEM((s, CHUNK), F32), pltpu.SemaphoreType.DMA((2,))],
        compiler_params=_params())(proj, f_bias)


def _attn_prep_bwd(proj, dproj, dqe, dke, dve, f_bias):
    s = proj.shape[0]
    pad = MRG0 - BF0
    scale = 1.0 / math.sqrt(HEAD_DIM)

    def body(proj_ref, dproj_in, dq_ref, dk_ref, dv_ref, fb_ref, dproj_ref, dfb_ref, z_buf, dqkv_buf, dz_buf, sems):
        del dproj_in
        _copy_all([(_window(proj_ref, BF0, CHUNK), z_buf)], sems)
        tri = _tri_ones(CHUNK, upper=True)
        lane = lax.broadcasted_iota(jnp.int32, (CHUNK, CHUNK), 1)
        dz_buf[...] = jnp.zeros_like(dz_buf)
        n = s // CHUNK

        def chunk(t, carry):
            suffix, acc = carry
            i = n - 1 - t
            rows = pl.ds(pl.multiple_of(i * CHUNK, CHUNK), CHUNK)
            dq = jnp.zeros((CHUNK, BR), F32)
            dk = jnp.zeros((CHUNK, BR), F32)
            dv = jnp.zeros((CHUNK, BR), F32)
            dcum = jnp.zeros((CHUNK, CHUNK), F32)
            for h in range(N_HEADS):
                back = _from_head(h)
                dqh = dq_ref[h, :, rows].T
                dkh = dk_ref[h, rows, :]
                dq = dq + _dot((dqh * scale).astype(BF16), back)
                dk = dk + _dot(dkh.astype(BF16), back)
                dv = dv + _dot(dv_ref[h, rows, :].astype(BF16), back)
                dcum = dcum + jnp.where(lane == h, dqh[:, LANE_CQ:LANE_CQ + 1] - dkh[:, LANE_CK:LANE_CK + 1], 0.0)
            dqkv_buf[rows, 0:BR] = dq.astype(BF16)
            dqkv_buf[rows, BR:2 * BR] = dk.astype(BF16)
            dqkv_buf[rows, 2 * BR:] = dv.astype(BF16)
            dlf = _tri_sum(tri, dcum) + suffix
            dz = dlf * _sigmoid(-(z_buf[rows, :] + fb_ref[...]))
            dz_buf[rows, pl.ds(0, CHUNK)] = dz.astype(BF16)
            return dlf[0:1, :], acc + _part8(dz)

        _, acc = lax.fori_loop(0, n, chunk, (jnp.zeros((1, CHUNK), F32), jnp.zeros((8, CHUNK), F32)))
        dfb_ref[...] = jnp.sum(acc, axis=0, keepdims=True)
        _copy_all([(dqkv_buf, _window(dproj_ref, QKV0, 3 * BR)), (dz_buf, _window(dproj_ref, BF0, pad))], sems)

    return pl.pallas_call(
        body, name="attn_prep_bwd", in_specs=[ANY, ANY, VMEM, VMEM, VMEM, VMEM], out_specs=[ANY, VMEM],
        out_shape=[jax.ShapeDtypeStruct(dproj.shape, dproj.dtype), jax.ShapeDtypeStruct((1, CHUNK), F32)],
        scratch_shapes=[pltpu.VMEM((s, CHUNK), F32), pltpu.VMEM((s, 3 * BR), BF16), pltpu.VMEM((s, pad), BF16),
                        pltpu.SemaphoreType.DMA((2,))],
        input_output_aliases={1: 0}, compiler_params=_params())(proj, dproj, dqe, dke, dve, f_bias)


ATT_TQ = 256
ATT_FWD_GROUP = 4
ATT_BWD_GROUP = 2
NEG_BIG = -1e30


def _causal_t(q0, k0, tk, tq):
    kpos = k0 + lax.broadcasted_iota(jnp.int32, (tk, tq), 0)
    qpos = q0 + lax.broadcasted_iota(jnp.int32, (tk, tq), 1)
    return kpos <= qpos


STAT_ROWS = 8


def _attn_fwd(qe, ke, vet):
    nh, s, w = qe.shape
    tq = min(ATT_TQ, s)
    tk = min(ATT_FWD_GROUP * tq, s)
    per = tk // tq

    def body(q_ref, k_ref, vt_ref, ot_ref, lse_ref):
        i = pl.program_id(1)
        q = q_ref[0]

        def step(j, carry, masked):
            m, l, acc = carry
            ks = pl.ds(pl.multiple_of(j * tk, tk), tk)
            st = _dot_nt(k_ref[0, ks, :], q)
            if masked:
                st = jnp.where(_causal_t(i * tq, j * tk, tk, tq), st, NEG_BIG)
            m_new = jnp.maximum(m, jnp.max(st, axis=0, keepdims=True))
            alpha = jnp.exp(m - m_new)
            pt = jnp.exp(st - m_new)
            l = alpha * l + jnp.sum(pt, axis=0, keepdims=True)
            acc = alpha * acc + _dot(vt_ref[0, :, ks], pt.astype(BF16))
            return m_new, l, acc

        init = (jnp.full((1, tq), NEG_BIG, F32), jnp.zeros((1, tq), F32), jnp.zeros((w, tq), F32))
        carry = lax.fori_loop(0, i // per, functools.partial(step, masked=False), init)
        m, l, acc = step(i // per, carry, True)
        ot_ref[0] = acc / l
        lse_ref[0] = jnp.broadcast_to(m + jnp.log(l), (STAT_ROWS, tq))

    return pl.pallas_call(
        body, name="attn_fwd", grid=(nh, s // tq),
        in_specs=[pl.BlockSpec((1, tq, w), lambda h, i: (h, i, 0)), pl.BlockSpec((1, s, w), lambda h, i: (h, 0, 0)),
                  pl.BlockSpec((1, w, s), lambda h, i: (h, 0, 0))],
        out_specs=[pl.BlockSpec((1, w, tq), lambda h, i: (h, 0, i)), pl.BlockSpec((1, STAT_ROWS, tq), lambda h, i: (h, 0, i))],
        out_shape=[jax.ShapeDtypeStruct((nh, w, s), F32), jax.ShapeDtypeStruct((nh, STAT_ROWS, s), F32)],
        compiler_params=_params("arbitrary", "arbitrary"))(qe, ke, vet)


def _attn_bwd(qe, ke, ket, ve, doe, lse, dd):
    nh, s, w = qe.shape
    tq = min(ATT_TQ, s)
    tk = min(ATT_BWD_GROUP * tq, s)
    per = tk // tq

    def body(q_ref, do_ref, lse_ref, dd_ref, k_ref, kt_ref, v_ref, dqt_ref, dk_ref, dv_ref):
        i = pl.program_id(1)

        @pl.when(i == 0)
        def _():
            dk_ref[...] = jnp.zeros_like(dk_ref)
            dv_ref[...] = jnp.zeros_like(dv_ref)

        q = q_ref[0]
        do = do_ref[0]
        lse_row = lse_ref[0, 0:1, :]
        dd_row = dd_ref[0, 0:1, :]

        def step(j, dqt, masked):
            ks = pl.ds(pl.multiple_of(j * tk, tk), tk)
            st = _dot_nt(k_ref[0, ks, :], q)
            pt = jnp.exp(st - lse_row)
            if masked:
                pt = jnp.where(_causal_t(i * tq, j * tk, tk, tq), pt, 0.0)
            dpt = _dot_nt(v_ref[0, ks, :], do)
            dst = (pt * (dpt - dd_row)).astype(BF16)
            dv_ref[0, ks, :] += _dot(pt.astype(BF16), do)
            dk_ref[0, ks, :] += _dot(dst, q)
            return dqt + _dot(kt_ref[0, :, ks], dst)

        dqt = lax.fori_loop(0, i // per, functools.partial(step, masked=False), jnp.zeros((w, tq), F32))
        dqt_ref[0] = step(i // per, dqt, True)

    qblk = pl.BlockSpec((1, tq, w), lambda h, i: (h, i, 0))
    stat = pl.BlockSpec((1, STAT_ROWS, tq), lambda h, i: (h, 0, i))
    whole = pl.BlockSpec((1, s, w), lambda h, i: (h, 0, 0))
    whole_t = pl.BlockSpec((1, w, s), lambda h, i: (h, 0, 0))
    return pl.pallas_call(
        body, name="attn_bwd", grid=(nh, s // tq),
        in_specs=[qblk, qblk, stat, stat, whole, whole_t, whole],
        out_specs=[pl.BlockSpec((1, w, tq), lambda h, i: (h, 0, i)), whole, whole],
        out_shape=[jax.ShapeDtypeStruct((nh, w, s), F32), jax.ShapeDtypeStruct((nh, s, w), F32), jax.ShapeDtypeStruct((nh, s, w), F32)],
        compiler_params=_params("arbitrary", "arbitrary"))(qe, doe, lse, dd, ke, ket, ve)


def _bgate_fwd(proj, ot):
    s = proj.shape[0]

    def body(proj_ref, ot_ref, y_ref, o_ref, g_buf, sems):
        _copy_all([(_window(proj_ref, BG0, BR), g_buf)], sems)

        def chunk(i, c):
            rows = pl.ds(pl.multiple_of(i * CHUNK, CHUNK), CHUNK)
            o = jnp.zeros((CHUNK, BR), F32)
            for h in range(N_HEADS):
                back = _from_head(h)
                for part in _split3(ot_ref[h, :, rows].T):
                    o = o + _dot(part, back)
            sg, _ = _silu_and_grad(g_buf[rows, :])
            o_ref[rows, :] = o
            y_ref[rows, :] = (o * sg).astype(BF16)
            return c

        lax.fori_loop(0, s // CHUNK, chunk, 0)

    return pl.pallas_call(
        body, name="bgate_fwd", in_specs=[ANY, VMEM], out_specs=[VMEM, VMEM],
        out_shape=[jax.ShapeDtypeStruct((s, BR), BF16), jax.ShapeDtypeStruct((s, BR), F32)],
        scratch_shapes=[pltpu.VMEM((s, BR), F32), pltpu.SemaphoreType.DMA((1,))], compiler_params=_params())(proj, ot)


def _bgate_bwd(proj, dproj, o, dy):
    s = proj.shape[0]

    def body(proj_ref, dproj_in, o_ref, dy_ref, dproj_ref, do_ref, dd_ref, g_buf, dg_buf, sems):
        del dproj_in
        _copy_all([(_window(proj_ref, BG0, BR), g_buf)], sems)
        lane = lax.broadcasted_iota(jnp.int32, (BR, CHUNK), 0) // HEAD_DIM
        col = lax.broadcasted_iota(jnp.int32, (BR, CHUNK), 1)
        sel = (lane == col).astype(BF16)

        def chunk(i, c):
            rows = pl.ds(pl.multiple_of(i * CHUNK, CHUNK), CHUNK)
            sg, dsg = _silu_and_grad(g_buf[rows, :])
            dyv = dy_ref[rows, :]
            ov = o_ref[rows, :]
            do = dyv * sg
            dg_buf[rows, :] = (dyv * ov * dsg).astype(BF16)
            prod = _split3(do * ov)
            ddt = (_dot(prod[0], sel) + _dot(prod[1], sel) + _dot(prod[2], sel)).T
            dob = do.astype(BF16)
            for h in range(N_HEADS):
                do_ref[h, rows, :] = _dot(dob, _to_head(h)).astype(BF16)
                dd_ref[h, :, rows] = jnp.broadcast_to(ddt[h:h + 1, :], (STAT_ROWS, CHUNK))
            return c

        lax.fori_loop(0, s // CHUNK, chunk, 0)
        _copy_all([(dg_buf, _window(dproj_ref, BG0, BR))], sems)

    return pl.pallas_call(
        body, name="bgate_bwd", in_specs=[ANY, ANY, VMEM, VMEM], out_specs=[ANY, VMEM, VMEM],
        out_shape=[jax.ShapeDtypeStruct(dproj.shape, dproj.dtype), jax.ShapeDtypeStruct((N_HEADS, s, EXT), BF16),
                   jax.ShapeDtypeStruct((N_HEADS, STAT_ROWS, s), F32)],
        scratch_shapes=[pltpu.VMEM((s, BR), F32), pltpu.VMEM((s, BR), BF16), pltpu.SemaphoreType.DMA((1,))],
        input_output_aliases={1: 0}, compiler_params=_params())(proj, dproj, o, dy)


C_PAD = 8


def _mix_c_fwd(proj, w):
    s = proj.shape[0]

    def body(proj_ref, w_ref, y_ref, cin_buf, g_buf, z_buf, sems):
        _copy_all([(_window(proj_ref, CIN0, 3 * BR), cin_buf), (_window(proj_ref, CG0, BR), g_buf)], sems)
        z_buf[pl.ds(0, C_PAD), :] = jnp.zeros((C_PAD, BR), F32)

        def fill(i, c):
            rows = pl.ds(pl.multiple_of(i * CHUNK, CHUNK), CHUNK)
            z_buf[pl.ds(pl.multiple_of(i * CHUNK + C_PAD, 8), CHUNK), :] = cin_buf[rows, BR:2 * BR] * cin_buf[rows, 2 * BR:]
            return c

        lax.fori_loop(0, s // CHUNK, fill, 0)

        def chunk(i, c):
            r0 = pl.multiple_of(i * CHUNK, CHUNK)
            rows = pl.ds(r0, CHUNK)
            ze = z_buf[pl.ds(r0, CHUNK + C_PAD), :]
            conv = jnp.zeros((CHUNK, BR), F32)
            for k in range(K_SHORT):
                off = C_PAD - (K_SHORT - 1) + k
                conv = conv + w_ref[k:k + 1, :] * ze[off:off + CHUNK]
            sg, _ = _silu_and_grad(g_buf[rows, :])
            y_ref[rows, :] = (cin_buf[rows, 0:BR] * conv * sg).astype(BF16)
            return c

        lax.fori_loop(0, s // CHUNK, chunk, 0)

    return pl.pallas_call(
        body, name="mix_c_fwd", in_specs=[ANY, VMEM], out_specs=VMEM, out_shape=jax.ShapeDtypeStruct((s, BR), BF16),
        scratch_shapes=[pltpu.VMEM((s, 3 * BR), F32), pltpu.VMEM((s, BR), F32), pltpu.VMEM((s + C_PAD, BR), F32),
                        pltpu.SemaphoreType.DMA((2,))],
        compiler_params=_params())(proj, w)


def _mix_c_bwd(proj, dproj, dy, w):
    s = proj.shape[0]

    def body(proj_ref, dproj_in, dy_ref, w_ref, dproj_ref, dw_ref, cin_buf, g_buf, z_buf, dc_buf, dcin_buf, dg_buf, acc, sems):
        del dproj_in
        _copy_all([(_window(proj_ref, CIN0, 3 * BR), cin_buf), (_window(proj_ref, CG0, BR), g_buf)], sems)
        z_buf[pl.ds(0, C_PAD), :] = jnp.zeros((C_PAD, BR), F32)
        dc_buf[pl.ds(s, C_PAD), :] = jnp.zeros((C_PAD, BR), F32)
        acc[...] = jnp.zeros_like(acc)

        def fill(i, c):
            rows = pl.ds(pl.multiple_of(i * CHUNK, CHUNK), CHUNK)
            z_buf[pl.ds(pl.multiple_of(i * CHUNK + C_PAD, 8), CHUNK), :] = cin_buf[rows, BR:2 * BR] * cin_buf[rows, 2 * BR:]
            return c

        lax.fori_loop(0, s // CHUNK, fill, 0)

        def chunk(i, c):
            r0 = pl.multiple_of(i * CHUNK, CHUNK)
            rows = pl.ds(r0, CHUNK)
            ze = z_buf[pl.ds(r0, CHUNK + C_PAD), :]
            taps = [ze[C_PAD - (K_SHORT - 1) + k:C_PAD - (K_SHORT - 1) + k + CHUNK] for k in range(K_SHORT)]
            conv = jnp.zeros((CHUNK, BR), F32)
            for k in range(K_SHORT):
                conv = conv + w_ref[k:k + 1, :] * taps[k]
            sg, dsg = _silu_and_grad(g_buf[rows, :])
            bg = cin_buf[rows, 0:BR]
            dyv = dy_ref[rows, :]
            dconv = dyv * bg * sg
            dc_buf[rows, :] = dconv
            dcin_buf[rows, 0:BR] = (dyv * conv * sg).astype(BF16)
            dg_buf[rows, :] = (dyv * bg * conv * dsg).astype(BF16)
            for k in range(K_SHORT):
                acc[k] += _part8(dconv * taps[k])
            return c

        lax.fori_loop(0, s // CHUNK, chunk, 0)

        def chunk2(i, c):
            r0 = pl.multiple_of(i * CHUNK, CHUNK)
            rows = pl.ds(r0, CHUNK)
            de = dc_buf[pl.ds(r0, CHUNK + C_PAD), :]
            dz = jnp.zeros((CHUNK, BR), F32)
            for k in range(K_SHORT):
                off = K_SHORT - 1 - k
                dz = dz + w_ref[k:k + 1, :] * de[off:off + CHUNK]
            dcin_buf[rows, BR:2 * BR] = (dz * cin_buf[rows, 2 * BR:]).astype(BF16)
            dcin_buf[rows, 2 * BR:] = (dz * cin_buf[rows, BR:2 * BR]).astype(BF16)
            return c

        lax.fori_loop(0, s // CHUNK, chunk2, 0)
        dw_ref[...] = jnp.zeros_like(dw_ref)
        for k in range(K_SHORT):
            dw_ref[k:k + 1, :] = jnp.sum(acc[k], axis=0, keepdims=True)
        _copy_all([(dcin_buf, _window(dproj_ref, CIN0, 3 * BR)), (dg_buf, _window(dproj_ref, CG0, BR))], sems)

    return pl.pallas_call(
        body, name="mix_c_bwd", in_specs=[ANY, ANY, VMEM, VMEM], out_specs=[ANY, VMEM],
        out_shape=[jax.ShapeDtypeStruct(dproj.shape, dproj.dtype), jax.ShapeDtypeStruct((8, BR), F32)],
        scratch_shapes=[pltpu.VMEM((s, 3 * BR), F32), pltpu.VMEM((s, BR), F32), pltpu.VMEM((s + C_PAD, BR), F32),
                        pltpu.VMEM((s + C_PAD, BR), F32), pltpu.VMEM((s, 3 * BR), BF16), pltpu.VMEM((s, BR), BF16),
                        pltpu.VMEM((K_SHORT, 8, BR), F32), pltpu.SemaphoreType.DMA((2,))],
        input_output_aliases={1: 0}, compiler_params=_params())(proj, dproj, dy, w)


D_PAD = 32


def _tap_windows(win, offsets, n):
    rows = win.shape[0]
    rolled, out = {}, []
    for off in offsets:
        r = off % 8
        if r not in rolled:
            rolled[r] = win if r == 0 else pltpu.roll(win, rows - r, 0)
        out.append(rolled[r][off - r:off - r + n])
    return out


def _mix_d_common(cin_ref, g_ref, w_ref, b_ref, lg_ref, lb_ref, hh_buf, r0):
    he = hh_buf[pl.ds(r0, CHUNK + D_PAD), :]
    taps = _tap_windows(he, [D_PAD - (K_CONF - 1) + k for k in range(K_CONF)], CHUNK)
    conv = jnp.zeros((CHUNK, BR), F32) + b_ref[...]
    for k in range(K_CONF):
        conv = conv + w_ref[k:k + 1, :] * taps[k]
    xh, rs = _ln_fwd(conv)
    sw, dsw = _silu_and_grad(xh * lg_ref[...] + lb_ref[...])
    sg, dsg = _silu_and_grad(g_ref[pl.ds(r0, CHUNK), :])
    return taps, xh, rs, sw, dsw, sg, dsg


def _mix_d_fill(cin_buf, hh_buf, s):
    hh_buf[pl.ds(0, D_PAD), :] = jnp.zeros((D_PAD, BR), F32)

    def fill(i, c):
        rows = pl.ds(pl.multiple_of(i * CHUNK, CHUNK), CHUNK)
        hh_buf[pl.ds(pl.multiple_of(i * CHUNK + D_PAD, 8), CHUNK), :] = cin_buf[rows, 0:BR] * _sigmoid(cin_buf[rows, BR:])
        return c

    lax.fori_loop(0, s // CHUNK, fill, 0)


def _mix_d_fwd(proj, w, b, lg, lb):
    s = proj.shape[0]

    def body(proj_ref, w_ref, b_ref, lg_ref, lb_ref, y_ref, cin_buf, g_buf, hh_buf, sems):
        _copy_all([(_window(proj_ref, DGLU0, 2 * BR), cin_buf), (_window(proj_ref, DG0, BR), g_buf)], sems)
        _mix_d_fill(cin_buf, hh_buf, s)

        def chunk(i, c):
            r0 = pl.multiple_of(i * CHUNK, CHUNK)
            _, _, _, sw, _, sg, _ = _mix_d_common(cin_buf, g_buf, w_ref, b_ref, lg_ref, lb_ref, hh_buf, r0)
            y_ref[pl.ds(r0, CHUNK), :] = (sw * sg).astype(BF16)
            return c

        lax.fori_loop(0, s // CHUNK, chunk, 0)

    return pl.pallas_call(
        body, name="mix_d_fwd", in_specs=[ANY, VMEM, VMEM, VMEM, VMEM], out_specs=VMEM,
        out_shape=jax.ShapeDtypeStruct((s, BR), BF16),
        scratch_shapes=[pltpu.VMEM((s, 2 * BR), F32), pltpu.VMEM((s, BR), F32), pltpu.VMEM((s + D_PAD, BR), F32),
                        pltpu.SemaphoreType.DMA((2,))],
        compiler_params=_params())(proj, w, b, lg, lb)


def _mix_d_bwd(proj, dproj, dy, w, b, lg, lb):
    s = proj.shape[0]

    def body(proj_ref, dproj_in, dy_ref, w_ref, b_ref, lg_ref, lb_ref, dproj_ref, dw_ref, db_ref, dlg_ref, dlb_ref,
             cin_buf, g_buf, hh_buf, dc_buf, dcin_buf, dg_buf, acc_w, acc_s, sems):
        del dproj_in
        _copy_all([(_window(proj_ref, DGLU0, 2 * BR), cin_buf), (_window(proj_ref, DG0, BR), g_buf)], sems)
        _mix_d_fill(cin_buf, hh_buf, s)
        dc_buf[pl.ds(s, D_PAD), :] = jnp.zeros((D_PAD, BR), F32)
        acc_w[...] = jnp.zeros_like(acc_w)
        acc_s[...] = jnp.zeros_like(acc_s)

        def chunk(i, c):
            r0 = pl.multiple_of(i * CHUNK, CHUNK)
            rows = pl.ds(r0, CHUNK)
            taps, xh, rs, sw, dsw, sg, dsg = _mix_d_common(cin_buf, g_buf, w_ref, b_ref, lg_ref, lb_ref, hh_buf, r0)
            dyv = dy_ref[rows, :]
            dg_buf[rows, :] = (dyv * sw * dsg).astype(BF16)
            d_ln = dyv * sg * dsw
            acc_s[0] += _part8(d_ln * xh)
            acc_s[1] += _part8(d_ln)
            dc = _ln_bwd(d_ln * lg_ref[...], xh, rs)
            acc_s[2] += _part8(dc)
            dc_buf[rows, :] = dc
            for k in range(K_CONF):
                acc_w[k] += _part8(dc * taps[k])
            return c

        lax.fori_loop(0, s // CHUNK, chunk, 0)

        def chunk2(i, c):
            r0 = pl.multiple_of(i * CHUNK, CHUNK)
            rows = pl.ds(r0, CHUNK)
            de = dc_buf[pl.ds(r0, CHUNK + D_PAD), :]
            dh = jnp.zeros((CHUNK, BR), F32)
            for k, win in enumerate(_tap_windows(de, [K_CONF - 1 - k for k in range(K_CONF)], CHUNK)):
                dh = dh + w_ref[k:k + 1, :] * win
            a = cin_buf[rows, 0:BR]
            sig = _sigmoid(cin_buf[rows, BR:])
            dcin_buf[rows, 0:BR] = (dh * sig).astype(BF16)
            dcin_buf[rows, BR:] = (dh * a * sig * (1.0 - sig)).astype(BF16)
            return c

        lax.fori_loop(0, s // CHUNK, chunk2, 0)
        dw_ref[...] = jnp.zeros_like(dw_ref)
        for k in range(K_CONF):
            dw_ref[k:k + 1, :] = jnp.sum(acc_w[k], axis=0, keepdims=True)
        dlg_ref[...] = jnp.sum(acc_s[0], axis=0, keepdims=True)
        dlb_ref[...] = jnp.sum(acc_s[1], axis=0, keepdims=True)
        db_ref[...] = jnp.sum(acc_s[2], axis=0, keepdims=True)
        _copy_all([(dcin_buf, _window(dproj_ref, DGLU0, 2 * BR)), (dg_buf, _window(dproj_ref, DG0, BR))], sems)

    vec = jax.ShapeDtypeStruct((1, BR), F32)
    return pl.pallas_call(
        body, name="mix_d_bwd", in_specs=[ANY, ANY, VMEM, VMEM, VMEM, VMEM, VMEM], out_specs=[ANY, VMEM, VMEM, VMEM, VMEM],
        out_shape=[jax.ShapeDtypeStruct(dproj.shape, dproj.dtype), jax.ShapeDtypeStruct((32, BR), F32), vec, vec, vec],
        scratch_shapes=[pltpu.VMEM((s, 2 * BR), F32), pltpu.VMEM((s, BR), F32), pltpu.VMEM((s + D_PAD, BR), F32),
                        pltpu.VMEM((s + D_PAD, BR), F32), pltpu.VMEM((s, 2 * BR), BF16), pltpu.VMEM((s, BR), BF16),
                        pltpu.VMEM((K_CONF, 8, BR), F32), pltpu.VMEM((3, 8, BR), F32), pltpu.SemaphoreType.DMA((2,))],
        input_output_aliases={1: 0}, compiler_params=_params())(proj, dproj, dy, w, b, lg, lb)


MERGE_TM = 256


def _merge_fwd(x, proj, ys, wb, wo):
    s = x.shape[0]
    tm = min(MERGE_TM, s)

    def body(x_ref, lg_ref, ya, yb, yc, yd, wb_ref, wo_ref, xn_ref, mg_ref):
        merged = jnp.zeros((tm, D_MODEL), F32)
        for n, y_ref in enumerate((ya, yb, yc, yd)):
            gate = _sigmoid(lg_ref[:, n * D_MODEL:(n + 1) * D_MODEL])
            merged = merged + gate * _dot(y_ref[...], wb_ref[n])
        mb = merged.astype(BF16)
        mg_ref[...] = mb
        xn_ref[...] = x_ref[...] + _dot(mb, wo_ref[...])

    row = lambda w: pl.BlockSpec((tm, w), lambda i: (i, 0))
    return pl.pallas_call(
        body, name="merge_fwd", grid=(s // tm,),
        in_specs=[row(D_MODEL), pl.BlockSpec((tm, MRG0), lambda i: (i, 1)), row(BR), row(BR), row(BR), row(BR),
                  pl.BlockSpec((N_HEADS, BR, D_MODEL), lambda i: (0, 0, 0)), pl.BlockSpec((D_MODEL, D_MODEL), lambda i: (0, 0))],
        out_specs=[row(D_MODEL), row(D_MODEL)],
        out_shape=[jax.ShapeDtypeStruct((s, D_MODEL), F32), jax.ShapeDtypeStruct((s, D_MODEL), BF16)],
        compiler_params=_params("arbitrary"))(x, proj, *ys, wb, wo)


def _merge_bwd(dxn, proj, ys, merged, wb, wo):
    s = dxn.shape[0]
    tm = min(MERGE_TM, s)

    def body(dx_ref, lg_ref, ya, yb, yc, yd, mg_ref, wb_ref, wo_ref, dlg_ref, da, db, dc, dd, dwo_ref, dwb_ref):
        @pl.when(pl.program_id(0) == 0)
        def _():
            dwo_ref[...] = jnp.zeros_like(dwo_ref)
            dwb_ref[...] = jnp.zeros_like(dwb_ref)

        dxb = dx_ref[...].astype(BF16)
        d_merged = _dot_nt(dxb, wo_ref[...])
        dwo_ref[...] += _dot_tn(mg_ref[...], dxb)
        for n, (y_ref, dy_ref) in enumerate(((ya, da), (yb, db), (yc, dc), (yd, dd))):
            yv = y_ref[...]
            gate = _sigmoid(lg_ref[:, n * D_MODEL:(n + 1) * D_MODEL])
            pn = _dot(yv, wb_ref[n])
            dlg_ref[:, n * D_MODEL:(n + 1) * D_MODEL] = (d_merged * pn * gate * (1.0 - gate)).astype(BF16)
            dpn = (d_merged * gate).astype(BF16)
            dy_ref[...] = _dot_nt(dpn, wb_ref[n])
            dwb_ref[n] += _dot_tn(yv, dpn)

    row = lambda w: pl.BlockSpec((tm, w), lambda i: (i, 0))
    wb_spec = pl.BlockSpec((N_HEADS, BR, D_MODEL), lambda i: (0, 0, 0))
    wo_spec = pl.BlockSpec((D_MODEL, D_MODEL), lambda i: (0, 0))
    dy_shape = jax.ShapeDtypeStruct((s, BR), F32)
    outs = pl.pallas_call(
        body, name="merge_bwd", grid=(s // tm,),
        in_specs=[row(D_MODEL), pl.BlockSpec((tm, MRG0), lambda i: (i, 1)), row(BR), row(BR), row(BR), row(BR), row(D_MODEL),
                  wb_spec, wo_spec],
        out_specs=[pl.BlockSpec((tm, MRG0), lambda i: (i, 1)), row(BR), row(BR), row(BR), row(BR), wo_spec, wb_spec],
        out_shape=[jax.ShapeDtypeStruct((s, PCOLS), BF16), dy_shape, dy_shape, dy_shape, dy_shape,
                   jax.ShapeDtypeStruct((D_MODEL, D_MODEL), F32), jax.ShapeDtypeStruct((N_HEADS, BR, D_MODEL), F32)],
        compiler_params=_params("arbitrary"))(dxn, proj, *ys, merged, wb, wo)
    return outs[0], outs[1:5], outs[5], outs[6]


def _loss_head(x, target, g):
    s = x.shape[0]
    tm = min(512, s)

    def body(x_ref, t_ref, g_ref, loss_ref, dx_ref, dg_ref):
        @pl.when(pl.program_id(0) == 0)
        def _():
            loss_ref[...] = jnp.zeros_like(loss_ref)
            dg_ref[...] = jnp.zeros_like(dg_ref)

        xv = x_ref[...]
        gv = g_ref[...]
        r = lax.rsqrt(jnp.mean(xv * xv, axis=-1, keepdims=True) + EPS)
        xn = xv * r
        err = xn * gv - t_ref[...]
        loss_ref[...] += 0.5 * jnp.sum(jnp.mean(err * err, axis=-1, keepdims=True))
        dy = err * (1.0 / D_MODEL)
        dg_ref[...] += _part8(dy * xn)
        gy = dy * gv
        dx_ref[...] = r * (gy - xn * jnp.mean(xn * gy, axis=-1, keepdims=True))

    row = pl.BlockSpec((tm, D_MODEL), lambda i: (i, 0))
    return pl.pallas_call(
        body, name="loss_head", grid=(s // tm,),
        in_specs=[row, row, pl.BlockSpec((1, D_MODEL), lambda i: (0, 0))],
        out_specs=[pl.BlockSpec((8, 128), lambda i: (0, 0)), row, pl.BlockSpec((8, D_MODEL), lambda i: (0, 0))],
        out_shape=[jax.ShapeDtypeStruct((8, 128), F32), jax.ShapeDtypeStruct((s, D_MODEL), F32), jax.ShapeDtypeStruct((8, D_MODEL), F32)],
        compiler_params=_params("arbitrary"))(x, target, g)


_SEGMENTS = ((0, 512, AUV0), (512, 256, AG0), (768, 768, QKV0), (1536, 4, BF0), (1540, 256, BG0), (1796, 768, CIN0),
             (2564, 256, CG0), (2820, 512, DGLU0), (3332, 256, DG0), (3588, 4096, MRG0))


def _to_aligned_cols(w):
    out, pos = [], 0
    for src, width, dst in sorted(_SEGMENTS, key=lambda t: t[2]):
        if dst > pos:
            out.append(jnp.zeros(w.shape[:-1] + (dst - pos,), w.dtype))
        out.append(w[..., src:src + width])
        pos = dst + width
    return jnp.concatenate(out, axis=-1)


def _from_aligned_cols(g):
    return jnp.concatenate([g[..., dst:dst + width] for _, width, dst in _SEGMENTS], axis=-1)


def _sgu_bias_rows(sgu_b):
    return jnp.repeat(sgu_b.T, HEAD_DIM, axis=1)


def _layer_fwd(x, p):
    proj, h = _inproj_fwd(x, p["norm_g"], p["w_in"])
    ya = _mix_a_fwd(proj, p["sgu_w"], _sgu_bias_rows(p["sgu_b"]), p["sgu_ln_g"], p["sgu_ln_b"])
    qe, ke, ve, ket, vet = _attn_prep_fwd(proj, p["f_bias"])
    ot, lse = _attn_fwd(qe, ke, vet)
    yb, o = _bgate_fwd(proj, ot)
    yc = _mix_c_fwd(proj, p["short_conv_w"])
    yd = _mix_d_fwd(proj, p["conf_dw_w"], p["conf_dw_b"], p["conf_ln_g"], p["conf_ln_b"])
    ys = (ya, yb, yc, yd)
    x_next, merged = _merge_fwd(x, proj, ys, p["w_branch"], p["w_out"])
    saved = dict(x=x, proj=proj, h=h, ys=ys, merged=merged, qe=qe, ke=ke, ve=ve, ket=ket, lse=lse, o=o)
    return x_next, saved


def _layer_bwd(dxn, p, sv):
    proj = sv["proj"]
    dproj, dys, d_wo, d_wb = _merge_bwd(dxn, proj, sv["ys"], sv["merged"], p["w_branch"], p["w_out"])
    dproj, d_sgu_w, d_bias_rows, d_sgu_lg, d_sgu_lb = _mix_a_bwd(
        proj, dproj, dys[0], p["sgu_w"], _sgu_bias_rows(p["sgu_b"]), p["sgu_ln_g"], p["sgu_ln_b"])
    dproj, doe, dd = _bgate_bwd(proj, dproj, sv["o"], dys[1])
    dqe, dke, dve = _attn_bwd(sv["qe"], sv["ke"], sv["ket"], sv["ve"], doe, sv["lse"], dd)
    dproj, d_fb = _attn_prep_bwd(proj, dproj, dqe, dke, dve, p["f_bias"])
    dproj, d_sc = _mix_c_bwd(proj, dproj, dys[2], p["short_conv_w"])
    dproj, d_cw, d_cb, d_clg, d_clb = _mix_d_bwd(proj, dproj, dys[3], p["conf_dw_w"], p["conf_dw_b"], p["conf_ln_g"], p["conf_ln_b"])
    dx, dg8 = _inproj_bwd_x(dproj, p["w_in"], sv["x"], dxn, p["norm_g"])
    d_win = _inproj_bwd_w(sv["h"], dproj)
    grads = dict(
        norm_g=jnp.sum(dg8, axis=0), w_in=d_win, f_bias=d_fb[0, :N_HEADS], sgu_w=d_sgu_w,
        sgu_b=d_bias_rows.reshape(CHUNK, N_HEADS, HEAD_DIM).sum(axis=-1).T,
        sgu_ln_g=d_sgu_lg[0], sgu_ln_b=d_sgu_lb[0], short_conv_w=d_sc[:K_SHORT], conf_dw_w=d_cw[:K_CONF],
        conf_dw_b=d_cb[0], conf_ln_g=d_clg[0], conf_ln_b=d_clb[0], w_branch=d_wb, w_out=d_wo)
    return dx, grads


def _local_step(x, target, layers, final_g):
    saved = []
    for p in layers:
        x, sv = _layer_fwd(x, p)
        saved.append(sv)
    loss8, dx, dfg8 = _loss_head(x, target, final_g)
    grads = [None] * len(layers)
    for l in reversed(range(len(layers))):
        dx, grads[l] = _layer_bwd(dx, layers[l], saved[l])
    return loss8[0, 0], dx, grads, jnp.sum(dfg8, axis=0)


def _place():
    return lax.axis_index("x"), lax.axis_index("y"), lax.axis_index("c")


def _other_chips(x, y):
    return [(1 - x, y), (x, 1 - y), (1 - x, 1 - y)]


def _gather_chips(blocks):
    n = len(blocks)

    def body(*refs):
        ins, outs, (send_sems, recv_sems) = refs[:n], refs[n:2 * n], refs[2 * n:]
        x, y, cc = _place()
        me, sibling = (x, y, cc), (x, y, 1 - cc)
        chips = _other_chips(x, y)

        def copy(a, k, chip, layer, to, src=None):
            dst = outs[a].at[2 * chip[0] + chip[1], layer]
            return pltpu.make_async_remote_copy(src_ref=dst if src is None else src, dst_ref=dst, send_sem=send_sems.at[a, k],
                                                recv_sem=recv_sems.at[a, k], device_id=to, device_id_type=MESH)

        first = [copy(a, j, (x, y), cc, (*chip, cc), src=ins[a]) for j, chip in enumerate(chips) for a in range(n)]
        for cp in first:
            cp.start()
        passed = []
        for j, chip in enumerate(chips):
            for a in range(n):
                copy(a, j, chip, cc, me).wait_recv()
                passed.append(copy(a, 3 + j, chip, cc, sibling))
                passed[-1].start()
        for j, chip in enumerate(chips):
            for a in range(n):
                copy(a, 3 + j, chip, 1 - cc, me).wait_recv()
        for cp in first + passed:
            cp.wait_send()

    return pl.pallas_call(
        body, name="gather_chips", in_specs=[ANY] * n, out_specs=[ANY] * n,
        out_shape=[jax.ShapeDtypeStruct((N_CHIPS, 2) + b.shape, b.dtype) for b in blocks],
        scratch_shapes=[pltpu.SemaphoreType.DMA((n, 6)), pltpu.SemaphoreType.DMA((n, 6))])(*blocks)


def _pair_exchange(arrays):
    n = len(arrays)

    def body(*refs):
        ins, outs, (send_sems, recv_sems) = refs[:n], refs[n:2 * n], refs[2 * n:]
        x, y, cc = _place()
        cps = [pltpu.make_async_remote_copy(src_ref=ins[a], dst_ref=outs[a], send_sem=send_sems.at[a], recv_sem=recv_sems.at[a],
                                            device_id=(x, y, 1 - cc), device_id_type=MESH) for a in range(n)]
        for cp in cps:
            cp.start()
        for cp in cps:
            cp.wait()

    return pl.pallas_call(
        body, name="pair_exchange", in_specs=[ANY] * n, out_specs=[ANY] * n,
        out_shape=[jax.ShapeDtypeStruct(a.shape, a.dtype) for a in arrays],
        scratch_shapes=[pltpu.SemaphoreType.DMA((n,)), pltpu.SemaphoreType.DMA((n,))])(*arrays)


def _chip_exchange(arrays):
    n = len(arrays)

    def body(*refs):
        ins, outs, (send_sems, recv_sems) = refs[:n], refs[n:2 * n], refs[2 * n:]
        x, y, cc = _place()
        cps = [pltpu.make_async_remote_copy(src_ref=ins[a].at[2 * px + py], dst_ref=outs[a].at[k], send_sem=send_sems.at[a, k],
                                            recv_sem=recv_sems.at[a, k], device_id=(px, py, cc), device_id_type=MESH)
               for k, (px, py) in enumerate(_other_chips(x, y)) for a in range(n)]
        for cp in cps:
            cp.start()
        for cp in cps:
            cp.wait()

    return pl.pallas_call(
        body, name="chip_exchange", in_specs=[ANY] * n, out_specs=[ANY] * n,
        out_shape=[jax.ShapeDtypeStruct((3,) + a.shape[1:], a.dtype) for a in arrays],
        scratch_shapes=[pltpu.SemaphoreType.DMA((n, 3)), pltpu.SemaphoreType.DMA((n, 3))])(*arrays)


def _allreduce8(arrays):
    n = len(arrays)

    def body(*refs):
        ins, outs, recvs = refs[:n], refs[n:2 * n], refs[2 * n:3 * n]
        send_sems, recv_sems = refs[3 * n:]
        x, y, cc = _place()
        for a in range(n):
            outs[a][...] = ins[a][...]
        for k, peer in enumerate([(x, y, 1 - cc), (1 - x, y, cc), (x, 1 - y, cc)]):
            cps = [pltpu.make_async_remote_copy(src_ref=outs[a], dst_ref=recvs[a].at[k], send_sem=send_sems.at[a, k],
                                                recv_sem=recv_sems.at[a, k], device_id=peer, device_id_type=MESH) for a in range(n)]
            for cp in cps:
                cp.start()
            for cp in cps:
                cp.wait()
            for a in range(n):
                outs[a][...] = outs[a][...] + recvs[a][k]

    return pl.pallas_call(
        body, name="allreduce8", in_specs=[VMEM] * n, out_specs=[VMEM] * n,
        out_shape=[jax.ShapeDtypeStruct(a.shape, F32) for a in arrays],
        scratch_shapes=[pltpu.VMEM((3,) + a.shape, F32) for a in arrays] + [pltpu.SemaphoreType.DMA((n, 3)), pltpu.SemaphoreType.DMA((n, 3))],
        compiler_params=_params())(*arrays)


def _row_tile(rows, cols, limit_bytes=1 << 20):
    t = rows
    while t % 16 == 0 and t * cols * 4 > limit_bytes:
        t //= 2
    return t


def _add_pair(a, b):
    r, c = a.shape
    t = _row_tile(r, c)

    def body(a_ref, b_ref, o_ref):
        o_ref[...] = (a_ref[...].astype(F32) + b_ref[...].astype(F32)).astype(o_ref.dtype)

    spec = pl.BlockSpec((t, c), lambda i: (i, 0))
    return pl.pallas_call(body, name="add_pair", grid=(r // t,), in_specs=[spec, spec], out_specs=spec,
                          out_shape=jax.ShapeDtypeStruct((r, c), a.dtype), compiler_params=_params("arbitrary"))(a, b)


def _sum_chips(own, recv):
    r, c = own.shape
    t = _row_tile(r, c)

    def body(o_ref, r_ref, out_ref):
        acc = o_ref[...].astype(F32)
        for k in range(3):
            acc = acc + r_ref[k].astype(F32)
        out_ref[...] = acc

    return pl.pallas_call(
        body, name="sum_chips", grid=(r // t,),
        in_specs=[pl.BlockSpec((t, c), lambda i: (i, 0)), pl.BlockSpec((3, t, c), lambda i: (0, i, 0))],
        out_specs=pl.BlockSpec((t, c), lambda i: (i, 0)), out_shape=jax.ShapeDtypeStruct((r, c), F32),
        compiler_params=_params("arbitrary"))(own, recv)


def _adamw_update(w_ref, m_ref, v_ref, g_ref, d_ref, mo_ref, vo_ref):
    gv = g_ref[...]
    mn = ADAM_B1 * m_ref[...] + (1.0 - ADAM_B1) * gv
    vn = ADAM_B2 * v_ref[...] + (1.0 - ADAM_B2) * (gv * gv)
    m_hat = mn / (1.0 - ADAM_B1 ** ADAM_STEP)
    v_hat = vn / (1.0 - ADAM_B2 ** ADAM_STEP)
    d_ref[...] = -ADAM_LR * (m_hat / (jnp.sqrt(v_hat) + ADAM_EPS) + ADAM_WD * w_ref[...])
    mo_ref[...] = mn
    vo_ref[...] = vn


def _adamw(w, m, v, g):
    r, c = w.shape
    t = _row_tile(r, c)

    def body(*refs):
        _adamw_update(*refs)

    spec = pl.BlockSpec((t, c), lambda i: (i, 0))
    shape = jax.ShapeDtypeStruct((r, c), F32)
    return pl.pallas_call(body, name="adamw", grid=(r // t,), in_specs=[spec] * 4, out_specs=[spec] * 3,
                          out_shape=[shape] * 3, compiler_params=_params("arbitrary"))(w, m, v, g)


def _adamw_small(ws, ms, vs, gs):
    n = len(ws)

    def body(*refs):
        w_refs, m_refs, v_refs, g_refs = refs[:n], refs[n:2 * n], refs[2 * n:3 * n], refs[3 * n:4 * n]
        d_refs, mo_refs, vo_refs = refs[4 * n:5 * n], refs[5 * n:6 * n], refs[6 * n:]
        for a in range(n):
            _adamw_update(w_refs[a], m_refs[a], v_refs[a], g_refs[a], d_refs[a], mo_refs[a], vo_refs[a])

    shapes = [jax.ShapeDtypeStruct(w.shape, F32) for w in ws]
    outs = pl.pallas_call(body, name="adamw_small", in_specs=[VMEM] * (4 * n), out_specs=[VMEM] * (3 * n),
                          out_shape=shapes * 3, compiler_params=_params())(*ws, *ms, *vs, *gs)
    return outs[:n], outs[n:2 * n], outs[2 * n:]


SMALL =("norm_g", "f_bias", "sgu_w", "sgu_b", "sgu_ln_g", "sgu_ln_b", "short_conv_w", "conf_dw_w", "conf_dw_b",
         "conf_ln_g", "conf_ln_b", "final_g")
WEIGHTS = ("norm_g", "w_in", "f_bias", "sgu_w", "sgu_b", "sgu_ln_g", "sgu_ln_b", "short_conv_w", "conf_dw_w",
           "conf_dw_b", "conf_ln_g", "conf_ln_b", "w_branch", "w_out", "final_g")


def _aligned_pieces():
    out, pos = [], 0
    for src, width, dst in sorted(_SEGMENTS, key=lambda t: t[2]):
        if dst > pos:
            out.append((None, 0, dst - pos))
        lo = src
        while lo < src + width:
            j = lo // SHARD_COLS
            hi = min(src + width, (j + 1) * SHARD_COLS)
            out.append((j, lo - j * SHARD_COLS, hi - lo))
            lo = hi
        pos = dst + width
    return out


def _shard_pieces(j):
    lo_s, hi_s = j * SHARD_COLS, (j + 1) * SHARD_COLS
    out = []
    for src, width, dst in _SEGMENTS:
        lo, hi = max(src, lo_s), min(src + width, hi_s)
        if lo < hi:
            out.append((dst + lo - src, hi - lo))
    return out


def _w_in_aligned(shards):
    rows, dtype = shards[0].shape[0], shards[0].dtype
    return jnp.concatenate([jnp.zeros((rows, w), dtype) if j is None else shards[j][:, c0:c0 + w]
                            for j, c0, w in _aligned_pieces()], axis=1)


def _w_in_shard(g, j):
    parts = [g[:, c0:c0 + w] for c0, w in _shard_pieces(j)]
    return jnp.concatenate(parts + [jnp.zeros((g.shape[0], SHARD_PAD - SHARD_COLS), g.dtype)], axis=1)


def _by_layer(pc, own, other):
    return jnp.where(pc == 0, jnp.stack([own, other]), jnp.stack([other, own]))


def kernel(x, norm_g, w_in, f_bias, sgu_w, sgu_b, sgu_ln_g, sgu_ln_b, short_conv_w, conf_dw_w, conf_dw_b, conf_ln_g, conf_ln_b, w_branch, w_out, final_g, loss_target, m_norm_g, m_w_in, m_f_bias, m_sgu_w, m_sgu_b, m_sgu_ln_g, m_sgu_ln_b, m_short_conv_w, m_conf_dw_w, m_conf_dw_b, m_conf_ln_g, m_conf_ln_b, m_w_branch, m_w_out, m_final_g, v_norm_g, v_w_in, v_f_bias, v_sgu_w, v_sgu_b, v_sgu_ln_g, v_sgu_ln_b, v_short_conv_w, v_conf_dw_w, v_conf_dw_b, v_conf_ln_g, v_conf_ln_b, v_w_branch, v_w_out, v_final_g):
    px, py, pc = _place()
    chip = 2 * px + py
    depth = w_in.shape[0]
    w = dict(norm_g=norm_g, w_in=w_in, f_bias=f_bias, sgu_w=sgu_w, sgu_b=sgu_b, sgu_ln_g=sgu_ln_g, sgu_ln_b=sgu_ln_b,
             short_conv_w=short_conv_w, conf_dw_w=conf_dw_w, conf_dw_b=conf_dw_b, conf_ln_g=conf_ln_g, conf_ln_b=conf_ln_b,
             w_branch=w_branch, w_out=w_out, final_g=final_g)
    m = dict(norm_g=m_norm_g, w_in=m_w_in, f_bias=m_f_bias, sgu_w=m_sgu_w, sgu_b=m_sgu_b, sgu_ln_g=m_sgu_ln_g,
             sgu_ln_b=m_sgu_ln_b, short_conv_w=m_short_conv_w, conf_dw_w=m_conf_dw_w, conf_dw_b=m_conf_dw_b,
             conf_ln_g=m_conf_ln_g, conf_ln_b=m_conf_ln_b, w_branch=m_w_branch, w_out=m_w_out, final_g=m_final_g)
    v = dict(norm_g=v_norm_g, w_in=v_w_in, f_bias=v_f_bias, sgu_w=v_sgu_w, sgu_b=v_sgu_b, sgu_ln_g=v_sgu_ln_g,
             sgu_ln_b=v_sgu_ln_b, short_conv_w=v_short_conv_w, conf_dw_w=v_conf_dw_w, conf_dw_b=v_conf_dw_b,
             conf_ln_g=v_conf_ln_g, conf_ln_b=v_conf_ln_b, w_branch=v_w_branch, w_out=v_w_out, final_g=v_final_g)

    local = (jnp.pad(w_in, ((0, 0), (0, 0), (0, SHARD_PAD - SHARD_COLS))).astype(BF16),
             w_branch.astype(BF16).reshape(depth, N_HEADS * BR, BR), w_out.astype(BF16))
    pick = lambda a, i: lax.dynamic_index_in_dim(a, i, 0, keepdims=False)
    gathered = _gather_chips([pick(a, pc) for a in local])
    wi_all, wb_all, wo_all = [lax.dynamic_update_index_in_dim(got, mine, chip, 0) for got, mine in zip(gathered, local)]
    conv_ch = BR // N_CHIPS
    place = lambda a: lax.dynamic_update_slice_in_dim(jnp.zeros(a.shape[:-1] + (BR,), F32), a * (pc == 0).astype(F32),
                                                      conv_ch * chip, axis=2).reshape(-1, BR)
    short_full, conf_full = _allreduce8([place(short_conv_w), place(conf_dw_w)])
    short_full, conf_full = short_full.reshape(depth, K_SHORT, BR), conf_full.reshape(depth, K_CONF, BR)

    layers = []
    for l in range(depth):
        wb = jnp.concatenate([wb_all[j, l].reshape(N_HEADS, BR, BR) for j in range(N_CHIPS)], axis=2)
        wo = jnp.concatenate([wo_all[j, l] for j in range(N_CHIPS)], axis=0)
        layers.append(dict(
            norm_g=norm_g[l][None], w_in=_w_in_aligned([wi_all[j, l] for j in range(N_CHIPS)]),
            f_bias=jnp.pad(f_bias[l], (0, CHUNK - N_HEADS))[None],
            sgu_w=sgu_w[l], sgu_b=sgu_b[l], sgu_ln_g=sgu_ln_g[l][None], sgu_ln_b=sgu_ln_b[l][None],
            short_conv_w=short_full[l], conf_dw_w=conf_full[l], conf_dw_b=conf_dw_b[l][None],
            conf_ln_g=conf_ln_g[l][None], conf_ln_b=conf_ln_b[l][None], w_branch=wb, w_out=wo))

    loss_local, grad_x, grads, d_final_g = _local_step(x[0], loss_target[0], layers, final_g[None])
    loss = lax.psum(loss_local, ("x", "y", "c"))

    g2 = (jnp.stack([jnp.stack([_w_in_shard(grads[l]["w_in"], j) for j in range(N_CHIPS)]) for l in range(depth)]),
          jnp.stack([jnp.stack([grads[l]["w_branch"][:, :, j * BR:(j + 1) * BR].reshape(N_HEADS * BR, BR).astype(BF16)
                                for j in range(N_CHIPS)]) for l in range(depth)]),
          jnp.stack([jnp.stack([grads[l]["w_out"][j * BR:(j + 1) * BR].astype(BF16) for j in range(N_CHIPS)]) for l in range(depth)]))
    flat = lambda a: a.reshape(-1, a.shape[-1])
    from_sibling = _pair_exchange([flat(pick(a, 1 - pc)) for a in g2])
    h = [_add_pair(flat(pick(a, pc)), r).reshape(a.shape[1:]) for a, r in zip(g2, from_sibling)]
    summed = [_sum_chips(pick(a, chip), r) for a, r in zip(h, _chip_exchange(h))]
    gi, gb, go = [_by_layer(pc, own, other) for own, other in zip(summed, _pair_exchange(summed))]
    g = dict(w_in=gi[:, :, :SHARD_COLS], w_branch=gb.reshape(depth, N_HEADS, BR, BR), w_out=go)

    two_d = lambda a: a.reshape(-1, a.shape[-1])
    small_local = [jnp.stack([grads[l][n] for l in range(depth)]) for n in SMALL[:-1]] + [d_final_g]
    for n, a, b in zip(SMALL, small_local, _allreduce8([two_d(a) for a in small_local])):
        g[n] = b.reshape(a.shape)
    for n in ("short_conv_w", "conf_dw_w"):
        g[n] = lax.dynamic_slice_in_dim(g[n], conv_ch * chip, conv_ch, axis=2)

    delta, new_m, new_v = {}, {}, {}
    for n in ("w_in", "w_branch", "w_out"):
        shp = w[n].shape
        d_, m_, v_ = _adamw(two_d(w[n]), two_d(m[n]), two_d(v[n]), two_d(g[n]))
        delta[n], new_m[n], new_v[n] = d_.reshape(shp), m_.reshape(shp), v_.reshape(shp)
    d_, m_, v_ = _adamw_small(*[[two_d(t[n]) for n in SMALL] for t in (w, m, v, g)])
    for n, a, b, c_ in zip(SMALL, d_, m_, v_):
        delta[n], new_m[n], new_v[n] = a.reshape(w[n].shape), b.reshape(w[n].shape), c_.reshape(w[n].shape)

    return (loss, grad_x[None], *[g[n] for n in WEIGHTS], *[delta[n] for n in WEIGHTS],
            *[new_m[n] for n in WEIGHTS], *[new_v[n] for n in WEIGHTS])
```

```python
import functools
import math

import jax
import jax.numpy as jnp
from jax import lax
from jax.experimental import pallas as pl
from jax.experimental.pallas import tpu as pltpu

F32 = jnp.float32
BF16 = jnp.bfloat16

D_MODEL = 1024
BR = 256
N_HEADS = 4
HEAD_DIM = 64
CHUNK = 128
K_SHORT = 3
K_CONF = 31
EPS = 1e-6
IN_COLS = 7684
SHARD_COLS = IN_COLS // 4
SHARD_PAD = 2048
N_CHIPS = 4

QKV0, CIN0, AUV0, DGLU0 = 0, 768, 1536, 2048
AG0, BG0, CG0, DG0 = 2560, 2816, 3072, 3328
BF0 = 3584
MRG0 = 4096
PCOLS = 8192

V7X_VMEM_BYTES = 64 * 1024 * 1024
VMEM_LIMIT = V7X_VMEM_BYTES * 7 // 8

ADAM_LR, ADAM_B1, ADAM_B2, ADAM_EPS, ADAM_WD, ADAM_STEP = 0.001, 0.9, 0.999, 1e-08, 0.01, 10

MESH = pl.DeviceIdType.MESH
ANY = pl.BlockSpec(memory_space=pl.ANY)
VMEM = pl.BlockSpec(memory_space=pltpu.VMEM)

GELU_C0 = math.sqrt(2.0 / math.pi)
GELU_C1 = 0.044715


def _params(*sem):
    return pltpu.CompilerParams(dimension_semantics=sem, vmem_limit_bytes=VMEM_LIMIT)


def _sigmoid(x):
    return 0.5 * jnp.tanh(0.5 * x) + 0.5


def _silu_and_grad(x):
    s = _sigmoid(x)
    return x * s, s * (1.0 + x * (1.0 - s))


def _gelu_and_grad(z):
    z2 = z * z
    t = jnp.tanh(GELU_C0 * (z + GELU_C1 * z2 * z))
    half = 0.5 * (1.0 + t)
    return z * half, half + 0.5 * z * (1.0 - t * t) * (GELU_C0 * (1.0 + 3.0 * GELU_C1 * z2))


def _ln_fwd(v):
    mu = jnp.mean(v, axis=-1, keepdims=True)
    xc = v - mu
    rs = lax.rsqrt(jnp.mean(xc * xc, axis=-1, keepdims=True) + EPS)
    return xc * rs, rs


def _ln_bwd(d_xh, xh, rs):
    return rs * (d_xh - jnp.mean(d_xh, axis=-1, keepdims=True) - xh * jnp.mean(d_xh * xh, axis=-1, keepdims=True))


def _part8(a):
    return a.reshape(a.shape[0] // 8, 8, a.shape[1]).sum(axis=0)


def _dot(a, b):
    return jnp.dot(a, b, preferred_element_type=F32)


def _dot_nt(a, b):
    return lax.dot_general(a, b, (((1,), (1,)), ((), ())), preferred_element_type=F32)


def _dot_tn(a, b):
    return lax.dot_general(a, b, (((0,), (0,)), ((), ())), preferred_element_type=F32)


def _head_masks(dtype):
    lane = lax.broadcasted_iota(jnp.int32, (1, BR), 1) // HEAD_DIM
    return [(lane == h).astype(dtype) for h in range(N_HEADS)]


def _window(ref, col0, width):
    return ref.at[:, pl.ds(col0, width)]


def _copy_all(pairs, sems):
    cps = [pltpu.make_async_copy(s, d, sems.at[i]) for i, (s, d) in enumerate(pairs)]
    for cp in cps:
        cp.start()
    for cp in cps:
        cp.wait()


def _place():
    return lax.axis_index("x"), lax.axis_index("y"), lax.axis_index("c")


def _other_chips(x, y):
    return [(1 - x, y), (x, 1 - y), (1 - x, 1 - y)]


class _Rider:
    def __init__(self, srcs, out_shapes, per_array, make, through=False):
        self.srcs, self.out_shapes, self.per_array, self.make, self.through = list(srcs), list(out_shapes), per_array, make, through
        self.n = len(self.srcs)

    def scratch(self):
        return [pltpu.SemaphoreType.DMA((self.n, self.per_array)), pltpu.SemaphoreType.DMA((self.n, self.per_array))]


def _ride(rider, first, last, src_refs, dst_refs, sems):
    if rider is None:
        return (lambda: None), (lambda: None)
    cps = rider.make(src_refs, dst_refs, *sems)

    def guarded(cond, fn):
        if cond is True:
            fn()
        else:
            pl.when(cond)(fn)

    def start():
        guarded(first, lambda: [cp.start() for cp in cps] and None)

    def finish():
        guarded(last, lambda: [cp.wait() for cp in cps] and None)

    return start, finish


def _rider_parts(rider, n_in, n_out):
    if rider is None:
        return [], [], [], [], [], {}
    aliases = {n_in + a: n_out + a for a in range(rider.n)} if rider.through else {}
    return rider.srcs, [ANY] * rider.n, [ANY] * rider.n, rider.out_shapes, rider.scratch(), aliases


def _remote(src, dst, send_sems, recv_sems, a, k, to):
    return pltpu.make_async_remote_copy(src_ref=src, dst_ref=dst, send_sem=send_sems.at[a, k], recv_sem=recv_sems.at[a, k],
                                        device_id=to, device_id_type=MESH)


def _gather_send_rider(blocks):
    def make(srcs, dsts, ss, rs):
        x, y, cc = _place()
        return [_remote(srcs[a], dsts[a].at[2 * x + y, cc], ss, rs, a, j, (*chip, cc))
                for j, chip in enumerate(_other_chips(x, y)) for a in range(len(srcs))]
    shapes = [jax.ShapeDtypeStruct((N_CHIPS, 2) + b.shape, b.dtype) for b in blocks]
    return _Rider(blocks, shapes, 3, make)


def _gather_forward_rider(landed):
    def make(srcs, dsts, ss, rs):
        x, y, cc = _place()
        return [_remote(dsts[a].at[2 * px + py, cc], dsts[a].at[2 * px + py, cc], ss, rs, a, j, (x, y, 1 - cc))
                for j, (px, py) in enumerate(_other_chips(x, y)) for a in range(len(dsts))]
    shapes = [jax.ShapeDtypeStruct(b.shape, b.dtype) for b in landed]
    return _Rider(landed, shapes, 3, make, through=True)


def _pair_rider(arrays):
    def make(srcs, dsts, ss, rs):
        x, y, cc = _place()
        return [_remote(srcs[a], dsts[a], ss, rs, a, 0, (x, y, 1 - cc)) for a in range(len(srcs))]
    return _Rider(arrays, [jax.ShapeDtypeStruct(b.shape, b.dtype) for b in arrays], 1, make)


def _chip_rider(arrays):
    def make(srcs, dsts, ss, rs):
        x, y, cc = _place()
        return [_remote(srcs[a].at[2 * px + py], dsts[a].at[k], ss, rs, a, k, (px, py, cc))
                for k, (px, py) in enumerate(_other_chips(x, y)) for a in range(len(srcs))]
    return _Rider(arrays, [jax.ShapeDtypeStruct((3,) + b.shape[1:], b.dtype) for b in arrays], 3, make)


def _inproj_fwd(x, g, w):
    s = x.shape[0]
    tm, tn = min(1024, s), 1024

    def body(x_ref, g_ref, w_ref, proj_ref, h_ref):
        @pl.when(pl.program_id(1) == 0)
        def _():
            xv = x_ref[...]
            r = lax.rsqrt(jnp.mean(xv * xv, axis=-1, keepdims=True) + EPS)
            h_ref[...] = ((xv * r) * g_ref[...]).astype(BF16)
        proj_ref[...] = _dot(h_ref[...], w_ref[...])

    return pl.pallas_call(
        body, name="inproj_fwd", grid=(s // tm, PCOLS // tn),
        in_specs=[pl.BlockSpec((tm, D_MODEL), lambda i, j: (i, 0)), pl.BlockSpec((1, D_MODEL), lambda i, j: (0, 0)),
                  pl.BlockSpec((D_MODEL, tn), lambda i, j: (0, j))],
        out_specs=[pl.BlockSpec((tm, tn), lambda i, j: (i, j)), pl.BlockSpec((tm, D_MODEL), lambda i, j: (i, 0))],
        out_shape=[jax.ShapeDtypeStruct((s, PCOLS), F32), jax.ShapeDtypeStruct((s, D_MODEL), BF16)],
        compiler_params=_params("arbitrary", "arbitrary"))(x, g, w)


def _rms_bwd(dh, x, g):
    r = lax.rsqrt(jnp.mean(x * x, axis=-1, keepdims=True) + EPS)
    xn = x * r
    gy = dh * g
    dx = r * (gy - xn * jnp.mean(xn * gy, axis=-1, keepdims=True))
    return dx, _part8(dh * xn)


def _inproj_bwd_x(dproj, w, x, dxn, g):
    s = x.shape[0]
    tm, tk = min(512, s), 4096
    nk = PCOLS // tk

    def body(dp_ref, w_ref, x_ref, dxn_ref, g_ref, dx_ref, dg_ref, acc_ref):
        i, k = pl.program_id(0), pl.program_id(1)

        @pl.when(k == 0)
        def _():
            acc_ref[...] = jnp.zeros_like(acc_ref)

        @pl.when((i == 0) & (k == 0))
        def _():
            dg_ref[...] = jnp.zeros_like(dg_ref)

        acc_ref[...] += _dot_nt(dp_ref[...], w_ref[...])

        @pl.when(k == nk - 1)
        def _():
            dx, dg8 = _rms_bwd(acc_ref[...], x_ref[...], g_ref[...])
            dx_ref[...] = dxn_ref[...] + dx
            dg_ref[...] += dg8

    dx, dg8 = pl.pallas_call(
        body, name="inproj_bwd_x", grid=(s // tm, nk),
        in_specs=[pl.BlockSpec((tm, tk), lambda i, k: (i, k)), pl.BlockSpec((D_MODEL, tk), lambda i, k: (0, k)),
                  pl.BlockSpec((tm, D_MODEL), lambda i, k: (i, 0)), pl.BlockSpec((tm, D_MODEL), lambda i, k: (i, 0)),
                  pl.BlockSpec((1, D_MODEL), lambda i, k: (0, 0))],
        out_specs=[pl.BlockSpec((tm, D_MODEL), lambda i, k: (i, 0)), pl.BlockSpec((8, D_MODEL), lambda i, k: (0, 0))],
        out_shape=[jax.ShapeDtypeStruct((s, D_MODEL), F32), jax.ShapeDtypeStruct((8, D_MODEL), F32)],
        scratch_shapes=[pltpu.VMEM((tm, D_MODEL), F32)],
        compiler_params=_params("arbitrary", "arbitrary"))(dproj, w, x, dxn, g)
    return dx, dg8


def _inproj_bwd_w(h, dproj):
    s = h.shape[0]
    tn, tk = 1024, min(1024, s)
    nk = s // tk

    def body(h_ref, dp_ref, dw_ref, acc_ref):
        k = pl.program_id(1)

        @pl.when(k == 0)
        def _():
            acc_ref[...] = jnp.zeros_like(acc_ref)

        acc_ref[...] += _dot_tn(h_ref[...], dp_ref[...])

        @pl.when(k == nk - 1)
        def _():
            dw_ref[...] = acc_ref[...].astype(BF16)

    return pl.pallas_call(
        body, name="inproj_bwd_w", grid=(PCOLS // tn, nk),
        in_specs=[pl.BlockSpec((tk, D_MODEL), lambda j, k: (k, 0)), pl.BlockSpec((tk, tn), lambda j, k: (k, j))],
        out_specs=pl.BlockSpec((D_MODEL, tn), lambda j, k: (0, j)),
        out_shape=jax.ShapeDtypeStruct((D_MODEL, PCOLS), BF16),
        scratch_shapes=[pltpu.VMEM((D_MODEL, tn), F32)],
        compiler_params=_params("arbitrary", "arbitrary"))(h, dproj)


def _mix_a_chunk(uvp, agp, wm_ref, bias, lg, lb):
    u, du = _gelu_and_grad(uvp[:, :BR])
    v, dv = _gelu_and_grad(uvp[:, BR:])
    xh, rs = _ln_fwd(v)
    vnb = (xh * lg + lb).astype(BF16)
    masks = _head_masks(BF16)
    mixed = bias
    for h in range(N_HEADS):
        mixed = mixed + _dot(wm_ref[h], vnb * masks[h])
    sg, dsg = _silu_and_grad(agp)
    return u, du, dv, xh, rs, vnb, masks, mixed, sg, dsg


def _store_masked_sgu(sw_ref, wm_ref):
    row = lax.broadcasted_iota(jnp.int32, (CHUNK, CHUNK), 0)
    col = lax.broadcasted_iota(jnp.int32, (CHUNK, CHUNK), 1)
    for h in range(N_HEADS):
        wm_ref[h] = jnp.where(row >= col, sw_ref[h], 0.0).astype(BF16)


def _mix_a_fwd(proj, sgu_w, bias, lg, lb):
    s = proj.shape[0]

    def body(proj_ref, sw_ref, bias_ref, lg_ref, lb_ref, y_ref, uv_buf, ag_buf, wm_ref, sems):
        _copy_all([(_window(proj_ref, AUV0, 2 * BR), uv_buf), (_window(proj_ref, AG0, BR), ag_buf)], sems)
        _store_masked_sgu(sw_ref, wm_ref)

        def chunk(i, c):
            rows = pl.ds(pl.multiple_of(i * CHUNK, CHUNK), CHUNK)
            u, _, _, _, _, _, _, mixed, sg, _ = _mix_a_chunk(uv_buf[rows, :], ag_buf[rows, :], wm_ref, bias_ref[...],
                                                            lg_ref[...], lb_ref[...])
            y_ref[rows, :] = (u * mixed * sg).astype(BF16)
            return c

        lax.fori_loop(0, s // CHUNK, chunk, 0)

    return pl.pallas_call(
        body, name="mix_a_fwd", in_specs=[ANY, VMEM, VMEM, VMEM, VMEM], out_specs=VMEM,
        out_shape=jax.ShapeDtypeStruct((s, BR), BF16),
        scratch_shapes=[pltpu.VMEM((s, 2 * BR), F32), pltpu.VMEM((s, BR), F32), pltpu.VMEM((N_HEADS, CHUNK, CHUNK), BF16),
                        pltpu.SemaphoreType.DMA((2,))],
        compiler_params=_params())(proj, sgu_w, bias, lg, lb)


def _mix_a_bwd(proj, dproj, dy, sgu_w, bias, lg, lb):
    s = proj.shape[0]

    def body(proj_ref, dproj_in, dy_ref, sw_ref, bias_ref, lg_ref, lb_ref,
             dproj_ref, dsw_ref, dbias_ref, dlg_ref, dlb_ref,
             uv_buf, ag_buf, duv_buf, dag_buf, wm_ref, acc_lg, acc_lb, sems):
        del dproj_in
        _copy_all([(_window(proj_ref, AUV0, 2 * BR), uv_buf), (_window(proj_ref, AG0, BR), ag_buf)], sems)
        _store_masked_sgu(sw_ref, wm_ref)
        dsw_ref[...] = jnp.zeros_like(dsw_ref)
        dbias_ref[...] = jnp.zeros_like(dbias_ref)
        acc_lg[...] = jnp.zeros_like(acc_lg)
        acc_lb[...] = jnp.zeros_like(acc_lb)

        def chunk(i, c):
            rows = pl.ds(pl.multiple_of(i * CHUNK, CHUNK), CHUNK)
            lg_v = lg_ref[...]
            u, du, dv, xh, rs, vnb, masks, mixed, sg, dsg = _mix_a_chunk(
                uv_buf[rows, :], ag_buf[rows, :], wm_ref, bias_ref[...], lg_v, lb_ref[...])
            dyv = dy_ref[rows, :]
            t1 = dyv * sg
            d_u = t1 * mixed
            d_mixed = t1 * u
            d_ag = dyv * u * mixed * dsg
            dbias_ref[...] += d_mixed
            dmb = d_mixed.astype(BF16)
            d_vn = jnp.zeros((CHUNK, BR), F32)
            for h in range(N_HEADS):
                dm_h = dmb * masks[h]
                dsw_ref[h] += _dot_nt(dm_h, vnb)
                d_vn = d_vn + _dot_tn(wm_ref[h], dm_h)
            acc_lg[...] += _part8(d_vn * xh)
            acc_lb[...] += _part8(d_vn)
            d_v = _ln_bwd(d_vn * lg_v, xh, rs)
            duv_buf[rows, :] = jnp.concatenate([d_u * du, d_v * dv], axis=1).astype(BF16)
            dag_buf[rows, :] = d_ag.astype(BF16)
            return c

        lax.fori_loop(0, s // CHUNK, chunk, 0)
        row = lax.broadcasted_iota(jnp.int32, (CHUNK, CHUNK), 0)
        col = lax.broadcasted_iota(jnp.int32, (CHUNK, CHUNK), 1)
        for h in range(N_HEADS):
            dsw_ref[h] = jnp.where(row >= col, dsw_ref[h], 0.0)
        dlg_ref[...] = jnp.sum(acc_lg[...], axis=0, keepdims=True)
        dlb_ref[...] = jnp.sum(acc_lb[...], axis=0, keepdims=True)
        _copy_all([(duv_buf, _window(dproj_ref, AUV0, 2 * BR)), (dag_buf, _window(dproj_ref, AG0, BR))], sems)

    return pl.pallas_call(
        body, name="mix_a_bwd", in_specs=[ANY, ANY, VMEM, VMEM, VMEM, VMEM, VMEM],
        out_specs=[ANY, VMEM, VMEM, VMEM, VMEM],
        out_shape=[jax.ShapeDtypeStruct(dproj.shape, dproj.dtype), jax.ShapeDtypeStruct((N_HEADS, CHUNK, CHUNK), F32),
                   jax.ShapeDtypeStruct((CHUNK, BR), F32), jax.ShapeDtypeStruct((1, BR), F32), jax.ShapeDtypeStruct((1, BR), F32)],
        scratch_shapes=[pltpu.VMEM((s, 2 * BR), F32), pltpu.VMEM((s, BR), F32), pltpu.VMEM((s, 2 * BR), BF16),
                        pltpu.VMEM((s, BR), BF16), pltpu.VMEM((N_HEADS, CHUNK, CHUNK), BF16),
                        pltpu.VMEM((8, BR), F32), pltpu.VMEM((8, BR), F32), pltpu.SemaphoreType.DMA((2,))],
        input_output_aliases={1: 0}, compiler_params=_params())(proj, dproj, dy, sgu_w, bias, lg, lb)


def _tri_ones(n, upper):
    row = lax.broadcasted_iota(jnp.int32, (n, n), 0)
    col = lax.broadcasted_iota(jnp.int32, (n, n), 1)
    return ((row <= col) if upper else (row >= col)).astype(BF16)


def _split3(c):
    hi = c.astype(BF16)
    r1 = c - hi.astype(F32)
    mid = r1.astype(BF16)
    lo = (r1 - mid.astype(F32)).astype(BF16)
    return [hi, mid, lo]


def _tri_sum(tri, a):
    parts = _split3(a)
    return _dot(tri, parts[0]) + _dot(tri, parts[1]) + _dot(tri, parts[2])


EXT = 2 * HEAD_DIM
LANE_CQ = HEAD_DIM
LANE_CK = HEAD_DIM + 3


def _to_head(h):
    r = lax.broadcasted_iota(jnp.int32, (BR, EXT), 0)
    c = lax.broadcasted_iota(jnp.int32, (BR, EXT), 1)
    return ((r == c + h * HEAD_DIM) & (c < HEAD_DIM)).astype(BF16)


def _from_head(h):
    r = lax.broadcasted_iota(jnp.int32, (EXT, BR), 0)
    c = lax.broadcasted_iota(jnp.int32, (EXT, BR), 1)
    return ((c == r + h * HEAD_DIM) & (r < HEAD_DIM)).astype(BF16)


def _attn_prep_fwd(proj, f_bias):
    s = proj.shape[0]
    scale = 1.0 / math.sqrt(HEAD_DIM)

    def body(proj_ref, fb_ref, qe_ref, ke_ref, ve_ref, ket_ref, vet_ref, qkv_buf, z_buf, sems):
        _copy_all([(_window(proj_ref, QKV0, 3 * BR), qkv_buf), (_window(proj_ref, BF0, CHUNK), z_buf)], sems)
        tri = _tri_ones(CHUNK, upper=False)
        lane = lax.broadcasted_iota(jnp.int32, (CHUNK, EXT), 1)
        ones_q = ((lane >= LANE_CK) & (lane < LANE_CK + 3)).astype(F32)
        ones_k = ((lane >= LANE_CQ) & (lane < LANE_CQ + 3)).astype(F32)

        def chunk(i, carry):
            rows = pl.ds(pl.multiple_of(i * CHUNK, CHUNK), CHUNK)
            cum = _tri_sum(tri, jax.nn.log_sigmoid(z_buf[rows, :] + fb_ref[...])) + carry
            qb = (qkv_buf[rows, 0:BR] * scale).astype(BF16)
            kb = qkv_buf[rows, BR:2 * BR].astype(BF16)
            vb = qkv_buf[rows, 2 * BR:].astype(BF16)
            parts = [p.astype(F32) for p in _split3(cum)]
            for h in range(N_HEADS):
                sel = _to_head(h)
                dec_q, dec_k = ones_q, ones_k
                for t, part in enumerate(parts):
                    pf = part[:, h:h + 1]
                    dec_q = dec_q + jnp.where(lane == LANE_CQ + t, pf, 0.0)
                    dec_k = dec_k - jnp.where(lane == LANE_CK + t, pf, 0.0)
                qe_ref[h, rows, :] = (_dot(qb, sel) + dec_q).astype(BF16)
                kh = _dot(kb, sel) + dec_k
                vh = _dot(vb, sel)
                ke_ref[h, rows, :] = kh.astype(BF16)
                ve_ref[h, rows, :] = vh.astype(BF16)
                ket_ref[h, :, rows] = kh.T.astype(BF16)
                vet_ref[h, :, rows] = vh.T.astype(BF16)
            return cum[CHUNK - 1:CHUNK, :]

        lax.fori_loop(0, s // CHUNK, chunk, jnp.zeros((1, CHUNK), F32))

    shape = jax.ShapeDtypeStruct((N_HEADS, s, EXT), BF16)
    shape_t = jax.ShapeDtypeStruct((N_HEADS, EXT, s), BF16)
    return pl.pallas_call(
        body, name="attn_prep_fwd", in_specs=[ANY, VMEM], out_specs=[VMEM] * 5, out_shape=[shape, shape, shape, shape_t, shape_t],
        scratch_shapes=[pltpu.VMEM((s, 3 * BR), F32), pltpu.VMEM((s, CHUNK), F32), pltpu.SemaphoreType.DMA((2,))],
        compiler_params=_params())(proj, f_bias)


def _attn_prep_bwd(proj, dproj, dqe, dke, dve, f_bias):
    s = proj.shape[0]
    pad = MRG0 - BF0
    scale = 1.0 / math.sqrt(HEAD_DIM)

    def body(proj_ref, dproj_in, dq_ref, dk_ref, dv_ref, fb_ref, dproj_ref, dfb_ref, z_buf, dqkv_buf, dz_buf, sems):
        del dproj_in
        _copy_all([(_window(proj_ref, BF0, CHUNK), z_buf)], sems)
        tri = _tri_ones(CHUNK, upper=True)
        lane = lax.broadcasted_iota(jnp.int32, (CHUNK, CHUNK), 1)
        dz_buf[...] = jnp.zeros_like(dz_buf)
        n = s // CHUNK

        def chunk(t, carry):
            suffix, acc = carry
            i = n - 1 - t
            rows = pl.ds(pl.multiple_of(i * CHUNK, CHUNK), CHUNK)
            dq = jnp.zeros((CHUNK, BR), F32)
            dk = jnp.zeros((CHUNK, BR), F32)
            dv = jnp.zeros((CHUNK, BR), F32)
            dcum = jnp.zeros((CHUNK, CHUNK), F32)
            for h in range(N_HEADS):
                back = _from_head(h)
                dqh = dq_ref[h, :, rows].T
                dkh = dk_ref[h, rows, :]
                dq = dq + _dot((dqh * scale).astype(BF16), back)
                dk = dk + _dot(dkh.astype(BF16), back)
                dv = dv + _dot(dv_ref[h, rows, :].astype(BF16), back)
                dcum = dcum + jnp.where(lane == h, dqh[:, LANE_CQ:LANE_CQ + 1] - dkh[:, LANE_CK:LANE_CK + 1], 0.0)
            dqkv_buf[rows, 0:BR] = dq.astype(BF16)
            dqkv_buf[rows, BR:2 * BR] = dk.astype(BF16)
            dqkv_buf[rows, 2 * BR:] = dv.astype(BF16)
            dlf = _tri_sum(tri, dcum) + suffix
            dz = dlf * _sigmoid(-(z_buf[rows, :] + fb_ref[...]))
            dz_buf[rows, pl.ds(0, CHUNK)] = dz.astype(BF16)
            return dlf[0:1, :], acc + _part8(dz)

        _, acc = lax.fori_loop(0, n, chunk, (jnp.zeros((1, CHUNK), F32), jnp.zeros((8, CHUNK), F32)))
        dfb_ref[...] = jnp.sum(acc, axis=0, keepdims=True)
        _copy_all([(dqkv_buf, _window(dproj_ref, QKV0, 3 * BR)), (dz_buf, _window(dproj_ref, BF0, pad))], sems)

    return pl.pallas_call(
        body, name="attn_prep_bwd", in_specs=[ANY, ANY, VMEM, VMEM, VMEM, VMEM], out_specs=[ANY, VMEM],
        out_shape=[jax.ShapeDtypeStruct(dproj.shape, dproj.dtype), jax.ShapeDtypeStruct((1, CHUNK), F32)],
        scratch_shapes=[pltpu.VMEM((s, CHUNK), F32), pltpu.VMEM((s, 3 * BR), BF16), pltpu.VMEM((s, pad), BF16),
                        pltpu.SemaphoreType.DMA((2,))],
        input_output_aliases={1: 0}, compiler_params=_params())(proj, dproj, dqe, dke, dve, f_bias)


ATT_TQ = 256
ATT_FWD_GROUP = 4
ATT_BWD_GROUP = 2
NEG_BIG = -1e30


def _causal_t(q0, k0, tk, tq):
    kpos = k0 + lax.broadcasted_iota(jnp.int32, (tk, tq), 0)
    qpos = q0 + lax.broadcasted_iota(jnp.int32, (tk, tq), 1)
    return kpos <= qpos


STAT_ROWS = 8


def _attn_fwd(qe, ke, vet, rider=None):
    nh, s, w = qe.shape
    tq = min(ATT_TQ, s)
    tk = min(ATT_FWD_GROUP * tq, s)
    per = tk // tq
    nq = s // tq
    r_in, r_in_specs, r_out_specs, r_shapes, r_scratch, r_alias = _rider_parts(rider, 3, 2)
    nr = len(r_in)

    def body(*refs):
        q_ref, k_ref, vt_ref = refs[:3]
        ot_ref, lse_ref = refs[3 + nr:5 + nr]
        h, i = pl.program_id(0), pl.program_id(1)
        start, finish = _ride(rider, (h == 0) & (i == 0), (h == nh - 1) & (i == nq - 1),
                              refs[3:3 + nr], refs[5 + nr:5 + 2 * nr], refs[5 + 2 * nr:])
        start()
        q = q_ref[0]

        def step(j, carry, masked):
            m, l, acc = carry
            ks = pl.ds(pl.multiple_of(j * tk, tk), tk)
            st = _dot_nt(k_ref[0, ks, :], q)
            if masked:
                st = jnp.where(_causal_t(i * tq, j * tk, tk, tq), st, NEG_BIG)
            m_new = jnp.maximum(m, jnp.max(st, axis=0, keepdims=True))
            alpha = jnp.exp(m - m_new)
            pt = jnp.exp(st - m_new)
            l = alpha * l + jnp.sum(pt, axis=0, keepdims=True)
            acc = alpha * acc + _dot(vt_ref[0, :, ks], pt.astype(BF16))
            return m_new, l, acc

        init = (jnp.full((1, tq), NEG_BIG, F32), jnp.zeros((1, tq), F32), jnp.zeros((w, tq), F32))
        carry = lax.fori_loop(0, i // per, functools.partial(step, masked=False), init)
        m, l, acc = step(i // per, carry, True)
        ot_ref[0] = acc / l
        lse_ref[0] = jnp.broadcast_to(m + jnp.log(l), (STAT_ROWS, tq))
        finish()

    outs = pl.pallas_call(
        body, name="attn_fwd", grid=(nh, nq),
        in_specs=[pl.BlockSpec((1, tq, w), lambda h, i: (h, i, 0)), pl.BlockSpec((1, s, w), lambda h, i: (h, 0, 0)),
                  pl.BlockSpec((1, w, s), lambda h, i: (h, 0, 0))] + r_in_specs,
        out_specs=[pl.BlockSpec((1, w, tq), lambda h, i: (h, 0, i)), pl.BlockSpec((1, STAT_ROWS, tq), lambda h, i: (h, 0, i))] + r_out_specs,
        out_shape=[jax.ShapeDtypeStruct((nh, w, s), F32), jax.ShapeDtypeStruct((nh, STAT_ROWS, s), F32)] + r_shapes,
        scratch_shapes=r_scratch, input_output_aliases=r_alias,
        compiler_params=_params("arbitrary", "arbitrary"))(qe, ke, vet, *r_in)
    return outs[0], outs[1], outs[2:]


def _attn_bwd(qe, ke, ket, ve, doe, lse, dd, rider=None):
    nh, s, w = qe.shape
    tq = min(ATT_TQ, s)
    tk = min(ATT_BWD_GROUP * tq, s)
    per = tk // tq
    nq = s // tq
    r_in, r_in_specs, r_out_specs, r_shapes, r_scratch, r_alias = _rider_parts(rider, 7, 3)
    nr = len(r_in)

    def body(*refs):
        q_ref, do_ref, lse_ref, dd_ref, k_ref, kt_ref, v_ref = refs[:7]
        dqt_ref, dk_ref, dv_ref = refs[7 + nr:10 + nr]
        h, i = pl.program_id(0), pl.program_id(1)
        start, finish = _ride(rider, (h == 0) & (i == 0), (h == nh - 1) & (i == nq - 1),
                              refs[7:7 + nr], refs[10 + nr:10 + 2 * nr], refs[10 + 2 * nr:])
        start()

        @pl.when(i == 0)
        def _():
            dk_ref[...] = jnp.zeros_like(dk_ref)
            dv_ref[...] = jnp.zeros_like(dv_ref)

        q = q_ref[0]
        do = do_ref[0]
        lse_row = lse_ref[0, 0:1, :]
        dd_row = dd_ref[0, 0:1, :]

        def step(j, dqt, masked):
            ks = pl.ds(pl.multiple_of(j * tk, tk), tk)
            st = _dot_nt(k_ref[0, ks, :], q)
            pt = jnp.exp(st - lse_row)
            if masked:
                pt = jnp.where(_causal_t(i * tq, j * tk, tk, tq), pt, 0.0)
            dpt = _dot_nt(v_ref[0, ks, :], do)
            dst = (pt * (dpt - dd_row)).astype(BF16)
            dv_ref[0, ks, :] += _dot(pt.astype(BF16), do)
            dk_ref[0, ks, :] += _dot(dst, q)
            return dqt + _dot(kt_ref[0, :, ks], dst)

        dqt = lax.fori_loop(0, i // per, functools.partial(step, masked=False), jnp.zeros((w, tq), F32))
        dqt_ref[0] = step(i // per, dqt, True)
        finish()

    qblk = pl.BlockSpec((1, tq, w), lambda h, i: (h, i, 0))
    stat = pl.BlockSpec((1, STAT_ROWS, tq), lambda h, i: (h, 0, i))
    whole = pl.BlockSpec((1, s, w), lambda h, i: (h, 0, 0))
    whole_t = pl.BlockSpec((1, w, s), lambda h, i: (h, 0, 0))
    outs = pl.pallas_call(
        body, name="attn_bwd", grid=(nh, nq),
        in_specs=[qblk, qblk, stat, stat, whole, whole_t, whole] + r_in_specs,
        out_specs=[pl.BlockSpec((1, w, tq), lambda h, i: (h, 0, i)), whole, whole] + r_out_specs,
        out_shape=[jax.ShapeDtypeStruct((nh, w, s), F32), jax.ShapeDtypeStruct((nh, s, w), F32),
                   jax.ShapeDtypeStruct((nh, s, w), F32)] + r_shapes,
        scratch_shapes=r_scratch, input_output_aliases=r_alias,
        compiler_params=_params("arbitrary", "arbitrary"))(qe, doe, lse, dd, ke, ket, ve, *r_in)
    return outs[0], outs[1], outs[2], outs[3:]


def _bgate_fwd(proj, ot, rider=None):
    s = proj.shape[0]
    r_in, r_in_specs, r_out_specs, r_shapes, r_scratch, r_alias = _rider_parts(rider, 2, 2)
    nr = len(r_in)

    def body(*refs):
        proj_ref, ot_ref = refs[:2]
        y_ref, o_ref = refs[2 + nr:4 + nr]
        g_buf, sems = refs[4 + 2 * nr:6 + 2 * nr]
        start, finish = _ride(rider, True, True, refs[2:2 + nr], refs[4 + nr:4 + 2 * nr], refs[6 + 2 * nr:])
        start()
        _copy_all([(_window(proj_ref, BG0, BR), g_buf)], sems)

        def chunk(i, c):
            rows = pl.ds(pl.multiple_of(i * CHUNK, CHUNK), CHUNK)
            o = jnp.zeros((CHUNK, BR), F32)
            for h in range(N_HEADS):
                back = _from_head(h)
                for part in _split3(ot_ref[h, :, rows].T):
                    o = o + _dot(part, back)
            sg, _ = _silu_and_grad(g_buf[rows, :])
            o_ref[rows, :] = o
            y_ref[rows, :] = (o * sg).astype(BF16)
            return c

        lax.fori_loop(0, s // CHUNK, chunk, 0)
        finish()

    outs = pl.pallas_call(
        body, name="bgate_fwd", in_specs=[ANY, VMEM] + r_in_specs, out_specs=[VMEM, VMEM] + r_out_specs,
        out_shape=[jax.ShapeDtypeStruct((s, BR), BF16), jax.ShapeDtypeStruct((s, BR), F32)] + r_shapes,
        scratch_shapes=[pltpu.VMEM((s, BR), F32), pltpu.SemaphoreType.DMA((1,))] + r_scratch,
        input_output_aliases=r_alias, compiler_params=_params())(proj, ot, *r_in)
    return outs[0], outs[1], outs[2:]


def _bgate_bwd(proj, dproj, o, dy):
    s = proj.shape[0]

    def body(proj_ref, dproj_in, o_ref, dy_ref, dproj_ref, do_ref, dd_ref, g_buf, dg_buf, sems):
        del dproj_in
        _copy_all([(_window(proj_ref, BG0, BR), g_buf)], sems)
        lane = lax.broadcasted_iota(jnp.int32, (BR, CHUNK), 0) // HEAD_DIM
        col = lax.broadcasted_iota(jnp.int32, (BR, CHUNK), 1)
        sel = (lane == col).astype(BF16)

        def chunk(i, c):
            rows = pl.ds(pl.multiple_of(i * CHUNK, CHUNK), CHUNK)
            sg, dsg = _silu_and_grad(g_buf[rows, :])
            dyv = dy_ref[rows, :]
            ov = o_ref[rows, :]
            do = dyv * sg
            dg_buf[rows, :] = (dyv * ov * dsg).astype(BF16)
            prod = _split3(do * ov)
            ddt = (_dot(prod[0], sel) + _dot(prod[1], sel) + _dot(prod[2], sel)).T
            dob = do.astype(BF16)
            for h in range(N_HEADS):
                do_ref[h, rows, :] = _dot(dob, _to_head(h)).astype(BF16)
                dd_ref[h, :, rows] = jnp.broadcast_to(ddt[h:h + 1, :], (STAT_ROWS, CHUNK))
            return c

        lax.fori_loop(0, s // CHUNK, chunk, 0)
        _copy_all([(dg_buf, _window(dproj_ref, BG0, BR))], sems)

    return pl.pallas_call(
        body, name="bgate_bwd", in_specs=[ANY, ANY, VMEM, VMEM], out_specs=[ANY, VMEM, VMEM],
        out_shape=[jax.ShapeDtypeStruct(dproj.shape, dproj.dtype), jax.ShapeDtypeStruct((N_HEADS, s, EXT), BF16),
                   jax.ShapeDtypeStruct((N_HEADS, STAT_ROWS, s), F32)],
        scratch_shapes=[pltpu.VMEM((s, BR), F32), pltpu.VMEM((s, BR), BF16), pltpu.SemaphoreType.DMA((1,))],
        input_output_aliases={1: 0}, compiler_params=_params())(proj, dproj, o, dy)


C_PAD = 8


def _mix_c_fwd(proj, w):
    s = proj.shape[0]

    def body(proj_ref, w_ref, y_ref, cin_buf, g_buf, z_buf, sems):
        _copy_all([(_window(proj_ref, CIN0, 3 * BR), cin_buf), (_window(proj_ref, CG0, BR), g_buf)], sems)
        z_buf[pl.ds(0, C_PAD), :] = jnp.zeros((C_PAD, BR), F32)

        def fill(i, c):
            rows = pl.ds(pl.multiple_of(i * CHUNK, CHUNK), CHUNK)
            z_buf[pl.ds(pl.multiple_of(i * CHUNK + C_PAD, 8), CHUNK), :] = cin_buf[rows, BR:2 * BR] * cin_buf[rows, 2 * BR:]
            return c

        lax.fori_loop(0, s // CHUNK, fill, 0)

        def chunk(i, c):
            r0 = pl.multiple_of(i * CHUNK, CHUNK)
            rows = pl.ds(r0, CHUNK)
            ze = z_buf[pl.ds(r0, CHUNK + C_PAD), :]
            conv = jnp.zeros((CHUNK, BR), F32)
            for k in range(K_SHORT):
                off = C_PAD - (K_SHORT - 1) + k
                conv = conv + w_ref[k:k + 1, :] * ze[off:off + CHUNK]
            sg, _ = _silu_and_grad(g_buf[rows, :])
            y_ref[rows, :] = (cin_buf[rows, 0:BR] * conv * sg).astype(BF16)
            return c

        lax.fori_loop(0, s // CHUNK, chunk, 0)

    return pl.pallas_call(
        body, name="mix_c_fwd", in_specs=[ANY, VMEM], out_specs=VMEM, out_shape=jax.ShapeDtypeStruct((s, BR), BF16),
        scratch_shapes=[pltpu.VMEM((s, 3 * BR), F32), pltpu.VMEM((s, BR), F32), pltpu.VMEM((s + C_PAD, BR), F32),
                        pltpu.SemaphoreType.DMA((2,))],
        compiler_params=_params())(proj, w)


def _mix_c_bwd(proj, dproj, dy, w):
    s = proj.shape[0]

    def body(proj_ref, dproj_in, dy_ref, w_ref, dproj_ref, dw_ref, cin_buf, g_buf, z_buf, dc_buf, dcin_buf, dg_buf, acc, sems):
        del dproj_in
        _copy_all([(_window(proj_ref, CIN0, 3 * BR), cin_buf), (_window(proj_ref, CG0, BR), g_buf)], sems)
        z_buf[pl.ds(0, C_PAD), :] = jnp.zeros((C_PAD, BR), F32)
        dc_buf[pl.ds(s, C_PAD), :] = jnp.zeros((C_PAD, BR), F32)
        acc[...] = jnp.zeros_like(acc)

        def fill(i, c):
            rows = pl.ds(pl.multiple_of(i * CHUNK, CHUNK), CHUNK)
            z_buf[pl.ds(pl.multiple_of(i * CHUNK + C_PAD, 8), CHUNK), :] = cin_buf[rows, BR:2 * BR] * cin_buf[rows, 2 * BR:]
            return c

        lax.fori_loop(0, s // CHUNK, fill, 0)

        def chunk(i, c):
            r0 = pl.multiple_of(i * CHUNK, CHUNK)
            rows = pl.ds(r0, CHUNK)
            ze = z_buf[pl.ds(r0, CHUNK + C_PAD), :]
            taps = [ze[C_PAD - (K_SHORT - 1) + k:C_PAD - (K_SHORT - 1) + k + CHUNK] for k in range(K_SHORT)]
            conv = jnp.zeros((CHUNK, BR), F32)
            for k in range(K_SHORT):
                conv = conv + w_ref[k:k + 1, :] * taps[k]
            sg, dsg = _silu_and_grad(g_buf[rows, :])
            bg = cin_buf[rows, 0:BR]
            dyv = dy_ref[rows, :]
            dconv = dyv * bg * sg
            dc_buf[rows, :] = dconv
            dcin_buf[rows, 0:BR] = (dyv * conv * sg).astype(BF16)
            dg_buf[rows, :] = (dyv * bg * conv * dsg).astype(BF16)
            for k in range(K_SHORT):
                acc[k] += _part8(dconv * taps[k])
            return c

        lax.fori_loop(0, s // CHUNK, chunk, 0)

        def chunk2(i, c):
            r0 = pl.multiple_of(i * CHUNK, CHUNK)
            rows = pl.ds(r0, CHUNK)
            de = dc_buf[pl.ds(r0, CHUNK + C_PAD), :]
            dz = jnp.zeros((CHUNK, BR), F32)
            for k in range(K_SHORT):
                off = K_SHORT - 1 - k
                dz = dz + w_ref[k:k + 1, :] * de[off:off + CHUNK]
            dcin_buf[rows, BR:2 * BR] = (dz * cin_buf[rows, 2 * BR:]).astype(BF16)
            dcin_buf[rows, 2 * BR:] = (dz * cin_buf[rows, BR:2 * BR]).astype(BF16)
            return c

        lax.fori_loop(0, s // CHUNK, chunk2, 0)
        dw_ref[...] = jnp.zeros_like(dw_ref)
        for k in range(K_SHORT):
            dw_ref[k:k + 1, :] = jnp.sum(acc[k], axis=0, keepdims=True)
        _copy_all([(dcin_buf, _window(dproj_ref, CIN0, 3 * BR)), (dg_buf, _window(dproj_ref, CG0, BR))], sems)

    return pl.pallas_call(
        body, name="mix_c_bwd", in_specs=[ANY, ANY, VMEM, VMEM], out_specs=[ANY, VMEM],
        out_shape=[jax.ShapeDtypeStruct(dproj.shape, dproj.dtype), jax.ShapeDtypeStruct((8, BR), F32)],
        scratch_shapes=[pltpu.VMEM((s, 3 * BR), F32), pltpu.VMEM((s, BR), F32), pltpu.VMEM((s + C_PAD, BR), F32),
                        pltpu.VMEM((s + C_PAD, BR), F32), pltpu.VMEM((s, 3 * BR), BF16), pltpu.VMEM((s, BR), BF16),
                        pltpu.VMEM((K_SHORT, 8, BR), F32), pltpu.SemaphoreType.DMA((2,))],
        input_output_aliases={1: 0}, compiler_params=_params())(proj, dproj, dy, w)


D_PAD = 32


def _tap_windows(win, offsets, n):
    rows = win.shape[0]
    rolled, out = {}, []
    for off in offsets:
        r = off % 8
        if r not in rolled:
            rolled[r] = win if r == 0 else pltpu.roll(win, rows - r, 0)
        out.append(rolled[r][off - r:off - r + n])
    return out


def _mix_d_common(cin_ref, g_ref, w_ref, b_ref, lg_ref, lb_ref, hh_buf, r0):
    he = hh_buf[pl.ds(r0, CHUNK + D_PAD), :]
    taps = _tap_windows(he, [D_PAD - (K_CONF - 1) + k for k in range(K_CONF)], CHUNK)
    conv = jnp.zeros((CHUNK, BR), F32) + b_ref[...]
    for k in range(K_CONF):
        conv = conv + w_ref[k:k + 1, :] * taps[k]
    xh, rs = _ln_fwd(conv)
    sw, dsw = _silu_and_grad(xh * lg_ref[...] + lb_ref[...])
    sg, dsg = _silu_and_grad(g_ref[pl.ds(r0, CHUNK), :])
    return taps, xh, rs, sw, dsw, sg, dsg


def _mix_d_fill(cin_buf, hh_buf, s):
    hh_buf[pl.ds(0, D_PAD), :] = jnp.zeros((D_PAD, BR), F32)

    def fill(i, c):
        rows = pl.ds(pl.multiple_of(i * CHUNK, CHUNK), CHUNK)
        hh_buf[pl.ds(pl.multiple_of(i * CHUNK + D_PAD, 8), CHUNK), :] = cin_buf[rows, 0:BR] * _sigmoid(cin_buf[rows, BR:])
        return c

    lax.fori_loop(0, s // CHUNK, fill, 0)


def _mix_d_fwd(proj, w, b, lg, lb):
    s = proj.shape[0]

    def body(proj_ref, w_ref, b_ref, lg_ref, lb_ref, y_ref, cin_buf, g_buf, hh_buf, sems):
        _copy_all([(_window(proj_ref, DGLU0, 2 * BR), cin_buf), (_window(proj_ref, DG0, BR), g_buf)], sems)
        _mix_d_fill(cin_buf, hh_buf, s)

        def chunk(i, c):
            r0 = pl.multiple_of(i * CHUNK, CHUNK)
            _, _, _, sw, _, sg, _ = _mix_d_common(cin_buf, g_buf, w_ref, b_ref, lg_ref, lb_ref, hh_buf, r0)
            y_ref[pl.ds(r0, CHUNK), :] = (sw * sg).astype(BF16)
            return c

        lax.fori_loop(0, s // CHUNK, chunk, 0)

    return pl.pallas_call(
        body, name="mix_d_fwd", in_specs=[ANY, VMEM, VMEM, VMEM, VMEM], out_specs=VMEM,
        out_shape=jax.ShapeDtypeStruct((s, BR), BF16),
        scratch_shapes=[pltpu.VMEM((s, 2 * BR), F32), pltpu.VMEM((s, BR), F32), pltpu.VMEM((s + D_PAD, BR), F32),
                        pltpu.SemaphoreType.DMA((2,))],
        compiler_params=_params())(proj, w, b, lg, lb)


def _mix_d_bwd(proj, dproj, dy, w, b, lg, lb):
    s = proj.shape[0]

    def body(proj_ref, dproj_in, dy_ref, w_ref, b_ref, lg_ref, lb_ref, dproj_ref, dw_ref, db_ref, dlg_ref, dlb_ref,
             cin_buf, g_buf, hh_buf, dc_buf, dcin_buf, dg_buf, acc_w, acc_s, sems):
        del dproj_in
        _copy_all([(_window(proj_ref, DGLU0, 2 * BR), cin_buf), (_window(proj_ref, DG0, BR), g_buf)], sems)
        _mix_d_fill(cin_buf, hh_buf, s)
        dc_buf[pl.ds(s, D_PAD), :] = jnp.zeros((D_PAD, BR), F32)
        acc_w[...] = jnp.zeros_like(acc_w)
        acc_s[...] = jnp.zeros_like(acc_s)

        def chunk(i, c):
            r0 = pl.multiple_of(i * CHUNK, CHUNK)
            rows = pl.ds(r0, CHUNK)
            taps, xh, rs, sw, dsw, sg, dsg = _mix_d_common(cin_buf, g_buf, w_ref, b_ref, lg_ref, lb_ref, hh_buf, r0)
            dyv = dy_ref[rows, :]
            dg_buf[rows, :] = (dyv * sw * dsg).astype(BF16)
            d_ln = dyv * sg * dsw
            acc_s[0] += _part8(d_ln * xh)
            acc_s[1] += _part8(d_ln)
            dc = _ln_bwd(d_ln * lg_ref[...], xh, rs)
            acc_s[2] += _part8(dc)
            dc_buf[rows, :] = dc
            for k in range(K_CONF):
                acc_w[k] += _part8(dc * taps[k])
            return c

        lax.fori_loop(0, s // CHUNK, chunk, 0)

        def chunk2(i, c):
            r0 = pl.multiple_of(i * CHUNK, CHUNK)
            rows = pl.ds(r0, CHUNK)
            de = dc_buf[pl.ds(r0, CHUNK + D_PAD), :]
            dh = jnp.zeros((CHUNK, BR), F32)
            for k, win in enumerate(_tap_windows(de, [K_CONF - 1 - k for k in range(K_CONF)], CHUNK)):
                dh = dh + w_ref[k:k + 1, :] * win
            a = cin_buf[rows, 0:BR]
            sig = _sigmoid(cin_buf[rows, BR:])
            dcin_buf[rows, 0:BR] = (dh * sig).astype(BF16)
            dcin_buf[rows, BR:] = (dh * a * sig * (1.0 - sig)).astype(BF16)
            return c

        lax.fori_loop(0, s // CHUNK, chunk2, 0)
        dw_ref[...] = jnp.zeros_like(dw_ref)
        for k in range(K_CONF):
            dw_ref[k:k + 1, :] = jnp.sum(acc_w[k], axis=0, keepdims=True)
        dlg_ref[...] = jnp.sum(acc_s[0], axis=0, keepdims=True)
        dlb_ref[...] = jnp.sum(acc_s[1], axis=0, keepdims=True)
        db_ref[...] = jnp.sum(acc_s[2], axis=0, keepdims=True)
        _copy_all([(dcin_buf, _window(dproj_ref, DGLU0, 2 * BR)), (dg_buf, _window(dproj_ref, DG0, BR))], sems)

    vec = jax.ShapeDtypeStruct((1, BR), F32)
    return pl.pallas_call(
        body, name="mix_d_bwd", in_specs=[ANY, ANY, VMEM, VMEM, VMEM, VMEM, VMEM], out_specs=[ANY, VMEM, VMEM, VMEM, VMEM],
        out_shape=[jax.ShapeDtypeStruct(dproj.shape, dproj.dtype), jax.ShapeDtypeStruct((32, BR), F32), vec, vec, vec],
        scratch_shapes=[pltpu.VMEM((s, 2 * BR), F32), pltpu.VMEM((s, BR), F32), pltpu.VMEM((s + D_PAD, BR), F32),
                        pltpu.VMEM((s + D_PAD, BR), F32), pltpu.VMEM((s, 2 * BR), BF16), pltpu.VMEM((s, BR), BF16),
                        pltpu.VMEM((K_CONF, 8, BR), F32), pltpu.VMEM((3, 8, BR), F32), pltpu.SemaphoreType.DMA((2,))],
        input_output_aliases={1: 0}, compiler_params=_params())(proj, dproj, dy, w, b, lg, lb)


MERGE_TM = 256


def _merge_fwd(x, proj, ys, wb, wo):
    s = x.shape[0]
    tm = min(MERGE_TM, s)

    def body(x_ref, lg_ref, ya, yb, yc, yd, wb_ref, wo_ref, xn_ref, mg_ref):
        merged = jnp.zeros((tm, D_MODEL), F32)
        for n, y_ref in enumerate((ya, yb, yc, yd)):
            gate = _sigmoid(lg_ref[:, n * D_MODEL:(n + 1) * D_MODEL])
            merged = merged + gate * _dot(y_ref[...], wb_ref[n])
        mb = merged.astype(BF16)
        mg_ref[...] = mb
        xn_ref[...] = x_ref[...] + _dot(mb, wo_ref[...])

    row = lambda w: pl.BlockSpec((tm, w), lambda i: (i, 0))
    return pl.pallas_call(
        body, name="merge_fwd", grid=(s // tm,),
        in_specs=[row(D_MODEL), pl.BlockSpec((tm, MRG0), lambda i: (i, 1)), row(BR), row(BR), row(BR), row(BR),
                  pl.BlockSpec((N_HEADS, BR, D_MODEL), lambda i: (0, 0, 0)), pl.BlockSpec((D_MODEL, D_MODEL), lambda i: (0, 0))],
        out_specs=[row(D_MODEL), row(D_MODEL)],
        out_shape=[jax.ShapeDtypeStruct((s, D_MODEL), F32), jax.ShapeDtypeStruct((s, D_MODEL), BF16)],
        compiler_params=_params("arbitrary"))(x, proj, *ys, wb, wo)


def _merge_bwd(dxn, proj, ys, merged, wb, wo, rider=None):
    s = dxn.shape[0]
    tm = min(MERGE_TM, s)
    steps = s // tm
    r_in, r_in_specs, r_out_specs, r_shapes, r_scratch, r_alias = _rider_parts(rider, 9, 7)
    nr = len(r_in)

    def body(*refs):
        dx_ref, lg_ref, ya, yb, yc, yd, mg_ref, wb_ref, wo_ref = refs[:9]
        dlg_ref, da, db, dc, dd, dwo_ref, dwb_ref = refs[9 + nr:16 + nr]
        start, finish = _ride(rider, pl.program_id(0) == 0, pl.program_id(0) == steps - 1,
                              refs[9:9 + nr], refs[16 + nr:16 + 2 * nr], refs[16 + 2 * nr:])
        start()

        @pl.when(pl.program_id(0) == 0)
        def _():
            dwo_ref[...] = jnp.zeros_like(dwo_ref)
            dwb_ref[...] = jnp.zeros_like(dwb_ref)

        dxb = dx_ref[...].astype(BF16)
        d_merged = _dot_nt(dxb, wo_ref[...])
        dwo_ref[...] += _dot_tn(mg_ref[...], dxb)
        for n, (y_ref, dy_ref) in enumerate(((ya, da), (yb, db), (yc, dc), (yd, dd))):
            yv = y_ref[...]
            gate = _sigmoid(lg_ref[:, n * D_MODEL:(n + 1) * D_MODEL])
            pn = _dot(yv, wb_ref[n])
            dlg_ref[:, n * D_MODEL:(n + 1) * D_MODEL] = (d_merged * pn * gate * (1.0 - gate)).astype(BF16)
            dpn = (d_merged * gate).astype(BF16)
            dy_ref[...] = _dot_nt(dpn, wb_ref[n])
            dwb_ref[n] += _dot_tn(yv, dpn)
        finish()

    row = lambda w: pl.BlockSpec((tm, w), lambda i: (i, 0))
    wb_spec = pl.BlockSpec((N_HEADS, BR, D_MODEL), lambda i: (0, 0, 0))
    wo_spec = pl.BlockSpec((D_MODEL, D_MODEL), lambda i: (0, 0))
    dy_shape = jax.ShapeDtypeStruct((s, BR), F32)
    outs = pl.pallas_call(
        body, name="merge_bwd", grid=(steps,),
        in_specs=[row(D_MODEL), pl.BlockSpec((tm, MRG0), lambda i: (i, 1)), row(BR), row(BR), row(BR), row(BR), row(D_MODEL),
                  wb_spec, wo_spec] + r_in_specs,
        out_specs=[pl.BlockSpec((tm, MRG0), lambda i: (i, 1)), row(BR), row(BR), row(BR), row(BR), wo_spec, wb_spec] + r_out_specs,
        out_shape=[jax.ShapeDtypeStruct((s, PCOLS), BF16), dy_shape, dy_shape, dy_shape, dy_shape,
                   jax.ShapeDtypeStruct((D_MODEL, D_MODEL), F32), jax.ShapeDtypeStruct((N_HEADS, BR, D_MODEL), F32)] + r_shapes,
        scratch_shapes=r_scratch, input_output_aliases=r_alias,
        compiler_params=_params("arbitrary"))(dxn, proj, *ys, merged, wb, wo, *r_in)
    return outs[0], outs[1:5], outs[5], outs[6], outs[7:]


def _loss_head(x, target, g):
    s = x.shape[0]
    tm = min(512, s)

    def body(x_ref, t_ref, g_ref, loss_ref, dx_ref, dg_ref):
        @pl.when(pl.program_id(0) == 0)
        def _():
            loss_ref[...] = jnp.zeros_like(loss_ref)
            dg_ref[...] = jnp.zeros_like(dg_ref)

        xv = x_ref[...]
        gv = g_ref[...]
        r = lax.rsqrt(jnp.mean(xv * xv, axis=-1, keepdims=True) + EPS)
        xn = xv * r
        err = xn * gv - t_ref[...]
        loss_ref[...] += 0.5 * jnp.sum(jnp.mean(err * err, axis=-1, keepdims=True))
        dy = err * (1.0 / D_MODEL)
        dg_ref[...] += _part8(dy * xn)
        gy = dy * gv
        dx_ref[...] = r * (gy - xn * jnp.mean(xn * gy, axis=-1, keepdims=True))

    row = pl.BlockSpec((tm, D_MODEL), lambda i: (i, 0))
    return pl.pallas_call(
        body, name="loss_head", grid=(s // tm,),
        in_specs=[row, row, pl.BlockSpec((1, D_MODEL), lambda i: (0, 0))],
        out_specs=[pl.BlockSpec((8, 128), lambda i: (0, 0)), row, pl.BlockSpec((8, D_MODEL), lambda i: (0, 0))],
        out_shape=[jax.ShapeDtypeStruct((8, 128), F32), jax.ShapeDtypeStruct((s, D_MODEL), F32), jax.ShapeDtypeStruct((8, D_MODEL), F32)],
        compiler_params=_params("arbitrary"))(x, target, g)


_SEGMENTS = ((0, 512, AUV0), (512, 256, AG0), (768, 768, QKV0), (1536, 4, BF0), (1540, 256, BG0), (1796, 768, CIN0),
             (2564, 256, CG0), (2820, 512, DGLU0), (3332, 256, DG0), (3588, 4096, MRG0))


def _to_aligned_cols(w):
    out, pos = [], 0
    for src, width, dst in sorted(_SEGMENTS, key=lambda t: t[2]):
        if dst > pos:
            out.append(jnp.zeros(w.shape[:-1] + (dst - pos,), w.dtype))
        out.append(w[..., src:src + width])
        pos = dst + width
    return jnp.concatenate(out, axis=-1)


def _from_aligned_cols(g):
    return jnp.concatenate([g[..., dst:dst + width] for _, width, dst in _SEGMENTS], axis=-1)


def _sgu_bias_rows(sgu_b):
    return jnp.repeat(sgu_b.T, HEAD_DIM, axis=1)


def _layer_fwd(x, p, next_blocks=None):
    proj, h = _inproj_fwd(x, p["norm_g"], p["w_in"])
    ya = _mix_a_fwd(proj, p["sgu_w"], _sgu_bias_rows(p["sgu_b"]), p["sgu_ln_g"], p["sgu_ln_b"])
    qe, ke, ve, ket, vet = _attn_prep_fwd(proj, p["f_bias"])
    ot, lse, landed = _attn_fwd(qe, ke, vet, rider=None if next_blocks is None else _gather_send_rider(next_blocks))
    yb, o, gathered = _bgate_fwd(proj, ot, rider=None if next_blocks is None else _gather_forward_rider(landed))
    yc = _mix_c_fwd(proj, p["short_conv_w"])
    yd = _mix_d_fwd(proj, p["conf_dw_w"], p["conf_dw_b"], p["conf_ln_g"], p["conf_ln_b"])
    ys = (ya, yb, yc, yd)
    x_next, merged = _merge_fwd(x, proj, ys, p["w_branch"], p["w_out"])
    saved = dict(x=x, proj=proj, h=h, ys=ys, merged=merged, qe=qe, ke=ke, ve=ve, ket=ket, lse=lse, o=o)
    return x_next, saved, gathered


def _layer_bwd(dxn, p, sv, prev_reduce=None):
    proj = sv["proj"]
    if prev_reduce is None:
        dproj, dys, d_wo, d_wb, _ = _merge_bwd(dxn, proj, sv["ys"], sv["merged"], p["w_branch"], p["w_out"])
        chip_rider = None
    else:
        pc, chip, arrays = prev_reduce
        mine, other = _row_halves(arrays, pc)
        dproj, dys, d_wo, d_wb, from_sibling = _merge_bwd(dxn, proj, sv["ys"], sv["merged"], p["w_branch"], p["w_out"],
                                                           rider=_pair_rider(other))
        halves = [_add_pair(a.reshape(-1, a.shape[-1]), b.reshape(-1, b.shape[-1])).reshape(a.shape)
                  for a, b in zip(mine, from_sibling)]
        chip_rider = _chip_rider(halves)
    dproj, d_sgu_w, d_bias_rows, d_sgu_lg, d_sgu_lb = _mix_a_bwd(
        proj, dproj, dys[0], p["sgu_w"], _sgu_bias_rows(p["sgu_b"]), p["sgu_ln_g"], p["sgu_ln_b"])
    dproj, doe, dd = _bgate_bwd(proj, dproj, sv["o"], dys[1])
    dqe, dke, dve, from_chips = _attn_bwd(sv["qe"], sv["ke"], sv["ket"], sv["ve"], doe, sv["lse"], dd, rider=chip_rider)
    reduced = None
    if prev_reduce is not None:
        reduced = [_sum_chips(lax.dynamic_index_in_dim(a, chip, 0, keepdims=False), r) for a, r in zip(halves, from_chips)]
    dproj, d_fb = _attn_prep_bwd(proj, dproj, dqe, dke, dve, p["f_bias"])
    dproj, d_sc = _mix_c_bwd(proj, dproj, dys[2], p["short_conv_w"])
    dproj, d_cw, d_cb, d_clg, d_clb = _mix_d_bwd(proj, dproj, dys[3], p["conf_dw_w"], p["conf_dw_b"], p["conf_ln_g"], p["conf_ln_b"])
    dx, dg8 = _inproj_bwd_x(dproj, p["w_in"], sv["x"], dxn, p["norm_g"])
    d_win = _inproj_bwd_w(sv["h"], dproj)
    grads = dict(
        norm_g=jnp.sum(dg8, axis=0), w_in=d_win, f_bias=d_fb[0, :N_HEADS], sgu_w=d_sgu_w,
        sgu_b=d_bias_rows.reshape(CHUNK, N_HEADS, HEAD_DIM).sum(axis=-1).T,
        sgu_ln_g=d_sgu_lg[0], sgu_ln_b=d_sgu_lb[0], short_conv_w=d_sc[:K_SHORT], conf_dw_w=d_cw[:K_CONF],
        conf_dw_b=d_cb[0], conf_ln_g=d_clg[0], conf_ln_b=d_clb[0], w_branch=d_wb, w_out=d_wo)
    return dx, grads, reduced


def _row_halves(arrays, pc):
    half = lambda a, i: lax.dynamic_slice_in_dim(a, i * (a.shape[-2] // 2), a.shape[-2] // 2, axis=a.ndim - 2)
    return [half(a, pc) for a in arrays], [half(a, 1 - pc) for a in arrays]


def _local_step(x, target, layers, final_g):
    saved = []
    for p in layers:
        x, sv, _ = _layer_fwd(x, p)
        saved.append(sv)
    loss8, dx, dfg8 = _loss_head(x, target, final_g)
    grads = [None] * len(layers)
    for l in reversed(range(len(layers))):
        dx, grads[l], _ = _layer_bwd(dx, layers[l], saved[l])
    return loss8[0, 0], dx, grads, jnp.sum(dfg8, axis=0)


def _gather_chips(blocks):
    n = len(blocks)

    def body(*refs):
        ins, outs, (send_sems, recv_sems) = refs[:n], refs[n:2 * n], refs[2 * n:]
        x, y, cc = _place()
        me, sibling = (x, y, cc), (x, y, 1 - cc)
        chips = _other_chips(x, y)

        def copy(a, k, chip, layer, to, src=None):
            dst = outs[a].at[2 * chip[0] + chip[1], layer]
            return pltpu.make_async_remote_copy(src_ref=dst if src is None else src, dst_ref=dst, send_sem=send_sems.at[a, k],
                                                recv_sem=recv_sems.at[a, k], device_id=to, device_id_type=MESH)

        first = [copy(a, j, (x, y), cc, (*chip, cc), src=ins[a]) for j, chip in enumerate(chips) for a in range(n)]
        for cp in first:
            cp.start()
        passed = []
        for j, chip in enumerate(chips):
            for a in range(n):
                copy(a, j, chip, cc, me).wait_recv()
                passed.append(copy(a, 3 + j, chip, cc, sibling))
                passed[-1].start()
        for j, chip in enumerate(chips):
            for a in range(n):
                copy(a, 3 + j, chip, 1 - cc, me).wait_recv()
        for cp in first + passed:
            cp.wait_send()

    return pl.pallas_call(
        body, name="gather_chips", in_specs=[ANY] * n, out_specs=[ANY] * n,
        out_shape=[jax.ShapeDtypeStruct((N_CHIPS, 2) + b.shape, b.dtype) for b in blocks],
        scratch_shapes=[pltpu.SemaphoreType.DMA((n, 6)), pltpu.SemaphoreType.DMA((n, 6))])(*blocks)


def _pair_exchange(arrays):
    n = len(arrays)

    def body(*refs):
        ins, outs, (send_sems, recv_sems) = refs[:n], refs[n:2 * n], refs[2 * n:]
        x, y, cc = _place()
        cps = [pltpu.make_async_remote_copy(src_ref=ins[a], dst_ref=outs[a], send_sem=send_sems.at[a], recv_sem=recv_sems.at[a],
                                            device_id=(x, y, 1 - cc), device_id_type=MESH) for a in range(n)]
        for cp in cps:
            cp.start()
        for cp in cps:
            cp.wait()

    return pl.pallas_call(
        body, name="pair_exchange", in_specs=[ANY] * n, out_specs=[ANY] * n,
        out_shape=[jax.ShapeDtypeStruct(a.shape, a.dtype) for a in arrays],
        scratch_shapes=[pltpu.SemaphoreType.DMA((n,)), pltpu.SemaphoreType.DMA((n,))])(*arrays)


def _chip_exchange(arrays):
    n = len(arrays)

    def body(*refs):
        ins, outs, (send_sems, recv_sems) = refs[:n], refs[n:2 * n], refs[2 * n:]
        x, y, cc = _place()
        cps = [pltpu.make_async_remote_copy(src_ref=ins[a].at[2 * px + py], dst_ref=outs[a].at[k], send_sem=send_sems.at[a, k],
                                            recv_sem=recv_sems.at[a, k], device_id=(px, py, cc), device_id_type=MESH)
               for k, (px, py) in enumerate(_other_chips(x, y)) for a in range(n)]
        for cp in cps:
            cp.start()
        for cp in cps:
            cp.wait()

    return pl.pallas_call(
        body, name="chip_exchange", in_specs=[ANY] * n, out_specs=[ANY] * n,
        out_shape=[jax.ShapeDtypeStruct((3,) + a.shape[1:], a.dtype) for a in arrays],
        scratch_shapes=[pltpu.SemaphoreType.DMA((n, 3)), pltpu.SemaphoreType.DMA((n, 3))])(*arrays)


def _allreduce8(arrays):
    n = len(arrays)

    def body(*refs):
        ins, outs, recvs = refs[:n], refs[n:2 * n], refs[2 * n:3 * n]
        send_sems, recv_sems = refs[3 * n:]
        x, y, cc = _place()
        for a in range(n):
            outs[a][...] = ins[a][...]
        for k, peer in enumerate([(x, y, 1 - cc), (1 - x, y, cc), (x, 1 - y, cc)]):
            cps = [pltpu.make_async_remote_copy(src_ref=outs[a], dst_ref=recvs[a].at[k], send_sem=send_sems.at[a, k],
                                                recv_sem=recv_sems.at[a, k], device_id=peer, device_id_type=MESH) for a in range(n)]
            for cp in cps:
                cp.start()
            for cp in cps:
                cp.wait()
            for a in range(n):
                outs[a][...] = outs[a][...] + recvs[a][k]

    return pl.pallas_call(
        body, name="allreduce8", in_specs=[VMEM] * n, out_specs=[VMEM] * n,
        out_shape=[jax.ShapeDtypeStruct(a.shape, F32) for a in arrays],
        scratch_shapes=[pltpu.VMEM((3,) + a.shape, F32) for a in arrays] + [pltpu.SemaphoreType.DMA((n, 3)), pltpu.SemaphoreType.DMA((n, 3))],
        compiler_params=_params())(*arrays)


def _row_tile(rows, cols, limit_bytes=1 << 20):
    t = rows
    while t % 16 == 0 and t * cols * 4 > limit_bytes:
        t //= 2
    return t


def _add_pair(a, b):
    r, c = a.shape
    t = _row_tile(r, c)

    def body(a_ref, b_ref, o_ref):
        o_ref[...] = (a_ref[...].astype(F32) + b_ref[...].astype(F32)).astype(o_ref.dtype)

    spec = pl.BlockSpec((t, c), lambda i: (i, 0))
    return pl.pallas_call(body, name="add_pair", grid=(r // t,), in_specs=[spec, spec], out_specs=spec,
                          out_shape=jax.ShapeDtypeStruct((r, c), a.dtype), compiler_params=_params("arbitrary"))(a, b)


def _sum_chips(own, recv):
    r, c = own.shape
    t = _row_tile(r, c)

    def body(o_ref, r_ref, out_ref):
        acc = o_ref[...].astype(F32)
        for k in range(3):
            acc = acc + r_ref[k].astype(F32)
        out_ref[...] = acc

    return pl.pallas_call(
        body, name="sum_chips", grid=(r // t,),
        in_specs=[pl.BlockSpec((t, c), lambda i: (i, 0)), pl.BlockSpec((3, t, c), lambda i: (0, i, 0))],
        out_specs=pl.BlockSpec((t, c), lambda i: (i, 0)), out_shape=jax.ShapeDtypeStruct((r, c), F32),
        compiler_params=_params("arbitrary"))(own, recv)


def _adamw_update(w_ref, m_ref, v_ref, g_ref, d_ref, mo_ref, vo_ref):
    gv = g_ref[...]
    mn = ADAM_B1 * m_ref[...] + (1.0 - ADAM_B1) * gv
    vn = ADAM_B2 * v_ref[...] + (1.0 - ADAM_B2) * (gv * gv)
    m_hat = mn / (1.0 - ADAM_B1 ** ADAM_STEP)
    v_hat = vn / (1.0 - ADAM_B2 ** ADAM_STEP)
    d_ref[...] = -ADAM_LR * (m_hat / (jnp.sqrt(v_hat) + ADAM_EPS) + ADAM_WD * w_ref[...])
    mo_ref[...] = mn
    vo_ref[...] = vn


def _adamw(w, m, v, g):
    r, c = w.shape
    t = _row_tile(r, c)

    def body(*refs):
        _adamw_update(*refs)

    spec = pl.BlockSpec((t, c), lambda i: (i, 0))
    shape = jax.ShapeDtypeStruct((r, c), F32)
    return pl.pallas_call(body, name="adamw", grid=(r // t,), in_specs=[spec] * 4, out_specs=[spec] * 3,
                          out_shape=[shape] * 3, compiler_params=_params("arbitrary"))(w, m, v, g)


def _adamw_small(ws, ms, vs, gs):
    n = len(ws)

    def body(*refs):
        w_refs, m_refs, v_refs, g_refs = refs[:n], refs[n:2 * n], refs[2 * n:3 * n], refs[3 * n:4 * n]
        d_refs, mo_refs, vo_refs = refs[4 * n:5 * n], refs[5 * n:6 * n], refs[6 * n:]
        for a in range(n):
            _adamw_update(w_refs[a], m_refs[a], v_refs[a], g_refs[a], d_refs[a], mo_refs[a], vo_refs[a])

    shapes = [jax.ShapeDtypeStruct(w.shape, F32) for w in ws]
    outs = pl.pallas_call(body, name="adamw_small", in_specs=[VMEM] * (4 * n), out_specs=[VMEM] * (3 * n),
                          out_shape=shapes * 3, compiler_params=_params())(*ws, *ms, *vs, *gs)
    return outs[:n], outs[n:2 * n], outs[2 * n:]


SMALL =("norm_g", "f_bias", "sgu_w", "sgu_b", "sgu_ln_g", "sgu_ln_b", "short_conv_w", "conf_dw_w", "conf_dw_b",
         "conf_ln_g", "conf_ln_b", "final_g")
WEIGHTS = ("norm_g", "w_in", "f_bias", "sgu_w", "sgu_b", "sgu_ln_g", "sgu_ln_b", "short_conv_w", "conf_dw_w",
           "conf_dw_b", "conf_ln_g", "conf_ln_b", "w_branch", "w_out", "final_g")


def _aligned_pieces():
    out, pos = [], 0
    for src, width, dst in sorted(_SEGMENTS, key=lambda t: t[2]):
        if dst > pos:
            out.append((None, 0, dst - pos))
        lo = src
        while lo < src + width:
            j = lo // SHARD_COLS
            hi = min(src + width, (j + 1) * SHARD_COLS)
            out.append((j, lo - j * SHARD_COLS, hi - lo))
            lo = hi
        pos = dst + width
    return out


def _shard_pieces(j):
    lo_s, hi_s = j * SHARD_COLS, (j + 1) * SHARD_COLS
    out = []
    for src, width, dst in _SEGMENTS:
        lo, hi = max(src, lo_s), min(src + width, hi_s)
        if lo < hi:
            out.append((dst + lo - src, hi - lo))
    return out


def _w_in_aligned(shards):
    rows, dtype = shards[0].shape[0], shards[0].dtype
    return jnp.concatenate([jnp.zeros((rows, w), dtype) if j is None else shards[j][:, c0:c0 + w]
                            for j, c0, w in _aligned_pieces()], axis=1)


def _w_in_shard(g, j):
    parts = [g[:, c0:c0 + w] for c0, w in _shard_pieces(j)]
    return jnp.concatenate(parts + [jnp.zeros((g.shape[0], SHARD_PAD - SHARD_COLS), g.dtype)], axis=1)


def _by_layer(pc, own, other):
    return jnp.where(pc == 0, jnp.stack([own, other]), jnp.stack([other, own]))


def kernel(x, norm_g, w_in, f_bias, sgu_w, sgu_b, sgu_ln_g, sgu_ln_b, short_conv_w, conf_dw_w, conf_dw_b, conf_ln_g, conf_ln_b, w_branch, w_out, final_g, loss_target, m_norm_g, m_w_in, m_f_bias, m_sgu_w, m_sgu_b, m_sgu_ln_g, m_sgu_ln_b, m_short_conv_w, m_conf_dw_w, m_conf_dw_b, m_conf_ln_g, m_conf_ln_b, m_w_branch, m_w_out, m_final_g, v_norm_g, v_w_in, v_f_bias, v_sgu_w, v_sgu_b, v_sgu_ln_g, v_sgu_ln_b, v_short_conv_w, v_conf_dw_w, v_conf_dw_b, v_conf_ln_g, v_conf_ln_b, v_w_branch, v_w_out, v_final_g):
    px, py, pc = _place()
    chip = 2 * px + py
    depth = w_in.shape[0]
    w = dict(norm_g=norm_g, w_in=w_in, f_bias=f_bias, sgu_w=sgu_w, sgu_b=sgu_b, sgu_ln_g=sgu_ln_g, sgu_ln_b=sgu_ln_b,
             short_conv_w=short_conv_w, conf_dw_w=conf_dw_w, conf_dw_b=conf_dw_b, conf_ln_g=conf_ln_g, conf_ln_b=conf_ln_b,
             w_branch=w_branch, w_out=w_out, final_g=final_g)
    m = dict(norm_g=m_norm_g, w_in=m_w_in, f_bias=m_f_bias, sgu_w=m_sgu_w, sgu_b=m_sgu_b, sgu_ln_g=m_sgu_ln_g,
             sgu_ln_b=m_sgu_ln_b, short_conv_w=m_short_conv_w, conf_dw_w=m_conf_dw_w, conf_dw_b=m_conf_dw_b,
             conf_ln_g=m_conf_ln_g, conf_ln_b=m_conf_ln_b, w_branch=m_w_branch, w_out=m_w_out, final_g=m_final_g)
    v = dict(norm_g=v_norm_g, w_in=v_w_in, f_bias=v_f_bias, sgu_w=v_sgu_w, sgu_b=v_sgu_b, sgu_ln_g=v_sgu_ln_g,
             sgu_ln_b=v_sgu_ln_b, short_conv_w=v_short_conv_w, conf_dw_w=v_conf_dw_w, conf_dw_b=v_conf_dw_b,
             conf_ln_g=v_conf_ln_g, conf_ln_b=v_conf_ln_b, w_branch=v_w_branch, w_out=v_w_out, final_g=v_final_g)

    local = (jnp.pad(w_in, ((0, 0), (0, 0), (0, SHARD_PAD - SHARD_COLS))).astype(BF16),
             w_branch.astype(BF16).reshape(depth, N_HEADS * BR, BR), w_out.astype(BF16))
    pick = lambda a, i: lax.dynamic_index_in_dim(a, i, 0, keepdims=False)
    flat = lambda a: a.reshape(-1, a.shape[-1])

    def my_half(l):
        return _row_halves([a[l] for a in local], pc)[0]

    def all_chips(gathered, l):
        return [lax.dynamic_update_index_in_dim(got.reshape((N_CHIPS,) + a.shape[1:]), a[l], chip, 0) for got, a in zip(gathered, local)]

    conv_ch = BR // N_CHIPS
    place = lambda a: lax.dynamic_update_slice_in_dim(jnp.zeros(a.shape[:-1] + (BR,), F32), a * (pc == 0).astype(F32),
                                                      conv_ch * chip, axis=2).reshape(-1, BR)
    short_full, conf_full = _allreduce8([place(short_conv_w), place(conf_dw_w)])
    short_full, conf_full = short_full.reshape(depth, K_SHORT, BR), conf_full.reshape(depth, K_CONF, BR)

    def layer_params(l, shards):
        wi_all, wb_all, wo_all = shards
        return dict(
            norm_g=norm_g[l][None], w_in=_w_in_aligned([wi_all[j] for j in range(N_CHIPS)]),
            f_bias=jnp.pad(f_bias[l], (0, CHUNK - N_HEADS))[None],
            sgu_w=sgu_w[l], sgu_b=sgu_b[l], sgu_ln_g=sgu_ln_g[l][None], sgu_ln_b=sgu_ln_b[l][None],
            short_conv_w=short_full[l], conf_dw_w=conf_full[l], conf_dw_b=conf_dw_b[l][None],
            conf_ln_g=conf_ln_g[l][None], conf_ln_b=conf_ln_b[l][None],
            w_branch=jnp.concatenate([wb_all[j].reshape(N_HEADS, BR, BR) for j in range(N_CHIPS)], axis=2),
            w_out=jnp.concatenate([wo_all[j] for j in range(N_CHIPS)], axis=0))

    shards = all_chips(_gather_chips(my_half(0)), 0)
    params, saved, xs = [], [], x[0]
    for l in range(depth):
        params.append(layer_params(l, shards))
        nxt = my_half(l + 1) if l + 1 < depth else None
        xs, sv, gathered = _layer_fwd(xs, params[l], next_blocks=nxt)
        saved.append(sv)
        if nxt is not None:
            shards = all_chips(gathered, l + 1)
    loss8, dx, dfg8 = _loss_head(xs, loss_target[0], final_g[None])
    d_final_g = jnp.sum(dfg8, axis=0)
    loss = lax.psum(loss8[0, 0], ("x", "y", "c"))

    def by_chip(gr):
        return [jnp.stack([_w_in_shard(gr["w_in"], j) for j in range(N_CHIPS)]),
                jnp.stack([gr["w_branch"][:, :, j * BR:(j + 1) * BR].reshape(N_HEADS * BR, BR).astype(BF16) for j in range(N_CHIPS)]),
                jnp.stack([gr["w_out"][j * BR:(j + 1) * BR].astype(BF16) for j in range(N_CHIPS)])]

    grads, reduced, pending = [None] * depth, [None] * depth, None
    for l in reversed(range(depth)):
        dx, grads[l], red = _layer_bwd(dx, params[l], saved[l], prev_reduce=None if pending is None else (pc, chip, pending))
        if pending is not None:
            reduced[l + 1] = red
        pending = by_chip(grads[l])
    grad_x = dx
    mine, other = _row_halves(pending, pc)
    halves = [_add_pair(flat(a), b).reshape(a.shape) for a, b in zip(mine, _pair_exchange([flat(a) for a in other]))]
    reduced[0] = [_sum_chips(pick(a, chip), r) for a, r in zip(halves, _chip_exchange(halves))]
    own = [a for red in reduced for a in red]
    rows = [jnp.where(pc == 0, jnp.concatenate([a, b], axis=0), jnp.concatenate([b, a], axis=0))
            for a, b in zip(own, _pair_exchange(own))]
    gi, gb, go = [jnp.stack([rows[3 * l + k] for l in range(depth)]) for k in range(3)]
    g = dict(w_in=gi[:, :, :SHARD_COLS], w_branch=gb.reshape(depth, N_HEADS, BR, BR), w_out=go)

    two_d = lambda a: a.reshape(-1, a.shape[-1])
    small_local = [jnp.stack([grads[l][n] for l in range(depth)]) for n in SMALL[:-1]] + [d_final_g]
    for n, a, b in zip(SMALL, small_local, _allreduce8([two_d(a) for a in small_local])):
        g[n] = b.reshape(a.shape)
    for n in ("short_conv_w", "conf_dw_w"):
        g[n] = lax.dynamic_slice_in_dim(g[n], conv_ch * chip, conv_ch, axis=2)

    delta, new_m, new_v = {}, {}, {}
    for n in ("w_in", "w_branch", "w_out"):
        shp = w[n].shape
        d_, m_, v_ = _adamw(two_d(w[n]), two_d(m[n]), two_d(v[n]), two_d(g[n]))
        delta[n], new_m[n], new_v[n] = d_.reshape(shp), m_.reshape(shp), v_.reshape(shp)
    d_, m_, v_ = _adamw_small(*[[two_d(t[n]) for n in SMALL] for t in (w, m, v, g)])
    for n, a, b, c_ in zip(SMALL, d_, m_, v_):
        delta[n], new_m[n], new_v[n] = a.reshape(w[n].shape), b.reshape(w[n].shape), c_.reshape(w[n].shape)

    return (loss, grad_x[None], *[g[n] for n in WEIGHTS], *[delta[n] for n in WEIGHTS],
            *[new_m[n] for n in WEIGHTS], *[new_v[n] for n in WEIGHTS])
```

```python
import functools
import math

import jax
import jax.numpy as jnp
from jax import lax
from jax.experimental import pallas as pl
from jax.experimental.pallas import tpu as pltpu

F32 = jnp.float32
BF16 = jnp.bfloat16

D_MODEL = 1024
BR = 256
N_HEADS = 4
HEAD_DIM = 64
CHUNK = 128
K_SHORT = 3
K_CONF = 31
EPS = 1e-6
IN_COLS = 7684
SHARD_COLS = IN_COLS // 4
SHARD_PAD = 2048
N_CHIPS = 4

QKV0, CIN0, AUV0, DGLU0 = 0, 768, 1536, 2048
AG0, BG0, CG0, DG0 = 2560, 2816, 3072, 3328
BF0 = 3584
MRG0 = 4096
PCOLS = 8192

V7X_VMEM_BYTES = 64 * 1024 * 1024
VMEM_LIMIT = V7X_VMEM_BYTES * 7 // 8

ADAM_LR, ADAM_B1, ADAM_B2, ADAM_EPS, ADAM_WD, ADAM_STEP = 0.001, 0.9, 0.999, 1e-08, 0.01, 10

MESH = pl.DeviceIdType.MESH
ANY = pl.BlockSpec(memory_space=pl.ANY)
VMEM = pl.BlockSpec(memory_space=pltpu.VMEM)

GELU_C0 = math.sqrt(2.0 / math.pi)
GELU_C1 = 0.044715


def _params(*sem):
    return pltpu.CompilerParams(dimension_semantics=sem, vmem_limit_bytes=VMEM_LIMIT)


def _sigmoid(x):
    return 0.5 * jnp.tanh(0.5 * x) + 0.5


def _silu_and_grad(x):
    s = _sigmoid(x)
    return x * s, s * (1.0 + x * (1.0 - s))


def _gelu_and_grad(z):
    z2 = z * z
    t = jnp.tanh(GELU_C0 * (z + GELU_C1 * z2 * z))
    half = 0.5 * (1.0 + t)
    return z * half, half + 0.5 * z * (1.0 - t * t) * (GELU_C0 * (1.0 + 3.0 * GELU_C1 * z2))


def _ln_fwd(v):
    mu = jnp.mean(v, axis=-1, keepdims=True)
    xc = v - mu
    rs = lax.rsqrt(jnp.mean(xc * xc, axis=-1, keepdims=True) + EPS)
    return xc * rs, rs


def _ln_bwd(d_xh, xh, rs):
    return rs * (d_xh - jnp.mean(d_xh, axis=-1, keepdims=True) - xh * jnp.mean(d_xh * xh, axis=-1, keepdims=True))


def _part8(a):
    return a.reshape(a.shape[0] // 8, 8, a.shape[1]).sum(axis=0)


def _dot(a, b):
    return jnp.dot(a, b, preferred_element_type=F32)


def _dot_nt(a, b):
    return lax.dot_general(a, b, (((1,), (1,)), ((), ())), preferred_element_type=F32)


def _dot_tn(a, b):
    return lax.dot_general(a, b, (((0,), (0,)), ((), ())), preferred_element_type=F32)


def _head_masks(dtype):
    lane = lax.broadcasted_iota(jnp.int32, (1, BR), 1) // HEAD_DIM
    return [(lane == h).astype(dtype) for h in range(N_HEADS)]


def _window(ref, col0, width):
    return ref.at[:, pl.ds(col0, width)]


def _copy_all(pairs, sems):
    cps = [pltpu.make_async_copy(s, d, sems.at[i]) for i, (s, d) in enumerate(pairs)]
    for cp in cps:
        cp.start()
    for cp in cps:
        cp.wait()


def _place():
    return lax.axis_index("x"), lax.axis_index("y"), lax.axis_index("c")


def _other_chips(x, y):
    return [(1 - x, y), (x, 1 - y), (1 - x, 1 - y)]


class _Rider:
    def __init__(self, srcs, out_shapes, per_array, make, through=False):
        self.srcs, self.out_shapes, self.per_array, self.make, self.through = list(srcs), list(out_shapes), per_array, make, through
        self.n = len(self.srcs)

    def scratch(self):
        return [pltpu.SemaphoreType.DMA((self.n, self.per_array)), pltpu.SemaphoreType.DMA((self.n, self.per_array))]


def _ride(rider, first, last, src_refs, dst_refs, sems):
    if rider is None:
        return (lambda: None), (lambda: None)
    cps = rider.make(src_refs, dst_refs, *sems)

    def guarded(cond, fn):
        if cond is True:
            fn()
        else:
            pl.when(cond)(fn)

    def start():
        guarded(first, lambda: [cp.start() for cp in cps] and None)

    def finish():
        guarded(last, lambda: [cp.wait() for cp in cps] and None)

    return start, finish


def _rider_parts(rider, n_in, n_out):
    if rider is None:
        return [], [], [], [], [], {}
    aliases = {n_in + a: n_out + a for a in range(rider.n)} if rider.through else {}
    return rider.srcs, [ANY] * rider.n, [ANY] * rider.n, rider.out_shapes, rider.scratch(), aliases


def _remote(src, dst, send_sems, recv_sems, a, k, to):
    return pltpu.make_async_remote_copy(src_ref=src, dst_ref=dst, send_sem=send_sems.at[a, k], recv_sem=recv_sems.at[a, k],
                                        device_id=to, device_id_type=MESH)


def _gather_send_rider(blocks):
    def make(srcs, dsts, ss, rs):
        x, y, cc = _place()
        return [_remote(srcs[a], dsts[a].at[2 * x + y, cc], ss, rs, a, j, (*chip, cc))
                for j, chip in enumerate(_other_chips(x, y)) for a in range(len(srcs))]
    shapes = [jax.ShapeDtypeStruct((N_CHIPS, 2) + b.shape, b.dtype) for b in blocks]
    return _Rider(blocks, shapes, 3, make)


def _gather_forward_rider(landed):
    def make(srcs, dsts, ss, rs):
        x, y, cc = _place()
        return [_remote(dsts[a].at[2 * px + py, cc], dsts[a].at[2 * px + py, cc], ss, rs, a, j, (x, y, 1 - cc))
                for j, (px, py) in enumerate(_other_chips(x, y)) for a in range(len(dsts))]
    shapes = [jax.ShapeDtypeStruct(b.shape, b.dtype) for b in landed]
    return _Rider(landed, shapes, 3, make, through=True)


def _pair_rider(arrays):
    def make(srcs, dsts, ss, rs):
        x, y, cc = _place()
        return [_remote(srcs[a], dsts[a], ss, rs, a, 0, (x, y, 1 - cc)) for a in range(len(srcs))]
    return _Rider(arrays, [jax.ShapeDtypeStruct(b.shape, b.dtype) for b in arrays], 1, make)


def _chip_rider(arrays):
    def make(srcs, dsts, ss, rs):
        x, y, cc = _place()
        return [_remote(srcs[a].at[2 * px + py], dsts[a].at[k], ss, rs, a, k, (px, py, cc))
                for k, (px, py) in enumerate(_other_chips(x, y)) for a in range(len(srcs))]
    return _Rider(arrays, [jax.ShapeDtypeStruct((3,) + b.shape[1:], b.dtype) for b in arrays], 3, make)


def _inproj_fwd(x, g, w):
    s = x.shape[0]
    tm, tn = min(1024, s), 1024

    def body(x_ref, g_ref, w_ref, proj_ref, h_ref):
        @pl.when(pl.program_id(1) == 0)
        def _():
            xv = x_ref[...]
            r = lax.rsqrt(jnp.mean(xv * xv, axis=-1, keepdims=True) + EPS)
            h_ref[...] = ((xv * r) * g_ref[...]).astype(BF16)
        proj_ref[...] = _dot(h_ref[...], w_ref[...])

    return pl.pallas_call(
        body, name="inproj_fwd", grid=(s // tm, PCOLS // tn),
        in_specs=[pl.BlockSpec((tm, D_MODEL), lambda i, j: (i, 0)), pl.BlockSpec((1, D_MODEL), lambda i, j: (0, 0)),
                  pl.BlockSpec((D_MODEL, tn), lambda i, j: (0, j))],
        out_specs=[pl.BlockSpec((tm, tn), lambda i, j: (i, j)), pl.BlockSpec((tm, D_MODEL), lambda i, j: (i, 0))],
        out_shape=[jax.ShapeDtypeStruct((s, PCOLS), F32), jax.ShapeDtypeStruct((s, D_MODEL), BF16)],
        compiler_params=_params("arbitrary", "arbitrary"))(x, g, w)


def _rms_bwd(dh, x, g):
    r = lax.rsqrt(jnp.mean(x * x, axis=-1, keepdims=True) + EPS)
    xn = x * r
    gy = dh * g
    dx = r * (gy - xn * jnp.mean(xn * gy, axis=-1, keepdims=True))
    return dx, _part8(dh * xn)


def _inproj_bwd_x(dproj, w, x, dxn, g, rider=None):
    s = x.shape[0]
    tm, tk = min(512, s), 4096
    nk = PCOLS // tk
    ni = s // tm
    r_in, r_in_specs, r_out_specs, r_shapes, r_scratch, r_alias = _rider_parts(rider, 5, 2)
    nr = len(r_in)

    def body(*refs):
        dp_ref, w_ref, x_ref, dxn_ref, g_ref = refs[:5]
        dx_ref, dg_ref = refs[5 + nr:7 + nr]
        acc_ref = refs[7 + 2 * nr]
        i, k = pl.program_id(0), pl.program_id(1)
        start, finish = _ride(rider, (i == 0) & (k == 0), (i == ni - 1) & (k == nk - 1),
                              refs[5:5 + nr], refs[7 + nr:7 + 2 * nr], refs[8 + 2 * nr:])
        start()

        @pl.when(k == 0)
        def _():
            acc_ref[...] = jnp.zeros_like(acc_ref)

        @pl.when((i == 0) & (k == 0))
        def _():
            dg_ref[...] = jnp.zeros_like(dg_ref)

        acc_ref[...] += _dot_nt(dp_ref[...], w_ref[...])

        @pl.when(k == nk - 1)
        def _():
            dx, dg8 = _rms_bwd(acc_ref[...], x_ref[...], g_ref[...])
            dx_ref[...] = dxn_ref[...] + dx
            dg_ref[...] += dg8

        finish()

    outs = pl.pallas_call(
        body, name="inproj_bwd_x", grid=(ni, nk),
        in_specs=[pl.BlockSpec((tm, tk), lambda i, k: (i, k)), pl.BlockSpec((D_MODEL, tk), lambda i, k: (0, k)),
                  pl.BlockSpec((tm, D_MODEL), lambda i, k: (i, 0)), pl.BlockSpec((tm, D_MODEL), lambda i, k: (i, 0)),
                  pl.BlockSpec((1, D_MODEL), lambda i, k: (0, 0))] + r_in_specs,
        out_specs=[pl.BlockSpec((tm, D_MODEL), lambda i, k: (i, 0)), pl.BlockSpec((8, D_MODEL), lambda i, k: (0, 0))] + r_out_specs,
        out_shape=[jax.ShapeDtypeStruct((s, D_MODEL), F32), jax.ShapeDtypeStruct((8, D_MODEL), F32)] + r_shapes,
        scratch_shapes=[pltpu.VMEM((tm, D_MODEL), F32)] + r_scratch, input_output_aliases=r_alias,
        compiler_params=_params("arbitrary", "arbitrary"))(dproj, w, x, dxn, g, *r_in)
    return outs[0], outs[1], outs[2:]


def _inproj_bwd_w(h, dproj):
    s = h.shape[0]
    tn, tk = 1024, min(1024, s)
    nk = s // tk

    def body(h_ref, dp_ref, dw_ref, acc_ref):
        k = pl.program_id(1)

        @pl.when(k == 0)
        def _():
            acc_ref[...] = jnp.zeros_like(acc_ref)

        acc_ref[...] += _dot_tn(h_ref[...], dp_ref[...])

        @pl.when(k == nk - 1)
        def _():
            dw_ref[...] = acc_ref[...].astype(BF16)

    return pl.pallas_call(
        body, name="inproj_bwd_w", grid=(PCOLS // tn, nk),
        in_specs=[pl.BlockSpec((tk, D_MODEL), lambda j, k: (k, 0)), pl.BlockSpec((tk, tn), lambda j, k: (k, j))],
        out_specs=pl.BlockSpec((D_MODEL, tn), lambda j, k: (0, j)),
        out_shape=jax.ShapeDtypeStruct((D_MODEL, PCOLS), BF16),
        scratch_shapes=[pltpu.VMEM((D_MODEL, tn), F32)],
        compiler_params=_params("arbitrary", "arbitrary"))(h, dproj)


def _mix_a_chunk(uvp, agp, wm_ref, bias, lg, lb):
    u, du = _gelu_and_grad(uvp[:, :BR])
    v, dv = _gelu_and_grad(uvp[:, BR:])
    xh, rs = _ln_fwd(v)
    vnb = (xh * lg + lb).astype(BF16)
    masks = _head_masks(BF16)
    mixed = bias
    for h in range(N_HEADS):
        mixed = mixed + _dot(wm_ref[h], vnb * masks[h])
    sg, dsg = _silu_and_grad(agp)
    return u, du, dv, xh, rs, vnb, masks, mixed, sg, dsg


def _store_masked_sgu(sw_ref, wm_ref):
    row = lax.broadcasted_iota(jnp.int32, (CHUNK, CHUNK), 0)
    col = lax.broadcasted_iota(jnp.int32, (CHUNK, CHUNK), 1)
    for h in range(N_HEADS):
        wm_ref[h] = jnp.where(row >= col, sw_ref[h], 0.0).astype(BF16)


def _mix_a_fwd(proj, sgu_w, bias, lg, lb):
    s = proj.shape[0]

    def body(proj_ref, sw_ref, bias_ref, lg_ref, lb_ref, y_ref, uv_buf, ag_buf, wm_ref, sems):
        _copy_all([(_window(proj_ref, AUV0, 2 * BR), uv_buf), (_window(proj_ref, AG0, BR), ag_buf)], sems)
        _store_masked_sgu(sw_ref, wm_ref)

        def chunk(i, c):
            rows = pl.ds(pl.multiple_of(i * CHUNK, CHUNK), CHUNK)
            u, _, _, _, _, _, _, mixed, sg, _ = _mix_a_chunk(uv_buf[rows, :], ag_buf[rows, :], wm_ref, bias_ref[...],
                                                            lg_ref[...], lb_ref[...])
            y_ref[rows, :] = (u * mixed * sg).astype(BF16)
            return c

        lax.fori_loop(0, s // CHUNK, chunk, 0)

    return pl.pallas_call(
        body, name="mix_a_fwd", in_specs=[ANY, VMEM, VMEM, VMEM, VMEM], out_specs=VMEM,
        out_shape=jax.ShapeDtypeStruct((s, BR), BF16),
        scratch_shapes=[pltpu.VMEM((s, 2 * BR), F32), pltpu.VMEM((s, BR), F32), pltpu.VMEM((N_HEADS, CHUNK, CHUNK), BF16),
                        pltpu.SemaphoreType.DMA((2,))],
        compiler_params=_params())(proj, sgu_w, bias, lg, lb)


def _mix_a_bwd(proj, dproj, dy, sgu_w, bias, lg, lb):
    s = proj.shape[0]

    def body(proj_ref, dproj_in, dy_ref, sw_ref, bias_ref, lg_ref, lb_ref,
             dproj_ref, dsw_ref, dbias_ref, dlg_ref, dlb_ref,
             uv_buf, ag_buf, duv_buf, dag_buf, wm_ref, acc_lg, acc_lb, sems):
        del dproj_in
        _copy_all([(_window(proj_ref, AUV0, 2 * BR), uv_buf), (_window(proj_ref, AG0, BR), ag_buf)], sems)
        _store_masked_sgu(sw_ref, wm_ref)
        dsw_ref[...] = jnp.zeros_like(dsw_ref)
        dbias_ref[...] = jnp.zeros_like(dbias_ref)
        acc_lg[...] = jnp.zeros_like(acc_lg)
        acc_lb[...] = jnp.zeros_like(acc_lb)

        def chunk(i, c):
            rows = pl.ds(pl.multiple_of(i * CHUNK, CHUNK), CHUNK)
            lg_v = lg_ref[...]
            u, du, dv, xh, rs, vnb, masks, mixed, sg, dsg = _mix_a_chunk(
                uv_buf[rows, :], ag_buf[rows, :], wm_ref, bias_ref[...], lg_v, lb_ref[...])
            dyv = dy_ref[rows, :]
            t1 = dyv * sg
            d_u = t1 * mixed
            d_mixed = t1 * u
            d_ag = dyv * u * mixed * dsg
            dbias_ref[...] += d_mixed
            dmb = d_mixed.astype(BF16)
            d_vn = jnp.zeros((CHUNK, BR), F32)
            for h in range(N_HEADS):
                dm_h = dmb * masks[h]
                dsw_ref[h] += _dot_nt(dm_h, vnb)
                d_vn = d_vn + _dot_tn(wm_ref[h], dm_h)
            acc_lg[...] += _part8(d_vn * xh)
            acc_lb[...] += _part8(d_vn)
            d_v = _ln_bwd(d_vn * lg_v, xh, rs)
            duv_buf[rows, :] = jnp.concatenate([d_u * du, d_v * dv], axis=1).astype(BF16)
            dag_buf[rows, :] = d_ag.astype(BF16)
            return c

        lax.fori_loop(0, s // CHUNK, chunk, 0)
        row = lax.broadcasted_iota(jnp.int32, (CHUNK, CHUNK), 0)
        col = lax.broadcasted_iota(jnp.int32, (CHUNK, CHUNK), 1)
        for h in range(N_HEADS):
            dsw_ref[h] = jnp.where(row >= col, dsw_ref[h], 0.0)
        dlg_ref[...] = jnp.sum(acc_lg[...], axis=0, keepdims=True)
        dlb_ref[...] = jnp.sum(acc_lb[...], axis=0, keepdims=True)
        _copy_all([(duv_buf, _window(dproj_ref, AUV0, 2 * BR)), (dag_buf, _window(dproj_ref, AG0, BR))], sems)

    return pl.pallas_call(
        body, name="mix_a_bwd", in_specs=[ANY, ANY, VMEM, VMEM, VMEM, VMEM, VMEM],
        out_specs=[ANY, VMEM, VMEM, VMEM, VMEM],
        out_shape=[jax.ShapeDtypeStruct(dproj.shape, dproj.dtype), jax.ShapeDtypeStruct((N_HEADS, CHUNK, CHUNK), F32),
                   jax.ShapeDtypeStruct((CHUNK, BR), F32), jax.ShapeDtypeStruct((1, BR), F32), jax.ShapeDtypeStruct((1, BR), F32)],
        scratch_shapes=[pltpu.VMEM((s, 2 * BR), F32), pltpu.VMEM((s, BR), F32), pltpu.VMEM((s, 2 * BR), BF16),
                        pltpu.VMEM((s, BR), BF16), pltpu.VMEM((N_HEADS, CHUNK, CHUNK), BF16),
                        pltpu.VMEM((8, BR), F32), pltpu.VMEM((8, BR), F32), pltpu.SemaphoreType.DMA((2,))],
        input_output_aliases={1: 0}, compiler_params=_params())(proj, dproj, dy, sgu_w, bias, lg, lb)


def _tri_ones(n, upper):
    row = lax.broadcasted_iota(jnp.int32, (n, n), 0)
    col = lax.broadcasted_iota(jnp.int32, (n, n), 1)
    return ((row <= col) if upper else (row >= col)).astype(BF16)


def _split3(c):
    hi = c.astype(BF16)
    r1 = c - hi.astype(F32)
    mid = r1.astype(BF16)
    lo = (r1 - mid.astype(F32)).astype(BF16)
    return [hi, mid, lo]


def _tri_sum(tri, a):
    parts = _split3(a)
    return _dot(tri, parts[0]) + _dot(tri, parts[1]) + _dot(tri, parts[2])


EXT = 2 * HEAD_DIM
LANE_CQ = HEAD_DIM
LANE_CK = HEAD_DIM + 3


def _to_head(h):
    r = lax.broadcasted_iota(jnp.int32, (BR, EXT), 0)
    c = lax.broadcasted_iota(jnp.int32, (BR, EXT), 1)
    return ((r == c + h * HEAD_DIM) & (c < HEAD_DIM)).astype(BF16)


def _from_head(h):
    r = lax.broadcasted_iota(jnp.int32, (EXT, BR), 0)
    c = lax.broadcasted_iota(jnp.int32, (EXT, BR), 1)
    return ((c == r + h * HEAD_DIM) & (r < HEAD_DIM)).astype(BF16)


def _attn_prep_fwd(proj, f_bias):
    s = proj.shape[0]
    scale = 1.0 / math.sqrt(HEAD_DIM)

    def body(proj_ref, fb_ref, qe_ref, ke_ref, ve_ref, ket_ref, vet_ref, qkv_buf, z_buf, sems):
        _copy_all([(_window(proj_ref, QKV0, 3 * BR), qkv_buf), (_window(proj_ref, BF0, CHUNK), z_buf)], sems)
        tri = _tri_ones(CHUNK, upper=False)
        lane = lax.broadcasted_iota(jnp.int32, (CHUNK, EXT), 1)
        ones_q = ((lane >= LANE_CK) & (lane < LANE_CK + 3)).astype(F32)
        ones_k = ((lane >= LANE_CQ) & (lane < LANE_CQ + 3)).astype(F32)

        def chunk(i, carry):
            rows = pl.ds(pl.multiple_of(i * CHUNK, CHUNK), CHUNK)
            cum = _tri_sum(tri, jax.nn.log_sigmoid(z_buf[rows, :] + fb_ref[...])) + carry
            qb = (qkv_buf[rows, 0:BR] * scale).astype(BF16)
            kb = qkv_buf[rows, BR:2 * BR].astype(BF16)
            vb = qkv_buf[rows, 2 * BR:].astype(BF16)
            parts = [p.astype(F32) for p in _split3(cum)]
            for h in range(N_HEADS):
                sel = _to_head(h)
                dec_q, dec_k = ones_q, ones_k
                for t, part in enumerate(parts):
                    pf = part[:, h:h + 1]
                    dec_q = dec_q + jnp.where(lane == LANE_CQ + t, pf, 0.0)
                    dec_k = dec_k - jnp.where(lane == LANE_CK + t, pf, 0.0)
                qe_ref[h, rows, :] = (_dot(qb, sel) + dec_q).astype(BF16)
                kh = _dot(kb, sel) + dec_k
                vh = _dot(vb, sel)
                ke_ref[h, rows, :] = kh.astype(BF16)
                ve_ref[h, rows, :] = vh.astype(BF16)
                ket_ref[h, :, rows] = kh.T.astype(BF16)
                vet_ref[h, :, rows] = vh.T.astype(BF16)
            return cum[CHUNK - 1:CHUNK, :]

        lax.fori_loop(0, s // CHUNK, chunk, jnp.zeros((1, CHUNK), F32))

    shape = jax.ShapeDtypeStruct((N_HEADS, s, EXT), BF16)
    shape_t = jax.ShapeDtypeStruct((N_HEADS, EXT, s), BF16)
    return pl.pallas_call(
        body, name="attn_prep_fwd", in_specs=[ANY, VMEM], out_specs=[VMEM] * 5, out_shape=[shape, shape, shape, shape_t, shape_t],
        scratch_shapes=[pltpu.VMEM((s, 3 * BR), F32), pltpu.VMEM((s, CHUNK), F32), pltpu.SemaphoreType.DMA((2,))],
        compiler_params=_params())(proj, f_bias)


def _attn_prep_bwd(proj, dproj, dqe, dke, dve, f_bias):
    s = proj.shape[0]
    pad = MRG0 - BF0
    scale = 1.0 / math.sqrt(HEAD_DIM)

    def body(proj_ref, dproj_in, dq_ref, dk_ref, dv_ref, fb_ref, dproj_ref, dfb_ref, z_buf, dqkv_buf, dz_buf, sems):
        del dproj_in
        _copy_all([(_window(proj_ref, BF0, CHUNK), z_buf)], sems)
        tri = _tri_ones(CHUNK, upper=True)
        lane = lax.broadcasted_iota(jnp.int32, (CHUNK, CHUNK), 1)
        dz_buf[...] = jnp.zeros_like(dz_buf)
        n = s // CHUNK

        def chunk(t, carry):
            suffix, acc = carry
            i = n - 1 - t
            rows = pl.ds(pl.multiple_of(i * CHUNK, CHUNK), CHUNK)
            dq = jnp.zeros((CHUNK, BR), F32)
            dk = jnp.zeros((CHUNK, BR), F32)
            dv = jnp.zeros((CHUNK, BR), F32)
            dcum = jnp.zeros((CHUNK, CHUNK), F32)
            for h in range(N_HEADS):
                back = _from_head(h)
                dqh = dq_ref[h, :, rows].T
                dkh = dk_ref[h, rows, :]
                dq = dq + _dot((dqh * scale).astype(BF16), back)
                dk = dk + _dot(dkh.astype(BF16), back)
                dv = dv + _dot(dv_ref[h, rows, :].astype(BF16), back)
                dcum = dcum + jnp.where(lane == h, dqh[:, LANE_CQ:LANE_CQ + 1] - dkh[:, LANE_CK:LANE_CK + 1], 0.0)
            dqkv_buf[rows, 0:BR] = dq.astype(BF16)
            dqkv_buf[rows, BR:2 * BR] = dk.astype(BF16)
            dqkv_buf[rows, 2 * BR:] = dv.astype(BF16)
            dlf = _tri_sum(tri, dcum) + suffix
            dz = dlf * _sigmoid(-(z_buf[rows, :] + fb_ref[...]))
            dz_buf[rows, pl.ds(0, CHUNK)] = dz.astype(BF16)
            return dlf[0:1, :], acc + _part8(dz)

        _, acc = lax.fori_loop(0, n, chunk, (jnp.zeros((1, CHUNK), F32), jnp.zeros((8, CHUNK), F32)))
        dfb_ref[...] = jnp.sum(acc, axis=0, keepdims=True)
        _copy_all([(dqkv_buf, _window(dproj_ref, QKV0, 3 * BR)), (dz_buf, _window(dproj_ref, BF0, pad))], sems)

    return pl.pallas_call(
        body, name="attn_prep_bwd", in_specs=[ANY, ANY, VMEM, VMEM, VMEM, VMEM], out_specs=[ANY, VMEM],
        out_shape=[jax.ShapeDtypeStruct(dproj.shape, dproj.dtype), jax.ShapeDtypeStruct((1, CHUNK), F32)],
        scratch_shapes=[pltpu.VMEM((s, CHUNK), F32), pltpu.VMEM((s, 3 * BR), BF16), pltpu.VMEM((s, pad), BF16),
                        pltpu.SemaphoreType.DMA((2,))],
        input_output_aliases={1: 0}, compiler_params=_params())(proj, dproj, dqe, dke, dve, f_bias)


ATT_TQ = 256
ATT_FWD_GROUP = 4
ATT_BWD_GROUP = 2
NEG_BIG = -1e30


def _causal_t(q0, k0, tk, tq):
    kpos = k0 + lax.broadcasted_iota(jnp.int32, (tk, tq), 0)
    qpos = q0 + lax.broadcasted_iota(jnp.int32, (tk, tq), 1)
    return kpos <= qpos


STAT_ROWS = 8


def _attn_fwd(qe, ke, vet, rider=None):
    nh, s, w = qe.shape
    tq = min(ATT_TQ, s)
    tk = min(ATT_FWD_GROUP * tq, s)
    per = tk // tq
    nq = s // tq
    r_in, r_in_specs, r_out_specs, r_shapes, r_scratch, r_alias = _rider_parts(rider, 3, 2)
    nr = len(r_in)

    def body(*refs):
        q_ref, k_ref, vt_ref = refs[:3]
        ot_ref, lse_ref = refs[3 + nr:5 + nr]
        h, i = pl.program_id(0), pl.program_id(1)
        start, finish = _ride(rider, (h == 0) & (i == 0), (h == nh - 1) & (i == nq - 1),
                              refs[3:3 + nr], refs[5 + nr:5 + 2 * nr], refs[5 + 2 * nr:])
        start()
        q = q_ref[0]

        def step(k0, carry, width, masked):
            m, l, acc = carry
            ks = pl.ds(pl.multiple_of(k0, tq), width)
            st = _dot_nt(k_ref[0, ks, :], q)
            if masked:
                st = jnp.where(_causal_t(i * tq, k0, width, tq), st, NEG_BIG)
            m_new = jnp.maximum(m, jnp.max(st, axis=0, keepdims=True))
            alpha = jnp.exp(m - m_new)
            pt = jnp.exp(st - m_new)
            l = alpha * l + jnp.sum(pt, axis=0, keepdims=True)
            acc = alpha * acc + _dot(vt_ref[0, :, ks], pt.astype(BF16))
            return m_new, l, acc

        full = i // per
        init = (jnp.full((1, tq), NEG_BIG, F32), jnp.zeros((1, tq), F32), jnp.zeros((w, tq), F32))
        carry = lax.fori_loop(0, full, lambda j, c: step(j * tk, c, tk, False), init)
        m, l, acc = lax.switch(i % per, [functools.partial(step, width=(r + 1) * tq, masked=True) for r in range(per)],
                               full * tk, carry)
        ot_ref[0] = acc / l
        lse_ref[0] = jnp.broadcast_to(m + jnp.log(l), (STAT_ROWS, tq))
        finish()

    outs = pl.pallas_call(
        body, name="attn_fwd", grid=(nh, nq),
        in_specs=[pl.BlockSpec((1, tq, w), lambda h, i: (h, i, 0)), pl.BlockSpec((1, s, w), lambda h, i: (h, 0, 0)),
                  pl.BlockSpec((1, w, s), lambda h, i: (h, 0, 0))] + r_in_specs,
        out_specs=[pl.BlockSpec((1, w, tq), lambda h, i: (h, 0, i)), pl.BlockSpec((1, STAT_ROWS, tq), lambda h, i: (h, 0, i))] + r_out_specs,
        out_shape=[jax.ShapeDtypeStruct((nh, w, s), F32), jax.ShapeDtypeStruct((nh, STAT_ROWS, s), F32)] + r_shapes,
        scratch_shapes=r_scratch, input_output_aliases=r_alias,
        compiler_params=_params("arbitrary", "arbitrary"))(qe, ke, vet, *r_in)
    return outs[0], outs[1], outs[2:]


def _attn_bwd(qe, ke, ket, ve, doe, lse, dd, rider=None):
    nh, s, w = qe.shape
    tq = min(ATT_TQ, s)
    tk = min(ATT_BWD_GROUP * tq, s)
    per = tk // tq
    nq = s // tq
    r_in, r_in_specs, r_out_specs, r_shapes, r_scratch, r_alias = _rider_parts(rider, 7, 3)
    nr = len(r_in)

    def body(*refs):
        q_ref, do_ref, lse_ref, dd_ref, k_ref, kt_ref, v_ref = refs[:7]
        dqt_ref, dk_ref, dv_ref = refs[7 + nr:10 + nr]
        h, i = pl.program_id(0), pl.program_id(1)
        start, finish = _ride(rider, (h == 0) & (i == 0), (h == nh - 1) & (i == nq - 1),
                              refs[7:7 + nr], refs[10 + nr:10 + 2 * nr], refs[10 + 2 * nr:])
        start()

        @pl.when(i == 0)
        def _():
            dk_ref[...] = jnp.zeros_like(dk_ref)
            dv_ref[...] = jnp.zeros_like(dv_ref)

        q = q_ref[0]
        do = do_ref[0]
        lse_row = lse_ref[0, 0:1, :]
        dd_row = dd_ref[0, 0:1, :]

        def step(k0, dqt, width, masked):
            ks = pl.ds(pl.multiple_of(k0, tq), width)
            st = _dot_nt(k_ref[0, ks, :], q)
            pt = jnp.exp(st - lse_row)
            if masked:
                pt = jnp.where(_causal_t(i * tq, k0, width, tq), pt, 0.0)
            dpt = _dot_nt(v_ref[0, ks, :], do)
            dst = (pt * (dpt - dd_row)).astype(BF16)
            dv_ref[0, ks, :] += _dot(pt.astype(BF16), do)
            dk_ref[0, ks, :] += _dot(dst, q)
            return dqt + _dot(kt_ref[0, :, ks], dst)

        full = i // per
        dqt = lax.fori_loop(0, full, lambda j, c: step(j * tk, c, tk, False), jnp.zeros((w, tq), F32))
        dqt_ref[0] = lax.switch(i % per, [functools.partial(step, width=(r + 1) * tq, masked=True) for r in range(per)],
                                full * tk, dqt)
        finish()

    qblk = pl.BlockSpec((1, tq, w), lambda h, i: (h, i, 0))
    stat = pl.BlockSpec((1, STAT_ROWS, tq), lambda h, i: (h, 0, i))
    whole = pl.BlockSpec((1, s, w), lambda h, i: (h, 0, 0))
    whole_t = pl.BlockSpec((1, w, s), lambda h, i: (h, 0, 0))
    outs = pl.pallas_call(
        body, name="attn_bwd", grid=(nh, nq),
        in_specs=[qblk, qblk, stat, stat, whole, whole_t, whole] + r_in_specs,
        out_specs=[pl.BlockSpec((1, w, tq), lambda h, i: (h, 0, i)), whole, whole] + r_out_specs,
        out_shape=[jax.ShapeDtypeStruct((nh, w, s), F32), jax.ShapeDtypeStruct((nh, s, w), F32),
                   jax.ShapeDtypeStruct((nh, s, w), F32)] + r_shapes,
        scratch_shapes=r_scratch, input_output_aliases=r_alias,
        compiler_params=_params("arbitrary", "arbitrary"))(qe, doe, lse, dd, ke, ket, ve, *r_in)
    return outs[0], outs[1], outs[2], outs[3:]


def _bgate_fwd(proj, ot, rider=None):
    s = proj.shape[0]
    r_in, r_in_specs, r_out_specs, r_shapes, r_scratch, r_alias = _rider_parts(rider, 2, 2)
    nr = len(r_in)

    def body(*refs):
        proj_ref, ot_ref = refs[:2]
        y_ref, o_ref = refs[2 + nr:4 + nr]
        g_buf, sems = refs[4 + 2 * nr:6 + 2 * nr]
        start, finish = _ride(rider, True, True, refs[2:2 + nr], refs[4 + nr:4 + 2 * nr], refs[6 + 2 * nr:])
        start()
        _copy_all([(_window(proj_ref, BG0, BR), g_buf)], sems)

        def chunk(i, c):
            rows = pl.ds(pl.multiple_of(i * CHUNK, CHUNK), CHUNK)
            o = jnp.zeros((CHUNK, BR), F32)
            for h in range(N_HEADS):
                back = _from_head(h)
                for part in _split3(ot_ref[h, :, rows].T):
                    o = o + _dot(part, back)
            sg, _ = _silu_and_grad(g_buf[rows, :])
            o_ref[rows, :] = o
            y_ref[rows, :] = (o * sg).astype(BF16)
            return c

        lax.fori_loop(0, s // CHUNK, chunk, 0)
        finish()

    outs = pl.pallas_call(
        body, name="bgate_fwd", in_specs=[ANY, VMEM] + r_in_specs, out_specs=[VMEM, VMEM] + r_out_specs,
        out_shape=[jax.ShapeDtypeStruct((s, BR), BF16), jax.ShapeDtypeStruct((s, BR), F32)] + r_shapes,
        scratch_shapes=[pltpu.VMEM((s, BR), F32), pltpu.SemaphoreType.DMA((1,))] + r_scratch,
        input_output_aliases=r_alias, compiler_params=_params())(proj, ot, *r_in)
    return outs[0], outs[1], outs[2:]


def _bgate_bwd(proj, dproj, o, dy):
    s = proj.shape[0]

    def body(proj_ref, dproj_in, o_ref, dy_ref, dproj_ref, do_ref, dd_ref, g_buf, dg_buf, sems):
        del dproj_in
        _copy_all([(_window(proj_ref, BG0, BR), g_buf)], sems)
        lane = lax.broadcasted_iota(jnp.int32, (BR, CHUNK), 0) // HEAD_DIM
        col = lax.broadcasted_iota(jnp.int32, (BR, CHUNK), 1)
        sel = (lane == col).astype(BF16)

        def chunk(i, c):
            rows = pl.ds(pl.multiple_of(i * CHUNK, CHUNK), CHUNK)
            sg, dsg = _silu_and_grad(g_buf[rows, :])
            dyv = dy_ref[rows, :]
            ov = o_ref[rows, :]
            do = dyv * sg
            dg_buf[rows, :] = (dyv * ov * dsg).astype(BF16)
            prod = _split3(do * ov)
            ddt = (_dot(prod[0], sel) + _dot(prod[1], sel) + _dot(prod[2], sel)).T
            dob = do.astype(BF16)
            for h in range(N_HEADS):
                do_ref[h, rows, :] = _dot(dob, _to_head(h)).astype(BF16)
                dd_ref[h, :, rows] = jnp.broadcast_to(ddt[h:h + 1, :], (STAT_ROWS, CHUNK))
            return c

        lax.fori_loop(0, s // CHUNK, chunk, 0)
        _copy_all([(dg_buf, _window(dproj_ref, BG0, BR))], sems)

    return pl.pallas_call(
        body, name="bgate_bwd", in_specs=[ANY, ANY, VMEM, VMEM], out_specs=[ANY, VMEM, VMEM],
        out_shape=[jax.ShapeDtypeStruct(dproj.shape, dproj.dtype), jax.ShapeDtypeStruct((N_HEADS, s, EXT), BF16),
                   jax.ShapeDtypeStruct((N_HEADS, STAT_ROWS, s), F32)],
        scratch_shapes=[pltpu.VMEM((s, BR), F32), pltpu.VMEM((s, BR), BF16), pltpu.SemaphoreType.DMA((1,))],
        input_output_aliases={1: 0}, compiler_params=_params())(proj, dproj, o, dy)


C_PAD = 8


def _mix_c_fwd(proj, w):
    s = proj.shape[0]

    def body(proj_ref, w_ref, y_ref, cin_buf, g_buf, z_buf, sems):
        _copy_all([(_window(proj_ref, CIN0, 3 * BR), cin_buf), (_window(proj_ref, CG0, BR), g_buf)], sems)
        z_buf[pl.ds(0, C_PAD), :] = jnp.zeros((C_PAD, BR), F32)

        def fill(i, c):
            rows = pl.ds(pl.multiple_of(i * CHUNK, CHUNK), CHUNK)
            z_buf[pl.ds(pl.multiple_of(i * CHUNK + C_PAD, 8), CHUNK), :] = cin_buf[rows, BR:2 * BR] * cin_buf[rows, 2 * BR:]
            return c

        lax.fori_loop(0, s // CHUNK, fill, 0)

        def chunk(i, c):
            r0 = pl.multiple_of(i * CHUNK, CHUNK)
            rows = pl.ds(r0, CHUNK)
            ze = z_buf[pl.ds(r0, CHUNK + C_PAD), :]
            conv = jnp.zeros((CHUNK, BR), F32)
            for k in range(K_SHORT):
                off = C_PAD - (K_SHORT - 1) + k
                conv = conv + w_ref[k:k + 1, :] * ze[off:off + CHUNK]
            sg, _ = _silu_and_grad(g_buf[rows, :])
            y_ref[rows, :] = (cin_buf[rows, 0:BR] * conv * sg).astype(BF16)
            return c

        lax.fori_loop(0, s // CHUNK, chunk, 0)

    return pl.pallas_call(
        body, name="mix_c_fwd", in_specs=[ANY, VMEM], out_specs=VMEM, out_shape=jax.ShapeDtypeStruct((s, BR), BF16),
        scratch_shapes=[pltpu.VMEM((s, 3 * BR), F32), pltpu.VMEM((s, BR), F32), pltpu.VMEM((s + C_PAD, BR), F32),
                        pltpu.SemaphoreType.DMA((2,))],
        compiler_params=_params())(proj, w)


def _mix_c_bwd(proj, dproj, dy, w):
    s = proj.shape[0]

    def body(proj_ref, dproj_in, dy_ref, w_ref, dproj_ref, dw_ref, cin_buf, g_buf, z_buf, dc_buf, dcin_buf, dg_buf, acc, sems):
        del dproj_in
        _copy_all([(_window(proj_ref, CIN0, 3 * BR), cin_buf), (_window(proj_ref, CG0, BR), g_buf)], sems)
        z_buf[pl.ds(0, C_PAD), :] = jnp.zeros((C_PAD, BR), F32)
        dc_buf[pl.ds(s, C_PAD), :] = jnp.zeros((C_PAD, BR), F32)
        acc[...] = jnp.zeros_like(acc)

        def fill(i, c):
            rows = pl.ds(pl.multiple_of(i * CHUNK, CHUNK), CHUNK)
            z_buf[pl.ds(pl.multiple_of(i * CHUNK + C_PAD, 8), CHUNK), :] = cin_buf[rows, BR:2 * BR] * cin_buf[rows, 2 * BR:]
            return c

        lax.fori_loop(0, s // CHUNK, fill, 0)

        def chunk(i, c):
            r0 = pl.multiple_of(i * CHUNK, CHUNK)
            rows = pl.ds(r0, CHUNK)
            ze = z_buf[pl.ds(r0, CHUNK + C_PAD), :]
            taps = [ze[C_PAD - (K_SHORT - 1) + k:C_PAD - (K_SHORT - 1) + k + CHUNK] for k in range(K_SHORT)]
            conv = jnp.zeros((CHUNK, BR), F32)
            for k in range(K_SHORT):
                conv = conv + w_ref[k:k + 1, :] * taps[k]
            sg, dsg = _silu_and_grad(g_buf[rows, :])
            bg = cin_buf[rows, 0:BR]
            dyv = dy_ref[rows, :]
            dconv = dyv * bg * sg
            dc_buf[rows, :] = dconv
            dcin_buf[rows, 0:BR] = (dyv * conv * sg).astype(BF16)
            dg_buf[rows, :] = (dyv * bg * conv * dsg).astype(BF16)
            for k in range(K_SHORT):
                acc[k] += _part8(dconv * taps[k])
            return c

        lax.fori_loop(0, s // CHUNK, chunk, 0)

        def chunk2(i, c):
            r0 = pl.multiple_of(i * CHUNK, CHUNK)
            rows = pl.ds(r0, CHUNK)
            de = dc_buf[pl.ds(r0, CHUNK + C_PAD), :]
            dz = jnp.zeros((CHUNK, BR), F32)
            for k in range(K_SHORT):
                off = K_SHORT - 1 - k
                dz = dz + w_ref[k:k + 1, :] * de[off:off + CHUNK]
            dcin_buf[rows, BR:2 * BR] = (dz * cin_buf[rows, 2 * BR:]).astype(BF16)
            dcin_buf[rows, 2 * BR:] = (dz * cin_buf[rows, BR:2 * BR]).astype(BF16)
            return c

        lax.fori_loop(0, s // CHUNK, chunk2, 0)
        dw_ref[...] = jnp.zeros_like(dw_ref)
        for k in range(K_SHORT):
            dw_ref[k:k + 1, :] = jnp.sum(acc[k], axis=0, keepdims=True)
        _copy_all([(dcin_buf, _window(dproj_ref, CIN0, 3 * BR)), (dg_buf, _window(dproj_ref, CG0, BR))], sems)

    return pl.pallas_call(
        body, name="mix_c_bwd", in_specs=[ANY, ANY, VMEM, VMEM], out_specs=[ANY, VMEM],
        out_shape=[jax.ShapeDtypeStruct(dproj.shape, dproj.dtype), jax.ShapeDtypeStruct((8, BR), F32)],
        scratch_shapes=[pltpu.VMEM((s, 3 * BR), F32), pltpu.VMEM((s, BR), F32), pltpu.VMEM((s + C_PAD, BR), F32),
                        pltpu.VMEM((s + C_PAD, BR), F32), pltpu.VMEM((s, 3 * BR), BF16), pltpu.VMEM((s, BR), BF16),
                        pltpu.VMEM((K_SHORT, 8, BR), F32), pltpu.SemaphoreType.DMA((2,))],
        input_output_aliases={1: 0}, compiler_params=_params())(proj, dproj, dy, w)


D_PAD = 32


def _tap_windows(win, offsets, n):
    rows = win.shape[0]
    rolled, out = {}, []
    for off in offsets:
        r = off % 8
        if r not in rolled:
            rolled[r] = win if r == 0 else pltpu.roll(win, rows - r, 0)
        out.append(rolled[r][off - r:off - r + n])
    return out


def _mix_d_common(cin_ref, g_ref, w_ref, b_ref, lg_ref, lb_ref, hh_buf, r0):
    he = hh_buf[pl.ds(r0, CHUNK + D_PAD), :]
    taps = _tap_windows(he, [D_PAD - (K_CONF - 1) + k for k in range(K_CONF)], CHUNK)
    conv = jnp.zeros((CHUNK, BR), F32) + b_ref[...]
    for k in range(K_CONF):
        conv = conv + w_ref[k:k + 1, :] * taps[k]
    xh, rs = _ln_fwd(conv)
    sw, dsw = _silu_and_grad(xh * lg_ref[...] + lb_ref[...])
    sg, dsg = _silu_and_grad(g_ref[pl.ds(r0, CHUNK), :])
    return taps, xh, rs, sw, dsw, sg, dsg


def _mix_d_fill(cin_buf, hh_buf, s):
    hh_buf[pl.ds(0, D_PAD), :] = jnp.zeros((D_PAD, BR), F32)

    def fill(i, c):
        rows = pl.ds(pl.multiple_of(i * CHUNK, CHUNK), CHUNK)
        hh_buf[pl.ds(pl.multiple_of(i * CHUNK + D_PAD, 8), CHUNK), :] = cin_buf[rows, 0:BR] * _sigmoid(cin_buf[rows, BR:])
        return c

    lax.fori_loop(0, s // CHUNK, fill, 0)


def _mix_d_fwd(proj, w, b, lg, lb):
    s = proj.shape[0]

    def body(proj_ref, w_ref, b_ref, lg_ref, lb_ref, y_ref, cin_buf, g_buf, hh_buf, sems):
        _copy_all([(_window(proj_ref, DGLU0, 2 * BR), cin_buf), (_window(proj_ref, DG0, BR), g_buf)], sems)
        _mix_d_fill(cin_buf, hh_buf, s)

        def chunk(i, c):
            r0 = pl.multiple_of(i * CHUNK, CHUNK)
            _, _, _, sw, _, sg, _ = _mix_d_common(cin_buf, g_buf, w_ref, b_ref, lg_ref, lb_ref, hh_buf, r0)
            y_ref[pl.ds(r0, CHUNK), :] = (sw * sg).astype(BF16)
            return c

        lax.fori_loop(0, s // CHUNK, chunk, 0)

    return pl.pallas_call(
        body, name="mix_d_fwd", in_specs=[ANY, VMEM, VMEM, VMEM, VMEM], out_specs=VMEM,
        out_shape=jax.ShapeDtypeStruct((s, BR), BF16),
        scratch_shapes=[pltpu.VMEM((s, 2 * BR), F32), pltpu.VMEM((s, BR), F32), pltpu.VMEM((s + D_PAD, BR), F32),
                        pltpu.SemaphoreType.DMA((2,))],
        compiler_params=_params())(proj, w, b, lg, lb)


def _mix_d_bwd(proj, dproj, dy, w, b, lg, lb):
    s = proj.shape[0]

    def body(proj_ref, dproj_in, dy_ref, w_ref, b_ref, lg_ref, lb_ref, dproj_ref, dw_ref, db_ref, dlg_ref, dlb_ref,
             cin_buf, g_buf, hh_buf, dc_buf, dcin_buf, dg_buf, acc_w, acc_s, sems):
        del dproj_in
        _copy_all([(_window(proj_ref, DGLU0, 2 * BR), cin_buf), (_window(proj_ref, DG0, BR), g_buf)], sems)
        _mix_d_fill(cin_buf, hh_buf, s)
        dc_buf[pl.ds(s, D_PAD), :] = jnp.zeros((D_PAD, BR), F32)
        acc_w[...] = jnp.zeros_like(acc_w)
        acc_s[...] = jnp.zeros_like(acc_s)

        def chunk(i, c):
            r0 = pl.multiple_of(i * CHUNK, CHUNK)
            rows = pl.ds(r0, CHUNK)
            taps, xh, rs, sw, dsw, sg, dsg = _mix_d_common(cin_buf, g_buf, w_ref, b_ref, lg_ref, lb_ref, hh_buf, r0)
            dyv = dy_ref[rows, :]
            dg_buf[rows, :] = (dyv * sw * dsg).astype(BF16)
            d_ln = dyv * sg * dsw
            acc_s[0] += _part8(d_ln * xh)
            acc_s[1] += _part8(d_ln)
            dc = _ln_bwd(d_ln * lg_ref[...], xh, rs)
            acc_s[2] += _part8(dc)
            dc_buf[rows, :] = dc
            for k in range(K_CONF):
                acc_w[k] += _part8(dc * taps[k])
            return c

        lax.fori_loop(0, s // CHUNK, chunk, 0)

        def chunk2(i, c):
            r0 = pl.multiple_of(i * CHUNK, CHUNK)
            rows = pl.ds(r0, CHUNK)
            de = dc_buf[pl.ds(r0, CHUNK + D_PAD), :]
            dh = jnp.zeros((CHUNK, BR), F32)
            for k, win in enumerate(_tap_windows(de, [K_CONF - 1 - k for k in range(K_CONF)], CHUNK)):
                dh = dh + w_ref[k:k + 1, :] * win
            a = cin_buf[rows, 0:BR]
            sig = _sigmoid(cin_buf[rows, BR:])
            dcin_buf[rows, 0:BR] = (dh * sig).astype(BF16)
            dcin_buf[rows, BR:] = (dh * a * sig * (1.0 - sig)).astype(BF16)
            return c

        lax.fori_loop(0, s // CHUNK, chunk2, 0)
        dw_ref[...] = jnp.zeros_like(dw_ref)
        for k in range(K_CONF):
            dw_ref[k:k + 1, :] = jnp.sum(acc_w[k], axis=0, keepdims=True)
        dlg_ref[...] = jnp.sum(acc_s[0], axis=0, keepdims=True)
        dlb_ref[...] = jnp.sum(acc_s[1], axis=0, keepdims=True)
        db_ref[...] = jnp.sum(acc_s[2], axis=0, keepdims=True)
        _copy_all([(dcin_buf, _window(dproj_ref, DGLU0, 2 * BR)), (dg_buf, _window(dproj_ref, DG0, BR))], sems)

    vec = jax.ShapeDtypeStruct((1, BR), F32)
    return pl.pallas_call(
        body, name="mix_d_bwd", in_specs=[ANY, ANY, VMEM, VMEM, VMEM, VMEM, VMEM], out_specs=[ANY, VMEM, VMEM, VMEM, VMEM],
        out_shape=[jax.ShapeDtypeStruct(dproj.shape, dproj.dtype), jax.ShapeDtypeStruct((32, BR), F32), vec, vec, vec],
        scratch_shapes=[pltpu.VMEM((s, 2 * BR), F32), pltpu.VMEM((s, BR), F32), pltpu.VMEM((s + D_PAD, BR), F32),
                        pltpu.VMEM((s + D_PAD, BR), F32), pltpu.VMEM((s, 2 * BR), BF16), pltpu.VMEM((s, BR), BF16),
                        pltpu.VMEM((K_CONF, 8, BR), F32), pltpu.VMEM((3, 8, BR), F32), pltpu.SemaphoreType.DMA((2,))],
        input_output_aliases={1: 0}, compiler_params=_params())(proj, dproj, dy, w, b, lg, lb)


MERGE_TM = 256


def _merge_fwd(x, proj, ys, wb, wo):
    s = x.shape[0]
    tm = min(MERGE_TM, s)

    def body(x_ref, lg_ref, ya, yb, yc, yd, wb_ref, wo_ref, xn_ref, mg_ref):
        merged = jnp.zeros((tm, D_MODEL), F32)
        for n, y_ref in enumerate((ya, yb, yc, yd)):
            gate = _sigmoid(lg_ref[:, n * D_MODEL:(n + 1) * D_MODEL])
            merged = merged + gate * _dot(y_ref[...], wb_ref[n])
        mb = merged.astype(BF16)
        mg_ref[...] = mb
        xn_ref[...] = x_ref[...] + _dot(mb, wo_ref[...])

    row = lambda w: pl.BlockSpec((tm, w), lambda i: (i, 0))
    return pl.pallas_call(
        body, name="merge_fwd", grid=(s // tm,),
        in_specs=[row(D_MODEL), pl.BlockSpec((tm, MRG0), lambda i: (i, 1)), row(BR), row(BR), row(BR), row(BR),
                  pl.BlockSpec((N_HEADS, BR, D_MODEL), lambda i: (0, 0, 0)), pl.BlockSpec((D_MODEL, D_MODEL), lambda i: (0, 0))],
        out_specs=[row(D_MODEL), row(D_MODEL)],
        out_shape=[jax.ShapeDtypeStruct((s, D_MODEL), F32), jax.ShapeDtypeStruct((s, D_MODEL), BF16)],
        compiler_params=_params("arbitrary"))(x, proj, *ys, wb, wo)


def _merge_bwd(dxn, proj, ys, merged, wb, wo, rider=None):
    s = dxn.shape[0]
    tm = min(MERGE_TM, s)
    steps = s // tm
    r_in, r_in_specs, r_out_specs, r_shapes, r_scratch, r_alias = _rider_parts(rider, 9, 7)
    nr = len(r_in)

    def body(*refs):
        dx_ref, lg_ref, ya, yb, yc, yd, mg_ref, wb_ref, wo_ref = refs[:9]
        dlg_ref, da, db, dc, dd, dwo_ref, dwb_ref = refs[9 + nr:16 + nr]
        start, finish = _ride(rider, pl.program_id(0) == 0, pl.program_id(0) == steps - 1,
                              refs[9:9 + nr], refs[16 + nr:16 + 2 * nr], refs[16 + 2 * nr:])
        start()

        @pl.when(pl.program_id(0) == 0)
        def _():
            dwo_ref[...] = jnp.zeros_like(dwo_ref)
            dwb_ref[...] = jnp.zeros_like(dwb_ref)

        dxb = dx_ref[...].astype(BF16)
        d_merged = _dot_nt(dxb, wo_ref[...])
        dwo_ref[...] += _dot_tn(mg_ref[...], dxb)
        for n, (y_ref, dy_ref) in enumerate(((ya, da), (yb, db), (yc, dc), (yd, dd))):
            yv = y_ref[...]
            gate = _sigmoid(lg_ref[:, n * D_MODEL:(n + 1) * D_MODEL])
            pn = _dot(yv, wb_ref[n])
            dlg_ref[:, n * D_MODEL:(n + 1) * D_MODEL] = (d_merged * pn * gate * (1.0 - gate)).astype(BF16)
            dpn = (d_merged * gate).astype(BF16)
            dy_ref[...] = _dot_nt(dpn, wb_ref[n])
            dwb_ref[n] += _dot_tn(yv, dpn)
        finish()

    row = lambda w: pl.BlockSpec((tm, w), lambda i: (i, 0))
    wb_spec = pl.BlockSpec((N_HEADS, BR, D_MODEL), lambda i: (0, 0, 0))
    wo_spec = pl.BlockSpec((D_MODEL, D_MODEL), lambda i: (0, 0))
    dy_shape = jax.ShapeDtypeStruct((s, BR), F32)
    outs = pl.pallas_call(
        body, name="merge_bwd", grid=(steps,),
        in_specs=[row(D_MODEL), pl.BlockSpec((tm, MRG0), lambda i: (i, 1)), row(BR), row(BR), row(BR), row(BR), row(D_MODEL),
                  wb_spec, wo_spec] + r_in_specs,
        out_specs=[pl.BlockSpec((tm, MRG0), lambda i: (i, 1)), row(BR), row(BR), row(BR), row(BR), wo_spec, wb_spec] + r_out_specs,
        out_shape=[jax.ShapeDtypeStruct((s, PCOLS), BF16), dy_shape, dy_shape, dy_shape, dy_shape,
                   jax.ShapeDtypeStruct((D_MODEL, D_MODEL), F32), jax.ShapeDtypeStruct((N_HEADS, BR, D_MODEL), F32)] + r_shapes,
        scratch_shapes=r_scratch, input_output_aliases=r_alias,
        compiler_params=_params("arbitrary"))(dxn, proj, *ys, merged, wb, wo, *r_in)
    return outs[0], outs[1:5], outs[5], outs[6], outs[7:]


def _loss_head(x, target, g):
    s = x.shape[0]
    tm = min(512, s)

    def body(x_ref, t_ref, g_ref, loss_ref, dx_ref, dg_ref):
        @pl.when(pl.program_id(0) == 0)
        def _():
            loss_ref[...] = jnp.zeros_like(loss_ref)
            dg_ref[...] = jnp.zeros_like(dg_ref)

        xv = x_ref[...]
        gv = g_ref[...]
        r = lax.rsqrt(jnp.mean(xv * xv, axis=-1, keepdims=True) + EPS)
        xn = xv * r
        err = xn * gv - t_ref[...]
        loss_ref[...] += 0.5 * jnp.sum(jnp.mean(err * err, axis=-1, keepdims=True))
        dy = err * (1.0 / D_MODEL)
        dg_ref[...] += _part8(dy * xn)
        gy = dy * gv
        dx_ref[...] = r * (gy - xn * jnp.mean(xn * gy, axis=-1, keepdims=True))

    row = pl.BlockSpec((tm, D_MODEL), lambda i: (i, 0))
    return pl.pallas_call(
        body, name="loss_head", grid=(s // tm,),
        in_specs=[row, row, pl.BlockSpec((1, D_MODEL), lambda i: (0, 0))],
        out_specs=[pl.BlockSpec((8, 128), lambda i: (0, 0)), row, pl.BlockSpec((8, D_MODEL), lambda i: (0, 0))],
        out_shape=[jax.ShapeDtypeStruct((8, 128), F32), jax.ShapeDtypeStruct((s, D_MODEL), F32), jax.ShapeDtypeStruct((8, D_MODEL), F32)],
        compiler_params=_params("arbitrary"))(x, target, g)


_SEGMENTS = ((0, 512, AUV0), (512, 256, AG0), (768, 768, QKV0), (1536, 4, BF0), (1540, 256, BG0), (1796, 768, CIN0),
             (2564, 256, CG0), (2820, 512, DGLU0), (3332, 256, DG0), (3588, 4096, MRG0))


def _to_aligned_cols(w):
    out, pos = [], 0
    for src, width, dst in sorted(_SEGMENTS, key=lambda t: t[2]):
        if dst > pos:
            out.append(jnp.zeros(w.shape[:-1] + (dst - pos,), w.dtype))
        out.append(w[..., src:src + width])
        pos = dst + width
    return jnp.concatenate(out, axis=-1)


def _from_aligned_cols(g):
    return jnp.concatenate([g[..., dst:dst + width] for _, width, dst in _SEGMENTS], axis=-1)


def _sgu_bias_rows(sgu_b):
    return jnp.repeat(sgu_b.T, HEAD_DIM, axis=1)


def _layer_fwd(x, p, next_blocks=None):
    proj, h = _inproj_fwd(x, p["norm_g"], p["w_in"])
    ya = _mix_a_fwd(proj, p["sgu_w"], _sgu_bias_rows(p["sgu_b"]), p["sgu_ln_g"], p["sgu_ln_b"])
    qe, ke, ve, ket, vet = _attn_prep_fwd(proj, p["f_bias"])
    ot, lse, landed = _attn_fwd(qe, ke, vet, rider=None if next_blocks is None else _gather_send_rider(next_blocks))
    yb, o, gathered = _bgate_fwd(proj, ot, rider=None if next_blocks is None else _gather_forward_rider(landed))
    yc = _mix_c_fwd(proj, p["short_conv_w"])
    yd = _mix_d_fwd(proj, p["conf_dw_w"], p["conf_dw_b"], p["conf_ln_g"], p["conf_ln_b"])
    ys = (ya, yb, yc, yd)
    x_next, merged = _merge_fwd(x, proj, ys, p["w_branch"], p["w_out"])
    saved = dict(x=x, proj=proj, h=h, ys=ys, merged=merged, qe=qe, ke=ke, ve=ve, ket=ket, lse=lse, o=o)
    return x_next, saved, gathered


def _grads_by_chip(d_win, d_wb, d_wo):
    return [jnp.stack([_w_in_shard(d_win, j) for j in range(N_CHIPS)]),
            jnp.stack([d_wb[:, :, j * BR:(j + 1) * BR].reshape(N_HEADS * BR, BR).astype(BF16) for j in range(N_CHIPS)]),
            jnp.stack([d_wo[j * BR:(j + 1) * BR].astype(BF16) for j in range(N_CHIPS)])]


def _layer_bwd(dxn, p, sv, prev_reduce=None, reduce_self=None):
    proj = sv["proj"]
    if prev_reduce is None:
        dproj, dys, d_wo, d_wb, _ = _merge_bwd(dxn, proj, sv["ys"], sv["merged"], p["w_branch"], p["w_out"])
        chip_rider = None
    else:
        pc, chip, arrays = prev_reduce
        mine, other = _row_halves(arrays, pc)
        dproj, dys, d_wo, d_wb, from_sibling = _merge_bwd(dxn, proj, sv["ys"], sv["merged"], p["w_branch"], p["w_out"],
                                                           rider=_pair_rider(other))
        halves = [_add_pair(a.reshape(-1, a.shape[-1]), b.reshape(-1, b.shape[-1])).reshape(a.shape)
                  for a, b in zip(mine, from_sibling)]
        chip_rider = _chip_rider(halves)
    dproj, d_sgu_w, d_bias_rows, d_sgu_lg, d_sgu_lb = _mix_a_bwd(
        proj, dproj, dys[0], p["sgu_w"], _sgu_bias_rows(p["sgu_b"]), p["sgu_ln_g"], p["sgu_ln_b"])
    dproj, doe, dd = _bgate_bwd(proj, dproj, sv["o"], dys[1])
    dqe, dke, dve, from_chips = _attn_bwd(sv["qe"], sv["ke"], sv["ket"], sv["ve"], doe, sv["lse"], dd, rider=chip_rider)
    reduced = None
    if prev_reduce is not None:
        reduced = [_sum_chips(lax.dynamic_index_in_dim(a, chip, 0, keepdims=False), r) for a, r in zip(halves, from_chips)]
    dproj, d_fb = _attn_prep_bwd(proj, dproj, dqe, dke, dve, p["f_bias"])
    dproj, d_sc = _mix_c_bwd(proj, dproj, dys[2], p["short_conv_w"])
    dproj, d_cw, d_cb, d_clg, d_clb = _mix_d_bwd(proj, dproj, dys[3], p["conf_dw_w"], p["conf_dw_b"], p["conf_ln_g"], p["conf_ln_b"])
    d_win = _inproj_bwd_w(sv["h"], dproj)
    reduced_self, self_rider = None, None
    if reduce_self is not None:
        pc, chip = reduce_self
        mine, other = _row_halves(_grads_by_chip(d_win, d_wb, d_wo), pc)
        flat = lambda a: a.reshape(-1, a.shape[-1])
        own_halves = [_add_pair(flat(a), b).reshape(a.shape) for a, b in zip(mine, _pair_exchange([flat(a) for a in other]))]
        self_rider = _chip_rider(own_halves)
    dx, dg8, from_chips_self = _inproj_bwd_x(dproj, p["w_in"], sv["x"], dxn, p["norm_g"], rider=self_rider)
    if reduce_self is not None:
        reduced_self = [_sum_chips(lax.dynamic_index_in_dim(a, chip, 0, keepdims=False), r)
                        for a, r in zip(own_halves, from_chips_self)]
    grads = dict(
        norm_g=jnp.sum(dg8, axis=0), w_in=d_win, f_bias=d_fb[0, :N_HEADS], sgu_w=d_sgu_w,
        sgu_b=d_bias_rows.reshape(CHUNK, N_HEADS, HEAD_DIM).sum(axis=-1).T,
        sgu_ln_g=d_sgu_lg[0], sgu_ln_b=d_sgu_lb[0], short_conv_w=d_sc[:K_SHORT], conf_dw_w=d_cw[:K_CONF],
        conf_dw_b=d_cb[0], conf_ln_g=d_clg[0], conf_ln_b=d_clb[0], w_branch=d_wb, w_out=d_wo)
    return dx, grads, reduced, reduced_self


def _row_halves(arrays, pc):
    half = lambda a, i: lax.dynamic_slice_in_dim(a, i * (a.shape[-2] // 2), a.shape[-2] // 2, axis=a.ndim - 2)
    return [half(a, pc) for a in arrays], [half(a, 1 - pc) for a in arrays]


def _local_step(x, target, layers, final_g):
    saved = []
    for p in layers:
        x, sv, _ = _layer_fwd(x, p)
        saved.append(sv)
    loss8, dx, dfg8 = _loss_head(x, target, final_g)
    grads = [None] * len(layers)
    for l in reversed(range(len(layers))):
        dx, grads[l], _, _ = _layer_bwd(dx, layers[l], saved[l])
    return loss8[0, 0], dx, grads, jnp.sum(dfg8, axis=0)


def _gather_chips(blocks):
    n = len(blocks)

    def body(*refs):
        ins, outs, (send_sems, recv_sems) = refs[:n], refs[n:2 * n], refs[2 * n:]
        x, y, cc = _place()
        me, sibling = (x, y, cc), (x, y, 1 - cc)
        chips = _other_chips(x, y)

        def copy(a, k, chip, layer, to, src=None):
            dst = outs[a].at[2 * chip[0] + chip[1], layer]
            return pltpu.make_async_remote_copy(src_ref=dst if src is None else src, dst_ref=dst, send_sem=send_sems.at[a, k],
                                                recv_sem=recv_sems.at[a, k], device_id=to, device_id_type=MESH)

        first = [copy(a, j, (x, y), cc, (*chip, cc), src=ins[a]) for j, chip in enumerate(chips) for a in range(n)]
        for cp in first:
            cp.start()
        passed = []
        for j, chip in enumerate(chips):
            for a in range(n):
                copy(a, j, chip, cc, me).wait_recv()
                passed.append(copy(a, 3 + j, chip, cc, sibling))
                passed[-1].start()
        for j, chip in enumerate(chips):
            for a in range(n):
                copy(a, 3 + j, chip, 1 - cc, me).wait_recv()
        for cp in first + passed:
            cp.wait_send()

    return pl.pallas_call(
        body, name="gather_chips", in_specs=[ANY] * n, out_specs=[ANY] * n,
        out_shape=[jax.ShapeDtypeStruct((N_CHIPS, 2) + b.shape, b.dtype) for b in blocks],
        scratch_shapes=[pltpu.SemaphoreType.DMA((n, 6)), pltpu.SemaphoreType.DMA((n, 6))])(*blocks)


def _pair_exchange(arrays):
    n = len(arrays)

    def body(*refs):
        ins, outs, (send_sems, recv_sems) = refs[:n], refs[n:2 * n], refs[2 * n:]
        x, y, cc = _place()
        cps = [pltpu.make_async_remote_copy(src_ref=ins[a], dst_ref=outs[a], send_sem=send_sems.at[a], recv_sem=recv_sems.at[a],
                                            device_id=(x, y, 1 - cc), device_id_type=MESH) for a in range(n)]
        for cp in cps:
            cp.start()
        for cp in cps:
            cp.wait()

    return pl.pallas_call(
        body, name="pair_exchange", in_specs=[ANY] * n, out_specs=[ANY] * n,
        out_shape=[jax.ShapeDtypeStruct(a.shape, a.dtype) for a in arrays],
        scratch_shapes=[pltpu.SemaphoreType.DMA((n,)), pltpu.SemaphoreType.DMA((n,))])(*arrays)


def _allreduce8(arrays):
    n = len(arrays)

    def body(*refs):
        ins, outs, recvs = refs[:n], refs[n:2 * n], refs[2 * n:3 * n]
        send_sems, recv_sems = refs[3 * n:]
        x, y, cc = _place()
        for a in range(n):
            outs[a][...] = ins[a][...]
        for k, peer in enumerate([(x, y, 1 - cc), (1 - x, y, cc), (x, 1 - y, cc)]):
            cps = [pltpu.make_async_remote_copy(src_ref=outs[a], dst_ref=recvs[a].at[k], send_sem=send_sems.at[a, k],
                                                recv_sem=recv_sems.at[a, k], device_id=peer, device_id_type=MESH) for a in range(n)]
            for cp in cps:
                cp.start()
            for cp in cps:
                cp.wait()
            for a in range(n):
                outs[a][...] = outs[a][...] + recvs[a][k]

    return pl.pallas_call(
        body, name="allreduce8", in_specs=[VMEM] * n, out_specs=[VMEM] * n,
        out_shape=[jax.ShapeDtypeStruct(a.shape, F32) for a in arrays],
        scratch_shapes=[pltpu.VMEM((3,) + a.shape, F32) for a in arrays] + [pltpu.SemaphoreType.DMA((n, 3)), pltpu.SemaphoreType.DMA((n, 3))],
        compiler_params=_params())(*arrays)


def _row_tile(rows, cols, limit_bytes=1 << 20):
    t = rows
    while t % 16 == 0 and t * cols * 4 > limit_bytes:
        t //= 2
    return t


def _add_pair(a, b):
    r, c = a.shape
    t = _row_tile(r, c)

    def body(a_ref, b_ref, o_ref):
        o_ref[...] = (a_ref[...].astype(F32) + b_ref[...].astype(F32)).astype(o_ref.dtype)

    spec = pl.BlockSpec((t, c), lambda i: (i, 0))
    return pl.pallas_call(body, name="add_pair", grid=(r // t,), in_specs=[spec, spec], out_specs=spec,
                          out_shape=jax.ShapeDtypeStruct((r, c), a.dtype), compiler_params=_params("arbitrary"))(a, b)


def _sum_chips(own, recv):
    r, c = own.shape
    t = _row_tile(r, c)

    def body(o_ref, r_ref, out_ref):
        acc = o_ref[...].astype(F32)
        for k in range(3):
            acc = acc + r_ref[k].astype(F32)
        out_ref[...] = acc

    return pl.pallas_call(
        body, name="sum_chips", grid=(r // t,),
        in_specs=[pl.BlockSpec((t, c), lambda i: (i, 0)), pl.BlockSpec((3, t, c), lambda i: (0, i, 0))],
        out_specs=pl.BlockSpec((t, c), lambda i: (i, 0)), out_shape=jax.ShapeDtypeStruct((r, c), F32),
        compiler_params=_params("arbitrary"))(own, recv)


def _adamw_update(w_ref, m_ref, v_ref, g_ref, d_ref, mo_ref, vo_ref):
    gv = g_ref[...]
    mn = ADAM_B1 * m_ref[...] + (1.0 - ADAM_B1) * gv
    vn = ADAM_B2 * v_ref[...] + (1.0 - ADAM_B2) * (gv * gv)
    m_hat = mn / (1.0 - ADAM_B1 ** ADAM_STEP)
    v_hat = vn / (1.0 - ADAM_B2 ** ADAM_STEP)
    d_ref[...] = -ADAM_LR * (m_hat / (jnp.sqrt(v_hat) + ADAM_EPS) + ADAM_WD * w_ref[...])
    mo_ref[...] = mn
    vo_ref[...] = vn


def _adamw(w, m, v, g):
    r, c = w.shape
    t = _row_tile(r, c)

    def body(*refs):
        _adamw_update(*refs)

    spec = pl.BlockSpec((t, c), lambda i: (i, 0))
    shape = jax.ShapeDtypeStruct((r, c), F32)
    return pl.pallas_call(body, name="adamw", grid=(r // t,), in_specs=[spec] * 4, out_specs=[spec] * 3,
                          out_shape=[shape] * 3, compiler_params=_params("arbitrary"))(w, m, v, g)


def _adamw_small(ws, ms, vs, gs):
    n = len(ws)

    def body(*refs):
        w_refs, m_refs, v_refs, g_refs = refs[:n], refs[n:2 * n], refs[2 * n:3 * n], refs[3 * n:4 * n]
        d_refs, mo_refs, vo_refs = refs[4 * n:5 * n], refs[5 * n:6 * n], refs[6 * n:]
        for a in range(n):
            _adamw_update(w_refs[a], m_refs[a], v_refs[a], g_refs[a], d_refs[a], mo_refs[a], vo_refs[a])

    shapes = [jax.ShapeDtypeStruct(w.shape, F32) for w in ws]
    outs = pl.pallas_call(body, name="adamw_small", in_specs=[VMEM] * (4 * n), out_specs=[VMEM] * (3 * n),
                          out_shape=shapes * 3, compiler_params=_params())(*ws, *ms, *vs, *gs)
    return outs[:n], outs[n:2 * n], outs[2 * n:]


SMALL =("norm_g", "f_bias", "sgu_w", "sgu_b", "sgu_ln_g", "sgu_ln_b", "short_conv_w", "conf_dw_w", "conf_dw_b",
         "conf_ln_g", "conf_ln_b", "final_g")
WEIGHTS = ("norm_g", "w_in", "f_bias", "sgu_w", "sgu_b", "sgu_ln_g", "sgu_ln_b", "short_conv_w", "conf_dw_w",
           "conf_dw_b", "conf_ln_g", "conf_ln_b", "w_branch", "w_out", "final_g")


def _aligned_pieces():
    out, pos = [], 0
    for src, width, dst in sorted(_SEGMENTS, key=lambda t: t[2]):
        if dst > pos:
            out.append((None, 0, dst - pos))
        lo = src
        while lo < src + width:
            j = lo // SHARD_COLS
            hi = min(src + width, (j + 1) * SHARD_COLS)
            out.append((j, lo - j * SHARD_COLS, hi - lo))
            lo = hi
        pos = dst + width
    return out


def _shard_pieces(j):
    lo_s, hi_s = j * SHARD_COLS, (j + 1) * SHARD_COLS
    out = []
    for src, width, dst in _SEGMENTS:
        lo, hi = max(src, lo_s), min(src + width, hi_s)
        if lo < hi:
            out.append((dst + lo - src, hi - lo))
    return out


def _w_in_aligned(shards):
    rows, dtype = shards[0].shape[0], shards[0].dtype
    return jnp.concatenate([jnp.zeros((rows, w), dtype) if j is None else shards[j][:, c0:c0 + w]
                            for j, c0, w in _aligned_pieces()], axis=1)


def _w_in_shard(g, j):
    parts = [g[:, c0:c0 + w] for c0, w in _shard_pieces(j)]
    return jnp.concatenate(parts + [jnp.zeros((g.shape[0], SHARD_PAD - SHARD_COLS), g.dtype)], axis=1)


def kernel(x, norm_g, w_in, f_bias, sgu_w, sgu_b, sgu_ln_g, sgu_ln_b, short_conv_w, conf_dw_w, conf_dw_b, conf_ln_g, conf_ln_b, w_branch, w_out, final_g, loss_target, m_norm_g, m_w_in, m_f_bias, m_sgu_w, m_sgu_b, m_sgu_ln_g, m_sgu_ln_b, m_short_conv_w, m_conf_dw_w, m_conf_dw_b, m_conf_ln_g, m_conf_ln_b, m_w_branch, m_w_out, m_final_g, v_norm_g, v_w_in, v_f_bias, v_sgu_w, v_sgu_b, v_sgu_ln_g, v_sgu_ln_b, v_short_conv_w, v_conf_dw_w, v_conf_dw_b, v_conf_ln_g, v_conf_ln_b, v_w_branch, v_w_out, v_final_g):
    px, py, pc = _place()
    chip = 2 * px + py
    depth = w_in.shape[0]
    w = dict(norm_g=norm_g, w_in=w_in, f_bias=f_bias, sgu_w=sgu_w, sgu_b=sgu_b, sgu_ln_g=sgu_ln_g, sgu_ln_b=sgu_ln_b,
             short_conv_w=short_conv_w, conf_dw_w=conf_dw_w, conf_dw_b=conf_dw_b, conf_ln_g=conf_ln_g, conf_ln_b=conf_ln_b,
             w_branch=w_branch, w_out=w_out, final_g=final_g)
    m = dict(norm_g=m_norm_g, w_in=m_w_in, f_bias=m_f_bias, sgu_w=m_sgu_w, sgu_b=m_sgu_b, sgu_ln_g=m_sgu_ln_g,
             sgu_ln_b=m_sgu_ln_b, short_conv_w=m_short_conv_w, conf_dw_w=m_conf_dw_w, conf_dw_b=m_conf_dw_b,
             conf_ln_g=m_conf_ln_g, conf_ln_b=m_conf_ln_b, w_branch=m_w_branch, w_out=m_w_out, final_g=m_final_g)
    v = dict(norm_g=v_norm_g, w_in=v_w_in, f_bias=v_f_bias, sgu_w=v_sgu_w, sgu_b=v_sgu_b, sgu_ln_g=v_sgu_ln_g,
             sgu_ln_b=v_sgu_ln_b, short_conv_w=v_short_conv_w, conf_dw_w=v_conf_dw_w, conf_dw_b=v_conf_dw_b,
             conf_ln_g=v_conf_ln_g, conf_ln_b=v_conf_ln_b, w_branch=v_w_branch, w_out=v_w_out, final_g=v_final_g)

    local = (jnp.pad(w_in, ((0, 0), (0, 0), (0, SHARD_PAD - SHARD_COLS))).astype(BF16),
             w_branch.astype(BF16).reshape(depth, N_HEADS * BR, BR), w_out.astype(BF16))
    pick = lambda a, i: lax.dynamic_index_in_dim(a, i, 0, keepdims=False)
    flat = lambda a: a.reshape(-1, a.shape[-1])

    def my_half(l):
        return _row_halves([a[l] for a in local], pc)[0]

    def all_chips(gathered, l):
        return [lax.dynamic_update_index_in_dim(got.reshape((N_CHIPS,) + a.shape[1:]), a[l], chip, 0) for got, a in zip(gathered, local)]

    conv_ch = BR // N_CHIPS
    place = lambda a: lax.dynamic_update_slice_in_dim(jnp.zeros(a.shape[:-1] + (BR,), F32), a * (pc == 0).astype(F32),
                                                      conv_ch * chip, axis=2).reshape(-1, BR)
    short_full, conf_full = _allreduce8([place(short_conv_w), place(conf_dw_w)])
    short_full, conf_full = short_full.reshape(depth, K_SHORT, BR), conf_full.reshape(depth, K_CONF, BR)

    def layer_params(l, shards):
        wi_all, wb_all, wo_all = shards
        return dict(
            norm_g=norm_g[l][None], w_in=_w_in_aligned([wi_all[j] for j in range(N_CHIPS)]),
            f_bias=jnp.pad(f_bias[l], (0, CHUNK - N_HEADS))[None],
            sgu_w=sgu_w[l], sgu_b=sgu_b[l], sgu_ln_g=sgu_ln_g[l][None], sgu_ln_b=sgu_ln_b[l][None],
            short_conv_w=short_full[l], conf_dw_w=conf_full[l], conf_dw_b=conf_dw_b[l][None],
            conf_ln_g=conf_ln_g[l][None], conf_ln_b=conf_ln_b[l][None],
            w_branch=jnp.concatenate([wb_all[j].reshape(N_HEADS, BR, BR) for j in range(N_CHIPS)], axis=2),
            w_out=jnp.concatenate([wo_all[j] for j in range(N_CHIPS)], axis=0))

    shards = all_chips(_gather_chips(my_half(0)), 0)
    params, saved, xs = [], [], x[0]
    for l in range(depth):
        params.append(layer_params(l, shards))
        nxt = my_half(l + 1) if l + 1 < depth else None
        xs, sv, gathered = _layer_fwd(xs, params[l], next_blocks=nxt)
        saved.append(sv)
        if nxt is not None:
            shards = all_chips(gathered, l + 1)
    loss8, dx, dfg8 = _loss_head(xs, loss_target[0], final_g[None])
    d_final_g = jnp.sum(dfg8, axis=0)
    loss = lax.psum(loss8[0, 0], ("x", "y", "c"))

    grads, reduced, pending = [None] * depth, [None] * depth, None
    for l in reversed(range(depth)):
        dx, grads[l], red, red_self = _layer_bwd(
            dx, params[l], saved[l], prev_reduce=None if pending is None else (pc, chip, pending),
            reduce_self=(pc, chip) if l == 0 else None)
        if pending is not None:
            reduced[l + 1] = red
        if l == 0:
            reduced[0] = red_self
        else:
            pending = _grads_by_chip(grads[l]["w_in"], grads[l]["w_branch"], grads[l]["w_out"])
    grad_x = dx
    own = [a for red in reduced for a in red]
    rows = [jnp.where(pc == 0, jnp.concatenate([a, b], axis=0), jnp.concatenate([b, a], axis=0))
            for a, b in zip(own, _pair_exchange(own))]
    gi, gb, go = [jnp.stack([rows[3 * l + k] for l in range(depth)]) for k in range(3)]
    g = dict(w_in=gi[:, :, :SHARD_COLS], w_branch=gb.reshape(depth, N_HEADS, BR, BR), w_out=go)

    two_d = lambda a: a.reshape(-1, a.shape[-1])
    small_local = [jnp.stack([grads[l][n] for l in range(depth)]) for n in SMALL[:-1]] + [d_final_g]
    for n, a, b in zip(SMALL, small_local, _allreduce8([two_d(a) for a in small_local])):
        g[n] = b.reshape(a.shape)
    for n in ("short_conv_w", "conf_dw_w"):
        g[n] = lax.dynamic_slice_in_dim(g[n], conv_ch * chip, conv_ch, axis=2)

    delta, new_m, new_v = {}, {}, {}
    for n in ("w_in", "w_branch", "w_out"):
        shp = w[n].shape
        d_, m_, v_ = _adamw(two_d(w[n]), two_d(m[n]), two_d(v[n]), two_d(g[n]))
        delta[n], new_m[n], new_v[n] = d_.reshape(shp), m_.reshape(shp), v_.reshape(shp)
    d_, m_, v_ = _adamw_small(*[[two_d(t[n]) for n in SMALL] for t in (w, m, v, g)])
    for n, a, b, c_ in zip(SMALL, d_, m_, v_):
        delta[n], new_m[n], new_v[n] = a.reshape(w[n].shape), b.reshape(w[n].shape), c_.reshape(w[n].shape)

    return (loss, grad_x[None], *[g[n] for n in WEIGHTS], *[delta[n] for n in WEIGHTS],
            *[new_m[n] for n in WEIGHTS], *[new_v[n] for n in WEIGHTS])
```

```python
import functools
import math

import jax
import jax.numpy as jnp
from jax import lax
from jax.experimental import pallas as pl
from jax.experimental.pallas import tpu as pltpu

F32 = jnp.float32
BF16 = jnp.bfloat16

D_MODEL = 1024
BR = 256
N_HEADS = 4
HEAD_DIM = 64
CHUNK = 128
K_SHORT = 3
K_CONF = 31
EPS = 1e-6
IN_COLS = 7684
SHARD_COLS = IN_COLS // 4
SHARD_PAD = 2048
N_CHIPS = 4

QKV0, CIN0, AUV0, DGLU0 = 0, 768, 1536, 2048
AG0, BG0, CG0, DG0 = 2560, 2816, 3072, 3328
BF0 = 3584
MRG0 = 4096
PCOLS = 8192

V7X_VMEM_BYTES = 64 * 1024 * 1024
VMEM_LIMIT = V7X_VMEM_BYTES * 7 // 8

ADAM_LR, ADAM_B1, ADAM_B2, ADAM_EPS, ADAM_WD, ADAM_STEP = 0.001, 0.9, 0.999, 1e-08, 0.01, 10

MESH = pl.DeviceIdType.MESH
ANY = pl.BlockSpec(memory_space=pl.ANY)
VMEM = pl.BlockSpec(memory_space=pltpu.VMEM)

GELU_C0 = math.sqrt(2.0 / math.pi)
GELU_C1 = 0.044715


def _params(*sem):
    return pltpu.CompilerParams(dimension_semantics=sem, vmem_limit_bytes=VMEM_LIMIT)


def _sigmoid(x):
    return 0.5 * jnp.tanh(0.5 * x) + 0.5


def _silu_and_grad(x):
    s = _sigmoid(x)
    return x * s, s * (1.0 + x * (1.0 - s))


def _gelu_and_grad(z):
    z2 = z * z
    t = jnp.tanh(GELU_C0 * (z + GELU_C1 * z2 * z))
    half = 0.5 * (1.0 + t)
    return z * half, half + 0.5 * z * (1.0 - t * t) * (GELU_C0 * (1.0 + 3.0 * GELU_C1 * z2))


def _ln_fwd(v):
    mu = jnp.mean(v, axis=-1, keepdims=True)
    xc = v - mu
    rs = lax.rsqrt(jnp.mean(xc * xc, axis=-1, keepdims=True) + EPS)
    return xc * rs, rs


def _ln_bwd(d_xh, xh, rs):
    return rs * (d_xh - jnp.mean(d_xh, axis=-1, keepdims=True) - xh * jnp.mean(d_xh * xh, axis=-1, keepdims=True))


def _part8(a):
    return a.reshape(a.shape[0] // 8, 8, a.shape[1]).sum(axis=0)


def _dot(a, b):
    return jnp.dot(a, b, preferred_element_type=F32)


def _dot_nt(a, b):
    return lax.dot_general(a, b, (((1,), (1,)), ((), ())), preferred_element_type=F32)


def _dot_tn(a, b):
    return lax.dot_general(a, b, (((0,), (0,)), ((), ())), preferred_element_type=F32)


def _head_masks(dtype):
    lane = lax.broadcasted_iota(jnp.int32, (1, BR), 1) // HEAD_DIM
    return [(lane == h).astype(dtype) for h in range(N_HEADS)]


def _window(ref, col0, width):
    return ref.at[:, pl.ds(col0, width)]


def _copy_all(pairs, sems):
    cps = [pltpu.make_async_copy(s, d, sems.at[i]) for i, (s, d) in enumerate(pairs)]
    for cp in cps:
        cp.start()
    for cp in cps:
        cp.wait()


def _place():
    return lax.axis_index("x"), lax.axis_index("y"), lax.axis_index("c")


def _other_chips(x, y):
    return [(1 - x, y), (x, 1 - y), (1 - x, 1 - y)]


class _Rider:
    def __init__(self, srcs, out_shapes, per_array, make, through=False):
        self.srcs, self.out_shapes, self.per_array, self.make, self.through = list(srcs), list(out_shapes), per_array, make, through
        self.n = len(self.srcs)

    def scratch(self):
        return [pltpu.SemaphoreType.DMA((self.n, self.per_array)), pltpu.SemaphoreType.DMA((self.n, self.per_array))]


def _ride(rider, first, last, src_refs, dst_refs, sems):
    if rider is None:
        return (lambda: None), (lambda: None)
    cps = rider.make(src_refs, dst_refs, *sems)

    def guarded(cond, fn):
        if cond is True:
            fn()
        else:
            pl.when(cond)(fn)

    def start():
        guarded(first, lambda: [cp.start() for cp in cps] and None)

    def finish():
        guarded(last, lambda: [cp.wait() for cp in cps] and None)

    return start, finish


def _rider_parts(rider, n_in, n_out):
    if rider is None:
        return [], [], [], [], [], {}
    aliases = {n_in + a: n_out + a for a in range(rider.n)} if rider.through else {}
    return rider.srcs, [ANY] * rider.n, [ANY] * rider.n, rider.out_shapes, rider.scratch(), aliases


def _remote(src, dst, send_sems, recv_sems, a, k, to):
    return pltpu.make_async_remote_copy(src_ref=src, dst_ref=dst, send_sem=send_sems.at[a, k], recv_sem=recv_sems.at[a, k],
                                        device_id=to, device_id_type=MESH)


def _gather_send_rider(blocks):
    def make(srcs, dsts, ss, rs):
        x, y, cc = _place()
        return [_remote(srcs[a], dsts[a].at[2 * x + y, cc], ss, rs, a, j, (*chip, cc))
                for j, chip in enumerate(_other_chips(x, y)) for a in range(len(srcs))]
    shapes = [jax.ShapeDtypeStruct((N_CHIPS, 2) + b.shape, b.dtype) for b in blocks]
    return _Rider(blocks, shapes, 3, make)


def _gather_forward_rider(landed):
    def make(srcs, dsts, ss, rs):
        x, y, cc = _place()
        return [_remote(dsts[a].at[2 * px + py, cc], dsts[a].at[2 * px + py, cc], ss, rs, a, j, (x, y, 1 - cc))
                for j, (px, py) in enumerate(_other_chips(x, y)) for a in range(len(dsts))]
    shapes = [jax.ShapeDtypeStruct(b.shape, b.dtype) for b in landed]
    return _Rider(landed, shapes, 3, make, through=True)


def _pair_rider(arrays):
    def make(srcs, dsts, ss, rs):
        x, y, cc = _place()
        return [_remote(srcs[a], dsts[a], ss, rs, a, 0, (x, y, 1 - cc)) for a in range(len(srcs))]
    return _Rider(arrays, [jax.ShapeDtypeStruct(b.shape, b.dtype) for b in arrays], 1, make)


def _chip_rider(arrays):
    def make(srcs, dsts, ss, rs):
        x, y, cc = _place()
        return [_remote(srcs[a].at[2 * px + py], dsts[a].at[k], ss, rs, a, k, (px, py, cc))
                for k, (px, py) in enumerate(_other_chips(x, y)) for a in range(len(srcs))]
    return _Rider(arrays, [jax.ShapeDtypeStruct((3,) + b.shape[1:], b.dtype) for b in arrays], 3, make)


def _inproj_fwd(x, g, w):
    s = x.shape[0]
    tm, tn = min(1024, s), 1024

    def body(x_ref, g_ref, w_ref, proj_ref, h_ref):
        @pl.when(pl.program_id(1) == 0)
        def _():
            xv = x_ref[...]
            r = lax.rsqrt(jnp.mean(xv * xv, axis=-1, keepdims=True) + EPS)
            h_ref[...] = ((xv * r) * g_ref[...]).astype(BF16)
        proj_ref[...] = _dot(h_ref[...], w_ref[...])

    return pl.pallas_call(
        body, name="inproj_fwd", grid=(s // tm, PCOLS // tn),
        in_specs=[pl.BlockSpec((tm, D_MODEL), lambda i, j: (i, 0)), pl.BlockSpec((1, D_MODEL), lambda i, j: (0, 0)),
                  pl.BlockSpec((D_MODEL, tn), lambda i, j: (0, j))],
        out_specs=[pl.BlockSpec((tm, tn), lambda i, j: (i, j)), pl.BlockSpec((tm, D_MODEL), lambda i, j: (i, 0))],
        out_shape=[jax.ShapeDtypeStruct((s, PCOLS), F32), jax.ShapeDtypeStruct((s, D_MODEL), BF16)],
        compiler_params=_params("arbitrary", "arbitrary"))(x, g, w)


def _rms_bwd(dh, x, g):
    r = lax.rsqrt(jnp.mean(x * x, axis=-1, keepdims=True) + EPS)
    xn = x * r
    gy = dh * g
    dx = r * (gy - xn * jnp.mean(xn * gy, axis=-1, keepdims=True))
    return dx, _part8(dh * xn)


def _inproj_bwd_x(dproj, w, x, dxn, g, rider=None):
    s = x.shape[0]
    tm, tk = min(512, s), 4096
    nk = PCOLS // tk
    ni = s // tm
    r_in, r_in_specs, r_out_specs, r_shapes, r_scratch, r_alias = _rider_parts(rider, 5, 2)
    nr = len(r_in)

    def body(*refs):
        dp_ref, w_ref, x_ref, dxn_ref, g_ref = refs[:5]
        dx_ref, dg_ref = refs[5 + nr:7 + nr]
        acc_ref = refs[7 + 2 * nr]
        i, k = pl.program_id(0), pl.program_id(1)
        start, finish = _ride(rider, (i == 0) & (k == 0), (i == ni - 1) & (k == nk - 1),
                              refs[5:5 + nr], refs[7 + nr:7 + 2 * nr], refs[8 + 2 * nr:])
        start()

        @pl.when(k == 0)
        def _():
            acc_ref[...] = jnp.zeros_like(acc_ref)

        @pl.when((i == 0) & (k == 0))
        def _():
            dg_ref[...] = jnp.zeros_like(dg_ref)

        acc_ref[...] += _dot_nt(dp_ref[...], w_ref[...])

        @pl.when(k == nk - 1)
        def _():
            dx, dg8 = _rms_bwd(acc_ref[...], x_ref[...], g_ref[...])
            dx_ref[...] = dxn_ref[...] + dx
            dg_ref[...] += dg8

        finish()

    outs = pl.pallas_call(
        body, name="inproj_bwd_x", grid=(ni, nk),
        in_specs=[pl.BlockSpec((tm, tk), lambda i, k: (i, k)), pl.BlockSpec((D_MODEL, tk), lambda i, k: (0, k)),
                  pl.BlockSpec((tm, D_MODEL), lambda i, k: (i, 0)), pl.BlockSpec((tm, D_MODEL), lambda i, k: (i, 0)),
                  pl.BlockSpec((1, D_MODEL), lambda i, k: (0, 0))] + r_in_specs,
        out_specs=[pl.BlockSpec((tm, D_MODEL), lambda i, k: (i, 0)), pl.BlockSpec((8, D_MODEL), lambda i, k: (0, 0))] + r_out_specs,
        out_shape=[jax.ShapeDtypeStruct((s, D_MODEL), F32), jax.ShapeDtypeStruct((8, D_MODEL), F32)] + r_shapes,
        scratch_shapes=[pltpu.VMEM((tm, D_MODEL), F32)] + r_scratch, input_output_aliases=r_alias,
        compiler_params=_params("arbitrary", "arbitrary"))(dproj, w, x, dxn, g, *r_in)
    return outs[0], outs[1], outs[2:]


def _inproj_bwd_w(h, dproj):
    s = h.shape[0]
    tn, tk = 1024, min(1024, s)
    nk = s // tk

    def body(h_ref, dp_ref, dw_ref, acc_ref):
        k = pl.program_id(1)

        @pl.when(k == 0)
        def _():
            acc_ref[...] = jnp.zeros_like(acc_ref)

        acc_ref[...] += _dot_tn(h_ref[...], dp_ref[...])

        @pl.when(k == nk - 1)
        def _():
            dw_ref[...] = acc_ref[...].astype(BF16)

    return pl.pallas_call(
        body, name="inproj_bwd_w", grid=(PCOLS // tn, nk),
        in_specs=[pl.BlockSpec((tk, D_MODEL), lambda j, k: (k, 0)), pl.BlockSpec((tk, tn), lambda j, k: (k, j))],
        out_specs=pl.BlockSpec((D_MODEL, tn), lambda j, k: (0, j)),
        out_shape=jax.ShapeDtypeStruct((D_MODEL, PCOLS), BF16),
        scratch_shapes=[pltpu.VMEM((D_MODEL, tn), F32)],
        compiler_params=_params("arbitrary", "arbitrary"))(h, dproj)


def _mix_a_chunk(uvp, agp, wm_ref, bias, lg, lb):
    u, du = _gelu_and_grad(uvp[:, :BR])
    v, dv = _gelu_and_grad(uvp[:, BR:])
    xh, rs = _ln_fwd(v)
    vnb = (xh * lg + lb).astype(BF16)
    masks = _head_masks(BF16)
    mixed = bias
    for h in range(N_HEADS):
        mixed = mixed + _dot(wm_ref[h], vnb * masks[h])
    sg, dsg = _silu_and_grad(agp)
    return u, du, dv, xh, rs, vnb, masks, mixed, sg, dsg


def _store_masked_sgu(sw_ref, wm_ref):
    row = lax.broadcasted_iota(jnp.int32, (CHUNK, CHUNK), 0)
    col = lax.broadcasted_iota(jnp.int32, (CHUNK, CHUNK), 1)
    for h in range(N_HEADS):
        wm_ref[h] = jnp.where(row >= col, sw_ref[h], 0.0).astype(BF16)


def _mix_a_fwd(proj, sgu_w, bias, lg, lb):
    s = proj.shape[0]

    def body(proj_ref, sw_ref, bias_ref, lg_ref, lb_ref, y_ref, uv_buf, ag_buf, wm_ref, sems):
        _copy_all([(_window(proj_ref, AUV0, 2 * BR), uv_buf), (_window(proj_ref, AG0, BR), ag_buf)], sems)
        _store_masked_sgu(sw_ref, wm_ref)

        def chunk(i, c):
            rows = pl.ds(pl.multiple_of(i * CHUNK, CHUNK), CHUNK)
            u, _, _, _, _, _, _, mixed, sg, _ = _mix_a_chunk(uv_buf[rows, :], ag_buf[rows, :], wm_ref, bias_ref[...],
                                                            lg_ref[...], lb_ref[...])
            y_ref[rows, :] = (u * mixed * sg).astype(BF16)
            return c

        lax.fori_loop(0, s // CHUNK, chunk, 0)

    return pl.pallas_call(
        body, name="mix_a_fwd", in_specs=[ANY, VMEM, VMEM, VMEM, VMEM], out_specs=VMEM,
        out_shape=jax.ShapeDtypeStruct((s, BR), BF16),
        scratch_shapes=[pltpu.VMEM((s, 2 * BR), F32), pltpu.VMEM((s, BR), F32), pltpu.VMEM((N_HEADS, CHUNK, CHUNK), BF16),
                        pltpu.SemaphoreType.DMA((2,))],
        compiler_params=_params())(proj, sgu_w, bias, lg, lb)


def _mix_a_bwd(proj, dproj, dy, sgu_w, bias, lg, lb):
    s = proj.shape[0]

    def body(proj_ref, dproj_in, dy_ref, sw_ref, bias_ref, lg_ref, lb_ref,
             dproj_ref, dsw_ref, dbias_ref, dlg_ref, dlb_ref,
             uv_buf, ag_buf, duv_buf, dag_buf, wm_ref, acc_lg, acc_lb, sems):
        del dproj_in
        _copy_all([(_window(proj_ref, AUV0, 2 * BR), uv_buf), (_window(proj_ref, AG0, BR), ag_buf)], sems)
        _store_masked_sgu(sw_ref, wm_ref)
        dsw_ref[...] = jnp.zeros_like(dsw_ref)
        dbias_ref[...] = jnp.zeros_like(dbias_ref)
        acc_lg[...] = jnp.zeros_like(acc_lg)
        acc_lb[...] = jnp.zeros_like(acc_lb)

        def chunk(i, c):
            rows = pl.ds(pl.multiple_of(i * CHUNK, CHUNK), CHUNK)
            lg_v = lg_ref[...]
            u, du, dv, xh, rs, vnb, masks, mixed, sg, dsg = _mix_a_chunk(
                uv_buf[rows, :], ag_buf[rows, :], wm_ref, bias_ref[...], lg_v, lb_ref[...])
            dyv = dy_ref[rows, :]
            t1 = dyv * sg
            d_u = t1 * mixed
            d_mixed = t1 * u
            d_ag = dyv * u * mixed * dsg
            dbias_ref[...] += d_mixed
            dmb = d_mixed.astype(BF16)
            d_vn = jnp.zeros((CHUNK, BR), F32)
            for h in range(N_HEADS):
                dm_h = dmb * masks[h]
                dsw_ref[h] += _dot_nt(dm_h, vnb)
                d_vn = d_vn + _dot_tn(wm_ref[h], dm_h)
            acc_lg[...] += _part8(d_vn * xh)
            acc_lb[...] += _part8(d_vn)
            d_v = _ln_bwd(d_vn * lg_v, xh, rs)
            duv_buf[rows, :] = jnp.concatenate([d_u * du, d_v * dv], axis=1).astype(BF16)
            dag_buf[rows, :] = d_ag.astype(BF16)
            return c

        lax.fori_loop(0, s // CHUNK, chunk, 0)
        row = lax.broadcasted_iota(jnp.int32, (CHUNK, CHUNK), 0)
        col = lax.broadcasted_iota(jnp.int32, (CHUNK, CHUNK), 1)
        for h in range(N_HEADS):
            dsw_ref[h] = jnp.where(row >= col, dsw_ref[h], 0.0)
        dlg_ref[...] = jnp.sum(acc_lg[...], axis=0, keepdims=True)
        dlb_ref[...] = jnp.sum(acc_lb[...], axis=0, keepdims=True)
        _copy_all([(duv_buf, _window(dproj_ref, AUV0, 2 * BR)), (dag_buf, _window(dproj_ref, AG0, BR))], sems)

    return pl.pallas_call(
        body, name="mix_a_bwd", in_specs=[ANY, ANY, VMEM, VMEM, VMEM, VMEM, VMEM],
        out_specs=[ANY, VMEM, VMEM, VMEM, VMEM],
        out_shape=[jax.ShapeDtypeStruct(dproj.shape, dproj.dtype), jax.ShapeDtypeStruct((N_HEADS, CHUNK, CHUNK), F32),
                   jax.ShapeDtypeStruct((CHUNK, BR), F32), jax.ShapeDtypeStruct((1, BR), F32), jax.ShapeDtypeStruct((1, BR), F32)],
        scratch_shapes=[pltpu.VMEM((s, 2 * BR), F32), pltpu.VMEM((s, BR), F32), pltpu.VMEM((s, 2 * BR), BF16),
                        pltpu.VMEM((s, BR), BF16), pltpu.VMEM((N_HEADS, CHUNK, CHUNK), BF16),
                        pltpu.VMEM((8, BR), F32), pltpu.VMEM((8, BR), F32), pltpu.SemaphoreType.DMA((2,))],
        input_output_aliases={1: 0}, compiler_params=_params())(proj, dproj, dy, sgu_w, bias, lg, lb)


def _tri_ones(n, upper):
    row = lax.broadcasted_iota(jnp.int32, (n, n), 0)
    col = lax.broadcasted_iota(jnp.int32, (n, n), 1)
    return ((row <= col) if upper else (row >= col)).astype(BF16)


def _split3(c):
    hi = c.astype(BF16)
    r1 = c - hi.astype(F32)
    mid = r1.astype(BF16)
    lo = (r1 - mid.astype(F32)).astype(BF16)
    return [hi, mid, lo]


def _tri_sum(tri, a):
    parts = _split3(a)
    return _dot(tri, parts[0]) + _dot(tri, parts[1]) + _dot(tri, parts[2])


EXT = 2 * HEAD_DIM
LANE_CQ = HEAD_DIM
LANE_CK = HEAD_DIM + 3


def _to_head(h):
    r = lax.broadcasted_iota(jnp.int32, (BR, EXT), 0)
    c = lax.broadcasted_iota(jnp.int32, (BR, EXT), 1)
    return ((r == c + h * HEAD_DIM) & (c < HEAD_DIM)).astype(BF16)


def _from_head(h):
    r = lax.broadcasted_iota(jnp.int32, (EXT, BR), 0)
    c = lax.broadcasted_iota(jnp.int32, (EXT, BR), 1)
    return ((c == r + h * HEAD_DIM) & (r < HEAD_DIM)).astype(BF16)


def _attn_prep_fwd(proj, f_bias):
    s = proj.shape[0]
    scale = 1.0 / math.sqrt(HEAD_DIM)

    def body(proj_ref, fb_ref, qe_ref, ke_ref, ve_ref, ket_ref, vet_ref, qkv_buf, z_buf, sems):
        _copy_all([(_window(proj_ref, QKV0, 3 * BR), qkv_buf), (_window(proj_ref, BF0, CHUNK), z_buf)], sems)
        tri = _tri_ones(CHUNK, upper=False)
        lane = lax.broadcasted_iota(jnp.int32, (CHUNK, EXT), 1)
        ones_q = ((lane >= LANE_CK) & (lane < LANE_CK + 3)).astype(F32)
        ones_k = ((lane >= LANE_CQ) & (lane < LANE_CQ + 3)).astype(F32)

        def chunk(i, carry):
            rows = pl.ds(pl.multiple_of(i * CHUNK, CHUNK), CHUNK)
            cum = _tri_sum(tri, jax.nn.log_sigmoid(z_buf[rows, :] + fb_ref[...])) + carry
            qb = (qkv_buf[rows, 0:BR] * scale).astype(BF16)
            kb = qkv_buf[rows, BR:2 * BR].astype(BF16)
            vb = qkv_buf[rows, 2 * BR:].astype(BF16)
            parts = [p.astype(F32) for p in _split3(cum)]
            for h in range(N_HEADS):
                sel = _to_head(h)
                dec_q, dec_k = ones_q, ones_k
                for t, part in enumerate(parts):
                    pf = part[:, h:h + 1]
                    dec_q = dec_q + jnp.where(lane == LANE_CQ + t, pf, 0.0)
                    dec_k = dec_k - jnp.where(lane == LANE_CK + t, pf, 0.0)
                qe_ref[h, rows, :] = (_dot(qb, sel) + dec_q).astype(BF16)
                kh = _dot(kb, sel) + dec_k
                vh = _dot(vb, sel)
                ke_ref[h, rows, :] = kh.astype(BF16)
                ve_ref[h, rows, :] = vh.astype(BF16)
                ket_ref[h, :, rows] = kh.T.astype(BF16)
                vet_ref[h, :, rows] = vh.T.astype(BF16)
            return cum[CHUNK - 1:CHUNK, :]

        lax.fori_loop(0, s // CHUNK, chunk, jnp.zeros((1, CHUNK), F32))

    shape = jax.ShapeDtypeStruct((N_HEADS, s, EXT), BF16)
    shape_t = jax.ShapeDtypeStruct((N_HEADS, EXT, s), BF16)
    return pl.pallas_call(
        body, name="attn_prep_fwd", in_specs=[ANY, VMEM], out_specs=[VMEM] * 5, out_shape=[shape, shape, shape, shape_t, shape_t],
        scratch_shapes=[pltpu.VMEM((s, 3 * BR), F32), pltpu.VMEM((s, CHUNK), F32), pltpu.SemaphoreType.DMA((2,))],
        compiler_params=_params())(proj, f_bias)


def _attn_prep_bwd(proj, dproj, dqe, dke, dve, f_bias):
    s = proj.shape[0]
    pad = MRG0 - BF0
    scale = 1.0 / math.sqrt(HEAD_DIM)

    def body(proj_ref, dproj_in, dq_ref, dk_ref, dv_ref, fb_ref, dproj_ref, dfb_ref, z_buf, dqkv_buf, dz_buf, sems):
        del dproj_in
        _copy_all([(_window(proj_ref, BF0, CHUNK), z_buf)], sems)
        tri = _tri_ones(CHUNK, upper=True)
        lane = lax.broadcasted_iota(jnp.int32, (CHUNK, CHUNK), 1)
        dz_buf[...] = jnp.zeros_like(dz_buf)
        n = s // CHUNK

        def chunk(t, carry):
            suffix, acc = carry
            i = n - 1 - t
            rows = pl.ds(pl.multiple_of(i * CHUNK, CHUNK), CHUNK)
            dq = jnp.zeros((CHUNK, BR), F32)
            dk = jnp.zeros((CHUNK, BR), F32)
            dv = jnp.zeros((CHUNK, BR), F32)
            dcum = jnp.zeros((CHUNK, CHUNK), F32)
            for h in range(N_HEADS):
                back = _from_head(h)
                dqh = dq_ref[h, :, rows].T
                dkh = dk_ref[h, rows, :]
                dq = dq + _dot((dqh * scale).astype(BF16), back)
                dk = dk + _dot(dkh.astype(BF16), back)
                dv = dv + _dot(dv_ref[h, rows, :].astype(BF16), back)
                dcum = dcum + jnp.where(lane == h, dqh[:, LANE_CQ:LANE_CQ + 1] - dkh[:, LANE_CK:LANE_CK + 1], 0.0)
            dqkv_buf[rows, 0:BR] = dq.astype(BF16)
            dqkv_buf[rows, BR:2 * BR] = dk.astype(BF16)
            dqkv_buf[rows, 2 * BR:] = dv.astype(BF16)
            dlf = _tri_sum(tri, dcum) + suffix
            dz = dlf * _sigmoid(-(z_buf[rows, :] + fb_ref[...]))
            dz_buf[rows, pl.ds(0, CHUNK)] = dz.astype(BF16)
            return dlf[0:1, :], acc + _part8(dz)

        _, acc = lax.fori_loop(0, n, chunk, (jnp.zeros((1, CHUNK), F32), jnp.zeros((8, CHUNK), F32)))
        dfb_ref[...] = jnp.sum(acc, axis=0, keepdims=True)
        _copy_all([(dqkv_buf, _window(dproj_ref, QKV0, 3 * BR)), (dz_buf, _window(dproj_ref, BF0, pad))], sems)

    return pl.pallas_call(
        body, name="attn_prep_bwd", in_specs=[ANY, ANY, VMEM, VMEM, VMEM, VMEM], out_specs=[ANY, VMEM],
        out_shape=[jax.ShapeDtypeStruct(dproj.shape, dproj.dtype), jax.ShapeDtypeStruct((1, CHUNK), F32)],
        scratch_shapes=[pltpu.VMEM((s, CHUNK), F32), pltpu.VMEM((s, 3 * BR), BF16), pltpu.VMEM((s, pad), BF16),
                        pltpu.SemaphoreType.DMA((2,))],
        input_output_aliases={1: 0}, compiler_params=_params())(proj, dproj, dqe, dke, dve, f_bias)


ATT_TQ = 256
ATT_FWD_GROUP = 8
ATT_BWD_GROUP = 4
NEG_BIG = -1e30


def _causal_t(q0, k0, tk, tq):
    kpos = k0 + lax.broadcasted_iota(jnp.int32, (tk, tq), 0)
    qpos = q0 + lax.broadcasted_iota(jnp.int32, (tk, tq), 1)
    return kpos <= qpos


STAT_ROWS = 8


def _attn_fwd(qe, ke, vet, rider=None):
    nh, s, w = qe.shape
    tq = min(ATT_TQ, s)
    tk = min(ATT_FWD_GROUP * tq, s)
    per = tk // tq
    nq = s // tq
    r_in, r_in_specs, r_out_specs, r_shapes, r_scratch, r_alias = _rider_parts(rider, 3, 2)
    nr = len(r_in)

    def body(*refs):
        q_ref, k_ref, vt_ref = refs[:3]
        ot_ref, lse_ref = refs[3 + nr:5 + nr]
        h, i = pl.program_id(0), pl.program_id(1)
        start, finish = _ride(rider, (h == 0) & (i == 0), (h == nh - 1) & (i == nq - 1),
                              refs[3:3 + nr], refs[5 + nr:5 + 2 * nr], refs[5 + 2 * nr:])
        start()
        q = q_ref[0]

        def step(k0, carry, width, masked):
            m, l, acc = carry
            ks = pl.ds(pl.multiple_of(k0, tq), width)
            st = _dot_nt(k_ref[0, ks, :], q)
            if masked:
                st = jnp.where(_causal_t(i * tq, k0, width, tq), st, NEG_BIG)
            m_new = jnp.maximum(m, jnp.max(st, axis=0, keepdims=True))
            alpha = jnp.exp(m - m_new)
            pt = jnp.exp(st - m_new)
            l = alpha * l + jnp.sum(pt, axis=0, keepdims=True)
            acc = alpha * acc + _dot(vt_ref[0, :, ks], pt.astype(BF16))
            return m_new, l, acc

        full = i // per
        init = (jnp.full((1, tq), NEG_BIG, F32), jnp.zeros((1, tq), F32), jnp.zeros((w, tq), F32))
        carry = lax.fori_loop(0, full, lambda j, c: step(j * tk, c, tk, False), init)
        m, l, acc = lax.switch(i % per, [functools.partial(step, width=(r + 1) * tq, masked=True) for r in range(per)],
                               full * tk, carry)
        ot_ref[0] = acc / l
        lse_ref[0] = jnp.broadcast_to(m + jnp.log(l), (STAT_ROWS, tq))
        finish()

    outs = pl.pallas_call(
        body, name="attn_fwd", grid=(nh, nq),
        in_specs=[pl.BlockSpec((1, tq, w), lambda h, i: (h, i, 0)), pl.BlockSpec((1, s, w), lambda h, i: (h, 0, 0)),
                  pl.BlockSpec((1, w, s), lambda h, i: (h, 0, 0))] + r_in_specs,
        out_specs=[pl.BlockSpec((1, w, tq), lambda h, i: (h, 0, i)), pl.BlockSpec((1, STAT_ROWS, tq), lambda h, i: (h, 0, i))] + r_out_specs,
        out_shape=[jax.ShapeDtypeStruct((nh, w, s), F32), jax.ShapeDtypeStruct((nh, STAT_ROWS, s), F32)] + r_shapes,
        scratch_shapes=r_scratch, input_output_aliases=r_alias,
        compiler_params=_params("arbitrary", "arbitrary"))(qe, ke, vet, *r_in)
    return outs[0], outs[1], outs[2:]


def _attn_bwd(qe, ke, ket, ve, doe, lse, dd, rider=None):
    nh, s, w = qe.shape
    tq = min(ATT_TQ, s)
    tk = min(ATT_BWD_GROUP * tq, s)
    per = tk // tq
    nq = s // tq
    r_in, r_in_specs, r_out_specs, r_shapes, r_scratch, r_alias = _rider_parts(rider, 7, 3)
    nr = len(r_in)

    def body(*refs):
        q_ref, do_ref, lse_ref, dd_ref, k_ref, kt_ref, v_ref = refs[:7]
        dqt_ref, dk_ref, dv_ref = refs[7 + nr:10 + nr]
        h, i = pl.program_id(0), pl.program_id(1)
        start, finish = _ride(rider, (h == 0) & (i == 0), (h == nh - 1) & (i == nq - 1),
                              refs[7:7 + nr], refs[10 + nr:10 + 2 * nr], refs[10 + 2 * nr:])
        start()

        @pl.when(i == 0)
        def _():
            dk_ref[...] = jnp.zeros_like(dk_ref)
            dv_ref[...] = jnp.zeros_like(dv_ref)

        q = q_ref[0]
        do = do_ref[0]
        lse_row = lse_ref[0, 0:1, :]
        dd_row = dd_ref[0, 0:1, :]

        def step(k0, dqt, width, masked):
            ks = pl.ds(pl.multiple_of(k0, tq), width)
            st = _dot_nt(k_ref[0, ks, :], q)
            pt = jnp.exp(st - lse_row)
            if masked:
                pt = jnp.where(_causal_t(i * tq, k0, width, tq), pt, 0.0)
            dpt = _dot_nt(v_ref[0, ks, :], do)
            dst = (pt * (dpt - dd_row)).astype(BF16)
            dv_ref[0, ks, :] += _dot(pt.astype(BF16), do)
            dk_ref[0, ks, :] += _dot(dst, q)
            return dqt + _dot(kt_ref[0, :, ks], dst)

        full = i // per
        dqt = lax.fori_loop(0, full, lambda j, c: step(j * tk, c, tk, False), jnp.zeros((w, tq), F32))
        dqt_ref[0] = lax.switch(i % per, [functools.partial(step, width=(r + 1) * tq, masked=True) for r in range(per)],
                                full * tk, dqt)
        finish()

    qblk = pl.BlockSpec((1, tq, w), lambda h, i: (h, i, 0))
    stat = pl.BlockSpec((1, STAT_ROWS, tq), lambda h, i: (h, 0, i))
    whole = pl.BlockSpec((1, s, w), lambda h, i: (h, 0, 0))
    whole_t = pl.BlockSpec((1, w, s), lambda h, i: (h, 0, 0))
    outs = pl.pallas_call(
        body, name="attn_bwd", grid=(nh, nq),
        in_specs=[qblk, qblk, stat, stat, whole, whole_t, whole] + r_in_specs,
        out_specs=[pl.BlockSpec((1, w, tq), lambda h, i: (h, 0, i)), whole, whole] + r_out_specs,
        out_shape=[jax.ShapeDtypeStruct((nh, w, s), F32), jax.ShapeDtypeStruct((nh, s, w), F32),
                   jax.ShapeDtypeStruct((nh, s, w), F32)] + r_shapes,
        scratch_shapes=r_scratch, input_output_aliases=r_alias,
        compiler_params=_params("arbitrary", "arbitrary"))(qe, doe, lse, dd, ke, ket, ve, *r_in)
    return outs[0], outs[1], outs[2], outs[3:]


def _bgate_fwd(proj, ot, rider=None):
    s = proj.shape[0]
    r_in, r_in_specs, r_out_specs, r_shapes, r_scratch, r_alias = _rider_parts(rider, 2, 2)
    nr = len(r_in)

    def body(*refs):
        proj_ref, ot_ref = refs[:2]
        y_ref, o_ref = refs[2 + nr:4 + nr]
        g_buf, sems = refs[4 + 2 * nr:6 + 2 * nr]
        start, finish = _ride(rider, True, True, refs[2:2 + nr], refs[4 + nr:4 + 2 * nr], refs[6 + 2 * nr:])
        start()
        _copy_all([(_window(proj_ref, BG0, BR), g_buf)], sems)

        def chunk(i, c):
            rows = pl.ds(pl.multiple_of(i * CHUNK, CHUNK), CHUNK)
            o = jnp.zeros((CHUNK, BR), F32)
            for h in range(N_HEADS):
                back = _from_head(h)
                for part in _split3(ot_ref[h, :, rows].T):
                    o = o + _dot(part, back)
            sg, _ = _silu_and_grad(g_buf[rows, :])
            o_ref[rows, :] = o
            y_ref[rows, :] = (o * sg).astype(BF16)
            return c

        lax.fori_loop(0, s // CHUNK, chunk, 0)
        finish()

    outs = pl.pallas_call(
        body, name="bgate_fwd", in_specs=[ANY, VMEM] + r_in_specs, out_specs=[VMEM, VMEM] + r_out_specs,
        out_shape=[jax.ShapeDtypeStruct((s, BR), BF16), jax.ShapeDtypeStruct((s, BR), F32)] + r_shapes,
        scratch_shapes=[pltpu.VMEM((s, BR), F32), pltpu.SemaphoreType.DMA((1,))] + r_scratch,
        input_output_aliases=r_alias, compiler_params=_params())(proj, ot, *r_in)
    return outs[0], outs[1], outs[2:]


def _bgate_bwd(proj, dproj, o, dy):
    s = proj.shape[0]

    def body(proj_ref, dproj_in, o_ref, dy_ref, dproj_ref, do_ref, dd_ref, g_buf, dg_buf, sems):
        del dproj_in
        _copy_all([(_window(proj_ref, BG0, BR), g_buf)], sems)
        lane = lax.broadcasted_iota(jnp.int32, (BR, CHUNK), 0) // HEAD_DIM
        col = lax.broadcasted_iota(jnp.int32, (BR, CHUNK), 1)
        sel = (lane == col).astype(BF16)

        def chunk(i, c):
            rows = pl.ds(pl.multiple_of(i * CHUNK, CHUNK), CHUNK)
            sg, dsg = _silu_and_grad(g_buf[rows, :])
            dyv = dy_ref[rows, :]
            ov = o_ref[rows, :]
            do = dyv * sg
            dg_buf[rows, :] = (dyv * ov * dsg).astype(BF16)
            prod = _split3(do * ov)
            ddt = (_dot(prod[0], sel) + _dot(prod[1], sel) + _dot(prod[2], sel)).T
            dob = do.astype(BF16)
            for h in range(N_HEADS):
                do_ref[h, rows, :] = _dot(dob, _to_head(h)).astype(BF16)
                dd_ref[h, :, rows] = jnp.broadcast_to(ddt[h:h + 1, :], (STAT_ROWS, CHUNK))
            return c

        lax.fori_loop(0, s // CHUNK, chunk, 0)
        _copy_all([(dg_buf, _window(dproj_ref, BG0, BR))], sems)

    return pl.pallas_call(
        body, name="bgate_bwd", in_specs=[ANY, ANY, VMEM, VMEM], out_specs=[ANY, VMEM, VMEM],
        out_shape=[jax.ShapeDtypeStruct(dproj.shape, dproj.dtype), jax.ShapeDtypeStruct((N_HEADS, s, EXT), BF16),
                   jax.ShapeDtypeStruct((N_HEADS, STAT_ROWS, s), F32)],
        scratch_shapes=[pltpu.VMEM((s, BR), F32), pltpu.VMEM((s, BR), BF16), pltpu.SemaphoreType.DMA((1,))],
        input_output_aliases={1: 0}, compiler_params=_params())(proj, dproj, o, dy)


C_PAD = 8


def _mix_c_fwd(proj, w):
    s = proj.shape[0]

    def body(proj_ref, w_ref, y_ref, cin_buf, g_buf, z_buf, sems):
        _copy_all([(_window(proj_ref, CIN0, 3 * BR), cin_buf), (_window(proj_ref, CG0, BR), g_buf)], sems)
        z_buf[pl.ds(0, C_PAD), :] = jnp.zeros((C_PAD, BR), F32)

        def fill(i, c):
            rows = pl.ds(pl.multiple_of(i * CHUNK, CHUNK), CHUNK)
            z_buf[pl.ds(pl.multiple_of(i * CHUNK + C_PAD, 8), CHUNK), :] = cin_buf[rows, BR:2 * BR] * cin_buf[rows, 2 * BR:]
            return c

        lax.fori_loop(0, s // CHUNK, fill, 0)

        def chunk(i, c):
            r0 = pl.multiple_of(i * CHUNK, CHUNK)
            rows = pl.ds(r0, CHUNK)
            ze = z_buf[pl.ds(r0, CHUNK + C_PAD), :]
            conv = jnp.zeros((CHUNK, BR), F32)
            for k in range(K_SHORT):
                off = C_PAD - (K_SHORT - 1) + k
                conv = conv + w_ref[k:k + 1, :] * ze[off:off + CHUNK]
            sg, _ = _silu_and_grad(g_buf[rows, :])
            y_ref[rows, :] = (cin_buf[rows, 0:BR] * conv * sg).astype(BF16)
            return c

        lax.fori_loop(0, s // CHUNK, chunk, 0)

    return pl.pallas_call(
        body, name="mix_c_fwd", in_specs=[ANY, VMEM], out_specs=VMEM, out_shape=jax.ShapeDtypeStruct((s, BR), BF16),
        scratch_shapes=[pltpu.VMEM((s, 3 * BR), F32), pltpu.VMEM((s, BR), F32), pltpu.VMEM((s + C_PAD, BR), F32),
                        pltpu.SemaphoreType.DMA((2,))],
        compiler_params=_params())(proj, w)


def _mix_c_bwd(proj, dproj, dy, w):
    s = proj.shape[0]

    def body(proj_ref, dproj_in, dy_ref, w_ref, dproj_ref, dw_ref, cin_buf, g_buf, z_buf, dc_buf, dcin_buf, dg_buf, acc, sems):
        del dproj_in
        _copy_all([(_window(proj_ref, CIN0, 3 * BR), cin_buf), (_window(proj_ref, CG0, BR), g_buf)], sems)
        z_buf[pl.ds(0, C_PAD), :] = jnp.zeros((C_PAD, BR), F32)
        dc_buf[pl.ds(s, C_PAD), :] = jnp.zeros((C_PAD, BR), F32)
        acc[...] = jnp.zeros_like(acc)

        def fill(i, c):
            rows = pl.ds(pl.multiple_of(i * CHUNK, CHUNK), CHUNK)
            z_buf[pl.ds(pl.multiple_of(i * CHUNK + C_PAD, 8), CHUNK), :] = cin_buf[rows, BR:2 * BR] * cin_buf[rows, 2 * BR:]
            return c

        lax.fori_loop(0, s // CHUNK, fill, 0)

        def chunk(i, c):
            r0 = pl.multiple_of(i * CHUNK, CHUNK)
            rows = pl.ds(r0, CHUNK)
            ze = z_buf[pl.ds(r0, CHUNK + C_PAD), :]
            taps = [ze[C_PAD - (K_SHORT - 1) + k:C_PAD - (K_SHORT - 1) + k + CHUNK] for k in range(K_SHORT)]
            conv = jnp.zeros((CHUNK, BR), F32)
            for k in range(K_SHORT):
                conv = conv + w_ref[k:k + 1, :] * taps[k]
            sg, dsg = _silu_and_grad(g_buf[rows, :])
            bg = cin_buf[rows, 0:BR]
            dyv = dy_ref[rows, :]
            dconv = dyv * bg * sg
            dc_buf[rows, :] = dconv
            dcin_buf[rows, 0:BR] = (dyv * conv * sg).astype(BF16)
            dg_buf[rows, :] = (dyv * bg * conv * dsg).astype(BF16)
            for k in range(K_SHORT):
                acc[k] += _part8(dconv * taps[k])
            return c

        lax.fori_loop(0, s // CHUNK, chunk, 0)

        def chunk2(i, c):
            r0 = pl.multiple_of(i * CHUNK, CHUNK)
            rows = pl.ds(r0, CHUNK)
            de = dc_buf[pl.ds(r0, CHUNK + C_PAD), :]
            dz = jnp.zeros((CHUNK, BR), F32)
            for k in range(K_SHORT):
                off = K_SHORT - 1 - k
                dz = dz + w_ref[k:k + 1, :] * de[off:off + CHUNK]
            dcin_buf[rows, BR:2 * BR] = (dz * cin_buf[rows, 2 * BR:]).astype(BF16)
            dcin_buf[rows, 2 * BR:] = (dz * cin_buf[rows, BR:2 * BR]).astype(BF16)
            return c

        lax.fori_loop(0, s // CHUNK, chunk2, 0)
        dw_ref[...] = jnp.zeros_like(dw_ref)
        for k in range(K_SHORT):
            dw_ref[k:k + 1, :] = jnp.sum(acc[k], axis=0, keepdims=True)
        _copy_all([(dcin_buf, _window(dproj_ref, CIN0, 3 * BR)), (dg_buf, _window(dproj_ref, CG0, BR))], sems)

    return pl.pallas_call(
        body, name="mix_c_bwd", in_specs=[ANY, ANY, VMEM, VMEM], out_specs=[ANY, VMEM],
        out_shape=[jax.ShapeDtypeStruct(dproj.shape, dproj.dtype), jax.ShapeDtypeStruct((8, BR), F32)],
        scratch_shapes=[pltpu.VMEM((s, 3 * BR), F32), pltpu.VMEM((s, BR), F32), pltpu.VMEM((s + C_PAD, BR), F32),
                        pltpu.VMEM((s + C_PAD, BR), F32), pltpu.VMEM((s, 3 * BR), BF16), pltpu.VMEM((s, BR), BF16),
                        pltpu.VMEM((K_SHORT, 8, BR), F32), pltpu.SemaphoreType.DMA((2,))],
        input_output_aliases={1: 0}, compiler_params=_params())(proj, dproj, dy, w)


D_PAD = 32


def _tap_windows(win, offsets, n):
    rows = win.shape[0]
    rolled, out = {}, []
    for off in offsets:
        r = off % 8
        if r not in rolled:
            rolled[r] = win if r == 0 else pltpu.roll(win, rows - r, 0)
        out.append(rolled[r][off - r:off - r + n])
    return out


def _mix_d_common(g_ref, w_ref, b_ref, lg_ref, lb_ref, hh_buf, r0):
    he = hh_buf[pl.ds(r0, CHUNK + D_PAD), :]
    taps = _tap_windows(he, [D_PAD - (K_CONF - 1) + k for k in range(K_CONF)], CHUNK)
    conv = jnp.zeros((CHUNK, BR), F32) + b_ref[...]
    for k in range(K_CONF):
        conv = conv + w_ref[k:k + 1, :] * taps[k]
    xh, rs = _ln_fwd(conv)
    sw, dsw = _silu_and_grad(xh * lg_ref[...] + lb_ref[...])
    sg, dsg = _silu_and_grad(g_ref[pl.ds(r0, CHUNK), :])
    return taps, xh, rs, sw, dsw, sg, dsg


def _mix_d_fill(cin_buf, hh_buf, s):
    hh_buf[pl.ds(0, D_PAD), :] = jnp.zeros((D_PAD, BR), F32)

    def fill(i, c):
        rows = pl.ds(pl.multiple_of(i * CHUNK, CHUNK), CHUNK)
        hh_buf[pl.ds(pl.multiple_of(i * CHUNK + D_PAD, 8), CHUNK), :] = cin_buf[rows, 0:BR] * _sigmoid(cin_buf[rows, BR:])
        return c

    lax.fori_loop(0, s // CHUNK, fill, 0)


def _mix_d_fwd(proj, w, b, lg, lb):
    s = proj.shape[0]

    def body(proj_ref, w_ref, b_ref, lg_ref, lb_ref, y_ref, cin_buf, g_buf, hh_buf, sems):
        _copy_all([(_window(proj_ref, DGLU0, 2 * BR), cin_buf), (_window(proj_ref, DG0, BR), g_buf)], sems)
        _mix_d_fill(cin_buf, hh_buf, s)

        def chunk(i, c):
            r0 = pl.multiple_of(i * CHUNK, CHUNK)
            _, _, _, sw, _, sg, _ = _mix_d_common(g_buf, w_ref, b_ref, lg_ref, lb_ref, hh_buf, r0)
            y_ref[pl.ds(r0, CHUNK), :] = (sw * sg).astype(BF16)
            return c

        lax.fori_loop(0, s // CHUNK, chunk, 0)

    return pl.pallas_call(
        body, name="mix_d_fwd", in_specs=[ANY, VMEM, VMEM, VMEM, VMEM], out_specs=VMEM,
        out_shape=jax.ShapeDtypeStruct((s, BR), BF16),
        scratch_shapes=[pltpu.VMEM((s, 2 * BR), F32), pltpu.VMEM((s, BR), F32), pltpu.VMEM((s + D_PAD, BR), F32),
                        pltpu.SemaphoreType.DMA((2,))],
        compiler_params=_params())(proj, w, b, lg, lb)


def _mix_d_bwd(proj, dproj, dy, w, b, lg, lb):
    s = proj.shape[0]

    def body(proj_ref, dproj_in, dy_ref, w_ref, b_ref, lg_ref, lb_ref, dproj_ref, dw_ref, db_ref, dlg_ref, dlb_ref,
             cin_buf, g_buf, hh_buf, dc_buf, dcin_buf, dg_buf, acc_w, acc_s, sems):
        del dproj_in
        _copy_all([(_window(proj_ref, DGLU0, 2 * BR), cin_buf), (_window(proj_ref, DG0, BR), g_buf)], sems)
        _mix_d_fill(cin_buf, hh_buf, s)
        dc_buf[pl.ds(s, D_PAD), :] = jnp.zeros((D_PAD, BR), F32)
        acc_w[...] = jnp.zeros_like(acc_w)
        acc_s[...] = jnp.zeros_like(acc_s)

        def chunk(i, c):
            r0 = pl.multiple_of(i * CHUNK, CHUNK)
            rows = pl.ds(r0, CHUNK)
            taps, xh, rs, sw, dsw, sg, dsg = _mix_d_common(g_buf, w_ref, b_ref, lg_ref, lb_ref, hh_buf, r0)
            dyv = dy_ref[rows, :]
            dg_buf[rows, :] = (dyv * sw * dsg).astype(BF16)
            d_ln = dyv * sg * dsw
            acc_s[0] += _part8(d_ln * xh)
            acc_s[1] += _part8(d_ln)
            dc = _ln_bwd(d_ln * lg_ref[...], xh, rs)
            acc_s[2] += _part8(dc)
            dc_buf[rows, :] = dc
            for k in range(K_CONF):
                acc_w[k] += _part8(dc * taps[k])
            return c

        lax.fori_loop(0, s // CHUNK, chunk, 0)

        def chunk2(i, c):
            r0 = pl.multiple_of(i * CHUNK, CHUNK)
            rows = pl.ds(r0, CHUNK)
            de = dc_buf[pl.ds(r0, CHUNK + D_PAD), :]
            dh = jnp.zeros((CHUNK, BR), F32)
            for k, win in enumerate(_tap_windows(de, [K_CONF - 1 - k for k in range(K_CONF)], CHUNK)):
                dh = dh + w_ref[k:k + 1, :] * win
            a = cin_buf[rows, 0:BR]
            sig = _sigmoid(cin_buf[rows, BR:])
            dcin_buf[rows, 0:BR] = (dh * sig).astype(BF16)
            dcin_buf[rows, BR:] = (dh * a * sig * (1.0 - sig)).astype(BF16)
            return c

        lax.fori_loop(0, s // CHUNK, chunk2, 0)
        dw_ref[...] = jnp.zeros_like(dw_ref)
        for k in range(K_CONF):
            dw_ref[k:k + 1, :] = jnp.sum(acc_w[k], axis=0, keepdims=True)
        dlg_ref[...] = jnp.sum(acc_s[0], axis=0, keepdims=True)
        dlb_ref[...] = jnp.sum(acc_s[1], axis=0, keepdims=True)
        db_ref[...] = jnp.sum(acc_s[2], axis=0, keepdims=True)
        _copy_all([(dcin_buf, _window(dproj_ref, DGLU0, 2 * BR)), (dg_buf, _window(dproj_ref, DG0, BR))], sems)

    vec = jax.ShapeDtypeStruct((1, BR), F32)
    return pl.pallas_call(
        body, name="mix_d_bwd", in_specs=[ANY, ANY, VMEM, VMEM, VMEM, VMEM, VMEM], out_specs=[ANY, VMEM, VMEM, VMEM, VMEM],
        out_shape=[jax.ShapeDtypeStruct(dproj.shape, dproj.dtype), jax.ShapeDtypeStruct((32, BR), F32), vec, vec, vec],
        scratch_shapes=[pltpu.VMEM((s, 2 * BR), F32), pltpu.VMEM((s, BR), F32), pltpu.VMEM((s + D_PAD, BR), F32),
                        pltpu.VMEM((s + D_PAD, BR), F32), pltpu.VMEM((s, 2 * BR), BF16), pltpu.VMEM((s, BR), BF16),
                        pltpu.VMEM((K_CONF, 8, BR), F32), pltpu.VMEM((3, 8, BR), F32), pltpu.SemaphoreType.DMA((2,))],
        input_output_aliases={1: 0}, compiler_params=_params())(proj, dproj, dy, w, b, lg, lb)


MERGE_TM = 256


def _merge_fwd(x, proj, ys, wb, wo):
    s = x.shape[0]
    tm = min(MERGE_TM, s)

    def body(x_ref, lg_ref, ya, yb, yc, yd, wb_ref, wo_ref, xn_ref, mg_ref):
        merged = jnp.zeros((tm, D_MODEL), F32)
        for n, y_ref in enumerate((ya, yb, yc, yd)):
            gate = _sigmoid(lg_ref[:, n * D_MODEL:(n + 1) * D_MODEL])
            merged = merged + gate * _dot(y_ref[...], wb_ref[n])
        mb = merged.astype(BF16)
        mg_ref[...] = mb
        xn_ref[...] = x_ref[...] + _dot(mb, wo_ref[...])

    row = lambda w: pl.BlockSpec((tm, w), lambda i: (i, 0))
    return pl.pallas_call(
        body, name="merge_fwd", grid=(s // tm,),
        in_specs=[row(D_MODEL), pl.BlockSpec((tm, MRG0), lambda i: (i, 1)), row(BR), row(BR), row(BR), row(BR),
                  pl.BlockSpec((N_HEADS, BR, D_MODEL), lambda i: (0, 0, 0)), pl.BlockSpec((D_MODEL, D_MODEL), lambda i: (0, 0))],
        out_specs=[row(D_MODEL), row(D_MODEL)],
        out_shape=[jax.ShapeDtypeStruct((s, D_MODEL), F32), jax.ShapeDtypeStruct((s, D_MODEL), BF16)],
        compiler_params=_params("arbitrary"))(x, proj, *ys, wb, wo)


def _merge_bwd(dxn, proj, ys, merged, wb, wo, rider=None):
    s = dxn.shape[0]
    tm = min(MERGE_TM, s)
    steps = s // tm
    r_in, r_in_specs, r_out_specs, r_shapes, r_scratch, r_alias = _rider_parts(rider, 9, 7)
    nr = len(r_in)

    def body(*refs):
        dx_ref, lg_ref, ya, yb, yc, yd, mg_ref, wb_ref, wo_ref = refs[:9]
        dlg_ref, da, db, dc, dd, dwo_ref, dwb_ref = refs[9 + nr:16 + nr]
        start, finish = _ride(rider, pl.program_id(0) == 0, pl.program_id(0) == steps - 1,
                              refs[9:9 + nr], refs[16 + nr:16 + 2 * nr], refs[16 + 2 * nr:])
        start()

        @pl.when(pl.program_id(0) == 0)
        def _():
            dwo_ref[...] = jnp.zeros_like(dwo_ref)
            dwb_ref[...] = jnp.zeros_like(dwb_ref)

        dxb = dx_ref[...].astype(BF16)
        d_merged = _dot_nt(dxb, wo_ref[...])
        dwo_ref[...] += _dot_tn(mg_ref[...], dxb)
        for n, (y_ref, dy_ref) in enumerate(((ya, da), (yb, db), (yc, dc), (yd, dd))):
            yv = y_ref[...]
            gate = _sigmoid(lg_ref[:, n * D_MODEL:(n + 1) * D_MODEL])
            pn = _dot(yv, wb_ref[n])
            dlg_ref[:, n * D_MODEL:(n + 1) * D_MODEL] = (d_merged * pn * gate * (1.0 - gate)).astype(BF16)
            dpn = (d_merged * gate).astype(BF16)
            dy_ref[...] = _dot_nt(dpn, wb_ref[n])
            dwb_ref[n] += _dot_tn(yv, dpn)
        finish()

    row = lambda w: pl.BlockSpec((tm, w), lambda i: (i, 0))
    wb_spec = pl.BlockSpec((N_HEADS, BR, D_MODEL), lambda i: (0, 0, 0))
    wo_spec = pl.BlockSpec((D_MODEL, D_MODEL), lambda i: (0, 0))
    dy_shape = jax.ShapeDtypeStruct((s, BR), F32)
    outs = pl.pallas_call(
        body, name="merge_bwd", grid=(steps,),
        in_specs=[row(D_MODEL), pl.BlockSpec((tm, MRG0), lambda i: (i, 1)), row(BR), row(BR), row(BR), row(BR), row(D_MODEL),
                  wb_spec, wo_spec] + r_in_specs,
        out_specs=[pl.BlockSpec((tm, MRG0), lambda i: (i, 1)), row(BR), row(BR), row(BR), row(BR), wo_spec, wb_spec] + r_out_specs,
        out_shape=[jax.ShapeDtypeStruct((s, PCOLS), BF16), dy_shape, dy_shape, dy_shape, dy_shape,
                   jax.ShapeDtypeStruct((D_MODEL, D_MODEL), F32), jax.ShapeDtypeStruct((N_HEADS, BR, D_MODEL), F32)] + r_shapes,
        scratch_shapes=r_scratch, input_output_aliases=r_alias,
        compiler_params=_params("arbitrary"))(dxn, proj, *ys, merged, wb, wo, *r_in)
    return outs[0], outs[1:5], outs[5], outs[6], outs[7:]


def _loss_head(x, target, g):
    s = x.shape[0]
    tm = min(512, s)

    def body(x_ref, t_ref, g_ref, loss_ref, dx_ref, dg_ref):
        @pl.when(pl.program_id(0) == 0)
        def _():
            loss_ref[...] = jnp.zeros_like(loss_ref)
            dg_ref[...] = jnp.zeros_like(dg_ref)

        xv = x_ref[...]
        gv = g_ref[...]
        r = lax.rsqrt(jnp.mean(xv * xv, axis=-1, keepdims=True) + EPS)
        xn = xv * r
        err = xn * gv - t_ref[...]
        loss_ref[...] += 0.5 * jnp.sum(jnp.mean(err * err, axis=-1, keepdims=True))
        dy = err * (1.0 / D_MODEL)
        dg_ref[...] += _part8(dy * xn)
        gy = dy * gv
        dx_ref[...] = r * (gy - xn * jnp.mean(xn * gy, axis=-1, keepdims=True))

    row = pl.BlockSpec((tm, D_MODEL), lambda i: (i, 0))
    return pl.pallas_call(
        body, name="loss_head", grid=(s // tm,),
        in_specs=[row, row, pl.BlockSpec((1, D_MODEL), lambda i: (0, 0))],
        out_specs=[pl.BlockSpec((8, 128), lambda i: (0, 0)), row, pl.BlockSpec((8, D_MODEL), lambda i: (0, 0))],
        out_shape=[jax.ShapeDtypeStruct((8, 128), F32), jax.ShapeDtypeStruct((s, D_MODEL), F32), jax.ShapeDtypeStruct((8, D_MODEL), F32)],
        compiler_params=_params("arbitrary"))(x, target, g)


_SEGMENTS = ((0, 512, AUV0), (512, 256, AG0), (768, 768, QKV0), (1536, 4, BF0), (1540, 256, BG0), (1796, 768, CIN0),
             (2564, 256, CG0), (2820, 512, DGLU0), (3332, 256, DG0), (3588, 4096, MRG0))


def _sgu_bias_rows(sgu_b):
    return jnp.repeat(sgu_b.T, HEAD_DIM, axis=1)


def _layer_fwd(x, p, next_blocks=None):
    proj, h = _inproj_fwd(x, p["norm_g"], p["w_in"])
    ya = _mix_a_fwd(proj, p["sgu_w"], _sgu_bias_rows(p["sgu_b"]), p["sgu_ln_g"], p["sgu_ln_b"])
    qe, ke, ve, ket, vet = _attn_prep_fwd(proj, p["f_bias"])
    ot, lse, landed = _attn_fwd(qe, ke, vet, rider=None if next_blocks is None else _gather_send_rider(next_blocks))
    yb, o, gathered = _bgate_fwd(proj, ot, rider=None if next_blocks is None else _gather_forward_rider(landed))
    yc = _mix_c_fwd(proj, p["short_conv_w"])
    yd = _mix_d_fwd(proj, p["conf_dw_w"], p["conf_dw_b"], p["conf_ln_g"], p["conf_ln_b"])
    ys = (ya, yb, yc, yd)
    x_next, merged = _merge_fwd(x, proj, ys, p["w_branch"], p["w_out"])
    saved = dict(x=x, proj=proj, h=h, ys=ys, merged=merged, qe=qe, ke=ke, ve=ve, ket=ket, lse=lse, o=o)
    return x_next, saved, gathered


def _grads_by_chip(d_win, d_wb, d_wo):
    return [jnp.stack([_w_in_shard(d_win, j) for j in range(N_CHIPS)]),
            jnp.stack([d_wb[:, :, j * BR:(j + 1) * BR].reshape(N_HEADS * BR, BR).astype(BF16) for j in range(N_CHIPS)]),
            jnp.stack([d_wo[j * BR:(j + 1) * BR].astype(BF16) for j in range(N_CHIPS)])]


def _layer_bwd(dxn, p, sv, prev_reduce=None, reduce_self=None):
    proj = sv["proj"]
    if prev_reduce is None:
        dproj, dys, d_wo, d_wb, _ = _merge_bwd(dxn, proj, sv["ys"], sv["merged"], p["w_branch"], p["w_out"])
        chip_rider = None
    else:
        pc, chip, arrays = prev_reduce
        mine, other = _row_halves(arrays, pc)
        dproj, dys, d_wo, d_wb, from_sibling = _merge_bwd(dxn, proj, sv["ys"], sv["merged"], p["w_branch"], p["w_out"],
                                                           rider=_pair_rider(other))
        halves = _add_pairs(mine, from_sibling)
        chip_rider = _chip_rider(halves)
    dproj, d_sgu_w, d_bias_rows, d_sgu_lg, d_sgu_lb = _mix_a_bwd(
        proj, dproj, dys[0], p["sgu_w"], _sgu_bias_rows(p["sgu_b"]), p["sgu_ln_g"], p["sgu_ln_b"])
    dproj, doe, dd = _bgate_bwd(proj, dproj, sv["o"], dys[1])
    dqe, dke, dve, from_chips = _attn_bwd(sv["qe"], sv["ke"], sv["ket"], sv["ve"], doe, sv["lse"], dd, rider=chip_rider)
    reduced = None
    if prev_reduce is not None:
        reduced = _sum_chips(halves, chip, from_chips)
    dproj, d_fb = _attn_prep_bwd(proj, dproj, dqe, dke, dve, p["f_bias"])
    dproj, d_sc = _mix_c_bwd(proj, dproj, dys[2], p["short_conv_w"])
    dproj, d_cw, d_cb, d_clg, d_clb = _mix_d_bwd(proj, dproj, dys[3], p["conf_dw_w"], p["conf_dw_b"], p["conf_ln_g"], p["conf_ln_b"])
    d_win = _inproj_bwd_w(sv["h"], dproj)
    reduced_self, self_rider = None, None
    if reduce_self is not None:
        pc, chip = reduce_self
        mine, other = _row_halves(_grads_by_chip(d_win, d_wb, d_wo), pc)
        own_halves = _add_pairs(mine, _pair_exchange(other))
        self_rider = _chip_rider(own_halves)
    dx, dg8, from_chips_self = _inproj_bwd_x(dproj, p["w_in"], sv["x"], dxn, p["norm_g"], rider=self_rider)
    if reduce_self is not None:
        reduced_self = _sum_chips(own_halves, chip, from_chips_self)
    grads = dict(
        norm_g=jnp.sum(dg8, axis=0), w_in=d_win, f_bias=d_fb[0, :N_HEADS], sgu_w=d_sgu_w,
        sgu_b=d_bias_rows.reshape(CHUNK, N_HEADS, HEAD_DIM).sum(axis=-1).T,
        sgu_ln_g=d_sgu_lg[0], sgu_ln_b=d_sgu_lb[0], short_conv_w=d_sc[:K_SHORT], conf_dw_w=d_cw[:K_CONF],
        conf_dw_b=d_cb[0], conf_ln_g=d_clg[0], conf_ln_b=d_clb[0], w_branch=d_wb, w_out=d_wo)
    return dx, grads, reduced, reduced_self


def _row_halves(arrays, pc):
    half = lambda a, i: lax.dynamic_slice_in_dim(a, i * (a.shape[-2] // 2), a.shape[-2] // 2, axis=a.ndim - 2)
    return [half(a, pc) for a in arrays], [half(a, 1 - pc) for a in arrays]


def _local_step(x, target, layers, final_g):
    saved = []
    for p in layers:
        x, sv, _ = _layer_fwd(x, p)
        saved.append(sv)
    loss8, dx, dfg8 = _loss_head(x, target, final_g)
    grads = [None] * len(layers)
    for l in reversed(range(len(layers))):
        dx, grads[l], _, _ = _layer_bwd(dx, layers[l], saved[l])
    return loss8[0, 0], dx, grads, jnp.sum(dfg8, axis=0)


def _gather_chips(blocks):
    n = len(blocks)

    def body(*refs):
        ins, outs, (send_sems, recv_sems) = refs[:n], refs[n:2 * n], refs[2 * n:]
        x, y, cc = _place()
        me, sibling = (x, y, cc), (x, y, 1 - cc)
        chips = _other_chips(x, y)

        def copy(a, k, chip, layer, to, src=None):
            dst = outs[a].at[2 * chip[0] + chip[1], layer]
            return pltpu.make_async_remote_copy(src_ref=dst if src is None else src, dst_ref=dst, send_sem=send_sems.at[a, k],
                                                recv_sem=recv_sems.at[a, k], device_id=to, device_id_type=MESH)

        first = [copy(a, j, (x, y), cc, (*chip, cc), src=ins[a]) for j, chip in enumerate(chips) for a in range(n)]
        for cp in first:
            cp.start()
        passed = []
        for j, chip in enumerate(chips):
            for a in range(n):
                copy(a, j, chip, cc, me).wait_recv()
                passed.append(copy(a, 3 + j, chip, cc, sibling))
                passed[-1].start()
        for j, chip in enumerate(chips):
            for a in range(n):
                copy(a, 3 + j, chip, 1 - cc, me).wait_recv()
        for cp in first + passed:
            cp.wait_send()

    return pl.pallas_call(
        body, name="gather_chips", in_specs=[ANY] * n, out_specs=[ANY] * n,
        out_shape=[jax.ShapeDtypeStruct((N_CHIPS, 2) + b.shape, b.dtype) for b in blocks],
        scratch_shapes=[pltpu.SemaphoreType.DMA((n, 6)), pltpu.SemaphoreType.DMA((n, 6))])(*blocks)


def _pair_exchange(arrays):
    n = len(arrays)

    def body(*refs):
        ins, outs, (send_sems, recv_sems) = refs[:n], refs[n:2 * n], refs[2 * n:]
        x, y, cc = _place()
        cps = [pltpu.make_async_remote_copy(src_ref=ins[a], dst_ref=outs[a], send_sem=send_sems.at[a], recv_sem=recv_sems.at[a],
                                            device_id=(x, y, 1 - cc), device_id_type=MESH) for a in range(n)]
        for cp in cps:
            cp.start()
        for cp in cps:
            cp.wait()

    return pl.pallas_call(
        body, name="pair_exchange", in_specs=[ANY] * n, out_specs=[ANY] * n,
        out_shape=[jax.ShapeDtypeStruct(a.shape, a.dtype) for a in arrays],
        scratch_shapes=[pltpu.SemaphoreType.DMA((n,)), pltpu.SemaphoreType.DMA((n,))])(*arrays)


def _allreduce8(arrays):
    n = len(arrays)

    def body(*refs):
        ins, outs, recvs = refs[:n], refs[n:2 * n], refs[2 * n:3 * n]
        send_sems, recv_sems = refs[3 * n:]
        x, y, cc = _place()
        for a in range(n):
            outs[a][...] = ins[a][...]
        for k, peer in enumerate([(x, y, 1 - cc), (1 - x, y, cc), (x, 1 - y, cc)]):
            cps = [pltpu.make_async_remote_copy(src_ref=outs[a], dst_ref=recvs[a].at[k], send_sem=send_sems.at[a, k],
                                                recv_sem=recv_sems.at[a, k], device_id=peer, device_id_type=MESH) for a in range(n)]
            for cp in cps:
                cp.start()
            for cp in cps:
                cp.wait()
            for a in range(n):
                outs[a][...] = outs[a][...] + recvs[a][k]

    return pl.pallas_call(
        body, name="allreduce8", in_specs=[VMEM] * n, out_specs=[VMEM] * n,
        out_shape=[jax.ShapeDtypeStruct(a.shape, F32) for a in arrays],
        scratch_shapes=[pltpu.VMEM((3,) + a.shape, F32) for a in arrays] + [pltpu.SemaphoreType.DMA((n, 3)), pltpu.SemaphoreType.DMA((n, 3))],
        compiler_params=_params())(*arrays)


def _row_tile(rows, cols, limit_bytes=1 << 20):
    t = rows
    while t % 16 == 0 and t * cols * 4 > limit_bytes:
        t //= 2
    return t


REDUCE_STEPS = 8


def _add_pairs(xs, ys):
    n = len(xs)

    def body(*refs):
        for a in range(n):
            refs[2 * n + a][...] = (refs[a][...].astype(F32) + refs[n + a][...].astype(F32)).astype(BF16)

    specs = [pl.BlockSpec((x.shape[0], x.shape[1] // REDUCE_STEPS, x.shape[2]), lambda i: (0, i, 0)) for x in xs]
    return pl.pallas_call(body, name="add_pairs", grid=(REDUCE_STEPS,), in_specs=specs + specs, out_specs=specs,
                          out_shape=[jax.ShapeDtypeStruct(x.shape, BF16) for x in xs], compiler_params=_params("arbitrary"))(*xs, *ys)


def _sum_chips(halves, chip, recvs):
    n = len(halves)

    def body(chip_ref, *refs):
        del chip_ref
        for a in range(n):
            acc = refs[a][0].astype(F32)
            for k in range(3):
                acc = acc + refs[n + a][k].astype(F32)
            refs[2 * n + a][...] = acc

    rows = [h.shape[1] // REDUCE_STEPS for h in halves]
    own_specs = [pl.BlockSpec((1, r, h.shape[2]), lambda i, c: (c[0], i, 0)) for h, r in zip(halves, rows)]
    recv_specs = [pl.BlockSpec((3, r, h.shape[2]), lambda i, c: (0, i, 0)) for h, r in zip(halves, rows)]
    out_specs = [pl.BlockSpec((r, h.shape[2]), lambda i, c: (i, 0)) for h, r in zip(halves, rows)]
    grid_spec = pltpu.PrefetchScalarGridSpec(num_scalar_prefetch=1, grid=(REDUCE_STEPS,), in_specs=own_specs + recv_specs,
                                             out_specs=out_specs)
    return pl.pallas_call(body, name="sum_chips", grid_spec=grid_spec,
                          out_shape=[jax.ShapeDtypeStruct(h.shape[1:], F32) for h in halves],
                          compiler_params=_params("arbitrary"))(jnp.reshape(chip, (1,)).astype(jnp.int32), *halves, *recvs)


def _adamw_update(w_ref, m_ref, v_ref, g_ref, d_ref, mo_ref, vo_ref):
    gv = g_ref[...]
    mn = ADAM_B1 * m_ref[...] + (1.0 - ADAM_B1) * gv
    vn = ADAM_B2 * v_ref[...] + (1.0 - ADAM_B2) * (gv * gv)
    m_hat = mn / (1.0 - ADAM_B1 ** ADAM_STEP)
    v_hat = vn / (1.0 - ADAM_B2 ** ADAM_STEP)
    d_ref[...] = -ADAM_LR * (m_hat / (jnp.sqrt(v_hat) + ADAM_EPS) + ADAM_WD * w_ref[...])
    mo_ref[...] = mn
    vo_ref[...] = vn


def _adamw(w, m, v, g):
    r, c = w.shape
    t = _row_tile(r, c)

    def body(*refs):
        _adamw_update(*refs)

    spec = pl.BlockSpec((t, c), lambda i: (i, 0))
    shape = jax.ShapeDtypeStruct((r, c), F32)
    return pl.pallas_call(body, name="adamw", grid=(r // t,), in_specs=[spec] * 4, out_specs=[spec] * 3,
                          out_shape=[shape] * 3, compiler_params=_params("arbitrary"))(w, m, v, g)


def _adamw_small(ws, ms, vs, gs):
    n = len(ws)

    def body(*refs):
        w_refs, m_refs, v_refs, g_refs = refs[:n], refs[n:2 * n], refs[2 * n:3 * n], refs[3 * n:4 * n]
        d_refs, mo_refs, vo_refs = refs[4 * n:5 * n], refs[5 * n:6 * n], refs[6 * n:]
        for a in range(n):
            _adamw_update(w_refs[a], m_refs[a], v_refs[a], g_refs[a], d_refs[a], mo_refs[a], vo_refs[a])

    shapes = [jax.ShapeDtypeStruct(w.shape, F32) for w in ws]
    outs = pl.pallas_call(body, name="adamw_small", in_specs=[VMEM] * (4 * n), out_specs=[VMEM] * (3 * n),
                          out_shape=shapes * 3, compiler_params=_params())(*ws, *ms, *vs, *gs)
    return outs[:n], outs[n:2 * n], outs[2 * n:]


SMALL =("norm_g", "f_bias", "sgu_w", "sgu_b", "sgu_ln_g", "sgu_ln_b", "short_conv_w", "conf_dw_w", "conf_dw_b",
         "conf_ln_g", "conf_ln_b", "final_g")
WEIGHTS = ("norm_g", "w_in", "f_bias", "sgu_w", "sgu_b", "sgu_ln_g", "sgu_ln_b", "short_conv_w", "conf_dw_w",
           "conf_dw_b", "conf_ln_g", "conf_ln_b", "w_branch", "w_out", "final_g")


def _aligned_pieces():
    out, pos = [], 0
    for src, width, dst in sorted(_SEGMENTS, key=lambda t: t[2]):
        if dst > pos:
            out.append((None, 0, dst - pos))
        lo = src
        while lo < src + width:
            j = lo // SHARD_COLS
            hi = min(src + width, (j + 1) * SHARD_COLS)
            out.append((j, lo - j * SHARD_COLS, hi - lo))
            lo = hi
        pos = dst + width
    return out


def _shard_pieces(j):
    lo_s, hi_s = j * SHARD_COLS, (j + 1) * SHARD_COLS
    out = []
    for src, width, dst in _SEGMENTS:
        lo, hi = max(src, lo_s), min(src + width, hi_s)
        if lo < hi:
            out.append((dst + lo - src, hi - lo))
    return out


def _w_in_aligned(shards):
    rows, dtype = shards[0].shape[0], shards[0].dtype
    return jnp.concatenate([jnp.zeros((rows, w), dtype) if j is None else shards[j][:, c0:c0 + w]
                            for j, c0, w in _aligned_pieces()], axis=1)


def _w_in_shard(g, j):
    parts = [g[:, c0:c0 + w] for c0, w in _shard_pieces(j)]
    return jnp.concatenate(parts + [jnp.zeros((g.shape[0], SHARD_PAD - SHARD_COLS), g.dtype)], axis=1)


def kernel(x, norm_g, w_in, f_bias, sgu_w, sgu_b, sgu_ln_g, sgu_ln_b, short_conv_w, conf_dw_w, conf_dw_b, conf_ln_g, conf_ln_b, w_branch, w_out, final_g, loss_target, m_norm_g, m_w_in, m_f_bias, m_sgu_w, m_sgu_b, m_sgu_ln_g, m_sgu_ln_b, m_short_conv_w, m_conf_dw_w, m_conf_dw_b, m_conf_ln_g, m_conf_ln_b, m_w_branch, m_w_out, m_final_g, v_norm_g, v_w_in, v_f_bias, v_sgu_w, v_sgu_b, v_sgu_ln_g, v_sgu_ln_b, v_short_conv_w, v_conf_dw_w, v_conf_dw_b, v_conf_ln_g, v_conf_ln_b, v_w_branch, v_w_out, v_final_g):
    px, py, pc = _place()
    chip = 2 * px + py
    depth = w_in.shape[0]
    w = dict(norm_g=norm_g, w_in=w_in, f_bias=f_bias, sgu_w=sgu_w, sgu_b=sgu_b, sgu_ln_g=sgu_ln_g, sgu_ln_b=sgu_ln_b,
             short_conv_w=short_conv_w, conf_dw_w=conf_dw_w, conf_dw_b=conf_dw_b, conf_ln_g=conf_ln_g, conf_ln_b=conf_ln_b,
             w_branch=w_branch, w_out=w_out, final_g=final_g)
    m = dict(norm_g=m_norm_g, w_in=m_w_in, f_bias=m_f_bias, sgu_w=m_sgu_w, sgu_b=m_sgu_b, sgu_ln_g=m_sgu_ln_g,
             sgu_ln_b=m_sgu_ln_b, short_conv_w=m_short_conv_w, conf_dw_w=m_conf_dw_w, conf_dw_b=m_conf_dw_b,
             conf_ln_g=m_conf_ln_g, conf_ln_b=m_conf_ln_b, w_branch=m_w_branch, w_out=m_w_out, final_g=m_final_g)
    v = dict(norm_g=v_norm_g, w_in=v_w_in, f_bias=v_f_bias, sgu_w=v_sgu_w, sgu_b=v_sgu_b, sgu_ln_g=v_sgu_ln_g,
             sgu_ln_b=v_sgu_ln_b, short_conv_w=v_short_conv_w, conf_dw_w=v_conf_dw_w, conf_dw_b=v_conf_dw_b,
             conf_ln_g=v_conf_ln_g, conf_ln_b=v_conf_ln_b, w_branch=v_w_branch, w_out=v_w_out, final_g=v_final_g)

    local = (jnp.pad(w_in, ((0, 0), (0, 0), (0, SHARD_PAD - SHARD_COLS))).astype(BF16),
             w_branch.astype(BF16).reshape(depth, N_HEADS * BR, BR), w_out.astype(BF16))
    pick = lambda a, i: lax.dynamic_index_in_dim(a, i, 0, keepdims=False)
    flat = lambda a: a.reshape(-1, a.shape[-1])

    def my_half(l):
        return _row_halves([a[l] for a in local], pc)[0]

    def all_chips(gathered, l):
        return [lax.dynamic_update_index_in_dim(got.reshape((N_CHIPS,) + a.shape[1:]), a[l], chip, 0) for got, a in zip(gathered, local)]

    conv_ch = BR // N_CHIPS
    place = lambda a: lax.dynamic_update_slice_in_dim(jnp.zeros(a.shape[:-1] + (BR,), F32), a * (pc == 0).astype(F32),
                                                      conv_ch * chip, axis=2).reshape(-1, BR)
    short_full, conf_full = _allreduce8([place(short_conv_w), place(conf_dw_w)])
    short_full, conf_full = short_full.reshape(depth, K_SHORT, BR), conf_full.reshape(depth, K_CONF, BR)

    def layer_params(l, shards):
        wi_all, wb_all, wo_all = shards
        return dict(
            norm_g=norm_g[l][None], w_in=_w_in_aligned([wi_all[j] for j in range(N_CHIPS)]),
            f_bias=jnp.pad(f_bias[l], (0, CHUNK - N_HEADS))[None],
            sgu_w=sgu_w[l], sgu_b=sgu_b[l], sgu_ln_g=sgu_ln_g[l][None], sgu_ln_b=sgu_ln_b[l][None],
            short_conv_w=short_full[l], conf_dw_w=conf_full[l], conf_dw_b=conf_dw_b[l][None],
            conf_ln_g=conf_ln_g[l][None], conf_ln_b=conf_ln_b[l][None],
            w_branch=jnp.concatenate([wb_all[j].reshape(N_HEADS, BR, BR) for j in range(N_CHIPS)], axis=2),
            w_out=jnp.concatenate([wo_all[j] for j in range(N_CHIPS)], axis=0))

    shards = all_chips(_gather_chips(my_half(0)), 0)
    params, saved, xs = [], [], x[0]
    for l in range(depth):
        params.append(layer_params(l, shards))
        nxt = my_half(l + 1) if l + 1 < depth else None
        xs, sv, gathered = _layer_fwd(xs, params[l], next_blocks=nxt)
        saved.append(sv)
        if nxt is not None:
            shards = all_chips(gathered, l + 1)
    loss8, dx, dfg8 = _loss_head(xs, loss_target[0], final_g[None])
    d_final_g = jnp.sum(dfg8, axis=0)
    loss = lax.psum(loss8[0, 0], ("x", "y", "c"))

    grads, reduced, pending = [None] * depth, [None] * depth, None
    for l in reversed(range(depth)):
        dx, grads[l], red, red_self = _layer_bwd(
            dx, params[l], saved[l], prev_reduce=None if pending is None else (pc, chip, pending),
            reduce_self=(pc, chip) if l == 0 else None)
        if pending is not None:
            reduced[l + 1] = red
        if l == 0:
            reduced[0] = red_self
        else:
            pending = _grads_by_chip(grads[l]["w_in"], grads[l]["w_branch"], grads[l]["w_out"])
    grad_x = dx
    own = [a for red in reduced for a in red]
    rows = [jnp.where(pc == 0, jnp.concatenate([a, b], axis=0), jnp.concatenate([b, a], axis=0))
            for a, b in zip(own, _pair_exchange(own))]
    gi, gb, go = [jnp.stack([rows[3 * l + k] for l in range(depth)]) for k in range(3)]
    g = dict(w_in=gi[:, :, :SHARD_COLS], w_branch=gb.reshape(depth, N_HEADS, BR, BR), w_out=go)

    two_d = lambda a: a.reshape(-1, a.shape[-1])
    at_least_2d = lambda a: a if a.ndim >= 2 else a[None]
    small_local = [jnp.stack([grads[l][n] for l in range(depth)]) for n in SMALL[:-1]] + [d_final_g]
    for n, a, b in zip(SMALL, small_local, _allreduce8([at_least_2d(a) for a in small_local])):
        g[n] = b.reshape(a.shape)
    for n in ("short_conv_w", "conf_dw_w"):
        g[n] = lax.dynamic_slice_in_dim(g[n], conv_ch * chip, conv_ch, axis=2)

    delta, new_m, new_v = {}, {}, {}
    for n in ("w_in", "w_branch", "w_out"):
        shp = w[n].shape
        d_, m_, v_ = _adamw(two_d(w[n]), two_d(m[n]), two_d(v[n]), two_d(g[n]))
        delta[n], new_m[n], new_v[n] = d_.reshape(shp), m_.reshape(shp), v_.reshape(shp)
    d_, m_, v_ = _adamw_small(*[[at_least_2d(t[n]) for n in SMALL] for t in (w, m, v, g)])
    for n, a, b, c_ in zip(SMALL, d_, m_, v_):
        delta[n], new_m[n], new_v[n] = a.reshape(w[n].shape), b.reshape(w[n].shape), c_.reshape(w[n].shape)

    return (loss, grad_x[None], *[g[n] for n in WEIGHTS], *[delta[n] for n in WEIGHTS],
            *[new_m[n] for n in WEIGHTS], *[new_v[n] for n in WEIGHTS])
```

```python
import functools
import math

import jax
import jax.numpy as jnp
from jax import lax
from jax.experimental import pallas as pl
from jax.experimental.pallas import tpu as pltpu

F32 = jnp.float32
BF16 = jnp.bfloat16

D_MODEL = 1024
BR = 256
N_HEADS = 4
HEAD_DIM = 64
CHUNK = 128
K_SHORT = 3
K_CONF = 31
EPS = 1e-6
IN_COLS = 7684
SHARD_COLS = IN_COLS // 4
SHARD_PAD = 2048
N_CHIPS = 4

MRG0, MRG_COLS = 0, 4096
QKV0, CIN0, AUV0, DGLU0 = 4096, 4864, 5632, 6144
AG0, BG0, CG0, DG0 = 6656, 6912, 7168, 7424
PCOLS = 7680
BF_SRC, BF_COLS = 1536, 4

V7X_VMEM_BYTES = 64 * 1024 * 1024
VMEM_LIMIT = V7X_VMEM_BYTES * 7 // 8

ADAM_LR, ADAM_B1, ADAM_B2, ADAM_EPS, ADAM_WD, ADAM_STEP = 0.001, 0.9, 0.999, 1e-08, 0.01, 10

MESH = pl.DeviceIdType.MESH
ANY = pl.BlockSpec(memory_space=pl.ANY)
VMEM = pl.BlockSpec(memory_space=pltpu.VMEM)

GELU_C0 = math.sqrt(2.0 / math.pi)
GELU_C1 = 0.044715


def _params(*sem):
    return pltpu.CompilerParams(dimension_semantics=sem, vmem_limit_bytes=VMEM_LIMIT)


def _sigmoid(x):
    return 0.5 * jnp.tanh(0.5 * x) + 0.5


def _silu_and_grad(x):
    s = _sigmoid(x)
    return x * s, s * (1.0 + x * (1.0 - s))


def _gelu_and_grad(z):
    z2 = z * z
    t = jnp.tanh(GELU_C0 * (z + GELU_C1 * z2 * z))
    half = 0.5 * (1.0 + t)
    return z * half, half + 0.5 * z * (1.0 - t * t) * (GELU_C0 * (1.0 + 3.0 * GELU_C1 * z2))


def _ln_fwd(v):
    mu = jnp.mean(v, axis=-1, keepdims=True)
    xc = v - mu
    rs = lax.rsqrt(jnp.mean(xc * xc, axis=-1, keepdims=True) + EPS)
    return xc * rs, rs


def _ln_bwd(d_xh, xh, rs):
    return rs * (d_xh - jnp.mean(d_xh, axis=-1, keepdims=True) - xh * jnp.mean(d_xh * xh, axis=-1, keepdims=True))


def _part8(a):
    return a.reshape(a.shape[0] // 8, 8, a.shape[1]).sum(axis=0)


def _dot(a, b):
    return jnp.dot(a, b, preferred_element_type=F32)


def _dot_nt(a, b):
    return lax.dot_general(a, b, (((1,), (1,)), ((), ())), preferred_element_type=F32)


def _dot_tn(a, b):
    return lax.dot_general(a, b, (((0,), (0,)), ((), ())), preferred_element_type=F32)


def _head_masks(dtype):
    lane = lax.broadcasted_iota(jnp.int32, (1, BR), 1) // HEAD_DIM
    return [(lane == h).astype(dtype) for h in range(N_HEADS)]


def _window(ref, col0, width):
    return ref.at[:, pl.ds(col0, width)]


def _copy_all(pairs, sems):
    cps = [pltpu.make_async_copy(s, d, sems.at[i]) for i, (s, d) in enumerate(pairs)]
    for cp in cps:
        cp.start()
    for cp in cps:
        cp.wait()


def _place():
    return lax.axis_index("x"), lax.axis_index("y"), lax.axis_index("c")


def _other_chips(x, y):
    return [(1 - x, y), (x, 1 - y), (1 - x, 1 - y)]


class _Rider:
    def __init__(self, srcs, out_shapes, per_array, make, through=False):
        self.srcs, self.out_shapes, self.per_array, self.make, self.through = list(srcs), list(out_shapes), per_array, make, through
        self.n = len(self.srcs)

    def scratch(self):
        return [pltpu.SemaphoreType.DMA((self.n, self.per_array)), pltpu.SemaphoreType.DMA((self.n, self.per_array))]


def _ride(rider, first, last, src_refs, dst_refs, sems):
    if rider is None:
        return (lambda: None), (lambda: None)
    cps = rider.make(src_refs, dst_refs, *sems)

    def guarded(cond, fn):
        if cond is True:
            fn()
        else:
            pl.when(cond)(fn)

    def start():
        guarded(first, lambda: [cp.start() for cp in cps] and None)

    def finish():
        guarded(last, lambda: [cp.wait() for cp in cps] and None)

    return start, finish


def _rider_parts(rider, n_in, n_out):
    if rider is None:
        return [], [], [], [], [], {}
    aliases = {n_in + a: n_out + a for a in range(rider.n)} if rider.through else {}
    return rider.srcs, [ANY] * rider.n, [ANY] * rider.n, rider.out_shapes, rider.scratch(), aliases


def _remote(src, dst, send_sems, recv_sems, a, k, to):
    return pltpu.make_async_remote_copy(src_ref=src, dst_ref=dst, send_sem=send_sems.at[a, k], recv_sem=recv_sems.at[a, k],
                                        device_id=to, device_id_type=MESH)


def _gather_send_rider(blocks):
    def make(srcs, dsts, ss, rs):
        x, y, cc = _place()
        return [_remote(srcs[a], dsts[a].at[2 * x + y, cc], ss, rs, a, j, (*chip, cc))
                for j, chip in enumerate(_other_chips(x, y)) for a in range(len(srcs))]
    shapes = [jax.ShapeDtypeStruct((N_CHIPS, 2) + b.shape, b.dtype) for b in blocks]
    return _Rider(blocks, shapes, 3, make)


def _gather_forward_rider(landed):
    def make(srcs, dsts, ss, rs):
        x, y, cc = _place()
        return [_remote(dsts[a].at[2 * px + py, cc], dsts[a].at[2 * px + py, cc], ss, rs, a, j, (x, y, 1 - cc))
                for j, (px, py) in enumerate(_other_chips(x, y)) for a in range(len(dsts))]
    shapes = [jax.ShapeDtypeStruct(b.shape, b.dtype) for b in landed]
    return _Rider(landed, shapes, 3, make, through=True)


def _pair_rider(arrays):
    def make(srcs, dsts, ss, rs):
        x, y, cc = _place()
        return [_remote(srcs[a], dsts[a], ss, rs, a, 0, (x, y, 1 - cc)) for a in range(len(srcs))]
    return _Rider(arrays, [jax.ShapeDtypeStruct(b.shape, b.dtype) for b in arrays], 1, make)


def _chip_rider(arrays):
    def make(srcs, dsts, ss, rs):
        x, y, cc = _place()
        return [_remote(srcs[a].at[2 * px + py], dsts[a].at[k], ss, rs, a, k, (px, py, cc))
                for k, (px, py) in enumerate(_other_chips(x, y)) for a in range(len(srcs))]
    return _Rider(arrays, [jax.ShapeDtypeStruct((3,) + b.shape[1:], b.dtype) for b in arrays], 3, make)


INPROJ_TN = 1536


def _inproj_fwd(x, g, w, w_bf):
    s = x.shape[0]
    tm, tn = min(1024, s), INPROJ_TN

    def body(x_ref, g_ref, w_ref, wbf_ref, proj_ref, pbf_ref, h_ref):
        @pl.when(pl.program_id(1) == 0)
        def _():
            xv = x_ref[...]
            r = lax.rsqrt(jnp.mean(xv * xv, axis=-1, keepdims=True) + EPS)
            h_ref[...] = ((xv * r) * g_ref[...]).astype(BF16)
            pbf_ref[...] = _dot(h_ref[...], wbf_ref[...])
        proj_ref[...] = _dot(h_ref[...], w_ref[...])

    return pl.pallas_call(
        body, name="inproj_fwd", grid=(s // tm, PCOLS // tn),
        in_specs=[pl.BlockSpec((tm, D_MODEL), lambda i, j: (i, 0)), pl.BlockSpec((1, D_MODEL), lambda i, j: (0, 0)),
                  pl.BlockSpec((D_MODEL, tn), lambda i, j: (0, j)), pl.BlockSpec((D_MODEL, CHUNK), lambda i, j: (0, 0))],
        out_specs=[pl.BlockSpec((tm, tn), lambda i, j: (i, j)), pl.BlockSpec((tm, CHUNK), lambda i, j: (i, 0)),
                   pl.BlockSpec((tm, D_MODEL), lambda i, j: (i, 0))],
        out_shape=[jax.ShapeDtypeStruct((s, PCOLS), F32), jax.ShapeDtypeStruct((s, CHUNK), F32), jax.ShapeDtypeStruct((s, D_MODEL), BF16)],
        compiler_params=_params("arbitrary", "arbitrary"))(x, g, w, w_bf)


def _rms_bwd(dh, x, g):
    r = lax.rsqrt(jnp.mean(x * x, axis=-1, keepdims=True) + EPS)
    xn = x * r
    gy = dh * g
    dx = r * (gy - xn * jnp.mean(xn * gy, axis=-1, keepdims=True))
    return dx, _part8(dh * xn)


def _inproj_bwd_x(dproj, w, x, dxn, g, dproj_bf, w_bf, rider=None):
    s = x.shape[0]
    tm, tk = min(512, s), PCOLS // 2
    nk = PCOLS // tk
    ni = s // tm
    r_in, r_in_specs, r_out_specs, r_shapes, r_scratch, r_alias = _rider_parts(rider, 7, 2)
    nr = len(r_in)

    def body(*refs):
        dp_ref, w_ref, x_ref, dxn_ref, g_ref, dpbf_ref, wbf_ref = refs[:7]
        dx_ref, dg_ref = refs[7 + nr:9 + nr]
        acc_ref = refs[9 + 2 * nr]
        i, k = pl.program_id(0), pl.program_id(1)
        start, finish = _ride(rider, (i == 0) & (k == 0), (i == ni - 1) & (k == nk - 1),
                              refs[7:7 + nr], refs[9 + nr:9 + 2 * nr], refs[10 + 2 * nr:])
        start()

        @pl.when(k == 0)
        def _():
            acc_ref[...] = _dot_nt(dpbf_ref[...], wbf_ref[...])

        @pl.when((i == 0) & (k == 0))
        def _():
            dg_ref[...] = jnp.zeros_like(dg_ref)

        acc_ref[...] += _dot_nt(dp_ref[...], w_ref[...])

        @pl.when(k == nk - 1)
        def _():
            dx, dg8 = _rms_bwd(acc_ref[...], x_ref[...], g_ref[...])
            dx_ref[...] = dxn_ref[...] + dx
            dg_ref[...] += dg8

        finish()

    outs = pl.pallas_call(
        body, name="inproj_bwd_x", grid=(ni, nk),
        in_specs=[pl.BlockSpec((tm, tk), lambda i, k: (i, k)), pl.BlockSpec((D_MODEL, tk), lambda i, k: (0, k)),
                  pl.BlockSpec((tm, D_MODEL), lambda i, k: (i, 0)), pl.BlockSpec((tm, D_MODEL), lambda i, k: (i, 0)),
                  pl.BlockSpec((1, D_MODEL), lambda i, k: (0, 0)), pl.BlockSpec((tm, CHUNK), lambda i, k: (i, 0)),
                  pl.BlockSpec((D_MODEL, CHUNK), lambda i, k: (0, 0))] + r_in_specs,
        out_specs=[pl.BlockSpec((tm, D_MODEL), lambda i, k: (i, 0)), pl.BlockSpec((8, D_MODEL), lambda i, k: (0, 0))] + r_out_specs,
        out_shape=[jax.ShapeDtypeStruct((s, D_MODEL), F32), jax.ShapeDtypeStruct((8, D_MODEL), F32)] + r_shapes,
        scratch_shapes=[pltpu.VMEM((tm, D_MODEL), F32)] + r_scratch, input_output_aliases=r_alias,
        compiler_params=_params("arbitrary", "arbitrary"))(dproj, w, x, dxn, g, dproj_bf, w_bf, *r_in)
    return outs[0], outs[1], outs[2:]


def _inproj_bwd_w(h, dproj, dproj_bf):
    s = h.shape[0]
    tn, tk = INPROJ_TN, min(1024, s)
    nk = s // tk

    def body(h_ref, dp_ref, dpbf_ref, dw_ref, dwbf_ref, acc_ref, accbf_ref):
        j, k = pl.program_id(0), pl.program_id(1)

        @pl.when(k == 0)
        def _():
            acc_ref[...] = jnp.zeros_like(acc_ref)

        acc_ref[...] += _dot_tn(h_ref[...], dp_ref[...])

        @pl.when(k == nk - 1)
        def _():
            dw_ref[...] = acc_ref[...].astype(BF16)

        @pl.when((j == 0) & (k == 0))
        def _():
            accbf_ref[...] = jnp.zeros_like(accbf_ref)

        @pl.when(j == 0)
        def _():
            accbf_ref[...] += _dot_tn(h_ref[...], dpbf_ref[...])

        @pl.when((j == 0) & (k == nk - 1))
        def _():
            dwbf_ref[...] = accbf_ref[...].astype(BF16)

    return pl.pallas_call(
        body, name="inproj_bwd_w", grid=(PCOLS // tn, nk),
        in_specs=[pl.BlockSpec((tk, D_MODEL), lambda j, k: (k, 0)), pl.BlockSpec((tk, tn), lambda j, k: (k, j)),
                  pl.BlockSpec((tk, CHUNK), lambda j, k: (k, 0))],
        out_specs=[pl.BlockSpec((D_MODEL, tn), lambda j, k: (0, j)), pl.BlockSpec((D_MODEL, CHUNK), lambda j, k: (0, 0))],
        out_shape=[jax.ShapeDtypeStruct((D_MODEL, PCOLS), BF16), jax.ShapeDtypeStruct((D_MODEL, CHUNK), BF16)],
        scratch_shapes=[pltpu.VMEM((D_MODEL, tn), F32), pltpu.VMEM((D_MODEL, CHUNK), F32)],
        compiler_params=_params("arbitrary", "arbitrary"))(h, dproj, dproj_bf)


def _mix_a_chunk(uvp, agp, wm_ref, bias, lg, lb):
    u, du = _gelu_and_grad(uvp[:, :BR])
    v, dv = _gelu_and_grad(uvp[:, BR:])
    xh, rs = _ln_fwd(v)
    vnb = (xh * lg + lb).astype(BF16)
    masks = _head_masks(BF16)
    mixed = bias
    for h in range(N_HEADS):
        mixed = mixed + _dot(wm_ref[h], vnb * masks[h])
    sg, dsg = _silu_and_grad(agp)
    return u, du, dv, xh, rs, vnb, masks, mixed, sg, dsg


def _store_masked_sgu(sw_ref, wm_ref):
    row = lax.broadcasted_iota(jnp.int32, (CHUNK, CHUNK), 0)
    col = lax.broadcasted_iota(jnp.int32, (CHUNK, CHUNK), 1)
    for h in range(N_HEADS):
        wm_ref[h] = jnp.where(row >= col, sw_ref[h], 0.0).astype(BF16)


def _mix_a_fwd(proj, sgu_w, bias, lg, lb):
    s = proj.shape[0]

    def body(proj_ref, sw_ref, bias_ref, lg_ref, lb_ref, y_ref, uv_buf, ag_buf, wm_ref, sems):
        _copy_all([(_window(proj_ref, AUV0, 2 * BR), uv_buf), (_window(proj_ref, AG0, BR), ag_buf)], sems)
        _store_masked_sgu(sw_ref, wm_ref)

        def chunk(i, c):
            rows = pl.ds(pl.multiple_of(i * CHUNK, CHUNK), CHUNK)
            u, _, _, _, _, _, _, mixed, sg, _ = _mix_a_chunk(uv_buf[rows, :], ag_buf[rows, :], wm_ref, bias_ref[...],
                                                            lg_ref[...], lb_ref[...])
            y_ref[rows, :] = (u * mixed * sg).astype(BF16)
            return c

        lax.fori_loop(0, s // CHUNK, chunk, 0)

    return pl.pallas_call(
        body, name="mix_a_fwd", in_specs=[ANY, VMEM, VMEM, VMEM, VMEM], out_specs=VMEM,
        out_shape=jax.ShapeDtypeStruct((s, BR), BF16),
        scratch_shapes=[pltpu.VMEM((s, 2 * BR), F32), pltpu.VMEM((s, BR), F32), pltpu.VMEM((N_HEADS, CHUNK, CHUNK), BF16),
                        pltpu.SemaphoreType.DMA((2,))],
        compiler_params=_params())(proj, sgu_w, bias, lg, lb)


def _mix_a_bwd(proj, dproj, dy, sgu_w, bias, lg, lb):
    s = proj.shape[0]

    def body(proj_ref, dproj_in, dy_ref, sw_ref, bias_ref, lg_ref, lb_ref,
             dproj_ref, dsw_ref, dbias_ref, dlg_ref, dlb_ref,
             uv_buf, ag_buf, duv_buf, dag_buf, wm_ref, acc_lg, acc_lb, sems):
        del dproj_in
        _copy_all([(_window(proj_ref, AUV0, 2 * BR), uv_buf), (_window(proj_ref, AG0, BR), ag_buf)], sems)
        _store_masked_sgu(sw_ref, wm_ref)
        dsw_ref[...] = jnp.zeros_like(dsw_ref)
        dbias_ref[...] = jnp.zeros_like(dbias_ref)
        acc_lg[...] = jnp.zeros_like(acc_lg)
        acc_lb[...] = jnp.zeros_like(acc_lb)

        def chunk(i, c):
            rows = pl.ds(pl.multiple_of(i * CHUNK, CHUNK), CHUNK)
            lg_v = lg_ref[...]
            u, du, dv, xh, rs, vnb, masks, mixed, sg, dsg = _mix_a_chunk(
                uv_buf[rows, :], ag_buf[rows, :], wm_ref, bias_ref[...], lg_v, lb_ref[...])
            dyv = dy_ref[rows, :]
            t1 = dyv * sg
            d_u = t1 * mixed
            d_mixed = t1 * u
            d_ag = dyv * u * mixed * dsg
            dbias_ref[...] += d_mixed
            dmb = d_mixed.astype(BF16)
            d_vn = jnp.zeros((CHUNK, BR), F32)
            for h in range(N_HEADS):
                dm_h = dmb * masks[h]
                dsw_ref[h] += _dot_nt(dm_h, vnb)
                d_vn = d_vn + _dot_tn(wm_ref[h], dm_h)
            acc_lg[...] += _part8(d_vn * xh)
            acc_lb[...] += _part8(d_vn)
            d_v = _ln_bwd(d_vn * lg_v, xh, rs)
            duv_buf[rows, :] = jnp.concatenate([d_u * du, d_v * dv], axis=1).astype(BF16)
            dag_buf[rows, :] = d_ag.astype(BF16)
            return c

        lax.fori_loop(0, s // CHUNK, chunk, 0)
        row = lax.broadcasted_iota(jnp.int32, (CHUNK, CHUNK), 0)
        col = lax.broadcasted_iota(jnp.int32, (CHUNK, CHUNK), 1)
        for h in range(N_HEADS):
            dsw_ref[h] = jnp.where(row >= col, dsw_ref[h], 0.0)
        dlg_ref[...] = jnp.sum(acc_lg[...], axis=0, keepdims=True)
        dlb_ref[...] = jnp.sum(acc_lb[...], axis=0, keepdims=True)
        _copy_all([(duv_buf, _window(dproj_ref, AUV0, 2 * BR)), (dag_buf, _window(dproj_ref, AG0, BR))], sems)

    return pl.pallas_call(
        body, name="mix_a_bwd", in_specs=[ANY, ANY, VMEM, VMEM, VMEM, VMEM, VMEM],
        out_specs=[ANY, VMEM, VMEM, VMEM, VMEM],
        out_shape=[jax.ShapeDtypeStruct(dproj.shape, dproj.dtype), jax.ShapeDtypeStruct((N_HEADS, CHUNK, CHUNK), F32),
                   jax.ShapeDtypeStruct((CHUNK, BR), F32), jax.ShapeDtypeStruct((1, BR), F32), jax.ShapeDtypeStruct((1, BR), F32)],
        scratch_shapes=[pltpu.VMEM((s, 2 * BR), F32), pltpu.VMEM((s, BR), F32), pltpu.VMEM((s, 2 * BR), BF16),
                        pltpu.VMEM((s, BR), BF16), pltpu.VMEM((N_HEADS, CHUNK, CHUNK), BF16),
                        pltpu.VMEM((8, BR), F32), pltpu.VMEM((8, BR), F32), pltpu.SemaphoreType.DMA((2,))],
        input_output_aliases={1: 0}, compiler_params=_params())(proj, dproj, dy, sgu_w, bias, lg, lb)


def _tri_ones(n, upper):
    row = lax.broadcasted_iota(jnp.int32, (n, n), 0)
    col = lax.broadcasted_iota(jnp.int32, (n, n), 1)
    return ((row <= col) if upper else (row >= col)).astype(BF16)


def _split3(c):
    hi = c.astype(BF16)
    r1 = c - hi.astype(F32)
    mid = r1.astype(BF16)
    lo = (r1 - mid.astype(F32)).astype(BF16)
    return [hi, mid, lo]


def _tri_sum(tri, a):
    parts = _split3(a)
    return _dot(tri, parts[0]) + _dot(tri, parts[1]) + _dot(tri, parts[2])


EXT = 2 * HEAD_DIM
LANE_CQ = HEAD_DIM
LANE_CK = HEAD_DIM + 3


def _to_head(h):
    r = lax.broadcasted_iota(jnp.int32, (BR, EXT), 0)
    c = lax.broadcasted_iota(jnp.int32, (BR, EXT), 1)
    return ((r == c + h * HEAD_DIM) & (c < HEAD_DIM)).astype(BF16)


def _from_head(h):
    r = lax.broadcasted_iota(jnp.int32, (EXT, BR), 0)
    c = lax.broadcasted_iota(jnp.int32, (EXT, BR), 1)
    return ((c == r + h * HEAD_DIM) & (r < HEAD_DIM)).astype(BF16)


def _attn_prep_fwd(proj, proj_bf, f_bias):
    s = proj.shape[0]
    scale = 1.0 / math.sqrt(HEAD_DIM)

    def body(proj_ref, z_buf, fb_ref, qe_ref, ke_ref, ve_ref, ket_ref, vet_ref, qkv_buf, sems):
        _copy_all([(_window(proj_ref, QKV0, 3 * BR), qkv_buf)], sems)
        tri = _tri_ones(CHUNK, upper=False)
        lane = lax.broadcasted_iota(jnp.int32, (CHUNK, EXT), 1)
        ones_q = ((lane >= LANE_CK) & (lane < LANE_CK + 3)).astype(F32)
        ones_k = ((lane >= LANE_CQ) & (lane < LANE_CQ + 3)).astype(F32)

        def chunk(i, carry):
            rows = pl.ds(pl.multiple_of(i * CHUNK, CHUNK), CHUNK)
            cum = _tri_sum(tri, jax.nn.log_sigmoid(z_buf[rows, :] + fb_ref[...])) + carry
            qb = (qkv_buf[rows, 0:BR] * scale).astype(BF16)
            kb = qkv_buf[rows, BR:2 * BR].astype(BF16)
            vb = qkv_buf[rows, 2 * BR:].astype(BF16)
            parts = [p.astype(F32) for p in _split3(cum)]
            for h in range(N_HEADS):
                sel = _to_head(h)
                dec_q, dec_k = ones_q, ones_k
                for t, part in enumerate(parts):
                    pf = part[:, h:h + 1]
                    dec_q = dec_q + jnp.where(lane == LANE_CQ + t, pf, 0.0)
                    dec_k = dec_k - jnp.where(lane == LANE_CK + t, pf, 0.0)
                qe_ref[h, rows, :] = (_dot(qb, sel) + dec_q).astype(BF16)
                kh = _dot(kb, sel) + dec_k
                vh = _dot(vb, sel)
                ke_ref[h, rows, :] = kh.astype(BF16)
                ve_ref[h, rows, :] = vh.astype(BF16)
                ket_ref[h, :, rows] = kh.T.astype(BF16)
                vet_ref[h, :, rows] = vh.T.astype(BF16)
            return cum[CHUNK - 1:CHUNK, :]

        lax.fori_loop(0, s // CHUNK, chunk, jnp.zeros((1, CHUNK), F32))

    shape = jax.ShapeDtypeStruct((N_HEADS, s, EXT), BF16)
    shape_t = jax.ShapeDtypeStruct((N_HEADS, EXT, s), BF16)
    return pl.pallas_call(
        body, name="attn_prep_fwd", in_specs=[ANY, VMEM, VMEM], out_specs=[VMEM] * 5, out_shape=[shape, shape, shape, shape_t, shape_t],
        scratch_shapes=[pltpu.VMEM((s, 3 * BR), F32), pltpu.SemaphoreType.DMA((1,))],
        compiler_params=_params())(proj, proj_bf, f_bias)


def _attn_prep_bwd(proj_bf, dproj, dqe, dke, dve, f_bias):
    s = proj_bf.shape[0]
    scale = 1.0 / math.sqrt(HEAD_DIM)

    def body(z_buf, dproj_in, dq_ref, dk_ref, dv_ref, fb_ref, dproj_ref, dz_buf, dfb_ref, dqkv_buf, sems):
        del dproj_in
        tri = _tri_ones(CHUNK, upper=True)
        lane = lax.broadcasted_iota(jnp.int32, (CHUNK, CHUNK), 1)
        n = s // CHUNK

        def chunk(t, carry):
            suffix, acc = carry
            i = n - 1 - t
            rows = pl.ds(pl.multiple_of(i * CHUNK, CHUNK), CHUNK)
            dq = jnp.zeros((CHUNK, BR), F32)
            dk = jnp.zeros((CHUNK, BR), F32)
            dv = jnp.zeros((CHUNK, BR), F32)
            dcum = jnp.zeros((CHUNK, CHUNK), F32)
            for h in range(N_HEADS):
                back = _from_head(h)
                dqh = dq_ref[h, :, rows].T
                dkh = dk_ref[h, rows, :]
                dq = dq + _dot((dqh * scale).astype(BF16), back)
                dk = dk + _dot(dkh.astype(BF16), back)
                dv = dv + _dot(dv_ref[h, rows, :].astype(BF16), back)
                dcum = dcum + jnp.where(lane == h, dqh[:, LANE_CQ:LANE_CQ + 1] - dkh[:, LANE_CK:LANE_CK + 1], 0.0)
            dqkv_buf[rows, 0:BR] = dq.astype(BF16)
            dqkv_buf[rows, BR:2 * BR] = dk.astype(BF16)
            dqkv_buf[rows, 2 * BR:] = dv.astype(BF16)
            dlf = _tri_sum(tri, dcum) + suffix
            dz = dlf * _sigmoid(-(z_buf[rows, :] + fb_ref[...]))
            dz_buf[rows, :] = dz.astype(BF16)
            return dlf[0:1, :], acc + _part8(dz)

        _, acc = lax.fori_loop(0, n, chunk, (jnp.zeros((1, CHUNK), F32), jnp.zeros((8, CHUNK), F32)))
        dfb_ref[...] = jnp.sum(acc, axis=0, keepdims=True)
        _copy_all([(dqkv_buf, _window(dproj_ref, QKV0, 3 * BR))], sems)

    return pl.pallas_call(
        body, name="attn_prep_bwd", in_specs=[VMEM, ANY, VMEM, VMEM, VMEM, VMEM], out_specs=[ANY, VMEM, VMEM],
        out_shape=[jax.ShapeDtypeStruct(dproj.shape, dproj.dtype), jax.ShapeDtypeStruct((s, CHUNK), BF16),
                   jax.ShapeDtypeStruct((1, CHUNK), F32)],
        scratch_shapes=[pltpu.VMEM((s, 3 * BR), BF16), pltpu.SemaphoreType.DMA((1,))],
        input_output_aliases={1: 0}, compiler_params=_params())(proj_bf, dproj, dqe, dke, dve, f_bias)


ATT_TQ = 256
ATT_FWD_GROUP = 8
ATT_BWD_GROUP = 4
NEG_BIG = -1e30


def _causal_t(q0, k0, tk, tq):
    kpos = k0 + lax.broadcasted_iota(jnp.int32, (tk, tq), 0)
    qpos = q0 + lax.broadcasted_iota(jnp.int32, (tk, tq), 1)
    return kpos <= qpos


STAT_ROWS = 8


def _attn_fwd(qe, ke, vet, rider=None):
    nh, s, w = qe.shape
    tq = min(ATT_TQ, s)
    tk = min(ATT_FWD_GROUP * tq, s)
    per = tk // tq
    nq = s // tq
    r_in, r_in_specs, r_out_specs, r_shapes, r_scratch, r_alias = _rider_parts(rider, 3, 2)
    nr = len(r_in)

    def body(*refs):
        q_ref, k_ref, vt_ref = refs[:3]
        ot_ref, lse_ref = refs[3 + nr:5 + nr]
        h, i = pl.program_id(0), pl.program_id(1)
        start, finish = _ride(rider, (h == 0) & (i == 0), (h == nh - 1) & (i == nq - 1),
                              refs[3:3 + nr], refs[5 + nr:5 + 2 * nr], refs[5 + 2 * nr:])
        start()
        q = q_ref[0]

        def step(k0, carry, width, masked):
            m, l, acc = carry
            ks = pl.ds(pl.multiple_of(k0, tq), width)
            st = _dot_nt(k_ref[0, ks, :], q)
            if masked:
                st = jnp.where(_causal_t(i * tq, k0, width, tq), st, NEG_BIG)
            m_new = jnp.maximum(m, jnp.max(st, axis=0, keepdims=True))
            alpha = jnp.exp(m - m_new)
            pt = jnp.exp(st - m_new)
            l = alpha * l + jnp.sum(pt, axis=0, keepdims=True)
            acc = alpha * acc + _dot(vt_ref[0, :, ks], pt.astype(BF16))
            return m_new, l, acc

        full = i // per
        init = (jnp.full((1, tq), NEG_BIG, F32), jnp.zeros((1, tq), F32), jnp.zeros((w, tq), F32))
        carry = lax.fori_loop(0, full, lambda j, c: step(j * tk, c, tk, False), init)
        m, l, acc = lax.switch(i % per, [functools.partial(step, width=(r + 1) * tq, masked=True) for r in range(per)],
                               full * tk, carry)
        ot_ref[0] = acc / l
        lse_ref[0] = jnp.broadcast_to(m + jnp.log(l), (STAT_ROWS, tq))
        finish()

    outs = pl.pallas_call(
        body, name="attn_fwd", grid=(nh, nq),
        in_specs=[pl.BlockSpec((1, tq, w), lambda h, i: (h, i, 0)), pl.BlockSpec((1, s, w), lambda h, i: (h, 0, 0)),
                  pl.BlockSpec((1, w, s), lambda h, i: (h, 0, 0))] + r_in_specs,
        out_specs=[pl.BlockSpec((1, w, tq), lambda h, i: (h, 0, i)), pl.BlockSpec((1, STAT_ROWS, tq), lambda h, i: (h, 0, i))] + r_out_specs,
        out_shape=[jax.ShapeDtypeStruct((nh, w, s), F32), jax.ShapeDtypeStruct((nh, STAT_ROWS, s), F32)] + r_shapes,
        scratch_shapes=r_scratch, input_output_aliases=r_alias,
        compiler_params=_params("arbitrary", "arbitrary"))(qe, ke, vet, *r_in)
    return outs[0], outs[1], outs[2:]


def _attn_bwd(qe, ke, ket, ve, doe, lse, dd, rider=None):
    nh, s, w = qe.shape
    tq = min(ATT_TQ, s)
    tk = min(ATT_BWD_GROUP * tq, s)
    per = tk // tq
    nq = s // tq
    r_in, r_in_specs, r_out_specs, r_shapes, r_scratch, r_alias = _rider_parts(rider, 7, 3)
    nr = len(r_in)

    def body(*refs):
        q_ref, do_ref, lse_ref, dd_ref, k_ref, kt_ref, v_ref = refs[:7]
        dqt_ref, dk_ref, dv_ref = refs[7 + nr:10 + nr]
        h, i = pl.program_id(0), pl.program_id(1)
        start, finish = _ride(rider, (h == 0) & (i == 0), (h == nh - 1) & (i == nq - 1),
                              refs[7:7 + nr], refs[10 + nr:10 + 2 * nr], refs[10 + 2 * nr:])
        start()

        @pl.when(i == 0)
        def _():
            dk_ref[...] = jnp.zeros_like(dk_ref)
            dv_ref[...] = jnp.zeros_like(dv_ref)

        q = q_ref[0]
        do = do_ref[0]
        lse_row = lse_ref[0, 0:1, :]
        dd_row = dd_ref[0, 0:1, :]

        def step(k0, dqt, width, masked):
            ks = pl.ds(pl.multiple_of(k0, tq), width)
            st = _dot_nt(k_ref[0, ks, :], q)
            pt = jnp.exp(st - lse_row)
            if masked:
                pt = jnp.where(_causal_t(i * tq, k0, width, tq), pt, 0.0)
            dpt = _dot_nt(v_ref[0, ks, :], do)
            dst = (pt * (dpt - dd_row)).astype(BF16)
            dv_ref[0, ks, :] += _dot(pt.astype(BF16), do)
            dk_ref[0, ks, :] += _dot(dst, q)
            return dqt + _dot(kt_ref[0, :, ks], dst)

        full = i // per
        dqt = lax.fori_loop(0, full, lambda j, c: step(j * tk, c, tk, False), jnp.zeros((w, tq), F32))
        dqt_ref[0] = lax.switch(i % per, [functools.partial(step, width=(r + 1) * tq, masked=True) for r in range(per)],
                                full * tk, dqt)
        finish()

    qblk = pl.BlockSpec((1, tq, w), lambda h, i: (h, i, 0))
    stat = pl.BlockSpec((1, STAT_ROWS, tq), lambda h, i: (h, 0, i))
    whole = pl.BlockSpec((1, s, w), lambda h, i: (h, 0, 0))
    whole_t = pl.BlockSpec((1, w, s), lambda h, i: (h, 0, 0))
    outs = pl.pallas_call(
        body, name="attn_bwd", grid=(nh, nq),
        in_specs=[qblk, qblk, stat, stat, whole, whole_t, whole] + r_in_specs,
        out_specs=[pl.BlockSpec((1, w, tq), lambda h, i: (h, 0, i)), whole, whole] + r_out_specs,
        out_shape=[jax.ShapeDtypeStruct((nh, w, s), F32), jax.ShapeDtypeStruct((nh, s, w), F32),
                   jax.ShapeDtypeStruct((nh, s, w), F32)] + r_shapes,
        scratch_shapes=r_scratch, input_output_aliases=r_alias,
        compiler_params=_params("arbitrary", "arbitrary"))(qe, doe, lse, dd, ke, ket, ve, *r_in)
    return outs[0], outs[1], outs[2], outs[3:]


def _bgate_fwd(proj, ot, rider=None):
    s = proj.shape[0]
    r_in, r_in_specs, r_out_specs, r_shapes, r_scratch, r_alias = _rider_parts(rider, 2, 2)
    nr = len(r_in)

    def body(*refs):
        proj_ref, ot_ref = refs[:2]
        y_ref, o_ref = refs[2 + nr:4 + nr]
        g_buf, sems = refs[4 + 2 * nr:6 + 2 * nr]
        start, finish = _ride(rider, True, True, refs[2:2 + nr], refs[4 + nr:4 + 2 * nr], refs[6 + 2 * nr:])
        start()
        _copy_all([(_window(proj_ref, BG0, BR), g_buf)], sems)

        def chunk(i, c):
            rows = pl.ds(pl.multiple_of(i * CHUNK, CHUNK), CHUNK)
            o = jnp.zeros((CHUNK, BR), F32)
            for h in range(N_HEADS):
                back = _from_head(h)
                for part in _split3(ot_ref[h, :, rows].T):
                    o = o + _dot(part, back)
            sg, _ = _silu_and_grad(g_buf[rows, :])
            o_ref[rows, :] = o
            y_ref[rows, :] = (o * sg).astype(BF16)
            return c

        lax.fori_loop(0, s // CHUNK, chunk, 0)
        finish()

    outs = pl.pallas_call(
        body, name="bgate_fwd", in_specs=[ANY, VMEM] + r_in_specs, out_specs=[VMEM, VMEM] + r_out_specs,
        out_shape=[jax.ShapeDtypeStruct((s, BR), BF16), jax.ShapeDtypeStruct((s, BR), F32)] + r_shapes,
        scratch_shapes=[pltpu.VMEM((s, BR), F32), pltpu.SemaphoreType.DMA((1,))] + r_scratch,
        input_output_aliases=r_alias, compiler_params=_params())(proj, ot, *r_in)
    return outs[0], outs[1], outs[2:]


def _bgate_bwd(proj, dproj, o, dy):
    s = proj.shape[0]

    def body(proj_ref, dproj_in, o_ref, dy_ref, dproj_ref, do_ref, dd_ref, g_buf, dg_buf, sems):
        del dproj_in
        _copy_all([(_window(proj_ref, BG0, BR), g_buf)], sems)
        lane = lax.broadcasted_iota(jnp.int32, (BR, CHUNK), 0) // HEAD_DIM
        col = lax.broadcasted_iota(jnp.int32, (BR, CHUNK), 1)
        sel = (lane == col).astype(BF16)

        def chunk(i, c):
            rows = pl.ds(pl.multiple_of(i * CHUNK, CHUNK), CHUNK)
            sg, dsg = _silu_and_grad(g_buf[rows, :])
            dyv = dy_ref[rows, :]
            ov = o_ref[rows, :]
            do = dyv * sg
            dg_buf[rows, :] = (dyv * ov * dsg).astype(BF16)
            prod = _split3(do * ov)
            ddt = (_dot(prod[0], sel) + _dot(prod[1], sel) + _dot(prod[2], sel)).T
            dob = do.astype(BF16)
            for h in range(N_HEADS):
                do_ref[h, rows, :] = _dot(dob, _to_head(h)).astype(BF16)
                dd_ref[h, :, rows] = jnp.broadcast_to(ddt[h:h + 1, :], (STAT_ROWS, CHUNK))
            return c

        lax.fori_loop(0, s // CHUNK, chunk, 0)
        _copy_all([(dg_buf, _window(dproj_ref, BG0, BR))], sems)

    return pl.pallas_call(
        body, name="bgate_bwd", in_specs=[ANY, ANY, VMEM, VMEM], out_specs=[ANY, VMEM, VMEM],
        out_shape=[jax.ShapeDtypeStruct(dproj.shape, dproj.dtype), jax.ShapeDtypeStruct((N_HEADS, s, EXT), BF16),
                   jax.ShapeDtypeStruct((N_HEADS, STAT_ROWS, s), F32)],
        scratch_shapes=[pltpu.VMEM((s, BR), F32), pltpu.VMEM((s, BR), BF16), pltpu.SemaphoreType.DMA((1,))],
        input_output_aliases={1: 0}, compiler_params=_params())(proj, dproj, o, dy)


C_PAD = 8


def _mix_c_fwd(proj, w):
    s = proj.shape[0]

    def body(proj_ref, w_ref, y_ref, cin_buf, g_buf, z_buf, sems):
        _copy_all([(_window(proj_ref, CIN0, 3 * BR), cin_buf), (_window(proj_ref, CG0, BR), g_buf)], sems)
        z_buf[pl.ds(0, C_PAD), :] = jnp.zeros((C_PAD, BR), F32)

        def fill(i, c):
            rows = pl.ds(pl.multiple_of(i * CHUNK, CHUNK), CHUNK)
            z_buf[pl.ds(pl.multiple_of(i * CHUNK + C_PAD, 8), CHUNK), :] = cin_buf[rows, BR:2 * BR] * cin_buf[rows, 2 * BR:]
            return c

        lax.fori_loop(0, s // CHUNK, fill, 0)

        def chunk(i, c):
            r0 = pl.multiple_of(i * CHUNK, CHUNK)
            rows = pl.ds(r0, CHUNK)
            ze = z_buf[pl.ds(r0, CHUNK + C_PAD), :]
            conv = jnp.zeros((CHUNK, BR), F32)
            for k in range(K_SHORT):
                off = C_PAD - (K_SHORT - 1) + k
                conv = conv + w_ref[k:k + 1, :] * ze[off:off + CHUNK]
            sg, _ = _silu_and_grad(g_buf[rows, :])
            y_ref[rows, :] = (cin_buf[rows, 0:BR] * conv * sg).astype(BF16)
            return c

        lax.fori_loop(0, s // CHUNK, chunk, 0)

    return pl.pallas_call(
        body, name="mix_c_fwd", in_specs=[ANY, VMEM], out_specs=VMEM, out_shape=jax.ShapeDtypeStruct((s, BR), BF16),
        scratch_shapes=[pltpu.VMEM((s, 3 * BR), F32), pltpu.VMEM((s, BR), F32), pltpu.VMEM((s + C_PAD, BR), F32),
                        pltpu.SemaphoreType.DMA((2,))],
        compiler_params=_params())(proj, w)


def _mix_c_bwd(proj, dproj, dy, w):
    s = proj.shape[0]

    def body(proj_ref, dproj_in, dy_ref, w_ref, dproj_ref, dw_ref, cin_buf, g_buf, z_buf, dc_buf, dcin_buf, dg_buf, acc, sems):
        del dproj_in
        _copy_all([(_window(proj_ref, CIN0, 3 * BR), cin_buf), (_window(proj_ref, CG0, BR), g_buf)], sems)
        z_buf[pl.ds(0, C_PAD), :] = jnp.zeros((C_PAD, BR), F32)
        dc_buf[pl.ds(s, C_PAD), :] = jnp.zeros((C_PAD, BR), F32)
        acc[...] = jnp.zeros_like(acc)

        def fill(i, c):
            rows = pl.ds(pl.multiple_of(i * CHUNK, CHUNK), CHUNK)
            z_buf[pl.ds(pl.multiple_of(i * CHUNK + C_PAD, 8), CHUNK), :] = cin_buf[rows, BR:2 * BR] * cin_buf[rows, 2 * BR:]
            return c

        lax.fori_loop(0, s // CHUNK, fill, 0)

        def chunk(i, c):
            r0 = pl.multiple_of(i * CHUNK, CHUNK)
            rows = pl.ds(r0, CHUNK)
            ze = z_buf[pl.ds(r0, CHUNK + C_PAD), :]
            taps = [ze[C_PAD - (K_SHORT - 1) + k:C_PAD - (K_SHORT - 1) + k + CHUNK] for k in range(K_SHORT)]
            conv = jnp.zeros((CHUNK, BR), F32)
            for k in range(K_SHORT):
                conv = conv + w_ref[k:k + 1, :] * taps[k]
            sg, dsg = _silu_and_grad(g_buf[rows, :])
            bg = cin_buf[rows, 0:BR]
            dyv = dy_ref[rows, :]
            dconv = dyv * bg * sg
            dc_buf[rows, :] = dconv
            dcin_buf[rows, 0:BR] = (dyv * conv * sg).astype(BF16)
            dg_buf[rows, :] = (dyv * bg * conv * dsg).astype(BF16)
            for k in range(K_SHORT):
                acc[k] += _part8(dconv * taps[k])
            return c

        lax.fori_loop(0, s // CHUNK, chunk, 0)

        def chunk2(i, c):
            r0 = pl.multiple_of(i * CHUNK, CHUNK)
            rows = pl.ds(r0, CHUNK)
            de = dc_buf[pl.ds(r0, CHUNK + C_PAD), :]
            dz = jnp.zeros((CHUNK, BR), F32)
            for k in range(K_SHORT):
                off = K_SHORT - 1 - k
                dz = dz + w_ref[k:k + 1, :] * de[off:off + CHUNK]
            dcin_buf[rows, BR:2 * BR] = (dz * cin_buf[rows, 2 * BR:]).astype(BF16)
            dcin_buf[rows, 2 * BR:] = (dz * cin_buf[rows, BR:2 * BR]).astype(BF16)
            return c

        lax.fori_loop(0, s // CHUNK, chunk2, 0)
        dw_ref[...] = jnp.zeros_like(dw_ref)
        for k in range(K_SHORT):
            dw_ref[k:k + 1, :] = jnp.sum(acc[k], axis=0, keepdims=True)
        _copy_all([(dcin_buf, _window(dproj_ref, CIN0, 3 * BR)), (dg_buf, _window(dproj_ref, CG0, BR))], sems)

    return pl.pallas_call(
        body, name="mix_c_bwd", in_specs=[ANY, ANY, VMEM, VMEM], out_specs=[ANY, VMEM],
        out_shape=[jax.ShapeDtypeStruct(dproj.shape, dproj.dtype), jax.ShapeDtypeStruct((8, BR), F32)],
        scratch_shapes=[pltpu.VMEM((s, 3 * BR), F32), pltpu.VMEM((s, BR), F32), pltpu.VMEM((s + C_PAD, BR), F32),
                        pltpu.VMEM((s + C_PAD, BR), F32), pltpu.VMEM((s, 3 * BR), BF16), pltpu.VMEM((s, BR), BF16),
                        pltpu.VMEM((K_SHORT, 8, BR), F32), pltpu.SemaphoreType.DMA((2,))],
        input_output_aliases={1: 0}, compiler_params=_params())(proj, dproj, dy, w)


D_PAD = 32


def _tap_windows(win, offsets, n):
    rows = win.shape[0]
    rolled, out = {}, []
    for off in offsets:
        r = off % 8
        if r not in rolled:
            rolled[r] = win if r == 0 else pltpu.roll(win, rows - r, 0)
        out.append(rolled[r][off - r:off - r + n])
    return out


def _mix_d_common(g_ref, w_ref, b_ref, lg_ref, lb_ref, hh_buf, r0):
    he = hh_buf[pl.ds(r0, CHUNK + D_PAD), :]
    taps = _tap_windows(he, [D_PAD - (K_CONF - 1) + k for k in range(K_CONF)], CHUNK)
    conv = jnp.zeros((CHUNK, BR), F32) + b_ref[...]
    for k in range(K_CONF):
        conv = conv + w_ref[k:k + 1, :] * taps[k]
    xh, rs = _ln_fwd(conv)
    sw, dsw = _silu_and_grad(xh * lg_ref[...] + lb_ref[...])
    sg, dsg = _silu_and_grad(g_ref[pl.ds(r0, CHUNK), :])
    return taps, xh, rs, sw, dsw, sg, dsg


def _mix_d_fill(cin_buf, hh_buf, s):
    hh_buf[pl.ds(0, D_PAD), :] = jnp.zeros((D_PAD, BR), F32)

    def fill(i, c):
        rows = pl.ds(pl.multiple_of(i * CHUNK, CHUNK), CHUNK)
        hh_buf[pl.ds(pl.multiple_of(i * CHUNK + D_PAD, 8), CHUNK), :] = cin_buf[rows, 0:BR] * _sigmoid(cin_buf[rows, BR:])
        return c

    lax.fori_loop(0, s // CHUNK, fill, 0)


def _mix_d_fwd(proj, w, b, lg, lb):
    s = proj.shape[0]

    def body(proj_ref, w_ref, b_ref, lg_ref, lb_ref, y_ref, cin_buf, g_buf, hh_buf, sems):
        _copy_all([(_window(proj_ref, DGLU0, 2 * BR), cin_buf), (_window(proj_ref, DG0, BR), g_buf)], sems)
        _mix_d_fill(cin_buf, hh_buf, s)

        def chunk(i, c):
            r0 = pl.multiple_of(i * CHUNK, CHUNK)
            _, _, _, sw, _, sg, _ = _mix_d_common(g_buf, w_ref, b_ref, lg_ref, lb_ref, hh_buf, r0)
            y_ref[pl.ds(r0, CHUNK), :] = (sw * sg).astype(BF16)
            return c

        lax.fori_loop(0, s // CHUNK, chunk, 0)

    return pl.pallas_call(
        body, name="mix_d_fwd", in_specs=[ANY, VMEM, VMEM, VMEM, VMEM], out_specs=VMEM,
        out_shape=jax.ShapeDtypeStruct((s, BR), BF16),
        scratch_shapes=[pltpu.VMEM((s, 2 * BR), F32), pltpu.VMEM((s, BR), F32), pltpu.VMEM((s + D_PAD, BR), F32),
                        pltpu.SemaphoreType.DMA((2,))],
        compiler_params=_params())(proj, w, b, lg, lb)


def _mix_d_bwd(proj, dproj, dy, w, b, lg, lb):
    s = proj.shape[0]

    def body(proj_ref, dproj_in, dy_ref, w_ref, b_ref, lg_ref, lb_ref, dproj_ref, dw_ref, db_ref, dlg_ref, dlb_ref,
             cin_buf, g_buf, hh_buf, dc_buf, dcin_buf, dg_buf, acc_w, acc_s, sems):
        del dproj_in
        _copy_all([(_window(proj_ref, DGLU0, 2 * BR), cin_buf), (_window(proj_ref, DG0, BR), g_buf)], sems)
        _mix_d_fill(cin_buf, hh_buf, s)
        dc_buf[pl.ds(s, D_PAD), :] = jnp.zeros((D_PAD, BR), F32)
        acc_w[...] = jnp.zeros_like(acc_w)
        acc_s[...] = jnp.zeros_like(acc_s)

        def chunk(i, c):
            r0 = pl.multiple_of(i * CHUNK, CHUNK)
            rows = pl.ds(r0, CHUNK)
            taps, xh, rs, sw, dsw, sg, dsg = _mix_d_common(g_buf, w_ref, b_ref, lg_ref, lb_ref, hh_buf, r0)
            dyv = dy_ref[rows, :]
            dg_buf[rows, :] = (dyv * sw * dsg).astype(BF16)
            d_ln = dyv * sg * dsw
            acc_s[0] += _part8(d_ln * xh)
            acc_s[1] += _part8(d_ln)
            dc = _ln_bwd(d_ln * lg_ref[...], xh, rs)
            acc_s[2] += _part8(dc)
            dc_buf[rows, :] = dc
            for k in range(K_CONF):
                acc_w[k] += _part8(dc * taps[k])
            return c

        lax.fori_loop(0, s // CHUNK, chunk, 0)

        def chunk2(i, c):
            r0 = pl.multiple_of(i * CHUNK, CHUNK)
            rows = pl.ds(r0, CHUNK)
            de = dc_buf[pl.ds(r0, CHUNK + D_PAD), :]
            dh = jnp.zeros((CHUNK, BR), F32)
            for k, win in enumerate(_tap_windows(de, [K_CONF - 1 - k for k in range(K_CONF)], CHUNK)):
                dh = dh + w_ref[k:k + 1, :] * win
            a = cin_buf[rows, 0:BR]
            sig = _sigmoid(cin_buf[rows, BR:])
            dcin_buf[rows, 0:BR] = (dh * sig).astype(BF16)
            dcin_buf[rows, BR:] = (dh * a * sig * (1.0 - sig)).astype(BF16)
            return c

        lax.fori_loop(0, s // CHUNK, chunk2, 0)
        dw_ref[...] = jnp.zeros_like(dw_ref)
        for k in range(K_CONF):
            dw_ref[k:k + 1, :] = jnp.sum(acc_w[k], axis=0, keepdims=True)
        dlg_ref[...] = jnp.sum(acc_s[0], axis=0, keepdims=True)
        dlb_ref[...] = jnp.sum(acc_s[1], axis=0, keepdims=True)
        db_ref[...] = jnp.sum(acc_s[2], axis=0, keepdims=True)
        _copy_all([(dcin_buf, _window(dproj_ref, DGLU0, 2 * BR)), (dg_buf, _window(dproj_ref, DG0, BR))], sems)

    vec = jax.ShapeDtypeStruct((1, BR), F32)
    return pl.pallas_call(
        body, name="mix_d_bwd", in_specs=[ANY, ANY, VMEM, VMEM, VMEM, VMEM, VMEM], out_specs=[ANY, VMEM, VMEM, VMEM, VMEM],
        out_shape=[jax.ShapeDtypeStruct(dproj.shape, dproj.dtype), jax.ShapeDtypeStruct((32, BR), F32), vec, vec, vec],
        scratch_shapes=[pltpu.VMEM((s, 2 * BR), F32), pltpu.VMEM((s, BR), F32), pltpu.VMEM((s + D_PAD, BR), F32),
                        pltpu.VMEM((s + D_PAD, BR), F32), pltpu.VMEM((s, 2 * BR), BF16), pltpu.VMEM((s, BR), BF16),
                        pltpu.VMEM((K_CONF, 8, BR), F32), pltpu.VMEM((3, 8, BR), F32), pltpu.SemaphoreType.DMA((2,))],
        input_output_aliases={1: 0}, compiler_params=_params())(proj, dproj, dy, w, b, lg, lb)


MERGE_TM = 256


def _merge_fwd(x, proj, ys, wb, wo):
    s = x.shape[0]
    tm = min(MERGE_TM, s)

    def body(x_ref, lg_ref, ya, yb, yc, yd, wb_ref, wo_ref, xn_ref, mg_ref):
        merged = jnp.zeros((tm, D_MODEL), F32)
        for n, y_ref in enumerate((ya, yb, yc, yd)):
            gate = _sigmoid(lg_ref[:, n * D_MODEL:(n + 1) * D_MODEL])
            merged = merged + gate * _dot(y_ref[...], wb_ref[n])
        mb = merged.astype(BF16)
        mg_ref[...] = mb
        xn_ref[...] = x_ref[...] + _dot(mb, wo_ref[...])

    row = lambda w: pl.BlockSpec((tm, w), lambda i: (i, 0))
    return pl.pallas_call(
        body, name="merge_fwd", grid=(s // tm,),
        in_specs=[row(D_MODEL), pl.BlockSpec((tm, MRG_COLS), lambda i: (i, MRG0 // MRG_COLS)), row(BR), row(BR), row(BR), row(BR),
                  pl.BlockSpec((N_HEADS, BR, D_MODEL), lambda i: (0, 0, 0)), pl.BlockSpec((D_MODEL, D_MODEL), lambda i: (0, 0))],
        out_specs=[row(D_MODEL), row(D_MODEL)],
        out_shape=[jax.ShapeDtypeStruct((s, D_MODEL), F32), jax.ShapeDtypeStruct((s, D_MODEL), BF16)],
        compiler_params=_params("arbitrary"))(x, proj, *ys, wb, wo)


def _merge_bwd(dxn, proj, ys, merged, wb, wo, rider=None):
    s = dxn.shape[0]
    tm = min(MERGE_TM, s)
    steps = s // tm
    r_in, r_in_specs, r_out_specs, r_shapes, r_scratch, r_alias = _rider_parts(rider, 9, 7)
    nr = len(r_in)

    def body(*refs):
        dx_ref, lg_ref, ya, yb, yc, yd, mg_ref, wb_ref, wo_ref = refs[:9]
        dlg_ref, da, db, dc, dd, dwo_ref, dwb_ref = refs[9 + nr:16 + nr]
        start, finish = _ride(rider, pl.program_id(0) == 0, pl.program_id(0) == steps - 1,
                              refs[9:9 + nr], refs[16 + nr:16 + 2 * nr], refs[16 + 2 * nr:])
        start()

        @pl.when(pl.program_id(0) == 0)
        def _():
            dwo_ref[...] = jnp.zeros_like(dwo_ref)
            dwb_ref[...] = jnp.zeros_like(dwb_ref)

        dxb = dx_ref[...].astype(BF16)
        d_merged = _dot_nt(dxb, wo_ref[...])
        dwo_ref[...] += _dot_tn(mg_ref[...], dxb)
        for n, (y_ref, dy_ref) in enumerate(((ya, da), (yb, db), (yc, dc), (yd, dd))):
            yv = y_ref[...]
            gate = _sigmoid(lg_ref[:, n * D_MODEL:(n + 1) * D_MODEL])
            pn = _dot(yv, wb_ref[n])
            dlg_ref[:, n * D_MODEL:(n + 1) * D_MODEL] = (d_merged * pn * gate * (1.0 - gate)).astype(BF16)
            dpn = (d_merged * gate).astype(BF16)
            dy_ref[...] = _dot_nt(dpn, wb_ref[n])
            dwb_ref[n] += _dot_tn(yv, dpn)
        finish()

    row = lambda w: pl.BlockSpec((tm, w), lambda i: (i, 0))
    wb_spec = pl.BlockSpec((N_HEADS, BR, D_MODEL), lambda i: (0, 0, 0))
    wo_spec = pl.BlockSpec((D_MODEL, D_MODEL), lambda i: (0, 0))
    dy_shape = jax.ShapeDtypeStruct((s, BR), F32)
    outs = pl.pallas_call(
        body, name="merge_bwd", grid=(steps,),
        in_specs=[row(D_MODEL), pl.BlockSpec((tm, MRG_COLS), lambda i: (i, MRG0 // MRG_COLS)), row(BR), row(BR), row(BR), row(BR), row(D_MODEL),
                  wb_spec, wo_spec] + r_in_specs,
        out_specs=[pl.BlockSpec((tm, MRG_COLS), lambda i: (i, MRG0 // MRG_COLS)), row(BR), row(BR), row(BR), row(BR), wo_spec, wb_spec] + r_out_specs,
        out_shape=[jax.ShapeDtypeStruct((s, PCOLS), BF16), dy_shape, dy_shape, dy_shape, dy_shape,
                   jax.ShapeDtypeStruct((D_MODEL, D_MODEL), F32), jax.ShapeDtypeStruct((N_HEADS, BR, D_MODEL), F32)] + r_shapes,
        scratch_shapes=r_scratch, input_output_aliases=r_alias,
        compiler_params=_params("arbitrary"))(dxn, proj, *ys, merged, wb, wo, *r_in)
    return outs[0], outs[1:5], outs[5], outs[6], outs[7:]


def _loss_head(x, target, g):
    s = x.shape[0]
    tm = min(512, s)

    def body(x_ref, t_ref, g_ref, loss_ref, dx_ref, dg_ref):
        @pl.when(pl.program_id(0) == 0)
        def _():
            loss_ref[...] = jnp.zeros_like(loss_ref)
            dg_ref[...] = jnp.zeros_like(dg_ref)

        xv = x_ref[...]
        gv = g_ref[...]
        r = lax.rsqrt(jnp.mean(xv * xv, axis=-1, keepdims=True) + EPS)
        xn = xv * r
        err = xn * gv - t_ref[...]
        loss_ref[...] += 0.5 * jnp.sum(jnp.mean(err * err, axis=-1, keepdims=True))
        dy = err * (1.0 / D_MODEL)
        dg_ref[...] += _part8(dy * xn)
        gy = dy * gv
        dx_ref[...] = r * (gy - xn * jnp.mean(xn * gy, axis=-1, keepdims=True))

    row = pl.BlockSpec((tm, D_MODEL), lambda i: (i, 0))
    return pl.pallas_call(
        body, name="loss_head", grid=(s // tm,),
        in_specs=[row, row, pl.BlockSpec((1, D_MODEL), lambda i: (0, 0))],
        out_specs=[pl.BlockSpec((8, 128), lambda i: (0, 0)), row, pl.BlockSpec((8, D_MODEL), lambda i: (0, 0))],
        out_shape=[jax.ShapeDtypeStruct((8, 128), F32), jax.ShapeDtypeStruct((s, D_MODEL), F32), jax.ShapeDtypeStruct((8, D_MODEL), F32)],
        compiler_params=_params("arbitrary"))(x, target, g)


_SEGMENTS = ((0, 512, AUV0), (512, 256, AG0), (768, 768, QKV0), (1540, 256, BG0), (1796, 768, CIN0),
             (2564, 256, CG0), (2820, 512, DGLU0), (3332, 256, DG0), (3588, 4096, MRG0))


def _sgu_bias_rows(sgu_b):
    return jnp.repeat(sgu_b.T, HEAD_DIM, axis=1)


def _layer_fwd(x, p, next_blocks=None):
    proj, proj_bf, h = _inproj_fwd(x, p["norm_g"], p["w_in"], p["w_bf"])
    ya = _mix_a_fwd(proj, p["sgu_w"], _sgu_bias_rows(p["sgu_b"]), p["sgu_ln_g"], p["sgu_ln_b"])
    qe, ke, ve, ket, vet = _attn_prep_fwd(proj, proj_bf, p["f_bias"])
    ot, lse, landed = _attn_fwd(qe, ke, vet, rider=None if next_blocks is None else _gather_send_rider(next_blocks))
    yb, o, gathered = _bgate_fwd(proj, ot, rider=None if next_blocks is None else _gather_forward_rider(landed))
    yc = _mix_c_fwd(proj, p["short_conv_w"])
    yd = _mix_d_fwd(proj, p["conf_dw_w"], p["conf_dw_b"], p["conf_ln_g"], p["conf_ln_b"])
    ys = (ya, yb, yc, yd)
    x_next, merged = _merge_fwd(x, proj, ys, p["w_branch"], p["w_out"])
    saved = dict(x=x, proj=proj, proj_bf=proj_bf, h=h, ys=ys, merged=merged, qe=qe, ke=ke, ve=ve, ket=ket, lse=lse, o=o)
    return x_next, saved, gathered


def _grads_by_chip(d_win, d_wb, d_wo):
    return [jnp.stack([_w_in_shard(d_win[0], d_win[1], j) for j in range(N_CHIPS)]),
            jnp.stack([d_wb[:, :, j * BR:(j + 1) * BR].reshape(N_HEADS * BR, BR).astype(BF16) for j in range(N_CHIPS)]),
            jnp.stack([d_wo[j * BR:(j + 1) * BR].astype(BF16) for j in range(N_CHIPS)])]


def _layer_bwd(dxn, p, sv, prev_reduce=None, reduce_self=None):
    proj = sv["proj"]
    if prev_reduce is None:
        dproj, dys, d_wo, d_wb, _ = _merge_bwd(dxn, proj, sv["ys"], sv["merged"], p["w_branch"], p["w_out"])
        chip_rider = None
    else:
        pc, chip, arrays = prev_reduce
        mine, other = _row_halves(arrays, pc)
        dproj, dys, d_wo, d_wb, from_sibling = _merge_bwd(dxn, proj, sv["ys"], sv["merged"], p["w_branch"], p["w_out"],
                                                           rider=_pair_rider(other))
        halves = _add_pairs(mine, from_sibling)
        chip_rider = _chip_rider(halves)
    dproj, d_sgu_w, d_bias_rows, d_sgu_lg, d_sgu_lb = _mix_a_bwd(
        proj, dproj, dys[0], p["sgu_w"], _sgu_bias_rows(p["sgu_b"]), p["sgu_ln_g"], p["sgu_ln_b"])
    dproj, doe, dd = _bgate_bwd(proj, dproj, sv["o"], dys[1])
    dqe, dke, dve, from_chips = _attn_bwd(sv["qe"], sv["ke"], sv["ket"], sv["ve"], doe, sv["lse"], dd, rider=chip_rider)
    reduced = None
    if prev_reduce is not None:
        reduced = _sum_chips(halves, chip, from_chips)
    dproj, dproj_bf, d_fb = _attn_prep_bwd(sv["proj_bf"], dproj, dqe, dke, dve, p["f_bias"])
    dproj, d_sc = _mix_c_bwd(proj, dproj, dys[2], p["short_conv_w"])
    dproj, d_cw, d_cb, d_clg, d_clb = _mix_d_bwd(proj, dproj, dys[3], p["conf_dw_w"], p["conf_dw_b"], p["conf_ln_g"], p["conf_ln_b"])
    d_win = _inproj_bwd_w(sv["h"], dproj, dproj_bf)
    reduced_self, self_rider = None, None
    if reduce_self is not None:
        pc, chip = reduce_self
        mine, other = _row_halves(_grads_by_chip(d_win, d_wb, d_wo), pc)
        own_halves = _add_pairs(mine, _pair_exchange(other))
        self_rider = _chip_rider(own_halves)
    dx, dg8, from_chips_self = _inproj_bwd_x(dproj, p["w_in"], sv["x"], dxn, p["norm_g"], dproj_bf, p["w_bf"], rider=self_rider)
    if reduce_self is not None:
        reduced_self = _sum_chips(own_halves, chip, from_chips_self)
    grads = dict(
        norm_g=jnp.sum(dg8, axis=0), w_in=d_win, f_bias=d_fb[0, :N_HEADS], sgu_w=d_sgu_w,
        sgu_b=d_bias_rows.reshape(CHUNK, N_HEADS, HEAD_DIM).sum(axis=-1).T,
        sgu_ln_g=d_sgu_lg[0], sgu_ln_b=d_sgu_lb[0], short_conv_w=d_sc[:K_SHORT], conf_dw_w=d_cw[:K_CONF],
        conf_dw_b=d_cb[0], conf_ln_g=d_clg[0], conf_ln_b=d_clb[0], w_branch=d_wb, w_out=d_wo)
    return dx, grads, reduced, reduced_self


def _row_halves(arrays, pc):
    half = lambda a, i: lax.dynamic_slice_in_dim(a, i * (a.shape[-2] // 2), a.shape[-2] // 2, axis=a.ndim - 2)
    return [half(a, pc) for a in arrays], [half(a, 1 - pc) for a in arrays]


def _local_step(x, target, layers, final_g):
    saved = []
    for p in layers:
        x, sv, _ = _layer_fwd(x, p)
        saved.append(sv)
    loss8, dx, dfg8 = _loss_head(x, target, final_g)
    grads = [None] * len(layers)
    for l in reversed(range(len(layers))):
        dx, grads[l], _, _ = _layer_bwd(dx, layers[l], saved[l])
    return loss8[0, 0], dx, grads, jnp.sum(dfg8, axis=0)


def _gather_chips(blocks):
    n = len(blocks)

    def body(*refs):
        ins, outs, (send_sems, recv_sems) = refs[:n], refs[n:2 * n], refs[2 * n:]
        x, y, cc = _place()
        me, sibling = (x, y, cc), (x, y, 1 - cc)
        chips = _other_chips(x, y)

        def copy(a, k, chip, layer, to, src=None):
            dst = outs[a].at[2 * chip[0] + chip[1], layer]
            return pltpu.make_async_remote_copy(src_ref=dst if src is None else src, dst_ref=dst, send_sem=send_sems.at[a, k],
                                                recv_sem=recv_sems.at[a, k], device_id=to, device_id_type=MESH)

        first = [copy(a, j, (x, y), cc, (*chip, cc), src=ins[a]) for j, chip in enumerate(chips) for a in range(n)]
        for cp in first:
            cp.start()
        passed = []
        for j, chip in enumerate(chips):
            for a in range(n):
                copy(a, j, chip, cc, me).wait_recv()
                passed.append(copy(a, 3 + j, chip, cc, sibling))
                passed[-1].start()
        for j, chip in enumerate(chips):
            for a in range(n):
                copy(a, 3 + j, chip, 1 - cc, me).wait_recv()
        for cp in first + passed:
            cp.wait_send()

    return pl.pallas_call(
        body, name="gather_chips", in_specs=[ANY] * n, out_specs=[ANY] * n,
        out_shape=[jax.ShapeDtypeStruct((N_CHIPS, 2) + b.shape, b.dtype) for b in blocks],
        scratch_shapes=[pltpu.SemaphoreType.DMA((n, 6)), pltpu.SemaphoreType.DMA((n, 6))])(*blocks)


def _pair_exchange(arrays):
    n = len(arrays)

    def body(*refs):
        ins, outs, (send_sems, recv_sems) = refs[:n], refs[n:2 * n], refs[2 * n:]
        x, y, cc = _place()
        cps = [pltpu.make_async_remote_copy(src_ref=ins[a], dst_ref=outs[a], send_sem=send_sems.at[a], recv_sem=recv_sems.at[a],
                                            device_id=(x, y, 1 - cc), device_id_type=MESH) for a in range(n)]
        for cp in cps:
            cp.start()
        for cp in cps:
            cp.wait()

    return pl.pallas_call(
        body, name="pair_exchange", in_specs=[ANY] * n, out_specs=[ANY] * n,
        out_shape=[jax.ShapeDtypeStruct(a.shape, a.dtype) for a in arrays],
        scratch_shapes=[pltpu.SemaphoreType.DMA((n,)), pltpu.SemaphoreType.DMA((n,))])(*arrays)


def _allreduce8(arrays):
    n = len(arrays)

    def body(*refs):
        ins, outs, recvs = refs[:n], refs[n:2 * n], refs[2 * n:3 * n]
        send_sems, recv_sems = refs[3 * n:]
        x, y, cc = _place()
        for a in range(n):
            outs[a][...] = ins[a][...]
        for k, peer in enumerate([(x, y, 1 - cc), (1 - x, y, cc), (x, 1 - y, cc)]):
            cps = [pltpu.make_async_remote_copy(src_ref=outs[a], dst_ref=recvs[a].at[k], send_sem=send_sems.at[a, k],
                                                recv_sem=recv_sems.at[a, k], device_id=peer, device_id_type=MESH) for a in range(n)]
            for cp in cps:
                cp.start()
            for cp in cps:
                cp.wait()
            for a in range(n):
                outs[a][...] = outs[a][...] + recvs[a][k]

    return pl.pallas_call(
        body, name="allreduce8", in_specs=[VMEM] * n, out_specs=[VMEM] * n,
        out_shape=[jax.ShapeDtypeStruct(a.shape, F32) for a in arrays],
        scratch_shapes=[pltpu.VMEM((3,) + a.shape, F32) for a in arrays] + [pltpu.SemaphoreType.DMA((n, 3)), pltpu.SemaphoreType.DMA((n, 3))],
        compiler_params=_params())(*arrays)


def _row_tile(rows, cols, limit_bytes=1 << 20):
    t = rows
    while t % 16 == 0 and t * cols * 4 > limit_bytes:
        t //= 2
    return t


REDUCE_STEPS = 8


def _add_pairs(xs, ys):
    n = len(xs)

    def body(*refs):
        for a in range(n):
            refs[2 * n + a][...] = (refs[a][...].astype(F32) + refs[n + a][...].astype(F32)).astype(BF16)

    specs = [pl.BlockSpec((x.shape[0], x.shape[1] // REDUCE_STEPS, x.shape[2]), lambda i: (0, i, 0)) for x in xs]
    return pl.pallas_call(body, name="add_pairs", grid=(REDUCE_STEPS,), in_specs=specs + specs, out_specs=specs,
                          out_shape=[jax.ShapeDtypeStruct(x.shape, BF16) for x in xs], compiler_params=_params("arbitrary"))(*xs, *ys)


def _sum_chips(halves, chip, recvs):
    n = len(halves)

    def body(chip_ref, *refs):
        del chip_ref
        for a in range(n):
            acc = refs[a][0].astype(F32)
            for k in range(3):
                acc = acc + refs[n + a][k].astype(F32)
            refs[2 * n + a][...] = acc

    rows = [h.shape[1] // REDUCE_STEPS for h in halves]
    own_specs = [pl.BlockSpec((1, r, h.shape[2]), lambda i, c: (c[0], i, 0)) for h, r in zip(halves, rows)]
    recv_specs = [pl.BlockSpec((3, r, h.shape[2]), lambda i, c: (0, i, 0)) for h, r in zip(halves, rows)]
    out_specs = [pl.BlockSpec((r, h.shape[2]), lambda i, c: (i, 0)) for h, r in zip(halves, rows)]
    grid_spec = pltpu.PrefetchScalarGridSpec(num_scalar_prefetch=1, grid=(REDUCE_STEPS,), in_specs=own_specs + recv_specs,
                                             out_specs=out_specs)
    return pl.pallas_call(body, name="sum_chips", grid_spec=grid_spec,
                          out_shape=[jax.ShapeDtypeStruct(h.shape[1:], F32) for h in halves],
                          compiler_params=_params("arbitrary"))(jnp.reshape(chip, (1,)).astype(jnp.int32), *halves, *recvs)


def _adamw_update(w_ref, m_ref, v_ref, g_ref, d_ref, mo_ref, vo_ref):
    gv = g_ref[...]
    mn = ADAM_B1 * m_ref[...] + (1.0 - ADAM_B1) * gv
    vn = ADAM_B2 * v_ref[...] + (1.0 - ADAM_B2) * (gv * gv)
    m_hat = mn / (1.0 - ADAM_B1 ** ADAM_STEP)
    v_hat = vn / (1.0 - ADAM_B2 ** ADAM_STEP)
    d_ref[...] = -ADAM_LR * (m_hat / (jnp.sqrt(v_hat) + ADAM_EPS) + ADAM_WD * w_ref[...])
    mo_ref[...] = mn
    vo_ref[...] = vn


def _adamw(w, m, v, g):
    r, c = w.shape
    t = _row_tile(r, c)

    def body(*refs):
        _adamw_update(*refs)

    spec = pl.BlockSpec((t, c), lambda i: (i, 0))
    shape = jax.ShapeDtypeStruct((r, c), F32)
    return pl.pallas_call(body, name="adamw", grid=(r // t,), in_specs=[spec] * 4, out_specs=[spec] * 3,
                          out_shape=[shape] * 3, compiler_params=_params("arbitrary"))(w, m, v, g)


def _adamw_small(ws, ms, vs, gs):
    n = len(ws)

    def body(*refs):
        w_refs, m_refs, v_refs, g_refs = refs[:n], refs[n:2 * n], refs[2 * n:3 * n], refs[3 * n:4 * n]
        d_refs, mo_refs, vo_refs = refs[4 * n:5 * n], refs[5 * n:6 * n], refs[6 * n:]
        for a in range(n):
            _adamw_update(w_refs[a], m_refs[a], v_refs[a], g_refs[a], d_refs[a], mo_refs[a], vo_refs[a])

    shapes = [jax.ShapeDtypeStruct(w.shape, F32) for w in ws]
    outs = pl.pallas_call(body, name="adamw_small", in_specs=[VMEM] * (4 * n), out_specs=[VMEM] * (3 * n),
                          out_shape=shapes * 3, compiler_params=_params())(*ws, *ms, *vs, *gs)
    return outs[:n], outs[n:2 * n], outs[2 * n:]


SMALL =("norm_g", "f_bias", "sgu_w", "sgu_b", "sgu_ln_g", "sgu_ln_b", "short_conv_w", "conf_dw_w", "conf_dw_b",
         "conf_ln_g", "conf_ln_b", "final_g")
WEIGHTS = ("norm_g", "w_in", "f_bias", "sgu_w", "sgu_b", "sgu_ln_g", "sgu_ln_b", "short_conv_w", "conf_dw_w",
           "conf_dw_b", "conf_ln_g", "conf_ln_b", "w_branch", "w_out", "final_g")


def _aligned_pieces():
    out, pos = [], 0
    for src, width, dst in sorted(_SEGMENTS, key=lambda t: t[2]):
        if dst > pos:
            out.append((None, 0, dst - pos))
        lo = src
        while lo < src + width:
            j = lo // SHARD_COLS
            hi = min(src + width, (j + 1) * SHARD_COLS)
            out.append((j, lo - j * SHARD_COLS, hi - lo))
            lo = hi
        pos = dst + width
    return out


def _w_in_aligned(shards):
    rows, dtype = shards[0].shape[0], shards[0].dtype
    return jnp.concatenate([jnp.zeros((rows, w), dtype) if j is None else shards[j][:, c0:c0 + w]
                            for j, c0, w in _aligned_pieces()], axis=1)


def _w_bf(shards):
    j, c0 = BF_SRC // SHARD_COLS, BF_SRC % SHARD_COLS
    assert c0 + BF_COLS <= SHARD_COLS
    return jnp.pad(shards[j][:, c0:c0 + BF_COLS], ((0, 0), (0, CHUNK - BF_COLS)))


def _w_in_shard(g, g_bf, j):
    lo_s, hi_s = j * SHARD_COLS, (j + 1) * SHARD_COLS
    parts = []
    for src, width, dst in sorted(_SEGMENTS + ((BF_SRC, BF_COLS, None),), key=lambda t: t[0]):
        lo, hi = max(src, lo_s), min(src + width, hi_s)
        if lo < hi:
            parts.append(g_bf[:, lo - src:hi - src] if dst is None else g[:, dst + lo - src:dst + hi - src])
    return jnp.concatenate(parts + [jnp.zeros((g.shape[0], SHARD_PAD - SHARD_COLS), g.dtype)], axis=1)


def kernel(x, norm_g, w_in, f_bias, sgu_w, sgu_b, sgu_ln_g, sgu_ln_b, short_conv_w, conf_dw_w, conf_dw_b, conf_ln_g, conf_ln_b, w_branch, w_out, final_g, loss_target, m_norm_g, m_w_in, m_f_bias, m_sgu_w, m_sgu_b, m_sgu_ln_g, m_sgu_ln_b, m_short_conv_w, m_conf_dw_w, m_conf_dw_b, m_conf_ln_g, m_conf_ln_b, m_w_branch, m_w_out, m_final_g, v_norm_g, v_w_in, v_f_bias, v_sgu_w, v_sgu_b, v_sgu_ln_g, v_sgu_ln_b, v_short_conv_w, v_conf_dw_w, v_conf_dw_b, v_conf_ln_g, v_conf_ln_b, v_w_branch, v_w_out, v_final_g):
    px, py, pc = _place()
    chip = 2 * px + py
    depth = w_in.shape[0]
    w = dict(norm_g=norm_g, w_in=w_in, f_bias=f_bias, sgu_w=sgu_w, sgu_b=sgu_b, sgu_ln_g=sgu_ln_g, sgu_ln_b=sgu_ln_b,
             short_conv_w=short_conv_w, conf_dw_w=conf_dw_w, conf_dw_b=conf_dw_b, conf_ln_g=conf_ln_g, conf_ln_b=conf_ln_b,
             w_branch=w_branch, w_out=w_out, final_g=final_g)
    m = dict(norm_g=m_norm_g, w_in=m_w_in, f_bias=m_f_bias, sgu_w=m_sgu_w, sgu_b=m_sgu_b, sgu_ln_g=m_sgu_ln_g,
             sgu_ln_b=m_sgu_ln_b, short_conv_w=m_short_conv_w, conf_dw_w=m_conf_dw_w, conf_dw_b=m_conf_dw_b,
             conf_ln_g=m_conf_ln_g, conf_ln_b=m_conf_ln_b, w_branch=m_w_branch, w_out=m_w_out, final_g=m_final_g)
    v = dict(norm_g=v_norm_g, w_in=v_w_in, f_bias=v_f_bias, sgu_w=v_sgu_w, sgu_b=v_sgu_b, sgu_ln_g=v_sgu_ln_g,
             sgu_ln_b=v_sgu_ln_b, short_conv_w=v_short_conv_w, conf_dw_w=v_conf_dw_w, conf_dw_b=v_conf_dw_b,
             conf_ln_g=v_conf_ln_g, conf_ln_b=v_conf_ln_b, w_branch=v_w_branch, w_out=v_w_out, final_g=v_final_g)

    local = (jnp.pad(w_in, ((0, 0), (0, 0), (0, SHARD_PAD - SHARD_COLS))).astype(BF16),
             w_branch.astype(BF16).reshape(depth, N_HEADS * BR, BR), w_out.astype(BF16))
    pick = lambda a, i: lax.dynamic_index_in_dim(a, i, 0, keepdims=False)
    flat = lambda a: a.reshape(-1, a.shape[-1])

    def my_half(l):
        return _row_halves([a[l] for a in local], pc)[0]

    def all_chips(gathered, l):
        return [lax.dynamic_update_index_in_dim(got.reshape((N_CHIPS,) + a.shape[1:]), a[l], chip, 0) for got, a in zip(gathered, local)]

    conv_ch = BR // N_CHIPS
    place = lambda a: lax.dynamic_update_slice_in_dim(jnp.zeros(a.shape[:-1] + (BR,), F32), a * (pc == 0).astype(F32),
                                                      conv_ch * chip, axis=2).reshape(-1, BR)
    short_full, conf_full = _allreduce8([place(short_conv_w), place(conf_dw_w)])
    short_full, conf_full = short_full.reshape(depth, K_SHORT, BR), conf_full.reshape(depth, K_CONF, BR)

    def layer_params(l, shards):
        wi_all, wb_all, wo_all = shards
        return dict(
            norm_g=norm_g[l][None], w_in=_w_in_aligned([wi_all[j] for j in range(N_CHIPS)]),
            w_bf=_w_bf([wi_all[j] for j in range(N_CHIPS)]),
            f_bias=jnp.pad(f_bias[l], (0, CHUNK - N_HEADS))[None],
            sgu_w=sgu_w[l], sgu_b=sgu_b[l], sgu_ln_g=sgu_ln_g[l][None], sgu_ln_b=sgu_ln_b[l][None],
            short_conv_w=short_full[l], conf_dw_w=conf_full[l], conf_dw_b=conf_dw_b[l][None],
            conf_ln_g=conf_ln_g[l][None], conf_ln_b=conf_ln_b[l][None],
            w_branch=jnp.concatenate([wb_all[j].reshape(N_HEADS, BR, BR) for j in range(N_CHIPS)], axis=2),
            w_out=jnp.concatenate([wo_all[j] for j in range(N_CHIPS)], axis=0))

    shards = all_chips(_gather_chips(my_half(0)), 0)
    params, saved, xs = [], [], x[0]
    for l in range(depth):
        params.append(layer_params(l, shards))
        nxt = my_half(l + 1) if l + 1 < depth else None
        xs, sv, gathered = _layer_fwd(xs, params[l], next_blocks=nxt)
        saved.append(sv)
        if nxt is not None:
            shards = all_chips(gathered, l + 1)
    loss8, dx, dfg8 = _loss_head(xs, loss_target[0], final_g[None])
    d_final_g = jnp.sum(dfg8, axis=0)
    loss = lax.psum(loss8[0, 0], ("x", "y", "c"))

    grads, reduced, pending = [None] * depth, [None] * depth, None
    for l in reversed(range(depth)):
        dx, grads[l], red, red_self = _layer_bwd(
            dx, params[l], saved[l], prev_reduce=None if pending is None else (pc, chip, pending),
            reduce_self=(pc, chip) if l == 0 else None)
        if pending is not None:
            reduced[l + 1] = red
        if l == 0:
            reduced[0] = red_self
        else:
            pending = _grads_by_chip(grads[l]["w_in"], grads[l]["w_branch"], grads[l]["w_out"])
    grad_x = dx
    own = [a for red in reduced for a in red]
    rows = [jnp.where(pc == 0, jnp.concatenate([a, b], axis=0), jnp.concatenate([b, a], axis=0))
            for a, b in zip(own, _pair_exchange(own))]
    gi, gb, go = [jnp.stack([rows[3 * l + k] for l in range(depth)]) for k in range(3)]
    g = dict(w_in=gi[:, :, :SHARD_COLS], w_branch=gb.reshape(depth, N_HEADS, BR, BR), w_out=go)

    two_d = lambda a: a.reshape(-1, a.shape[-1])
    at_least_2d = lambda a: a if a.ndim >= 2 else a[None]
    small_local = [jnp.stack([grads[l][n] for l in range(depth)]) for n in SMALL[:-1]] + [d_final_g]
    for n, a, b in zip(SMALL, small_local, _allreduce8([at_least_2d(a) for a in small_local])):
        g[n] = b.reshape(a.shape)
    for n in ("short_conv_w", "conf_dw_w"):
        g[n] = lax.dynamic_slice_in_dim(g[n], conv_ch * chip, conv_ch, axis=2)

    delta, new_m, new_v = {}, {}, {}
    for n in ("w_in", "w_branch", "w_out"):
        shp = w[n].shape
        d_, m_, v_ = _adamw(two_d(w[n]), two_d(m[n]), two_d(v[n]), two_d(g[n]))
        delta[n], new_m[n], new_v[n] = d_.reshape(shp), m_.reshape(shp), v_.reshape(shp)
    d_, m_, v_ = _adamw_small(*[[at_least_2d(t[n]) for n in SMALL] for t in (w, m, v, g)])
    for n, a, b, c_ in zip(SMALL, d_, m_, v_):
        delta[n], new_m[n], new_v[n] = a.reshape(w[n].shape), b.reshape(w[n].shape), c_.reshape(w[n].shape)

    return (loss, grad_x[None], *[g[n] for n in WEIGHTS], *[delta[n] for n in WEIGHTS],
            *[new_m[n] for n in WEIGHTS], *[new_v[n] for n in WEIGHTS])
```

```python
import functools
import math

import jax
import jax.numpy as jnp
from jax import lax
from jax.experimental import pallas as pl
from jax.experimental.pallas import tpu as pltpu

F32 = jnp.float32
BF16 = jnp.bfloat16

D_MODEL = 1024
BR = 256
N_HEADS = 4
HEAD_DIM = 64
CHUNK = 128
K_SHORT = 3
K_CONF = 31
EPS = 1e-6
IN_COLS = 7684
SHARD_COLS = IN_COLS // 4
SHARD_PAD = 2048
N_CHIPS = 4

MRG0, MRG_COLS = 0, 4096
QKV0, CIN0, AUV0, DGLU0 = 4096, 4864, 5632, 6144
AG0, BG0, CG0, DG0 = 6656, 6912, 7168, 7424
PCOLS = 7680
BF_SRC, BF_COLS = 1536, 4

V7X_VMEM_BYTES = 64 * 1024 * 1024
VMEM_LIMIT = V7X_VMEM_BYTES * 7 // 8

ADAM_LR, ADAM_B1, ADAM_B2, ADAM_EPS, ADAM_WD, ADAM_STEP = 0.001, 0.9, 0.999, 1e-08, 0.01, 10

MESH = pl.DeviceIdType.MESH
ANY = pl.BlockSpec(memory_space=pl.ANY)
VMEM = pl.BlockSpec(memory_space=pltpu.VMEM)

GELU_C0 = math.sqrt(2.0 / math.pi)
GELU_C1 = 0.044715


def _params(*sem):
    return pltpu.CompilerParams(dimension_semantics=sem, vmem_limit_bytes=VMEM_LIMIT)


def _sigmoid(x):
    return 0.5 * jnp.tanh(0.5 * x) + 0.5


def _silu_and_grad(x):
    s = _sigmoid(x)
    return x * s, s * (1.0 + x * (1.0 - s))


def _gelu_and_grad(z):
    z2 = z * z
    t = jnp.tanh(GELU_C0 * (z + GELU_C1 * z2 * z))
    half = 0.5 * (1.0 + t)
    return z * half, half + 0.5 * z * (1.0 - t * t) * (GELU_C0 * (1.0 + 3.0 * GELU_C1 * z2))


def _ln_fwd(v):
    mu = jnp.mean(v, axis=-1, keepdims=True)
    xc = v - mu
    rs = lax.rsqrt(jnp.mean(xc * xc, axis=-1, keepdims=True) + EPS)
    return xc * rs, rs


def _ln_bwd(d_xh, xh, rs):
    return rs * (d_xh - jnp.mean(d_xh, axis=-1, keepdims=True) - xh * jnp.mean(d_xh * xh, axis=-1, keepdims=True))


def _part8(a):
    return a.reshape(a.shape[0] // 8, 8, a.shape[1]).sum(axis=0)


def _dot(a, b):
    return jnp.dot(a, b, preferred_element_type=F32)


def _dot_nt(a, b):
    return lax.dot_general(a, b, (((1,), (1,)), ((), ())), preferred_element_type=F32)


def _dot_tn(a, b):
    return lax.dot_general(a, b, (((0,), (0,)), ((), ())), preferred_element_type=F32)


def _head_masks(dtype):
    lane = lax.broadcasted_iota(jnp.int32, (1, BR), 1) // HEAD_DIM
    return [(lane == h).astype(dtype) for h in range(N_HEADS)]


def _window(ref, col0, width):
    return ref.at[:, pl.ds(col0, width)]


def _copy_all(pairs, sems):
    cps = [pltpu.make_async_copy(s, d, sems.at[i]) for i, (s, d) in enumerate(pairs)]
    for cp in cps:
        cp.start()
    for cp in cps:
        cp.wait()


def _place():
    return lax.axis_index("x"), lax.axis_index("y"), lax.axis_index("c")


def _other_chips(x, y):
    return [(1 - x, y), (x, 1 - y), (1 - x, 1 - y)]


class _Rider:
    def __init__(self, srcs, out_shapes, per_array, make, through=False):
        self.srcs, self.out_shapes, self.per_array, self.make, self.through = list(srcs), list(out_shapes), per_array, make, through
        self.n = len(self.srcs)

    def scratch(self):
        return [pltpu.SemaphoreType.DMA((self.n, self.per_array)), pltpu.SemaphoreType.DMA((self.n, self.per_array))]


def _ride(rider, first, last, src_refs, dst_refs, sems):
    if rider is None:
        return (lambda: None), (lambda: None)
    cps = rider.make(src_refs, dst_refs, *sems)

    def guarded(cond, fn):
        if cond is True:
            fn()
        else:
            pl.when(cond)(fn)

    def start():
        guarded(first, lambda: [cp.start() for cp in cps] and None)

    def finish():
        guarded(last, lambda: [cp.wait() for cp in cps] and None)

    return start, finish


def _rider_parts(rider, n_in, n_out):
    if rider is None:
        return [], [], [], [], [], {}
    aliases = {n_in + a: n_out + a for a in range(rider.n)} if rider.through else {}
    return rider.srcs, [ANY] * rider.n, [ANY] * rider.n, rider.out_shapes, rider.scratch(), aliases


def _remote(src, dst, send_sems, recv_sems, a, k, to):
    return pltpu.make_async_remote_copy(src_ref=src, dst_ref=dst, send_sem=send_sems.at[a, k], recv_sem=recv_sems.at[a, k],
                                        device_id=to, device_id_type=MESH)


def _gather_send_rider(blocks):
    def make(srcs, dsts, ss, rs):
        x, y, cc = _place()
        return [_remote(srcs[a], dsts[a].at[2 * x + y, cc], ss, rs, a, j, (*chip, cc))
                for j, chip in enumerate(_other_chips(x, y)) for a in range(len(srcs))]
    shapes = [jax.ShapeDtypeStruct((N_CHIPS, 2) + b.shape, b.dtype) for b in blocks]
    return _Rider(blocks, shapes, 3, make)


def _gather_forward_rider(landed):
    def make(srcs, dsts, ss, rs):
        x, y, cc = _place()
        return [_remote(dsts[a].at[2 * px + py, cc], dsts[a].at[2 * px + py, cc], ss, rs, a, j, (x, y, 1 - cc))
                for j, (px, py) in enumerate(_other_chips(x, y)) for a in range(len(dsts))]
    shapes = [jax.ShapeDtypeStruct(b.shape, b.dtype) for b in landed]
    return _Rider(landed, shapes, 3, make, through=True)


def _pair_rider(arrays):
    def make(srcs, dsts, ss, rs):
        x, y, cc = _place()
        return [_remote(srcs[a], dsts[a], ss, rs, a, 0, (x, y, 1 - cc)) for a in range(len(srcs))]
    return _Rider(arrays, [jax.ShapeDtypeStruct(b.shape, b.dtype) for b in arrays], 1, make)


def _all_devices_rider(arrays):
    flips = [(fx, fy, fc) for fx in (0, 1) for fy in (0, 1) for fc in (0, 1) if fx or fy or fc]

    def make(srcs, dsts, ss, rs):
        x, y, cc = _place()
        me = 4 * x + 2 * y + cc
        cps = []
        for a in range(len(srcs)):
            for k, (fx, fy, fc) in enumerate(flips):
                peer = (1 - x if fx else x, 1 - y if fy else y, 1 - cc if fc else cc)
                cps.append(_remote(srcs[a], dsts[a].at[me], ss, rs, a, k, peer))
            cps.append(pltpu.make_async_copy(srcs[a], dsts[a].at[me], ss.at[a, len(flips)]))
        return cps
    return _Rider(arrays, [jax.ShapeDtypeStruct((8,) + b.shape, b.dtype) for b in arrays], len(flips) + 1, make)


def _chip_rider(arrays):
    def make(srcs, dsts, ss, rs):
        x, y, cc = _place()
        return [_remote(srcs[a].at[2 * px + py], dsts[a].at[k], ss, rs, a, k, (px, py, cc))
                for k, (px, py) in enumerate(_other_chips(x, y)) for a in range(len(srcs))]
    return _Rider(arrays, [jax.ShapeDtypeStruct((3,) + b.shape[1:], b.dtype) for b in arrays], 3, make)


INPROJ_TN = 1536


def _inproj_fwd(x, g, w, w_bf):
    s = x.shape[0]
    tm, tn = min(1024, s), INPROJ_TN

    def body(x_ref, g_ref, w_ref, wbf_ref, proj_ref, pbf_ref, h_ref):
        @pl.when(pl.program_id(1) == 0)
        def _():
            xv = x_ref[...]
            r = lax.rsqrt(jnp.mean(xv * xv, axis=-1, keepdims=True) + EPS)
            h_ref[...] = ((xv * r) * g_ref[...]).astype(BF16)
            pbf_ref[...] = _dot(h_ref[...], wbf_ref[...])
        proj_ref[...] = _dot(h_ref[...], w_ref[...])

    return pl.pallas_call(
        body, name="inproj_fwd", grid=(s // tm, PCOLS // tn),
        in_specs=[pl.BlockSpec((tm, D_MODEL), lambda i, j: (i, 0)), pl.BlockSpec((1, D_MODEL), lambda i, j: (0, 0)),
                  pl.BlockSpec((D_MODEL, tn), lambda i, j: (0, j)), pl.BlockSpec((D_MODEL, CHUNK), lambda i, j: (0, 0))],
        out_specs=[pl.BlockSpec((tm, tn), lambda i, j: (i, j)), pl.BlockSpec((tm, CHUNK), lambda i, j: (i, 0)),
                   pl.BlockSpec((tm, D_MODEL), lambda i, j: (i, 0))],
        out_shape=[jax.ShapeDtypeStruct((s, PCOLS), F32), jax.ShapeDtypeStruct((s, CHUNK), F32), jax.ShapeDtypeStruct((s, D_MODEL), BF16)],
        compiler_params=_params("arbitrary", "arbitrary"))(x, g, w, w_bf)


def _rms_bwd(dh, x, g):
    r = lax.rsqrt(jnp.mean(x * x, axis=-1, keepdims=True) + EPS)
    xn = x * r
    gy = dh * g
    dx = r * (gy - xn * jnp.mean(xn * gy, axis=-1, keepdims=True))
    return dx, _part8(dh * xn)


def _inproj_bwd_x(dproj, w, x, dxn, g, dproj_bf, w_bf, rider=None):
    s = x.shape[0]
    tm, tk = min(512, s), PCOLS // 2
    nk = PCOLS // tk
    ni = s // tm
    r_in, r_in_specs, r_out_specs, r_shapes, r_scratch, r_alias = _rider_parts(rider, 7, 2)
    nr = len(r_in)

    def body(*refs):
        dp_ref, w_ref, x_ref, dxn_ref, g_ref, dpbf_ref, wbf_ref = refs[:7]
        dx_ref, dg_ref = refs[7 + nr:9 + nr]
        acc_ref = refs[9 + 2 * nr]
        i, k = pl.program_id(0), pl.program_id(1)
        start, finish = _ride(rider, (i == 0) & (k == 0), (i == ni - 1) & (k == nk - 1),
                              refs[7:7 + nr], refs[9 + nr:9 + 2 * nr], refs[10 + 2 * nr:])
        start()

        @pl.when(k == 0)
        def _():
            acc_ref[...] = _dot_nt(dpbf_ref[...], wbf_ref[...])

        @pl.when((i == 0) & (k == 0))
        def _():
            dg_ref[...] = jnp.zeros_like(dg_ref)

        acc_ref[...] += _dot_nt(dp_ref[...], w_ref[...])

        @pl.when(k == nk - 1)
        def _():
            dx, dg8 = _rms_bwd(acc_ref[...], x_ref[...], g_ref[...])
            dx_ref[...] = dxn_ref[...] + dx
            dg_ref[...] += dg8

        finish()

    outs = pl.pallas_call(
        body, name="inproj_bwd_x", grid=(ni, nk),
        in_specs=[pl.BlockSpec((tm, tk), lambda i, k: (i, k)), pl.BlockSpec((D_MODEL, tk), lambda i, k: (0, k)),
                  pl.BlockSpec((tm, D_MODEL), lambda i, k: (i, 0)), pl.BlockSpec((tm, D_MODEL), lambda i, k: (i, 0)),
                  pl.BlockSpec((1, D_MODEL), lambda i, k: (0, 0)), pl.BlockSpec((tm, CHUNK), lambda i, k: (i, 0)),
                  pl.BlockSpec((D_MODEL, CHUNK), lambda i, k: (0, 0))] + r_in_specs,
        out_specs=[pl.BlockSpec((tm, D_MODEL), lambda i, k: (i, 0)), pl.BlockSpec((8, D_MODEL), lambda i, k: (0, 0))] + r_out_specs,
        out_shape=[jax.ShapeDtypeStruct((s, D_MODEL), F32), jax.ShapeDtypeStruct((8, D_MODEL), F32)] + r_shapes,
        scratch_shapes=[pltpu.VMEM((tm, D_MODEL), F32)] + r_scratch, input_output_aliases=r_alias,
        compiler_params=_params("arbitrary", "arbitrary"))(dproj, w, x, dxn, g, dproj_bf, w_bf, *r_in)
    return outs[0], outs[1], outs[2:]


def _inproj_bwd_w(h, dproj, dproj_bf):
    s = h.shape[0]
    tn, tk = INPROJ_TN, min(1024, s)
    nk = s // tk

    def body(h_ref, dp_ref, dpbf_ref, dw_ref, dwbf_ref, acc_ref, accbf_ref):
        j, k = pl.program_id(0), pl.program_id(1)

        @pl.when(k == 0)
        def _():
            acc_ref[...] = jnp.zeros_like(acc_ref)

        acc_ref[...] += _dot_tn(h_ref[...], dp_ref[...])

        @pl.when(k == nk - 1)
        def _():
            dw_ref[...] = acc_ref[...].astype(BF16)

        @pl.when((j == 0) & (k == 0))
        def _():
            accbf_ref[...] = jnp.zeros_like(accbf_ref)

        @pl.when(j == 0)
        def _():
            accbf_ref[...] += _dot_tn(h_ref[...], dpbf_ref[...])

        @pl.when((j == 0) & (k == nk - 1))
        def _():
            dwbf_ref[...] = accbf_ref[...].astype(BF16)

    return pl.pallas_call(
        body, name="inproj_bwd_w", grid=(PCOLS // tn, nk),
        in_specs=[pl.BlockSpec((tk, D_MODEL), lambda j, k: (k, 0)), pl.BlockSpec((tk, tn), lambda j, k: (k, j)),
                  pl.BlockSpec((tk, CHUNK), lambda j, k: (k, 0))],
        out_specs=[pl.BlockSpec((D_MODEL, tn), lambda j, k: (0, j)), pl.BlockSpec((D_MODEL, CHUNK), lambda j, k: (0, 0))],
        out_shape=[jax.ShapeDtypeStruct((D_MODEL, PCOLS), BF16), jax.ShapeDtypeStruct((D_MODEL, CHUNK), BF16)],
        scratch_shapes=[pltpu.VMEM((D_MODEL, tn), F32), pltpu.VMEM((D_MODEL, CHUNK), F32)],
        compiler_params=_params("arbitrary", "arbitrary"))(h, dproj, dproj_bf)


def _mix_a_chunk(uvp, agp, wm_ref, bias, lg, lb):
    u, du = _gelu_and_grad(uvp[:, :BR])
    v, dv = _gelu_and_grad(uvp[:, BR:])
    xh, rs = _ln_fwd(v)
    vnb = (xh * lg + lb).astype(BF16)
    masks = _head_masks(BF16)
    mixed = bias
    for h in range(N_HEADS):
        mixed = mixed + _dot(wm_ref[h], vnb * masks[h])
    sg, dsg = _silu_and_grad(agp)
    return u, du, dv, xh, rs, vnb, masks, mixed, sg, dsg


def _store_masked_sgu(sw_ref, wm_ref):
    row = lax.broadcasted_iota(jnp.int32, (CHUNK, CHUNK), 0)
    col = lax.broadcasted_iota(jnp.int32, (CHUNK, CHUNK), 1)
    for h in range(N_HEADS):
        wm_ref[h] = jnp.where(row >= col, sw_ref[h], 0.0).astype(BF16)


def _mix_a_fwd(proj, sgu_w, bias, lg, lb):
    s = proj.shape[0]

    def body(proj_ref, sw_ref, bias_ref, lg_ref, lb_ref, y_ref, uv_buf, ag_buf, wm_ref, sems):
        _copy_all([(_window(proj_ref, AUV0, 2 * BR), uv_buf), (_window(proj_ref, AG0, BR), ag_buf)], sems)
        _store_masked_sgu(sw_ref, wm_ref)

        def chunk(i, c):
            rows = pl.ds(pl.multiple_of(i * CHUNK, CHUNK), CHUNK)
            u, _, _, _, _, _, _, mixed, sg, _ = _mix_a_chunk(uv_buf[rows, :], ag_buf[rows, :], wm_ref, bias_ref[...],
                                                            lg_ref[...], lb_ref[...])
            y_ref[rows, :] = (u * mixed * sg).astype(BF16)
            return c

        lax.fori_loop(0, s // CHUNK, chunk, 0)

    return pl.pallas_call(
        body, name="mix_a_fwd", in_specs=[ANY, VMEM, VMEM, VMEM, VMEM], out_specs=VMEM,
        out_shape=jax.ShapeDtypeStruct((s, BR), BF16),
        scratch_shapes=[pltpu.VMEM((s, 2 * BR), F32), pltpu.VMEM((s, BR), F32), pltpu.VMEM((N_HEADS, CHUNK, CHUNK), BF16),
                        pltpu.SemaphoreType.DMA((2,))],
        compiler_params=_params())(proj, sgu_w, bias, lg, lb)


def _mix_a_bwd(proj, dproj, dy, sgu_w, bias, lg, lb):
    s = proj.shape[0]

    def body(proj_ref, dproj_in, dy_ref, sw_ref, bias_ref, lg_ref, lb_ref,
             dproj_ref, dsw_ref, dbias_ref, dlg_ref, dlb_ref,
             uv_buf, ag_buf, duv_buf, dag_buf, wm_ref, acc_lg, acc_lb, sems):
        del dproj_in
        _copy_all([(_window(proj_ref, AUV0, 2 * BR), uv_buf), (_window(proj_ref, AG0, BR), ag_buf)], sems)
        _store_masked_sgu(sw_ref, wm_ref)
        dsw_ref[...] = jnp.zeros_like(dsw_ref)
        dbias_ref[...] = jnp.zeros_like(dbias_ref)
        acc_lg[...] = jnp.zeros_like(acc_lg)
        acc_lb[...] = jnp.zeros_like(acc_lb)

        def chunk(i, c):
            rows = pl.ds(pl.multiple_of(i * CHUNK, CHUNK), CHUNK)
            lg_v = lg_ref[...]
            u, du, dv, xh, rs, vnb, masks, mixed, sg, dsg = _mix_a_chunk(
                uv_buf[rows, :], ag_buf[rows, :], wm_ref, bias_ref[...], lg_v, lb_ref[...])
            dyv = dy_ref[rows, :]
            t1 = dyv * sg
            d_u = t1 * mixed
            d_mixed = t1 * u
            d_ag = dyv * u * mixed * dsg
            dbias_ref[...] += d_mixed
            dmb = d_mixed.astype(BF16)
            d_vn = jnp.zeros((CHUNK, BR), F32)
            for h in range(N_HEADS):
                dm_h = dmb * masks[h]
                dsw_ref[h] += _dot_nt(dm_h, vnb)
                d_vn = d_vn + _dot_tn(wm_ref[h], dm_h)
            acc_lg[...] += _part8(d_vn * xh)
            acc_lb[...] += _part8(d_vn)
            d_v = _ln_bwd(d_vn * lg_v, xh, rs)
            duv_buf[rows, :] = jnp.concatenate([d_u * du, d_v * dv], axis=1).astype(BF16)
            dag_buf[rows, :] = d_ag.astype(BF16)
            return c

        lax.fori_loop(0, s // CHUNK, chunk, 0)
        row = lax.broadcasted_iota(jnp.int32, (CHUNK, CHUNK), 0)
        col = lax.broadcasted_iota(jnp.int32, (CHUNK, CHUNK), 1)
        for h in range(N_HEADS):
            dsw_ref[h] = jnp.where(row >= col, dsw_ref[h], 0.0)
        dlg_ref[...] = jnp.sum(acc_lg[...], axis=0, keepdims=True)
        dlb_ref[...] = jnp.sum(acc_lb[...], axis=0, keepdims=True)
        _copy_all([(duv_buf, _window(dproj_ref, AUV0, 2 * BR)), (dag_buf, _window(dproj_ref, AG0, BR))], sems)

    return pl.pallas_call(
        body, name="mix_a_bwd", in_specs=[ANY, ANY, VMEM, VMEM, VMEM, VMEM, VMEM],
        out_specs=[ANY, VMEM, VMEM, VMEM, VMEM],
        out_shape=[jax.ShapeDtypeStruct(dproj.shape, dproj.dtype), jax.ShapeDtypeStruct((N_HEADS, CHUNK, CHUNK), F32),
                   jax.ShapeDtypeStruct((CHUNK, BR), F32), jax.ShapeDtypeStruct((1, BR), F32), jax.ShapeDtypeStruct((1, BR), F32)],
        scratch_shapes=[pltpu.VMEM((s, 2 * BR), F32), pltpu.VMEM((s, BR), F32), pltpu.VMEM((s, 2 * BR), BF16),
                        pltpu.VMEM((s, BR), BF16), pltpu.VMEM((N_HEADS, CHUNK, CHUNK), BF16),
                        pltpu.VMEM((8, BR), F32), pltpu.VMEM((8, BR), F32), pltpu.SemaphoreType.DMA((2,))],
        input_output_aliases={1: 0}, compiler_params=_params())(proj, dproj, dy, sgu_w, bias, lg, lb)


def _tri_ones(n, upper):
    row = lax.broadcasted_iota(jnp.int32, (n, n), 0)
    col = lax.broadcasted_iota(jnp.int32, (n, n), 1)
    return ((row <= col) if upper else (row >= col)).astype(BF16)


def _split3(c):
    hi = c.astype(BF16)
    r1 = c - hi.astype(F32)
    mid = r1.astype(BF16)
    lo = (r1 - mid.astype(F32)).astype(BF16)
    return [hi, mid, lo]


def _tri_sum(tri, a):
    parts = _split3(a)
    return _dot(tri, parts[0]) + _dot(tri, parts[1]) + _dot(tri, parts[2])


EXT = 2 * HEAD_DIM
LANE_CQ = HEAD_DIM
LANE_CK = HEAD_DIM + 3


def _to_head(h):
    r = lax.broadcasted_iota(jnp.int32, (BR, EXT), 0)
    c = lax.broadcasted_iota(jnp.int32, (BR, EXT), 1)
    return ((r == c + h * HEAD_DIM) & (c < HEAD_DIM)).astype(BF16)


def _from_head(h):
    r = lax.broadcasted_iota(jnp.int32, (EXT, BR), 0)
    c = lax.broadcasted_iota(jnp.int32, (EXT, BR), 1)
    return ((c == r + h * HEAD_DIM) & (r < HEAD_DIM)).astype(BF16)


def _attn_prep_fwd(proj, proj_bf, f_bias):
    s = proj.shape[0]
    scale = 1.0 / math.sqrt(HEAD_DIM)

    def body(proj_ref, z_buf, fb_ref, qe_ref, ke_ref, ve_ref, ket_ref, vet_ref, qkv_buf, sems):
        _copy_all([(_window(proj_ref, QKV0, 3 * BR), qkv_buf)], sems)
        tri = _tri_ones(CHUNK, upper=False)
        lane = lax.broadcasted_iota(jnp.int32, (CHUNK, EXT), 1)
        ones_q = ((lane >= LANE_CK) & (lane < LANE_CK + 3)).astype(F32)
        ones_k = ((lane >= LANE_CQ) & (lane < LANE_CQ + 3)).astype(F32)

        def chunk(i, carry):
            rows = pl.ds(pl.multiple_of(i * CHUNK, CHUNK), CHUNK)
            cum = _tri_sum(tri, jax.nn.log_sigmoid(z_buf[rows, :] + fb_ref[...])) + carry
            qb = (qkv_buf[rows, 0:BR] * scale).astype(BF16)
            kb = qkv_buf[rows, BR:2 * BR].astype(BF16)
            vb = qkv_buf[rows, 2 * BR:].astype(BF16)
            parts = [p.astype(F32) for p in _split3(cum)]
            for h in range(N_HEADS):
                sel = _to_head(h)
                dec_q, dec_k = ones_q, ones_k
                for t, part in enumerate(parts):
                    pf = part[:, h:h + 1]
                    dec_q = dec_q + jnp.where(lane == LANE_CQ + t, pf, 0.0)
                    dec_k = dec_k - jnp.where(lane == LANE_CK + t, pf, 0.0)
                qe_ref[h, rows, :] = (_dot(qb, sel) + dec_q).astype(BF16)
                kh = _dot(kb, sel) + dec_k
                vh = _dot(vb, sel)
                ke_ref[h, rows, :] = kh.astype(BF16)
                ve_ref[h, rows, :] = vh.astype(BF16)
                ket_ref[h, :, rows] = kh.T.astype(BF16)
                vet_ref[h, :, rows] = vh.T.astype(BF16)
            return cum[CHUNK - 1:CHUNK, :]

        lax.fori_loop(0, s // CHUNK, chunk, jnp.zeros((1, CHUNK), F32))

    shape = jax.ShapeDtypeStruct((N_HEADS, s, EXT), BF16)
    shape_t = jax.ShapeDtypeStruct((N_HEADS, EXT, s), BF16)
    return pl.pallas_call(
        body, name="attn_prep_fwd", in_specs=[ANY, VMEM, VMEM], out_specs=[VMEM] * 5, out_shape=[shape, shape, shape, shape_t, shape_t],
        scratch_shapes=[pltpu.VMEM((s, 3 * BR), F32), pltpu.SemaphoreType.DMA((1,))],
        compiler_params=_params())(proj, proj_bf, f_bias)


def _attn_prep_bwd(proj_bf, dproj, dqe, dke, dve, f_bias):
    s = proj_bf.shape[0]
    scale = 1.0 / math.sqrt(HEAD_DIM)

    def body(z_buf, dproj_in, dq_ref, dk_ref, dv_ref, fb_ref, dproj_ref, dz_buf, dfb_ref, dqkv_buf, sems):
        del dproj_in
        tri = _tri_ones(CHUNK, upper=True)
        lane = lax.broadcasted_iota(jnp.int32, (CHUNK, CHUNK), 1)
        n = s // CHUNK

        def chunk(t, carry):
            suffix, acc = carry
            i = n - 1 - t
            rows = pl.ds(pl.multiple_of(i * CHUNK, CHUNK), CHUNK)
            dq = jnp.zeros((CHUNK, BR), F32)
            dk = jnp.zeros((CHUNK, BR), F32)
            dv = jnp.zeros((CHUNK, BR), F32)
            dcum = jnp.zeros((CHUNK, CHUNK), F32)
            for h in range(N_HEADS):
                back = _from_head(h)
                dqh = dq_ref[h, :, rows].T
                dkh = dk_ref[h, rows, :]
                dq = dq + _dot((dqh * scale).astype(BF16), back)
                dk = dk + _dot(dkh.astype(BF16), back)
                dv = dv + _dot(dv_ref[h, rows, :].astype(BF16), back)
                dcum = dcum + jnp.where(lane == h, dqh[:, LANE_CQ:LANE_CQ + 1] - dkh[:, LANE_CK:LANE_CK + 1], 0.0)
            dqkv_buf[rows, 0:BR] = dq.astype(BF16)
            dqkv_buf[rows, BR:2 * BR] = dk.astype(BF16)
            dqkv_buf[rows, 2 * BR:] = dv.astype(BF16)
            dlf = _tri_sum(tri, dcum) + suffix
            dz = dlf * _sigmoid(-(z_buf[rows, :] + fb_ref[...]))
            dz_buf[rows, :] = dz.astype(BF16)
            return dlf[0:1, :], acc + _part8(dz)

        _, acc = lax.fori_loop(0, n, chunk, (jnp.zeros((1, CHUNK), F32), jnp.zeros((8, CHUNK), F32)))
        dfb_ref[...] = jnp.sum(acc, axis=0, keepdims=True)
        _copy_all([(dqkv_buf, _window(dproj_ref, QKV0, 3 * BR))], sems)

    return pl.pallas_call(
        body, name="attn_prep_bwd", in_specs=[VMEM, ANY, VMEM, VMEM, VMEM, VMEM], out_specs=[ANY, VMEM, VMEM],
        out_shape=[jax.ShapeDtypeStruct(dproj.shape, dproj.dtype), jax.ShapeDtypeStruct((s, CHUNK), BF16),
                   jax.ShapeDtypeStruct((1, CHUNK), F32)],
        scratch_shapes=[pltpu.VMEM((s, 3 * BR), BF16), pltpu.SemaphoreType.DMA((1,))],
        input_output_aliases={1: 0}, compiler_params=_params())(proj_bf, dproj, dqe, dke, dve, f_bias)


ATT_TQ = 256
ATT_FWD_GROUP = 8
ATT_BWD_GROUP = 4
NEG_BIG = -1e30


def _causal_t(q0, k0, tk, tq):
    kpos = k0 + lax.broadcasted_iota(jnp.int32, (tk, tq), 0)
    qpos = q0 + lax.broadcasted_iota(jnp.int32, (tk, tq), 1)
    return kpos <= qpos


STAT_ROWS = 8


def _attn_fwd(qe, ke, vet, rider=None):
    nh, s, w = qe.shape
    tq = min(ATT_TQ, s)
    tk = min(ATT_FWD_GROUP * tq, s)
    per = tk // tq
    nq = s // tq
    r_in, r_in_specs, r_out_specs, r_shapes, r_scratch, r_alias = _rider_parts(rider, 3, 2)
    nr = len(r_in)

    def body(*refs):
        q_ref, k_ref, vt_ref = refs[:3]
        ot_ref, lse_ref = refs[3 + nr:5 + nr]
        h, i = pl.program_id(0), pl.program_id(1)
        start, finish = _ride(rider, (h == 0) & (i == 0), (h == nh - 1) & (i == nq - 1),
                              refs[3:3 + nr], refs[5 + nr:5 + 2 * nr], refs[5 + 2 * nr:])
        start()
        q = q_ref[0]

        def step(k0, carry, width, masked):
            m, l, acc = carry
            ks = pl.ds(pl.multiple_of(k0, tq), width)
            st = _dot_nt(k_ref[0, ks, :], q)
            if masked:
                st = jnp.where(_causal_t(i * tq, k0, width, tq), st, NEG_BIG)
            m_new = jnp.maximum(m, jnp.max(st, axis=0, keepdims=True))
            alpha = jnp.exp(m - m_new)
            pt = jnp.exp(st - m_new)
            l = alpha * l + jnp.sum(pt, axis=0, keepdims=True)
            acc = alpha * acc + _dot(vt_ref[0, :, ks], pt.astype(BF16))
            return m_new, l, acc

        full = i // per
        init = (jnp.full((1, tq), NEG_BIG, F32), jnp.zeros((1, tq), F32), jnp.zeros((w, tq), F32))
        carry = lax.fori_loop(0, full, lambda j, c: step(j * tk, c, tk, False), init)
        m, l, acc = lax.switch(i % per, [functools.partial(step, width=(r + 1) * tq, masked=True) for r in range(per)],
                               full * tk, carry)
        ot_ref[0] = acc / l
        lse_ref[0] = jnp.broadcast_to(m + jnp.log(l), (STAT_ROWS, tq))
        finish()

    outs = pl.pallas_call(
        body, name="attn_fwd", grid=(nh, nq),
        in_specs=[pl.BlockSpec((1, tq, w), lambda h, i: (h, i, 0)), pl.BlockSpec((1, s, w), lambda h, i: (h, 0, 0)),
                  pl.BlockSpec((1, w, s), lambda h, i: (h, 0, 0))] + r_in_specs,
        out_specs=[pl.BlockSpec((1, w, tq), lambda h, i: (h, 0, i)), pl.BlockSpec((1, STAT_ROWS, tq), lambda h, i: (h, 0, i))] + r_out_specs,
        out_shape=[jax.ShapeDtypeStruct((nh, w, s), F32), jax.ShapeDtypeStruct((nh, STAT_ROWS, s), F32)] + r_shapes,
        scratch_shapes=r_scratch, input_output_aliases=r_alias,
        compiler_params=_params("arbitrary", "arbitrary"))(qe, ke, vet, *r_in)
    return outs[0], outs[1], outs[2:]


def _attn_bwd(qe, ke, ket, ve, doe, lse, dd, rider=None):
    nh, s, w = qe.shape
    tq = min(ATT_TQ, s)
    tk = min(ATT_BWD_GROUP * tq, s)
    per = tk // tq
    nq = s // tq
    r_in, r_in_specs, r_out_specs, r_shapes, r_scratch, r_alias = _rider_parts(rider, 7, 3)
    nr = len(r_in)

    def body(*refs):
        q_ref, do_ref, lse_ref, dd_ref, k_ref, kt_ref, v_ref = refs[:7]
        dqt_ref, dk_ref, dv_ref = refs[7 + nr:10 + nr]
        h, i = pl.program_id(0), pl.program_id(1)
        start, finish = _ride(rider, (h == 0) & (i == 0), (h == nh - 1) & (i == nq - 1),
                              refs[7:7 + nr], refs[10 + nr:10 + 2 * nr], refs[10 + 2 * nr:])
        start()

        @pl.when(i == 0)
        def _():
            dk_ref[...] = jnp.zeros_like(dk_ref)
            dv_ref[...] = jnp.zeros_like(dv_ref)

        q = q_ref[0]
        do = do_ref[0]
        lse_row = lse_ref[0, 0:1, :]
        dd_row = dd_ref[0, 0:1, :]

        def step(k0, dqt, width, masked):
            ks = pl.ds(pl.multiple_of(k0, tq), width)
            st = _dot_nt(k_ref[0, ks, :], q)
            pt = jnp.exp(st - lse_row)
            if masked:
                pt = jnp.where(_causal_t(i * tq, k0, width, tq), pt, 0.0)
            dpt = _dot_nt(v_ref[0, ks, :], do)
            dst = (pt * (dpt - dd_row)).astype(BF16)
            dv_ref[0, ks, :] += _dot(pt.astype(BF16), do)
            dk_ref[0, ks, :] += _dot(dst, q)
            return dqt + _dot(kt_ref[0, :, ks], dst)

        full = i // per
        dqt = lax.fori_loop(0, full, lambda j, c: step(j * tk, c, tk, False), jnp.zeros((w, tq), F32))
        dqt_ref[0] = lax.switch(i % per, [functools.partial(step, width=(r + 1) * tq, masked=True) for r in range(per)],
                                full * tk, dqt)
        finish()

    qblk = pl.BlockSpec((1, tq, w), lambda h, i: (h, i, 0))
    stat = pl.BlockSpec((1, STAT_ROWS, tq), lambda h, i: (h, 0, i))
    whole = pl.BlockSpec((1, s, w), lambda h, i: (h, 0, 0))
    whole_t = pl.BlockSpec((1, w, s), lambda h, i: (h, 0, 0))
    outs = pl.pallas_call(
        body, name="attn_bwd", grid=(nh, nq),
        in_specs=[qblk, qblk, stat, stat, whole, whole_t, whole] + r_in_specs,
        out_specs=[pl.BlockSpec((1, w, tq), lambda h, i: (h, 0, i)), whole, whole] + r_out_specs,
        out_shape=[jax.ShapeDtypeStruct((nh, w, s), F32), jax.ShapeDtypeStruct((nh, s, w), F32),
                   jax.ShapeDtypeStruct((nh, s, w), F32)] + r_shapes,
        scratch_shapes=r_scratch, input_output_aliases=r_alias,
        compiler_params=_params("arbitrary", "arbitrary"))(qe, doe, lse, dd, ke, ket, ve, *r_in)
    return outs[0], outs[1], outs[2], outs[3:]


def _bgate_fwd(proj, ot, rider=None):
    s = proj.shape[0]
    r_in, r_in_specs, r_out_specs, r_shapes, r_scratch, r_alias = _rider_parts(rider, 2, 2)
    nr = len(r_in)

    def body(*refs):
        proj_ref, ot_ref = refs[:2]
        y_ref, o_ref = refs[2 + nr:4 + nr]
        g_buf, sems = refs[4 + 2 * nr:6 + 2 * nr]
        start, finish = _ride(rider, True, True, refs[2:2 + nr], refs[4 + nr:4 + 2 * nr], refs[6 + 2 * nr:])
        start()
        _copy_all([(_window(proj_ref, BG0, BR), g_buf)], sems)

        def chunk(i, c):
            rows = pl.ds(pl.multiple_of(i * CHUNK, CHUNK), CHUNK)
            o = jnp.zeros((CHUNK, BR), F32)
            for h in range(N_HEADS):
                back = _from_head(h)
                for part in _split3(ot_ref[h, :, rows].T):
                    o = o + _dot(part, back)
            sg, _ = _silu_and_grad(g_buf[rows, :])
            o_ref[rows, :] = o
            y_ref[rows, :] = (o * sg).astype(BF16)
            return c

        lax.fori_loop(0, s // CHUNK, chunk, 0)
        finish()

    outs = pl.pallas_call(
        body, name="bgate_fwd", in_specs=[ANY, VMEM] + r_in_specs, out_specs=[VMEM, VMEM] + r_out_specs,
        out_shape=[jax.ShapeDtypeStruct((s, BR), BF16), jax.ShapeDtypeStruct((s, BR), F32)] + r_shapes,
        scratch_shapes=[pltpu.VMEM((s, BR), F32), pltpu.SemaphoreType.DMA((1,))] + r_scratch,
        input_output_aliases=r_alias, compiler_params=_params())(proj, ot, *r_in)
    return outs[0], outs[1], outs[2:]


def _bgate_bwd(proj, dproj, o, dy):
    s = proj.shape[0]

    def body(proj_ref, dproj_in, o_ref, dy_ref, dproj_ref, do_ref, dd_ref, g_buf, dg_buf, sems):
        del dproj_in
        _copy_all([(_window(proj_ref, BG0, BR), g_buf)], sems)
        lane = lax.broadcasted_iota(jnp.int32, (BR, CHUNK), 0) // HEAD_DIM
        col = lax.broadcasted_iota(jnp.int32, (BR, CHUNK), 1)
        sel = (lane == col).astype(BF16)

        def chunk(i, c):
            rows = pl.ds(pl.multiple_of(i * CHUNK, CHUNK), CHUNK)
            sg, dsg = _silu_and_grad(g_buf[rows, :])
            dyv = dy_ref[rows, :]
            ov = o_ref[rows, :]
            do = dyv * sg
            dg_buf[rows, :] = (dyv * ov * dsg).astype(BF16)
            prod = _split3(do * ov)
            ddt = (_dot(prod[0], sel) + _dot(prod[1], sel) + _dot(prod[2], sel)).T
            dob = do.astype(BF16)
            for h in range(N_HEADS):
                do_ref[h, rows, :] = _dot(dob, _to_head(h)).astype(BF16)
                dd_ref[h, :, rows] = jnp.broadcast_to(ddt[h:h + 1, :], (STAT_ROWS, CHUNK))
            return c

        lax.fori_loop(0, s // CHUNK, chunk, 0)
        _copy_all([(dg_buf, _window(dproj_ref, BG0, BR))], sems)

    return pl.pallas_call(
        body, name="bgate_bwd", in_specs=[ANY, ANY, VMEM, VMEM], out_specs=[ANY, VMEM, VMEM],
        out_shape=[jax.ShapeDtypeStruct(dproj.shape, dproj.dtype), jax.ShapeDtypeStruct((N_HEADS, s, EXT), BF16),
                   jax.ShapeDtypeStruct((N_HEADS, STAT_ROWS, s), F32)],
        scratch_shapes=[pltpu.VMEM((s, BR), F32), pltpu.VMEM((s, BR), BF16), pltpu.SemaphoreType.DMA((1,))],
        input_output_aliases={1: 0}, compiler_params=_params())(proj, dproj, o, dy)


C_PAD = 8


def _mix_c_fwd(proj, w):
    s = proj.shape[0]

    def body(proj_ref, w_ref, y_ref, cin_buf, g_buf, z_buf, sems):
        _copy_all([(_window(proj_ref, CIN0, 3 * BR), cin_buf), (_window(proj_ref, CG0, BR), g_buf)], sems)
        z_buf[pl.ds(0, C_PAD), :] = jnp.zeros((C_PAD, BR), F32)

        def fill(i, c):
            rows = pl.ds(pl.multiple_of(i * CHUNK, CHUNK), CHUNK)
            z_buf[pl.ds(pl.multiple_of(i * CHUNK + C_PAD, 8), CHUNK), :] = cin_buf[rows, BR:2 * BR] * cin_buf[rows, 2 * BR:]
            return c

        lax.fori_loop(0, s // CHUNK, fill, 0)

        def chunk(i, c):
            r0 = pl.multiple_of(i * CHUNK, CHUNK)
            rows = pl.ds(r0, CHUNK)
            ze = z_buf[pl.ds(r0, CHUNK + C_PAD), :]
            conv = jnp.zeros((CHUNK, BR), F32)
            for k in range(K_SHORT):
                off = C_PAD - (K_SHORT - 1) + k
                conv = conv + w_ref[k:k + 1, :] * ze[off:off + CHUNK]
            sg, _ = _silu_and_grad(g_buf[rows, :])
            y_ref[rows, :] = (cin_buf[rows, 0:BR] * conv * sg).astype(BF16)
            return c

        lax.fori_loop(0, s // CHUNK, chunk, 0)

    return pl.pallas_call(
        body, name="mix_c_fwd", in_specs=[ANY, VMEM], out_specs=VMEM, out_shape=jax.ShapeDtypeStruct((s, BR), BF16),
        scratch_shapes=[pltpu.VMEM((s, 3 * BR), F32), pltpu.VMEM((s, BR), F32), pltpu.VMEM((s + C_PAD, BR), F32),
                        pltpu.SemaphoreType.DMA((2,))],
        compiler_params=_params())(proj, w)


def _mix_c_bwd(proj, dproj, dy, w):
    s = proj.shape[0]

    def body(proj_ref, dproj_in, dy_ref, w_ref, dproj_ref, dw_ref, cin_buf, g_buf, z_buf, dc_buf, dcin_buf, dg_buf, acc, sems):
        del dproj_in
        _copy_all([(_window(proj_ref, CIN0, 3 * BR), cin_buf), (_window(proj_ref, CG0, BR), g_buf)], sems)
        z_buf[pl.ds(0, C_PAD), :] = jnp.zeros((C_PAD, BR), F32)
        dc_buf[pl.ds(s, C_PAD), :] = jnp.zeros((C_PAD, BR), F32)
        acc[...] = jnp.zeros_like(acc)

        def fill(i, c):
            rows = pl.ds(pl.multiple_of(i * CHUNK, CHUNK), CHUNK)
            z_buf[pl.ds(pl.multiple_of(i * CHUNK + C_PAD, 8), CHUNK), :] = cin_buf[rows, BR:2 * BR] * cin_buf[rows, 2 * BR:]
            return c

        lax.fori_loop(0, s // CHUNK, fill, 0)

        def chunk(i, c):
            r0 = pl.multiple_of(i * CHUNK, CHUNK)
            rows = pl.ds(r0, CHUNK)
            ze = z_buf[pl.ds(r0, CHUNK + C_PAD), :]
            taps = [ze[C_PAD - (K_SHORT - 1) + k:C_PAD - (K_SHORT - 1) + k + CHUNK] for k in range(K_SHORT)]
            conv = jnp.zeros((CHUNK, BR), F32)
            for k in range(K_SHORT):
                conv = conv + w_ref[k:k + 1, :] * taps[k]
            sg, dsg = _silu_and_grad(g_buf[rows, :])
            bg = cin_buf[rows, 0:BR]
            dyv = dy_ref[rows, :]
            dconv = dyv * bg * sg
            dc_buf[rows, :] = dconv
            dcin_buf[rows, 0:BR] = (dyv * conv * sg).astype(BF16)
            dg_buf[rows, :] = (dyv * bg * conv * dsg).astype(BF16)
            for k in range(K_SHORT):
                acc[k] += _part8(dconv * taps[k])
            return c

        lax.fori_loop(0, s // CHUNK, chunk, 0)

        def chunk2(i, c):
            r0 = pl.multiple_of(i * CHUNK, CHUNK)
            rows = pl.ds(r0, CHUNK)
            de = dc_buf[pl.ds(r0, CHUNK + C_PAD), :]
            dz = jnp.zeros((CHUNK, BR), F32)
            for k in range(K_SHORT):
                off = K_SHORT - 1 - k
                dz = dz + w_ref[k:k + 1, :] * de[off:off + CHUNK]
            dcin_buf[rows, BR:2 * BR] = (dz * cin_buf[rows, 2 * BR:]).astype(BF16)
            dcin_buf[rows, 2 * BR:] = (dz * cin_buf[rows, BR:2 * BR]).astype(BF16)
            return c

        lax.fori_loop(0, s // CHUNK, chunk2, 0)
        dw_ref[...] = jnp.zeros_like(dw_ref)
        for k in range(K_SHORT):
            dw_ref[k:k + 1, :] = jnp.sum(acc[k], axis=0, keepdims=True)
        _copy_all([(dcin_buf, _window(dproj_ref, CIN0, 3 * BR)), (dg_buf, _window(dproj_ref, CG0, BR))], sems)

    return pl.pallas_call(
        body, name="mix_c_bwd", in_specs=[ANY, ANY, VMEM, VMEM], out_specs=[ANY, VMEM],
        out_shape=[jax.ShapeDtypeStruct(dproj.shape, dproj.dtype), jax.ShapeDtypeStruct((8, BR), F32)],
        scratch_shapes=[pltpu.VMEM((s, 3 * BR), F32), pltpu.VMEM((s, BR), F32), pltpu.VMEM((s + C_PAD, BR), F32),
                        pltpu.VMEM((s + C_PAD, BR), F32), pltpu.VMEM((s, 3 * BR), BF16), pltpu.VMEM((s, BR), BF16),
                        pltpu.VMEM((K_SHORT, 8, BR), F32), pltpu.SemaphoreType.DMA((2,))],
        input_output_aliases={1: 0}, compiler_params=_params())(proj, dproj, dy, w)


D_PAD = 32


def _tap_windows(win, offsets, n):
    rows = win.shape[0]
    rolled, out = {}, []
    for off in offsets:
        r = off % 8
        if r not in rolled:
            rolled[r] = win if r == 0 else pltpu.roll(win, rows - r, 0)
        out.append(rolled[r][off - r:off - r + n])
    return out


def _mix_d_common(g_ref, w_ref, b_ref, lg_ref, lb_ref, hh_buf, r0):
    he = hh_buf[pl.ds(r0, CHUNK + D_PAD), :]
    taps = _tap_windows(he, [D_PAD - (K_CONF - 1) + k for k in range(K_CONF)], CHUNK)
    conv = jnp.zeros((CHUNK, BR), F32) + b_ref[...]
    for k in range(K_CONF):
        conv = conv + w_ref[k:k + 1, :] * taps[k]
    xh, rs = _ln_fwd(conv)
    sw, dsw = _silu_and_grad(xh * lg_ref[...] + lb_ref[...])
    sg, dsg = _silu_and_grad(g_ref[pl.ds(r0, CHUNK), :])
    return taps, xh, rs, sw, dsw, sg, dsg


def _mix_d_fill(cin_buf, hh_buf, s):
    hh_buf[pl.ds(0, D_PAD), :] = jnp.zeros((D_PAD, BR), F32)

    def fill(i, c):
        rows = pl.ds(pl.multiple_of(i * CHUNK, CHUNK), CHUNK)
        hh_buf[pl.ds(pl.multiple_of(i * CHUNK + D_PAD, 8), CHUNK), :] = cin_buf[rows, 0:BR] * _sigmoid(cin_buf[rows, BR:])
        return c

    lax.fori_loop(0, s // CHUNK, fill, 0)


def _mix_d_fwd(proj, w, b, lg, lb):
    s = proj.shape[0]

    def body(proj_ref, w_ref, b_ref, lg_ref, lb_ref, y_ref, cin_buf, g_buf, hh_buf, sems):
        _copy_all([(_window(proj_ref, DGLU0, 2 * BR), cin_buf), (_window(proj_ref, DG0, BR), g_buf)], sems)
        _mix_d_fill(cin_buf, hh_buf, s)

        def chunk(i, c):
            r0 = pl.multiple_of(i * CHUNK, CHUNK)
            _, _, _, sw, _, sg, _ = _mix_d_common(g_buf, w_ref, b_ref, lg_ref, lb_ref, hh_buf, r0)
            y_ref[pl.ds(r0, CHUNK), :] = (sw * sg).astype(BF16)
            return c

        lax.fori_loop(0, s // CHUNK, chunk, 0)

    return pl.pallas_call(
        body, name="mix_d_fwd", in_specs=[ANY, VMEM, VMEM, VMEM, VMEM], out_specs=VMEM,
        out_shape=jax.ShapeDtypeStruct((s, BR), BF16),
        scratch_shapes=[pltpu.VMEM((s, 2 * BR), F32), pltpu.VMEM((s, BR), F32), pltpu.VMEM((s + D_PAD, BR), F32),
                        pltpu.SemaphoreType.DMA((2,))],
        compiler_params=_params())(proj, w, b, lg, lb)


def _mix_d_bwd(proj, dproj, dy, w, b, lg, lb):
    s = proj.shape[0]

    def body(proj_ref, dproj_in, dy_ref, w_ref, b_ref, lg_ref, lb_ref, dproj_ref, dw_ref, db_ref, dlg_ref, dlb_ref,
             cin_buf, g_buf, hh_buf, dc_buf, dcin_buf, dg_buf, acc_w, acc_s, sems):
        del dproj_in
        _copy_all([(_window(proj_ref, DGLU0, 2 * BR), cin_buf), (_window(proj_ref, DG0, BR), g_buf)], sems)
        _mix_d_fill(cin_buf, hh_buf, s)
        dc_buf[pl.ds(s, D_PAD), :] = jnp.zeros((D_PAD, BR), F32)
        acc_w[...] = jnp.zeros_like(acc_w)
        acc_s[...] = jnp.zeros_like(acc_s)

        def chunk(i, c):
            r0 = pl.multiple_of(i * CHUNK, CHUNK)
            rows = pl.ds(r0, CHUNK)
            taps, xh, rs, sw, dsw, sg, dsg = _mix_d_common(g_buf, w_ref, b_ref, lg_ref, lb_ref, hh_buf, r0)
            dyv = dy_ref[rows, :]
            dg_buf[rows, :] = (dyv * sw * dsg).astype(BF16)
            d_ln = dyv * sg * dsw
            acc_s[0] += _part8(d_ln * xh)
            acc_s[1] += _part8(d_ln)
            dc = _ln_bwd(d_ln * lg_ref[...], xh, rs)
            acc_s[2] += _part8(dc)
            dc_buf[rows, :] = dc
            for k in range(K_CONF):
                acc_w[k] += _part8(dc * taps[k])
            return c

        lax.fori_loop(0, s // CHUNK, chunk, 0)

        def chunk2(i, c):
            r0 = pl.multiple_of(i * CHUNK, CHUNK)
            rows = pl.ds(r0, CHUNK)
            de = dc_buf[pl.ds(r0, CHUNK + D_PAD), :]
            dh = jnp.zeros((CHUNK, BR), F32)
            for k, win in enumerate(_tap_windows(de, [K_CONF - 1 - k for k in range(K_CONF)], CHUNK)):
                dh = dh + w_ref[k:k + 1, :] * win
            a = cin_buf[rows, 0:BR]
            sig = _sigmoid(cin_buf[rows, BR:])
            dcin_buf[rows, 0:BR] = (dh * sig).astype(BF16)
            dcin_buf[rows, BR:] = (dh * a * sig * (1.0 - sig)).astype(BF16)
            return c

        lax.fori_loop(0, s // CHUNK, chunk2, 0)
        dw_ref[...] = jnp.zeros_like(dw_ref)
        for k in range(K_CONF):
            dw_ref[k:k + 1, :] = jnp.sum(acc_w[k], axis=0, keepdims=True)
        dlg_ref[...] = jnp.sum(acc_s[0], axis=0, keepdims=True)
        dlb_ref[...] = jnp.sum(acc_s[1], axis=0, keepdims=True)
        db_ref[...] = jnp.sum(acc_s[2], axis=0, keepdims=True)
        _copy_all([(dcin_buf, _window(dproj_ref, DGLU0, 2 * BR)), (dg_buf, _window(dproj_ref, DG0, BR))], sems)

    vec = jax.ShapeDtypeStruct((1, BR), F32)
    return pl.pallas_call(
        body, name="mix_d_bwd", in_specs=[ANY, ANY, VMEM, VMEM, VMEM, VMEM, VMEM], out_specs=[ANY, VMEM, VMEM, VMEM, VMEM],
        out_shape=[jax.ShapeDtypeStruct(dproj.shape, dproj.dtype), jax.ShapeDtypeStruct((32, BR), F32), vec, vec, vec],
        scratch_shapes=[pltpu.VMEM((s, 2 * BR), F32), pltpu.VMEM((s, BR), F32), pltpu.VMEM((s + D_PAD, BR), F32),
                        pltpu.VMEM((s + D_PAD, BR), F32), pltpu.VMEM((s, 2 * BR), BF16), pltpu.VMEM((s, BR), BF16),
                        pltpu.VMEM((K_CONF, 8, BR), F32), pltpu.VMEM((3, 8, BR), F32), pltpu.SemaphoreType.DMA((2,))],
        input_output_aliases={1: 0}, compiler_params=_params())(proj, dproj, dy, w, b, lg, lb)


MERGE_TM = 256


def _merge_fwd(x, proj, ys, wb, wo):
    s = x.shape[0]
    tm = min(MERGE_TM, s)

    def body(x_ref, lg_ref, ya, yb, yc, yd, wb_ref, wo_ref, xn_ref, mg_ref):
        merged = jnp.zeros((tm, D_MODEL), F32)
        for n, y_ref in enumerate((ya, yb, yc, yd)):
            gate = _sigmoid(lg_ref[:, n * D_MODEL:(n + 1) * D_MODEL])
            merged = merged + gate * _dot(y_ref[...], wb_ref[n])
        mb = merged.astype(BF16)
        mg_ref[...] = mb
        xn_ref[...] = x_ref[...] + _dot(mb, wo_ref[...])

    row = lambda w: pl.BlockSpec((tm, w), lambda i: (i, 0))
    return pl.pallas_call(
        body, name="merge_fwd", grid=(s // tm,),
        in_specs=[row(D_MODEL), pl.BlockSpec((tm, MRG_COLS), lambda i: (i, MRG0 // MRG_COLS)), row(BR), row(BR), row(BR), row(BR),
                  pl.BlockSpec((N_HEADS, BR, D_MODEL), lambda i: (0, 0, 0)), pl.BlockSpec((D_MODEL, D_MODEL), lambda i: (0, 0))],
        out_specs=[row(D_MODEL), row(D_MODEL)],
        out_shape=[jax.ShapeDtypeStruct((s, D_MODEL), F32), jax.ShapeDtypeStruct((s, D_MODEL), BF16)],
        compiler_params=_params("arbitrary"))(x, proj, *ys, wb, wo)


def _merge_bwd(dxn, proj, ys, merged, wb, wo, rider=None):
    s = dxn.shape[0]
    tm = min(MERGE_TM, s)
    steps = s // tm
    r_in, r_in_specs, r_out_specs, r_shapes, r_scratch, r_alias = _rider_parts(rider, 9, 7)
    nr = len(r_in)

    def body(*refs):
        dx_ref, lg_ref, ya, yb, yc, yd, mg_ref, wb_ref, wo_ref = refs[:9]
        dlg_ref, da, db, dc, dd, dwo_ref, dwb_ref = refs[9 + nr:16 + nr]
        start, finish = _ride(rider, pl.program_id(0) == 0, pl.program_id(0) == steps - 1,
                              refs[9:9 + nr], refs[16 + nr:16 + 2 * nr], refs[16 + 2 * nr:])
        start()

        @pl.when(pl.program_id(0) == 0)
        def _():
            dwo_ref[...] = jnp.zeros_like(dwo_ref)
            dwb_ref[...] = jnp.zeros_like(dwb_ref)

        dxb = dx_ref[...].astype(BF16)
        d_merged = _dot_nt(dxb, wo_ref[...])
        dwo_ref[...] += _dot_tn(mg_ref[...], dxb)
        for n, (y_ref, dy_ref) in enumerate(((ya, da), (yb, db), (yc, dc), (yd, dd))):
            yv = y_ref[...]
            gate = _sigmoid(lg_ref[:, n * D_MODEL:(n + 1) * D_MODEL])
            pn = _dot(yv, wb_ref[n])
            dlg_ref[:, n * D_MODEL:(n + 1) * D_MODEL] = (d_merged * pn * gate * (1.0 - gate)).astype(BF16)
            dpn = (d_merged * gate).astype(BF16)
            dy_ref[...] = _dot_nt(dpn, wb_ref[n])
            dwb_ref[n] += _dot_tn(yv, dpn)
        finish()

    row = lambda w: pl.BlockSpec((tm, w), lambda i: (i, 0))
    wb_spec = pl.BlockSpec((N_HEADS, BR, D_MODEL), lambda i: (0, 0, 0))
    wo_spec = pl.BlockSpec((D_MODEL, D_MODEL), lambda i: (0, 0))
    dy_shape = jax.ShapeDtypeStruct((s, BR), F32)
    outs = pl.pallas_call(
        body, name="merge_bwd", grid=(steps,),
        in_specs=[row(D_MODEL), pl.BlockSpec((tm, MRG_COLS), lambda i: (i, MRG0 // MRG_COLS)), row(BR), row(BR), row(BR), row(BR), row(D_MODEL),
                  wb_spec, wo_spec] + r_in_specs,
        out_specs=[pl.BlockSpec((tm, MRG_COLS), lambda i: (i, MRG0 // MRG_COLS)), row(BR), row(BR), row(BR), row(BR), wo_spec, wb_spec] + r_out_specs,
        out_shape=[jax.ShapeDtypeStruct((s, PCOLS), BF16), dy_shape, dy_shape, dy_shape, dy_shape,
                   jax.ShapeDtypeStruct((D_MODEL, D_MODEL), F32), jax.ShapeDtypeStruct((N_HEADS, BR, D_MODEL), F32)] + r_shapes,
        scratch_shapes=r_scratch, input_output_aliases=r_alias,
        compiler_params=_params("arbitrary"))(dxn, proj, *ys, merged, wb, wo, *r_in)
    return outs[0], outs[1:5], outs[5], outs[6], outs[7:]


def _loss_head(x, target, g):
    s = x.shape[0]
    tm = min(512, s)

    def body(x_ref, t_ref, g_ref, loss_ref, dx_ref, dg_ref):
        @pl.when(pl.program_id(0) == 0)
        def _():
            loss_ref[...] = jnp.zeros_like(loss_ref)
            dg_ref[...] = jnp.zeros_like(dg_ref)

        xv = x_ref[...]
        gv = g_ref[...]
        r = lax.rsqrt(jnp.mean(xv * xv, axis=-1, keepdims=True) + EPS)
        xn = xv * r
        err = xn * gv - t_ref[...]
        loss_ref[...] += 0.5 * jnp.sum(jnp.mean(err * err, axis=-1, keepdims=True))
        dy = err * (1.0 / D_MODEL)
        dg_ref[...] += _part8(dy * xn)
        gy = dy * gv
        dx_ref[...] = r * (gy - xn * jnp.mean(xn * gy, axis=-1, keepdims=True))

    row = pl.BlockSpec((tm, D_MODEL), lambda i: (i, 0))
    return pl.pallas_call(
        body, name="loss_head", grid=(s // tm,),
        in_specs=[row, row, pl.BlockSpec((1, D_MODEL), lambda i: (0, 0))],
        out_specs=[pl.BlockSpec((8, 128), lambda i: (0, 0)), row, pl.BlockSpec((8, D_MODEL), lambda i: (0, 0))],
        out_shape=[jax.ShapeDtypeStruct((8, 128), F32), jax.ShapeDtypeStruct((s, D_MODEL), F32), jax.ShapeDtypeStruct((8, D_MODEL), F32)],
        compiler_params=_params("arbitrary"))(x, target, g)


_SEGMENTS = ((0, 512, AUV0), (512, 256, AG0), (768, 768, QKV0), (1540, 256, BG0), (1796, 768, CIN0),
             (2564, 256, CG0), (2820, 512, DGLU0), (3332, 256, DG0), (3588, 4096, MRG0))


def _sgu_bias_rows(sgu_b):
    return jnp.repeat(sgu_b.T, HEAD_DIM, axis=1)


def _layer_fwd(x, p, next_blocks=None):
    proj, proj_bf, h = _inproj_fwd(x, p["norm_g"], p["w_in"], p["w_bf"])
    ya = _mix_a_fwd(proj, p["sgu_w"], _sgu_bias_rows(p["sgu_b"]), p["sgu_ln_g"], p["sgu_ln_b"])
    qe, ke, ve, ket, vet = _attn_prep_fwd(proj, proj_bf, p["f_bias"])
    ot, lse, landed = _attn_fwd(qe, ke, vet, rider=None if next_blocks is None else _gather_send_rider(next_blocks))
    yb, o, gathered = _bgate_fwd(proj, ot, rider=None if next_blocks is None else _gather_forward_rider(landed))
    yc = _mix_c_fwd(proj, p["short_conv_w"])
    yd = _mix_d_fwd(proj, p["conf_dw_w"], p["conf_dw_b"], p["conf_ln_g"], p["conf_ln_b"])
    ys = (ya, yb, yc, yd)
    x_next, merged = _merge_fwd(x, proj, ys, p["w_branch"], p["w_out"])
    saved = dict(x=x, proj=proj, proj_bf=proj_bf, h=h, ys=ys, merged=merged, qe=qe, ke=ke, ve=ve, ket=ket, lse=lse, o=o)
    return x_next, saved, gathered


def _grads_by_chip(d_win, d_wb, d_wo):
    return [jnp.stack([_w_in_shard(d_win[0], d_win[1], j) for j in range(N_CHIPS)]),
            jnp.stack([d_wb[:, :, j * BR:(j + 1) * BR].reshape(N_HEADS * BR, BR).astype(BF16) for j in range(N_CHIPS)]),
            jnp.stack([d_wo[j * BR:(j + 1) * BR].astype(BF16) for j in range(N_CHIPS)])]


def _layer_bwd(dxn, p, sv, prev_reduce=None, reduce_self=None):
    proj = sv["proj"]
    if prev_reduce is None:
        dproj, dys, d_wo, d_wb, _ = _merge_bwd(dxn, proj, sv["ys"], sv["merged"], p["w_branch"], p["w_out"])
        chip_rider = None
    else:
        pc, chip, arrays = prev_reduce
        mine, other = _row_halves(arrays, pc)
        dproj, dys, d_wo, d_wb, from_sibling = _merge_bwd(dxn, proj, sv["ys"], sv["merged"], p["w_branch"], p["w_out"],
                                                           rider=_pair_rider(other))
        halves = _add_pairs(mine, from_sibling)
        chip_rider = _chip_rider(halves)
    dproj, d_sgu_w, d_bias_rows, d_sgu_lg, d_sgu_lb = _mix_a_bwd(
        proj, dproj, dys[0], p["sgu_w"], _sgu_bias_rows(p["sgu_b"]), p["sgu_ln_g"], p["sgu_ln_b"])
    dproj, doe, dd = _bgate_bwd(proj, dproj, sv["o"], dys[1])
    dqe, dke, dve, from_chips = _attn_bwd(sv["qe"], sv["ke"], sv["ket"], sv["ve"], doe, sv["lse"], dd, rider=chip_rider)
    reduced = None
    if prev_reduce is not None:
        reduced = _sum_chips(halves, chip, from_chips)
    dproj, dproj_bf, d_fb = _attn_prep_bwd(sv["proj_bf"], dproj, dqe, dke, dve, p["f_bias"])
    dproj, d_sc = _mix_c_bwd(proj, dproj, dys[2], p["short_conv_w"])
    dproj, d_cw, d_cb, d_clg, d_clb = _mix_d_bwd(proj, dproj, dys[3], p["conf_dw_w"], p["conf_dw_b"], p["conf_ln_g"], p["conf_ln_b"])
    d_win = _inproj_bwd_w(sv["h"], dproj, dproj_bf)
    reduced_self, self_rider = None, None
    if reduce_self is not None:
        pc, chip = reduce_self
        mine, other = _row_halves(_grads_by_chip(d_win, d_wb, d_wo), pc)
        own_halves = _add_pairs(mine, _pair_exchange(other))
        self_rider = _chip_rider(own_halves)
    dx, dg8, from_chips_self = _inproj_bwd_x(dproj, p["w_in"], sv["x"], dxn, p["norm_g"], dproj_bf, p["w_bf"], rider=self_rider)
    if reduce_self is not None:
        reduced_self = _sum_chips(own_halves, chip, from_chips_self)
    grads = dict(
        norm_g=jnp.sum(dg8, axis=0), w_in=d_win, f_bias=d_fb[0, :N_HEADS], sgu_w=d_sgu_w,
        sgu_b=d_bias_rows.reshape(CHUNK, N_HEADS, HEAD_DIM).sum(axis=-1).T,
        sgu_ln_g=d_sgu_lg[0], sgu_ln_b=d_sgu_lb[0], short_conv_w=d_sc[:K_SHORT], conf_dw_w=d_cw[:K_CONF],
        conf_dw_b=d_cb[0], conf_ln_g=d_clg[0], conf_ln_b=d_clb[0], w_branch=d_wb, w_out=d_wo)
    return dx, grads, reduced, reduced_self


def _row_halves(arrays, pc):
    half = lambda a, i: lax.dynamic_slice_in_dim(a, i * (a.shape[-2] // 2), a.shape[-2] // 2, axis=a.ndim - 2)
    return [half(a, pc) for a in arrays], [half(a, 1 - pc) for a in arrays]


def _local_step(x, target, layers, final_g):
    saved = []
    for p in layers:
        x, sv, _ = _layer_fwd(x, p)
        saved.append(sv)
    loss8, dx, dfg8 = _loss_head(x, target, final_g)
    grads = [None] * len(layers)
    for l in reversed(range(len(layers))):
        dx, grads[l], _, _ = _layer_bwd(dx, layers[l], saved[l])
    return loss8[0, 0], dx, grads, jnp.sum(dfg8, axis=0)


def _gather_chips(blocks):
    n = len(blocks)

    def body(*refs):
        ins, outs, (send_sems, recv_sems) = refs[:n], refs[n:2 * n], refs[2 * n:]
        x, y, cc = _place()
        me, sibling = (x, y, cc), (x, y, 1 - cc)
        chips = _other_chips(x, y)

        def copy(a, k, chip, layer, to, src=None):
            dst = outs[a].at[2 * chip[0] + chip[1], layer]
            return pltpu.make_async_remote_copy(src_ref=dst if src is None else src, dst_ref=dst, send_sem=send_sems.at[a, k],
                                                recv_sem=recv_sems.at[a, k], device_id=to, device_id_type=MESH)

        first = [copy(a, j, (x, y), cc, (*chip, cc), src=ins[a]) for j, chip in enumerate(chips) for a in range(n)]
        for cp in first:
            cp.start()
        passed = []
        for j, chip in enumerate(chips):
            for a in range(n):
                copy(a, j, chip, cc, me).wait_recv()
                passed.append(copy(a, 3 + j, chip, cc, sibling))
                passed[-1].start()
        for j, chip in enumerate(chips):
            for a in range(n):
                copy(a, 3 + j, chip, 1 - cc, me).wait_recv()
        for cp in first + passed:
            cp.wait_send()

    return pl.pallas_call(
        body, name="gather_chips", in_specs=[ANY] * n, out_specs=[ANY] * n,
        out_shape=[jax.ShapeDtypeStruct((N_CHIPS, 2) + b.shape, b.dtype) for b in blocks],
        scratch_shapes=[pltpu.SemaphoreType.DMA((n, 6)), pltpu.SemaphoreType.DMA((n, 6))])(*blocks)


def _pair_exchange(arrays):
    n = len(arrays)

    def body(*refs):
        ins, outs, (send_sems, recv_sems) = refs[:n], refs[n:2 * n], refs[2 * n:]
        x, y, cc = _place()
        cps = [pltpu.make_async_remote_copy(src_ref=ins[a], dst_ref=outs[a], send_sem=send_sems.at[a], recv_sem=recv_sems.at[a],
                                            device_id=(x, y, 1 - cc), device_id_type=MESH) for a in range(n)]
        for cp in cps:
            cp.start()
        for cp in cps:
            cp.wait()

    return pl.pallas_call(
        body, name="pair_exchange", in_specs=[ANY] * n, out_specs=[ANY] * n,
        out_shape=[jax.ShapeDtypeStruct(a.shape, a.dtype) for a in arrays],
        scratch_shapes=[pltpu.SemaphoreType.DMA((n,)), pltpu.SemaphoreType.DMA((n,))])(*arrays)


def _exchange(*riders):
    counts = [r.n for r in riders]
    total = sum(counts)

    def body(*refs):
        ins, outs, sems = refs[:total], refs[total:2 * total], refs[2 * total:]
        cps, pos = [], 0
        for k, r in enumerate(riders):
            cps += r.make(ins[pos:pos + r.n], outs[pos:pos + r.n], sems[2 * k], sems[2 * k + 1])
            pos += r.n
        for cp in cps:
            cp.start()
        for cp in cps:
            cp.wait()

    outs = pl.pallas_call(
        body, name="exchange", in_specs=[ANY] * total, out_specs=[ANY] * total,
        out_shape=[sh for r in riders for sh in r.out_shapes],
        scratch_shapes=[sem for r in riders for sem in r.scratch()])(*[a for r in riders for a in r.srcs])
    split, pos = [], 0
    for c in counts:
        split.append(outs[pos:pos + c])
        pos += c
    return split


def _allreduce8(arrays):
    n = len(arrays)

    def body(*refs):
        ins, outs, recvs = refs[:n], refs[n:2 * n], refs[2 * n:3 * n]
        send_sems, recv_sems = refs[3 * n:]
        x, y, cc = _place()
        for a in range(n):
            outs[a][...] = ins[a][...]
        for k, peer in enumerate([(x, y, 1 - cc), (1 - x, y, cc), (x, 1 - y, cc)]):
            cps = [pltpu.make_async_remote_copy(src_ref=outs[a], dst_ref=recvs[a].at[k], send_sem=send_sems.at[a, k],
                                                recv_sem=recv_sems.at[a, k], device_id=peer, device_id_type=MESH) for a in range(n)]
            for cp in cps:
                cp.start()
            for cp in cps:
                cp.wait()
            for a in range(n):
                outs[a][...] = outs[a][...] + recvs[a][k]

    return pl.pallas_call(
        body, name="allreduce8", in_specs=[VMEM] * n, out_specs=[VMEM] * n,
        out_shape=[jax.ShapeDtypeStruct(a.shape, F32) for a in arrays],
        scratch_shapes=[pltpu.VMEM((3,) + a.shape, F32) for a in arrays] + [pltpu.SemaphoreType.DMA((n, 3)), pltpu.SemaphoreType.DMA((n, 3))],
        compiler_params=_params())(*arrays)


def _row_tile(rows, cols, limit_bytes=1 << 20):
    t = rows
    while t % 16 == 0 and t * cols * 4 > limit_bytes:
        t //= 2
    return t


REDUCE_STEPS = 8


def _add_pairs(xs, ys):
    n = len(xs)

    def body(*refs):
        for a in range(n):
            refs[2 * n + a][...] = (refs[a][...].astype(F32) + refs[n + a][...].astype(F32)).astype(BF16)

    specs = [pl.BlockSpec((x.shape[0], x.shape[1] // REDUCE_STEPS, x.shape[2]), lambda i: (0, i, 0)) for x in xs]
    return pl.pallas_call(body, name="add_pairs", grid=(REDUCE_STEPS,), in_specs=specs + specs, out_specs=specs,
                          out_shape=[jax.ShapeDtypeStruct(x.shape, BF16) for x in xs], compiler_params=_params("arbitrary"))(*xs, *ys)


def _sum_chips(halves, chip, recvs):
    n = len(halves)

    def body(chip_ref, *refs):
        del chip_ref
        for a in range(n):
            acc = refs[a][0].astype(F32)
            for k in range(3):
                acc = acc + refs[n + a][k].astype(F32)
            refs[2 * n + a][...] = acc

    rows = [h.shape[1] // REDUCE_STEPS for h in halves]
    own_specs = [pl.BlockSpec((1, r, h.shape[2]), lambda i, c: (c[0], i, 0)) for h, r in zip(halves, rows)]
    recv_specs = [pl.BlockSpec((3, r, h.shape[2]), lambda i, c: (0, i, 0)) for h, r in zip(halves, rows)]
    out_specs = [pl.BlockSpec((r, h.shape[2]), lambda i, c: (i, 0)) for h, r in zip(halves, rows)]
    grid_spec = pltpu.PrefetchScalarGridSpec(num_scalar_prefetch=1, grid=(REDUCE_STEPS,), in_specs=own_specs + recv_specs,
                                             out_specs=out_specs)
    return pl.pallas_call(body, name="sum_chips", grid_spec=grid_spec,
                          out_shape=[jax.ShapeDtypeStruct(h.shape[1:], F32) for h in halves],
                          compiler_params=_params("arbitrary"))(jnp.reshape(chip, (1,)).astype(jnp.int32), *halves, *recvs)


def _adamw_update(w_ref, m_ref, v_ref, g_ref, d_ref, mo_ref, vo_ref):
    gv = g_ref[...]
    mn = ADAM_B1 * m_ref[...] + (1.0 - ADAM_B1) * gv
    vn = ADAM_B2 * v_ref[...] + (1.0 - ADAM_B2) * (gv * gv)
    m_hat = mn / (1.0 - ADAM_B1 ** ADAM_STEP)
    v_hat = vn / (1.0 - ADAM_B2 ** ADAM_STEP)
    d_ref[...] = -ADAM_LR * (m_hat / (jnp.sqrt(v_hat) + ADAM_EPS) + ADAM_WD * w_ref[...])
    mo_ref[...] = mn
    vo_ref[...] = vn


def _adamw(w, m, v, g):
    r, c = w.shape
    t = _row_tile(r, c)

    def body(*refs):
        _adamw_update(*refs)

    spec = pl.BlockSpec((t, c), lambda i: (i, 0))
    shape = jax.ShapeDtypeStruct((r, c), F32)
    return pl.pallas_call(body, name="adamw", grid=(r // t,), in_specs=[spec] * 4, out_specs=[spec] * 3,
                          out_shape=[shape] * 3, compiler_params=_params("arbitrary"))(w, m, v, g)


def _sum_slots(slotted):
    n = len(slotted)

    def body(*refs):
        for a in range(n):
            acc = refs[a][0]
            for d in range(1, 8):
                acc = acc + refs[a][d]
            refs[n + a][...] = acc

    return pl.pallas_call(body, name="sum_slots", in_specs=[VMEM] * n, out_specs=[VMEM] * n,
                          out_shape=[jax.ShapeDtypeStruct(a.shape[1:], F32) for a in slotted], compiler_params=_params())(*slotted)


def _adamw_small(ws, ms, vs, gs):
    n = len(ws)

    def body(*refs):
        w_refs, m_refs, v_refs, g_refs = refs[:n], refs[n:2 * n], refs[2 * n:3 * n], refs[3 * n:4 * n]
        d_refs, mo_refs, vo_refs = refs[4 * n:5 * n], refs[5 * n:6 * n], refs[6 * n:]
        for a in range(n):
            _adamw_update(w_refs[a], m_refs[a], v_refs[a], g_refs[a], d_refs[a], mo_refs[a], vo_refs[a])

    shapes = [jax.ShapeDtypeStruct(w.shape, F32) for w in ws]
    outs = pl.pallas_call(body, name="adamw_small", in_specs=[VMEM] * (4 * n), out_specs=[VMEM] * (3 * n),
                          out_shape=shapes * 3, compiler_params=_params())(*ws, *ms, *vs, *gs)
    return outs[:n], outs[n:2 * n], outs[2 * n:]


SMALL =("norm_g", "f_bias", "sgu_w", "sgu_b", "sgu_ln_g", "sgu_ln_b", "short_conv_w", "conf_dw_w", "conf_dw_b",
         "conf_ln_g", "conf_ln_b", "final_g")
WEIGHTS = ("norm_g", "w_in", "f_bias", "sgu_w", "sgu_b", "sgu_ln_g", "sgu_ln_b", "short_conv_w", "conf_dw_w",
           "conf_dw_b", "conf_ln_g", "conf_ln_b", "w_branch", "w_out", "final_g")


def _aligned_pieces():
    out, pos = [], 0
    for src, width, dst in sorted(_SEGMENTS, key=lambda t: t[2]):
        if dst > pos:
            out.append((None, 0, dst - pos))
        lo = src
        while lo < src + width:
            j = lo // SHARD_COLS
            hi = min(src + width, (j + 1) * SHARD_COLS)
            out.append((j, lo - j * SHARD_COLS, hi - lo))
            lo = hi
        pos = dst + width
    return out


def _w_in_aligned(shards):
    rows, dtype = shards[0].shape[0], shards[0].dtype
    return jnp.concatenate([jnp.zeros((rows, w), dtype) if j is None else shards[j][:, c0:c0 + w]
                            for j, c0, w in _aligned_pieces()], axis=1)


def _w_bf(shards):
    j, c0 = BF_SRC // SHARD_COLS, BF_SRC % SHARD_COLS
    assert c0 + BF_COLS <= SHARD_COLS
    return jnp.pad(shards[j][:, c0:c0 + BF_COLS], ((0, 0), (0, CHUNK - BF_COLS)))


def _w_in_shard(g, g_bf, j):
    lo_s, hi_s = j * SHARD_COLS, (j + 1) * SHARD_COLS
    parts = []
    for src, width, dst in sorted(_SEGMENTS + ((BF_SRC, BF_COLS, None),), key=lambda t: t[0]):
        lo, hi = max(src, lo_s), min(src + width, hi_s)
        if lo < hi:
            parts.append(g_bf[:, lo - src:hi - src] if dst is None else g[:, dst + lo - src:dst + hi - src])
    return jnp.concatenate(parts + [jnp.zeros((g.shape[0], SHARD_PAD - SHARD_COLS), g.dtype)], axis=1)


def kernel(x, norm_g, w_in, f_bias, sgu_w, sgu_b, sgu_ln_g, sgu_ln_b, short_conv_w, conf_dw_w, conf_dw_b, conf_ln_g, conf_ln_b, w_branch, w_out, final_g, loss_target, m_norm_g, m_w_in, m_f_bias, m_sgu_w, m_sgu_b, m_sgu_ln_g, m_sgu_ln_b, m_short_conv_w, m_conf_dw_w, m_conf_dw_b, m_conf_ln_g, m_conf_ln_b, m_w_branch, m_w_out, m_final_g, v_norm_g, v_w_in, v_f_bias, v_sgu_w, v_sgu_b, v_sgu_ln_g, v_sgu_ln_b, v_short_conv_w, v_conf_dw_w, v_conf_dw_b, v_conf_ln_g, v_conf_ln_b, v_w_branch, v_w_out, v_final_g):
    px, py, pc = _place()
    chip = 2 * px + py
    depth = w_in.shape[0]
    w = dict(norm_g=norm_g, w_in=w_in, f_bias=f_bias, sgu_w=sgu_w, sgu_b=sgu_b, sgu_ln_g=sgu_ln_g, sgu_ln_b=sgu_ln_b,
             short_conv_w=short_conv_w, conf_dw_w=conf_dw_w, conf_dw_b=conf_dw_b, conf_ln_g=conf_ln_g, conf_ln_b=conf_ln_b,
             w_branch=w_branch, w_out=w_out, final_g=final_g)
    m = dict(norm_g=m_norm_g, w_in=m_w_in, f_bias=m_f_bias, sgu_w=m_sgu_w, sgu_b=m_sgu_b, sgu_ln_g=m_sgu_ln_g,
             sgu_ln_b=m_sgu_ln_b, short_conv_w=m_short_conv_w, conf_dw_w=m_conf_dw_w, conf_dw_b=m_conf_dw_b,
             conf_ln_g=m_conf_ln_g, conf_ln_b=m_conf_ln_b, w_branch=m_w_branch, w_out=m_w_out, final_g=m_final_g)
    v = dict(norm_g=v_norm_g, w_in=v_w_in, f_bias=v_f_bias, sgu_w=v_sgu_w, sgu_b=v_sgu_b, sgu_ln_g=v_sgu_ln_g,
             sgu_ln_b=v_sgu_ln_b, short_conv_w=v_short_conv_w, conf_dw_w=v_conf_dw_w, conf_dw_b=v_conf_dw_b,
             conf_ln_g=v_conf_ln_g, conf_ln_b=v_conf_ln_b, w_branch=v_w_branch, w_out=v_w_out, final_g=v_final_g)

    local = (jnp.pad(w_in, ((0, 0), (0, 0), (0, SHARD_PAD - SHARD_COLS))).astype(BF16),
             w_branch.astype(BF16).reshape(depth, N_HEADS * BR, BR), w_out.astype(BF16))
    pick = lambda a, i: lax.dynamic_index_in_dim(a, i, 0, keepdims=False)
    flat = lambda a: a.reshape(-1, a.shape[-1])

    def my_half(l):
        return _row_halves([a[l] for a in local], pc)[0]

    def all_chips(gathered, l):
        return [lax.dynamic_update_index_in_dim(got.reshape((N_CHIPS,) + a.shape[1:]), a[l], chip, 0) for got, a in zip(gathered, local)]

    conv_ch = BR // N_CHIPS
    place = lambda a: lax.dynamic_update_slice_in_dim(jnp.zeros(a.shape[:-1] + (BR,), F32), a * (pc == 0).astype(F32),
                                                      conv_ch * chip, axis=2).reshape(-1, BR)
    short_full, conf_full = _allreduce8([place(short_conv_w), place(conf_dw_w)])
    short_full, conf_full = short_full.reshape(depth, K_SHORT, BR), conf_full.reshape(depth, K_CONF, BR)

    def layer_params(l, shards):
        wi_all, wb_all, wo_all = shards
        return dict(
            norm_g=norm_g[l][None], w_in=_w_in_aligned([wi_all[j] for j in range(N_CHIPS)]),
            w_bf=_w_bf([wi_all[j] for j in range(N_CHIPS)]),
            f_bias=jnp.pad(f_bias[l], (0, CHUNK - N_HEADS))[None],
            sgu_w=sgu_w[l], sgu_b=sgu_b[l], sgu_ln_g=sgu_ln_g[l][None], sgu_ln_b=sgu_ln_b[l][None],
            short_conv_w=short_full[l], conf_dw_w=conf_full[l], conf_dw_b=conf_dw_b[l][None],
            conf_ln_g=conf_ln_g[l][None], conf_ln_b=conf_ln_b[l][None],
            w_branch=jnp.concatenate([wb_all[j].reshape(N_HEADS, BR, BR) for j in range(N_CHIPS)], axis=2),
            w_out=jnp.concatenate([wo_all[j] for j in range(N_CHIPS)], axis=0))

    shards = all_chips(_gather_chips(my_half(0)), 0)
    params, saved, xs = [], [], x[0]
    for l in range(depth):
        params.append(layer_params(l, shards))
        nxt = my_half(l + 1) if l + 1 < depth else None
        xs, sv, gathered = _layer_fwd(xs, params[l], next_blocks=nxt)
        saved.append(sv)
        if nxt is not None:
            shards = all_chips(gathered, l + 1)
    loss8, dx, dfg8 = _loss_head(xs, loss_target[0], final_g[None])
    d_final_g = jnp.sum(dfg8, axis=0)
    loss = lax.psum(loss8[0, 0], ("x", "y", "c"))

    grads, reduced, pending = [None] * depth, [None] * depth, None
    for l in reversed(range(depth)):
        dx, grads[l], red, red_self = _layer_bwd(
            dx, params[l], saved[l], prev_reduce=None if pending is None else (pc, chip, pending),
            reduce_self=(pc, chip) if l == 0 else None)
        if pending is not None:
            reduced[l + 1] = red
        if l == 0:
            reduced[0] = red_self
        else:
            pending = _grads_by_chip(grads[l]["w_in"], grads[l]["w_branch"], grads[l]["w_out"])
    grad_x = dx
    at_least_2d = lambda a: a if a.ndim >= 2 else a[None]
    small_local = [jnp.stack([grads[l][n] for l in range(depth)]) for n in SMALL[:-1]] + [d_final_g]
    own = [a for red in reduced for a in red]
    from_sibling, slotted = _exchange(_pair_rider(own), _all_devices_rider([at_least_2d(a) for a in small_local]))
    rows = [jnp.where(pc == 0, jnp.concatenate([a, b], axis=0), jnp.concatenate([b, a], axis=0))
            for a, b in zip(own, from_sibling)]
    gi, gb, go = [jnp.stack([rows[3 * l + k] for l in range(depth)]) for k in range(3)]
    g = dict(w_in=gi[:, :, :SHARD_COLS], w_branch=gb.reshape(depth, N_HEADS, BR, BR), w_out=go)

    two_d = lambda a: a.reshape(-1, a.shape[-1])
    delta, new_m, new_v = {}, {}, {}
    for n in ("w_in", "w_branch", "w_out"):
        shp = w[n].shape
        d_, m_, v_ = _adamw(two_d(w[n]), two_d(m[n]), two_d(v[n]), two_d(g[n]))
        delta[n], new_m[n], new_v[n] = d_.reshape(shp), m_.reshape(shp), v_.reshape(shp)

    for n, a, b in zip(SMALL, small_local, _sum_slots(slotted)):
        g[n] = b.reshape(a.shape)
    for n in ("short_conv_w", "conf_dw_w"):
        g[n] = lax.dynamic_slice_in_dim(g[n], conv_ch * chip, conv_ch, axis=2)
    d_, m_, v_ = _adamw_small(*[[at_least_2d(t[n]) for n in SMALL] for t in (w, m, v, g)])
    for n, a, b, c_ in zip(SMALL, d_, m_, v_):
        delta[n], new_m[n], new_v[n] = a.reshape(w[n].shape), b.reshape(w[n].shape), c_.reshape(w[n].shape)

    return (loss, grad_x[None], *[g[n] for n in WEIGHTS], *[delta[n] for n in WEIGHTS],
            *[new_m[n] for n in WEIGHTS], *[new_v[n] for n in WEIGHTS])
```

```python
import functools
import math

import jax
import jax.numpy as jnp
from jax import lax
from jax.experimental import pallas as pl
from jax.experimental.pallas import tpu as pltpu

F32 = jnp.float32
BF16 = jnp.bfloat16

D_MODEL = 1024
BR = 256
N_HEADS = 4
HEAD_DIM = 64
CHUNK = 128
K_SHORT = 3
K_CONF = 31
EPS = 1e-6
IN_COLS = 7684
SHARD_COLS = IN_COLS // 4
SHARD_PAD = 2048
N_CHIPS = 4

MRG0, MRG_COLS = 0, 4096
QKV0, CIN0, AUV0, DGLU0 = 4096, 4864, 5632, 6144
AG0, BG0, CG0, DG0 = 6656, 6912, 7168, 7424
PCOLS = 7680
BF_SRC, BF_COLS = 1536, 4

V7X_VMEM_BYTES = 64 * 1024 * 1024
VMEM_LIMIT = V7X_VMEM_BYTES * 7 // 8

ADAM_LR, ADAM_B1, ADAM_B2, ADAM_EPS, ADAM_WD, ADAM_STEP = 0.001, 0.9, 0.999, 1e-08, 0.01, 10

MESH = pl.DeviceIdType.MESH
ANY = pl.BlockSpec(memory_space=pl.ANY)
VMEM = pl.BlockSpec(memory_space=pltpu.VMEM)

GELU_C0 = math.sqrt(2.0 / math.pi)
GELU_C1 = 0.044715


def _params(*sem):
    return pltpu.CompilerParams(dimension_semantics=sem, vmem_limit_bytes=VMEM_LIMIT)


def _sigmoid(x):
    return 0.5 * jnp.tanh(0.5 * x) + 0.5


def _silu_and_grad(x):
    s = _sigmoid(x)
    return x * s, s * (1.0 + x * (1.0 - s))


def _gelu_and_grad(z):
    z2 = z * z
    t = jnp.tanh(GELU_C0 * (z + GELU_C1 * z2 * z))
    half = 0.5 * (1.0 + t)
    return z * half, half + 0.5 * z * (1.0 - t * t) * (GELU_C0 * (1.0 + 3.0 * GELU_C1 * z2))


def _ln_fwd(v):
    mu = jnp.mean(v, axis=-1, keepdims=True)
    xc = v - mu
    rs = lax.rsqrt(jnp.mean(xc * xc, axis=-1, keepdims=True) + EPS)
    return xc * rs, rs


def _ln_bwd(d_xh, xh, rs):
    return rs * (d_xh - jnp.mean(d_xh, axis=-1, keepdims=True) - xh * jnp.mean(d_xh * xh, axis=-1, keepdims=True))


def _part8(a):
    return a.reshape(a.shape[0] // 8, 8, a.shape[1]).sum(axis=0)


def _dot(a, b):
    return jnp.dot(a, b, preferred_element_type=F32)


def _dot_nt(a, b):
    return lax.dot_general(a, b, (((1,), (1,)), ((), ())), preferred_element_type=F32)


def _dot_tn(a, b):
    return lax.dot_general(a, b, (((0,), (0,)), ((), ())), preferred_element_type=F32)


def _head_masks(dtype):
    lane = lax.broadcasted_iota(jnp.int32, (1, BR), 1) // HEAD_DIM
    return [(lane == h).astype(dtype) for h in range(N_HEADS)]


def _window(ref, col0, width):
    return ref.at[:, pl.ds(col0, width)]


def _copy_all(pairs, sems):
    cps = [pltpu.make_async_copy(s, d, sems.at[i]) for i, (s, d) in enumerate(pairs)]
    for cp in cps:
        cp.start()
    for cp in cps:
        cp.wait()


def _place():
    return lax.axis_index("x"), lax.axis_index("y"), lax.axis_index("c")


def _other_chips(x, y):
    return [(1 - x, y), (x, 1 - y), (1 - x, 1 - y)]


class _Rider:
    def __init__(self, srcs, out_shapes, per_array, make, through=False):
        self.srcs, self.out_shapes, self.per_array, self.make, self.through = list(srcs), list(out_shapes), per_array, make, through
        self.n = len(self.srcs)

    def scratch(self):
        return [pltpu.SemaphoreType.DMA((self.n, self.per_array)), pltpu.SemaphoreType.DMA((self.n, self.per_array))]


def _ride(rider, first, last, src_refs, dst_refs, sems):
    if rider is None:
        return (lambda: None), (lambda: None)
    cps = rider.make(src_refs, dst_refs, *sems)

    def guarded(cond, fn):
        if cond is True:
            fn()
        else:
            pl.when(cond)(fn)

    def start():
        guarded(first, lambda: [cp.start() for cp in cps] and None)

    def finish():
        guarded(last, lambda: [cp.wait() for cp in cps] and None)

    return start, finish


def _rider_parts(rider, n_in, n_out):
    if rider is None:
        return [], [], [], [], [], {}
    aliases = {n_in + a: n_out + a for a in range(rider.n)} if rider.through else {}
    return rider.srcs, [ANY] * rider.n, [ANY] * rider.n, rider.out_shapes, rider.scratch(), aliases


def _remote(src, dst, send_sems, recv_sems, a, k, to):
    return pltpu.make_async_remote_copy(src_ref=src, dst_ref=dst, send_sem=send_sems.at[a, k], recv_sem=recv_sems.at[a, k],
                                        device_id=to, device_id_type=MESH)


def _gather_send_rider(blocks):
    def make(srcs, dsts, ss, rs):
        x, y, cc = _place()
        return [_remote(srcs[a], dsts[a].at[2 * x + y, cc], ss, rs, a, j, (*chip, cc))
                for j, chip in enumerate(_other_chips(x, y)) for a in range(len(srcs))]
    shapes = [jax.ShapeDtypeStruct((N_CHIPS, 2) + b.shape, b.dtype) for b in blocks]
    return _Rider(blocks, shapes, 3, make)


def _gather_forward_rider(landed):
    def make(srcs, dsts, ss, rs):
        x, y, cc = _place()
        return [_remote(dsts[a].at[2 * px + py, cc], dsts[a].at[2 * px + py, cc], ss, rs, a, j, (x, y, 1 - cc))
                for j, (px, py) in enumerate(_other_chips(x, y)) for a in range(len(dsts))]
    shapes = [jax.ShapeDtypeStruct(b.shape, b.dtype) for b in landed]
    return _Rider(landed, shapes, 3, make, through=True)


def _pair_rider(arrays):
    def make(srcs, dsts, ss, rs):
        x, y, cc = _place()
        return [_remote(srcs[a], dsts[a], ss, rs, a, 0, (x, y, 1 - cc)) for a in range(len(srcs))]
    return _Rider(arrays, [jax.ShapeDtypeStruct(b.shape, b.dtype) for b in arrays], 1, make)


def _chip_rider(arrays):
    def make(srcs, dsts, ss, rs):
        x, y, cc = _place()
        return [_remote(srcs[a].at[2 * px + py], dsts[a].at[k], ss, rs, a, k, (px, py, cc))
                for k, (px, py) in enumerate(_other_chips(x, y)) for a in range(len(srcs))]
    return _Rider(arrays, [jax.ShapeDtypeStruct((3,) + b.shape[1:], b.dtype) for b in arrays], 3, make)


INPROJ_TN = 1536


def _inproj_fwd(x, g, w, w_bf):
    s = x.shape[0]
    tm, tn = min(1024, s), INPROJ_TN

    def body(x_ref, g_ref, w_ref, wbf_ref, proj_ref, pbf_ref, h_ref):
        @pl.when(pl.program_id(1) == 0)
        def _():
            xv = x_ref[...]
            r = lax.rsqrt(jnp.mean(xv * xv, axis=-1, keepdims=True) + EPS)
            h_ref[...] = ((xv * r) * g_ref[...]).astype(BF16)
            pbf_ref[...] = _dot(h_ref[...], wbf_ref[...])
        proj_ref[...] = _dot(h_ref[...], w_ref[...])

    return pl.pallas_call(
        body, name="inproj_fwd", grid=(s // tm, PCOLS // tn),
        in_specs=[pl.BlockSpec((tm, D_MODEL), lambda i, j: (i, 0)), pl.BlockSpec((1, D_MODEL), lambda i, j: (0, 0)),
                  pl.BlockSpec((D_MODEL, tn), lambda i, j: (0, j)), pl.BlockSpec((D_MODEL, CHUNK), lambda i, j: (0, 0))],
        out_specs=[pl.BlockSpec((tm, tn), lambda i, j: (i, j)), pl.BlockSpec((tm, CHUNK), lambda i, j: (i, 0)),
                   pl.BlockSpec((tm, D_MODEL), lambda i, j: (i, 0))],
        out_shape=[jax.ShapeDtypeStruct((s, PCOLS), F32), jax.ShapeDtypeStruct((s, CHUNK), F32), jax.ShapeDtypeStruct((s, D_MODEL), BF16)],
        compiler_params=_params("arbitrary", "arbitrary"))(x, g, w, w_bf)


def _rms_bwd(dh, x, g):
    r = lax.rsqrt(jnp.mean(x * x, axis=-1, keepdims=True) + EPS)
    xn = x * r
    gy = dh * g
    dx = r * (gy - xn * jnp.mean(xn * gy, axis=-1, keepdims=True))
    return dx, _part8(dh * xn)


def _inproj_bwd_x(dproj, w, x, dxn, g, dproj_bf, w_bf, rider=None):
    s = x.shape[0]
    tm, tk = min(512, s), PCOLS // 2
    nk = PCOLS // tk
    ni = s // tm
    r_in, r_in_specs, r_out_specs, r_shapes, r_scratch, r_alias = _rider_parts(rider, 7, 2)
    nr = len(r_in)

    def body(*refs):
        dp_ref, w_ref, x_ref, dxn_ref, g_ref, dpbf_ref, wbf_ref = refs[:7]
        dx_ref, dg_ref = refs[7 + nr:9 + nr]
        acc_ref = refs[9 + 2 * nr]
        i, k = pl.program_id(0), pl.program_id(1)
        start, finish = _ride(rider, (i == 0) & (k == 0), (i == ni - 1) & (k == nk - 1),
                              refs[7:7 + nr], refs[9 + nr:9 + 2 * nr], refs[10 + 2 * nr:])
        start()

        @pl.when(k == 0)
        def _():
            acc_ref[...] = _dot_nt(dpbf_ref[...], wbf_ref[...])

        @pl.when((i == 0) & (k == 0))
        def _():
            dg_ref[...] = jnp.zeros_like(dg_ref)

        acc_ref[...] += _dot_nt(dp_ref[...], w_ref[...])

        @pl.when(k == nk - 1)
        def _():
            dx, dg8 = _rms_bwd(acc_ref[...], x_ref[...], g_ref[...])
            dx_ref[...] = dxn_ref[...] + dx
            dg_ref[...] += dg8

        finish()

    outs = pl.pallas_call(
        body, name="inproj_bwd_x", grid=(ni, nk),
        in_specs=[pl.BlockSpec((tm, tk), lambda i, k: (i, k)), pl.BlockSpec((D_MODEL, tk), lambda i, k: (0, k)),
                  pl.BlockSpec((tm, D_MODEL), lambda i, k: (i, 0)), pl.BlockSpec((tm, D_MODEL), lambda i, k: (i, 0)),
                  pl.BlockSpec((1, D_MODEL), lambda i, k: (0, 0)), pl.BlockSpec((tm, CHUNK), lambda i, k: (i, 0)),
                  pl.BlockSpec((D_MODEL, CHUNK), lambda i, k: (0, 0))] + r_in_specs,
        out_specs=[pl.BlockSpec((tm, D_MODEL), lambda i, k: (i, 0)), pl.BlockSpec((8, D_MODEL), lambda i, k: (0, 0))] + r_out_specs,
        out_shape=[jax.ShapeDtypeStruct((s, D_MODEL), F32), jax.ShapeDtypeStruct((8, D_MODEL), F32)] + r_shapes,
        scratch_shapes=[pltpu.VMEM((tm, D_MODEL), F32)] + r_scratch, input_output_aliases=r_alias,
        compiler_params=_params("arbitrary", "arbitrary"))(dproj, w, x, dxn, g, dproj_bf, w_bf, *r_in)
    return outs[0], outs[1], outs[2:]


def _inproj_bwd_w(h, dproj, dproj_bf):
    s = h.shape[0]
    tn, tk = INPROJ_TN, min(1024, s)
    nk = s // tk

    def body(h_ref, dp_ref, dpbf_ref, dw_ref, dwbf_ref, acc_ref, accbf_ref):
        j, k = pl.program_id(0), pl.program_id(1)

        @pl.when(k == 0)
        def _():
            acc_ref[...] = jnp.zeros_like(acc_ref)

        acc_ref[...] += _dot_tn(h_ref[...], dp_ref[...])

        @pl.when(k == nk - 1)
        def _():
            dw_ref[...] = acc_ref[...].astype(BF16)

        @pl.when((j == 0) & (k == 0))
        def _():
            accbf_ref[...] = jnp.zeros_like(accbf_ref)

        @pl.when(j == 0)
        def _():
            accbf_ref[...] += _dot_tn(h_ref[...], dpbf_ref[...])

        @pl.when((j == 0) & (k == nk - 1))
        def _():
            dwbf_ref[...] = accbf_ref[...].astype(BF16)

    return pl.pallas_call(
        body, name="inproj_bwd_w", grid=(PCOLS // tn, nk),
        in_specs=[pl.BlockSpec((tk, D_MODEL), lambda j, k: (k, 0)), pl.BlockSpec((tk, tn), lambda j, k: (k, j)),
                  pl.BlockSpec((tk, CHUNK), lambda j, k: (k, 0))],
        out_specs=[pl.BlockSpec((D_MODEL, tn), lambda j, k: (0, j)), pl.BlockSpec((D_MODEL, CHUNK), lambda j, k: (0, 0))],
        out_shape=[jax.ShapeDtypeStruct((D_MODEL, PCOLS), BF16), jax.ShapeDtypeStruct((D_MODEL, CHUNK), BF16)],
        scratch_shapes=[pltpu.VMEM((D_MODEL, tn), F32), pltpu.VMEM((D_MODEL, CHUNK), F32)],
        compiler_params=_params("arbitrary", "arbitrary"))(h, dproj, dproj_bf)


def _mix_a_chunk(uvp, agp, wm_ref, bias, lg, lb):
    u, du = _gelu_and_grad(uvp[:, :BR])
    v, dv = _gelu_and_grad(uvp[:, BR:])
    xh, rs = _ln_fwd(v)
    vnb = (xh * lg + lb).astype(BF16)
    masks = _head_masks(BF16)
    mixed = bias
    for h in range(N_HEADS):
        mixed = mixed + _dot(wm_ref[h], vnb * masks[h])
    sg, dsg = _silu_and_grad(agp)
    return u, du, dv, xh, rs, vnb, masks, mixed, sg, dsg


def _store_masked_sgu(sw_ref, wm_ref):
    row = lax.broadcasted_iota(jnp.int32, (CHUNK, CHUNK), 0)
    col = lax.broadcasted_iota(jnp.int32, (CHUNK, CHUNK), 1)
    for h in range(N_HEADS):
        wm_ref[h] = jnp.where(row >= col, sw_ref[h], 0.0).astype(BF16)


def _mix_a_fwd(proj, sgu_w, bias, lg, lb):
    s = proj.shape[0]

    def body(proj_ref, sw_ref, bias_ref, lg_ref, lb_ref, y_ref, uv_buf, ag_buf, wm_ref, sems):
        _copy_all([(_window(proj_ref, AUV0, 2 * BR), uv_buf), (_window(proj_ref, AG0, BR), ag_buf)], sems)
        _store_masked_sgu(sw_ref, wm_ref)

        def chunk(i, c):
            rows = pl.ds(pl.multiple_of(i * CHUNK, CHUNK), CHUNK)
            u, _, _, _, _, _, _, mixed, sg, _ = _mix_a_chunk(uv_buf[rows, :], ag_buf[rows, :], wm_ref, bias_ref[...],
                                                            lg_ref[...], lb_ref[...])
            y_ref[rows, :] = (u * mixed * sg).astype(BF16)
            return c

        lax.fori_loop(0, s // CHUNK, chunk, 0)

    return pl.pallas_call(
        body, name="mix_a_fwd", in_specs=[ANY, VMEM, VMEM, VMEM, VMEM], out_specs=VMEM,
        out_shape=jax.ShapeDtypeStruct((s, BR), BF16),
        scratch_shapes=[pltpu.VMEM((s, 2 * BR), F32), pltpu.VMEM((s, BR), F32), pltpu.VMEM((N_HEADS, CHUNK, CHUNK), BF16),
                        pltpu.SemaphoreType.DMA((2,))],
        compiler_params=_params())(proj, sgu_w, bias, lg, lb)


def _mix_a_bwd(proj, dproj, dy, sgu_w, bias, lg, lb):
    s = proj.shape[0]

    def body(proj_ref, dproj_in, dy_ref, sw_ref, bias_ref, lg_ref, lb_ref,
             dproj_ref, dsw_ref, dbias_ref, dlg_ref, dlb_ref,
             uv_buf, ag_buf, duv_buf, dag_buf, wm_ref, acc_lg, acc_lb, sems):
        del dproj_in
        _copy_all([(_window(proj_ref, AUV0, 2 * BR), uv_buf), (_window(proj_ref, AG0, BR), ag_buf)], sems)
        _store_masked_sgu(sw_ref, wm_ref)
        dsw_ref[...] = jnp.zeros_like(dsw_ref)
        dbias_ref[...] = jnp.zeros_like(dbias_ref)
        acc_lg[...] = jnp.zeros_like(acc_lg)
        acc_lb[...] = jnp.zeros_like(acc_lb)

        def chunk(i, c):
            rows = pl.ds(pl.multiple_of(i * CHUNK, CHUNK), CHUNK)
            lg_v = lg_ref[...]
            u, du, dv, xh, rs, vnb, masks, mixed, sg, dsg = _mix_a_chunk(
                uv_buf[rows, :], ag_buf[rows, :], wm_ref, bias_ref[...], lg_v, lb_ref[...])
            dyv = dy_ref[rows, :]
            t1 = dyv * sg
            d_u = t1 * mixed
            d_mixed = t1 * u
            d_ag = dyv * u * mixed * dsg
            dbias_ref[...] += d_mixed
            dmb = d_mixed.astype(BF16)
            d_vn = jnp.zeros((CHUNK, BR), F32)
            for h in range(N_HEADS):
                dm_h = dmb * masks[h]
                dsw_ref[h] += _dot_nt(dm_h, vnb)
                d_vn = d_vn + _dot_tn(wm_ref[h], dm_h)
            acc_lg[...] += _part8(d_vn * xh)
            acc_lb[...] += _part8(d_vn)
            d_v = _ln_bwd(d_vn * lg_v, xh, rs)
            duv_buf[rows, :] = jnp.concatenate([d_u * du, d_v * dv], axis=1).astype(BF16)
            dag_buf[rows, :] = d_ag.astype(BF16)
            return c

        lax.fori_loop(0, s // CHUNK, chunk, 0)
        row = lax.broadcasted_iota(jnp.int32, (CHUNK, CHUNK), 0)
        col = lax.broadcasted_iota(jnp.int32, (CHUNK, CHUNK), 1)
        for h in range(N_HEADS):
            dsw_ref[h] = jnp.where(row >= col, dsw_ref[h], 0.0)
        dlg_ref[...] = jnp.sum(acc_lg[...], axis=0, keepdims=True)
        dlb_ref[...] = jnp.sum(acc_lb[...], axis=0, keepdims=True)
        _copy_all([(duv_buf, _window(dproj_ref, AUV0, 2 * BR)), (dag_buf, _window(dproj_ref, AG0, BR))], sems)

    return pl.pallas_call(
        body, name="mix_a_bwd", in_specs=[ANY, ANY, VMEM, VMEM, VMEM, VMEM, VMEM],
        out_specs=[ANY, VMEM, VMEM, VMEM, VMEM],
        out_shape=[jax.ShapeDtypeStruct(dproj.shape, dproj.dtype), jax.ShapeDtypeStruct((N_HEADS, CHUNK, CHUNK), F32),
                   jax.ShapeDtypeStruct((CHUNK, BR), F32), jax.ShapeDtypeStruct((1, BR), F32), jax.ShapeDtypeStruct((1, BR), F32)],
        scratch_shapes=[pltpu.VMEM((s, 2 * BR), F32), pltpu.VMEM((s, BR), F32), pltpu.VMEM((s, 2 * BR), BF16),
                        pltpu.VMEM((s, BR), BF16), pltpu.VMEM((N_HEADS, CHUNK, CHUNK), BF16),
                        pltpu.VMEM((8, BR), F32), pltpu.VMEM((8, BR), F32), pltpu.SemaphoreType.DMA((2,))],
        input_output_aliases={1: 0}, compiler_params=_params())(proj, dproj, dy, sgu_w, bias, lg, lb)


def _tri_ones(n, upper):
    row = lax.broadcasted_iota(jnp.int32, (n, n), 0)
    col = lax.broadcasted_iota(jnp.int32, (n, n), 1)
    return ((row <= col) if upper else (row >= col)).astype(BF16)


def _split3(c):
    hi = c.astype(BF16)
    r1 = c - hi.astype(F32)
    mid = r1.astype(BF16)
    lo = (r1 - mid.astype(F32)).astype(BF16)
    return [hi, mid, lo]


def _tri_sum(tri, a):
    parts = _split3(a)
    return _dot(tri, parts[0]) + _dot(tri, parts[1]) + _dot(tri, parts[2])


EXT = 2 * HEAD_DIM
LANE_CQ = HEAD_DIM
LANE_CK = HEAD_DIM + 3


def _to_head(h):
    r = lax.broadcasted_iota(jnp.int32, (BR, EXT), 0)
    c = lax.broadcasted_iota(jnp.int32, (BR, EXT), 1)
    return ((r == c + h * HEAD_DIM) & (c < HEAD_DIM)).astype(BF16)


def _from_head(h):
    r = lax.broadcasted_iota(jnp.int32, (EXT, BR), 0)
    c = lax.broadcasted_iota(jnp.int32, (EXT, BR), 1)
    return ((c == r + h * HEAD_DIM) & (r < HEAD_DIM)).astype(BF16)


def _attn_prep_fwd(proj, proj_bf, f_bias):
    s = proj.shape[0]
    scale = 1.0 / math.sqrt(HEAD_DIM)

    def body(proj_ref, z_buf, fb_ref, qe_ref, ke_ref, ve_ref, ket_ref, vet_ref, qkv_buf, sems):
        _copy_all([(_window(proj_ref, QKV0, 3 * BR), qkv_buf)], sems)
        tri = _tri_ones(CHUNK, upper=False)
        lane = lax.broadcasted_iota(jnp.int32, (CHUNK, EXT), 1)
        ones_q = ((lane >= LANE_CK) & (lane < LANE_CK + 3)).astype(F32)
        ones_k = ((lane >= LANE_CQ) & (lane < LANE_CQ + 3)).astype(F32)

        def chunk(i, carry):
            rows = pl.ds(pl.multiple_of(i * CHUNK, CHUNK), CHUNK)
            cum = _tri_sum(tri, jax.nn.log_sigmoid(z_buf[rows, :] + fb_ref[...])) + carry
            qb = (qkv_buf[rows, 0:BR] * scale).astype(BF16)
            kb = qkv_buf[rows, BR:2 * BR].astype(BF16)
            vb = qkv_buf[rows, 2 * BR:].astype(BF16)
            parts = [p.astype(F32) for p in _split3(cum)]
            for h in range(N_HEADS):
                sel = _to_head(h)
                dec_q, dec_k = ones_q, ones_k
                for t, part in enumerate(parts):
                    pf = part[:, h:h + 1]
                    dec_q = dec_q + jnp.where(lane == LANE_CQ + t, pf, 0.0)
                    dec_k = dec_k - jnp.where(lane == LANE_CK + t, pf, 0.0)
                qe_ref[h, rows, :] = (_dot(qb, sel) + dec_q).astype(BF16)
                kh = _dot(kb, sel) + dec_k
                vh = _dot(vb, sel)
                ke_ref[h, rows, :] = kh.astype(BF16)
                ve_ref[h, rows, :] = vh.astype(BF16)
                ket_ref[h, :, rows] = kh.T.astype(BF16)
                vet_ref[h, :, rows] = vh.T.astype(BF16)
            return cum[CHUNK - 1:CHUNK, :]

        lax.fori_loop(0, s // CHUNK, chunk, jnp.zeros((1, CHUNK), F32))

    shape = jax.ShapeDtypeStruct((N_HEADS, s, EXT), BF16)
    shape_t = jax.ShapeDtypeStruct((N_HEADS, EXT, s), BF16)
    return pl.pallas_call(
        body, name="attn_prep_fwd", in_specs=[ANY, VMEM, VMEM], out_specs=[VMEM] * 5, out_shape=[shape, shape, shape, shape_t, shape_t],
        scratch_shapes=[pltpu.VMEM((s, 3 * BR), F32), pltpu.SemaphoreType.DMA((1,))],
        compiler_params=_params())(proj, proj_bf, f_bias)


def _attn_prep_bwd(proj_bf, dproj, dqe, dke, dve, f_bias):
    s = proj_bf.shape[0]
    scale = 1.0 / math.sqrt(HEAD_DIM)

    def body(z_buf, dproj_in, dq_ref, dk_ref, dv_ref, fb_ref, dproj_ref, dz_buf, dfb_ref, dqkv_buf, sems):
        del dproj_in
        tri = _tri_ones(CHUNK, upper=True)
        lane = lax.broadcasted_iota(jnp.int32, (CHUNK, CHUNK), 1)
        n = s // CHUNK

        def chunk(t, carry):
            suffix, acc = carry
            i = n - 1 - t
            rows = pl.ds(pl.multiple_of(i * CHUNK, CHUNK), CHUNK)
            dq = jnp.zeros((CHUNK, BR), F32)
            dk = jnp.zeros((CHUNK, BR), F32)
            dv = jnp.zeros((CHUNK, BR), F32)
            dcum = jnp.zeros((CHUNK, CHUNK), F32)
            for h in range(N_HEADS):
                back = _from_head(h)
                dqh = dq_ref[h, :, rows].T
                dkh = dk_ref[h, rows, :]
                dq = dq + _dot((dqh * scale).astype(BF16), back)
                dk = dk + _dot(dkh.astype(BF16), back)
                dv = dv + _dot(dv_ref[h, rows, :].astype(BF16), back)
                dcum = dcum + jnp.where(lane == h, dqh[:, LANE_CQ:LANE_CQ + 1] - dkh[:, LANE_CK:LANE_CK + 1], 0.0)
            dqkv_buf[rows, 0:BR] = dq.astype(BF16)
            dqkv_buf[rows, BR:2 * BR] = dk.astype(BF16)
            dqkv_buf[rows, 2 * BR:] = dv.astype(BF16)
            dlf = _tri_sum(tri, dcum) + suffix
            dz = dlf * _sigmoid(-(z_buf[rows, :] + fb_ref[...]))
            dz_buf[rows, :] = dz.astype(BF16)
            return dlf[0:1, :], acc + _part8(dz)

        _, acc = lax.fori_loop(0, n, chunk, (jnp.zeros((1, CHUNK), F32), jnp.zeros((8, CHUNK), F32)))
        dfb_ref[...] = jnp.sum(acc, axis=0, keepdims=True)
        _copy_all([(dqkv_buf, _window(dproj_ref, QKV0, 3 * BR))], sems)

    return pl.pallas_call(
        body, name="attn_prep_bwd", in_specs=[VMEM, ANY, VMEM, VMEM, VMEM, VMEM], out_specs=[ANY, VMEM, VMEM],
        out_shape=[jax.ShapeDtypeStruct(dproj.shape, dproj.dtype), jax.ShapeDtypeStruct((s, CHUNK), BF16),
                   jax.ShapeDtypeStruct((1, CHUNK), F32)],
        scratch_shapes=[pltpu.VMEM((s, 3 * BR), BF16), pltpu.SemaphoreType.DMA((1,))],
        input_output_aliases={1: 0}, compiler_params=_params())(proj_bf, dproj, dqe, dke, dve, f_bias)


ATT_TQ = 256
ATT_FWD_GROUP = 8
ATT_BWD_GROUP = 8
NEG_BIG = -1e30


def _causal_t(q0, k0, tk, tq):
    kpos = k0 + lax.broadcasted_iota(jnp.int32, (tk, tq), 0)
    qpos = q0 + lax.broadcasted_iota(jnp.int32, (tk, tq), 1)
    return kpos <= qpos


STAT_ROWS = 8


def _attn_fwd(qe, ke, vet, rider=None):
    nh, s, w = qe.shape
    tq = min(ATT_TQ, s)
    tk = min(ATT_FWD_GROUP * tq, s)
    per = tk // tq
    nq = s // tq
    r_in, r_in_specs, r_out_specs, r_shapes, r_scratch, r_alias = _rider_parts(rider, 3, 2)
    nr = len(r_in)

    def body(*refs):
        q_ref, k_ref, vt_ref = refs[:3]
        ot_ref, lse_ref = refs[3 + nr:5 + nr]
        h, i = pl.program_id(0), pl.program_id(1)
        start, finish = _ride(rider, (h == 0) & (i == 0), (h == nh - 1) & (i == nq - 1),
                              refs[3:3 + nr], refs[5 + nr:5 + 2 * nr], refs[5 + 2 * nr:])
        start()
        q = q_ref[0]

        def step(k0, carry, width, masked):
            m, l, acc = carry
            ks = pl.ds(pl.multiple_of(k0, tq), width)
            st = _dot_nt(k_ref[0, ks, :], q)
            if masked:
                st = jnp.where(_causal_t(i * tq, k0, width, tq), st, NEG_BIG)
            m_new = jnp.maximum(m, jnp.max(st, axis=0, keepdims=True))
            alpha = jnp.exp(m - m_new)
            pt = jnp.exp(st - m_new)
            l = alpha * l + jnp.sum(pt, axis=0, keepdims=True)
            acc = alpha * acc + _dot(vt_ref[0, :, ks], pt.astype(BF16))
            return m_new, l, acc

        full = i // per
        init = (jnp.full((1, tq), NEG_BIG, F32), jnp.zeros((1, tq), F32), jnp.zeros((w, tq), F32))
        carry = lax.fori_loop(0, full, lambda j, c: step(j * tk, c, tk, False), init)
        m, l, acc = lax.switch(i % per, [functools.partial(step, width=(r + 1) * tq, masked=True) for r in range(per)],
                               full * tk, carry)
        ot_ref[0] = acc / l
        lse_ref[0] = jnp.broadcast_to(m + jnp.log(l), (STAT_ROWS, tq))
        finish()

    outs = pl.pallas_call(
        body, name="attn_fwd", grid=(nh, nq),
        in_specs=[pl.BlockSpec((1, tq, w), lambda h, i: (h, i, 0)), pl.BlockSpec((1, s, w), lambda h, i: (h, 0, 0)),
                  pl.BlockSpec((1, w, s), lambda h, i: (h, 0, 0))] + r_in_specs,
        out_specs=[pl.BlockSpec((1, w, tq), lambda h, i: (h, 0, i)), pl.BlockSpec((1, STAT_ROWS, tq), lambda h, i: (h, 0, i))] + r_out_specs,
        out_shape=[jax.ShapeDtypeStruct((nh, w, s), F32), jax.ShapeDtypeStruct((nh, STAT_ROWS, s), F32)] + r_shapes,
        scratch_shapes=r_scratch, input_output_aliases=r_alias,
        compiler_params=_params("arbitrary", "arbitrary"))(qe, ke, vet, *r_in)
    return outs[0], outs[1], outs[2:]


def _attn_bwd(qe, ke, ket, ve, doe, lse, dd, rider=None):
    nh, s, w = qe.shape
    tq = min(ATT_TQ, s)
    tk = min(ATT_BWD_GROUP * tq, s)
    per = tk // tq
    nq = s // tq
    r_in, r_in_specs, r_out_specs, r_shapes, r_scratch, r_alias = _rider_parts(rider, 7, 3)
    nr = len(r_in)

    def body(*refs):
        q_ref, do_ref, lse_ref, dd_ref, k_ref, kt_ref, v_ref = refs[:7]
        dqt_ref, dk_ref, dv_ref = refs[7 + nr:10 + nr]
        h, i = pl.program_id(0), pl.program_id(1)
        start, finish = _ride(rider, (h == 0) & (i == 0), (h == nh - 1) & (i == nq - 1),
                              refs[7:7 + nr], refs[10 + nr:10 + 2 * nr], refs[10 + 2 * nr:])
        start()

        @pl.when(i == 0)
        def _():
            dk_ref[...] = jnp.zeros_like(dk_ref)
            dv_ref[...] = jnp.zeros_like(dv_ref)

        q = q_ref[0]
        do = do_ref[0]
        lse_row = lse_ref[0, 0:1, :]
        dd_row = dd_ref[0, 0:1, :]

        def step(k0, dqt, width, masked):
            ks = pl.ds(pl.multiple_of(k0, tq), width)
            st = _dot_nt(k_ref[0, ks, :], q)
            pt = jnp.exp(st - lse_row)
            if masked:
                pt = jnp.where(_causal_t(i * tq, k0, width, tq), pt, 0.0)
            dpt = _dot_nt(v_ref[0, ks, :], do)
            dst = (pt * (dpt - dd_row)).astype(BF16)
            dv_ref[0, ks, :] += _dot(pt.astype(BF16), do)
            dk_ref[0, ks, :] += _dot(dst, q)
            return dqt + _dot(kt_ref[0, :, ks], dst)

        full = i // per
        dqt = lax.fori_loop(0, full, lambda j, c: step(j * tk, c, tk, False), jnp.zeros((w, tq), F32))
        dqt_ref[0] = lax.switch(i % per, [functools.partial(step, width=(r + 1) * tq, masked=True) for r in range(per)],
                                full * tk, dqt)
        finish()

    qblk = pl.BlockSpec((1, tq, w), lambda h, i: (h, i, 0))
    stat = pl.BlockSpec((1, STAT_ROWS, tq), lambda h, i: (h, 0, i))
    whole = pl.BlockSpec((1, s, w), lambda h, i: (h, 0, 0))
    whole_t = pl.BlockSpec((1, w, s), lambda h, i: (h, 0, 0))
    outs = pl.pallas_call(
        body, name="attn_bwd", grid=(nh, nq),
        in_specs=[qblk, qblk, stat, stat, whole, whole_t, whole] + r_in_specs,
        out_specs=[pl.BlockSpec((1, w, tq), lambda h, i: (h, 0, i)), whole, whole] + r_out_specs,
        out_shape=[jax.ShapeDtypeStruct((nh, w, s), F32), jax.ShapeDtypeStruct((nh, s, w), F32),
                   jax.ShapeDtypeStruct((nh, s, w), F32)] + r_shapes,
        scratch_shapes=r_scratch, input_output_aliases=r_alias,
        compiler_params=_params("arbitrary", "arbitrary"))(qe, doe, lse, dd, ke, ket, ve, *r_in)
    return outs[0], outs[1], outs[2], outs[3:]


def _bgate_fwd(proj, ot, rider=None):
    s = proj.shape[0]
    r_in, r_in_specs, r_out_specs, r_shapes, r_scratch, r_alias = _rider_parts(rider, 2, 2)
    nr = len(r_in)

    def body(*refs):
        proj_ref, ot_ref = refs[:2]
        y_ref, o_ref = refs[2 + nr:4 + nr]
        g_buf, sems = refs[4 + 2 * nr:6 + 2 * nr]
        start, finish = _ride(rider, True, True, refs[2:2 + nr], refs[4 + nr:4 + 2 * nr], refs[6 + 2 * nr:])
        start()
        _copy_all([(_window(proj_ref, BG0, BR), g_buf)], sems)

        def chunk(i, c):
            rows = pl.ds(pl.multiple_of(i * CHUNK, CHUNK), CHUNK)
            o = jnp.zeros((CHUNK, BR), F32)
            for h in range(N_HEADS):
                back = _from_head(h)
                for part in _split3(ot_ref[h, :, rows].T):
                    o = o + _dot(part, back)
            sg, _ = _silu_and_grad(g_buf[rows, :])
            o_ref[rows, :] = o
            y_ref[rows, :] = (o * sg).astype(BF16)
            return c

        lax.fori_loop(0, s // CHUNK, chunk, 0)
        finish()

    outs = pl.pallas_call(
        body, name="bgate_fwd", in_specs=[ANY, VMEM] + r_in_specs, out_specs=[VMEM, VMEM] + r_out_specs,
        out_shape=[jax.ShapeDtypeStruct((s, BR), BF16), jax.ShapeDtypeStruct((s, BR), F32)] + r_shapes,
        scratch_shapes=[pltpu.VMEM((s, BR), F32), pltpu.SemaphoreType.DMA((1,))] + r_scratch,
        input_output_aliases=r_alias, compiler_params=_params())(proj, ot, *r_in)
    return outs[0], outs[1], outs[2:]


def _bgate_bwd(proj, dproj, o, dy):
    s = proj.shape[0]

    def body(proj_ref, dproj_in, o_ref, dy_ref, dproj_ref, do_ref, dd_ref, g_buf, dg_buf, sems):
        del dproj_in
        _copy_all([(_window(proj_ref, BG0, BR), g_buf)], sems)
        lane = lax.broadcasted_iota(jnp.int32, (BR, CHUNK), 0) // HEAD_DIM
        col = lax.broadcasted_iota(jnp.int32, (BR, CHUNK), 1)
        sel = (lane == col).astype(BF16)

        def chunk(i, c):
            rows = pl.ds(pl.multiple_of(i * CHUNK, CHUNK), CHUNK)
            sg, dsg = _silu_and_grad(g_buf[rows, :])
            dyv = dy_ref[rows, :]
            ov = o_ref[rows, :]
            do = dyv * sg
            dg_buf[rows, :] = (dyv * ov * dsg).astype(BF16)
            prod = _split3(do * ov)
            ddt = (_dot(prod[0], sel) + _dot(prod[1], sel) + _dot(prod[2], sel)).T
            dob = do.astype(BF16)
            for h in range(N_HEADS):
                do_ref[h, rows, :] = _dot(dob, _to_head(h)).astype(BF16)
                dd_ref[h, :, rows] = jnp.broadcast_to(ddt[h:h + 1, :], (STAT_ROWS, CHUNK))
            return c

        lax.fori_loop(0, s // CHUNK, chunk, 0)
        _copy_all([(dg_buf, _window(dproj_ref, BG0, BR))], sems)

    return pl.pallas_call(
        body, name="bgate_bwd", in_specs=[ANY, ANY, VMEM, VMEM], out_specs=[ANY, VMEM, VMEM],
        out_shape=[jax.ShapeDtypeStruct(dproj.shape, dproj.dtype), jax.ShapeDtypeStruct((N_HEADS, s, EXT), BF16),
                   jax.ShapeDtypeStruct((N_HEADS, STAT_ROWS, s), F32)],
        scratch_shapes=[pltpu.VMEM((s, BR), F32), pltpu.VMEM((s, BR), BF16), pltpu.SemaphoreType.DMA((1,))],
        input_output_aliases={1: 0}, compiler_params=_params())(proj, dproj, o, dy)


C_PAD = 8


def _mix_c_fwd(proj, w):
    s = proj.shape[0]

    def body(proj_ref, w_ref, y_ref, cin_buf, g_buf, z_buf, sems):
        _copy_all([(_window(proj_ref, CIN0, 3 * BR), cin_buf), (_window(proj_ref, CG0, BR), g_buf)], sems)
        z_buf[pl.ds(0, C_PAD), :] = jnp.zeros((C_PAD, BR), F32)

        def fill(i, c):
            rows = pl.ds(pl.multiple_of(i * CHUNK, CHUNK), CHUNK)
            z_buf[pl.ds(pl.multiple_of(i * CHUNK + C_PAD, 8), CHUNK), :] = cin_buf[rows, BR:2 * BR] * cin_buf[rows, 2 * BR:]
            return c

        lax.fori_loop(0, s // CHUNK, fill, 0)

        def chunk(i, c):
            r0 = pl.multiple_of(i * CHUNK, CHUNK)
            rows = pl.ds(r0, CHUNK)
            ze = z_buf[pl.ds(r0, CHUNK + C_PAD), :]
            conv = jnp.zeros((CHUNK, BR), F32)
            for k in range(K_SHORT):
                off = C_PAD - (K_SHORT - 1) + k
                conv = conv + w_ref[k:k + 1, :] * ze[off:off + CHUNK]
            sg, _ = _silu_and_grad(g_buf[rows, :])
            y_ref[rows, :] = (cin_buf[rows, 0:BR] * conv * sg).astype(BF16)
            return c

        lax.fori_loop(0, s // CHUNK, chunk, 0)

    return pl.pallas_call(
        body, name="mix_c_fwd", in_specs=[ANY, VMEM], out_specs=VMEM, out_shape=jax.ShapeDtypeStruct((s, BR), BF16),
        scratch_shapes=[pltpu.VMEM((s, 3 * BR), F32), pltpu.VMEM((s, BR), F32), pltpu.VMEM((s + C_PAD, BR), F32),
                        pltpu.SemaphoreType.DMA((2,))],
        compiler_params=_params())(proj, w)


def _mix_c_bwd(proj, dproj, dy, w):
    s = proj.shape[0]

    def body(proj_ref, dproj_in, dy_ref, w_ref, dproj_ref, dw_ref, cin_buf, g_buf, z_buf, dc_buf, dcin_buf, dg_buf, acc, sems):
        del dproj_in
        _copy_all([(_window(proj_ref, CIN0, 3 * BR), cin_buf), (_window(proj_ref, CG0, BR), g_buf)], sems)
        z_buf[pl.ds(0, C_PAD), :] = jnp.zeros((C_PAD, BR), F32)
        dc_buf[pl.ds(s, C_PAD), :] = jnp.zeros((C_PAD, BR), F32)
        acc[...] = jnp.zeros_like(acc)

        def fill(i, c):
            rows = pl.ds(pl.multiple_of(i * CHUNK, CHUNK), CHUNK)
            z_buf[pl.ds(pl.multiple_of(i * CHUNK + C_PAD, 8), CHUNK), :] = cin_buf[rows, BR:2 * BR] * cin_buf[rows, 2 * BR:]
            return c

        lax.fori_loop(0, s // CHUNK, fill, 0)

        def chunk(i, c):
            r0 = pl.multiple_of(i * CHUNK, CHUNK)
            rows = pl.ds(r0, CHUNK)
            ze = z_buf[pl.ds(r0, CHUNK + C_PAD), :]
            taps = [ze[C_PAD - (K_SHORT - 1) + k:C_PAD - (K_SHORT - 1) + k + CHUNK] for k in range(K_SHORT)]
            conv = jnp.zeros((CHUNK, BR), F32)
            for k in range(K_SHORT):
                conv = conv + w_ref[k:k + 1, :] * taps[k]
            sg, dsg = _silu_and_grad(g_buf[rows, :])
            bg = cin_buf[rows, 0:BR]
            dyv = dy_ref[rows, :]
            dconv = dyv * bg * sg
            dc_buf[rows, :] = dconv
            dcin_buf[rows, 0:BR] = (dyv * conv * sg).astype(BF16)
            dg_buf[rows, :] = (dyv * bg * conv * dsg).astype(BF16)
            for k in range(K_SHORT):
                acc[k] += _part8(dconv * taps[k])
            return c

        lax.fori_loop(0, s // CHUNK, chunk, 0)

        def chunk2(i, c):
            r0 = pl.multiple_of(i * CHUNK, CHUNK)
            rows = pl.ds(r0, CHUNK)
            de = dc_buf[pl.ds(r0, CHUNK + C_PAD), :]
            dz = jnp.zeros((CHUNK, BR), F32)
            for k in range(K_SHORT):
                off = K_SHORT - 1 - k
                dz = dz + w_ref[k:k + 1, :] * de[off:off + CHUNK]
            dcin_buf[rows, BR:2 * BR] = (dz * cin_buf[rows, 2 * BR:]).astype(BF16)
            dcin_buf[rows, 2 * BR:] = (dz * cin_buf[rows, BR:2 * BR]).astype(BF16)
            return c

        lax.fori_loop(0, s // CHUNK, chunk2, 0)
        dw_ref[...] = jnp.zeros_like(dw_ref)
        for k in range(K_SHORT):
            dw_ref[k:k + 1, :] = jnp.sum(acc[k], axis=0, keepdims=True)
        _copy_all([(dcin_buf, _window(dproj_ref, CIN0, 3 * BR)), (dg_buf, _window(dproj_ref, CG0, BR))], sems)

    return pl.pallas_call(
        body, name="mix_c_bwd", in_specs=[ANY, ANY, VMEM, VMEM], out_specs=[ANY, VMEM],
        out_shape=[jax.ShapeDtypeStruct(dproj.shape, dproj.dtype), jax.ShapeDtypeStruct((8, BR), F32)],
        scratch_shapes=[pltpu.VMEM((s, 3 * BR), F32), pltpu.VMEM((s, BR), F32), pltpu.VMEM((s + C_PAD, BR), F32),
                        pltpu.VMEM((s + C_PAD, BR), F32), pltpu.VMEM((s, 3 * BR), BF16), pltpu.VMEM((s, BR), BF16),
                        pltpu.VMEM((K_SHORT, 8, BR), F32), pltpu.SemaphoreType.DMA((2,))],
        input_output_aliases={1: 0}, compiler_params=_params())(proj, dproj, dy, w)


D_PAD = 32


def _tap_windows(win, offsets, n):
    rows = win.shape[0]
    rolled, out = {}, []
    for off in offsets:
        r = off % 8
        if r not in rolled:
            rolled[r] = win if r == 0 else pltpu.roll(win, rows - r, 0)
        out.append(rolled[r][off - r:off - r + n])
    return out


def _mix_d_common(g_ref, w_ref, b_ref, lg_ref, lb_ref, hh_buf, r0):
    he = hh_buf[pl.ds(r0, CHUNK + D_PAD), :]
    taps = _tap_windows(he, [D_PAD - (K_CONF - 1) + k for k in range(K_CONF)], CHUNK)
    conv = jnp.zeros((CHUNK, BR), F32) + b_ref[...]
    for k in range(K_CONF):
        conv = conv + w_ref[k:k + 1, :] * taps[k]
    xh, rs = _ln_fwd(conv)
    sw, dsw = _silu_and_grad(xh * lg_ref[...] + lb_ref[...])
    sg, dsg = _silu_and_grad(g_ref[pl.ds(r0, CHUNK), :])
    return taps, xh, rs, sw, dsw, sg, dsg


def _mix_d_fill(cin_buf, hh_buf, s):
    hh_buf[pl.ds(0, D_PAD), :] = jnp.zeros((D_PAD, BR), F32)

    def fill(i, c):
        rows = pl.ds(pl.multiple_of(i * CHUNK, CHUNK), CHUNK)
        hh_buf[pl.ds(pl.multiple_of(i * CHUNK + D_PAD, 8), CHUNK), :] = cin_buf[rows, 0:BR] * _sigmoid(cin_buf[rows, BR:])
        return c

    lax.fori_loop(0, s // CHUNK, fill, 0)


def _mix_d_fwd(proj, w, b, lg, lb):
    s = proj.shape[0]

    def body(proj_ref, w_ref, b_ref, lg_ref, lb_ref, y_ref, cin_buf, g_buf, hh_buf, sems):
        _copy_all([(_window(proj_ref, DGLU0, 2 * BR), cin_buf), (_window(proj_ref, DG0, BR), g_buf)], sems)
        _mix_d_fill(cin_buf, hh_buf, s)

        def chunk(i, c):
            r0 = pl.multiple_of(i * CHUNK, CHUNK)
            _, _, _, sw, _, sg, _ = _mix_d_common(g_buf, w_ref, b_ref, lg_ref, lb_ref, hh_buf, r0)
            y_ref[pl.ds(r0, CHUNK), :] = (sw * sg).astype(BF16)
            return c

        lax.fori_loop(0, s // CHUNK, chunk, 0)

    return pl.pallas_call(
        body, name="mix_d_fwd", in_specs=[ANY, VMEM, VMEM, VMEM, VMEM], out_specs=VMEM,
        out_shape=jax.ShapeDtypeStruct((s, BR), BF16),
        scratch_shapes=[pltpu.VMEM((s, 2 * BR), F32), pltpu.VMEM((s, BR), F32), pltpu.VMEM((s + D_PAD, BR), F32),
                        pltpu.SemaphoreType.DMA((2,))],
        compiler_params=_params())(proj, w, b, lg, lb)


def _mix_d_bwd(proj, dproj, dy, w, b, lg, lb):
    s = proj.shape[0]

    def body(proj_ref, dproj_in, dy_ref, w_ref, b_ref, lg_ref, lb_ref, dproj_ref, dw_ref, db_ref, dlg_ref, dlb_ref,
             cin_buf, g_buf, hh_buf, dc_buf, dcin_buf, dg_buf, acc_w, acc_s, sems):
        del dproj_in
        _copy_all([(_window(proj_ref, DGLU0, 2 * BR), cin_buf), (_window(proj_ref, DG0, BR), g_buf)], sems)
        _mix_d_fill(cin_buf, hh_buf, s)
        dc_buf[pl.ds(s, D_PAD), :] = jnp.zeros((D_PAD, BR), F32)
        acc_w[...] = jnp.zeros_like(acc_w)
        acc_s[...] = jnp.zeros_like(acc_s)

        def chunk(i, c):
            r0 = pl.multiple_of(i * CHUNK, CHUNK)
            rows = pl.ds(r0, CHUNK)
            taps, xh, rs, sw, dsw, sg, dsg = _mix_d_common(g_buf, w_ref, b_ref, lg_ref, lb_ref, hh_buf, r0)
            dyv = dy_ref[rows, :]
            dg_buf[rows, :] = (dyv * sw * dsg).astype(BF16)
            d_ln = dyv * sg * dsw
            acc_s[0] += _part8(d_ln * xh)
            acc_s[1] += _part8(d_ln)
            dc = _ln_bwd(d_ln * lg_ref[...], xh, rs)
            acc_s[2] += _part8(dc)
            dc_buf[rows, :] = dc
            for k in range(K_CONF):
                acc_w[k] += _part8(dc * taps[k])
            return c

        lax.fori_loop(0, s // CHUNK, chunk, 0)

        def chunk2(i, c):
            r0 = pl.multiple_of(i * CHUNK, CHUNK)
            rows = pl.ds(r0, CHUNK)
            de = dc_buf[pl.ds(r0, CHUNK + D_PAD), :]
            dh = jnp.zeros((CHUNK, BR), F32)
            for k, win in enumerate(_tap_windows(de, [K_CONF - 1 - k for k in range(K_CONF)], CHUNK)):
                dh = dh + w_ref[k:k + 1, :] * win
            a = cin_buf[rows, 0:BR]
            sig = _sigmoid(cin_buf[rows, BR:])
            dcin_buf[rows, 0:BR] = (dh * sig).astype(BF16)
            dcin_buf[rows, BR:] = (dh * a * sig * (1.0 - sig)).astype(BF16)
            return c

        lax.fori_loop(0, s // CHUNK, chunk2, 0)
        dw_ref[...] = jnp.zeros_like(dw_ref)
        for k in range(K_CONF):
            dw_ref[k:k + 1, :] = jnp.sum(acc_w[k], axis=0, keepdims=True)
        dlg_ref[...] = jnp.sum(acc_s[0], axis=0, keepdims=True)
        dlb_ref[...] = jnp.sum(acc_s[1], axis=0, keepdims=True)
        db_ref[...] = jnp.sum(acc_s[2], axis=0, keepdims=True)
        _copy_all([(dcin_buf, _window(dproj_ref, DGLU0, 2 * BR)), (dg_buf, _window(dproj_ref, DG0, BR))], sems)

    vec = jax.ShapeDtypeStruct((1, BR), F32)
    return pl.pallas_call(
        body, name="mix_d_bwd", in_specs=[ANY, ANY, VMEM, VMEM, VMEM, VMEM, VMEM], out_specs=[ANY, VMEM, VMEM, VMEM, VMEM],
        out_shape=[jax.ShapeDtypeStruct(dproj.shape, dproj.dtype), jax.ShapeDtypeStruct((32, BR), F32), vec, vec, vec],
        scratch_shapes=[pltpu.VMEM((s, 2 * BR), F32), pltpu.VMEM((s, BR), F32), pltpu.VMEM((s + D_PAD, BR), F32),
                        pltpu.VMEM((s + D_PAD, BR), F32), pltpu.VMEM((s, 2 * BR), BF16), pltpu.VMEM((s, BR), BF16),
                        pltpu.VMEM((K_CONF, 8, BR), F32), pltpu.VMEM((3, 8, BR), F32), pltpu.SemaphoreType.DMA((2,))],
        input_output_aliases={1: 0}, compiler_params=_params())(proj, dproj, dy, w, b, lg, lb)


MERGE_TM = 256


def _merge_fwd(x, proj, ys, wb, wo):
    s = x.shape[0]
    tm = min(MERGE_TM, s)

    def body(x_ref, lg_ref, ya, yb, yc, yd, wb_ref, wo_ref, xn_ref, mg_ref):
        merged = jnp.zeros((tm, D_MODEL), F32)
        for n, y_ref in enumerate((ya, yb, yc, yd)):
            gate = _sigmoid(lg_ref[:, n * D_MODEL:(n + 1) * D_MODEL])
            merged = merged + gate * _dot(y_ref[...], wb_ref[n])
        mb = merged.astype(BF16)
        mg_ref[...] = mb
        xn_ref[...] = x_ref[...] + _dot(mb, wo_ref[...])

    row = lambda w: pl.BlockSpec((tm, w), lambda i: (i, 0))
    return pl.pallas_call(
        body, name="merge_fwd", grid=(s // tm,),
        in_specs=[row(D_MODEL), pl.BlockSpec((tm, MRG_COLS), lambda i: (i, MRG0 // MRG_COLS)), row(BR), row(BR), row(BR), row(BR),
                  pl.BlockSpec((N_HEADS, BR, D_MODEL), lambda i: (0, 0, 0)), pl.BlockSpec((D_MODEL, D_MODEL), lambda i: (0, 0))],
        out_specs=[row(D_MODEL), row(D_MODEL)],
        out_shape=[jax.ShapeDtypeStruct((s, D_MODEL), F32), jax.ShapeDtypeStruct((s, D_MODEL), BF16)],
        compiler_params=_params("arbitrary"))(x, proj, *ys, wb, wo)


def _merge_bwd(dxn, proj, ys, merged, wb, wo, rider=None):
    s = dxn.shape[0]
    tm = min(MERGE_TM, s)
    steps = s // tm
    r_in, r_in_specs, r_out_specs, r_shapes, r_scratch, r_alias = _rider_parts(rider, 9, 7)
    nr = len(r_in)

    def body(*refs):
        dx_ref, lg_ref, ya, yb, yc, yd, mg_ref, wb_ref, wo_ref = refs[:9]
        dlg_ref, da, db, dc, dd, dwo_ref, dwb_ref = refs[9 + nr:16 + nr]
        start, finish = _ride(rider, pl.program_id(0) == 0, pl.program_id(0) == steps - 1,
                              refs[9:9 + nr], refs[16 + nr:16 + 2 * nr], refs[16 + 2 * nr:])
        start()

        @pl.when(pl.program_id(0) == 0)
        def _():
            dwo_ref[...] = jnp.zeros_like(dwo_ref)
            dwb_ref[...] = jnp.zeros_like(dwb_ref)

        dxb = dx_ref[...].astype(BF16)
        d_merged = _dot_nt(dxb, wo_ref[...])
        dwo_ref[...] += _dot_tn(mg_ref[...], dxb)
        for n, (y_ref, dy_ref) in enumerate(((ya, da), (yb, db), (yc, dc), (yd, dd))):
            yv = y_ref[...]
            gate = _sigmoid(lg_ref[:, n * D_MODEL:(n + 1) * D_MODEL])
            pn = _dot(yv, wb_ref[n])
            dlg_ref[:, n * D_MODEL:(n + 1) * D_MODEL] = (d_merged * pn * gate * (1.0 - gate)).astype(BF16)
            dpn = (d_merged * gate).astype(BF16)
            dy_ref[...] = _dot_nt(dpn, wb_ref[n])
            dwb_ref[n] += _dot_tn(yv, dpn)
        finish()

    row = lambda w: pl.BlockSpec((tm, w), lambda i: (i, 0))
    wb_spec = pl.BlockSpec((N_HEADS, BR, D_MODEL), lambda i: (0, 0, 0))
    wo_spec = pl.BlockSpec((D_MODEL, D_MODEL), lambda i: (0, 0))
    dy_shape = jax.ShapeDtypeStruct((s, BR), F32)
    outs = pl.pallas_call(
        body, name="merge_bwd", grid=(steps,),
        in_specs=[row(D_MODEL), pl.BlockSpec((tm, MRG_COLS), lambda i: (i, MRG0 // MRG_COLS)), row(BR), row(BR), row(BR), row(BR), row(D_MODEL),
                  wb_spec, wo_spec] + r_in_specs,
        out_specs=[pl.BlockSpec((tm, MRG_COLS), lambda i: (i, MRG0 // MRG_COLS)), row(BR), row(BR), row(BR), row(BR), wo_spec, wb_spec] + r_out_specs,
        out_shape=[jax.ShapeDtypeStruct((s, PCOLS), BF16), dy_shape, dy_shape, dy_shape, dy_shape,
                   jax.ShapeDtypeStruct((D_MODEL, D_MODEL), F32), jax.ShapeDtypeStruct((N_HEADS, BR, D_MODEL), F32)] + r_shapes,
        scratch_shapes=r_scratch, input_output_aliases=r_alias,
        compiler_params=_params("arbitrary"))(dxn, proj, *ys, merged, wb, wo, *r_in)
    return outs[0], outs[1:5], outs[5], outs[6], outs[7:]


def _loss_head(x, target, g):
    s = x.shape[0]
    tm = min(512, s)

    def body(x_ref, t_ref, g_ref, loss_ref, dx_ref, dg_ref):
        @pl.when(pl.program_id(0) == 0)
        def _():
            loss_ref[...] = jnp.zeros_like(loss_ref)
            dg_ref[...] = jnp.zeros_like(dg_ref)

        xv = x_ref[...]
        gv = g_ref[...]
        r = lax.rsqrt(jnp.mean(xv * xv, axis=-1, keepdims=True) + EPS)
        xn = xv * r
        err = xn * gv - t_ref[...]
        loss_ref[...] += 0.5 * jnp.sum(jnp.mean(err * err, axis=-1, keepdims=True))
        dy = err * (1.0 / D_MODEL)
        dg_ref[...] += _part8(dy * xn)
        gy = dy * gv
        dx_ref[...] = r * (gy - xn * jnp.mean(xn * gy, axis=-1, keepdims=True))

    row = pl.BlockSpec((tm, D_MODEL), lambda i: (i, 0))
    return pl.pallas_call(
        body, name="loss_head", grid=(s // tm,),
        in_specs=[row, row, pl.BlockSpec((1, D_MODEL), lambda i: (0, 0))],
        out_specs=[pl.BlockSpec((8, 128), lambda i: (0, 0)), row, pl.BlockSpec((8, D_MODEL), lambda i: (0, 0))],
        out_shape=[jax.ShapeDtypeStruct((8, 128), F32), jax.ShapeDtypeStruct((s, D_MODEL), F32), jax.ShapeDtypeStruct((8, D_MODEL), F32)],
        compiler_params=_params("arbitrary"))(x, target, g)


_SEGMENTS = ((0, 512, AUV0), (512, 256, AG0), (768, 768, QKV0), (1540, 256, BG0), (1796, 768, CIN0),
             (2564, 256, CG0), (2820, 512, DGLU0), (3332, 256, DG0), (3588, 4096, MRG0))


def _sgu_bias_rows(sgu_b):
    return jnp.repeat(sgu_b.T, HEAD_DIM, axis=1)


def _layer_fwd(x, p, next_blocks=None):
    proj, proj_bf, h = _inproj_fwd(x, p["norm_g"], p["w_in"], p["w_bf"])
    ya = _mix_a_fwd(proj, p["sgu_w"], _sgu_bias_rows(p["sgu_b"]), p["sgu_ln_g"], p["sgu_ln_b"])
    qe, ke, ve, ket, vet = _attn_prep_fwd(proj, proj_bf, p["f_bias"])
    ot, lse, landed = _attn_fwd(qe, ke, vet, rider=None if next_blocks is None else _gather_send_rider(next_blocks))
    yb, o, gathered = _bgate_fwd(proj, ot, rider=None if next_blocks is None else _gather_forward_rider(landed))
    yc = _mix_c_fwd(proj, p["short_conv_w"])
    yd = _mix_d_fwd(proj, p["conf_dw_w"], p["conf_dw_b"], p["conf_ln_g"], p["conf_ln_b"])
    ys = (ya, yb, yc, yd)
    x_next, merged = _merge_fwd(x, proj, ys, p["w_branch"], p["w_out"])
    saved = dict(x=x, proj=proj, proj_bf=proj_bf, h=h, ys=ys, merged=merged, qe=qe, ke=ke, ve=ve, ket=ket, lse=lse, o=o)
    return x_next, saved, gathered


def _grads_by_chip(d_win, d_wb, d_wo):
    return [jnp.stack([_w_in_shard(d_win[0], d_win[1], j) for j in range(N_CHIPS)]),
            jnp.stack([d_wb[:, :, j * BR:(j + 1) * BR].reshape(N_HEADS * BR, BR).astype(BF16) for j in range(N_CHIPS)]),
            jnp.stack([d_wo[j * BR:(j + 1) * BR].astype(BF16) for j in range(N_CHIPS)])]


def _layer_bwd(dxn, p, sv, prev_reduce=None, reduce_self=None):
    proj = sv["proj"]
    if prev_reduce is None:
        dproj, dys, d_wo, d_wb, _ = _merge_bwd(dxn, proj, sv["ys"], sv["merged"], p["w_branch"], p["w_out"])
        chip_rider = None
    else:
        pc, chip, arrays = prev_reduce
        mine, other = _row_halves(arrays, pc)
        dproj, dys, d_wo, d_wb, from_sibling = _merge_bwd(dxn, proj, sv["ys"], sv["merged"], p["w_branch"], p["w_out"],
                                                           rider=_pair_rider(other))
        halves = _add_pairs(mine, from_sibling)
        chip_rider = _chip_rider(halves)
    dproj, d_sgu_w, d_bias_rows, d_sgu_lg, d_sgu_lb = _mix_a_bwd(
        proj, dproj, dys[0], p["sgu_w"], _sgu_bias_rows(p["sgu_b"]), p["sgu_ln_g"], p["sgu_ln_b"])
    dproj, doe, dd = _bgate_bwd(proj, dproj, sv["o"], dys[1])
    dqe, dke, dve, from_chips = _attn_bwd(sv["qe"], sv["ke"], sv["ket"], sv["ve"], doe, sv["lse"], dd, rider=chip_rider)
    reduced = None
    if prev_reduce is not None:
        reduced = _sum_chips(halves, chip, from_chips)
    dproj, dproj_bf, d_fb = _attn_prep_bwd(sv["proj_bf"], dproj, dqe, dke, dve, p["f_bias"])
    dproj, d_sc = _mix_c_bwd(proj, dproj, dys[2], p["short_conv_w"])
    dproj, d_cw, d_cb, d_clg, d_clb = _mix_d_bwd(proj, dproj, dys[3], p["conf_dw_w"], p["conf_dw_b"], p["conf_ln_g"], p["conf_ln_b"])
    d_win = _inproj_bwd_w(sv["h"], dproj, dproj_bf)
    reduced_self, self_rider = None, None
    if reduce_self is not None:
        pc, chip = reduce_self
        mine, other = _row_halves(_grads_by_chip(d_win, d_wb, d_wo), pc)
        own_halves = _add_pairs(mine, _pair_exchange(other))
        self_rider = _chip_rider(own_halves)
    dx, dg8, from_chips_self = _inproj_bwd_x(dproj, p["w_in"], sv["x"], dxn, p["norm_g"], dproj_bf, p["w_bf"], rider=self_rider)
    if reduce_self is not None:
        reduced_self = _sum_chips(own_halves, chip, from_chips_self)
    grads = dict(
        norm_g=jnp.sum(dg8, axis=0), w_in=d_win, f_bias=d_fb[0, :N_HEADS], sgu_w=d_sgu_w,
        sgu_b=d_bias_rows.reshape(CHUNK, N_HEADS, HEAD_DIM).sum(axis=-1).T,
        sgu_ln_g=d_sgu_lg[0], sgu_ln_b=d_sgu_lb[0], short_conv_w=d_sc[:K_SHORT], conf_dw_w=d_cw[:K_CONF],
        conf_dw_b=d_cb[0], conf_ln_g=d_clg[0], conf_ln_b=d_clb[0], w_branch=d_wb, w_out=d_wo)
    return dx, grads, reduced, reduced_self


def _row_halves(arrays, pc):
    half = lambda a, i: lax.dynamic_slice_in_dim(a, i * (a.shape[-2] // 2), a.shape[-2] // 2, axis=a.ndim - 2)
    return [half(a, pc) for a in arrays], [half(a, 1 - pc) for a in arrays]


def _local_step(x, target, layers, final_g):
    saved = []
    for p in layers:
        x, sv, _ = _layer_fwd(x, p)
        saved.append(sv)
    loss8, dx, dfg8 = _loss_head(x, target, final_g)
    grads = [None] * len(layers)
    for l in reversed(range(len(layers))):
        dx, grads[l], _, _ = _layer_bwd(dx, layers[l], saved[l])
    return loss8[0, 0], dx, grads, jnp.sum(dfg8, axis=0)


def _gather_chips(blocks):
    n = len(blocks)

    def body(*refs):
        ins, outs, (send_sems, recv_sems) = refs[:n], refs[n:2 * n], refs[2 * n:]
        x, y, cc = _place()
        me, sibling = (x, y, cc), (x, y, 1 - cc)
        chips = _other_chips(x, y)

        def copy(a, k, chip, layer, to, src=None):
            dst = outs[a].at[2 * chip[0] + chip[1], layer]
            return pltpu.make_async_remote_copy(src_ref=dst if src is None else src, dst_ref=dst, send_sem=send_sems.at[a, k],
                                                recv_sem=recv_sems.at[a, k], device_id=to, device_id_type=MESH)

        first = [copy(a, j, (x, y), cc, (*chip, cc), src=ins[a]) for j, chip in enumerate(chips) for a in range(n)]
        for cp in first:
            cp.start()
        passed = []
        for j, chip in enumerate(chips):
            for a in range(n):
                copy(a, j, chip, cc, me).wait_recv()
                passed.append(copy(a, 3 + j, chip, cc, sibling))
                passed[-1].start()
        for j, chip in enumerate(chips):
            for a in range(n):
                copy(a, 3 + j, chip, 1 - cc, me).wait_recv()
        for cp in first + passed:
            cp.wait_send()

    return pl.pallas_call(
        body, name="gather_chips", in_specs=[ANY] * n, out_specs=[ANY] * n,
        out_shape=[jax.ShapeDtypeStruct((N_CHIPS, 2) + b.shape, b.dtype) for b in blocks],
        scratch_shapes=[pltpu.SemaphoreType.DMA((n, 6)), pltpu.SemaphoreType.DMA((n, 6))])(*blocks)


def _pair_exchange(arrays):
    n = len(arrays)

    def body(*refs):
        ins, outs, (send_sems, recv_sems) = refs[:n], refs[n:2 * n], refs[2 * n:]
        x, y, cc = _place()
        cps = [pltpu.make_async_remote_copy(src_ref=ins[a], dst_ref=outs[a], send_sem=send_sems.at[a], recv_sem=recv_sems.at[a],
                                            device_id=(x, y, 1 - cc), device_id_type=MESH) for a in range(n)]
        for cp in cps:
            cp.start()
        for cp in cps:
            cp.wait()

    return pl.pallas_call(
        body, name="pair_exchange", in_specs=[ANY] * n, out_specs=[ANY] * n,
        out_shape=[jax.ShapeDtypeStruct(a.shape, a.dtype) for a in arrays],
        scratch_shapes=[pltpu.SemaphoreType.DMA((n,)), pltpu.SemaphoreType.DMA((n,))])(*arrays)


def _allreduce8(arrays):
    n = len(arrays)

    def body(*refs):
        ins, outs, recvs = refs[:n], refs[n:2 * n], refs[2 * n:3 * n]
        send_sems, recv_sems = refs[3 * n:]
        x, y, cc = _place()
        for a in range(n):
            outs[a][...] = ins[a][...]
        for k, peer in enumerate([(x, y, 1 - cc), (1 - x, y, cc), (x, 1 - y, cc)]):
            cps = [pltpu.make_async_remote_copy(src_ref=outs[a], dst_ref=recvs[a].at[k], send_sem=send_sems.at[a, k],
                                                recv_sem=recv_sems.at[a, k], device_id=peer, device_id_type=MESH) for a in range(n)]
            for cp in cps:
                cp.start()
            for cp in cps:
                cp.wait()
            for a in range(n):
                outs[a][...] = outs[a][...] + recvs[a][k]

    return pl.pallas_call(
        body, name="allreduce8", in_specs=[VMEM] * n, out_specs=[VMEM] * n,
        out_shape=[jax.ShapeDtypeStruct(a.shape, F32) for a in arrays],
        scratch_shapes=[pltpu.VMEM((3,) + a.shape, F32) for a in arrays] + [pltpu.SemaphoreType.DMA((n, 3)), pltpu.SemaphoreType.DMA((n, 3))],
        compiler_params=_params())(*arrays)


def _row_tile(rows, cols, limit_bytes=1 << 20):
    t = rows
    while t % 16 == 0 and t * cols * 4 > limit_bytes:
        t //= 2
    return t


REDUCE_STEPS = 8


def _add_pairs(xs, ys):
    n = len(xs)

    def body(*refs):
        for a in range(n):
            refs[2 * n + a][...] = (refs[a][...].astype(F32) + refs[n + a][...].astype(F32)).astype(BF16)

    specs = [pl.BlockSpec((x.shape[0], x.shape[1] // REDUCE_STEPS, x.shape[2]), lambda i: (0, i, 0)) for x in xs]
    return pl.pallas_call(body, name="add_pairs", grid=(REDUCE_STEPS,), in_specs=specs + specs, out_specs=specs,
                          out_shape=[jax.ShapeDtypeStruct(x.shape, BF16) for x in xs], compiler_params=_params("arbitrary"))(*xs, *ys)


def _sum_chips(halves, chip, recvs):
    n = len(halves)

    def body(chip_ref, *refs):
        del chip_ref
        for a in range(n):
            acc = refs[a][0].astype(F32)
            for k in range(3):
                acc = acc + refs[n + a][k].astype(F32)
            refs[2 * n + a][...] = acc

    rows = [h.shape[1] // REDUCE_STEPS for h in halves]
    own_specs = [pl.BlockSpec((1, r, h.shape[2]), lambda i, c: (c[0], i, 0)) for h, r in zip(halves, rows)]
    recv_specs = [pl.BlockSpec((3, r, h.shape[2]), lambda i, c: (0, i, 0)) for h, r in zip(halves, rows)]
    out_specs = [pl.BlockSpec((r, h.shape[2]), lambda i, c: (i, 0)) for h, r in zip(halves, rows)]
    grid_spec = pltpu.PrefetchScalarGridSpec(num_scalar_prefetch=1, grid=(REDUCE_STEPS,), in_specs=own_specs + recv_specs,
                                             out_specs=out_specs)
    return pl.pallas_call(body, name="sum_chips", grid_spec=grid_spec,
                          out_shape=[jax.ShapeDtypeStruct(h.shape[1:], F32) for h in halves],
                          compiler_params=_params("arbitrary"))(jnp.reshape(chip, (1,)).astype(jnp.int32), *halves, *recvs)


def _adamw_update(w_ref, m_ref, v_ref, g_ref, d_ref, mo_ref, vo_ref):
    gv = g_ref[...]
    mn = ADAM_B1 * m_ref[...] + (1.0 - ADAM_B1) * gv
    vn = ADAM_B2 * v_ref[...] + (1.0 - ADAM_B2) * (gv * gv)
    m_hat = mn / (1.0 - ADAM_B1 ** ADAM_STEP)
    v_hat = vn / (1.0 - ADAM_B2 ** ADAM_STEP)
    d_ref[...] = -ADAM_LR * (m_hat / (jnp.sqrt(v_hat) + ADAM_EPS) + ADAM_WD * w_ref[...])
    mo_ref[...] = mn
    vo_ref[...] = vn


def _adamw(w, m, v, g):
    r, c = w.shape
    t = _row_tile(r, c)

    def body(*refs):
        _adamw_update(*refs)

    spec = pl.BlockSpec((t, c), lambda i: (i, 0))
    shape = jax.ShapeDtypeStruct((r, c), F32)
    return pl.pallas_call(body, name="adamw", grid=(r // t,), in_specs=[spec] * 4, out_specs=[spec] * 3,
                          out_shape=[shape] * 3, compiler_params=_params("arbitrary"))(w, m, v, g)


def _adamw_small(ws, ms, vs, gs):
    n = len(ws)

    def body(*refs):
        w_refs, m_refs, v_refs, g_refs = refs[:n], refs[n:2 * n], refs[2 * n:3 * n], refs[3 * n:4 * n]
        d_refs, mo_refs, vo_refs = refs[4 * n:5 * n], refs[5 * n:6 * n], refs[6 * n:]
        for a in range(n):
            _adamw_update(w_refs[a], m_refs[a], v_refs[a], g_refs[a], d_refs[a], mo_refs[a], vo_refs[a])

    shapes = [jax.ShapeDtypeStruct(w.shape, F32) for w in ws]
    outs = pl.pallas_call(body, name="adamw_small", in_specs=[VMEM] * (4 * n), out_specs=[VMEM] * (3 * n),
                          out_shape=shapes * 3, compiler_params=_params())(*ws, *ms, *vs, *gs)
    return outs[:n], outs[n:2 * n], outs[2 * n:]


SMALL =("norm_g", "f_bias", "sgu_w", "sgu_b", "sgu_ln_g", "sgu_ln_b", "short_conv_w", "conf_dw_w", "conf_dw_b",
         "conf_ln_g", "conf_ln_b", "final_g")
WEIGHTS = ("norm_g", "w_in", "f_bias", "sgu_w", "sgu_b", "sgu_ln_g", "sgu_ln_b", "short_conv_w", "conf_dw_w",
           "conf_dw_b", "conf_ln_g", "conf_ln_b", "w_branch", "w_out", "final_g")


def _aligned_pieces():
    out, pos = [], 0
    for src, width, dst in sorted(_SEGMENTS, key=lambda t: t[2]):
        if dst > pos:
            out.append((None, 0, dst - pos))
        lo = src
        while lo < src + width:
            j = lo // SHARD_COLS
            hi = min(src + width, (j + 1) * SHARD_COLS)
            out.append((j, lo - j * SHARD_COLS, hi - lo))
            lo = hi
        pos = dst + width
    return out


def _w_in_aligned(shards):
    rows, dtype = shards[0].shape[0], shards[0].dtype
    return jnp.concatenate([jnp.zeros((rows, w), dtype) if j is None else shards[j][:, c0:c0 + w]
                            for j, c0, w in _aligned_pieces()], axis=1)


def _w_bf(shards):
    j, c0 = BF_SRC // SHARD_COLS, BF_SRC % SHARD_COLS
    assert c0 + BF_COLS <= SHARD_COLS
    return jnp.pad(shards[j][:, c0:c0 + BF_COLS], ((0, 0), (0, CHUNK - BF_COLS)))


def _w_in_shard(g, g_bf, j):
    lo_s, hi_s = j * SHARD_COLS, (j + 1) * SHARD_COLS
    parts = []
    for src, width, dst in sorted(_SEGMENTS + ((BF_SRC, BF_COLS, None),), key=lambda t: t[0]):
        lo, hi = max(src, lo_s), min(src + width, hi_s)
        if lo < hi:
            parts.append(g_bf[:, lo - src:hi - src] if dst is None else g[:, dst + lo - src:dst + hi - src])
    return jnp.concatenate(parts + [jnp.zeros((g.shape[0], SHARD_PAD - SHARD_COLS), g.dtype)], axis=1)


def kernel(x, norm_g, w_in, f_bias, sgu_w, sgu_b, sgu_ln_g, sgu_ln_b, short_conv_w, conf_dw_w, conf_dw_b, conf_ln_g, conf_ln_b, w_branch, w_out, final_g, loss_target, m_norm_g, m_w_in, m_f_bias, m_sgu_w, m_sgu_b, m_sgu_ln_g, m_sgu_ln_b, m_short_conv_w, m_conf_dw_w, m_conf_dw_b, m_conf_ln_g, m_conf_ln_b, m_w_branch, m_w_out, m_final_g, v_norm_g, v_w_in, v_f_bias, v_sgu_w, v_sgu_b, v_sgu_ln_g, v_sgu_ln_b, v_short_conv_w, v_conf_dw_w, v_conf_dw_b, v_conf_ln_g, v_conf_ln_b, v_w_branch, v_w_out, v_final_g):
    px, py, pc = _place()
    chip = 2 * px + py
    depth = w_in.shape[0]
    w = dict(norm_g=norm_g, w_in=w_in, f_bias=f_bias, sgu_w=sgu_w, sgu_b=sgu_b, sgu_ln_g=sgu_ln_g, sgu_ln_b=sgu_ln_b,
             short_conv_w=short_conv_w, conf_dw_w=conf_dw_w, conf_dw_b=conf_dw_b, conf_ln_g=conf_ln_g, conf_ln_b=conf_ln_b,
             w_branch=w_branch, w_out=w_out, final_g=final_g)
    m = dict(norm_g=m_norm_g, w_in=m_w_in, f_bias=m_f_bias, sgu_w=m_sgu_w, sgu_b=m_sgu_b, sgu_ln_g=m_sgu_ln_g,
             sgu_ln_b=m_sgu_ln_b, short_conv_w=m_short_conv_w, conf_dw_w=m_conf_dw_w, conf_dw_b=m_conf_dw_b,
             conf_ln_g=m_conf_ln_g, conf_ln_b=m_conf_ln_b, w_branch=m_w_branch, w_out=m_w_out, final_g=m_final_g)
    v = dict(norm_g=v_norm_g, w_in=v_w_in, f_bias=v_f_bias, sgu_w=v_sgu_w, sgu_b=v_sgu_b, sgu_ln_g=v_sgu_ln_g,
             sgu_ln_b=v_sgu_ln_b, short_conv_w=v_short_conv_w, conf_dw_w=v_conf_dw_w, conf_dw_b=v_conf_dw_b,
             conf_ln_g=v_conf_ln_g, conf_ln_b=v_conf_ln_b, w_branch=v_w_branch, w_out=v_w_out, final_g=v_final_g)

    local = (jnp.pad(w_in, ((0, 0), (0, 0), (0, SHARD_PAD - SHARD_COLS))).astype(BF16),
             w_branch.astype(BF16).reshape(depth, N_HEADS * BR, BR), w_out.astype(BF16))
    pick = lambda a, i: lax.dynamic_index_in_dim(a, i, 0, keepdims=False)
    flat = lambda a: a.reshape(-1, a.shape[-1])

    def my_half(l):
        return _row_halves([a[l] for a in local], pc)[0]

    def all_chips(gathered, l):
        return [lax.dynamic_update_index_in_dim(got.reshape((N_CHIPS,) + a.shape[1:]), a[l], chip, 0) for got, a in zip(gathered, local)]

    conv_ch = BR // N_CHIPS
    place = lambda a: lax.dynamic_update_slice_in_dim(jnp.zeros(a.shape[:-1] + (BR,), F32), a * (pc == 0).astype(F32),
                                                      conv_ch * chip, axis=2).reshape(-1, BR)
    short_full, conf_full = _allreduce8([place(short_conv_w), place(conf_dw_w)])
    short_full, conf_full = short_full.reshape(depth, K_SHORT, BR), conf_full.reshape(depth, K_CONF, BR)

    def layer_params(l, shards):
        wi_all, wb_all, wo_all = shards
        return dict(
            norm_g=norm_g[l][None], w_in=_w_in_aligned([wi_all[j] for j in range(N_CHIPS)]),
            w_bf=_w_bf([wi_all[j] for j in range(N_CHIPS)]),
            f_bias=jnp.pad(f_bias[l], (0, CHUNK - N_HEADS))[None],
            sgu_w=sgu_w[l], sgu_b=sgu_b[l], sgu_ln_g=sgu_ln_g[l][None], sgu_ln_b=sgu_ln_b[l][None],
            short_conv_w=short_full[l], conf_dw_w=conf_full[l], conf_dw_b=conf_dw_b[l][None],
            conf_ln_g=conf_ln_g[l][None], conf_ln_b=conf_ln_b[l][None],
            w_branch=jnp.concatenate([wb_all[j].reshape(N_HEADS, BR, BR) for j in range(N_CHIPS)], axis=2),
            w_out=jnp.concatenate([wo_all[j] for j in range(N_CHIPS)], axis=0))

    shards = all_chips(_gather_chips(my_half(0)), 0)
    params, saved, xs = [], [], x[0]
    for l in range(depth):
        params.append(layer_params(l, shards))
        nxt = my_half(l + 1) if l + 1 < depth else None
        xs, sv, gathered = _layer_fwd(xs, params[l], next_blocks=nxt)
        saved.append(sv)
        if nxt is not None:
            shards = all_chips(gathered, l + 1)
    loss8, dx, dfg8 = _loss_head(xs, loss_target[0], final_g[None])
    d_final_g = jnp.sum(dfg8, axis=0)
    loss = lax.psum(loss8[0, 0], ("x", "y", "c"))

    grads, reduced, pending = [None] * depth, [None] * depth, None
    for l in reversed(range(depth)):
        dx, grads[l], red, red_self = _layer_bwd(
            dx, params[l], saved[l], prev_reduce=None if pending is None else (pc, chip, pending),
            reduce_self=(pc, chip) if l == 0 else None)
        if pending is not None:
            reduced[l + 1] = red
        if l == 0:
            reduced[0] = red_self
        else:
            pending = _grads_by_chip(grads[l]["w_in"], grads[l]["w_branch"], grads[l]["w_out"])
    grad_x = dx
    own = [a for red in reduced for a in red]
    rows = [jnp.where(pc == 0, jnp.concatenate([a, b], axis=0), jnp.concatenate([b, a], axis=0))
            for a, b in zip(own, _pair_exchange(own))]
    gi, gb, go = [jnp.stack([rows[3 * l + k] for l in range(depth)]) for k in range(3)]
    g = dict(w_in=gi[:, :, :SHARD_COLS], w_branch=gb.reshape(depth, N_HEADS, BR, BR), w_out=go)

    two_d = lambda a: a.reshape(-1, a.shape[-1])
    at_least_2d = lambda a: a if a.ndim >= 2 else a[None]
    small_local = [jnp.stack([grads[l][n] for l in range(depth)]) for n in SMALL[:-1]] + [d_final_g]
    for n, a, b in zip(SMALL, small_local, _allreduce8([at_least_2d(a) for a in small_local])):
        g[n] = b.reshape(a.shape)
    for n in ("short_conv_w", "conf_dw_w"):
        g[n] = lax.dynamic_slice_in_dim(g[n], conv_ch * chip, conv_ch, axis=2)

    delta, new_m, new_v = {}, {}, {}
    for n in ("w_in", "w_branch", "w_out"):
        shp = w[n].shape
        d_, m_, v_ = _adamw(two_d(w[n]), two_d(m[n]), two_d(v[n]), two_d(g[n]))
        delta[n], new_m[n], new_v[n] = d_.reshape(shp), m_.reshape(shp), v_.reshape(shp)
    d_, m_, v_ = _adamw_small(*[[at_least_2d(t[n]) for n in SMALL] for t in (w, m, v, g)])
    for n, a, b, c_ in zip(SMALL, d_, m_, v_):
        delta[n], new_m[n], new_v[n] = a.reshape(w[n].shape), b.reshape(w[n].shape), c_.reshape(w[n].shape)

    return (loss, grad_x[None], *[g[n] for n in WEIGHTS], *[delta[n] for n in WEIGHTS],
            *[new_m[n] for n in WEIGHTS], *[new_v[n] for n in WEIGHTS])
```

```python
import functools
import math

import jax
import jax.numpy as jnp
from jax import lax
from jax.experimental import pallas as pl
from jax.experimental.pallas import tpu as pltpu

F32 = jnp.float32
BF16 = jnp.bfloat16

D_MODEL = 1024
BR = 256
N_HEADS = 4
HEAD_DIM = 64
CHUNK = 128
K_SHORT = 3
K_CONF = 31
EPS = 1e-6
IN_COLS = 7684
SHARD_COLS = IN_COLS // 4
SHARD_PAD = 2048
N_CHIPS = 4

MRG0, MRG_COLS = 0, 4096
QKV0, CIN0, AUV0, DGLU0 = 4096, 4864, 5632, 6144
AG0, BG0, CG0, DG0 = 6656, 6912, 7168, 7424
PCOLS = 7680
BF_SRC, BF_COLS = 1536, 4

V7X_VMEM_BYTES = 64 * 1024 * 1024
VMEM_LIMIT = V7X_VMEM_BYTES * 7 // 8

ADAM_LR, ADAM_B1, ADAM_B2, ADAM_EPS, ADAM_WD, ADAM_STEP = 0.001, 0.9, 0.999, 1e-08, 0.01, 10

MESH = pl.DeviceIdType.MESH
ANY = pl.BlockSpec(memory_space=pl.ANY)
VMEM = pl.BlockSpec(memory_space=pltpu.VMEM)

GELU_C0 = math.sqrt(2.0 / math.pi)
GELU_C1 = 0.044715


def _params(*sem):
    return pltpu.CompilerParams(dimension_semantics=sem, vmem_limit_bytes=VMEM_LIMIT)


def _sigmoid(x):
    return 0.5 * jnp.tanh(0.5 * x) + 0.5


def _silu_and_grad(x):
    s = _sigmoid(x)
    return x * s, s * (1.0 + x * (1.0 - s))


def _gelu_and_grad(z):
    z2 = z * z
    t = jnp.tanh(GELU_C0 * (z + GELU_C1 * z2 * z))
    half = 0.5 * (1.0 + t)
    return z * half, half + 0.5 * z * (1.0 - t * t) * (GELU_C0 * (1.0 + 3.0 * GELU_C1 * z2))


def _ln_fwd(v):
    mu = jnp.mean(v, axis=-1, keepdims=True)
    xc = v - mu
    rs = lax.rsqrt(jnp.mean(xc * xc, axis=-1, keepdims=True) + EPS)
    return xc * rs, rs


def _ln_bwd(d_xh, xh, rs):
    return rs * (d_xh - jnp.mean(d_xh, axis=-1, keepdims=True) - xh * jnp.mean(d_xh * xh, axis=-1, keepdims=True))


def _part8(a):
    return a.reshape(a.shape[0] // 8, 8, a.shape[1]).sum(axis=0)


def _dot(a, b):
    return jnp.dot(a, b, preferred_element_type=F32)


def _dot_nt(a, b):
    return lax.dot_general(a, b, (((1,), (1,)), ((), ())), preferred_element_type=F32)


def _dot_tn(a, b):
    return lax.dot_general(a, b, (((0,), (0,)), ((), ())), preferred_element_type=F32)


def _head_masks(dtype):
    lane = lax.broadcasted_iota(jnp.int32, (1, BR), 1) // HEAD_DIM
    return [(lane == h).astype(dtype) for h in range(N_HEADS)]


def _window(ref, col0, width):
    return ref.at[:, pl.ds(col0, width)]


def _copy_all(pairs, sems):
    cps = [pltpu.make_async_copy(s, d, sems.at[i]) for i, (s, d) in enumerate(pairs)]
    for cp in cps:
        cp.start()
    for cp in cps:
        cp.wait()


def _place():
    return lax.axis_index("x"), lax.axis_index("y"), lax.axis_index("c")


def _other_chips(x, y):
    return [(1 - x, y), (x, 1 - y), (1 - x, 1 - y)]


class _Rider:
    def __init__(self, srcs, out_shapes, per_array, make, through=False):
        self.srcs, self.out_shapes, self.per_array, self.make, self.through = list(srcs), list(out_shapes), per_array, make, through
        self.n = len(self.srcs)

    def scratch(self):
        return [pltpu.SemaphoreType.DMA((self.n, self.per_array)), pltpu.SemaphoreType.DMA((self.n, self.per_array))]


def _ride(rider, first, last, src_refs, dst_refs, sems):
    if rider is None:
        return (lambda: None), (lambda: None)
    cps = rider.make(src_refs, dst_refs, *sems)

    def guarded(cond, fn):
        if cond is True:
            fn()
        else:
            pl.when(cond)(fn)

    def start():
        guarded(first, lambda: [cp.start() for cp in cps] and None)

    def finish():
        guarded(last, lambda: [cp.wait() for cp in cps] and None)

    return start, finish


def _rider_parts(rider, n_in, n_out):
    if rider is None:
        return [], [], [], [], [], {}
    aliases = {n_in + a: n_out + a for a in range(rider.n)} if rider.through else {}
    return rider.srcs, [ANY] * rider.n, [ANY] * rider.n, rider.out_shapes, rider.scratch(), aliases


def _remote(src, dst, send_sems, recv_sems, a, k, to):
    return pltpu.make_async_remote_copy(src_ref=src, dst_ref=dst, send_sem=send_sems.at[a, k], recv_sem=recv_sems.at[a, k],
                                        device_id=to, device_id_type=MESH)


def _gather_send_rider(blocks):
    def make(srcs, dsts, ss, rs):
        x, y, cc = _place()
        return [_remote(srcs[a], dsts[a].at[2 * x + y, cc], ss, rs, a, j, (*chip, cc))
                for j, chip in enumerate(_other_chips(x, y)) for a in range(len(srcs))]
    shapes = [jax.ShapeDtypeStruct((N_CHIPS, 2) + b.shape, b.dtype) for b in blocks]
    return _Rider(blocks, shapes, 3, make)


def _gather_forward_rider(landed):
    def make(srcs, dsts, ss, rs):
        x, y, cc = _place()
        return [_remote(dsts[a].at[2 * px + py, cc], dsts[a].at[2 * px + py, cc], ss, rs, a, j, (x, y, 1 - cc))
                for j, (px, py) in enumerate(_other_chips(x, y)) for a in range(len(dsts))]
    shapes = [jax.ShapeDtypeStruct(b.shape, b.dtype) for b in landed]
    return _Rider(landed, shapes, 3, make, through=True)


def _pair_rider(arrays):
    def make(srcs, dsts, ss, rs):
        x, y, cc = _place()
        return [_remote(srcs[a], dsts[a], ss, rs, a, 0, (x, y, 1 - cc)) for a in range(len(srcs))]
    return _Rider(arrays, [jax.ShapeDtypeStruct(b.shape, b.dtype) for b in arrays], 1, make)


def _chip_rider(arrays):
    def make(srcs, dsts, ss, rs):
        x, y, cc = _place()
        return [_remote(srcs[a].at[2 * px + py], dsts[a].at[k], ss, rs, a, k, (px, py, cc))
                for k, (px, py) in enumerate(_other_chips(x, y)) for a in range(len(srcs))]
    return _Rider(arrays, [jax.ShapeDtypeStruct((3,) + b.shape[1:], b.dtype) for b in arrays], 3, make)


INPROJ_TN = 1536


def _inproj_fwd(x, g, w, w_bf):
    s = x.shape[0]
    tm, tn = min(1024, s), INPROJ_TN

    def body(x_ref, g_ref, w_ref, wbf_ref, proj_ref, pbf_ref, h_ref):
        @pl.when(pl.program_id(1) == 0)
        def _():
            xv = x_ref[...]
            r = lax.rsqrt(jnp.mean(xv * xv, axis=-1, keepdims=True) + EPS)
            h_ref[...] = ((xv * r) * g_ref[...]).astype(BF16)
            pbf_ref[...] = _dot(h_ref[...], wbf_ref[...])
        proj_ref[...] = _dot(h_ref[...], w_ref[...])

    return pl.pallas_call(
        body, name="inproj_fwd", grid=(s // tm, PCOLS // tn),
        in_specs=[pl.BlockSpec((tm, D_MODEL), lambda i, j: (i, 0)), pl.BlockSpec((1, D_MODEL), lambda i, j: (0, 0)),
                  pl.BlockSpec((D_MODEL, tn), lambda i, j: (0, j)), pl.BlockSpec((D_MODEL, CHUNK), lambda i, j: (0, 0))],
        out_specs=[pl.BlockSpec((tm, tn), lambda i, j: (i, j)), pl.BlockSpec((tm, CHUNK), lambda i, j: (i, 0)),
                   pl.BlockSpec((tm, D_MODEL), lambda i, j: (i, 0))],
        out_shape=[jax.ShapeDtypeStruct((s, PCOLS), F32), jax.ShapeDtypeStruct((s, CHUNK), F32), jax.ShapeDtypeStruct((s, D_MODEL), BF16)],
        compiler_params=_params("arbitrary", "arbitrary"))(x, g, w, w_bf)


def _rms_bwd(dh, x, g):
    r = lax.rsqrt(jnp.mean(x * x, axis=-1, keepdims=True) + EPS)
    xn = x * r
    gy = dh * g
    dx = r * (gy - xn * jnp.mean(xn * gy, axis=-1, keepdims=True))
    return dx, _part8(dh * xn)


def _inproj_bwd_x(dproj, w, x, dxn, g, dproj_bf, w_bf, rider=None):
    s = x.shape[0]
    tm, tk = min(256, s), PCOLS
    nk = PCOLS // tk
    ni = s // tm
    r_in, r_in_specs, r_out_specs, r_shapes, r_scratch, r_alias = _rider_parts(rider, 7, 2)
    nr = len(r_in)

    def body(*refs):
        dp_ref, w_ref, x_ref, dxn_ref, g_ref, dpbf_ref, wbf_ref = refs[:7]
        dx_ref, dg_ref = refs[7 + nr:9 + nr]
        acc_ref = refs[9 + 2 * nr]
        i, k = pl.program_id(0), pl.program_id(1)
        start, finish = _ride(rider, (i == 0) & (k == 0), (i == ni - 1) & (k == nk - 1),
                              refs[7:7 + nr], refs[9 + nr:9 + 2 * nr], refs[10 + 2 * nr:])
        start()

        @pl.when(k == 0)
        def _():
            acc_ref[...] = _dot_nt(dpbf_ref[...], wbf_ref[...])

        @pl.when((i == 0) & (k == 0))
        def _():
            dg_ref[...] = jnp.zeros_like(dg_ref)

        acc_ref[...] += _dot_nt(dp_ref[...], w_ref[...])

        @pl.when(k == nk - 1)
        def _():
            dx, dg8 = _rms_bwd(acc_ref[...], x_ref[...], g_ref[...])
            dx_ref[...] = dxn_ref[...] + dx
            dg_ref[...] += dg8

        finish()

    outs = pl.pallas_call(
        body, name="inproj_bwd_x", grid=(ni, nk),
        in_specs=[pl.BlockSpec((tm, tk), lambda i, k: (i, k)), pl.BlockSpec((D_MODEL, tk), lambda i, k: (0, k)),
                  pl.BlockSpec((tm, D_MODEL), lambda i, k: (i, 0)), pl.BlockSpec((tm, D_MODEL), lambda i, k: (i, 0)),
                  pl.BlockSpec((1, D_MODEL), lambda i, k: (0, 0)), pl.BlockSpec((tm, CHUNK), lambda i, k: (i, 0)),
                  pl.BlockSpec((D_MODEL, CHUNK), lambda i, k: (0, 0))] + r_in_specs,
        out_specs=[pl.BlockSpec((tm, D_MODEL), lambda i, k: (i, 0)), pl.BlockSpec((8, D_MODEL), lambda i, k: (0, 0))] + r_out_specs,
        out_shape=[jax.ShapeDtypeStruct((s, D_MODEL), F32), jax.ShapeDtypeStruct((8, D_MODEL), F32)] + r_shapes,
        scratch_shapes=[pltpu.VMEM((tm, D_MODEL), F32)] + r_scratch, input_output_aliases=r_alias,
        compiler_params=_params("arbitrary", "arbitrary"))(dproj, w, x, dxn, g, dproj_bf, w_bf, *r_in)
    return outs[0], outs[1], outs[2:]


def _inproj_bwd_w(h, dproj, dproj_bf):
    s = h.shape[0]
    tn, tk = INPROJ_TN, min(1024, s)
    nk = s // tk

    def body(h_ref, dp_ref, dpbf_ref, dw_ref, dwbf_ref, acc_ref, accbf_ref):
        j, k = pl.program_id(0), pl.program_id(1)

        @pl.when(k == 0)
        def _():
            acc_ref[...] = jnp.zeros_like(acc_ref)

        acc_ref[...] += _dot_tn(h_ref[...], dp_ref[...])

        @pl.when(k == nk - 1)
        def _():
            dw_ref[...] = acc_ref[...].astype(BF16)

        @pl.when((j == 0) & (k == 0))
        def _():
            accbf_ref[...] = jnp.zeros_like(accbf_ref)

        @pl.when(j == 0)
        def _():
            accbf_ref[...] += _dot_tn(h_ref[...], dpbf_ref[...])

        @pl.when((j == 0) & (k == nk - 1))
        def _():
            dwbf_ref[...] = accbf_ref[...].astype(BF16)

    return pl.pallas_call(
        body, name="inproj_bwd_w", grid=(PCOLS // tn, nk),
        in_specs=[pl.BlockSpec((tk, D_MODEL), lambda j, k: (k, 0)), pl.BlockSpec((tk, tn), lambda j, k: (k, j)),
                  pl.BlockSpec((tk, CHUNK), lambda j, k: (k, 0))],
        out_specs=[pl.BlockSpec((D_MODEL, tn), lambda j, k: (0, j)), pl.BlockSpec((D_MODEL, CHUNK), lambda j, k: (0, 0))],
        out_shape=[jax.ShapeDtypeStruct((D_MODEL, PCOLS), BF16), jax.ShapeDtypeStruct((D_MODEL, CHUNK), BF16)],
        scratch_shapes=[pltpu.VMEM((D_MODEL, tn), F32), pltpu.VMEM((D_MODEL, CHUNK), F32)],
        compiler_params=_params("arbitrary", "arbitrary"))(h, dproj, dproj_bf)


def _mix_a_chunk(uvp, agp, wm_ref, bias, lg, lb):
    u, du = _gelu_and_grad(uvp[:, :BR])
    v, dv = _gelu_and_grad(uvp[:, BR:])
    xh, rs = _ln_fwd(v)
    vnb = (xh * lg + lb).astype(BF16)
    masks = _head_masks(BF16)
    mixed = bias
    for h in range(N_HEADS):
        mixed = mixed + _dot(wm_ref[h], vnb * masks[h])
    sg, dsg = _silu_and_grad(agp)
    return u, du, dv, xh, rs, vnb, masks, mixed, sg, dsg


def _store_masked_sgu(sw_ref, wm_ref):
    row = lax.broadcasted_iota(jnp.int32, (CHUNK, CHUNK), 0)
    col = lax.broadcasted_iota(jnp.int32, (CHUNK, CHUNK), 1)
    for h in range(N_HEADS):
        wm_ref[h] = jnp.where(row >= col, sw_ref[h], 0.0).astype(BF16)


def _mix_a_fwd(proj, sgu_w, bias, lg, lb):
    s = proj.shape[0]

    def body(proj_ref, sw_ref, bias_ref, lg_ref, lb_ref, y_ref, uv_buf, ag_buf, wm_ref, sems):
        _copy_all([(_window(proj_ref, AUV0, 2 * BR), uv_buf), (_window(proj_ref, AG0, BR), ag_buf)], sems)
        _store_masked_sgu(sw_ref, wm_ref)

        def chunk(i, c):
            rows = pl.ds(pl.multiple_of(i * CHUNK, CHUNK), CHUNK)
            u, _, _, _, _, _, _, mixed, sg, _ = _mix_a_chunk(uv_buf[rows, :], ag_buf[rows, :], wm_ref, bias_ref[...],
                                                            lg_ref[...], lb_ref[...])
            y_ref[rows, :] = (u * mixed * sg).astype(BF16)
            return c

        lax.fori_loop(0, s // CHUNK, chunk, 0)

    return pl.pallas_call(
        body, name="mix_a_fwd", in_specs=[ANY, VMEM, VMEM, VMEM, VMEM], out_specs=VMEM,
        out_shape=jax.ShapeDtypeStruct((s, BR), BF16),
        scratch_shapes=[pltpu.VMEM((s, 2 * BR), F32), pltpu.VMEM((s, BR), F32), pltpu.VMEM((N_HEADS, CHUNK, CHUNK), BF16),
                        pltpu.SemaphoreType.DMA((2,))],
        compiler_params=_params())(proj, sgu_w, bias, lg, lb)


def _mix_a_bwd(proj, dproj, dy, sgu_w, bias, lg, lb):
    s = proj.shape[0]

    def body(proj_ref, dproj_in, dy_ref, sw_ref, bias_ref, lg_ref, lb_ref,
             dproj_ref, dsw_ref, dbias_ref, dlg_ref, dlb_ref,
             uv_buf, ag_buf, duv_buf, dag_buf, wm_ref, acc_lg, acc_lb, sems):
        del dproj_in
        _copy_all([(_window(proj_ref, AUV0, 2 * BR), uv_buf), (_window(proj_ref, AG0, BR), ag_buf)], sems)
        _store_masked_sgu(sw_ref, wm_ref)
        dsw_ref[...] = jnp.zeros_like(dsw_ref)
        dbias_ref[...] = jnp.zeros_like(dbias_ref)
        acc_lg[...] = jnp.zeros_like(acc_lg)
        acc_lb[...] = jnp.zeros_like(acc_lb)

        def chunk(i, c):
            rows = pl.ds(pl.multiple_of(i * CHUNK, CHUNK), CHUNK)
            lg_v = lg_ref[...]
            u, du, dv, xh, rs, vnb, masks, mixed, sg, dsg = _mix_a_chunk(
                uv_buf[rows, :], ag_buf[rows, :], wm_ref, bias_ref[...], lg_v, lb_ref[...])
            dyv = dy_ref[rows, :]
            t1 = dyv * sg
            d_u = t1 * mixed
            d_mixed = t1 * u
            d_ag = dyv * u * mixed * dsg
            dbias_ref[...] += d_mixed
            dmb = d_mixed.astype(BF16)
            d_vn = jnp.zeros((CHUNK, BR), F32)
            for h in range(N_HEADS):
                dm_h = dmb * masks[h]
                dsw_ref[h] += _dot_nt(dm_h, vnb)
                d_vn = d_vn + _dot_tn(wm_ref[h], dm_h)
            acc_lg[...] += _part8(d_vn * xh)
            acc_lb[...] += _part8(d_vn)
            d_v = _ln_bwd(d_vn * lg_v, xh, rs)
            duv_buf[rows, :] = jnp.concatenate([d_u * du, d_v * dv], axis=1).astype(BF16)
            dag_buf[rows, :] = d_ag.astype(BF16)
            return c

        lax.fori_loop(0, s // CHUNK, chunk, 0)
        row = lax.broadcasted_iota(jnp.int32, (CHUNK, CHUNK), 0)
        col = lax.broadcasted_iota(jnp.int32, (CHUNK, CHUNK), 1)
        for h in range(N_HEADS):
            dsw_ref[h] = jnp.where(row >= col, dsw_ref[h], 0.0)
        dlg_ref[...] = jnp.sum(acc_lg[...], axis=0, keepdims=True)
        dlb_ref[...] = jnp.sum(acc_lb[...], axis=0, keepdims=True)
        _copy_all([(duv_buf, _window(dproj_ref, AUV0, 2 * BR)), (dag_buf, _window(dproj_ref, AG0, BR))], sems)

    return pl.pallas_call(
        body, name="mix_a_bwd", in_specs=[ANY, ANY, VMEM, VMEM, VMEM, VMEM, VMEM],
        out_specs=[ANY, VMEM, VMEM, VMEM, VMEM],
        out_shape=[jax.ShapeDtypeStruct(dproj.shape, dproj.dtype), jax.ShapeDtypeStruct((N_HEADS, CHUNK, CHUNK), F32),
                   jax.ShapeDtypeStruct((CHUNK, BR), F32), jax.ShapeDtypeStruct((1, BR), F32), jax.ShapeDtypeStruct((1, BR), F32)],
        scratch_shapes=[pltpu.VMEM((s, 2 * BR), F32), pltpu.VMEM((s, BR), F32), pltpu.VMEM((s, 2 * BR), BF16),
                        pltpu.VMEM((s, BR), BF16), pltpu.VMEM((N_HEADS, CHUNK, CHUNK), BF16),
                        pltpu.VMEM((8, BR), F32), pltpu.VMEM((8, BR), F32), pltpu.SemaphoreType.DMA((2,))],
        input_output_aliases={1: 0}, compiler_params=_params())(proj, dproj, dy, sgu_w, bias, lg, lb)


def _tri_ones(n, upper):
    row = lax.broadcasted_iota(jnp.int32, (n, n), 0)
    col = lax.broadcasted_iota(jnp.int32, (n, n), 1)
    return ((row <= col) if upper else (row >= col)).astype(BF16)


def _split3(c):
    hi = c.astype(BF16)
    r1 = c - hi.astype(F32)
    mid = r1.astype(BF16)
    lo = (r1 - mid.astype(F32)).astype(BF16)
    return [hi, mid, lo]


def _tri_sum(tri, a):
    parts = _split3(a)
    return _dot(tri, parts[0]) + _dot(tri, parts[1]) + _dot(tri, parts[2])


EXT = 2 * HEAD_DIM
LANE_CQ = HEAD_DIM
LANE_CK = HEAD_DIM + 3


def _to_head(h):
    r = lax.broadcasted_iota(jnp.int32, (BR, EXT), 0)
    c = lax.broadcasted_iota(jnp.int32, (BR, EXT), 1)
    return ((r == c + h * HEAD_DIM) & (c < HEAD_DIM)).astype(BF16)


def _from_head(h):
    r = lax.broadcasted_iota(jnp.int32, (EXT, BR), 0)
    c = lax.broadcasted_iota(jnp.int32, (EXT, BR), 1)
    return ((c == r + h * HEAD_DIM) & (r < HEAD_DIM)).astype(BF16)


def _attn_prep_fwd(proj, proj_bf, f_bias):
    s = proj.shape[0]
    scale = 1.0 / math.sqrt(HEAD_DIM)

    def body(proj_ref, z_buf, fb_ref, qe_ref, ke_ref, ve_ref, ket_ref, vet_ref, qkv_buf, sems):
        _copy_all([(_window(proj_ref, QKV0, 3 * BR), qkv_buf)], sems)
        tri = _tri_ones(CHUNK, upper=False)
        lane = lax.broadcasted_iota(jnp.int32, (CHUNK, EXT), 1)
        ones_q = ((lane >= LANE_CK) & (lane < LANE_CK + 3)).astype(F32)
        ones_k = ((lane >= LANE_CQ) & (lane < LANE_CQ + 3)).astype(F32)

        def chunk(i, carry):
            rows = pl.ds(pl.multiple_of(i * CHUNK, CHUNK), CHUNK)
            cum = _tri_sum(tri, jax.nn.log_sigmoid(z_buf[rows, :] + fb_ref[...])) + carry
            qb = (qkv_buf[rows, 0:BR] * scale).astype(BF16)
            kb = qkv_buf[rows, BR:2 * BR].astype(BF16)
            vb = qkv_buf[rows, 2 * BR:].astype(BF16)
            parts = [p.astype(F32) for p in _split3(cum)]
            for h in range(N_HEADS):
                sel = _to_head(h)
                dec_q, dec_k = ones_q, ones_k
                for t, part in enumerate(parts):
                    pf = part[:, h:h + 1]
                    dec_q = dec_q + jnp.where(lane == LANE_CQ + t, pf, 0.0)
                    dec_k = dec_k - jnp.where(lane == LANE_CK + t, pf, 0.0)
                qe_ref[h, rows, :] = (_dot(qb, sel) + dec_q).astype(BF16)
                kh = _dot(kb, sel) + dec_k
                vh = _dot(vb, sel)
                ke_ref[h, rows, :] = kh.astype(BF16)
                ve_ref[h, rows, :] = vh.astype(BF16)
                ket_ref[h, :, rows] = kh.T.astype(BF16)
                vet_ref[h, :, rows] = vh.T.astype(BF16)
            return cum[CHUNK - 1:CHUNK, :]

        lax.fori_loop(0, s // CHUNK, chunk, jnp.zeros((1, CHUNK), F32))

    shape = jax.ShapeDtypeStruct((N_HEADS, s, EXT), BF16)
    shape_t = jax.ShapeDtypeStruct((N_HEADS, EXT, s), BF16)
    return pl.pallas_call(
        body, name="attn_prep_fwd", in_specs=[ANY, VMEM, VMEM], out_specs=[VMEM] * 5, out_shape=[shape, shape, shape, shape_t, shape_t],
        scratch_shapes=[pltpu.VMEM((s, 3 * BR), F32), pltpu.SemaphoreType.DMA((1,))],
        compiler_params=_params())(proj, proj_bf, f_bias)


def _attn_prep_bwd(proj_bf, dproj, dqe, dke, dve, f_bias):
    s = proj_bf.shape[0]
    scale = 1.0 / math.sqrt(HEAD_DIM)

    def body(z_buf, dproj_in, dq_ref, dk_ref, dv_ref, fb_ref, dproj_ref, dz_buf, dfb_ref, dqkv_buf, sems):
        del dproj_in
        tri = _tri_ones(CHUNK, upper=True)
        lane = lax.broadcasted_iota(jnp.int32, (CHUNK, CHUNK), 1)
        n = s // CHUNK

        def chunk(t, carry):
            suffix, acc = carry
            i = n - 1 - t
            rows = pl.ds(pl.multiple_of(i * CHUNK, CHUNK), CHUNK)
            dq = jnp.zeros((CHUNK, BR), F32)
            dk = jnp.zeros((CHUNK, BR), F32)
            dv = jnp.zeros((CHUNK, BR), F32)
            dcum = jnp.zeros((CHUNK, CHUNK), F32)
            for h in range(N_HEADS):
                back = _from_head(h)
                dqh = dq_ref[h, :, rows].T
                dkh = dk_ref[h, rows, :]
                dq = dq + _dot((dqh * scale).astype(BF16), back)
                dk = dk + _dot(dkh.astype(BF16), back)
                dv = dv + _dot(dv_ref[h, rows, :].astype(BF16), back)
                dcum = dcum + jnp.where(lane == h, dqh[:, LANE_CQ:LANE_CQ + 1] - dkh[:, LANE_CK:LANE_CK + 1], 0.0)
            dqkv_buf[rows, 0:BR] = dq.astype(BF16)
            dqkv_buf[rows, BR:2 * BR] = dk.astype(BF16)
            dqkv_buf[rows, 2 * BR:] = dv.astype(BF16)
            dlf = _tri_sum(tri, dcum) + suffix
            dz = dlf * _sigmoid(-(z_buf[rows, :] + fb_ref[...]))
            dz_buf[rows, :] = dz.astype(BF16)
            return dlf[0:1, :], acc + _part8(dz)

        _, acc = lax.fori_loop(0, n, chunk, (jnp.zeros((1, CHUNK), F32), jnp.zeros((8, CHUNK), F32)))
        dfb_ref[...] = jnp.sum(acc, axis=0, keepdims=True)
        _copy_all([(dqkv_buf, _window(dproj_ref, QKV0, 3 * BR))], sems)

    return pl.pallas_call(
        body, name="attn_prep_bwd", in_specs=[VMEM, ANY, VMEM, VMEM, VMEM, VMEM], out_specs=[ANY, VMEM, VMEM],
        out_shape=[jax.ShapeDtypeStruct(dproj.shape, dproj.dtype), jax.ShapeDtypeStruct((s, CHUNK), BF16),
                   jax.ShapeDtypeStruct((1, CHUNK), F32)],
        scratch_shapes=[pltpu.VMEM((s, 3 * BR), BF16), pltpu.SemaphoreType.DMA((1,))],
        input_output_aliases={1: 0}, compiler_params=_params())(proj_bf, dproj, dqe, dke, dve, f_bias)


ATT_TQ = 256
ATT_FWD_GROUP = 8
ATT_BWD_GROUP = 8
NEG_BIG = -1e30


def _causal_t(q0, k0, tk, tq):
    kpos = k0 + lax.broadcasted_iota(jnp.int32, (tk, tq), 0)
    qpos = q0 + lax.broadcasted_iota(jnp.int32, (tk, tq), 1)
    return kpos <= qpos


STAT_ROWS = 8


def _attn_fwd(qe, ke, vet, rider=None):
    nh, s, w = qe.shape
    tq = min(ATT_TQ, s)
    tk = min(ATT_FWD_GROUP * tq, s)
    per = tk // tq
    nq = s // tq
    r_in, r_in_specs, r_out_specs, r_shapes, r_scratch, r_alias = _rider_parts(rider, 3, 2)
    nr = len(r_in)

    def body(*refs):
        q_ref, k_ref, vt_ref = refs[:3]
        ot_ref, lse_ref = refs[3 + nr:5 + nr]
        h, i = pl.program_id(0), pl.program_id(1)
        start, finish = _ride(rider, (h == 0) & (i == 0), (h == nh - 1) & (i == nq - 1),
                              refs[3:3 + nr], refs[5 + nr:5 + 2 * nr], refs[5 + 2 * nr:])
        start()
        q = q_ref[0]

        def step(k0, carry, width, masked):
            m, l, acc = carry
            ks = pl.ds(pl.multiple_of(k0, tq), width)
            st = _dot_nt(k_ref[0, ks, :], q)
            if masked:
                st = jnp.where(_causal_t(i * tq, k0, width, tq), st, NEG_BIG)
            m_new = jnp.maximum(m, jnp.max(st, axis=0, keepdims=True))
            alpha = jnp.exp(m - m_new)
            pt = jnp.exp(st - m_new)
            l = alpha * l + jnp.sum(pt, axis=0, keepdims=True)
            acc = alpha * acc + _dot(vt_ref[0, :, ks], pt.astype(BF16))
            return m_new, l, acc

        full = i // per
        init = (jnp.full((1, tq), NEG_BIG, F32), jnp.zeros((1, tq), F32), jnp.zeros((w, tq), F32))
        carry = lax.fori_loop(0, full, lambda j, c: step(j * tk, c, tk, False), init)
        m, l, acc = lax.switch(i % per, [functools.partial(step, width=(r + 1) * tq, masked=True) for r in range(per)],
                               full * tk, carry)
        ot_ref[0] = acc / l
        lse_ref[0] = jnp.broadcast_to(m + jnp.log(l), (STAT_ROWS, tq))
        finish()

    outs = pl.pallas_call(
        body, name="attn_fwd", grid=(nh, nq),
        in_specs=[pl.BlockSpec((1, tq, w), lambda h, i: (h, i, 0)), pl.BlockSpec((1, s, w), lambda h, i: (h, 0, 0)),
                  pl.BlockSpec((1, w, s), lambda h, i: (h, 0, 0))] + r_in_specs,
        out_specs=[pl.BlockSpec((1, w, tq), lambda h, i: (h, 0, i)), pl.BlockSpec((1, STAT_ROWS, tq), lambda h, i: (h, 0, i))] + r_out_specs,
        out_shape=[jax.ShapeDtypeStruct((nh, w, s), F32), jax.ShapeDtypeStruct((nh, STAT_ROWS, s), F32)] + r_shapes,
        scratch_shapes=r_scratch, input_output_aliases=r_alias,
        compiler_params=_params("arbitrary", "arbitrary"))(qe, ke, vet, *r_in)
    return outs[0], outs[1], outs[2:]


def _attn_bwd(qe, ke, ket, ve, doe, lse, dd, rider=None):
    nh, s, w = qe.shape
    tq = min(ATT_TQ, s)
    tk = min(ATT_BWD_GROUP * tq, s)
    per = tk // tq
    nq = s // tq
    r_in, r_in_specs, r_out_specs, r_shapes, r_scratch, r_alias = _rider_parts(rider, 7, 3)
    nr = len(r_in)

    def body(*refs):
        q_ref, do_ref, lse_ref, dd_ref, k_ref, kt_ref, v_ref = refs[:7]
        dqt_ref, dk_ref, dv_ref = refs[7 + nr:10 + nr]
        h, i = pl.program_id(0), pl.program_id(1)
        start, finish = _ride(rider, (h == 0) & (i == 0), (h == nh - 1) & (i == nq - 1),
                              refs[7:7 + nr], refs[10 + nr:10 + 2 * nr], refs[10 + 2 * nr:])
        start()

        @pl.when(i == 0)
        def _():
            dk_ref[...] = jnp.zeros_like(dk_ref)
            dv_ref[...] = jnp.zeros_like(dv_ref)

        q = q_ref[0]
        do = do_ref[0]
        lse_row = lse_ref[0, 0:1, :]
        dd_row = dd_ref[0, 0:1, :]

        def step(k0, dqt, width, masked):
            ks = pl.ds(pl.multiple_of(k0, tq), width)
            st = _dot_nt(k_ref[0, ks, :], q)
            pt = jnp.exp(st - lse_row)
            if masked:
                pt = jnp.where(_causal_t(i * tq, k0, width, tq), pt, 0.0)
            dpt = _dot_nt(v_ref[0, ks, :], do)
            dst = (pt * (dpt - dd_row)).astype(BF16)
            dv_ref[0, ks, :] += _dot(pt.astype(BF16), do)
            dk_ref[0, ks, :] += _dot(dst, q)
            return dqt + _dot(kt_ref[0, :, ks], dst)

        full = i // per
        dqt = lax.fori_loop(0, full, lambda j, c: step(j * tk, c, tk, False), jnp.zeros((w, tq), F32))
        dqt_ref[0] = lax.switch(i % per, [functools.partial(step, width=(r + 1) * tq, masked=True) for r in range(per)],
                                full * tk, dqt)
        finish()

    qblk = pl.BlockSpec((1, tq, w), lambda h, i: (h, i, 0))
    stat = pl.BlockSpec((1, STAT_ROWS, tq), lambda h, i: (h, 0, i))
    whole = pl.BlockSpec((1, s, w), lambda h, i: (h, 0, 0))
    whole_t = pl.BlockSpec((1, w, s), lambda h, i: (h, 0, 0))
    outs = pl.pallas_call(
        body, name="attn_bwd", grid=(nh, nq),
        in_specs=[qblk, qblk, stat, stat, whole, whole_t, whole] + r_in_specs,
        out_specs=[pl.BlockSpec((1, w, tq), lambda h, i: (h, 0, i)), whole, whole] + r_out_specs,
        out_shape=[jax.ShapeDtypeStruct((nh, w, s), F32), jax.ShapeDtypeStruct((nh, s, w), F32),
                   jax.ShapeDtypeStruct((nh, s, w), F32)] + r_shapes,
        scratch_shapes=r_scratch, input_output_aliases=r_alias,
        compiler_params=_params("arbitrary", "arbitrary"))(qe, doe, lse, dd, ke, ket, ve, *r_in)
    return outs[0], outs[1], outs[2], outs[3:]


def _bgate_fwd(proj, ot, rider=None):
    s = proj.shape[0]
    r_in, r_in_specs, r_out_specs, r_shapes, r_scratch, r_alias = _rider_parts(rider, 2, 2)
    nr = len(r_in)

    def body(*refs):
        proj_ref, ot_ref = refs[:2]
        y_ref, o_ref = refs[2 + nr:4 + nr]
        g_buf, sems = refs[4 + 2 * nr:6 + 2 * nr]
        start, finish = _ride(rider, True, True, refs[2:2 + nr], refs[4 + nr:4 + 2 * nr], refs[6 + 2 * nr:])
        start()
        _copy_all([(_window(proj_ref, BG0, BR), g_buf)], sems)

        def chunk(i, c):
            rows = pl.ds(pl.multiple_of(i * CHUNK, CHUNK), CHUNK)
            o = jnp.zeros((CHUNK, BR), F32)
            for h in range(N_HEADS):
                back = _from_head(h)
                for part in _split3(ot_ref[h, :, rows].T):
                    o = o + _dot(part, back)
            sg, _ = _silu_and_grad(g_buf[rows, :])
            o_ref[rows, :] = o
            y_ref[rows, :] = (o * sg).astype(BF16)
            return c

        lax.fori_loop(0, s // CHUNK, chunk, 0)
        finish()

    outs = pl.pallas_call(
        body, name="bgate_fwd", in_specs=[ANY, VMEM] + r_in_specs, out_specs=[VMEM, VMEM] + r_out_specs,
        out_shape=[jax.ShapeDtypeStruct((s, BR), BF16), jax.ShapeDtypeStruct((s, BR), F32)] + r_shapes,
        scratch_shapes=[pltpu.VMEM((s, BR), F32), pltpu.SemaphoreType.DMA((1,))] + r_scratch,
        input_output_aliases=r_alias, compiler_params=_params())(proj, ot, *r_in)
    return outs[0], outs[1], outs[2:]


def _bgate_bwd(proj, dproj, o, dy):
    s = proj.shape[0]

    def body(proj_ref, dproj_in, o_ref, dy_ref, dproj_ref, do_ref, dd_ref, g_buf, dg_buf, sems):
        del dproj_in
        _copy_all([(_window(proj_ref, BG0, BR), g_buf)], sems)
        lane = lax.broadcasted_iota(jnp.int32, (BR, CHUNK), 0) // HEAD_DIM
        col = lax.broadcasted_iota(jnp.int32, (BR, CHUNK), 1)
        sel = (lane == col).astype(BF16)

        def chunk(i, c):
            rows = pl.ds(pl.multiple_of(i * CHUNK, CHUNK), CHUNK)
            sg, dsg = _silu_and_grad(g_buf[rows, :])
            dyv = dy_ref[rows, :]
            ov = o_ref[rows, :]
            do = dyv * sg
            dg_buf[rows, :] = (dyv * ov * dsg).astype(BF16)
            prod = _split3(do * ov)
            ddt = (_dot(prod[0], sel) + _dot(prod[1], sel) + _dot(prod[2], sel)).T
            dob = do.astype(BF16)
            for h in range(N_HEADS):
                do_ref[h, rows, :] = _dot(dob, _to_head(h)).astype(BF16)
                dd_ref[h, :, rows] = jnp.broadcast_to(ddt[h:h + 1, :], (STAT_ROWS, CHUNK))
            return c

        lax.fori_loop(0, s // CHUNK, chunk, 0)
        _copy_all([(dg_buf, _window(dproj_ref, BG0, BR))], sems)

    return pl.pallas_call(
        body, name="bgate_bwd", in_specs=[ANY, ANY, VMEM, VMEM], out_specs=[ANY, VMEM, VMEM],
        out_shape=[jax.ShapeDtypeStruct(dproj.shape, dproj.dtype), jax.ShapeDtypeStruct((N_HEADS, s, EXT), BF16),
                   jax.ShapeDtypeStruct((N_HEADS, STAT_ROWS, s), F32)],
        scratch_shapes=[pltpu.VMEM((s, BR), F32), pltpu.VMEM((s, BR), BF16), pltpu.SemaphoreType.DMA((1,))],
        input_output_aliases={1: 0}, compiler_params=_params())(proj, dproj, o, dy)


C_PAD = 8


def _mix_c_fwd(proj, w):
    s = proj.shape[0]

    def body(proj_ref, w_ref, y_ref, cin_buf, g_buf, z_buf, sems):
        _copy_all([(_window(proj_ref, CIN0, 3 * BR), cin_buf), (_window(proj_ref, CG0, BR), g_buf)], sems)
        z_buf[pl.ds(0, C_PAD), :] = jnp.zeros((C_PAD, BR), F32)

        def fill(i, c):
            rows = pl.ds(pl.multiple_of(i * CHUNK, CHUNK), CHUNK)
            z_buf[pl.ds(pl.multiple_of(i * CHUNK + C_PAD, 8), CHUNK), :] = cin_buf[rows, BR:2 * BR] * cin_buf[rows, 2 * BR:]
            return c

        lax.fori_loop(0, s // CHUNK, fill, 0)

        def chunk(i, c):
            r0 = pl.multiple_of(i * CHUNK, CHUNK)
            rows = pl.ds(r0, CHUNK)
            ze = z_buf[pl.ds(r0, CHUNK + C_PAD), :]
            conv = jnp.zeros((CHUNK, BR), F32)
            for k in range(K_SHORT):
                off = C_PAD - (K_SHORT - 1) + k
                conv = conv + w_ref[k:k + 1, :] * ze[off:off + CHUNK]
            sg, _ = _silu_and_grad(g_buf[rows, :])
            y_ref[rows, :] = (cin_buf[rows, 0:BR] * conv * sg).astype(BF16)
            return c

        lax.fori_loop(0, s // CHUNK, chunk, 0)

    return pl.pallas_call(
        body, name="mix_c_fwd", in_specs=[ANY, VMEM], out_specs=VMEM, out_shape=jax.ShapeDtypeStruct((s, BR), BF16),
        scratch_shapes=[pltpu.VMEM((s, 3 * BR), F32), pltpu.VMEM((s, BR), F32), pltpu.VMEM((s + C_PAD, BR), F32),
                        pltpu.SemaphoreType.DMA((2,))],
        compiler_params=_params())(proj, w)


def _mix_c_bwd(proj, dproj, dy, w):
    s = proj.shape[0]

    def body(proj_ref, dproj_in, dy_ref, w_ref, dproj_ref, dw_ref, cin_buf, g_buf, z_buf, dc_buf, dcin_buf, dg_buf, acc, sems):
        del dproj_in
        _copy_all([(_window(proj_ref, CIN0, 3 * BR), cin_buf), (_window(proj_ref, CG0, BR), g_buf)], sems)
        z_buf[pl.ds(0, C_PAD), :] = jnp.zeros((C_PAD, BR), F32)
        dc_buf[pl.ds(s, C_PAD), :] = jnp.zeros((C_PAD, BR), F32)
        acc[...] = jnp.zeros_like(acc)

        def fill(i, c):
            rows = pl.ds(pl.multiple_of(i * CHUNK, CHUNK), CHUNK)
            z_buf[pl.ds(pl.multiple_of(i * CHUNK + C_PAD, 8), CHUNK), :] = cin_buf[rows, BR:2 * BR] * cin_buf[rows, 2 * BR:]
            return c

        lax.fori_loop(0, s // CHUNK, fill, 0)

        def chunk(i, c):
            r0 = pl.multiple_of(i * CHUNK, CHUNK)
            rows = pl.ds(r0, CHUNK)
            ze = z_buf[pl.ds(r0, CHUNK + C_PAD), :]
            taps = [ze[C_PAD - (K_SHORT - 1) + k:C_PAD - (K_SHORT - 1) + k + CHUNK] for k in range(K_SHORT)]
            conv = jnp.zeros((CHUNK, BR), F32)
            for k in range(K_SHORT):
                conv = conv + w_ref[k:k + 1, :] * taps[k]
            sg, dsg = _silu_and_grad(g_buf[rows, :])
            bg = cin_buf[rows, 0:BR]
            dyv = dy_ref[rows, :]
            dconv = dyv * bg * sg
            dc_buf[rows, :] = dconv
            dcin_buf[rows, 0:BR] = (dyv * conv * sg).astype(BF16)
            dg_buf[rows, :] = (dyv * bg * conv * dsg).astype(BF16)
            for k in range(K_SHORT):
                acc[k] += _part8(dconv * taps[k])
            return c

        lax.fori_loop(0, s // CHUNK, chunk, 0)

        def chunk2(i, c):
            r0 = pl.multiple_of(i * CHUNK, CHUNK)
            rows = pl.ds(r0, CHUNK)
            de = dc_buf[pl.ds(r0, CHUNK + C_PAD), :]
            dz = jnp.zeros((CHUNK, BR), F32)
            for k in range(K_SHORT):
                off = K_SHORT - 1 - k
                dz = dz + w_ref[k:k + 1, :] * de[off:off + CHUNK]
            dcin_buf[rows, BR:2 * BR] = (dz * cin_buf[rows, 2 * BR:]).astype(BF16)
            dcin_buf[rows, 2 * BR:] = (dz * cin_buf[rows, BR:2 * BR]).astype(BF16)
            return c

        lax.fori_loop(0, s // CHUNK, chunk2, 0)
        dw_ref[...] = jnp.zeros_like(dw_ref)
        for k in range(K_SHORT):
            dw_ref[k:k + 1, :] = jnp.sum(acc[k], axis=0, keepdims=True)
        _copy_all([(dcin_buf, _window(dproj_ref, CIN0, 3 * BR)), (dg_buf, _window(dproj_ref, CG0, BR))], sems)

    return pl.pallas_call(
        body, name="mix_c_bwd", in_specs=[ANY, ANY, VMEM, VMEM], out_specs=[ANY, VMEM],
        out_shape=[jax.ShapeDtypeStruct(dproj.shape, dproj.dtype), jax.ShapeDtypeStruct((8, BR), F32)],
        scratch_shapes=[pltpu.VMEM((s, 3 * BR), F32), pltpu.VMEM((s, BR), F32), pltpu.VMEM((s + C_PAD, BR), F32),
                        pltpu.VMEM((s + C_PAD, BR), F32), pltpu.VMEM((s, 3 * BR), BF16), pltpu.VMEM((s, BR), BF16),
                        pltpu.VMEM((K_SHORT, 8, BR), F32), pltpu.SemaphoreType.DMA((2,))],
        input_output_aliases={1: 0}, compiler_params=_params())(proj, dproj, dy, w)


D_PAD = 32


def _tap_windows(win, offsets, n):
    rows = win.shape[0]
    rolled, out = {}, []
    for off in offsets:
        r = off % 8
        if r not in rolled:
            rolled[r] = win if r == 0 else pltpu.roll(win, rows - r, 0)
        out.append(rolled[r][off - r:off - r + n])
    return out


def _mix_d_common(g_ref, w_ref, b_ref, lg_ref, lb_ref, hh_buf, r0):
    he = hh_buf[pl.ds(r0, CHUNK + D_PAD), :]
    taps = _tap_windows(he, [D_PAD - (K_CONF - 1) + k for k in range(K_CONF)], CHUNK)
    conv = jnp.zeros((CHUNK, BR), F32) + b_ref[...]
    for k in range(K_CONF):
        conv = conv + w_ref[k:k + 1, :] * taps[k]
    xh, rs = _ln_fwd(conv)
    sw, dsw = _silu_and_grad(xh * lg_ref[...] + lb_ref[...])
    sg, dsg = _silu_and_grad(g_ref[pl.ds(r0, CHUNK), :])
    return taps, xh, rs, sw, dsw, sg, dsg


def _mix_d_fill(cin_buf, hh_buf, s):
    hh_buf[pl.ds(0, D_PAD), :] = jnp.zeros((D_PAD, BR), F32)

    def fill(i, c):
        rows = pl.ds(pl.multiple_of(i * CHUNK, CHUNK), CHUNK)
        hh_buf[pl.ds(pl.multiple_of(i * CHUNK + D_PAD, 8), CHUNK), :] = cin_buf[rows, 0:BR] * _sigmoid(cin_buf[rows, BR:])
        return c

    lax.fori_loop(0, s // CHUNK, fill, 0)


def _mix_d_fwd(proj, w, b, lg, lb):
    s = proj.shape[0]

    def body(proj_ref, w_ref, b_ref, lg_ref, lb_ref, y_ref, cin_buf, g_buf, hh_buf, sems):
        _copy_all([(_window(proj_ref, DGLU0, 2 * BR), cin_buf), (_window(proj_ref, DG0, BR), g_buf)], sems)
        _mix_d_fill(cin_buf, hh_buf, s)

        def chunk(i, c):
            r0 = pl.multiple_of(i * CHUNK, CHUNK)
            _, _, _, sw, _, sg, _ = _mix_d_common(g_buf, w_ref, b_ref, lg_ref, lb_ref, hh_buf, r0)
            y_ref[pl.ds(r0, CHUNK), :] = (sw * sg).astype(BF16)
            return c

        lax.fori_loop(0, s // CHUNK, chunk, 0)

    return pl.pallas_call(
        body, name="mix_d_fwd", in_specs=[ANY, VMEM, VMEM, VMEM, VMEM], out_specs=VMEM,
        out_shape=jax.ShapeDtypeStruct((s, BR), BF16),
        scratch_shapes=[pltpu.VMEM((s, 2 * BR), F32), pltpu.VMEM((s, BR), F32), pltpu.VMEM((s + D_PAD, BR), F32),
                        pltpu.SemaphoreType.DMA((2,))],
        compiler_params=_params())(proj, w, b, lg, lb)


def _mix_d_bwd(proj, dproj, dy, w, b, lg, lb):
    s = proj.shape[0]

    def body(proj_ref, dproj_in, dy_ref, w_ref, b_ref, lg_ref, lb_ref, dproj_ref, dw_ref, db_ref, dlg_ref, dlb_ref,
             cin_buf, g_buf, hh_buf, dc_buf, dcin_buf, dg_buf, acc_w, acc_s, sems):
        del dproj_in
        _copy_all([(_window(proj_ref, DGLU0, 2 * BR), cin_buf), (_window(proj_ref, DG0, BR), g_buf)], sems)
        _mix_d_fill(cin_buf, hh_buf, s)
        dc_buf[pl.ds(s, D_PAD), :] = jnp.zeros((D_PAD, BR), F32)
        acc_w[...] = jnp.zeros_like(acc_w)
        acc_s[...] = jnp.zeros_like(acc_s)

        def chunk(i, c):
            r0 = pl.multiple_of(i * CHUNK, CHUNK)
            rows = pl.ds(r0, CHUNK)
            taps, xh, rs, sw, dsw, sg, dsg = _mix_d_common(g_buf, w_ref, b_ref, lg_ref, lb_ref, hh_buf, r0)
            dyv = dy_ref[rows, :]
            dg_buf[rows, :] = (dyv * sw * dsg).astype(BF16)
            d_ln = dyv * sg * dsw
            acc_s[0] += _part8(d_ln * xh)
            acc_s[1] += _part8(d_ln)
            dc = _ln_bwd(d_ln * lg_ref[...], xh, rs)
            acc_s[2] += _part8(dc)
            dc_buf[rows, :] = dc
            for k in range(K_CONF):
                acc_w[k] += _part8(dc * taps[k])
            return c

        lax.fori_loop(0, s // CHUNK, chunk, 0)

        def chunk2(i, c):
            r0 = pl.multiple_of(i * CHUNK, CHUNK)
            rows = pl.ds(r0, CHUNK)
            de = dc_buf[pl.ds(r0, CHUNK + D_PAD), :]
            dh = jnp.zeros((CHUNK, BR), F32)
            for k, win in enumerate(_tap_windows(de, [K_CONF - 1 - k for k in range(K_CONF)], CHUNK)):
                dh = dh + w_ref[k:k + 1, :] * win
            a = cin_buf[rows, 0:BR]
            sig = _sigmoid(cin_buf[rows, BR:])
            dcin_buf[rows, 0:BR] = (dh * sig).astype(BF16)
            dcin_buf[rows, BR:] = (dh * a * sig * (1.0 - sig)).astype(BF16)
            return c

        lax.fori_loop(0, s // CHUNK, chunk2, 0)
        dw_ref[...] = jnp.zeros_like(dw_ref)
        for k in range(K_CONF):
            dw_ref[k:k + 1, :] = jnp.sum(acc_w[k], axis=0, keepdims=True)
        dlg_ref[...] = jnp.sum(acc_s[0], axis=0, keepdims=True)
        dlb_ref[...] = jnp.sum(acc_s[1], axis=0, keepdims=True)
        db_ref[...] = jnp.sum(acc_s[2], axis=0, keepdims=True)
        _copy_all([(dcin_buf, _window(dproj_ref, DGLU0, 2 * BR)), (dg_buf, _window(dproj_ref, DG0, BR))], sems)

    vec = jax.ShapeDtypeStruct((1, BR), F32)
    return pl.pallas_call(
        body, name="mix_d_bwd", in_specs=[ANY, ANY, VMEM, VMEM, VMEM, VMEM, VMEM], out_specs=[ANY, VMEM, VMEM, VMEM, VMEM],
        out_shape=[jax.ShapeDtypeStruct(dproj.shape, dproj.dtype), jax.ShapeDtypeStruct((32, BR), F32), vec, vec, vec],
        scratch_shapes=[pltpu.VMEM((s, 2 * BR), F32), pltpu.VMEM((s, BR), F32), pltpu.VMEM((s + D_PAD, BR), F32),
                        pltpu.VMEM((s + D_PAD, BR), F32), pltpu.VMEM((s, 2 * BR), BF16), pltpu.VMEM((s, BR), BF16),
                        pltpu.VMEM((K_CONF, 8, BR), F32), pltpu.VMEM((3, 8, BR), F32), pltpu.SemaphoreType.DMA((2,))],
        input_output_aliases={1: 0}, compiler_params=_params())(proj, dproj, dy, w, b, lg, lb)


MERGE_TM = 256


def _merge_fwd(x, proj, ys, wb, wo):
    s = x.shape[0]
    tm = min(MERGE_TM, s)

    def body(x_ref, lg_ref, ya, yb, yc, yd, wb_ref, wo_ref, xn_ref, mg_ref):
        merged = jnp.zeros((tm, D_MODEL), F32)
        for n, y_ref in enumerate((ya, yb, yc, yd)):
            gate = _sigmoid(lg_ref[:, n * D_MODEL:(n + 1) * D_MODEL])
            merged = merged + gate * _dot(y_ref[...], wb_ref[n])
        mb = merged.astype(BF16)
        mg_ref[...] = mb
        xn_ref[...] = x_ref[...] + _dot(mb, wo_ref[...])

    row = lambda w: pl.BlockSpec((tm, w), lambda i: (i, 0))
    return pl.pallas_call(
        body, name="merge_fwd", grid=(s // tm,),
        in_specs=[row(D_MODEL), pl.BlockSpec((tm, MRG_COLS), lambda i: (i, MRG0 // MRG_COLS)), row(BR), row(BR), row(BR), row(BR),
                  pl.BlockSpec((N_HEADS, BR, D_MODEL), lambda i: (0, 0, 0)), pl.BlockSpec((D_MODEL, D_MODEL), lambda i: (0, 0))],
        out_specs=[row(D_MODEL), row(D_MODEL)],
        out_shape=[jax.ShapeDtypeStruct((s, D_MODEL), F32), jax.ShapeDtypeStruct((s, D_MODEL), BF16)],
        compiler_params=_params("arbitrary"))(x, proj, *ys, wb, wo)


def _merge_bwd(dxn, proj, ys, merged, wb, wo, rider=None):
    s = dxn.shape[0]
    tm = min(MERGE_TM, s)
    steps = s // tm
    r_in, r_in_specs, r_out_specs, r_shapes, r_scratch, r_alias = _rider_parts(rider, 9, 7)
    nr = len(r_in)

    def body(*refs):
        dx_ref, lg_ref, ya, yb, yc, yd, mg_ref, wb_ref, wo_ref = refs[:9]
        dlg_ref, da, db, dc, dd, dwo_ref, dwb_ref = refs[9 + nr:16 + nr]
        start, finish = _ride(rider, pl.program_id(0) == 0, pl.program_id(0) == steps - 1,
                              refs[9:9 + nr], refs[16 + nr:16 + 2 * nr], refs[16 + 2 * nr:])
        start()

        @pl.when(pl.program_id(0) == 0)
        def _():
            dwo_ref[...] = jnp.zeros_like(dwo_ref)
            dwb_ref[...] = jnp.zeros_like(dwb_ref)

        dxb = dx_ref[...].astype(BF16)
        d_merged = _dot_nt(dxb, wo_ref[...])
        dwo_ref[...] += _dot_tn(mg_ref[...], dxb)
        for n, (y_ref, dy_ref) in enumerate(((ya, da), (yb, db), (yc, dc), (yd, dd))):
            yv = y_ref[...]
            gate = _sigmoid(lg_ref[:, n * D_MODEL:(n + 1) * D_MODEL])
            pn = _dot(yv, wb_ref[n])
            dlg_ref[:, n * D_MODEL:(n + 1) * D_MODEL] = (d_merged * pn * gate * (1.0 - gate)).astype(BF16)
            dpn = (d_merged * gate).astype(BF16)
            dy_ref[...] = _dot_nt(dpn, wb_ref[n])
            dwb_ref[n] += _dot_tn(yv, dpn)
        finish()

    row = lambda w: pl.BlockSpec((tm, w), lambda i: (i, 0))
    wb_spec = pl.BlockSpec((N_HEADS, BR, D_MODEL), lambda i: (0, 0, 0))
    wo_spec = pl.BlockSpec((D_MODEL, D_MODEL), lambda i: (0, 0))
    dy_shape = jax.ShapeDtypeStruct((s, BR), F32)
    outs = pl.pallas_call(
        body, name="merge_bwd", grid=(steps,),
        in_specs=[row(D_MODEL), pl.BlockSpec((tm, MRG_COLS), lambda i: (i, MRG0 // MRG_COLS)), row(BR), row(BR), row(BR), row(BR), row(D_MODEL),
                  wb_spec, wo_spec] + r_in_specs,
        out_specs=[pl.BlockSpec((tm, MRG_COLS), lambda i: (i, MRG0 // MRG_COLS)), row(BR), row(BR), row(BR), row(BR), wo_spec, wb_spec] + r_out_specs,
        out_shape=[jax.ShapeDtypeStruct((s, PCOLS), BF16), dy_shape, dy_shape, dy_shape, dy_shape,
                   jax.ShapeDtypeStruct((D_MODEL, D_MODEL), F32), jax.ShapeDtypeStruct((N_HEADS, BR, D_MODEL), F32)] + r_shapes,
        scratch_shapes=r_scratch, input_output_aliases=r_alias,
        compiler_params=_params("arbitrary"))(dxn, proj, *ys, merged, wb, wo, *r_in)
    return outs[0], outs[1:5], outs[5], outs[6], outs[7:]


def _loss_head(x, target, g):
    s = x.shape[0]
    tm = min(512, s)

    def body(x_ref, t_ref, g_ref, loss_ref, dx_ref, dg_ref):
        @pl.when(pl.program_id(0) == 0)
        def _():
            loss_ref[...] = jnp.zeros_like(loss_ref)
            dg_ref[...] = jnp.zeros_like(dg_ref)

        xv = x_ref[...]
        gv = g_ref[...]
        r = lax.rsqrt(jnp.mean(xv * xv, axis=-1, keepdims=True) + EPS)
        xn = xv * r
        err = xn * gv - t_ref[...]
        loss_ref[...] += 0.5 * jnp.sum(jnp.mean(err * err, axis=-1, keepdims=True))
        dy = err * (1.0 / D_MODEL)
        dg_ref[...] += _part8(dy * xn)
        gy = dy * gv
        dx_ref[...] = r * (gy - xn * jnp.mean(xn * gy, axis=-1, keepdims=True))

    row = pl.BlockSpec((tm, D_MODEL), lambda i: (i, 0))
    return pl.pallas_call(
        body, name="loss_head", grid=(s // tm,),
        in_specs=[row, row, pl.BlockSpec((1, D_MODEL), lambda i: (0, 0))],
        out_specs=[pl.BlockSpec((8, 128), lambda i: (0, 0)), row, pl.BlockSpec((8, D_MODEL), lambda i: (0, 0))],
        out_shape=[jax.ShapeDtypeStruct((8, 128), F32), jax.ShapeDtypeStruct((s, D_MODEL), F32), jax.ShapeDtypeStruct((8, D_MODEL), F32)],
        compiler_params=_params("arbitrary"))(x, target, g)


_SEGMENTS = ((0, 512, AUV0), (512, 256, AG0), (768, 768, QKV0), (1540, 256, BG0), (1796, 768, CIN0),
             (2564, 256, CG0), (2820, 512, DGLU0), (3332, 256, DG0), (3588, 4096, MRG0))


def _sgu_bias_rows(sgu_b):
    return jnp.repeat(sgu_b.T, HEAD_DIM, axis=1)


def _layer_fwd(x, p, next_blocks=None):
    proj, proj_bf, h = _inproj_fwd(x, p["norm_g"], p["w_in"], p["w_bf"])
    ya = _mix_a_fwd(proj, p["sgu_w"], _sgu_bias_rows(p["sgu_b"]), p["sgu_ln_g"], p["sgu_ln_b"])
    qe, ke, ve, ket, vet = _attn_prep_fwd(proj, proj_bf, p["f_bias"])
    ot, lse, landed = _attn_fwd(qe, ke, vet, rider=None if next_blocks is None else _gather_send_rider(next_blocks))
    yb, o, gathered = _bgate_fwd(proj, ot, rider=None if next_blocks is None else _gather_forward_rider(landed))
    yc = _mix_c_fwd(proj, p["short_conv_w"])
    yd = _mix_d_fwd(proj, p["conf_dw_w"], p["conf_dw_b"], p["conf_ln_g"], p["conf_ln_b"])
    ys = (ya, yb, yc, yd)
    x_next, merged = _merge_fwd(x, proj, ys, p["w_branch"], p["w_out"])
    saved = dict(x=x, proj=proj, proj_bf=proj_bf, h=h, ys=ys, merged=merged, qe=qe, ke=ke, ve=ve, ket=ket, lse=lse, o=o)
    return x_next, saved, gathered


def _grads_by_chip(d_win, d_wb, d_wo):
    return [jnp.stack([_w_in_shard(d_win[0], d_win[1], j) for j in range(N_CHIPS)]),
            jnp.stack([d_wb[:, :, j * BR:(j + 1) * BR].reshape(N_HEADS * BR, BR).astype(BF16) for j in range(N_CHIPS)]),
            jnp.stack([d_wo[j * BR:(j + 1) * BR].astype(BF16) for j in range(N_CHIPS)])]


def _layer_bwd(dxn, p, sv, prev_reduce=None, reduce_self=None):
    proj = sv["proj"]
    if prev_reduce is None:
        dproj, dys, d_wo, d_wb, _ = _merge_bwd(dxn, proj, sv["ys"], sv["merged"], p["w_branch"], p["w_out"])
        chip_rider = None
    else:
        pc, chip, arrays = prev_reduce
        mine, other = _row_halves(arrays, pc)
        dproj, dys, d_wo, d_wb, from_sibling = _merge_bwd(dxn, proj, sv["ys"], sv["merged"], p["w_branch"], p["w_out"],
                                                           rider=_pair_rider(other))
        halves = _add_pairs(mine, from_sibling)
        chip_rider = _chip_rider(halves)
    dproj, d_sgu_w, d_bias_rows, d_sgu_lg, d_sgu_lb = _mix_a_bwd(
        proj, dproj, dys[0], p["sgu_w"], _sgu_bias_rows(p["sgu_b"]), p["sgu_ln_g"], p["sgu_ln_b"])
    dproj, doe, dd = _bgate_bwd(proj, dproj, sv["o"], dys[1])
    dqe, dke, dve, from_chips = _attn_bwd(sv["qe"], sv["ke"], sv["ket"], sv["ve"], doe, sv["lse"], dd, rider=chip_rider)
    reduced = None
    if prev_reduce is not None:
        reduced = _sum_chips(halves, chip, from_chips)
    dproj, dproj_bf, d_fb = _attn_prep_bwd(sv["proj_bf"], dproj, dqe, dke, dve, p["f_bias"])
    dproj, d_sc = _mix_c_bwd(proj, dproj, dys[2], p["short_conv_w"])
    dproj, d_cw, d_cb, d_clg, d_clb = _mix_d_bwd(proj, dproj, dys[3], p["conf_dw_w"], p["conf_dw_b"], p["conf_ln_g"], p["conf_ln_b"])
    d_win = _inproj_bwd_w(sv["h"], dproj, dproj_bf)
    reduced_self, self_rider = None, None
    if reduce_self is not None:
        pc, chip = reduce_self
        mine, other = _row_halves(_grads_by_chip(d_win, d_wb, d_wo), pc)
        own_halves = _add_pairs(mine, _pair_exchange(other))
        self_rider = _chip_rider(own_halves)
    dx, dg8, from_chips_self = _inproj_bwd_x(dproj, p["w_in"], sv["x"], dxn, p["norm_g"], dproj_bf, p["w_bf"], rider=self_rider)
    if reduce_self is not None:
        reduced_self = _sum_chips(own_halves, chip, from_chips_self)
    grads = dict(
        norm_g=jnp.sum(dg8, axis=0), w_in=d_win, f_bias=d_fb[0, :N_HEADS], sgu_w=d_sgu_w,
        sgu_b=d_bias_rows.reshape(CHUNK, N_HEADS, HEAD_DIM).sum(axis=-1).T,
        sgu_ln_g=d_sgu_lg[0], sgu_ln_b=d_sgu_lb[0], short_conv_w=d_sc[:K_SHORT], conf_dw_w=d_cw[:K_CONF],
        conf_dw_b=d_cb[0], conf_ln_g=d_clg[0], conf_ln_b=d_clb[0], w_branch=d_wb, w_out=d_wo)
    return dx, grads, reduced, reduced_self


def _row_halves(arrays, pc):
    half = lambda a, i: lax.dynamic_slice_in_dim(a, i * (a.shape[-2] // 2), a.shape[-2] // 2, axis=a.ndim - 2)
    return [half(a, pc) for a in arrays], [half(a, 1 - pc) for a in arrays]


def _local_step(x, target, layers, final_g):
    saved = []
    for p in layers:
        x, sv, _ = _layer_fwd(x, p)
        saved.append(sv)
    loss8, dx, dfg8 = _loss_head(x, target, final_g)
    grads = [None] * len(layers)
    for l in reversed(range(len(layers))):
        dx, grads[l], _, _ = _layer_bwd(dx, layers[l], saved[l])
    return loss8[0, 0], dx, grads, jnp.sum(dfg8, axis=0)


def _gather_chips(blocks):
    n = len(blocks)

    def body(*refs):
        ins, outs, (send_sems, recv_sems) = refs[:n], refs[n:2 * n], refs[2 * n:]
        x, y, cc = _place()
        me, sibling = (x, y, cc), (x, y, 1 - cc)
        chips = _other_chips(x, y)

        def copy(a, k, chip, layer, to, src=None):
            dst = outs[a].at[2 * chip[0] + chip[1], layer]
            return pltpu.make_async_remote_copy(src_ref=dst if src is None else src, dst_ref=dst, send_sem=send_sems.at[a, k],
                                                recv_sem=recv_sems.at[a, k], device_id=to, device_id_type=MESH)

        first = [copy(a, j, (x, y), cc, (*chip, cc), src=ins[a]) for j, chip in enumerate(chips) for a in range(n)]
        for cp in first:
            cp.start()
        passed = []
        for j, chip in enumerate(chips):
            for a in range(n):
                copy(a, j, chip, cc, me).wait_recv()
                passed.append(copy(a, 3 + j, chip, cc, sibling))
                passed[-1].start()
        for j, chip in enumerate(chips):
            for a in range(n):
                copy(a, 3 + j, chip, 1 - cc, me).wait_recv()
        for cp in first + passed:
            cp.wait_send()

    return pl.pallas_call(
        body, name="gather_chips", in_specs=[ANY] * n, out_specs=[ANY] * n,
        out_shape=[jax.ShapeDtypeStruct((N_CHIPS, 2) + b.shape, b.dtype) for b in blocks],
        scratch_shapes=[pltpu.SemaphoreType.DMA((n, 6)), pltpu.SemaphoreType.DMA((n, 6))])(*blocks)


def _pair_exchange(arrays):
    n = len(arrays)

    def body(*refs):
        ins, outs, (send_sems, recv_sems) = refs[:n], refs[n:2 * n], refs[2 * n:]
        x, y, cc = _place()
        cps = [pltpu.make_async_remote_copy(src_ref=ins[a], dst_ref=outs[a], send_sem=send_sems.at[a], recv_sem=recv_sems.at[a],
                                            device_id=(x, y, 1 - cc), device_id_type=MESH) for a in range(n)]
        for cp in cps:
            cp.start()
        for cp in cps:
            cp.wait()

    return pl.pallas_call(
        body, name="pair_exchange", in_specs=[ANY] * n, out_specs=[ANY] * n,
        out_shape=[jax.ShapeDtypeStruct(a.shape, a.dtype) for a in arrays],
        scratch_shapes=[pltpu.SemaphoreType.DMA((n,)), pltpu.SemaphoreType.DMA((n,))])(*arrays)


def _allreduce8(arrays):
    n = len(arrays)

    def body(*refs):
        ins, outs, recvs = refs[:n], refs[n:2 * n], refs[2 * n:3 * n]
        send_sems, recv_sems = refs[3 * n:]
        x, y, cc = _place()
        for a in range(n):
            outs[a][...] = ins[a][...]
        for k, peer in enumerate([(x, y, 1 - cc), (1 - x, y, cc), (x, 1 - y, cc)]):
            cps = [pltpu.make_async_remote_copy(src_ref=outs[a], dst_ref=recvs[a].at[k], send_sem=send_sems.at[a, k],
                                                recv_sem=recv_sems.at[a, k], device_id=peer, device_id_type=MESH) for a in range(n)]
            for cp in cps:
                cp.start()
            for cp in cps:
                cp.wait()
            for a in range(n):
                outs[a][...] = outs[a][...] + recvs[a][k]

    return pl.pallas_call(
        body, name="allreduce8", in_specs=[VMEM] * n, out_specs=[VMEM] * n,
        out_shape=[jax.ShapeDtypeStruct(a.shape, F32) for a in arrays],
        scratch_shapes=[pltpu.VMEM((3,) + a.shape, F32) for a in arrays] + [pltpu.SemaphoreType.DMA((n, 3)), pltpu.SemaphoreType.DMA((n, 3))],
        compiler_params=_params())(*arrays)


def _row_tile(rows, cols, limit_bytes=1 << 20):
    t = rows
    while t % 16 == 0 and t * cols * 4 > limit_bytes:
        t //= 2
    return t


REDUCE_STEPS = 8


def _add_pairs(xs, ys):
    n = len(xs)

    def body(*refs):
        for a in range(n):
            refs[2 * n + a][...] = (refs[a][...].astype(F32) + refs[n + a][...].astype(F32)).astype(BF16)

    specs = [pl.BlockSpec((x.shape[0], x.shape[1] // REDUCE_STEPS, x.shape[2]), lambda i: (0, i, 0)) for x in xs]
    return pl.pallas_call(body, name="add_pairs", grid=(REDUCE_STEPS,), in_specs=specs + specs, out_specs=specs,
                          out_shape=[jax.ShapeDtypeStruct(x.shape, BF16) for x in xs], compiler_params=_params("arbitrary"))(*xs, *ys)


def _sum_chips(halves, chip, recvs):
    n = len(halves)

    def body(chip_ref, *refs):
        del chip_ref
        for a in range(n):
            acc = refs[a][0].astype(F32)
            for k in range(3):
                acc = acc + refs[n + a][k].astype(F32)
            refs[2 * n + a][...] = acc

    rows = [h.shape[1] // REDUCE_STEPS for h in halves]
    own_specs = [pl.BlockSpec((1, r, h.shape[2]), lambda i, c: (c[0], i, 0)) for h, r in zip(halves, rows)]
    recv_specs = [pl.BlockSpec((3, r, h.shape[2]), lambda i, c: (0, i, 0)) for h, r in zip(halves, rows)]
    out_specs = [pl.BlockSpec((r, h.shape[2]), lambda i, c: (i, 0)) for h, r in zip(halves, rows)]
    grid_spec = pltpu.PrefetchScalarGridSpec(num_scalar_prefetch=1, grid=(REDUCE_STEPS,), in_specs=own_specs + recv_specs,
                                             out_specs=out_specs)
    return pl.pallas_call(body, name="sum_chips", grid_spec=grid_spec,
                          out_shape=[jax.ShapeDtypeStruct(h.shape[1:], F32) for h in halves],
                          compiler_params=_params("arbitrary"))(jnp.reshape(chip, (1,)).astype(jnp.int32), *halves, *recvs)


def _adamw_update(w_ref, m_ref, v_ref, g_ref, d_ref, mo_ref, vo_ref):
    gv = g_ref[...]
    mn = ADAM_B1 * m_ref[...] + (1.0 - ADAM_B1) * gv
    vn = ADAM_B2 * v_ref[...] + (1.0 - ADAM_B2) * (gv * gv)
    m_hat = mn / (1.0 - ADAM_B1 ** ADAM_STEP)
    v_hat = vn / (1.0 - ADAM_B2 ** ADAM_STEP)
    d_ref[...] = -ADAM_LR * (m_hat / (jnp.sqrt(v_hat) + ADAM_EPS) + ADAM_WD * w_ref[...])
    mo_ref[...] = mn
    vo_ref[...] = vn


def _adamw(w, m, v, g):
    r, c = w.shape
    t = _row_tile(r, c)

    def body(*refs):
        _adamw_update(*refs)

    spec = pl.BlockSpec((t, c), lambda i: (i, 0))
    shape = jax.ShapeDtypeStruct((r, c), F32)
    return pl.pallas_call(body, name="adamw", grid=(r // t,), in_specs=[spec] * 4, out_specs=[spec] * 3,
                          out_shape=[shape] * 3, compiler_params=_params("arbitrary"))(w, m, v, g)


def _adamw_small(ws, ms, vs, gs):
    n = len(ws)

    def body(*refs):
        w_refs, m_refs, v_refs, g_refs = refs[:n], refs[n:2 * n], refs[2 * n:3 * n], refs[3 * n:4 * n]
        d_refs, mo_refs, vo_refs = refs[4 * n:5 * n], refs[5 * n:6 * n], refs[6 * n:]
        for a in range(n):
            _adamw_update(w_refs[a], m_refs[a], v_refs[a], g_refs[a], d_refs[a], mo_refs[a], vo_refs[a])

    shapes = [jax.ShapeDtypeStruct(w.shape, F32) for w in ws]
    outs = pl.pallas_call(body, name="adamw_small", in_specs=[VMEM] * (4 * n), out_specs=[VMEM] * (3 * n),
                          out_shape=shapes * 3, compiler_params=_params())(*ws, *ms, *vs, *gs)
    return outs[:n], outs[n:2 * n], outs[2 * n:]


SMALL =("norm_g", "f_bias", "sgu_w", "sgu_b", "sgu_ln_g", "sgu_ln_b", "short_conv_w", "conf_dw_w", "conf_dw_b",
         "conf_ln_g", "conf_ln_b", "final_g")
WEIGHTS = ("norm_g", "w_in", "f_bias", "sgu_w", "sgu_b", "sgu_ln_g", "sgu_ln_b", "short_conv_w", "conf_dw_w",
           "conf_dw_b", "conf_ln_g", "conf_ln_b", "w_branch", "w_out", "final_g")


def _aligned_pieces():
    out, pos = [], 0
    for src, width, dst in sorted(_SEGMENTS, key=lambda t: t[2]):
        if dst > pos:
            out.append((None, 0, dst - pos))
        lo = src
        while lo < src + width:
            j = lo // SHARD_COLS
            hi = min(src + width, (j + 1) * SHARD_COLS)
            out.append((j, lo - j * SHARD_COLS, hi - lo))
            lo = hi
        pos = dst + width
    return out


def _w_in_aligned(shards):
    rows, dtype = shards[0].shape[0], shards[0].dtype
    return jnp.concatenate([jnp.zeros((rows, w), dtype) if j is None else shards[j][:, c0:c0 + w]
                            for j, c0, w in _aligned_pieces()], axis=1)


def _w_bf(shards):
    j, c0 = BF_SRC // SHARD_COLS, BF_SRC % SHARD_COLS
    assert c0 + BF_COLS <= SHARD_COLS
    return jnp.pad(shards[j][:, c0:c0 + BF_COLS], ((0, 0), (0, CHUNK - BF_COLS)))


def _w_in_shard(g, g_bf, j):
    lo_s, hi_s = j * SHARD_COLS, (j + 1) * SHARD_COLS
    parts = []
    for src, width, dst in sorted(_SEGMENTS + ((BF_SRC, BF_COLS, None),), key=lambda t: t[0]):
        lo, hi = max(src, lo_s), min(src + width, hi_s)
        if lo < hi:
            parts.append(g_bf[:, lo - src:hi - src] if dst is None else g[:, dst + lo - src:dst + hi - src])
    return jnp.concatenate(parts + [jnp.zeros((g.shape[0], SHARD_PAD - SHARD_COLS), g.dtype)], axis=1)


def kernel(x, norm_g, w_in, f_bias, sgu_w, sgu_b, sgu_ln_g, sgu_ln_b, short_conv_w, conf_dw_w, conf_dw_b, conf_ln_g, conf_ln_b, w_branch, w_out, final_g, loss_target, m_norm_g, m_w_in, m_f_bias, m_sgu_w, m_sgu_b, m_sgu_ln_g, m_sgu_ln_b, m_short_conv_w, m_conf_dw_w, m_conf_dw_b, m_conf_ln_g, m_conf_ln_b, m_w_branch, m_w_out, m_final_g, v_norm_g, v_w_in, v_f_bias, v_sgu_w, v_sgu_b, v_sgu_ln_g, v_sgu_ln_b, v_short_conv_w, v_conf_dw_w, v_conf_dw_b, v_conf_ln_g, v_conf_ln_b, v_w_branch, v_w_out, v_final_g):
    px, py, pc = _place()
    chip = 2 * px + py
    depth = w_in.shape[0]
    w = dict(norm_g=norm_g, w_in=w_in, f_bias=f_bias, sgu_w=sgu_w, sgu_b=sgu_b, sgu_ln_g=sgu_ln_g, sgu_ln_b=sgu_ln_b,
             short_conv_w=short_conv_w, conf_dw_w=conf_dw_w, conf_dw_b=conf_dw_b, conf_ln_g=conf_ln_g, conf_ln_b=conf_ln_b,
             w_branch=w_branch, w_out=w_out, final_g=final_g)
    m = dict(norm_g=m_norm_g, w_in=m_w_in, f_bias=m_f_bias, sgu_w=m_sgu_w, sgu_b=m_sgu_b, sgu_ln_g=m_sgu_ln_g,
             sgu_ln_b=m_sgu_ln_b, short_conv_w=m_short_conv_w, conf_dw_w=m_conf_dw_w, conf_dw_b=m_conf_dw_b,
             conf_ln_g=m_conf_ln_g, conf_ln_b=m_conf_ln_b, w_branch=m_w_branch, w_out=m_w_out, final_g=m_final_g)
    v = dict(norm_g=v_norm_g, w_in=v_w_in, f_bias=v_f_bias, sgu_w=v_sgu_w, sgu_b=v_sgu_b, sgu_ln_g=v_sgu_ln_g,
             sgu_ln_b=v_sgu_ln_b, short_conv_w=v_short_conv_w, conf_dw_w=v_conf_dw_w, conf_dw_b=v_conf_dw_b,
             conf_ln_g=v_conf_ln_g, conf_ln_b=v_conf_ln_b, w_branch=v_w_branch, w_out=v_w_out, final_g=v_final_g)

    local = (jnp.pad(w_in, ((0, 0), (0, 0), (0, SHARD_PAD - SHARD_COLS))).astype(BF16),
             w_branch.astype(BF16).reshape(depth, N_HEADS * BR, BR), w_out.astype(BF16))
    pick = lambda a, i: lax.dynamic_index_in_dim(a, i, 0, keepdims=False)
    flat = lambda a: a.reshape(-1, a.shape[-1])

    def my_half(l):
        return _row_halves([a[l] for a in local], pc)[0]

    def all_chips(gathered, l):
        return [lax.dynamic_update_index_in_dim(got.reshape((N_CHIPS,) + a.shape[1:]), a[l], chip, 0) for got, a in zip(gathered, local)]

    conv_ch = BR // N_CHIPS
    place = lambda a: lax.dynamic_update_slice_in_dim(jnp.zeros(a.shape[:-1] + (BR,), F32), a * (pc == 0).astype(F32),
                                                      conv_ch * chip, axis=2).reshape(-1, BR)
    short_full, conf_full = _allreduce8([place(short_conv_w), place(conf_dw_w)])
    short_full, conf_full = short_full.reshape(depth, K_SHORT, BR), conf_full.reshape(depth, K_CONF, BR)

    def layer_params(l, shards):
        wi_all, wb_all, wo_all = shards
        return dict(
            norm_g=norm_g[l][None], w_in=_w_in_aligned([wi_all[j] for j in range(N_CHIPS)]),
            w_bf=_w_bf([wi_all[j] for j in range(N_CHIPS)]),
            f_bias=jnp.pad(f_bias[l], (0, CHUNK - N_HEADS))[None],
            sgu_w=sgu_w[l], sgu_b=sgu_b[l], sgu_ln_g=sgu_ln_g[l][None], sgu_ln_b=sgu_ln_b[l][None],
            short_conv_w=short_full[l], conf_dw_w=conf_full[l], conf_dw_b=conf_dw_b[l][None],
            conf_ln_g=conf_ln_g[l][None], conf_ln_b=conf_ln_b[l][None],
            w_branch=jnp.concatenate([wb_all[j].reshape(N_HEADS, BR, BR) for j in range(N_CHIPS)], axis=2),
            w_out=jnp.concatenate([wo_all[j] for j in range(N_CHIPS)], axis=0))

    shards = all_chips(_gather_chips(my_half(0)), 0)
    params, saved, xs = [], [], x[0]
    for l in range(depth):
        params.append(layer_params(l, shards))
        nxt = my_half(l + 1) if l + 1 < depth else None
        xs, sv, gathered = _layer_fwd(xs, params[l], next_blocks=nxt)
        saved.append(sv)
        if nxt is not None:
            shards = all_chips(gathered, l + 1)
    loss8, dx, dfg8 = _loss_head(xs, loss_target[0], final_g[None])
    d_final_g = jnp.sum(dfg8, axis=0)
    loss = lax.psum(loss8[0, 0], ("x", "y", "c"))

    grads, reduced, pending = [None] * depth, [None] * depth, None
    for l in reversed(range(depth)):
        dx, grads[l], red, red_self = _layer_bwd(
            dx, params[l], saved[l], prev_reduce=None if pending is None else (pc, chip, pending),
            reduce_self=(pc, chip) if l == 0 else None)
        if pending is not None:
            reduced[l + 1] = red
        if l == 0:
            reduced[0] = red_self
        else:
            pending = _grads_by_chip(grads[l]["w_in"], grads[l]["w_branch"], grads[l]["w_out"])
    grad_x = dx
    own = [a for red in reduced for a in red]
    rows = [jnp.where(pc == 0, jnp.concatenate([a, b], axis=0), jnp.concatenate([b, a], axis=0))
            for a, b in zip(own, _pair_exchange(own))]
    gi, gb, go = [jnp.stack([rows[3 * l + k] for l in range(depth)]) for k in range(3)]
    g = dict(w_in=gi[:, :, :SHARD_COLS], w_branch=gb.reshape(depth, N_HEADS, BR, BR), w_out=go)

    two_d = lambda a: a.reshape(-1, a.shape[-1])
    at_least_2d = lambda a: a if a.ndim >= 2 else a[None]
    small_local = [jnp.stack([grads[l][n] for l in range(depth)]) for n in SMALL[:-1]] + [d_final_g]
    for n, a, b in zip(SMALL, small_local, _allreduce8([at_least_2d(a) for a in small_local])):
        g[n] = b.reshape(a.shape)
    for n in ("short_conv_w", "conf_dw_w"):
        g[n] = lax.dynamic_slice_in_dim(g[n], conv_ch * chip, conv_ch, axis=2)

    delta, new_m, new_v = {}, {}, {}
    for n in ("w_in", "w_branch", "w_out"):
        shp = w[n].shape
        d_, m_, v_ = _adamw(two_d(w[n]), two_d(m[n]), two_d(v[n]), two_d(g[n]))
        delta[n], new_m[n], new_v[n] = d_.reshape(shp), m_.reshape(shp), v_.reshape(shp)
    d_, m_, v_ = _adamw_small(*[[at_least_2d(t[n]) for n in SMALL] for t in (w, m, v, g)])
    for n, a, b, c_ in zip(SMALL, d_, m_, v_):
        delta[n], new_m[n], new_v[n] = a.reshape(w[n].shape), b.reshape(w[n].shape), c_.reshape(w[n].shape)

    return (loss, grad_x[None], *[g[n] for n in WEIGHTS], *[delta[n] for n in WEIGHTS],
            *[new_m[n] for n in WEIGHTS], *[new_v[n] for n in WEIGHTS])
```

```python
import functools
import math

import jax
import jax.numpy as jnp
from jax import lax
from jax.experimental import pallas as pl
from jax.experimental.pallas import tpu as pltpu

F32 = jnp.float32
BF16 = jnp.bfloat16

D_MODEL = 1024
BR = 256
N_HEADS = 4
HEAD_DIM = 64
CHUNK = 128
K_SHORT = 3
K_CONF = 31
EPS = 1e-6
IN_COLS = 7684
SHARD_COLS = IN_COLS // 4
SHARD_PAD = 2048
N_CHIPS = 4

MRG0, MRG_COLS = 0, 4096
QKV0, CIN0, AUV0, DGLU0 = 4096, 4864, 5632, 6144
AG0, BG0, CG0, DG0 = 6656, 6912, 7168, 7424
PCOLS = 7680
BF_SRC, BF_COLS = 1536, 4

V7X_VMEM_BYTES = 64 * 1024 * 1024
VMEM_LIMIT = V7X_VMEM_BYTES * 7 // 8

ADAM_LR, ADAM_B1, ADAM_B2, ADAM_EPS, ADAM_WD, ADAM_STEP = 0.001, 0.9, 0.999, 1e-08, 0.01, 10

MESH = pl.DeviceIdType.MESH
ANY = pl.BlockSpec(memory_space=pl.ANY)
VMEM = pl.BlockSpec(memory_space=pltpu.VMEM)

GELU_C0 = math.sqrt(2.0 / math.pi)
GELU_C1 = 0.044715


def _params(*sem):
    return pltpu.CompilerParams(dimension_semantics=sem, vmem_limit_bytes=VMEM_LIMIT)


def _sigmoid(x):
    return 0.5 * jnp.tanh(0.5 * x) + 0.5


def _silu_and_grad(x):
    s = _sigmoid(x)
    return x * s, s * (1.0 + x * (1.0 - s))


def _gelu_and_grad(z):
    z2 = z * z
    t = jnp.tanh(GELU_C0 * (z + GELU_C1 * z2 * z))
    half = 0.5 * (1.0 + t)
    return z * half, half + 0.5 * z * (1.0 - t * t) * (GELU_C0 * (1.0 + 3.0 * GELU_C1 * z2))


def _ln_fwd(v):
    mu = jnp.mean(v, axis=-1, keepdims=True)
    xc = v - mu
    rs = lax.rsqrt(jnp.mean(xc * xc, axis=-1, keepdims=True) + EPS)
    return xc * rs, rs


def _ln_bwd(d_xh, xh, rs):
    return rs * (d_xh - jnp.mean(d_xh, axis=-1, keepdims=True) - xh * jnp.mean(d_xh * xh, axis=-1, keepdims=True))


def _part8(a):
    return a.reshape(a.shape[0] // 8, 8, a.shape[1]).sum(axis=0)


def _dot(a, b):
    return jnp.dot(a, b, preferred_element_type=F32)


def _dot_nt(a, b):
    return lax.dot_general(a, b, (((1,), (1,)), ((), ())), preferred_element_type=F32)


def _dot_tn(a, b):
    return lax.dot_general(a, b, (((0,), (0,)), ((), ())), preferred_element_type=F32)


def _head_masks(dtype):
    lane = lax.broadcasted_iota(jnp.int32, (1, BR), 1) // HEAD_DIM
    return [(lane == h).astype(dtype) for h in range(N_HEADS)]


def _window(ref, col0, width):
    return ref.at[:, pl.ds(col0, width)]


def _copy_all(pairs, sems):
    cps = [pltpu.make_async_copy(s, d, sems.at[i]) for i, (s, d) in enumerate(pairs)]
    for cp in cps:
        cp.start()
    for cp in cps:
        cp.wait()


def _place():
    return lax.axis_index("x"), lax.axis_index("y"), lax.axis_index("c")


def _other_chips(x, y):
    return [(1 - x, y), (x, 1 - y), (1 - x, 1 - y)]


class _Rider:
    def __init__(self, srcs, out_shapes, per_array, make, through=False):
        self.srcs, self.out_shapes, self.per_array, self.make, self.through = list(srcs), list(out_shapes), per_array, make, through
        self.n = len(self.srcs)

    def scratch(self):
        return [pltpu.SemaphoreType.DMA((self.n, self.per_array)), pltpu.SemaphoreType.DMA((self.n, self.per_array))]


def _ride(rider, first, last, src_refs, dst_refs, sems):
    if rider is None:
        return (lambda: None), (lambda: None)
    cps = rider.make(src_refs, dst_refs, *sems)

    def guarded(cond, fn):
        if cond is True:
            fn()
        else:
            pl.when(cond)(fn)

    def start():
        guarded(first, lambda: [cp.start() for cp in cps] and None)

    def finish():
        guarded(last, lambda: [cp.wait() for cp in cps] and None)

    return start, finish


def _rider_parts(rider, n_in, n_out):
    if rider is None:
        return [], [], [], [], [], {}
    aliases = {n_in + a: n_out + a for a in range(rider.n)} if rider.through else {}
    return rider.srcs, [ANY] * rider.n, [ANY] * rider.n, rider.out_shapes, rider.scratch(), aliases


def _remote(src, dst, send_sems, recv_sems, a, k, to):
    return pltpu.make_async_remote_copy(src_ref=src, dst_ref=dst, send_sem=send_sems.at[a, k], recv_sem=recv_sems.at[a, k],
                                        device_id=to, device_id_type=MESH)


def _gather_send_rider(blocks):
    def make(srcs, dsts, ss, rs):
        x, y, cc = _place()
        return [_remote(srcs[a], dsts[a].at[2 * x + y, cc], ss, rs, a, j, (*chip, cc))
                for j, chip in enumerate(_other_chips(x, y)) for a in range(len(srcs))]
    shapes = [jax.ShapeDtypeStruct((N_CHIPS, 2) + b.shape, b.dtype) for b in blocks]
    return _Rider(blocks, shapes, 3, make)


def _gather_forward_rider(landed):
    def make(srcs, dsts, ss, rs):
        x, y, cc = _place()
        return [_remote(dsts[a].at[2 * px + py, cc], dsts[a].at[2 * px + py, cc], ss, rs, a, j, (x, y, 1 - cc))
                for j, (px, py) in enumerate(_other_chips(x, y)) for a in range(len(dsts))]
    shapes = [jax.ShapeDtypeStruct(b.shape, b.dtype) for b in landed]
    return _Rider(landed, shapes, 3, make, through=True)


def _pair_rider(arrays):
    def make(srcs, dsts, ss, rs):
        x, y, cc = _place()
        return [_remote(srcs[a], dsts[a], ss, rs, a, 0, (x, y, 1 - cc)) for a in range(len(srcs))]
    return _Rider(arrays, [jax.ShapeDtypeStruct(b.shape, b.dtype) for b in arrays], 1, make)


def _chip_rider(arrays):
    def make(srcs, dsts, ss, rs):
        x, y, cc = _place()
        return [_remote(srcs[a].at[2 * px + py], dsts[a].at[k], ss, rs, a, k, (px, py, cc))
                for k, (px, py) in enumerate(_other_chips(x, y)) for a in range(len(srcs))]
    return _Rider(arrays, [jax.ShapeDtypeStruct((3,) + b.shape[1:], b.dtype) for b in arrays], 3, make)


INPROJ_TN = 1536


def _inproj_fwd(x, g, w, w_bf):
    s = x.shape[0]
    tm, tn = min(256, s), PCOLS

    def body(x_ref, g_ref, w_ref, wbf_ref, proj_ref, pbf_ref, h_ref):
        @pl.when(pl.program_id(1) == 0)
        def _():
            xv = x_ref[...]
            r = lax.rsqrt(jnp.mean(xv * xv, axis=-1, keepdims=True) + EPS)
            h_ref[...] = ((xv * r) * g_ref[...]).astype(BF16)
            pbf_ref[...] = _dot(h_ref[...], wbf_ref[...])
        proj_ref[...] = _dot(h_ref[...], w_ref[...])

    return pl.pallas_call(
        body, name="inproj_fwd", grid=(s // tm, PCOLS // tn),
        in_specs=[pl.BlockSpec((tm, D_MODEL), lambda i, j: (i, 0)), pl.BlockSpec((1, D_MODEL), lambda i, j: (0, 0)),
                  pl.BlockSpec((D_MODEL, tn), lambda i, j: (0, j)), pl.BlockSpec((D_MODEL, CHUNK), lambda i, j: (0, 0))],
        out_specs=[pl.BlockSpec((tm, tn), lambda i, j: (i, j)), pl.BlockSpec((tm, CHUNK), lambda i, j: (i, 0)),
                   pl.BlockSpec((tm, D_MODEL), lambda i, j: (i, 0))],
        out_shape=[jax.ShapeDtypeStruct((s, PCOLS), F32), jax.ShapeDtypeStruct((s, CHUNK), F32), jax.ShapeDtypeStruct((s, D_MODEL), BF16)],
        compiler_params=_params("arbitrary", "arbitrary"))(x, g, w, w_bf)


def _rms_bwd(dh, x, g):
    r = lax.rsqrt(jnp.mean(x * x, axis=-1, keepdims=True) + EPS)
    xn = x * r
    gy = dh * g
    dx = r * (gy - xn * jnp.mean(xn * gy, axis=-1, keepdims=True))
    return dx, _part8(dh * xn)


def _inproj_bwd_x(dproj, w, x, dxn, g, dproj_bf, w_bf, rider=None):
    s = x.shape[0]
    tm, tk = min(256, s), PCOLS
    nk = PCOLS // tk
    ni = s // tm
    r_in, r_in_specs, r_out_specs, r_shapes, r_scratch, r_alias = _rider_parts(rider, 7, 2)
    nr = len(r_in)

    def body(*refs):
        dp_ref, w_ref, x_ref, dxn_ref, g_ref, dpbf_ref, wbf_ref = refs[:7]
        dx_ref, dg_ref = refs[7 + nr:9 + nr]
        acc_ref = refs[9 + 2 * nr]
        i, k = pl.program_id(0), pl.program_id(1)
        start, finish = _ride(rider, (i == 0) & (k == 0), (i == ni - 1) & (k == nk - 1),
                              refs[7:7 + nr], refs[9 + nr:9 + 2 * nr], refs[10 + 2 * nr:])
        start()

        @pl.when(k == 0)
        def _():
            acc_ref[...] = _dot_nt(dpbf_ref[...], wbf_ref[...])

        @pl.when((i == 0) & (k == 0))
        def _():
            dg_ref[...] = jnp.zeros_like(dg_ref)

        acc_ref[...] += _dot_nt(dp_ref[...], w_ref[...])

        @pl.when(k == nk - 1)
        def _():
            dx, dg8 = _rms_bwd(acc_ref[...], x_ref[...], g_ref[...])
            dx_ref[...] = dxn_ref[...] + dx
            dg_ref[...] += dg8

        finish()

    outs = pl.pallas_call(
        body, name="inproj_bwd_x", grid=(ni, nk),
        in_specs=[pl.BlockSpec((tm, tk), lambda i, k: (i, k)), pl.BlockSpec((D_MODEL, tk), lambda i, k: (0, k)),
                  pl.BlockSpec((tm, D_MODEL), lambda i, k: (i, 0)), pl.BlockSpec((tm, D_MODEL), lambda i, k: (i, 0)),
                  pl.BlockSpec((1, D_MODEL), lambda i, k: (0, 0)), pl.BlockSpec((tm, CHUNK), lambda i, k: (i, 0)),
                  pl.BlockSpec((D_MODEL, CHUNK), lambda i, k: (0, 0))] + r_in_specs,
        out_specs=[pl.BlockSpec((tm, D_MODEL), lambda i, k: (i, 0)), pl.BlockSpec((8, D_MODEL), lambda i, k: (0, 0))] + r_out_specs,
        out_shape=[jax.ShapeDtypeStruct((s, D_MODEL), F32), jax.ShapeDtypeStruct((8, D_MODEL), F32)] + r_shapes,
        scratch_shapes=[pltpu.VMEM((tm, D_MODEL), F32)] + r_scratch, input_output_aliases=r_alias,
        compiler_params=_params("arbitrary", "arbitrary"))(dproj, w, x, dxn, g, dproj_bf, w_bf, *r_in)
    return outs[0], outs[1], outs[2:]


def _inproj_bwd_w(h, dproj, dproj_bf):
    s = h.shape[0]
    tn, tk = INPROJ_TN, min(1024, s)
    nk = s // tk

    def body(h_ref, dp_ref, dpbf_ref, dw_ref, dwbf_ref, acc_ref, accbf_ref):
        j, k = pl.program_id(0), pl.program_id(1)

        @pl.when(k == 0)
        def _():
            acc_ref[...] = jnp.zeros_like(acc_ref)

        acc_ref[...] += _dot_tn(h_ref[...], dp_ref[...])

        @pl.when(k == nk - 1)
        def _():
            dw_ref[...] = acc_ref[...].astype(BF16)

        @pl.when((j == 0) & (k == 0))
        def _():
            accbf_ref[...] = jnp.zeros_like(accbf_ref)

        @pl.when(j == 0)
        def _():
            accbf_ref[...] += _dot_tn(h_ref[...], dpbf_ref[...])

        @pl.when((j == 0) & (k == nk - 1))
        def _():
            dwbf_ref[...] = accbf_ref[...].astype(BF16)

    return pl.pallas_call(
        body, name="inproj_bwd_w", grid=(PCOLS // tn, nk),
        in_specs=[pl.BlockSpec((tk, D_MODEL), lambda j, k: (k, 0)), pl.BlockSpec((tk, tn), lambda j, k: (k, j)),
                  pl.BlockSpec((tk, CHUNK), lambda j, k: (k, 0))],
        out_specs=[pl.BlockSpec((D_MODEL, tn), lambda j, k: (0, j)), pl.BlockSpec((D_MODEL, CHUNK), lambda j, k: (0, 0))],
        out_shape=[jax.ShapeDtypeStruct((D_MODEL, PCOLS), BF16), jax.ShapeDtypeStruct((D_MODEL, CHUNK), BF16)],
        scratch_shapes=[pltpu.VMEM((D_MODEL, tn), F32), pltpu.VMEM((D_MODEL, CHUNK), F32)],
        compiler_params=_params("arbitrary", "arbitrary"))(h, dproj, dproj_bf)


def _mix_a_chunk(uvp, agp, wm_ref, bias, lg, lb):
    u, du = _gelu_and_grad(uvp[:, :BR])
    v, dv = _gelu_and_grad(uvp[:, BR:])
    xh, rs = _ln_fwd(v)
    vnb = (xh * lg + lb).astype(BF16)
    masks = _head_masks(BF16)
    mixed = bias
    for h in range(N_HEADS):
        mixed = mixed + _dot(wm_ref[h], vnb * masks[h])
    sg, dsg = _silu_and_grad(agp)
    return u, du, dv, xh, rs, vnb, masks, mixed, sg, dsg


def _store_masked_sgu(sw_ref, wm_ref):
    row = lax.broadcasted_iota(jnp.int32, (CHUNK, CHUNK), 0)
    col = lax.broadcasted_iota(jnp.int32, (CHUNK, CHUNK), 1)
    for h in range(N_HEADS):
        wm_ref[h] = jnp.where(row >= col, sw_ref[h], 0.0).astype(BF16)


def _mix_a_fwd(proj, sgu_w, bias, lg, lb):
    s = proj.shape[0]

    def body(proj_ref, sw_ref, bias_ref, lg_ref, lb_ref, y_ref, uv_buf, ag_buf, wm_ref, sems):
        _copy_all([(_window(proj_ref, AUV0, 2 * BR), uv_buf), (_window(proj_ref, AG0, BR), ag_buf)], sems)
        _store_masked_sgu(sw_ref, wm_ref)

        def chunk(i, c):
            rows = pl.ds(pl.multiple_of(i * CHUNK, CHUNK), CHUNK)
            u, _, _, _, _, _, _, mixed, sg, _ = _mix_a_chunk(uv_buf[rows, :], ag_buf[rows, :], wm_ref, bias_ref[...],
                                                            lg_ref[...], lb_ref[...])
            y_ref[rows, :] = (u * mixed * sg).astype(BF16)
            return c

        lax.fori_loop(0, s // CHUNK, chunk, 0)

    return pl.pallas_call(
        body, name="mix_a_fwd", in_specs=[ANY, VMEM, VMEM, VMEM, VMEM], out_specs=VMEM,
        out_shape=jax.ShapeDtypeStruct((s, BR), BF16),
        scratch_shapes=[pltpu.VMEM((s, 2 * BR), F32), pltpu.VMEM((s, BR), F32), pltpu.VMEM((N_HEADS, CHUNK, CHUNK), BF16),
                        pltpu.SemaphoreType.DMA((2,))],
        compiler_params=_params())(proj, sgu_w, bias, lg, lb)


def _mix_a_bwd(proj, dproj, dy, sgu_w, bias, lg, lb):
    s = proj.shape[0]

    def body(proj_ref, dproj_in, dy_ref, sw_ref, bias_ref, lg_ref, lb_ref,
             dproj_ref, dsw_ref, dbias_ref, dlg_ref, dlb_ref,
             uv_buf, ag_buf, duv_buf, dag_buf, wm_ref, acc_lg, acc_lb, sems):
        del dproj_in
        _copy_all([(_window(proj_ref, AUV0, 2 * BR), uv_buf), (_window(proj_ref, AG0, BR), ag_buf)], sems)
        _store_masked_sgu(sw_ref, wm_ref)
        dsw_ref[...] = jnp.zeros_like(dsw_ref)
        dbias_ref[...] = jnp.zeros_like(dbias_ref)
        acc_lg[...] = jnp.zeros_like(acc_lg)
        acc_lb[...] = jnp.zeros_like(acc_lb)

        def chunk(i, c):
            rows = pl.ds(pl.multiple_of(i * CHUNK, CHUNK), CHUNK)
            lg_v = lg_ref[...]
            u, du, dv, xh, rs, vnb, masks, mixed, sg, dsg = _mix_a_chunk(
                uv_buf[rows, :], ag_buf[rows, :], wm_ref, bias_ref[...], lg_v, lb_ref[...])
            dyv = dy_ref[rows, :]
            t1 = dyv * sg
            d_u = t1 * mixed
            d_mixed = t1 * u
            d_ag = dyv * u * mixed * dsg
            dbias_ref[...] += d_mixed
            dmb = d_mixed.astype(BF16)
            d_vn = jnp.zeros((CHUNK, BR), F32)
            for h in range(N_HEADS):
                dm_h = dmb * masks[h]
                dsw_ref[h] += _dot_nt(dm_h, vnb)
                d_vn = d_vn + _dot_tn(wm_ref[h], dm_h)
            acc_lg[...] += _part8(d_vn * xh)
            acc_lb[...] += _part8(d_vn)
            d_v = _ln_bwd(d_vn * lg_v, xh, rs)
            duv_buf[rows, :] = jnp.concatenate([d_u * du, d_v * dv], axis=1).astype(BF16)
            dag_buf[rows, :] = d_ag.astype(BF16)
            return c

        lax.fori_loop(0, s // CHUNK, chunk, 0)
        row = lax.broadcasted_iota(jnp.int32, (CHUNK, CHUNK), 0)
        col = lax.broadcasted_iota(jnp.int32, (CHUNK, CHUNK), 1)
        for h in range(N_HEADS):
            dsw_ref[h] = jnp.where(row >= col, dsw_ref[h], 0.0)
        dlg_ref[...] = jnp.sum(acc_lg[...], axis=0, keepdims=True)
        dlb_ref[...] = jnp.sum(acc_lb[...], axis=0, keepdims=True)
        _copy_all([(duv_buf, _window(dproj_ref, AUV0, 2 * BR)), (dag_buf, _window(dproj_ref, AG0, BR))], sems)

    return pl.pallas_call(
        body, name="mix_a_bwd", in_specs=[ANY, ANY, VMEM, VMEM, VMEM, VMEM, VMEM],
        out_specs=[ANY, VMEM, VMEM, VMEM, VMEM],
        out_shape=[jax.ShapeDtypeStruct(dproj.shape, dproj.dtype), jax.ShapeDtypeStruct((N_HEADS, CHUNK, CHUNK), F32),
                   jax.ShapeDtypeStruct((CHUNK, BR), F32), jax.ShapeDtypeStruct((1, BR), F32), jax.ShapeDtypeStruct((1, BR), F32)],
        scratch_shapes=[pltpu.VMEM((s, 2 * BR), F32), pltpu.VMEM((s, BR), F32), pltpu.VMEM((s, 2 * BR), BF16),
                        pltpu.VMEM((s, BR), BF16), pltpu.VMEM((N_HEADS, CHUNK, CHUNK), BF16),
                        pltpu.VMEM((8, BR), F32), pltpu.VMEM((8, BR), F32), pltpu.SemaphoreType.DMA((2,))],
        input_output_aliases={1: 0}, compiler_params=_params())(proj, dproj, dy, sgu_w, bias, lg, lb)


def _tri_ones(n, upper):
    row = lax.broadcasted_iota(jnp.int32, (n, n), 0)
    col = lax.broadcasted_iota(jnp.int32, (n, n), 1)
    return ((row <= col) if upper else (row >= col)).astype(BF16)


def _split3(c):
    hi = c.astype(BF16)
    r1 = c - hi.astype(F32)
    mid = r1.astype(BF16)
    lo = (r1 - mid.astype(F32)).astype(BF16)
    return [hi, mid, lo]


def _tri_sum(tri, a):
    parts = _split3(a)
    return _dot(tri, parts[0]) + _dot(tri, parts[1]) + _dot(tri, parts[2])


EXT = 2 * HEAD_DIM
LANE_CQ = HEAD_DIM
LANE_CK = HEAD_DIM + 3


def _to_head(h):
    r = lax.broadcasted_iota(jnp.int32, (BR, EXT), 0)
    c = lax.broadcasted_iota(jnp.int32, (BR, EXT), 1)
    return ((r == c + h * HEAD_DIM) & (c < HEAD_DIM)).astype(BF16)


def _from_head(h):
    r = lax.broadcasted_iota(jnp.int32, (EXT, BR), 0)
    c = lax.broadcasted_iota(jnp.int32, (EXT, BR), 1)
    return ((c == r + h * HEAD_DIM) & (r < HEAD_DIM)).astype(BF16)


def _attn_prep_fwd(proj, proj_bf, f_bias):
    s = proj.shape[0]
    scale = 1.0 / math.sqrt(HEAD_DIM)

    def body(proj_ref, z_buf, fb_ref, qe_ref, ke_ref, ve_ref, ket_ref, vet_ref, qkv_buf, sems):
        _copy_all([(_window(proj_ref, QKV0, 3 * BR), qkv_buf)], sems)
        tri = _tri_ones(CHUNK, upper=False)
        lane = lax.broadcasted_iota(jnp.int32, (CHUNK, EXT), 1)
        ones_q = ((lane >= LANE_CK) & (lane < LANE_CK + 3)).astype(F32)
        ones_k = ((lane >= LANE_CQ) & (lane < LANE_CQ + 3)).astype(F32)

        def chunk(i, carry):
            rows = pl.ds(pl.multiple_of(i * CHUNK, CHUNK), CHUNK)
            cum = _tri_sum(tri, jax.nn.log_sigmoid(z_buf[rows, :] + fb_ref[...])) + carry
            qb = (qkv_buf[rows, 0:BR] * scale).astype(BF16)
            kb = qkv_buf[rows, BR:2 * BR].astype(BF16)
            vb = qkv_buf[rows, 2 * BR:].astype(BF16)
            parts = [p.astype(F32) for p in _split3(cum)]
            for h in range(N_HEADS):
                sel = _to_head(h)
                dec_q, dec_k = ones_q, ones_k
                for t, part in enumerate(parts):
                    pf = part[:, h:h + 1]
                    dec_q = dec_q + jnp.where(lane == LANE_CQ + t, pf, 0.0)
                    dec_k = dec_k - jnp.where(lane == LANE_CK + t, pf, 0.0)
                qe_ref[h, rows, :] = (_dot(qb, sel) + dec_q).astype(BF16)
                kh = _dot(kb, sel) + dec_k
                vh = _dot(vb, sel)
                ke_ref[h, rows, :] = kh.astype(BF16)
                ve_ref[h, rows, :] = vh.astype(BF16)
                ket_ref[h, :, rows] = kh.T.astype(BF16)
                vet_ref[h, :, rows] = vh.T.astype(BF16)
            return cum[CHUNK - 1:CHUNK, :]

        lax.fori_loop(0, s // CHUNK, chunk, jnp.zeros((1, CHUNK), F32))

    shape = jax.ShapeDtypeStruct((N_HEADS, s, EXT), BF16)
    shape_t = jax.ShapeDtypeStruct((N_HEADS, EXT, s), BF16)
    return pl.pallas_call(
        body, name="attn_prep_fwd", in_specs=[ANY, VMEM, VMEM], out_specs=[VMEM] * 5, out_shape=[shape, shape, shape, shape_t, shape_t],
        scratch_shapes=[pltpu.VMEM((s, 3 * BR), F32), pltpu.SemaphoreType.DMA((1,))],
        compiler_params=_params())(proj, proj_bf, f_bias)


def _attn_prep_bwd(proj_bf, dproj, dqe, dke, dve, f_bias):
    s = proj_bf.shape[0]
    scale = 1.0 / math.sqrt(HEAD_DIM)

    def body(z_buf, dproj_in, dq_ref, dk_ref, dv_ref, fb_ref, dproj_ref, dz_buf, dfb_ref, dqkv_buf, sems):
        del dproj_in
        tri = _tri_ones(CHUNK, upper=True)
        lane = lax.broadcasted_iota(jnp.int32, (CHUNK, CHUNK), 1)
        n = s // CHUNK

        def chunk(t, carry):
            suffix, acc = carry
            i = n - 1 - t
            rows = pl.ds(pl.multiple_of(i * CHUNK, CHUNK), CHUNK)
            dq = jnp.zeros((CHUNK, BR), F32)
            dk = jnp.zeros((CHUNK, BR), F32)
            dv = jnp.zeros((CHUNK, BR), F32)
            dcum = jnp.zeros((CHUNK, CHUNK), F32)
            for h in range(N_HEADS):
                back = _from_head(h)
                dqh = dq_ref[h, :, rows].T
                dkh = dk_ref[h, rows, :]
                dq = dq + _dot((dqh * scale).astype(BF16), back)
                dk = dk + _dot(dkh.astype(BF16), back)
                dv = dv + _dot(dv_ref[h, rows, :].astype(BF16), back)
                dcum = dcum + jnp.where(lane == h, dqh[:, LANE_CQ:LANE_CQ + 1] - dkh[:, LANE_CK:LANE_CK + 1], 0.0)
            dqkv_buf[rows, 0:BR] = dq.astype(BF16)
            dqkv_buf[rows, BR:2 * BR] = dk.astype(BF16)
            dqkv_buf[rows, 2 * BR:] = dv.astype(BF16)
            dlf = _tri_sum(tri, dcum) + suffix
            dz = dlf * _sigmoid(-(z_buf[rows, :] + fb_ref[...]))
            dz_buf[rows, :] = dz.astype(BF16)
            return dlf[0:1, :], acc + _part8(dz)

        _, acc = lax.fori_loop(0, n, chunk, (jnp.zeros((1, CHUNK), F32), jnp.zeros((8, CHUNK), F32)))
        dfb_ref[...] = jnp.sum(acc, axis=0, keepdims=True)
        _copy_all([(dqkv_buf, _window(dproj_ref, QKV0, 3 * BR))], sems)

    return pl.pallas_call(
        body, name="attn_prep_bwd", in_specs=[VMEM, ANY, VMEM, VMEM, VMEM, VMEM], out_specs=[ANY, VMEM, VMEM],
        out_shape=[jax.ShapeDtypeStruct(dproj.shape, dproj.dtype), jax.ShapeDtypeStruct((s, CHUNK), BF16),
                   jax.ShapeDtypeStruct((1, CHUNK), F32)],
        scratch_shapes=[pltpu.VMEM((s, 3 * BR), BF16), pltpu.SemaphoreType.DMA((1,))],
        input_output_aliases={1: 0}, compiler_params=_params())(proj_bf, dproj, dqe, dke, dve, f_bias)


ATT_TQ = 256
ATT_FWD_GROUP = 8
ATT_BWD_GROUP = 8
NEG_BIG = -1e30


def _causal_t(q0, k0, tk, tq):
    kpos = k0 + lax.broadcasted_iota(jnp.int32, (tk, tq), 0)
    qpos = q0 + lax.broadcasted_iota(jnp.int32, (tk, tq), 1)
    return kpos <= qpos


STAT_ROWS = 8


def _attn_fwd(qe, ke, vet, rider=None):
    nh, s, w = qe.shape
    tq = min(ATT_TQ, s)
    tk = min(ATT_FWD_GROUP * tq, s)
    per = tk // tq
    nq = s // tq
    r_in, r_in_specs, r_out_specs, r_shapes, r_scratch, r_alias = _rider_parts(rider, 3, 2)
    nr = len(r_in)

    def body(*refs):
        q_ref, k_ref, vt_ref = refs[:3]
        ot_ref, lse_ref = refs[3 + nr:5 + nr]
        h, i = pl.program_id(0), pl.program_id(1)
        start, finish = _ride(rider, (h == 0) & (i == 0), (h == nh - 1) & (i == nq - 1),
                              refs[3:3 + nr], refs[5 + nr:5 + 2 * nr], refs[5 + 2 * nr:])
        start()
        q = q_ref[0]

        def step(k0, carry, width, masked):
            m, l, acc = carry
            ks = pl.ds(pl.multiple_of(k0, tq), width)
            st = _dot_nt(k_ref[0, ks, :], q)
            if masked:
                st = jnp.where(_causal_t(i * tq, k0, width, tq), st, NEG_BIG)
            m_new = jnp.maximum(m, jnp.max(st, axis=0, keepdims=True))
            alpha = jnp.exp(m - m_new)
            pt = jnp.exp(st - m_new)
            l = alpha * l + jnp.sum(pt, axis=0, keepdims=True)
            acc = alpha * acc + _dot(vt_ref[0, :, ks], pt.astype(BF16))
            return m_new, l, acc

        full = i // per
        init = (jnp.full((1, tq), NEG_BIG, F32), jnp.zeros((1, tq), F32), jnp.zeros((w, tq), F32))
        carry = lax.fori_loop(0, full, lambda j, c: step(j * tk, c, tk, False), init)
        m, l, acc = lax.switch(i % per, [functools.partial(step, width=(r + 1) * tq, masked=True) for r in range(per)],
                               full * tk, carry)
        ot_ref[0] = acc / l
        lse_ref[0] = jnp.broadcast_to(m + jnp.log(l), (STAT_ROWS, tq))
        finish()

    outs = pl.pallas_call(
        body, name="attn_fwd", grid=(nh, nq),
        in_specs=[pl.BlockSpec((1, tq, w), lambda h, i: (h, i, 0)), pl.BlockSpec((1, s, w), lambda h, i: (h, 0, 0)),
                  pl.BlockSpec((1, w, s), lambda h, i: (h, 0, 0))] + r_in_specs,
        out_specs=[pl.BlockSpec((1, w, tq), lambda h, i: (h, 0, i)), pl.BlockSpec((1, STAT_ROWS, tq), lambda h, i: (h, 0, i))] + r_out_specs,
        out_shape=[jax.ShapeDtypeStruct((nh, w, s), F32), jax.ShapeDtypeStruct((nh, STAT_ROWS, s), F32)] + r_shapes,
        scratch_shapes=r_scratch, input_output_aliases=r_alias,
        compiler_params=_params("arbitrary", "arbitrary"))(qe, ke, vet, *r_in)
    return outs[0], outs[1], outs[2:]


def _attn_bwd(qe, ke, ket, ve, doe, lse, dd, rider=None):
    nh, s, w = qe.shape
    tq = min(ATT_TQ, s)
    tk = min(ATT_BWD_GROUP * tq, s)
    per = tk // tq
    nq = s // tq
    r_in, r_in_specs, r_out_specs, r_shapes, r_scratch, r_alias = _rider_parts(rider, 7, 3)
    nr = len(r_in)

    def body(*refs):
        q_ref, do_ref, lse_ref, dd_ref, k_ref, kt_ref, v_ref = refs[:7]
        dqt_ref, dk_ref, dv_ref = refs[7 + nr:10 + nr]
        h, i = pl.program_id(0), pl.program_id(1)
        start, finish = _ride(rider, (h == 0) & (i == 0), (h == nh - 1) & (i == nq - 1),
                              refs[7:7 + nr], refs[10 + nr:10 + 2 * nr], refs[10 + 2 * nr:])
        start()

        @pl.when(i == 0)
        def _():
            dk_ref[...] = jnp.zeros_like(dk_ref)
            dv_ref[...] = jnp.zeros_like(dv_ref)

        q = q_ref[0]
        do = do_ref[0]
        lse_row = lse_ref[0, 0:1, :]
        dd_row = dd_ref[0, 0:1, :]

        def step(k0, dqt, width, masked):
            ks = pl.ds(pl.multiple_of(k0, tq), width)
            st = _dot_nt(k_ref[0, ks, :], q)
            pt = jnp.exp(st - lse_row)
            if masked:
                pt = jnp.where(_causal_t(i * tq, k0, width, tq), pt, 0.0)
            dpt = _dot_nt(v_ref[0, ks, :], do)
            dst = (pt * (dpt - dd_row)).astype(BF16)
            dv_ref[0, ks, :] += _dot(pt.astype(BF16), do)
            dk_ref[0, ks, :] += _dot(dst, q)
            return dqt + _dot(kt_ref[0, :, ks], dst)

        full = i // per
        dqt = lax.fori_loop(0, full, lambda j, c: step(j * tk, c, tk, False), jnp.zeros((w, tq), F32))
        dqt_ref[0] = lax.switch(i % per, [functools.partial(step, width=(r + 1) * tq, masked=True) for r in range(per)],
                                full * tk, dqt)
        finish()

    qblk = pl.BlockSpec((1, tq, w), lambda h, i: (h, i, 0))
    stat = pl.BlockSpec((1, STAT_ROWS, tq), lambda h, i: (h, 0, i))
    whole = pl.BlockSpec((1, s, w), lambda h, i: (h, 0, 0))
    whole_t = pl.BlockSpec((1, w, s), lambda h, i: (h, 0, 0))
    outs = pl.pallas_call(
        body, name="attn_bwd", grid=(nh, nq),
        in_specs=[qblk, qblk, stat, stat, whole, whole_t, whole] + r_in_specs,
        out_specs=[pl.BlockSpec((1, w, tq), lambda h, i: (h, 0, i)), whole, whole] + r_out_specs,
        out_shape=[jax.ShapeDtypeStruct((nh, w, s), F32), jax.ShapeDtypeStruct((nh, s, w), F32),
                   jax.ShapeDtypeStruct((nh, s, w), F32)] + r_shapes,
        scratch_shapes=r_scratch, input_output_aliases=r_alias,
        compiler_params=_params("arbitrary", "arbitrary"))(qe, doe, lse, dd, ke, ket, ve, *r_in)
    return outs[0], outs[1], outs[2], outs[3:]


def _bgate_fwd(proj, ot, rider=None):
    s = proj.shape[0]
    r_in, r_in_specs, r_out_specs, r_shapes, r_scratch, r_alias = _rider_parts(rider, 2, 2)
    nr = len(r_in)

    def body(*refs):
        proj_ref, ot_ref = refs[:2]
        y_ref, o_ref = refs[2 + nr:4 + nr]
        g_buf, sems = refs[4 + 2 * nr:6 + 2 * nr]
        start, finish = _ride(rider, True, True, refs[2:2 + nr], refs[4 + nr:4 + 2 * nr], refs[6 + 2 * nr:])
        start()
        _copy_all([(_window(proj_ref, BG0, BR), g_buf)], sems)

        def chunk(i, c):
            rows = pl.ds(pl.multiple_of(i * CHUNK, CHUNK), CHUNK)
            o = jnp.zeros((CHUNK, BR), F32)
            for h in range(N_HEADS):
                back = _from_head(h)
                for part in _split3(ot_ref[h, :, rows].T):
                    o = o + _dot(part, back)
            sg, _ = _silu_and_grad(g_buf[rows, :])
            o_ref[rows, :] = o
            y_ref[rows, :] = (o * sg).astype(BF16)
            return c

        lax.fori_loop(0, s // CHUNK, chunk, 0)
        finish()

    outs = pl.pallas_call(
        body, name="bgate_fwd", in_specs=[ANY, VMEM] + r_in_specs, out_specs=[VMEM, VMEM] + r_out_specs,
        out_shape=[jax.ShapeDtypeStruct((s, BR), BF16), jax.ShapeDtypeStruct((s, BR), F32)] + r_shapes,
        scratch_shapes=[pltpu.VMEM((s, BR), F32), pltpu.SemaphoreType.DMA((1,))] + r_scratch,
        input_output_aliases=r_alias, compiler_params=_params())(proj, ot, *r_in)
    return outs[0], outs[1], outs[2:]


def _bgate_bwd(proj, dproj, o, dy):
    s = proj.shape[0]

    def body(proj_ref, dproj_in, o_ref, dy_ref, dproj_ref, do_ref, dd_ref, g_buf, dg_buf, sems):
        del dproj_in
        _copy_all([(_window(proj_ref, BG0, BR), g_buf)], sems)
        lane = lax.broadcasted_iota(jnp.int32, (BR, CHUNK), 0) // HEAD_DIM
        col = lax.broadcasted_iota(jnp.int32, (BR, CHUNK), 1)
        sel = (lane == col).astype(BF16)

        def chunk(i, c):
            rows = pl.ds(pl.multiple_of(i * CHUNK, CHUNK), CHUNK)
            sg, dsg = _silu_and_grad(g_buf[rows, :])
            dyv = dy_ref[rows, :]
            ov = o_ref[rows, :]
            do = dyv * sg
            dg_buf[rows, :] = (dyv * ov * dsg).astype(BF16)
            prod = _split3(do * ov)
            ddt = (_dot(prod[0], sel) + _dot(prod[1], sel) + _dot(prod[2], sel)).T
            dob = do.astype(BF16)
            for h in range(N_HEADS):
                do_ref[h, rows, :] = _dot(dob, _to_head(h)).astype(BF16)
                dd_ref[h, :, rows] = jnp.broadcast_to(ddt[h:h + 1, :], (STAT_ROWS, CHUNK))
            return c

        lax.fori_loop(0, s // CHUNK, chunk, 0)
        _copy_all([(dg_buf, _window(dproj_ref, BG0, BR))], sems)

    return pl.pallas_call(
        body, name="bgate_bwd", in_specs=[ANY, ANY, VMEM, VMEM], out_specs=[ANY, VMEM, VMEM],
        out_shape=[jax.ShapeDtypeStruct(dproj.shape, dproj.dtype), jax.ShapeDtypeStruct((N_HEADS, s, EXT), BF16),
                   jax.ShapeDtypeStruct((N_HEADS, STAT_ROWS, s), F32)],
        scratch_shapes=[pltpu.VMEM((s, BR), F32), pltpu.VMEM((s, BR), BF16), pltpu.SemaphoreType.DMA((1,))],
        input_output_aliases={1: 0}, compiler_params=_params())(proj, dproj, o, dy)


C_PAD = 8


def _mix_c_fwd(proj, w):
    s = proj.shape[0]

    def body(proj_ref, w_ref, y_ref, cin_buf, g_buf, z_buf, sems):
        _copy_all([(_window(proj_ref, CIN0, 3 * BR), cin_buf), (_window(proj_ref, CG0, BR), g_buf)], sems)
        z_buf[pl.ds(0, C_PAD), :] = jnp.zeros((C_PAD, BR), F32)

        def fill(i, c):
            rows = pl.ds(pl.multiple_of(i * CHUNK, CHUNK), CHUNK)
            z_buf[pl.ds(pl.multiple_of(i * CHUNK + C_PAD, 8), CHUNK), :] = cin_buf[rows, BR:2 * BR] * cin_buf[rows, 2 * BR:]
            return c

        lax.fori_loop(0, s // CHUNK, fill, 0)

        def chunk(i, c):
            r0 = pl.multiple_of(i * CHUNK, CHUNK)
            rows = pl.ds(r0, CHUNK)
            ze = z_buf[pl.ds(r0, CHUNK + C_PAD), :]
            conv = jnp.zeros((CHUNK, BR), F32)
            for k in range(K_SHORT):
                off = C_PAD - (K_SHORT - 1) + k
                conv = conv + w_ref[k:k + 1, :] * ze[off:off + CHUNK]
            sg, _ = _silu_and_grad(g_buf[rows, :])
            y_ref[rows, :] = (cin_buf[rows, 0:BR] * conv * sg).astype(BF16)
            return c

        lax.fori_loop(0, s // CHUNK, chunk, 0)

    return pl.pallas_call(
        body, name="mix_c_fwd", in_specs=[ANY, VMEM], out_specs=VMEM, out_shape=jax.ShapeDtypeStruct((s, BR), BF16),
        scratch_shapes=[pltpu.VMEM((s, 3 * BR), F32), pltpu.VMEM((s, BR), F32), pltpu.VMEM((s + C_PAD, BR), F32),
                        pltpu.SemaphoreType.DMA((2,))],
        compiler_params=_params())(proj, w)


def _mix_c_bwd(proj, dproj, dy, w):
    s = proj.shape[0]

    def body(proj_ref, dproj_in, dy_ref, w_ref, dproj_ref, dw_ref, cin_buf, g_buf, z_buf, dc_buf, dcin_buf, dg_buf, acc, sems):
        del dproj_in
        _copy_all([(_window(proj_ref, CIN0, 3 * BR), cin_buf), (_window(proj_ref, CG0, BR), g_buf)], sems)
        z_buf[pl.ds(0, C_PAD), :] = jnp.zeros((C_PAD, BR), F32)
        dc_buf[pl.ds(s, C_PAD), :] = jnp.zeros((C_PAD, BR), F32)
        acc[...] = jnp.zeros_like(acc)

        def fill(i, c):
            rows = pl.ds(pl.multiple_of(i * CHUNK, CHUNK), CHUNK)
            z_buf[pl.ds(pl.multiple_of(i * CHUNK + C_PAD, 8), CHUNK), :] = cin_buf[rows, BR:2 * BR] * cin_buf[rows, 2 * BR:]
            return c

        lax.fori_loop(0, s // CHUNK, fill, 0)

        def chunk(i, c):
            r0 = pl.multiple_of(i * CHUNK, CHUNK)
            rows = pl.ds(r0, CHUNK)
            ze = z_buf[pl.ds(r0, CHUNK + C_PAD), :]
            taps = [ze[C_PAD - (K_SHORT - 1) + k:C_PAD - (K_SHORT - 1) + k + CHUNK] for k in range(K_SHORT)]
            conv = jnp.zeros((CHUNK, BR), F32)
            for k in range(K_SHORT):
                conv = conv + w_ref[k:k + 1, :] * taps[k]
            sg, dsg = _silu_and_grad(g_buf[rows, :])
            bg = cin_buf[rows, 0:BR]
            dyv = dy_ref[rows, :]
            dconv = dyv * bg * sg
            dc_buf[rows, :] = dconv
            dcin_buf[rows, 0:BR] = (dyv * conv * sg).astype(BF16)
            dg_buf[rows, :] = (dyv * bg * conv * dsg).astype(BF16)
            for k in range(K_SHORT):
                acc[k] += _part8(dconv * taps[k])
            return c

        lax.fori_loop(0, s // CHUNK, chunk, 0)

        def chunk2(i, c):
            r0 = pl.multiple_of(i * CHUNK, CHUNK)
            rows = pl.ds(r0, CHUNK)
            de = dc_buf[pl.ds(r0, CHUNK + C_PAD), :]
            dz = jnp.zeros((CHUNK, BR), F32)
            for k in range(K_SHORT):
                off = K_SHORT - 1 - k
                dz = dz + w_ref[k:k + 1, :] * de[off:off + CHUNK]
            dcin_buf[rows, BR:2 * BR] = (dz * cin_buf[rows, 2 * BR:]).astype(BF16)
            dcin_buf[rows, 2 * BR:] = (dz * cin_buf[rows, BR:2 * BR]).astype(BF16)
            return c

        lax.fori_loop(0, s // CHUNK, chunk2, 0)
        dw_ref[...] = jnp.zeros_like(dw_ref)
        for k in range(K_SHORT):
            dw_ref[k:k + 1, :] = jnp.sum(acc[k], axis=0, keepdims=True)
        _copy_all([(dcin_buf, _window(dproj_ref, CIN0, 3 * BR)), (dg_buf, _window(dproj_ref, CG0, BR))], sems)

    return pl.pallas_call(
        body, name="mix_c_bwd", in_specs=[ANY, ANY, VMEM, VMEM], out_specs=[ANY, VMEM],
        out_shape=[jax.ShapeDtypeStruct(dproj.shape, dproj.dtype), jax.ShapeDtypeStruct((8, BR), F32)],
        scratch_shapes=[pltpu.VMEM((s, 3 * BR), F32), pltpu.VMEM((s, BR), F32), pltpu.VMEM((s + C_PAD, BR), F32),
                        pltpu.VMEM((s + C_PAD, BR), F32), pltpu.VMEM((s, 3 * BR), BF16), pltpu.VMEM((s, BR), BF16),
                        pltpu.VMEM((K_SHORT, 8, BR), F32), pltpu.SemaphoreType.DMA((2,))],
        input_output_aliases={1: 0}, compiler_params=_params())(proj, dproj, dy, w)


D_PAD = 32


def _tap_windows(win, offsets, n):
    rows = win.shape[0]
    rolled, out = {}, []
    for off in offsets:
        r = off % 8
        if r not in rolled:
            rolled[r] = win if r == 0 else pltpu.roll(win, rows - r, 0)
        out.append(rolled[r][off - r:off - r + n])
    return out


def _mix_d_common(g_ref, w_ref, b_ref, lg_ref, lb_ref, hh_buf, r0):
    he = hh_buf[pl.ds(r0, CHUNK + D_PAD), :]
    taps = _tap_windows(he, [D_PAD - (K_CONF - 1) + k for k in range(K_CONF)], CHUNK)
    conv = jnp.zeros((CHUNK, BR), F32) + b_ref[...]
    for k in range(K_CONF):
        conv = conv + w_ref[k:k + 1, :] * taps[k]
    xh, rs = _ln_fwd(conv)
    sw, dsw = _silu_and_grad(xh * lg_ref[...] + lb_ref[...])
    sg, dsg = _silu_and_grad(g_ref[pl.ds(r0, CHUNK), :])
    return taps, xh, rs, sw, dsw, sg, dsg


def _mix_d_fill(cin_buf, hh_buf, s):
    hh_buf[pl.ds(0, D_PAD), :] = jnp.zeros((D_PAD, BR), F32)

    def fill(i, c):
        rows = pl.ds(pl.multiple_of(i * CHUNK, CHUNK), CHUNK)
        hh_buf[pl.ds(pl.multiple_of(i * CHUNK + D_PAD, 8), CHUNK), :] = cin_buf[rows, 0:BR] * _sigmoid(cin_buf[rows, BR:])
        return c

    lax.fori_loop(0, s // CHUNK, fill, 0)


def _mix_d_fwd(proj, w, b, lg, lb):
    s = proj.shape[0]

    def body(proj_ref, w_ref, b_ref, lg_ref, lb_ref, y_ref, cin_buf, g_buf, hh_buf, sems):
        _copy_all([(_window(proj_ref, DGLU0, 2 * BR), cin_buf), (_window(proj_ref, DG0, BR), g_buf)], sems)
        _mix_d_fill(cin_buf, hh_buf, s)

        def chunk(i, c):
            r0 = pl.multiple_of(i * CHUNK, CHUNK)
            _, _, _, sw, _, sg, _ = _mix_d_common(g_buf, w_ref, b_ref, lg_ref, lb_ref, hh_buf, r0)
            y_ref[pl.ds(r0, CHUNK), :] = (sw * sg).astype(BF16)
            return c

        lax.fori_loop(0, s // CHUNK, chunk, 0)

    return pl.pallas_call(
        body, name="mix_d_fwd", in_specs=[ANY, VMEM, VMEM, VMEM, VMEM], out_specs=VMEM,
        out_shape=jax.ShapeDtypeStruct((s, BR), BF16),
        scratch_shapes=[pltpu.VMEM((s, 2 * BR), F32), pltpu.VMEM((s, BR), F32), pltpu.VMEM((s + D_PAD, BR), F32),
                        pltpu.SemaphoreType.DMA((2,))],
        compiler_params=_params())(proj, w, b, lg, lb)


def _mix_d_bwd(proj, dproj, dy, w, b, lg, lb):
    s = proj.shape[0]

    def body(proj_ref, dproj_in, dy_ref, w_ref, b_ref, lg_ref, lb_ref, dproj_ref, dw_ref, db_ref, dlg_ref, dlb_ref,
             cin_buf, g_buf, hh_buf, dc_buf, dcin_buf, dg_buf, acc_w, acc_s, sems):
        del dproj_in
        _copy_all([(_window(proj_ref, DGLU0, 2 * BR), cin_buf), (_window(proj_ref, DG0, BR), g_buf)], sems)
        _mix_d_fill(cin_buf, hh_buf, s)
        dc_buf[pl.ds(s, D_PAD), :] = jnp.zeros((D_PAD, BR), F32)
        acc_w[...] = jnp.zeros_like(acc_w)
        acc_s[...] = jnp.zeros_like(acc_s)

        def chunk(i, c):
            r0 = pl.multiple_of(i * CHUNK, CHUNK)
            rows = pl.ds(r0, CHUNK)
            taps, xh, rs, sw, dsw, sg, dsg = _mix_d_common(g_buf, w_ref, b_ref, lg_ref, lb_ref, hh_buf, r0)
            dyv = dy_ref[rows, :]
            dg_buf[rows, :] = (dyv * sw * dsg).astype(BF16)
            d_ln = dyv * sg * dsw
            acc_s[0] += _part8(d_ln * xh)
            acc_s[1] += _part8(d_ln)
            dc = _ln_bwd(d_ln * lg_ref[...], xh, rs)
            acc_s[2] += _part8(dc)
            dc_buf[rows, :] = dc
            for k in range(K_CONF):
                acc_w[k] += _part8(dc * taps[k])
            return c

        lax.fori_loop(0, s // CHUNK, chunk, 0)

        def chunk2(i, c):
            r0 = pl.multiple_of(i * CHUNK, CHUNK)
            rows = pl.ds(r0, CHUNK)
            de = dc_buf[pl.ds(r0, CHUNK + D_PAD), :]
            dh = jnp.zeros((CHUNK, BR), F32)
            for k, win in enumerate(_tap_windows(de, [K_CONF - 1 - k for k in range(K_CONF)], CHUNK)):
                dh = dh + w_ref[k:k + 1, :] * win
            a = cin_buf[rows, 0:BR]
            sig = _sigmoid(cin_buf[rows, BR:])
            dcin_buf[rows, 0:BR] = (dh * sig).astype(BF16)
            dcin_buf[rows, BR:] = (dh * a * sig * (1.0 - sig)).astype(BF16)
            return c

        lax.fori_loop(0, s // CHUNK, chunk2, 0)
        dw_ref[...] = jnp.zeros_like(dw_ref)
        for k in range(K_CONF):
            dw_ref[k:k + 1, :] = jnp.sum(acc_w[k], axis=0, keepdims=True)
        dlg_ref[...] = jnp.sum(acc_s[0], axis=0, keepdims=True)
        dlb_ref[...] = jnp.sum(acc_s[1], axis=0, keepdims=True)
        db_ref[...] = jnp.sum(acc_s[2], axis=0, keepdims=True)
        _copy_all([(dcin_buf, _window(dproj_ref, DGLU0, 2 * BR)), (dg_buf, _window(dproj_ref, DG0, BR))], sems)

    vec = jax.ShapeDtypeStruct((1, BR), F32)
    return pl.pallas_call(
        body, name="mix_d_bwd", in_specs=[ANY, ANY, VMEM, VMEM, VMEM, VMEM, VMEM], out_specs=[ANY, VMEM, VMEM, VMEM, VMEM],
        out_shape=[jax.ShapeDtypeStruct(dproj.shape, dproj.dtype), jax.ShapeDtypeStruct((32, BR), F32), vec, vec, vec],
        scratch_shapes=[pltpu.VMEM((s, 2 * BR), F32), pltpu.VMEM((s, BR), F32), pltpu.VMEM((s + D_PAD, BR), F32),
                        pltpu.VMEM((s + D_PAD, BR), F32), pltpu.VMEM((s, 2 * BR), BF16), pltpu.VMEM((s, BR), BF16),
                        pltpu.VMEM((K_CONF, 8, BR), F32), pltpu.VMEM((3, 8, BR), F32), pltpu.SemaphoreType.DMA((2,))],
        input_output_aliases={1: 0}, compiler_params=_params())(proj, dproj, dy, w, b, lg, lb)


MERGE_TM = 256


def _merge_fwd(x, proj, ys, wb, wo):
    s = x.shape[0]
    tm = min(MERGE_TM, s)

    def body(x_ref, lg_ref, ya, yb, yc, yd, wb_ref, wo_ref, xn_ref, mg_ref):
        merged = jnp.zeros((tm, D_MODEL), F32)
        for n, y_ref in enumerate((ya, yb, yc, yd)):
            gate = _sigmoid(lg_ref[:, n * D_MODEL:(n + 1) * D_MODEL])
            merged = merged + gate * _dot(y_ref[...], wb_ref[n])
        mb = merged.astype(BF16)
        mg_ref[...] = mb
        xn_ref[...] = x_ref[...] + _dot(mb, wo_ref[...])

    row = lambda w: pl.BlockSpec((tm, w), lambda i: (i, 0))
    return pl.pallas_call(
        body, name="merge_fwd", grid=(s // tm,),
        in_specs=[row(D_MODEL), pl.BlockSpec((tm, MRG_COLS), lambda i: (i, MRG0 // MRG_COLS)), row(BR), row(BR), row(BR), row(BR),
                  pl.BlockSpec((N_HEADS, BR, D_MODEL), lambda i: (0, 0, 0)), pl.BlockSpec((D_MODEL, D_MODEL), lambda i: (0, 0))],
        out_specs=[row(D_MODEL), row(D_MODEL)],
        out_shape=[jax.ShapeDtypeStruct((s, D_MODEL), F32), jax.ShapeDtypeStruct((s, D_MODEL), BF16)],
        compiler_params=_params("arbitrary"))(x, proj, *ys, wb, wo)


def _merge_bwd(dxn, proj, ys, merged, wb, wo, rider=None):
    s = dxn.shape[0]
    tm = min(MERGE_TM, s)
    steps = s // tm
    r_in, r_in_specs, r_out_specs, r_shapes, r_scratch, r_alias = _rider_parts(rider, 9, 7)
    nr = len(r_in)

    def body(*refs):
        dx_ref, lg_ref, ya, yb, yc, yd, mg_ref, wb_ref, wo_ref = refs[:9]
        dlg_ref, da, db, dc, dd, dwo_ref, dwb_ref = refs[9 + nr:16 + nr]
        start, finish = _ride(rider, pl.program_id(0) == 0, pl.program_id(0) == steps - 1,
                              refs[9:9 + nr], refs[16 + nr:16 + 2 * nr], refs[16 + 2 * nr:])
        start()

        @pl.when(pl.program_id(0) == 0)
        def _():
            dwo_ref[...] = jnp.zeros_like(dwo_ref)
            dwb_ref[...] = jnp.zeros_like(dwb_ref)

        dxb = dx_ref[...].astype(BF16)
        d_merged = _dot_nt(dxb, wo_ref[...])
        dwo_ref[...] += _dot_tn(mg_ref[...], dxb)
        for n, (y_ref, dy_ref) in enumerate(((ya, da), (yb, db), (yc, dc), (yd, dd))):
            yv = y_ref[...]
            gate = _sigmoid(lg_ref[:, n * D_MODEL:(n + 1) * D_MODEL])
            pn = _dot(yv, wb_ref[n])
            dlg_ref[:, n * D_MODEL:(n + 1) * D_MODEL] = (d_merged * pn * gate * (1.0 - gate)).astype(BF16)
            dpn = (d_merged * gate).astype(BF16)
            dy_ref[...] = _dot_nt(dpn, wb_ref[n])
            dwb_ref[n] += _dot_tn(yv, dpn)
        finish()

    row = lambda w: pl.BlockSpec((tm, w), lambda i: (i, 0))
    wb_spec = pl.BlockSpec((N_HEADS, BR, D_MODEL), lambda i: (0, 0, 0))
    wo_spec = pl.BlockSpec((D_MODEL, D_MODEL), lambda i: (0, 0))
    dy_shape = jax.ShapeDtypeStruct((s, BR), F32)
    outs = pl.pallas_call(
        body, name="merge_bwd", grid=(steps,),
        in_specs=[row(D_MODEL), pl.BlockSpec((tm, MRG_COLS), lambda i: (i, MRG0 // MRG_COLS)), row(BR), row(BR), row(BR), row(BR), row(D_MODEL),
                  wb_spec, wo_spec] + r_in_specs,
        out_specs=[pl.BlockSpec((tm, MRG_COLS), lambda i: (i, MRG0 // MRG_COLS)), row(BR), row(BR), row(BR), row(BR), wo_spec, wb_spec] + r_out_specs,
        out_shape=[jax.ShapeDtypeStruct((s, PCOLS), BF16), dy_shape, dy_shape, dy_shape, dy_shape,
                   jax.ShapeDtypeStruct((D_MODEL, D_MODEL), F32), jax.ShapeDtypeStruct((N_HEADS, BR, D_MODEL), F32)] + r_shapes,
        scratch_shapes=r_scratch, input_output_aliases=r_alias,
        compiler_params=_params("arbitrary"))(dxn, proj, *ys, merged, wb, wo, *r_in)
    return outs[0], outs[1:5], outs[5], outs[6], outs[7:]


def _loss_head(x, target, g):
    s = x.shape[0]
    tm = min(512, s)

    def body(x_ref, t_ref, g_ref, loss_ref, dx_ref, dg_ref):
        @pl.when(pl.program_id(0) == 0)
        def _():
            loss_ref[...] = jnp.zeros_like(loss_ref)
            dg_ref[...] = jnp.zeros_like(dg_ref)

        xv = x_ref[...]
        gv = g_ref[...]
        r = lax.rsqrt(jnp.mean(xv * xv, axis=-1, keepdims=True) + EPS)
        xn = xv * r
        err = xn * gv - t_ref[...]
        loss_ref[...] += 0.5 * jnp.sum(jnp.mean(err * err, axis=-1, keepdims=True))
        dy = err * (1.0 / D_MODEL)
        dg_ref[...] += _part8(dy * xn)
        gy = dy * gv
        dx_ref[...] = r * (gy - xn * jnp.mean(xn * gy, axis=-1, keepdims=True))

    row = pl.BlockSpec((tm, D_MODEL), lambda i: (i, 0))
    return pl.pallas_call(
        body, name="loss_head", grid=(s // tm,),
        in_specs=[row, row, pl.BlockSpec((1, D_MODEL), lambda i: (0, 0))],
        out_specs=[pl.BlockSpec((8, 128), lambda i: (0, 0)), row, pl.BlockSpec((8, D_MODEL), lambda i: (0, 0))],
        out_shape=[jax.ShapeDtypeStruct((8, 128), F32), jax.ShapeDtypeStruct((s, D_MODEL), F32), jax.ShapeDtypeStruct((8, D_MODEL), F32)],
        compiler_params=_params("arbitrary"))(x, target, g)


_SEGMENTS = ((0, 512, AUV0), (512, 256, AG0), (768, 768, QKV0), (1540, 256, BG0), (1796, 768, CIN0),
             (2564, 256, CG0), (2820, 512, DGLU0), (3332, 256, DG0), (3588, 4096, MRG0))


def _sgu_bias_rows(sgu_b):
    return jnp.repeat(sgu_b.T, HEAD_DIM, axis=1)


def _layer_fwd(x, p, next_blocks=None):
    proj, proj_bf, h = _inproj_fwd(x, p["norm_g"], p["w_in"], p["w_bf"])
    ya = _mix_a_fwd(proj, p["sgu_w"], _sgu_bias_rows(p["sgu_b"]), p["sgu_ln_g"], p["sgu_ln_b"])
    qe, ke, ve, ket, vet = _attn_prep_fwd(proj, proj_bf, p["f_bias"])
    ot, lse, landed = _attn_fwd(qe, ke, vet, rider=None if next_blocks is None else _gather_send_rider(next_blocks))
    yb, o, gathered = _bgate_fwd(proj, ot, rider=None if next_blocks is None else _gather_forward_rider(landed))
    yc = _mix_c_fwd(proj, p["short_conv_w"])
    yd = _mix_d_fwd(proj, p["conf_dw_w"], p["conf_dw_b"], p["conf_ln_g"], p["conf_ln_b"])
    ys = (ya, yb, yc, yd)
    x_next, merged = _merge_fwd(x, proj, ys, p["w_branch"], p["w_out"])
    saved = dict(x=x, proj=proj, proj_bf=proj_bf, h=h, ys=ys, merged=merged, qe=qe, ke=ke, ve=ve, ket=ket, lse=lse, o=o)
    return x_next, saved, gathered


def _grads_by_chip(d_win, d_wb, d_wo):
    return [jnp.stack([_w_in_shard(d_win[0], d_win[1], j) for j in range(N_CHIPS)]),
            jnp.stack([d_wb[:, :, j * BR:(j + 1) * BR].reshape(N_HEADS * BR, BR).astype(BF16) for j in range(N_CHIPS)]),
            jnp.stack([d_wo[j * BR:(j + 1) * BR].astype(BF16) for j in range(N_CHIPS)])]


def _layer_bwd(dxn, p, sv, prev_reduce=None, reduce_self=None):
    proj = sv["proj"]
    if prev_reduce is None:
        dproj, dys, d_wo, d_wb, _ = _merge_bwd(dxn, proj, sv["ys"], sv["merged"], p["w_branch"], p["w_out"])
        chip_rider = None
    else:
        pc, chip, arrays = prev_reduce
        mine, other = _row_halves(arrays, pc)
        dproj, dys, d_wo, d_wb, from_sibling = _merge_bwd(dxn, proj, sv["ys"], sv["merged"], p["w_branch"], p["w_out"],
                                                           rider=_pair_rider(other))
        halves = _add_pairs(mine, from_sibling)
        chip_rider = _chip_rider(halves)
    dproj, d_sgu_w, d_bias_rows, d_sgu_lg, d_sgu_lb = _mix_a_bwd(
        proj, dproj, dys[0], p["sgu_w"], _sgu_bias_rows(p["sgu_b"]), p["sgu_ln_g"], p["sgu_ln_b"])
    dproj, doe, dd = _bgate_bwd(proj, dproj, sv["o"], dys[1])
    dqe, dke, dve, from_chips = _attn_bwd(sv["qe"], sv["ke"], sv["ket"], sv["ve"], doe, sv["lse"], dd, rider=chip_rider)
    reduced = None
    if prev_reduce is not None:
        reduced = _sum_chips(halves, chip, from_chips)
    dproj, dproj_bf, d_fb = _attn_prep_bwd(sv["proj_bf"], dproj, dqe, dke, dve, p["f_bias"])
    dproj, d_sc = _mix_c_bwd(proj, dproj, dys[2], p["short_conv_w"])
    dproj, d_cw, d_cb, d_clg, d_clb = _mix_d_bwd(proj, dproj, dys[3], p["conf_dw_w"], p["conf_dw_b"], p["conf_ln_g"], p["conf_ln_b"])
    d_win = _inproj_bwd_w(sv["h"], dproj, dproj_bf)
    reduced_self, self_rider = None, None
    if reduce_self is not None:
        pc, chip = reduce_self
        mine, other = _row_halves(_grads_by_chip(d_win, d_wb, d_wo), pc)
        own_halves = _add_pairs(mine, _pair_exchange(other))
        self_rider = _chip_rider(own_halves)
    dx, dg8, from_chips_self = _inproj_bwd_x(dproj, p["w_in"], sv["x"], dxn, p["norm_g"], dproj_bf, p["w_bf"], rider=self_rider)
    if reduce_self is not None:
        reduced_self = _sum_chips(own_halves, chip, from_chips_self)
    grads = dict(
        norm_g=jnp.sum(dg8, axis=0), w_in=d_win, f_bias=d_fb[0, :N_HEADS], sgu_w=d_sgu_w,
        sgu_b=d_bias_rows.reshape(CHUNK, N_HEADS, HEAD_DIM).sum(axis=-1).T,
        sgu_ln_g=d_sgu_lg[0], sgu_ln_b=d_sgu_lb[0], short_conv_w=d_sc[:K_SHORT], conf_dw_w=d_cw[:K_CONF],
        conf_dw_b=d_cb[0], conf_ln_g=d_clg[0], conf_ln_b=d_clb[0], w_branch=d_wb, w_out=d_wo)
    return dx, grads, reduced, reduced_self


def _row_halves(arrays, pc):
    half = lambda a, i: lax.dynamic_slice_in_dim(a, i * (a.shape[-2] // 2), a.shape[-2] // 2, axis=a.ndim - 2)
    return [half(a, pc) for a in arrays], [half(a, 1 - pc) for a in arrays]


def _local_step(x, target, layers, final_g):
    saved = []
    for p in layers:
        x, sv, _ = _layer_fwd(x, p)
        saved.append(sv)
    loss8, dx, dfg8 = _loss_head(x, target, final_g)
    grads = [None] * len(layers)
    for l in reversed(range(len(layers))):
        dx, grads[l], _, _ = _layer_bwd(dx, layers[l], saved[l])
    return loss8[0, 0], dx, grads, jnp.sum(dfg8, axis=0)


def _gather_chips(blocks):
    n = len(blocks)

    def body(*refs):
        ins, outs, (send_sems, recv_sems) = refs[:n], refs[n:2 * n], refs[2 * n:]
        x, y, cc = _place()
        me, sibling = (x, y, cc), (x, y, 1 - cc)
        chips = _other_chips(x, y)

        def copy(a, k, chip, layer, to, src=None):
            dst = outs[a].at[2 * chip[0] + chip[1], layer]
            return pltpu.make_async_remote_copy(src_ref=dst if src is None else src, dst_ref=dst, send_sem=send_sems.at[a, k],
                                                recv_sem=recv_sems.at[a, k], device_id=to, device_id_type=MESH)

        first = [copy(a, j, (x, y), cc, (*chip, cc), src=ins[a]) for j, chip in enumerate(chips) for a in range(n)]
        for cp in first:
            cp.start()
        passed = []
        for j, chip in enumerate(chips):
            for a in range(n):
                copy(a, j, chip, cc, me).wait_recv()
                passed.append(copy(a, 3 + j, chip, cc, sibling))
                passed[-1].start()
        for j, chip in enumerate(chips):
            for a in range(n):
                copy(a, 3 + j, chip, 1 - cc, me).wait_recv()
        for cp in first + passed:
            cp.wait_send()

    return pl.pallas_call(
        body, name="gather_chips", in_specs=[ANY] * n, out_specs=[ANY] * n,
        out_shape=[jax.ShapeDtypeStruct((N_CHIPS, 2) + b.shape, b.dtype) for b in blocks],
        scratch_shapes=[pltpu.SemaphoreType.DMA((n, 6)), pltpu.SemaphoreType.DMA((n, 6))])(*blocks)


def _pair_exchange(arrays):
    n = len(arrays)

    def body(*refs):
        ins, outs, (send_sems, recv_sems) = refs[:n], refs[n:2 * n], refs[2 * n:]
        x, y, cc = _place()
        cps = [pltpu.make_async_remote_copy(src_ref=ins[a], dst_ref=outs[a], send_sem=send_sems.at[a], recv_sem=recv_sems.at[a],
                                            device_id=(x, y, 1 - cc), device_id_type=MESH) for a in range(n)]
        for cp in cps:
            cp.start()
        for cp in cps:
            cp.wait()

    return pl.pallas_call(
        body, name="pair_exchange", in_specs=[ANY] * n, out_specs=[ANY] * n,
        out_shape=[jax.ShapeDtypeStruct(a.shape, a.dtype) for a in arrays],
        scratch_shapes=[pltpu.SemaphoreType.DMA((n,)), pltpu.SemaphoreType.DMA((n,))])(*arrays)


def _allreduce8(arrays):
    n = len(arrays)

    def body(*refs):
        ins, outs, recvs = refs[:n], refs[n:2 * n], refs[2 * n:3 * n]
        send_sems, recv_sems = refs[3 * n:]
        x, y, cc = _place()
        for a in range(n):
            outs[a][...] = ins[a][...]
        for k, peer in enumerate([(x, y, 1 - cc), (1 - x, y, cc), (x, 1 - y, cc)]):
            cps = [pltpu.make_async_remote_copy(src_ref=outs[a], dst_ref=recvs[a].at[k], send_sem=send_sems.at[a, k],
                                                recv_sem=recv_sems.at[a, k], device_id=peer, device_id_type=MESH) for a in range(n)]
            for cp in cps:
                cp.start()
            for cp in cps:
                cp.wait()
            for a in range(n):
                outs[a][...] = outs[a][...] + recvs[a][k]

    return pl.pallas_call(
        body, name="allreduce8", in_specs=[VMEM] * n, out_specs=[VMEM] * n,
        out_shape=[jax.ShapeDtypeStruct(a.shape, F32) for a in arrays],
        scratch_shapes=[pltpu.VMEM((3,) + a.shape, F32) for a in arrays] + [pltpu.SemaphoreType.DMA((n, 3)), pltpu.SemaphoreType.DMA((n, 3))],
        compiler_params=_params())(*arrays)


def _row_tile(rows, cols, limit_bytes=1 << 20):
    t = rows
    while t % 16 == 0 and t * cols * 4 > limit_bytes:
        t //= 2
    return t


REDUCE_STEPS = 8


def _add_pairs(xs, ys):
    n = len(xs)

    def body(*refs):
        for a in range(n):
            refs[2 * n + a][...] = (refs[a][...].astype(F32) + refs[n + a][...].astype(F32)).astype(BF16)

    specs = [pl.BlockSpec((x.shape[0], x.shape[1] // REDUCE_STEPS, x.shape[2]), lambda i: (0, i, 0)) for x in xs]
    return pl.pallas_call(body, name="add_pairs", grid=(REDUCE_STEPS,), in_specs=specs + specs, out_specs=specs,
                          out_shape=[jax.ShapeDtypeStruct(x.shape, BF16) for x in xs], compiler_params=_params("arbitrary"))(*xs, *ys)


def _sum_chips(halves, chip, recvs):
    n = len(halves)

    def body(chip_ref, *refs):
        del chip_ref
        for a in range(n):
            acc = refs[a][0].astype(F32)
            for k in range(3):
                acc = acc + refs[n + a][k].astype(F32)
            refs[2 * n + a][...] = acc

    rows = [h.shape[1] // REDUCE_STEPS for h in halves]
    own_specs = [pl.BlockSpec((1, r, h.shape[2]), lambda i, c: (c[0], i, 0)) for h, r in zip(halves, rows)]
    recv_specs = [pl.BlockSpec((3, r, h.shape[2]), lambda i, c: (0, i, 0)) for h, r in zip(halves, rows)]
    out_specs = [pl.BlockSpec((r, h.shape[2]), lambda i, c: (i, 0)) for h, r in zip(halves, rows)]
    grid_spec = pltpu.PrefetchScalarGridSpec(num_scalar_prefetch=1, grid=(REDUCE_STEPS,), in_specs=own_specs + recv_specs,
                                             out_specs=out_specs)
    return pl.pallas_call(body, name="sum_chips", grid_spec=grid_spec,
                          out_shape=[jax.ShapeDtypeStruct(h.shape[1:], F32) for h in halves],
                          compiler_params=_params("arbitrary"))(jnp.reshape(chip, (1,)).astype(jnp.int32), *halves, *recvs)


def _adamw_update(w_ref, m_ref, v_ref, g_ref, d_ref, mo_ref, vo_ref):
    gv = g_ref[...]
    mn = ADAM_B1 * m_ref[...] + (1.0 - ADAM_B1) * gv
    vn = ADAM_B2 * v_ref[...] + (1.0 - ADAM_B2) * (gv * gv)
    m_hat = mn / (1.0 - ADAM_B1 ** ADAM_STEP)
    v_hat = vn / (1.0 - ADAM_B2 ** ADAM_STEP)
    d_ref[...] = -ADAM_LR * (m_hat / (jnp.sqrt(v_hat) + ADAM_EPS) + ADAM_WD * w_ref[...])
    mo_ref[...] = mn
    vo_ref[...] = vn


def _adamw(w, m, v, g):
    r, c = w.shape
    t = _row_tile(r, c)

    def body(*refs):
        _adamw_update(*refs)

    spec = pl.BlockSpec((t, c), lambda i: (i, 0))
    shape = jax.ShapeDtypeStruct((r, c), F32)
    return pl.pallas_call(body, name="adamw", grid=(r // t,), in_specs=[spec] * 4, out_specs=[spec] * 3,
                          out_shape=[shape] * 3, compiler_params=_params("arbitrary"))(w, m, v, g)


def _adamw_small(ws, ms, vs, gs):
    n = len(ws)

    def body(*refs):
        w_refs, m_refs, v_refs, g_refs = refs[:n], refs[n:2 * n], refs[2 * n:3 * n], refs[3 * n:4 * n]
        d_refs, mo_refs, vo_refs = refs[4 * n:5 * n], refs[5 * n:6 * n], refs[6 * n:]
        for a in range(n):
            _adamw_update(w_refs[a], m_refs[a], v_refs[a], g_refs[a], d_refs[a], mo_refs[a], vo_refs[a])

    shapes = [jax.ShapeDtypeStruct(w.shape, F32) for w in ws]
    outs = pl.pallas_call(body, name="adamw_small", in_specs=[VMEM] * (4 * n), out_specs=[VMEM] * (3 * n),
                          out_shape=shapes * 3, compiler_params=_params())(*ws, *ms, *vs, *gs)
    return outs[:n], outs[n:2 * n], outs[2 * n:]


SMALL =("norm_g", "f_bias", "sgu_w", "sgu_b", "sgu_ln_g", "sgu_ln_b", "short_conv_w", "conf_dw_w", "conf_dw_b",
         "conf_ln_g", "conf_ln_b", "final_g")
WEIGHTS = ("norm_g", "w_in", "f_bias", "sgu_w", "sgu_b", "sgu_ln_g", "sgu_ln_b", "short_conv_w", "conf_dw_w",
           "conf_dw_b", "conf_ln_g", "conf_ln_b", "w_branch", "w_out", "final_g")


def _aligned_pieces():
    out, pos = [], 0
    for src, width, dst in sorted(_SEGMENTS, key=lambda t: t[2]):
        if dst > pos:
            out.append((None, 0, dst - pos))
        lo = src
        while lo < src + width:
            j = lo // SHARD_COLS
            hi = min(src + width, (j + 1) * SHARD_COLS)
            out.append((j, lo - j * SHARD_COLS, hi - lo))
            lo = hi
        pos = dst + width
    return out


def _w_in_aligned(shards):
    rows, dtype = shards[0].shape[0], shards[0].dtype
    return jnp.concatenate([jnp.zeros((rows, w), dtype) if j is None else shards[j][:, c0:c0 + w]
                            for j, c0, w in _aligned_pieces()], axis=1)


def _w_bf(shards):
    j, c0 = BF_SRC // SHARD_COLS, BF_SRC % SHARD_COLS
    assert c0 + BF_COLS <= SHARD_COLS
    return jnp.pad(shards[j][:, c0:c0 + BF_COLS], ((0, 0), (0, CHUNK - BF_COLS)))


def _w_in_shard(g, g_bf, j):
    lo_s, hi_s = j * SHARD_COLS, (j + 1) * SHARD_COLS
    parts = []
    for src, width, dst in sorted(_SEGMENTS + ((BF_SRC, BF_COLS, None),), key=lambda t: t[0]):
        lo, hi = max(src, lo_s), min(src + width, hi_s)
        if lo < hi:
            parts.append(g_bf[:, lo - src:hi - src] if dst is None else g[:, dst + lo - src:dst + hi - src])
    return jnp.concatenate(parts + [jnp.zeros((g.shape[0], SHARD_PAD - SHARD_COLS), g.dtype)], axis=1)


def kernel(x, norm_g, w_in, f_bias, sgu_w, sgu_b, sgu_ln_g, sgu_ln_b, short_conv_w, conf_dw_w, conf_dw_b, conf_ln_g, conf_ln_b, w_branch, w_out, final_g, loss_target, m_norm_g, m_w_in, m_f_bias, m_sgu_w, m_sgu_b, m_sgu_ln_g, m_sgu_ln_b, m_short_conv_w, m_conf_dw_w, m_conf_dw_b, m_conf_ln_g, m_conf_ln_b, m_w_branch, m_w_out, m_final_g, v_norm_g, v_w_in, v_f_bias, v_sgu_w, v_sgu_b, v_sgu_ln_g, v_sgu_ln_b, v_short_conv_w, v_conf_dw_w, v_conf_dw_b, v_conf_ln_g, v_conf_ln_b, v_w_branch, v_w_out, v_final_g):
    px, py, pc = _place()
    chip = 2 * px + py
    depth = w_in.shape[0]
    w = dict(norm_g=norm_g, w_in=w_in, f_bias=f_bias, sgu_w=sgu_w, sgu_b=sgu_b, sgu_ln_g=sgu_ln_g, sgu_ln_b=sgu_ln_b,
             short_conv_w=short_conv_w, conf_dw_w=conf_dw_w, conf_dw_b=conf_dw_b, conf_ln_g=conf_ln_g, conf_ln_b=conf_ln_b,
             w_branch=w_branch, w_out=w_out, final_g=final_g)
    m = dict(norm_g=m_norm_g, w_in=m_w_in, f_bias=m_f_bias, sgu_w=m_sgu_w, sgu_b=m_sgu_b, sgu_ln_g=m_sgu_ln_g,
             sgu_ln_b=m_sgu_ln_b, short_conv_w=m_short_conv_w, conf_dw_w=m_conf_dw_w, conf_dw_b=m_conf_dw_b,
             conf_ln_g=m_conf_ln_g, conf_ln_b=m_conf_ln_b, w_branch=m_w_branch, w_out=m_w_out, final_g=m_final_g)
    v = dict(norm_g=v_norm_g, w_in=v_w_in, f_bias=v_f_bias, sgu_w=v_sgu_w, sgu_b=v_sgu_b, sgu_ln_g=v_sgu_ln_g,
             sgu_ln_b=v_sgu_ln_b, short_conv_w=v_short_conv_w, conf_dw_w=v_conf_dw_w, conf_dw_b=v_conf_dw_b,
             conf_ln_g=v_conf_ln_g, conf_ln_b=v_conf_ln_b, w_branch=v_w_branch, w_out=v_w_out, final_g=v_final_g)

    local = (jnp.pad(w_in, ((0, 0), (0, 0), (0, SHARD_PAD - SHARD_COLS))).astype(BF16),
             w_branch.astype(BF16).reshape(depth, N_HEADS * BR, BR), w_out.astype(BF16))
    pick = lambda a, i: lax.dynamic_index_in_dim(a, i, 0, keepdims=False)
    flat = lambda a: a.reshape(-1, a.shape[-1])

    def my_half(l):
        return _row_halves([a[l] for a in local], pc)[0]

    def all_chips(gathered, l):
        return [lax.dynamic_update_index_in_dim(got.reshape((N_CHIPS,) + a.shape[1:]), a[l], chip, 0) for got, a in zip(gathered, local)]

    conv_ch = BR // N_CHIPS
    place = lambda a: lax.dynamic_update_slice_in_dim(jnp.zeros(a.shape[:-1] + (BR,), F32), a * (pc == 0).astype(F32),
                                                      conv_ch * chip, axis=2).reshape(-1, BR)
    short_full, conf_full = _allreduce8([place(short_conv_w), place(conf_dw_w)])
    short_full, conf_full = short_full.reshape(depth, K_SHORT, BR), conf_full.reshape(depth, K_CONF, BR)

    def layer_params(l, shards):
        wi_all, wb_all, wo_all = shards
        return dict(
            norm_g=norm_g[l][None], w_in=_w_in_aligned([wi_all[j] for j in range(N_CHIPS)]),
            w_bf=_w_bf([wi_all[j] for j in range(N_CHIPS)]),
            f_bias=jnp.pad(f_bias[l], (0, CHUNK - N_HEADS))[None],
            sgu_w=sgu_w[l], sgu_b=sgu_b[l], sgu_ln_g=sgu_ln_g[l][None], sgu_ln_b=sgu_ln_b[l][None],
            short_conv_w=short_full[l], conf_dw_w=conf_full[l], conf_dw_b=conf_dw_b[l][None],
            conf_ln_g=conf_ln_g[l][None], conf_ln_b=conf_ln_b[l][None],
            w_branch=jnp.concatenate([wb_all[j].reshape(N_HEADS, BR, BR) for j in range(N_CHIPS)], axis=2),
            w_out=jnp.concatenate([wo_all[j] for j in range(N_CHIPS)], axis=0))

    shards = all_chips(_gather_chips(my_half(0)), 0)
    params, saved, xs = [], [], x[0]
    for l in range(depth):
        params.append(layer_params(l, shards))
        nxt = my_half(l + 1) if l + 1 < depth else None
        xs, sv, gathered = _layer_fwd(xs, params[l], next_blocks=nxt)
        saved.append(sv)
        if nxt is not None:
            shards = all_chips(gathered, l + 1)
    loss8, dx, dfg8 = _loss_head(xs, loss_target[0], final_g[None])
    d_final_g = jnp.sum(dfg8, axis=0)
    loss = lax.psum(loss8[0, 0], ("x", "y", "c"))

    grads, reduced, pending = [None] * depth, [None] * depth, None
    for l in reversed(range(depth)):
        dx, grads[l], red, red_self = _layer_bwd(
            dx, params[l], saved[l], prev_reduce=None if pending is None else (pc, chip, pending),
            reduce_self=(pc, chip) if l == 0 else None)
        if pending is not None:
            reduced[l + 1] = red
        if l == 0:
            reduced[0] = red_self
        else:
            pending = _grads_by_chip(grads[l]["w_in"], grads[l]["w_branch"], grads[l]["w_out"])
    grad_x = dx
    own = [a for red in reduced for a in red]
    rows = [jnp.where(pc == 0, jnp.concatenate([a, b], axis=0), jnp.concatenate([b, a], axis=0))
            for a, b in zip(own, _pair_exchange(own))]
    gi, gb, go = [jnp.stack([rows[3 * l + k] for l in range(depth)]) for k in range(3)]
    g = dict(w_in=gi[:, :, :SHARD_COLS], w_branch=gb.reshape(depth, N_HEADS, BR, BR), w_out=go)

    two_d = lambda a: a.reshape(-1, a.shape[-1])
    at_least_2d = lambda a: a if a.ndim >= 2 else a[None]
    small_local = [jnp.stack([grads[l][n] for l in range(depth)]) for n in SMALL[:-1]] + [d_final_g]
    for n, a, b in zip(SMALL, small_local, _allreduce8([at_least_2d(a) for a in small_local])):
        g[n] = b.reshape(a.shape)
    for n in ("short_conv_w", "conf_dw_w"):
        g[n] = lax.dynamic_slice_in_dim(g[n], conv_ch * chip, conv_ch, axis=2)

    delta, new_m, new_v = {}, {}, {}
    for n in ("w_in", "w_branch", "w_out"):
        shp = w[n].shape
        d_, m_, v_ = _adamw(two_d(w[n]), two_d(m[n]), two_d(v[n]), two_d(g[n]))
        delta[n], new_m[n], new_v[n] = d_.reshape(shp), m_.reshape(shp), v_.reshape(shp)
    d_, m_, v_ = _adamw_small(*[[at_least_2d(t[n]) for n in SMALL] for t in (w, m, v, g)])
    for n, a, b, c_ in zip(SMALL, d_, m_, v_):
        delta[n], new_m[n], new_v[n] = a.reshape(w[n].shape), b.reshape(w[n].shape), c_.reshape(w[n].shape)

    return (loss, grad_x[None], *[g[n] for n in WEIGHTS], *[delta[n] for n in WEIGHTS],
            *[new_m[n] for n in WEIGHTS], *[new_v[n] for n in WEIGHTS])
```

```python
import functools
import math

import jax
import jax.numpy as jnp
from jax import lax
from jax.experimental import pallas as pl
from jax.experimental.pallas import tpu as pltpu

F32 = jnp.float32
BF16 = jnp.bfloat16

D_MODEL = 1024
BR = 256
N_HEADS = 4
HEAD_DIM = 64
CHUNK = 128
K_SHORT = 3
K_CONF = 31
EPS = 1e-6
IN_COLS = 7684
SHARD_COLS = IN_COLS // 4
SHARD_PAD = 2048
N_CHIPS = 4

MRG0, MRG_COLS = 0, 4096
QKV0, CIN0, AUV0, DGLU0 = 4096, 4864, 5632, 6144
AG0, BG0, CG0, DG0 = 6656, 6912, 7168, 7424
PCOLS = 7680
BF_SRC, BF_COLS = 1536, 4

V7X_VMEM_BYTES = 64 * 1024 * 1024
VMEM_LIMIT = V7X_VMEM_BYTES * 7 // 8

ADAM_LR, ADAM_B1, ADAM_B2, ADAM_EPS, ADAM_WD, ADAM_STEP = 0.001, 0.9, 0.999, 1e-08, 0.01, 10

MESH = pl.DeviceIdType.MESH
ANY = pl.BlockSpec(memory_space=pl.ANY)
VMEM = pl.BlockSpec(memory_space=pltpu.VMEM)

GELU_C0 = math.sqrt(2.0 / math.pi)
GELU_C1 = 0.044715


def _params(*sem):
    return pltpu.CompilerParams(dimension_semantics=sem, vmem_limit_bytes=VMEM_LIMIT)


def _sigmoid(x):
    return 0.5 * jnp.tanh(0.5 * x) + 0.5


def _silu_and_grad(x):
    s = _sigmoid(x)
    return x * s, s * (1.0 + x * (1.0 - s))


def _gelu_and_grad(z):
    z2 = z * z
    t = jnp.tanh(GELU_C0 * (z + GELU_C1 * z2 * z))
    half = 0.5 * (1.0 + t)
    return z * half, half + 0.5 * z * (1.0 - t * t) * (GELU_C0 * (1.0 + 3.0 * GELU_C1 * z2))


def _ln_fwd(v):
    mu = jnp.mean(v, axis=-1, keepdims=True)
    xc = v - mu
    rs = lax.rsqrt(jnp.mean(xc * xc, axis=-1, keepdims=True) + EPS)
    return xc * rs, rs


def _ln_bwd(d_xh, xh, rs):
    return rs * (d_xh - jnp.mean(d_xh, axis=-1, keepdims=True) - xh * jnp.mean(d_xh * xh, axis=-1, keepdims=True))


def _part8(a):
    return a.reshape(a.shape[0] // 8, 8, a.shape[1]).sum(axis=0)


def _dot(a, b):
    return jnp.dot(a, b, preferred_element_type=F32)


def _dot_nt(a, b):
    return lax.dot_general(a, b, (((1,), (1,)), ((), ())), preferred_element_type=F32)


def _dot_tn(a, b):
    return lax.dot_general(a, b, (((0,), (0,)), ((), ())), preferred_element_type=F32)


def _head_masks(dtype):
    lane = lax.broadcasted_iota(jnp.int32, (1, BR), 1) // HEAD_DIM
    return [(lane == h).astype(dtype) for h in range(N_HEADS)]


def _window(ref, col0, width):
    return ref.at[:, pl.ds(col0, width)]


def _copy_all(pairs, sems):
    cps = [pltpu.make_async_copy(s, d, sems.at[i]) for i, (s, d) in enumerate(pairs)]
    for cp in cps:
        cp.start()
    for cp in cps:
        cp.wait()


def _place():
    return lax.axis_index("x"), lax.axis_index("y"), lax.axis_index("c")


def _other_chips(x, y):
    return [(1 - x, y), (x, 1 - y), (1 - x, 1 - y)]


class _Rider:
    def __init__(self, srcs, out_shapes, per_array, make, through=False):
        self.srcs, self.out_shapes, self.per_array, self.make, self.through = list(srcs), list(out_shapes), per_array, make, through
        self.n = len(self.srcs)

    def scratch(self):
        return [pltpu.SemaphoreType.DMA((self.n, self.per_array)), pltpu.SemaphoreType.DMA((self.n, self.per_array))]


def _ride(rider, first, last, src_refs, dst_refs, sems):
    if rider is None:
        return (lambda: None), (lambda: None)
    cps = rider.make(src_refs, dst_refs, *sems)

    def guarded(cond, fn):
        if cond is True:
            fn()
        else:
            pl.when(cond)(fn)

    def start():
        guarded(first, lambda: [cp.start() for cp in cps] and None)

    def finish():
        guarded(last, lambda: [cp.wait() for cp in cps] and None)

    return start, finish


def _rider_parts(rider, n_in, n_out):
    if rider is None:
        return [], [], [], [], [], {}
    aliases = {n_in + a: n_out + a for a in range(rider.n)} if rider.through else {}
    return rider.srcs, [ANY] * rider.n, [ANY] * rider.n, rider.out_shapes, rider.scratch(), aliases


def _remote(src, dst, send_sems, recv_sems, a, k, to):
    return pltpu.make_async_remote_copy(src_ref=src, dst_ref=dst, send_sem=send_sems.at[a, k], recv_sem=recv_sems.at[a, k],
                                        device_id=to, device_id_type=MESH)


def _gather_send_rider(blocks):
    def make(srcs, dsts, ss, rs):
        x, y, cc = _place()
        return [_remote(srcs[a], dsts[a].at[2 * x + y, cc], ss, rs, a, j, (*chip, cc))
                for j, chip in enumerate(_other_chips(x, y)) for a in range(len(srcs))]
    shapes = [jax.ShapeDtypeStruct((N_CHIPS, 2) + b.shape, b.dtype) for b in blocks]
    return _Rider(blocks, shapes, 3, make)


def _gather_forward_rider(landed):
    def make(srcs, dsts, ss, rs):
        x, y, cc = _place()
        return [_remote(dsts[a].at[2 * px + py, cc], dsts[a].at[2 * px + py, cc], ss, rs, a, j, (x, y, 1 - cc))
                for j, (px, py) in enumerate(_other_chips(x, y)) for a in range(len(dsts))]
    shapes = [jax.ShapeDtypeStruct(b.shape, b.dtype) for b in landed]
    return _Rider(landed, shapes, 3, make, through=True)


def _pair_rider(arrays):
    def make(srcs, dsts, ss, rs):
        x, y, cc = _place()
        return [_remote(srcs[a], dsts[a], ss, rs, a, 0, (x, y, 1 - cc)) for a in range(len(srcs))]
    return _Rider(arrays, [jax.ShapeDtypeStruct(b.shape, b.dtype) for b in arrays], 1, make)


def _chip_rider(arrays):
    def make(srcs, dsts, ss, rs):
        x, y, cc = _place()
        return [_remote(srcs[a].at[2 * px + py], dsts[a].at[k], ss, rs, a, k, (px, py, cc))
                for k, (px, py) in enumerate(_other_chips(x, y)) for a in range(len(srcs))]
    return _Rider(arrays, [jax.ShapeDtypeStruct((3,) + b.shape[1:], b.dtype) for b in arrays], 3, make)


INPROJ_TN = 1536


def _inproj_fwd(x, g, w, w_bf):
    s = x.shape[0]
    tm, tn = min(256, s), PCOLS

    def body(x_ref, g_ref, w_ref, wbf_ref, proj_ref, pbf_ref, h_ref):
        @pl.when(pl.program_id(1) == 0)
        def _():
            xv = x_ref[...]
            r = lax.rsqrt(jnp.mean(xv * xv, axis=-1, keepdims=True) + EPS)
            h_ref[...] = ((xv * r) * g_ref[...]).astype(BF16)
            pbf_ref[...] = _dot(h_ref[...], wbf_ref[...])
        proj_ref[...] = _dot(h_ref[...], w_ref[...])

    return pl.pallas_call(
        body, name="inproj_fwd", grid=(s // tm, PCOLS // tn),
        in_specs=[pl.BlockSpec((tm, D_MODEL), lambda i, j: (i, 0)), pl.BlockSpec((1, D_MODEL), lambda i, j: (0, 0)),
                  pl.BlockSpec((D_MODEL, tn), lambda i, j: (0, j)), pl.BlockSpec((D_MODEL, CHUNK), lambda i, j: (0, 0))],
        out_specs=[pl.BlockSpec((tm, tn), lambda i, j: (i, j)), pl.BlockSpec((tm, CHUNK), lambda i, j: (i, 0)),
                   pl.BlockSpec((tm, D_MODEL), lambda i, j: (i, 0))],
        out_shape=[jax.ShapeDtypeStruct((s, PCOLS), F32), jax.ShapeDtypeStruct((s, CHUNK), F32), jax.ShapeDtypeStruct((s, D_MODEL), BF16)],
        compiler_params=_params("arbitrary", "arbitrary"))(x, g, w, w_bf)


def _rms_bwd(dh, x, g):
    r = lax.rsqrt(jnp.mean(x * x, axis=-1, keepdims=True) + EPS)
    xn = x * r
    gy = dh * g
    dx = r * (gy - xn * jnp.mean(xn * gy, axis=-1, keepdims=True))
    return dx, _part8(dh * xn)


def _inproj_bwd_x(dproj, w, x, dxn, g, dproj_bf, w_bf, rider=None):
    s = x.shape[0]
    tm, tk = min(256, s), PCOLS
    nk = PCOLS // tk
    ni = s // tm
    r_in, r_in_specs, r_out_specs, r_shapes, r_scratch, r_alias = _rider_parts(rider, 7, 2)
    nr = len(r_in)

    def body(*refs):
        dp_ref, w_ref, x_ref, dxn_ref, g_ref, dpbf_ref, wbf_ref = refs[:7]
        dx_ref, dg_ref = refs[7 + nr:9 + nr]
        acc_ref = refs[9 + 2 * nr]
        i, k = pl.program_id(0), pl.program_id(1)
        start, finish = _ride(rider, (i == 0) & (k == 0), (i == ni - 1) & (k == nk - 1),
                              refs[7:7 + nr], refs[9 + nr:9 + 2 * nr], refs[10 + 2 * nr:])
        start()

        @pl.when(k == 0)
        def _():
            acc_ref[...] = _dot_nt(dpbf_ref[...], wbf_ref[...])

        @pl.when((i == 0) & (k == 0))
        def _():
            dg_ref[...] = jnp.zeros_like(dg_ref)

        acc_ref[...] += _dot_nt(dp_ref[...], w_ref[...])

        @pl.when(k == nk - 1)
        def _():
            dx, dg8 = _rms_bwd(acc_ref[...], x_ref[...], g_ref[...])
            dx_ref[...] = dxn_ref[...] + dx
            dg_ref[...] += dg8

        finish()

    outs = pl.pallas_call(
        body, name="inproj_bwd_x", grid=(ni, nk),
        in_specs=[pl.BlockSpec((tm, tk), lambda i, k: (i, k)), pl.BlockSpec((D_MODEL, tk), lambda i, k: (0, k)),
                  pl.BlockSpec((tm, D_MODEL), lambda i, k: (i, 0)), pl.BlockSpec((tm, D_MODEL), lambda i, k: (i, 0)),
                  pl.BlockSpec((1, D_MODEL), lambda i, k: (0, 0)), pl.BlockSpec((tm, CHUNK), lambda i, k: (i, 0)),
                  pl.BlockSpec((D_MODEL, CHUNK), lambda i, k: (0, 0))] + r_in_specs,
        out_specs=[pl.BlockSpec((tm, D_MODEL), lambda i, k: (i, 0)), pl.BlockSpec((8, D_MODEL), lambda i, k: (0, 0))] + r_out_specs,
        out_shape=[jax.ShapeDtypeStruct((s, D_MODEL), F32), jax.ShapeDtypeStruct((8, D_MODEL), F32)] + r_shapes,
        scratch_shapes=[pltpu.VMEM((tm, D_MODEL), F32)] + r_scratch, input_output_aliases=r_alias,
        compiler_params=_params("arbitrary", "arbitrary"))(dproj, w, x, dxn, g, dproj_bf, w_bf, *r_in)
    return outs[0], outs[1], outs[2:]


def _inproj_bwd_w(h, dproj, dproj_bf):
    s = h.shape[0]
    tn, tk = INPROJ_TN, min(1024, s)
    nk = s // tk

    def body(h_ref, dp_ref, dpbf_ref, dw_ref, dwbf_ref, acc_ref, accbf_ref):
        j, k = pl.program_id(0), pl.program_id(1)

        @pl.when(k == 0)
        def _():
            acc_ref[...] = jnp.zeros_like(acc_ref)

        acc_ref[...] += _dot_tn(h_ref[...], dp_ref[...])

        @pl.when(k == nk - 1)
        def _():
            dw_ref[...] = acc_ref[...].astype(BF16)

        @pl.when((j == 0) & (k == 0))
        def _():
            accbf_ref[...] = jnp.zeros_like(accbf_ref)

        @pl.when(j == 0)
        def _():
            accbf_ref[...] += _dot_tn(h_ref[...], dpbf_ref[...])

        @pl.when((j == 0) & (k == nk - 1))
        def _():
            dwbf_ref[...] = accbf_ref[...].astype(BF16)

    return pl.pallas_call(
        body, name="inproj_bwd_w", grid=(PCOLS // tn, nk),
        in_specs=[pl.BlockSpec((tk, D_MODEL), lambda j, k: (k, 0)), pl.BlockSpec((tk, tn), lambda j, k: (k, j)),
                  pl.BlockSpec((tk, CHUNK), lambda j, k: (k, 0))],
        out_specs=[pl.BlockSpec((D_MODEL, tn), lambda j, k: (0, j)), pl.BlockSpec((D_MODEL, CHUNK), lambda j, k: (0, 0))],
        out_shape=[jax.ShapeDtypeStruct((D_MODEL, PCOLS), BF16), jax.ShapeDtypeStruct((D_MODEL, CHUNK), BF16)],
        scratch_shapes=[pltpu.VMEM((D_MODEL, tn), F32), pltpu.VMEM((D_MODEL, CHUNK), F32)],
        compiler_params=_params("arbitrary", "arbitrary"))(h, dproj, dproj_bf)


def _mix_a_chunk(uvp, agp, wm_ref, bias, lg, lb):
    u, du = _gelu_and_grad(uvp[:, :BR])
    v, dv = _gelu_and_grad(uvp[:, BR:])
    xh, rs = _ln_fwd(v)
    vnb = (xh * lg + lb).astype(BF16)
    masks = _head_masks(BF16)
    mixed = bias
    for h in range(N_HEADS):
        mixed = mixed + _dot(wm_ref[h], vnb * masks[h])
    sg, dsg = _silu_and_grad(agp)
    return u, du, dv, xh, rs, vnb, masks, mixed, sg, dsg


def _store_masked_sgu(sw_ref, wm_ref):
    row = lax.broadcasted_iota(jnp.int32, (CHUNK, CHUNK), 0)
    col = lax.broadcasted_iota(jnp.int32, (CHUNK, CHUNK), 1)
    for h in range(N_HEADS):
        wm_ref[h] = jnp.where(row >= col, sw_ref[h], 0.0).astype(BF16)


def _mix_a_fwd(proj, sgu_w, bias, lg, lb):
    s = proj.shape[0]

    def body(proj_ref, sw_ref, bias_ref, lg_ref, lb_ref, y_ref, uv_buf, ag_buf, wm_ref, sems):
        _copy_all([(_window(proj_ref, AUV0, 2 * BR), uv_buf), (_window(proj_ref, AG0, BR), ag_buf)], sems)
        _store_masked_sgu(sw_ref, wm_ref)

        def chunk(i, c):
            rows = pl.ds(pl.multiple_of(i * CHUNK, CHUNK), CHUNK)
            u, _, _, _, _, _, _, mixed, sg, _ = _mix_a_chunk(uv_buf[rows, :], ag_buf[rows, :], wm_ref, bias_ref[...],
                                                            lg_ref[...], lb_ref[...])
            y_ref[rows, :] = (u * mixed * sg).astype(BF16)
            return c

        lax.fori_loop(0, s // CHUNK, chunk, 0)

    return pl.pallas_call(
        body, name="mix_a_fwd", in_specs=[ANY, VMEM, VMEM, VMEM, VMEM], out_specs=VMEM,
        out_shape=jax.ShapeDtypeStruct((s, BR), BF16),
        scratch_shapes=[pltpu.VMEM((s, 2 * BR), F32), pltpu.VMEM((s, BR), F32), pltpu.VMEM((N_HEADS, CHUNK, CHUNK), BF16),
                        pltpu.SemaphoreType.DMA((2,))],
        compiler_params=_params())(proj, sgu_w, bias, lg, lb)


def _mix_a_bwd(proj, dproj, dy, sgu_w, bias, lg, lb):
    s = proj.shape[0]

    def body(proj_ref, dproj_in, dy_ref, sw_ref, bias_ref, lg_ref, lb_ref,
             dproj_ref, dsw_ref, dbias_ref, dlg_ref, dlb_ref,
             uv_buf, ag_buf, duv_buf, dag_buf, wm_ref, acc_lg, acc_lb, sems):
        del dproj_in
        _copy_all([(_window(proj_ref, AUV0, 2 * BR), uv_buf), (_window(proj_ref, AG0, BR), ag_buf)], sems)
        _store_masked_sgu(sw_ref, wm_ref)
        dsw_ref[...] = jnp.zeros_like(dsw_ref)
        dbias_ref[...] = jnp.zeros_like(dbias_ref)
        acc_lg[...] = jnp.zeros_like(acc_lg)
        acc_lb[...] = jnp.zeros_like(acc_lb)

        def chunk(i, c):
            rows = pl.ds(pl.multiple_of(i * CHUNK, CHUNK), CHUNK)
            lg_v = lg_ref[...]
            u, du, dv, xh, rs, vnb, masks, mixed, sg, dsg = _mix_a_chunk(
                uv_buf[rows, :], ag_buf[rows, :], wm_ref, bias_ref[...], lg_v, lb_ref[...])
            dyv = dy_ref[rows, :]
            t1 = dyv * sg
            d_u = t1 * mixed
            d_mixed = t1 * u
            d_ag = dyv * u * mixed * dsg
            dbias_ref[...] += d_mixed
            dmb = d_mixed.astype(BF16)
            d_vn = jnp.zeros((CHUNK, BR), F32)
            for h in range(N_HEADS):
                dm_h = dmb * masks[h]
                dsw_ref[h] += _dot_nt(dm_h, vnb)
                d_vn = d_vn + _dot_tn(wm_ref[h], dm_h)
            acc_lg[...] += _part8(d_vn * xh)
            acc_lb[...] += _part8(d_vn)
            d_v = _ln_bwd(d_vn * lg_v, xh, rs)
            duv_buf[rows, :] = jnp.concatenate([d_u * du, d_v * dv], axis=1).astype(BF16)
            dag_buf[rows, :] = d_ag.astype(BF16)
            return c

        lax.fori_loop(0, s // CHUNK, chunk, 0)
        row = lax.broadcasted_iota(jnp.int32, (CHUNK, CHUNK), 0)
        col = lax.broadcasted_iota(jnp.int32, (CHUNK, CHUNK), 1)
        for h in range(N_HEADS):
            dsw_ref[h] = jnp.where(row >= col, dsw_ref[h], 0.0)
        dlg_ref[...] = jnp.sum(acc_lg[...], axis=0, keepdims=True)
        dlb_ref[...] = jnp.sum(acc_lb[...], axis=0, keepdims=True)
        _copy_all([(duv_buf, _window(dproj_ref, AUV0, 2 * BR)), (dag_buf, _window(dproj_ref, AG0, BR))], sems)

    return pl.pallas_call(
        body, name="mix_a_bwd", in_specs=[ANY, ANY, VMEM, VMEM, VMEM, VMEM, VMEM],
        out_specs=[ANY, VMEM, VMEM, VMEM, VMEM],
        out_shape=[jax.ShapeDtypeStruct(dproj.shape, dproj.dtype), jax.ShapeDtypeStruct((N_HEADS, CHUNK, CHUNK), F32),
                   jax.ShapeDtypeStruct((CHUNK, BR), F32), jax.ShapeDtypeStruct((1, BR), F32), jax.ShapeDtypeStruct((1, BR), F32)],
        scratch_shapes=[pltpu.VMEM((s, 2 * BR), F32), pltpu.VMEM((s, BR), F32), pltpu.VMEM((s, 2 * BR), BF16),
                        pltpu.VMEM((s, BR), BF16), pltpu.VMEM((N_HEADS, CHUNK, CHUNK), BF16),
                        pltpu.VMEM((8, BR), F32), pltpu.VMEM((8, BR), F32), pltpu.SemaphoreType.DMA((2,))],
        input_output_aliases={1: 0}, compiler_params=_params())(proj, dproj, dy, sgu_w, bias, lg, lb)


def _tri_ones(n, upper):
    row = lax.broadcasted_iota(jnp.int32, (n, n), 0)
    col = lax.broadcasted_iota(jnp.int32, (n, n), 1)
    return ((row <= col) if upper else (row >= col)).astype(BF16)


def _split3(c):
    hi = c.astype(BF16)
    r1 = c - hi.astype(F32)
    mid = r1.astype(BF16)
    lo = (r1 - mid.astype(F32)).astype(BF16)
    return [hi, mid, lo]


def _tri_sum(tri, a):
    parts = _split3(a)
    return _dot(tri, parts[0]) + _dot(tri, parts[1]) + _dot(tri, parts[2])


EXT = 2 * HEAD_DIM
LANE_CQ = HEAD_DIM
LANE_CK = HEAD_DIM + 3


def _to_head(h):
    r = lax.broadcasted_iota(jnp.int32, (BR, EXT), 0)
    c = lax.broadcasted_iota(jnp.int32, (BR, EXT), 1)
    return ((r == c + h * HEAD_DIM) & (c < HEAD_DIM)).astype(BF16)


def _from_head(h):
    r = lax.broadcasted_iota(jnp.int32, (EXT, BR), 0)
    c = lax.broadcasted_iota(jnp.int32, (EXT, BR), 1)
    return ((c == r + h * HEAD_DIM) & (r < HEAD_DIM)).astype(BF16)


def _attn_prep_fwd(proj, proj_bf, f_bias):
    s = proj.shape[0]
    scale = 1.0 / math.sqrt(HEAD_DIM)

    def body(proj_ref, z_buf, fb_ref, qe_ref, ke_ref, ve_ref, ket_ref, vet_ref, qkv_buf, sems):
        _copy_all([(_window(proj_ref, QKV0, 3 * BR), qkv_buf)], sems)
        tri = _tri_ones(CHUNK, upper=False)
        lane = lax.broadcasted_iota(jnp.int32, (CHUNK, EXT), 1)
        ones_q = ((lane >= LANE_CK) & (lane < LANE_CK + 3)).astype(F32)
        ones_k = ((lane >= LANE_CQ) & (lane < LANE_CQ + 3)).astype(F32)

        def chunk(i, carry):
            rows = pl.ds(pl.multiple_of(i * CHUNK, CHUNK), CHUNK)
            cum = _tri_sum(tri, jax.nn.log_sigmoid(z_buf[rows, :] + fb_ref[...])) + carry
            qb = (qkv_buf[rows, 0:BR] * scale).astype(BF16)
            kb = qkv_buf[rows, BR:2 * BR].astype(BF16)
            vb = qkv_buf[rows, 2 * BR:].astype(BF16)
            parts = [p.astype(F32) for p in _split3(cum)]
            for h in range(N_HEADS):
                sel = _to_head(h)
                dec_q, dec_k = ones_q, ones_k
                for t, part in enumerate(parts):
                    pf = part[:, h:h + 1]
                    dec_q = dec_q + jnp.where(lane == LANE_CQ + t, pf, 0.0)
                    dec_k = dec_k - jnp.where(lane == LANE_CK + t, pf, 0.0)
                qe_ref[h, rows, :] = (_dot(qb, sel) + dec_q).astype(BF16)
                kh = _dot(kb, sel) + dec_k
                vh = _dot(vb, sel)
                ke_ref[h, rows, :] = kh.astype(BF16)
                ve_ref[h, rows, :] = vh.astype(BF16)
                ket_ref[h, :, rows] = kh.T.astype(BF16)
                vet_ref[h, :, rows] = vh.T.astype(BF16)
            return cum[CHUNK - 1:CHUNK, :]

        lax.fori_loop(0, s // CHUNK, chunk, jnp.zeros((1, CHUNK), F32))

    shape = jax.ShapeDtypeStruct((N_HEADS, s, EXT), BF16)
    shape_t = jax.ShapeDtypeStruct((N_HEADS, EXT, s), BF16)
    return pl.pallas_call(
        body, name="attn_prep_fwd", in_specs=[ANY, VMEM, VMEM], out_specs=[VMEM] * 5, out_shape=[shape, shape, shape, shape_t, shape_t],
        scratch_shapes=[pltpu.VMEM((s, 3 * BR), F32), pltpu.SemaphoreType.DMA((1,))],
        compiler_params=_params())(proj, proj_bf, f_bias)


def _attn_prep_bwd(proj_bf, dproj, dqe, dke, dve, f_bias):
    s = proj_bf.shape[0]
    scale = 1.0 / math.sqrt(HEAD_DIM)

    def body(z_buf, dproj_in, dq_ref, dk_ref, dv_ref, fb_ref, dproj_ref, dz_buf, dfb_ref, dqkv_buf, sems):
        del dproj_in
        tri = _tri_ones(CHUNK, upper=True)
        lane = lax.broadcasted_iota(jnp.int32, (CHUNK, CHUNK), 1)
        n = s // CHUNK

        def chunk(t, carry):
            suffix, acc = carry
            i = n - 1 - t
            rows = pl.ds(pl.multiple_of(i * CHUNK, CHUNK), CHUNK)
            dq = jnp.zeros((CHUNK, BR), F32)
            dk = jnp.zeros((CHUNK, BR), F32)
            dv = jnp.zeros((CHUNK, BR), F32)
            dcum = jnp.zeros((CHUNK, CHUNK), F32)
            for h in range(N_HEADS):
                back = _from_head(h)
                dqh = dq_ref[h, :, rows].T
                dkh = dk_ref[h, rows, :]
                dq = dq + _dot((dqh * scale).astype(BF16), back)
                dk = dk + _dot(dkh.astype(BF16), back)
                dv = dv + _dot(dv_ref[h, rows, :].astype(BF16), back)
                dcum = dcum + jnp.where(lane == h, dqh[:, LANE_CQ:LANE_CQ + 1] - dkh[:, LANE_CK:LANE_CK + 1], 0.0)
            dqkv_buf[rows, 0:BR] = dq.astype(BF16)
            dqkv_buf[rows, BR:2 * BR] = dk.astype(BF16)
            dqkv_buf[rows, 2 * BR:] = dv.astype(BF16)
            dlf = _tri_sum(tri, dcum) + suffix
            dz = dlf * _sigmoid(-(z_buf[rows, :] + fb_ref[...]))
            dz_buf[rows, :] = dz.astype(BF16)
            return dlf[0:1, :], acc + _part8(dz)

        _, acc = lax.fori_loop(0, n, chunk, (jnp.zeros((1, CHUNK), F32), jnp.zeros((8, CHUNK), F32)))
        dfb_ref[...] = jnp.sum(acc, axis=0, keepdims=True)
        _copy_all([(dqkv_buf, _window(dproj_ref, QKV0, 3 * BR))], sems)

    return pl.pallas_call(
        body, name="attn_prep_bwd", in_specs=[VMEM, ANY, VMEM, VMEM, VMEM, VMEM], out_specs=[ANY, VMEM, VMEM],
        out_shape=[jax.ShapeDtypeStruct(dproj.shape, dproj.dtype), jax.ShapeDtypeStruct((s, CHUNK), BF16),
                   jax.ShapeDtypeStruct((1, CHUNK), F32)],
        scratch_shapes=[pltpu.VMEM((s, 3 * BR), BF16), pltpu.SemaphoreType.DMA((1,))],
        input_output_aliases={1: 0}, compiler_params=_params())(proj_bf, dproj, dqe, dke, dve, f_bias)


ATT_TQ = 256
ATT_FWD_GROUP = 8
ATT_BWD_GROUP = 8
NEG_BIG = -1e30


def _causal_t(q0, k0, tk, tq):
    kpos = k0 + lax.broadcasted_iota(jnp.int32, (tk, tq), 0)
    qpos = q0 + lax.broadcasted_iota(jnp.int32, (tk, tq), 1)
    return kpos <= qpos


STAT_ROWS = 8


def _attn_fwd(qe, ke, vet, rider=None):
    nh, s, w = qe.shape
    tq = min(ATT_TQ, s)
    tk = min(ATT_FWD_GROUP * tq, s)
    per = tk // tq
    nq = s // tq
    r_in, r_in_specs, r_out_specs, r_shapes, r_scratch, r_alias = _rider_parts(rider, 3, 2)
    nr = len(r_in)

    def body(*refs):
        q_ref, k_ref, vt_ref = refs[:3]
        ot_ref, lse_ref = refs[3 + nr:5 + nr]
        h, i = pl.program_id(0), pl.program_id(1)
        start, finish = _ride(rider, (h == 0) & (i == 0), (h == nh - 1) & (i == nq - 1),
                              refs[3:3 + nr], refs[5 + nr:5 + 2 * nr], refs[5 + 2 * nr:])
        start()
        q = q_ref[0]

        def step(k0, carry, width, masked):
            m, l, acc = carry
            ks = pl.ds(pl.multiple_of(k0, tq), width)
            st = _dot_nt(k_ref[0, ks, :], q)
            if masked:
                st = jnp.where(_causal_t(i * tq, k0, width, tq), st, NEG_BIG)
            m_new = jnp.maximum(m, jnp.max(st, axis=0, keepdims=True))
            alpha = jnp.exp(m - m_new)
            pt = jnp.exp(st - m_new)
            l = alpha * l + jnp.sum(pt, axis=0, keepdims=True)
            acc = alpha * acc + _dot(vt_ref[0, :, ks], pt.astype(BF16))
            return m_new, l, acc

        full = i // per
        init = (jnp.full((1, tq), NEG_BIG, F32), jnp.zeros((1, tq), F32), jnp.zeros((w, tq), F32))
        carry = lax.fori_loop(0, full, lambda j, c: step(j * tk, c, tk, False), init)
        m, l, acc = lax.switch(i % per, [functools.partial(step, width=(r + 1) * tq, masked=True) for r in range(per)],
                               full * tk, carry)
        ot_ref[0] = acc / l
        lse_ref[0] = jnp.broadcast_to(m + jnp.log(l), (STAT_ROWS, tq))
        finish()

    outs = pl.pallas_call(
        body, name="attn_fwd", grid=(nh, nq),
        in_specs=[pl.BlockSpec((1, tq, w), lambda h, i: (h, i, 0)), pl.BlockSpec((1, s, w), lambda h, i: (h, 0, 0)),
                  pl.BlockSpec((1, w, s), lambda h, i: (h, 0, 0))] + r_in_specs,
        out_specs=[pl.BlockSpec((1, w, tq), lambda h, i: (h, 0, i)), pl.BlockSpec((1, STAT_ROWS, tq), lambda h, i: (h, 0, i))] + r_out_specs,
        out_shape=[jax.ShapeDtypeStruct((nh, w, s), F32), jax.ShapeDtypeStruct((nh, STAT_ROWS, s), F32)] + r_shapes,
        scratch_shapes=r_scratch, input_output_aliases=r_alias,
        compiler_params=_params("arbitrary", "arbitrary"))(qe, ke, vet, *r_in)
    return outs[0], outs[1], outs[2:]


def _attn_bwd(qe, ke, ket, ve, doe, lse, dd, rider=None):
    nh, s, w = qe.shape
    tq = min(ATT_TQ, s)
    tk = min(ATT_BWD_GROUP * tq, s)
    per = tk // tq
    nq = s // tq
    r_in, r_in_specs, r_out_specs, r_shapes, r_scratch, r_alias = _rider_parts(rider, 7, 3)
    nr = len(r_in)

    def body(*refs):
        q_ref, do_ref, lse_ref, dd_ref, k_ref, kt_ref, v_ref = refs[:7]
        dqt_ref, dk_ref, dv_ref = refs[7 + nr:10 + nr]
        h, i = pl.program_id(0), pl.program_id(1)
        start, finish = _ride(rider, (h == 0) & (i == 0), (h == nh - 1) & (i == nq - 1),
                              refs[7:7 + nr], refs[10 + nr:10 + 2 * nr], refs[10 + 2 * nr:])
        start()

        @pl.when(i == 0)
        def _():
            dk_ref[...] = jnp.zeros_like(dk_ref)
            dv_ref[...] = jnp.zeros_like(dv_ref)

        q = q_ref[0]
        do = do_ref[0]
        lse_row = lse_ref[0, 0:1, :]
        dd_row = dd_ref[0, 0:1, :]

        def step(k0, dqt, width, masked):
            ks = pl.ds(pl.multiple_of(k0, tq), width)
            st = _dot_nt(k_ref[0, ks, :], q)
            pt = jnp.exp(st - lse_row)
            if masked:
                pt = jnp.where(_causal_t(i * tq, k0, width, tq), pt, 0.0)
            dpt = _dot_nt(v_ref[0, ks, :], do)
            dst = (pt * (dpt - dd_row)).astype(BF16)
            dv_ref[0, ks, :] += _dot(pt.astype(BF16), do)
            dk_ref[0, ks, :] += _dot(dst, q)
            return dqt + _dot(kt_ref[0, :, ks], dst)

        full = i // per
        dqt = lax.fori_loop(0, full, lambda j, c: step(j * tk, c, tk, False), jnp.zeros((w, tq), F32))
        dqt_ref[0] = lax.switch(i % per, [functools.partial(step, width=(r + 1) * tq, masked=True) for r in range(per)],
                                full * tk, dqt)
        finish()

    qblk = pl.BlockSpec((1, tq, w), lambda h, i: (h, i, 0))
    stat = pl.BlockSpec((1, STAT_ROWS, tq), lambda h, i: (h, 0, i))
    whole = pl.BlockSpec((1, s, w), lambda h, i: (h, 0, 0))
    whole_t = pl.BlockSpec((1, w, s), lambda h, i: (h, 0, 0))
    outs = pl.pallas_call(
        body, name="attn_bwd", grid=(nh, nq),
        in_specs=[qblk, qblk, stat, stat, whole, whole_t, whole] + r_in_specs,
        out_specs=[pl.BlockSpec((1, w, tq), lambda h, i: (h, 0, i)), whole, whole] + r_out_specs,
        out_shape=[jax.ShapeDtypeStruct((nh, w, s), F32), jax.ShapeDtypeStruct((nh, s, w), F32),
                   jax.ShapeDtypeStruct((nh, s, w), F32)] + r_shapes,
        scratch_shapes=r_scratch, input_output_aliases=r_alias,
        compiler_params=_params("arbitrary", "arbitrary"))(qe, doe, lse, dd, ke, ket, ve, *r_in)
    return outs[0], outs[1], outs[2], outs[3:]


def _bgate_fwd(proj, ot, rider=None):
    s = proj.shape[0]
    r_in, r_in_specs, r_out_specs, r_shapes, r_scratch, r_alias = _rider_parts(rider, 2, 2)
    nr = len(r_in)

    def body(*refs):
        proj_ref, ot_ref = refs[:2]
        y_ref, o_ref = refs[2 + nr:4 + nr]
        g_buf, sems = refs[4 + 2 * nr:6 + 2 * nr]
        start, finish = _ride(rider, True, True, refs[2:2 + nr], refs[4 + nr:4 + 2 * nr], refs[6 + 2 * nr:])
        start()
        _copy_all([(_window(proj_ref, BG0, BR), g_buf)], sems)

        def chunk(i, c):
            rows = pl.ds(pl.multiple_of(i * CHUNK, CHUNK), CHUNK)
            o = jnp.zeros((CHUNK, BR), F32)
            for h in range(N_HEADS):
                back = _from_head(h)
                for part in _split3(ot_ref[h, :, rows].T):
                    o = o + _dot(part, back)
            sg, _ = _silu_and_grad(g_buf[rows, :])
            o_ref[rows, :] = o
            y_ref[rows, :] = (o * sg).astype(BF16)
            return c

        lax.fori_loop(0, s // CHUNK, chunk, 0)
        finish()

    outs = pl.pallas_call(
        body, name="bgate_fwd", in_specs=[ANY, VMEM] + r_in_specs, out_specs=[VMEM, VMEM] + r_out_specs,
        out_shape=[jax.ShapeDtypeStruct((s, BR), BF16), jax.ShapeDtypeStruct((s, BR), F32)] + r_shapes,
        scratch_shapes=[pltpu.VMEM((s, BR), F32), pltpu.SemaphoreType.DMA((1,))] + r_scratch,
        input_output_aliases=r_alias, compiler_params=_params())(proj, ot, *r_in)
    return outs[0], outs[1], outs[2:]


def _bgate_bwd(proj, dproj, o, dy):
    s = proj.shape[0]

    def body(proj_ref, dproj_in, o_ref, dy_ref, dproj_ref, do_ref, dd_ref, g_buf, dg_buf, sems):
        del dproj_in
        _copy_all([(_window(proj_ref, BG0, BR), g_buf)], sems)
        lane = lax.broadcasted_iota(jnp.int32, (BR, CHUNK), 0) // HEAD_DIM
        col = lax.broadcasted_iota(jnp.int32, (BR, CHUNK), 1)
        sel = (lane == col).astype(BF16)

        def chunk(i, c):
            rows = pl.ds(pl.multiple_of(i * CHUNK, CHUNK), CHUNK)
            sg, dsg = _silu_and_grad(g_buf[rows, :])
            dyv = dy_ref[rows, :]
            ov = o_ref[rows, :]
            do = dyv * sg
            dg_buf[rows, :] = (dyv * ov * dsg).astype(BF16)
            prod = _split3(do * ov)
            ddt = (_dot(prod[0], sel) + _dot(prod[1], sel) + _dot(prod[2], sel)).T
            dob = do.astype(BF16)
            for h in range(N_HEADS):
                do_ref[h, rows, :] = _dot(dob, _to_head(h)).astype(BF16)
                dd_ref[h, :, rows] = jnp.broadcast_to(ddt[h:h + 1, :], (STAT_ROWS, CHUNK))
            return c

        lax.fori_loop(0, s // CHUNK, chunk, 0)
        _copy_all([(dg_buf, _window(dproj_ref, BG0, BR))], sems)

    return pl.pallas_call(
        body, name="bgate_bwd", in_specs=[ANY, ANY, VMEM, VMEM], out_specs=[ANY, VMEM, VMEM],
        out_shape=[jax.ShapeDtypeStruct(dproj.shape, dproj.dtype), jax.ShapeDtypeStruct((N_HEADS, s, EXT), BF16),
                   jax.ShapeDtypeStruct((N_HEADS, STAT_ROWS, s), F32)],
        scratch_shapes=[pltpu.VMEM((s, BR), F32), pltpu.VMEM((s, BR), BF16), pltpu.SemaphoreType.DMA((1,))],
        input_output_aliases={1: 0}, compiler_params=_params())(proj, dproj, o, dy)


C_PAD = 8


def _mix_c_fwd(proj, w):
    s = proj.shape[0]

    def body(proj_ref, w_ref, y_ref, cin_buf, g_buf, z_buf, sems):
        _copy_all([(_window(proj_ref, CIN0, 3 * BR), cin_buf), (_window(proj_ref, CG0, BR), g_buf)], sems)
        z_buf[pl.ds(0, C_PAD), :] = jnp.zeros((C_PAD, BR), F32)

        def fill(i, c):
            rows = pl.ds(pl.multiple_of(i * CHUNK, CHUNK), CHUNK)
            z_buf[pl.ds(pl.multiple_of(i * CHUNK + C_PAD, 8), CHUNK), :] = cin_buf[rows, BR:2 * BR] * cin_buf[rows, 2 * BR:]
            return c

        lax.fori_loop(0, s // CHUNK, fill, 0)

        def chunk(i, c):
            r0 = pl.multiple_of(i * CHUNK, CHUNK)
            rows = pl.ds(r0, CHUNK)
            ze = z_buf[pl.ds(r0, CHUNK + C_PAD), :]
            conv = jnp.zeros((CHUNK, BR), F32)
            for k in range(K_SHORT):
                off = C_PAD - (K_SHORT - 1) + k
                conv = conv + w_ref[k:k + 1, :] * ze[off:off + CHUNK]
            sg, _ = _silu_and_grad(g_buf[rows, :])
            y_ref[rows, :] = (cin_buf[rows, 0:BR] * conv * sg).astype(BF16)
            return c

        lax.fori_loop(0, s // CHUNK, chunk, 0)

    return pl.pallas_call(
        body, name="mix_c_fwd", in_specs=[ANY, VMEM], out_specs=VMEM, out_shape=jax.ShapeDtypeStruct((s, BR), BF16),
        scratch_shapes=[pltpu.VMEM((s, 3 * BR), F32), pltpu.VMEM((s, BR), F32), pltpu.VMEM((s + C_PAD, BR), F32),
                        pltpu.SemaphoreType.DMA((2,))],
        compiler_params=_params())(proj, w)


def _mix_c_bwd(proj, dproj, dy, w):
    s = proj.shape[0]

    def body(proj_ref, dproj_in, dy_ref, w_ref, dproj_ref, dw_ref, cin_buf, g_buf, z_buf, dc_buf, dcin_buf, dg_buf, acc, sems):
        del dproj_in
        _copy_all([(_window(proj_ref, CIN0, 3 * BR), cin_buf), (_window(proj_ref, CG0, BR), g_buf)], sems)
        z_buf[pl.ds(0, C_PAD), :] = jnp.zeros((C_PAD, BR), F32)
        dc_buf[pl.ds(s, C_PAD), :] = jnp.zeros((C_PAD, BR), F32)
        acc[...] = jnp.zeros_like(acc)

        def fill(i, c):
            rows = pl.ds(pl.multiple_of(i * CHUNK, CHUNK), CHUNK)
            z_buf[pl.ds(pl.multiple_of(i * CHUNK + C_PAD, 8), CHUNK), :] = cin_buf[rows, BR:2 * BR] * cin_buf[rows, 2 * BR:]
            return c

        lax.fori_loop(0, s // CHUNK, fill, 0)

        def chunk(i, c):
            r0 = pl.multiple_of(i * CHUNK, CHUNK)
            rows = pl.ds(r0, CHUNK)
            ze = z_buf[pl.ds(r0, CHUNK + C_PAD), :]
            taps = [ze[C_PAD - (K_SHORT - 1) + k:C_PAD - (K_SHORT - 1) + k + CHUNK] for k in range(K_SHORT)]
            conv = jnp.zeros((CHUNK, BR), F32)
            for k in range(K_SHORT):
                conv = conv + w_ref[k:k + 1, :] * taps[k]
            sg, dsg = _silu_and_grad(g_buf[rows, :])
            bg = cin_buf[rows, 0:BR]
            dyv = dy_ref[rows, :]
            dconv = dyv * bg * sg
            dc_buf[rows, :] = dconv
            dcin_buf[rows, 0:BR] = (dyv * conv * sg).astype(BF16)
            dg_buf[rows, :] = (dyv * bg * conv * dsg).astype(BF16)
            for k in range(K_SHORT):
                acc[k] += _part8(dconv * taps[k])
            return c

        lax.fori_loop(0, s // CHUNK, chunk, 0)

        def chunk2(i, c):
            r0 = pl.multiple_of(i * CHUNK, CHUNK)
            rows = pl.ds(r0, CHUNK)
            de = dc_buf[pl.ds(r0, CHUNK + C_PAD), :]
            dz = jnp.zeros((CHUNK, BR), F32)
            for k in range(K_SHORT):
                off = K_SHORT - 1 - k
                dz = dz + w_ref[k:k + 1, :] * de[off:off + CHUNK]
            dcin_buf[rows, BR:2 * BR] = (dz * cin_buf[rows, 2 * BR:]).astype(BF16)
            dcin_buf[rows, 2 * BR:] = (dz * cin_buf[rows, BR:2 * BR]).astype(BF16)
            return c

        lax.fori_loop(0, s // CHUNK, chunk2, 0)
        dw_ref[...] = jnp.zeros_like(dw_ref)
        for k in range(K_SHORT):
            dw_ref[k:k + 1, :] = jnp.sum(acc[k], axis=0, keepdims=True)
        _copy_all([(dcin_buf, _window(dproj_ref, CIN0, 3 * BR)), (dg_buf, _window(dproj_ref, CG0, BR))], sems)

    return pl.pallas_call(
        body, name="mix_c_bwd", in_specs=[ANY, ANY, VMEM, VMEM], out_specs=[ANY, VMEM],
        out_shape=[jax.ShapeDtypeStruct(dproj.shape, dproj.dtype), jax.ShapeDtypeStruct((8, BR), F32)],
        scratch_shapes=[pltpu.VMEM((s, 3 * BR), F32), pltpu.VMEM((s, BR), F32), pltpu.VMEM((s + C_PAD, BR), F32),
                        pltpu.VMEM((s + C_PAD, BR), F32), pltpu.VMEM((s, 3 * BR), BF16), pltpu.VMEM((s, BR), BF16),
                        pltpu.VMEM((K_SHORT, 8, BR), F32), pltpu.SemaphoreType.DMA((2,))],
        input_output_aliases={1: 0}, compiler_params=_params())(proj, dproj, dy, w)


D_PAD = 32


def _tap_windows(win, offsets, n):
    rows = win.shape[0]
    rolled, out = {}, []
    for off in offsets:
        r = off % 8
        if r not in rolled:
            rolled[r] = win if r == 0 else pltpu.roll(win, rows - r, 0)
        out.append(rolled[r][off - r:off - r + n])
    return out


def _mix_d_common(g_ref, w_ref, b_ref, lg_ref, lb_ref, hh_buf, r0):
    he = hh_buf[pl.ds(r0, CHUNK + D_PAD), :]
    taps = _tap_windows(he, [D_PAD - (K_CONF - 1) + k for k in range(K_CONF)], CHUNK)
    conv = jnp.zeros((CHUNK, BR), F32) + b_ref[...]
    for k in range(K_CONF):
        conv = conv + w_ref[k:k + 1, :] * taps[k]
    xh, rs = _ln_fwd(conv)
    sw, dsw = _silu_and_grad(xh * lg_ref[...] + lb_ref[...])
    sg, dsg = _silu_and_grad(g_ref[pl.ds(r0, CHUNK), :])
    return taps, xh, rs, sw, dsw, sg, dsg


def _mix_d_fill(cin_buf, hh_buf, s):
    hh_buf[pl.ds(0, D_PAD), :] = jnp.zeros((D_PAD, BR), F32)

    def fill(i, c):
        rows = pl.ds(pl.multiple_of(i * CHUNK, CHUNK), CHUNK)
        hh_buf[pl.ds(pl.multiple_of(i * CHUNK + D_PAD, 8), CHUNK), :] = cin_buf[rows, 0:BR] * _sigmoid(cin_buf[rows, BR:])
        return c

    lax.fori_loop(0, s // CHUNK, fill, 0)


def _mix_d_fwd(proj, w, b, lg, lb):
    s = proj.shape[0]

    def body(proj_ref, w_ref, b_ref, lg_ref, lb_ref, y_ref, cin_buf, g_buf, hh_buf, sems):
        _copy_all([(_window(proj_ref, DGLU0, 2 * BR), cin_buf), (_window(proj_ref, DG0, BR), g_buf)], sems)
        _mix_d_fill(cin_buf, hh_buf, s)

        def chunk(i, c):
            r0 = pl.multiple_of(i * CHUNK, CHUNK)
            _, _, _, sw, _, sg, _ = _mix_d_common(g_buf, w_ref, b_ref, lg_ref, lb_ref, hh_buf, r0)
            y_ref[pl.ds(r0, CHUNK), :] = (sw * sg).astype(BF16)
            return c

        lax.fori_loop(0, s // CHUNK, chunk, 0)

    return pl.pallas_call(
        body, name="mix_d_fwd", in_specs=[ANY, VMEM, VMEM, VMEM, VMEM], out_specs=VMEM,
        out_shape=jax.ShapeDtypeStruct((s, BR), BF16),
        scratch_shapes=[pltpu.VMEM((s, 2 * BR), F32), pltpu.VMEM((s, BR), F32), pltpu.VMEM((s + D_PAD, BR), F32),
                        pltpu.SemaphoreType.DMA((2,))],
        compiler_params=_params())(proj, w, b, lg, lb)


def _mix_d_bwd(proj, dproj, dy, w, b, lg, lb):
    s = proj.shape[0]

    def body(proj_ref, dproj_in, dy_ref, w_ref, b_ref, lg_ref, lb_ref, dproj_ref, dw_ref, db_ref, dlg_ref, dlb_ref,
             cin_buf, g_buf, hh_buf, dc_buf, dcin_buf, dg_buf, acc_w, acc_s, sems):
        del dproj_in
        _copy_all([(_window(proj_ref, DGLU0, 2 * BR), cin_buf), (_window(proj_ref, DG0, BR), g_buf)], sems)
        _mix_d_fill(cin_buf, hh_buf, s)
        dc_buf[pl.ds(s, D_PAD), :] = jnp.zeros((D_PAD, BR), F32)
        acc_w[...] = jnp.zeros_like(acc_w)
        acc_s[...] = jnp.zeros_like(acc_s)

        def chunk(i, c):
            r0 = pl.multiple_of(i * CHUNK, CHUNK)
            rows = pl.ds(r0, CHUNK)
            taps, xh, rs, sw, dsw, sg, dsg = _mix_d_common(g_buf, w_ref, b_ref, lg_ref, lb_ref, hh_buf, r0)
            dyv = dy_ref[rows, :]
            dg_buf[rows, :] = (dyv * sw * dsg).astype(BF16)
            d_ln = dyv * sg * dsw
            acc_s[0] += _part8(d_ln * xh)
            acc_s[1] += _part8(d_ln)
            dc = _ln_bwd(d_ln * lg_ref[...], xh, rs)
            acc_s[2] += _part8(dc)
            dc_buf[rows, :] = dc
            for k in range(K_CONF):
                acc_w[k] += _part8(dc * taps[k])
            return c

        lax.fori_loop(0, s // CHUNK, chunk, 0)

        def chunk2(i, c):
            r0 = pl.multiple_of(i * CHUNK, CHUNK)
            rows = pl.ds(r0, CHUNK)
            de = dc_buf[pl.ds(r0, CHUNK + D_PAD), :]
            dh = jnp.zeros((CHUNK, BR), F32)
            for k, win in enumerate(_tap_windows(de, [K_CONF - 1 - k for k in range(K_CONF)], CHUNK)):
                dh = dh + w_ref[k:k + 1, :] * win
            a = cin_buf[rows, 0:BR]
            sig = _sigmoid(cin_buf[rows, BR:])
            dcin_buf[rows, 0:BR] = (dh * sig).astype(BF16)
            dcin_buf[rows, BR:] = (dh * a * sig * (1.0 - sig)).astype(BF16)
            return c

        lax.fori_loop(0, s // CHUNK, chunk2, 0)
        dw_ref[...] = jnp.zeros_like(dw_ref)
        for k in range(K_CONF):
            dw_ref[k:k + 1, :] = jnp.sum(acc_w[k], axis=0, keepdims=True)
        dlg_ref[...] = jnp.sum(acc_s[0], axis=0, keepdims=True)
        dlb_ref[...] = jnp.sum(acc_s[1], axis=0, keepdims=True)
        db_ref[...] = jnp.sum(acc_s[2], axis=0, keepdims=True)
        _copy_all([(dcin_buf, _window(dproj_ref, DGLU0, 2 * BR)), (dg_buf, _window(dproj_ref, DG0, BR))], sems)

    vec = jax.ShapeDtypeStruct((1, BR), F32)
    return pl.pallas_call(
        body, name="mix_d_bwd", in_specs=[ANY, ANY, VMEM, VMEM, VMEM, VMEM, VMEM], out_specs=[ANY, VMEM, VMEM, VMEM, VMEM],
        out_shape=[jax.ShapeDtypeStruct(dproj.shape, dproj.dtype), jax.ShapeDtypeStruct((32, BR), F32), vec, vec, vec],
        scratch_shapes=[pltpu.VMEM((s, 2 * BR), F32), pltpu.VMEM((s, BR), F32), pltpu.VMEM((s + D_PAD, BR), F32),
                        pltpu.VMEM((s + D_PAD, BR), F32), pltpu.VMEM((s, 2 * BR), BF16), pltpu.VMEM((s, BR), BF16),
                        pltpu.VMEM((K_CONF, 8, BR), F32), pltpu.VMEM((3, 8, BR), F32), pltpu.SemaphoreType.DMA((2,))],
        input_output_aliases={1: 0}, compiler_params=_params())(proj, dproj, dy, w, b, lg, lb)


MERGE_TM = 256


def _merge_fwd(x, proj, ys, wb, wo):
    s = x.shape[0]
    tm = min(2 * MERGE_TM, s)

    def body(x_ref, lg_ref, ya, yb, yc, yd, wb_ref, wo_ref, xn_ref, mg_ref):
        merged = jnp.zeros((tm, D_MODEL), F32)
        for n, y_ref in enumerate((ya, yb, yc, yd)):
            gate = _sigmoid(lg_ref[:, n * D_MODEL:(n + 1) * D_MODEL])
            merged = merged + gate * _dot(y_ref[...], wb_ref[n])
        mb = merged.astype(BF16)
        mg_ref[...] = mb
        xn_ref[...] = x_ref[...] + _dot(mb, wo_ref[...])

    row = lambda w: pl.BlockSpec((tm, w), lambda i: (i, 0))
    return pl.pallas_call(
        body, name="merge_fwd", grid=(s // tm,),
        in_specs=[row(D_MODEL), pl.BlockSpec((tm, MRG_COLS), lambda i: (i, MRG0 // MRG_COLS)), row(BR), row(BR), row(BR), row(BR),
                  pl.BlockSpec((N_HEADS, BR, D_MODEL), lambda i: (0, 0, 0)), pl.BlockSpec((D_MODEL, D_MODEL), lambda i: (0, 0))],
        out_specs=[row(D_MODEL), row(D_MODEL)],
        out_shape=[jax.ShapeDtypeStruct((s, D_MODEL), F32), jax.ShapeDtypeStruct((s, D_MODEL), BF16)],
        compiler_params=_params("arbitrary"))(x, proj, *ys, wb, wo)


def _merge_bwd(dxn, proj, ys, merged, wb, wo, rider=None):
    s = dxn.shape[0]
    tm = min(MERGE_TM, s)
    steps = s // tm
    r_in, r_in_specs, r_out_specs, r_shapes, r_scratch, r_alias = _rider_parts(rider, 9, 7)
    nr = len(r_in)

    def body(*refs):
        dx_ref, lg_ref, ya, yb, yc, yd, mg_ref, wb_ref, wo_ref = refs[:9]
        dlg_ref, da, db, dc, dd, dwo_ref, dwb_ref = refs[9 + nr:16 + nr]
        start, finish = _ride(rider, pl.program_id(0) == 0, pl.program_id(0) == steps - 1,
                              refs[9:9 + nr], refs[16 + nr:16 + 2 * nr], refs[16 + 2 * nr:])
        start()

        @pl.when(pl.program_id(0) == 0)
        def _():
            dwo_ref[...] = jnp.zeros_like(dwo_ref)
            dwb_ref[...] = jnp.zeros_like(dwb_ref)

        dxb = dx_ref[...].astype(BF16)
        d_merged = _dot_nt(dxb, wo_ref[...])
        dwo_ref[...] += _dot_tn(mg_ref[...], dxb)
        for n, (y_ref, dy_ref) in enumerate(((ya, da), (yb, db), (yc, dc), (yd, dd))):
            yv = y_ref[...]
            gate = _sigmoid(lg_ref[:, n * D_MODEL:(n + 1) * D_MODEL])
            pn = _dot(yv, wb_ref[n])
            dlg_ref[:, n * D_MODEL:(n + 1) * D_MODEL] = (d_merged * pn * gate * (1.0 - gate)).astype(BF16)
            dpn = (d_merged * gate).astype(BF16)
            dy_ref[...] = _dot_nt(dpn, wb_ref[n])
            dwb_ref[n] += _dot_tn(yv, dpn)
        finish()

    row = lambda w: pl.BlockSpec((tm, w), lambda i: (i, 0))
    wb_spec = pl.BlockSpec((N_HEADS, BR, D_MODEL), lambda i: (0, 0, 0))
    wo_spec = pl.BlockSpec((D_MODEL, D_MODEL), lambda i: (0, 0))
    dy_shape = jax.ShapeDtypeStruct((s, BR), F32)
    outs = pl.pallas_call(
        body, name="merge_bwd", grid=(steps,),
        in_specs=[row(D_MODEL), pl.BlockSpec((tm, MRG_COLS), lambda i: (i, MRG0 // MRG_COLS)), row(BR), row(BR), row(BR), row(BR), row(D_MODEL),
                  wb_spec, wo_spec] + r_in_specs,
        out_specs=[pl.BlockSpec((tm, MRG_COLS), lambda i: (i, MRG0 // MRG_COLS)), row(BR), row(BR), row(BR), row(BR), wo_spec, wb_spec] + r_out_specs,
        out_shape=[jax.ShapeDtypeStruct((s, PCOLS), BF16), dy_shape, dy_shape, dy_shape, dy_shape,
                   jax.ShapeDtypeStruct((D_MODEL, D_MODEL), F32), jax.ShapeDtypeStruct((N_HEADS, BR, D_MODEL), F32)] + r_shapes,
        scratch_shapes=r_scratch, input_output_aliases=r_alias,
        compiler_params=_params("arbitrary"))(dxn, proj, *ys, merged, wb, wo, *r_in)
    return outs[0], outs[1:5], outs[5], outs[6], outs[7:]


def _loss_head(x, target, g):
    s = x.shape[0]
    tm = min(512, s)

    def body(x_ref, t_ref, g_ref, loss_ref, dx_ref, dg_ref):
        @pl.when(pl.program_id(0) == 0)
        def _():
            loss_ref[...] = jnp.zeros_like(loss_ref)
            dg_ref[...] = jnp.zeros_like(dg_ref)

        xv = x_ref[...]
        gv = g_ref[...]
        r = lax.rsqrt(jnp.mean(xv * xv, axis=-1, keepdims=True) + EPS)
        xn = xv * r
        err = xn * gv - t_ref[...]
        loss_ref[...] += 0.5 * jnp.sum(jnp.mean(err * err, axis=-1, keepdims=True))
        dy = err * (1.0 / D_MODEL)
        dg_ref[...] += _part8(dy * xn)
        gy = dy * gv
        dx_ref[...] = r * (gy - xn * jnp.mean(xn * gy, axis=-1, keepdims=True))

    row = pl.BlockSpec((tm, D_MODEL), lambda i: (i, 0))
    return pl.pallas_call(
        body, name="loss_head", grid=(s // tm,),
        in_specs=[row, row, pl.BlockSpec((1, D_MODEL), lambda i: (0, 0))],
        out_specs=[pl.BlockSpec((8, 128), lambda i: (0, 0)), row, pl.BlockSpec((8, D_MODEL), lambda i: (0, 0))],
        out_shape=[jax.ShapeDtypeStruct((8, 128), F32), jax.ShapeDtypeStruct((s, D_MODEL), F32), jax.ShapeDtypeStruct((8, D_MODEL), F32)],
        compiler_params=_params("arbitrary"))(x, target, g)


_SEGMENTS = ((0, 512, AUV0), (512, 256, AG0), (768, 768, QKV0), (1540, 256, BG0), (1796, 768, CIN0),
             (2564, 256, CG0), (2820, 512, DGLU0), (3332, 256, DG0), (3588, 4096, MRG0))


def _sgu_bias_rows(sgu_b):
    return jnp.repeat(sgu_b.T, HEAD_DIM, axis=1)


def _layer_fwd(x, p, next_blocks=None):
    proj, proj_bf, h = _inproj_fwd(x, p["norm_g"], p["w_in"], p["w_bf"])
    ya = _mix_a_fwd(proj, p["sgu_w"], _sgu_bias_rows(p["sgu_b"]), p["sgu_ln_g"], p["sgu_ln_b"])
    qe, ke, ve, ket, vet = _attn_prep_fwd(proj, proj_bf, p["f_bias"])
    ot, lse, landed = _attn_fwd(qe, ke, vet, rider=None if next_blocks is None else _gather_send_rider(next_blocks))
    yb, o, gathered = _bgate_fwd(proj, ot, rider=None if next_blocks is None else _gather_forward_rider(landed))
    yc = _mix_c_fwd(proj, p["short_conv_w"])
    yd = _mix_d_fwd(proj, p["conf_dw_w"], p["conf_dw_b"], p["conf_ln_g"], p["conf_ln_b"])
    ys = (ya, yb, yc, yd)
    x_next, merged = _merge_fwd(x, proj, ys, p["w_branch"], p["w_out"])
    saved = dict(x=x, proj=proj, proj_bf=proj_bf, h=h, ys=ys, merged=merged, qe=qe, ke=ke, ve=ve, ket=ket, lse=lse, o=o)
    return x_next, saved, gathered


def _grads_by_chip(d_win, d_wb, d_wo):
    return [jnp.stack([_w_in_shard(d_win[0], d_win[1], j) for j in range(N_CHIPS)]),
            jnp.stack([d_wb[:, :, j * BR:(j + 1) * BR].reshape(N_HEADS * BR, BR).astype(BF16) for j in range(N_CHIPS)]),
            jnp.stack([d_wo[j * BR:(j + 1) * BR].astype(BF16) for j in range(N_CHIPS)])]


def _layer_bwd(dxn, p, sv, prev_reduce=None, reduce_self=None):
    proj = sv["proj"]
    if prev_reduce is None:
        dproj, dys, d_wo, d_wb, _ = _merge_bwd(dxn, proj, sv["ys"], sv["merged"], p["w_branch"], p["w_out"])
        chip_rider = None
    else:
        pc, chip, arrays = prev_reduce
        mine, other = _row_halves(arrays, pc)
        dproj, dys, d_wo, d_wb, from_sibling = _merge_bwd(dxn, proj, sv["ys"], sv["merged"], p["w_branch"], p["w_out"],
                                                           rider=_pair_rider(other))
        halves = _add_pairs(mine, from_sibling)
        chip_rider = _chip_rider(halves)
    dproj, d_sgu_w, d_bias_rows, d_sgu_lg, d_sgu_lb = _mix_a_bwd(
        proj, dproj, dys[0], p["sgu_w"], _sgu_bias_rows(p["sgu_b"]), p["sgu_ln_g"], p["sgu_ln_b"])
    dproj, doe, dd = _bgate_bwd(proj, dproj, sv["o"], dys[1])
    dqe, dke, dve, from_chips = _attn_bwd(sv["qe"], sv["ke"], sv["ket"], sv["ve"], doe, sv["lse"], dd, rider=chip_rider)
    reduced = None
    if prev_reduce is not None:
        reduced = _sum_chips(halves, chip, from_chips)
    dproj, dproj_bf, d_fb = _attn_prep_bwd(sv["proj_bf"], dproj, dqe, dke, dve, p["f_bias"])
    dproj, d_sc = _mix_c_bwd(proj, dproj, dys[2], p["short_conv_w"])
    dproj, d_cw, d_cb, d_clg, d_clb = _mix_d_bwd(proj, dproj, dys[3], p["conf_dw_w"], p["conf_dw_b"], p["conf_ln_g"], p["conf_ln_b"])
    d_win = _inproj_bwd_w(sv["h"], dproj, dproj_bf)
    reduced_self, self_rider = None, None
    if reduce_self is not None:
        pc, chip = reduce_self
        mine, other = _row_halves(_grads_by_chip(d_win, d_wb, d_wo), pc)
        own_halves = _add_pairs(mine, _pair_exchange(other))
        self_rider = _chip_rider(own_halves)
    dx, dg8, from_chips_self = _inproj_bwd_x(dproj, p["w_in"], sv["x"], dxn, p["norm_g"], dproj_bf, p["w_bf"], rider=self_rider)
    if reduce_self is not None:
        reduced_self = _sum_chips(own_halves, chip, from_chips_self)
    grads = dict(
        norm_g=jnp.sum(dg8, axis=0), w_in=d_win, f_bias=d_fb[0, :N_HEADS], sgu_w=d_sgu_w,
        sgu_b=d_bias_rows.reshape(CHUNK, N_HEADS, HEAD_DIM).sum(axis=-1).T,
        sgu_ln_g=d_sgu_lg[0], sgu_ln_b=d_sgu_lb[0], short_conv_w=d_sc[:K_SHORT], conf_dw_w=d_cw[:K_CONF],
        conf_dw_b=d_cb[0], conf_ln_g=d_clg[0], conf_ln_b=d_clb[0], w_branch=d_wb, w_out=d_wo)
    return dx, grads, reduced, reduced_self


def _row_halves(arrays, pc):
    half = lambda a, i: lax.dynamic_slice_in_dim(a, i * (a.shape[-2] // 2), a.shape[-2] // 2, axis=a.ndim - 2)
    return [half(a, pc) for a in arrays], [half(a, 1 - pc) for a in arrays]


def _local_step(x, target, layers, final_g):
    saved = []
    for p in layers:
        x, sv, _ = _layer_fwd(x, p)
        saved.append(sv)
    loss8, dx, dfg8 = _loss_head(x, target, final_g)
    grads = [None] * len(layers)
    for l in reversed(range(len(layers))):
        dx, grads[l], _, _ = _layer_bwd(dx, layers[l], saved[l])
    return loss8[0, 0], dx, grads, jnp.sum(dfg8, axis=0)


def _gather_chips(blocks):
    n = len(blocks)

    def body(*refs):
        ins, outs, (send_sems, recv_sems) = refs[:n], refs[n:2 * n], refs[2 * n:]
        x, y, cc = _place()
        me, sibling = (x, y, cc), (x, y, 1 - cc)
        chips = _other_chips(x, y)

        def copy(a, k, chip, layer, to, src=None):
            dst = outs[a].at[2 * chip[0] + chip[1], layer]
            return pltpu.make_async_remote_copy(src_ref=dst if src is None else src, dst_ref=dst, send_sem=send_sems.at[a, k],
                                                recv_sem=recv_sems.at[a, k], device_id=to, device_id_type=MESH)

        first = [copy(a, j, (x, y), cc, (*chip, cc), src=ins[a]) for j, chip in enumerate(chips) for a in range(n)]
        for cp in first:
            cp.start()
        passed = []
        for j, chip in enumerate(chips):
            for a in range(n):
                copy(a, j, chip, cc, me).wait_recv()
                passed.append(copy(a, 3 + j, chip, cc, sibling))
                passed[-1].start()
        for j, chip in enumerate(chips):
            for a in range(n):
                copy(a, 3 + j, chip, 1 - cc, me).wait_recv()
        for cp in first + passed:
            cp.wait_send()

    return pl.pallas_call(
        body, name="gather_chips", in_specs=[ANY] * n, out_specs=[ANY] * n,
        out_shape=[jax.ShapeDtypeStruct((N_CHIPS, 2) + b.shape, b.dtype) for b in blocks],
        scratch_shapes=[pltpu.SemaphoreType.DMA((n, 6)), pltpu.SemaphoreType.DMA((n, 6))])(*blocks)


def _pair_exchange(arrays):
    n = len(arrays)

    def body(*refs):
        ins, outs, (send_sems, recv_sems) = refs[:n], refs[n:2 * n], refs[2 * n:]
        x, y, cc = _place()
        cps = [pltpu.make_async_remote_copy(src_ref=ins[a], dst_ref=outs[a], send_sem=send_sems.at[a], recv_sem=recv_sems.at[a],
                                            device_id=(x, y, 1 - cc), device_id_type=MESH) for a in range(n)]
        for cp in cps:
            cp.start()
        for cp in cps:
            cp.wait()

    return pl.pallas_call(
        body, name="pair_exchange", in_specs=[ANY] * n, out_specs=[ANY] * n,
        out_shape=[jax.ShapeDtypeStruct(a.shape, a.dtype) for a in arrays],
        scratch_shapes=[pltpu.SemaphoreType.DMA((n,)), pltpu.SemaphoreType.DMA((n,))])(*arrays)


def _allreduce8(arrays):
    n = len(arrays)

    def body(*refs):
        ins, outs, recvs = refs[:n], refs[n:2 * n], refs[2 * n:3 * n]
        send_sems, recv_sems = refs[3 * n:]
        x, y, cc = _place()
        for a in range(n):
            outs[a][...] = ins[a][...]
        for k, peer in enumerate([(x, y, 1 - cc), (1 - x, y, cc), (x, 1 - y, cc)]):
            cps = [pltpu.make_async_remote_copy(src_ref=outs[a], dst_ref=recvs[a].at[k], send_sem=send_sems.at[a, k],
                                                recv_sem=recv_sems.at[a, k], device_id=peer, device_id_type=MESH) for a in range(n)]
            for cp in cps:
                cp.start()
            for cp in cps:
                cp.wait()
            for a in range(n):
                outs[a][...] = outs[a][...] + recvs[a][k]

    return pl.pallas_call(
        body, name="allreduce8", in_specs=[VMEM] * n, out_specs=[VMEM] * n,
        out_shape=[jax.ShapeDtypeStruct(a.shape, F32) for a in arrays],
        scratch_shapes=[pltpu.VMEM((3,) + a.shape, F32) for a in arrays] + [pltpu.SemaphoreType.DMA((n, 3)), pltpu.SemaphoreType.DMA((n, 3))],
        compiler_params=_params())(*arrays)


def _row_tile(rows, cols, limit_bytes=1 << 20):
    t = rows
    while t % 16 == 0 and t * cols * 4 > limit_bytes:
        t //= 2
    return t


REDUCE_STEPS = 8


def _add_pairs(xs, ys):
    n = len(xs)

    def body(*refs):
        for a in range(n):
            refs[2 * n + a][...] = (refs[a][...].astype(F32) + refs[n + a][...].astype(F32)).astype(BF16)

    specs = [pl.BlockSpec((x.shape[0], x.shape[1] // REDUCE_STEPS, x.shape[2]), lambda i: (0, i, 0)) for x in xs]
    return pl.pallas_call(body, name="add_pairs", grid=(REDUCE_STEPS,), in_specs=specs + specs, out_specs=specs,
                          out_shape=[jax.ShapeDtypeStruct(x.shape, BF16) for x in xs], compiler_params=_params("arbitrary"))(*xs, *ys)


def _sum_chips(halves, chip, recvs):
    n = len(halves)

    def body(chip_ref, *refs):
        del chip_ref
        for a in range(n):
            acc = refs[a][0].astype(F32)
            for k in range(3):
                acc = acc + refs[n + a][k].astype(F32)
            refs[2 * n + a][...] = acc

    rows = [h.shape[1] // REDUCE_STEPS for h in halves]
    own_specs = [pl.BlockSpec((1, r, h.shape[2]), lambda i, c: (c[0], i, 0)) for h, r in zip(halves, rows)]
    recv_specs = [pl.BlockSpec((3, r, h.shape[2]), lambda i, c: (0, i, 0)) for h, r in zip(halves, rows)]
    out_specs = [pl.BlockSpec((r, h.shape[2]), lambda i, c: (i, 0)) for h, r in zip(halves, rows)]
    grid_spec = pltpu.PrefetchScalarGridSpec(num_scalar_prefetch=1, grid=(REDUCE_STEPS,), in_specs=own_specs + recv_specs,
                                             out_specs=out_specs)
    return pl.pallas_call(body, name="sum_chips", grid_spec=grid_spec,
                          out_shape=[jax.ShapeDtypeStruct(h.shape[1:], F32) for h in halves],
                          compiler_params=_params("arbitrary"))(jnp.reshape(chip, (1,)).astype(jnp.int32), *halves, *recvs)


def _adamw_update(w_ref, m_ref, v_ref, g_ref, d_ref, mo_ref, vo_ref):
    gv = g_ref[...]
    mn = ADAM_B1 * m_ref[...] + (1.0 - ADAM_B1) * gv
    vn = ADAM_B2 * v_ref[...] + (1.0 - ADAM_B2) * (gv * gv)
    m_hat = mn / (1.0 - ADAM_B1 ** ADAM_STEP)
    v_hat = vn / (1.0 - ADAM_B2 ** ADAM_STEP)
    d_ref[...] = -ADAM_LR * (m_hat / (jnp.sqrt(v_hat) + ADAM_EPS) + ADAM_WD * w_ref[...])
    mo_ref[...] = mn
    vo_ref[...] = vn


def _adamw(w, m, v, g):
    r, c = w.shape
    t = _row_tile(r, c)

    def body(*refs):
        _adamw_update(*refs)

    spec = pl.BlockSpec((t, c), lambda i: (i, 0))
    shape = jax.ShapeDtypeStruct((r, c), F32)
    return pl.pallas_call(body, name="adamw", grid=(r // t,), in_specs=[spec] * 4, out_specs=[spec] * 3,
                          out_shape=[shape] * 3, compiler_params=_params("arbitrary"))(w, m, v, g)


def _adamw_small(ws, ms, vs, gs):
    n = len(ws)

    def body(*refs):
        w_refs, m_refs, v_refs, g_refs = refs[:n], refs[n:2 * n], refs[2 * n:3 * n], refs[3 * n:4 * n]
        d_refs, mo_refs, vo_refs = refs[4 * n:5 * n], refs[5 * n:6 * n], refs[6 * n:]
        for a in range(n):
            _adamw_update(w_refs[a], m_refs[a], v_refs[a], g_refs[a], d_refs[a], mo_refs[a], vo_refs[a])

    shapes = [jax.ShapeDtypeStruct(w.shape, F32) for w in ws]
    outs = pl.pallas_call(body, name="adamw_small", in_specs=[VMEM] * (4 * n), out_specs=[VMEM] * (3 * n),
                          out_shape=shapes * 3, compiler_params=_params())(*ws, *ms, *vs, *gs)
    return outs[:n], outs[n:2 * n], outs[2 * n:]


SMALL =("norm_g", "f_bias", "sgu_w", "sgu_b", "sgu_ln_g", "sgu_ln_b", "short_conv_w", "conf_dw_w", "conf_dw_b",
         "conf_ln_g", "conf_ln_b", "final_g")
WEIGHTS = ("norm_g", "w_in", "f_bias", "sgu_w", "sgu_b", "sgu_ln_g", "sgu_ln_b", "short_conv_w", "conf_dw_w",
           "conf_dw_b", "conf_ln_g", "conf_ln_b", "w_branch", "w_out", "final_g")


def _aligned_pieces():
    out, pos = [], 0
    for src, width, dst in sorted(_SEGMENTS, key=lambda t: t[2]):
        if dst > pos:
            out.append((None, 0, dst - pos))
        lo = src
        while lo < src + width:
            j = lo // SHARD_COLS
            hi = min(src + width, (j + 1) * SHARD_COLS)
            out.append((j, lo - j * SHARD_COLS, hi - lo))
            lo = hi
        pos = dst + width
    return out


def _w_in_aligned(shards):
    rows, dtype = shards[0].shape[0], shards[0].dtype
    return jnp.concatenate([jnp.zeros((rows, w), dtype) if j is None else shards[j][:, c0:c0 + w]
                            for j, c0, w in _aligned_pieces()], axis=1)


def _w_bf(shards):
    j, c0 = BF_SRC // SHARD_COLS, BF_SRC % SHARD_COLS
    assert c0 + BF_COLS <= SHARD_COLS
    return jnp.pad(shards[j][:, c0:c0 + BF_COLS], ((0, 0), (0, CHUNK - BF_COLS)))


def _w_in_shard(g, g_bf, j):
    lo_s, hi_s = j * SHARD_COLS, (j + 1) * SHARD_COLS
    parts = []
    for src, width, dst in sorted(_SEGMENTS + ((BF_SRC, BF_COLS, None),), key=lambda t: t[0]):
        lo, hi = max(src, lo_s), min(src + width, hi_s)
        if lo < hi:
            parts.append(g_bf[:, lo - src:hi - src] if dst is None else g[:, dst + lo - src:dst + hi - src])
    return jnp.concatenate(parts + [jnp.zeros((g.shape[0], SHARD_PAD - SHARD_COLS), g.dtype)], axis=1)


def kernel(x, norm_g, w_in, f_bias, sgu_w, sgu_b, sgu_ln_g, sgu_ln_b, short_conv_w, conf_dw_w, conf_dw_b, conf_ln_g, conf_ln_b, w_branch, w_out, final_g, loss_target, m_norm_g, m_w_in, m_f_bias, m_sgu_w, m_sgu_b, m_sgu_ln_g, m_sgu_ln_b, m_short_conv_w, m_conf_dw_w, m_conf_dw_b, m_conf_ln_g, m_conf_ln_b, m_w_branch, m_w_out, m_final_g, v_norm_g, v_w_in, v_f_bias, v_sgu_w, v_sgu_b, v_sgu_ln_g, v_sgu_ln_b, v_short_conv_w, v_conf_dw_w, v_conf_dw_b, v_conf_ln_g, v_conf_ln_b, v_w_branch, v_w_out, v_final_g):
    px, py, pc = _place()
    chip = 2 * px + py
    depth = w_in.shape[0]
    w = dict(norm_g=norm_g, w_in=w_in, f_bias=f_bias, sgu_w=sgu_w, sgu_b=sgu_b, sgu_ln_g=sgu_ln_g, sgu_ln_b=sgu_ln_b,
             short_conv_w=short_conv_w, conf_dw_w=conf_dw_w, conf_dw_b=conf_dw_b, conf_ln_g=conf_ln_g, conf_ln_b=conf_ln_b,
             w_branch=w_branch, w_out=w_out, final_g=final_g)
    m = dict(norm_g=m_norm_g, w_in=m_w_in, f_bias=m_f_bias, sgu_w=m_sgu_w, sgu_b=m_sgu_b, sgu_ln_g=m_sgu_ln_g,
             sgu_ln_b=m_sgu_ln_b, short_conv_w=m_short_conv_w, conf_dw_w=m_conf_dw_w, conf_dw_b=m_conf_dw_b,
             conf_ln_g=m_conf_ln_g, conf_ln_b=m_conf_ln_b, w_branch=m_w_branch, w_out=m_w_out, final_g=m_final_g)
    v = dict(norm_g=v_norm_g, w_in=v_w_in, f_bias=v_f_bias, sgu_w=v_sgu_w, sgu_b=v_sgu_b, sgu_ln_g=v_sgu_ln_g,
             sgu_ln_b=v_sgu_ln_b, short_conv_w=v_short_conv_w, conf_dw_w=v_conf_dw_w, conf_dw_b=v_conf_dw_b,
             conf_ln_g=v_conf_ln_g, conf_ln_b=v_conf_ln_b, w_branch=v_w_branch, w_out=v_w_out, final_g=v_final_g)

    local = (jnp.pad(w_in, ((0, 0), (0, 0), (0, SHARD_PAD - SHARD_COLS))).astype(BF16),
             w_branch.astype(BF16).reshape(depth, N_HEADS * BR, BR), w_out.astype(BF16))
    pick = lambda a, i: lax.dynamic_index_in_dim(a, i, 0, keepdims=False)
    flat = lambda a: a.reshape(-1, a.shape[-1])

    def my_half(l):
        return _row_halves([a[l] for a in local], pc)[0]

    def all_chips(gathered, l):
        return [lax.dynamic_update_index_in_dim(got.reshape((N_CHIPS,) + a.shape[1:]), a[l], chip, 0) for got, a in zip(gathered, local)]

    conv_ch = BR // N_CHIPS
    place = lambda a: lax.dynamic_update_slice_in_dim(jnp.zeros(a.shape[:-1] + (BR,), F32), a * (pc == 0).astype(F32),
                                                      conv_ch * chip, axis=2).reshape(-1, BR)
    short_full, conf_full = _allreduce8([place(short_conv_w), place(conf_dw_w)])
    short_full, conf_full = short_full.reshape(depth, K_SHORT, BR), conf_full.reshape(depth, K_CONF, BR)

    def layer_params(l, shards):
        wi_all, wb_all, wo_all = shards
        return dict(
            norm_g=norm_g[l][None], w_in=_w_in_aligned([wi_all[j] for j in range(N_CHIPS)]),
            w_bf=_w_bf([wi_all[j] for j in range(N_CHIPS)]),
            f_bias=jnp.pad(f_bias[l], (0, CHUNK - N_HEADS))[None],
            sgu_w=sgu_w[l], sgu_b=sgu_b[l], sgu_ln_g=sgu_ln_g[l][None], sgu_ln_b=sgu_ln_b[l][None],
            short_conv_w=short_full[l], conf_dw_w=conf_full[l], conf_dw_b=conf_dw_b[l][None],
            conf_ln_g=conf_ln_g[l][None], conf_ln_b=conf_ln_b[l][None],
            w_branch=jnp.concatenate([wb_all[j].reshape(N_HEADS, BR, BR) for j in range(N_CHIPS)], axis=2),
            w_out=jnp.concatenate([wo_all[j] for j in range(N_CHIPS)], axis=0))

    shards = all_chips(_gather_chips(my_half(0)), 0)
    params, saved, xs = [], [], x[0]
    for l in range(depth):
        params.append(layer_params(l, shards))
        nxt = my_half(l + 1) if l + 1 < depth else None
        xs, sv, gathered = _layer_fwd(xs, params[l], next_blocks=nxt)
        saved.append(sv)
        if nxt is not None:
            shards = all_chips(gathered, l + 1)
    loss8, dx, dfg8 = _loss_head(xs, loss_target[0], final_g[None])
    d_final_g = jnp.sum(dfg8, axis=0)
    loss = lax.psum(loss8[0, 0], ("x", "y", "c"))

    grads, reduced, pending = [None] * depth, [None] * depth, None
    for l in reversed(range(depth)):
        dx, grads[l], red, red_self = _layer_bwd(
            dx, params[l], saved[l], prev_reduce=None if pending is None else (pc, chip, pending),
            reduce_self=(pc, chip) if l == 0 else None)
        if pending is not None:
            reduced[l + 1] = red
        if l == 0:
            reduced[0] = red_self
        else:
            pending = _grads_by_chip(grads[l]["w_in"], grads[l]["w_branch"], grads[l]["w_out"])
    grad_x = dx
    own = [a for red in reduced for a in red]
    rows = [jnp.where(pc == 0, jnp.concatenate([a, b], axis=0), jnp.concatenate([b, a], axis=0))
            for a, b in zip(own, _pair_exchange(own))]
    gi, gb, go = [jnp.stack([rows[3 * l + k] for l in range(depth)]) for k in range(3)]
    g = dict(w_in=gi[:, :, :SHARD_COLS], w_branch=gb.reshape(depth, N_HEADS, BR, BR), w_out=go)

    two_d = lambda a: a.reshape(-1, a.shape[-1])
    at_least_2d = lambda a: a if a.ndim >= 2 else a[None]
    small_local = [jnp.stack([grads[l][n] for l in range(depth)]) for n in SMALL[:-1]] + [d_final_g]
    for n, a, b in zip(SMALL, small_local, _allreduce8([at_least_2d(a) for a in small_local])):
        g[n] = b.reshape(a.shape)
    for n in ("short_conv_w", "conf_dw_w"):
        g[n] = lax.dynamic_slice_in_dim(g[n], conv_ch * chip, conv_ch, axis=2)

    delta, new_m, new_v = {}, {}, {}
    for n in ("w_in", "w_branch", "w_out"):
        shp = w[n].shape
        d_, m_, v_ = _adamw(two_d(w[n]), two_d(m[n]), two_d(v[n]), two_d(g[n]))
        delta[n], new_m[n], new_v[n] = d_.reshape(shp), m_.reshape(shp), v_.reshape(shp)
    d_, m_, v_ = _adamw_small(*[[at_least_2d(t[n]) for n in SMALL] for t in (w, m, v, g)])
    for n, a, b, c_ in zip(SMALL, d_, m_, v_):
        delta[n], new_m[n], new_v[n] = a.reshape(w[n].shape), b.reshape(w[n].shape), c_.reshape(w[n].shape)

    return (loss, grad_x[None], *[g[n] for n in WEIGHTS], *[delta[n] for n in WEIGHTS],
            *[new_m[n] for n in WEIGHTS], *[new_v[n] for n in WEIGHTS])
```
